```python
import jax, jax.numpy as jnp
from jax import lax
import numpy as np

D_MODEL = 2048
BATCH = 8
SEQ = 8192
DEPTH = 1

GDN_HEADS = 8
GDN_DK = D_MODEL // 16
GDN_DV = D_MODEL // GDN_HEADS
GDN_QK = GDN_HEADS * GDN_DK
GDN_V = GDN_HEADS * GDN_DV
CONV_WIDTH = 4
RET_HEADS = 8
RET_DK = D_MODEL // 16
RET_DV = D_MODEL // RET_HEADS
RET_QK = RET_HEADS * RET_DK
RET_V = RET_HEADS * RET_DV
ROPE_BASE = 10000.0
CHUNK = 64
D_FF = ((8 * D_MODEL // 3 + 255) // 256) * 256
EPS = 1e-6
IN_SPLITS = (2 * GDN_QK + GDN_V, GDN_V, GDN_HEADS, GDN_HEADS,
             RET_QK, RET_QK, RET_V, RET_V, D_MODEL, D_MODEL)
N_IN = sum(IN_SPLITS)

kernel_name = "hybrid_gdn_retention_gated_merge"


def rmsnorm(x, w):
    xf = x.astype(jnp.float32)
    xf = xf * lax.rsqrt(jnp.mean(xf * xf, axis=-1, keepdims=True) + EPS)
    return xf.astype(x.dtype) * w


def group_norm_heads(o, w):
    mu = jnp.mean(o, axis=-1, keepdims=True)
    var = jnp.mean(jnp.square(o - mu), axis=-1, keepdims=True)
    return (o - mu) * lax.rsqrt(var + EPS) * w.reshape(o.shape[2], o.shape[3]).astype(jnp.float32)


def l2norm(x):
    return x * lax.rsqrt(jnp.sum(x * x, axis=-1, keepdims=True) + EPS)


def split_cols(p):
    offs = [int(o) for o in np.cumsum(IN_SPLITS)[:-1]]
    return jnp.split(p, offs, axis=-1)


def causal_conv(x, w):
    width, s = w.shape[0], x.shape[1]
    xp = jnp.pad(x, ((0, 0), (width - 1, 0), (0, 0)))
    out = xp[:, 0:s] * w[0]
    for j in range(1, width):
        out = out + xp[:, j:j + s] * w[j]
    return out


def rotary(x):
    d, s = x.shape[-1], x.shape[1]
    inv = ROPE_BASE ** (-jnp.arange(0, d, 2, dtype=jnp.float32) / d)
    ang = jnp.arange(s, dtype=jnp.float32)[:, None] * inv[None, :]
    cos = jnp.cos(ang)[None, :, None, :]
    sin = jnp.sin(ang)[None, :, None, :]
    x1, x2 = x[..., : d // 2], x[..., d // 2:]
    return jnp.concatenate([x1 * cos - x2 * sin, x2 * cos + x1 * sin], axis=-1)


def to_chunks(t):
    b, s, h, d = t.shape
    return t.reshape(b, s // CHUNK, CHUNK, h, d).transpose(0, 3, 1, 2, 4)


def to_chunks_scalar(t):
    b, s, h = t.shape
    return t.reshape(b, s // CHUNK, CHUNK, h).transpose(0, 3, 1, 2)


def from_scan(o):
    n, b, h, c, d = o.shape
    return o.transpose(1, 0, 3, 2, 4).reshape(b, n * c, h, d)


def gated_delta_rule(q, k, v, beta, g):
    c = q.shape[-2]
    incl = jnp.tril(jnp.ones((c, c), dtype=bool))
    strict = jnp.tril(jnp.ones((c, c), dtype=bool), -1)
    g = jnp.cumsum(g, axis=-1)
    decay = jnp.exp(jnp.where(incl, g[..., :, None] - g[..., None, :], -jnp.inf))
    kb = k * beta[..., None]
    a = jnp.where(strict, jnp.einsum('bhnid,bhnjd->bhnij', kb, k) * decay, 0.0)
    t = a + jnp.eye(c, dtype=a.dtype)
    u = lax.linalg.triangular_solve(t, v * beta[..., None], left_side=True, lower=True, unit_diagonal=True)
    w = lax.linalg.triangular_solve(t, kb * jnp.exp(g)[..., None], left_side=True, lower=True, unit_diagonal=True)
    attn = jnp.einsum('bhnid,bhnjd->bhnij', q, k) * decay

    def step(state, xs):
        q_c, k_c, u_c, w_c, g_c, attn_c = xs
        v_new = u_c - jnp.einsum('bhcd,bhde->bhce', w_c, state)
        o_c = (jnp.einsum('bhcd,bhde->bhce', q_c * jnp.exp(g_c)[..., None], state)
               + jnp.einsum('bhij,bhje->bhie', attn_c, v_new))
        g_last = g_c[..., -1:]
        state = (state * jnp.exp(g_last)[..., None]
                 + jnp.einsum('bhcd,bhce->bhde', k_c * jnp.exp(g_last - g_c)[..., None], v_new))
        return state, o_c

    b, h, _, _, dk = q.shape
    dv = v.shape[-1]
    s0 = jnp.zeros((b, h, dk, dv), jnp.float32)
    xs = tuple(jnp.moveaxis(z, 2, 0) for z in (q, k, u, w, g, attn))
    _, o = lax.scan(step, s0, xs)
    return from_scan(o)


def retention_chunked(q, k, v, log_gamma):
    c = q.shape[-2]
    pos = jnp.arange(c, dtype=jnp.float32)
    dist = pos[:, None] - pos[None, :]
    dmat = jnp.exp(jnp.where(dist >= 0, dist * log_gamma[:, None, None], -jnp.inf))
    scores = jnp.einsum('bhnid,bhnjd->bhnij', q, k) * dmat[:, None]
    intra = jnp.einsum('bhnij,bhnje->bhnie', scores, v)
    xi = jnp.exp((pos + 1.0) * log_gamma[:, None])[:, :, None]
    zeta = jnp.exp((c - 1.0 - pos) * log_gamma[:, None])[:, :, None]
    gamma_c = jnp.exp(c * log_gamma)[:, None, None]

    def step(state, xs):
        q_c, k_c, v_c = xs
        o_c = jnp.einsum('bhcd,bhde->bhce', q_c, state) * xi
        state = state * gamma_c + jnp.einsum('bhcd,bhce->bhde', k_c * zeta, v_c)
        return state, o_c

    b, h, _, _, dk = q.shape
    dv = v.shape[-1]
    s0 = jnp.zeros((b, h, dk, dv), jnp.float32)
    xs = tuple(jnp.moveaxis(z, 2, 0) for z in (q, k, v))
    _, inter = lax.scan(step, s0, xs)
    return from_scan(jnp.moveaxis(intra, 2, 0) + inter)


def _fwd_setup_inputs(seed: int = 0) -> dict:
    key = jax.random.key(seed)
    ks = jax.random.split(key, 16)
    f32 = jnp.float32
    x = jax.random.normal(ks[0], (BATCH, SEQ, D_MODEL), f32)
    norm1_w = 1.0 + 0.02 * jax.random.normal(ks[1], (DEPTH, D_MODEL), f32)
    w_in = jax.random.normal(ks[2], (DEPTH, D_MODEL, N_IN), f32) * D_MODEL ** -0.5
    conv_w = jax.random.normal(ks[3], (DEPTH, CONV_WIDTH, 2 * GDN_QK + GDN_V), f32) * CONV_WIDTH ** -0.5
    a_log = jnp.log(jax.random.uniform(ks[4], (DEPTH, GDN_HEADS), f32, 1.0, 16.0))
    dt = jnp.exp(jax.random.uniform(ks[5], (DEPTH, GDN_HEADS), f32, float(np.log(1e-3)), float(np.log(1e-1))))
    dt_bias = dt + jnp.log(-jnp.expm1(-dt))
    gdn_norm_w = 1.0 + 0.02 * jax.random.normal(ks[6], (DEPTH, GDN_DV), f32)
    ret_norm_w = 1.0 + 0.02 * jax.random.normal(ks[7], (DEPTH, RET_V), f32)
    w_out = jax.random.normal(ks[8], (DEPTH, D_MODEL, D_MODEL), f32) * D_MODEL ** -0.5
    norm2_w = 1.0 + 0.02 * jax.random.normal(ks[9], (DEPTH, D_MODEL), f32)
    w_gate = jax.random.normal(ks[10], (DEPTH, D_MODEL, D_FF), f32) * D_MODEL ** -0.5
    w_up = jax.random.normal(ks[11], (DEPTH, D_MODEL, D_FF), f32) * D_MODEL ** -0.5
    w_down = jax.random.normal(ks[12], (DEPTH, D_FF, D_MODEL), f32) * D_FF ** -0.5
    norm_f_w = 1.0 + 0.02 * jax.random.normal(ks[13], (D_MODEL,), f32)
    return {"x": x, "norm1_w": norm1_w, "w_in": w_in, "conv_w": conv_w, "a_log": a_log,
            "dt_bias": dt_bias, "gdn_norm_w": gdn_norm_w, "ret_norm_w": ret_norm_w,
            "w_out": w_out, "norm2_w": norm2_w, "w_gate": w_gate, "w_up": w_up,
            "w_down": w_down, "norm_f_w": norm_f_w}


def _fwd_reference(x, norm1_w, w_in, conv_w, a_log, dt_bias, gdn_norm_w, ret_norm_w,
              w_out, norm2_w, w_gate, w_up, w_down, norm_f_w):
    b, s, _ = x.shape
    f32 = jnp.float32
    log_gamma = jnp.log1p(-jnp.exp2(-5.0 - jnp.arange(RET_HEADS, dtype=f32)))
    h = x
    for l in range(DEPTH):
        u = rmsnorm(h, norm1_w[l])
        proj = u @ w_in[l]
        a_qkv, a_z, a_b, a_a, r_q, r_k, r_v, r_g, gate_a, gate_b = split_cols(proj)

        qkv = jax.nn.silu(causal_conv(a_qkv, conv_w[l])).astype(f32)
        q_a = l2norm(qkv[..., :GDN_QK].reshape(b, s, GDN_HEADS, GDN_DK)) * GDN_DK ** -0.5
        k_a = l2norm(qkv[..., GDN_QK:2 * GDN_QK].reshape(b, s, GDN_HEADS, GDN_DK))
        v_a = qkv[..., 2 * GDN_QK:].reshape(b, s, GDN_HEADS, GDN_DV)
        beta = jax.nn.sigmoid(a_b.astype(f32))
        g = -jnp.exp(a_log[l].astype(f32)) * jax.nn.softplus(a_a.astype(f32) + dt_bias[l].astype(f32))
        o_a = gated_delta_rule(to_chunks(q_a), to_chunks(k_a), to_chunks(v_a),
                               to_chunks_scalar(beta), to_chunks_scalar(g))
        o_a = rmsnorm(o_a, gdn_norm_w[l].astype(f32)) * jax.nn.silu(a_z.astype(f32).reshape(b, s, GDN_HEADS, GDN_DV))
        o_a = o_a.reshape(b, s, GDN_V).astype(x.dtype)

        q_b = rotary(r_q.astype(f32).reshape(b, s, RET_HEADS, RET_DK))
        k_b = rotary(r_k.astype(f32).reshape(b, s, RET_HEADS, RET_DK)) * RET_DK ** -0.5
        v_b = r_v.astype(f32).reshape(b, s, RET_HEADS, RET_DV)
        o_b = retention_chunked(to_chunks(q_b), to_chunks(k_b), to_chunks(v_b), log_gamma)
        o_b = group_norm_heads(o_b, ret_norm_w[l]).reshape(b, s, RET_V) * jax.nn.silu(r_g.astype(f32))
        o_b = o_b.astype(x.dtype)

        mixed = jax.nn.sigmoid(gate_a) * o_a + jax.nn.sigmoid(gate_b) * o_b
        h = h + mixed @ w_out[l]

        hn = rmsnorm(h, norm2_w[l])
        h = h + (jax.nn.silu(hn @ w_gate[l]) * (hn @ w_up[l])) @ w_down[l]
    return rmsnorm(h, norm_f_w)


import jax as _jax
import jax.numpy as _jnp

TWIN_FORMAT = 'train_step'
FWD_PARAMS = ['x', 'norm1_w', 'w_in', 'conv_w', 'a_log', 'dt_bias', 'gdn_norm_w', 'ret_norm_w', 'w_out', 'norm2_w', 'w_gate', 'w_up', 'w_down', 'norm_f_w']
TWIN_WEIGHTS = ['norm1_w', 'w_in', 'conv_w', 'a_log', 'dt_bias', 'gdn_norm_w', 'ret_norm_w', 'w_out', 'norm2_w', 'w_gate', 'w_up', 'w_down', 'norm_f_w']
TWIN_DIFF_INPUT = 'x'
TWIN_INPUTS = ['x', 'norm1_w', 'w_in', 'conv_w', 'a_log', 'dt_bias', 'gdn_norm_w', 'ret_norm_w', 'w_out', 'norm2_w', 'w_gate', 'w_up', 'w_down', 'norm_f_w', 'loss_target', 'm_norm1_w', 'm_w_in', 'm_conv_w', 'm_a_log', 'm_dt_bias', 'm_gdn_norm_w', 'm_ret_norm_w', 'm_w_out', 'm_norm2_w', 'm_w_gate', 'm_w_up', 'm_w_down', 'm_norm_f_w', 'v_norm1_w', 'v_w_in', 'v_conv_w', 'v_a_log', 'v_dt_bias', 'v_gdn_norm_w', 'v_ret_norm_w', 'v_w_out', 'v_norm2_w', 'v_w_gate', 'v_w_up', 'v_w_down', 'v_norm_f_w']
TWIN_OUTPUTS = ['loss', 'grad_x', 'grad_norm1_w', 'grad_w_in', 'grad_conv_w', 'grad_a_log', 'grad_dt_bias', 'grad_gdn_norm_w', 'grad_ret_norm_w', 'grad_w_out', 'grad_norm2_w', 'grad_w_gate', 'grad_w_up', 'grad_w_down', 'grad_norm_f_w', 'delta_norm1_w', 'delta_w_in', 'delta_conv_w', 'delta_a_log', 'delta_dt_bias', 'delta_gdn_norm_w', 'delta_ret_norm_w', 'delta_w_out', 'delta_norm2_w', 'delta_w_gate', 'delta_w_up', 'delta_w_down', 'delta_norm_f_w', 'new_m_norm1_w', 'new_m_w_in', 'new_m_conv_w', 'new_m_a_log', 'new_m_dt_bias', 'new_m_gdn_norm_w', 'new_m_ret_norm_w', 'new_m_w_out', 'new_m_norm2_w', 'new_m_w_gate', 'new_m_w_up', 'new_m_w_down', 'new_m_norm_f_w', 'new_v_norm1_w', 'new_v_w_in', 'new_v_conv_w', 'new_v_a_log', 'new_v_dt_bias', 'new_v_gdn_norm_w', 'new_v_ret_norm_w', 'new_v_w_out', 'new_v_norm2_w', 'new_v_w_gate', 'new_v_w_up', 'new_v_w_down', 'new_v_norm_f_w']
TWIN_LEAF_KINDS = {'loss': 'loss', 'grad_x': 'grad_x', 'grad_norm1_w': 'grad_w', 'grad_w_in': 'grad_w', 'grad_conv_w': 'grad_w', 'grad_a_log': 'grad_w', 'grad_dt_bias': 'grad_w', 'grad_gdn_norm_w': 'grad_w', 'grad_ret_norm_w': 'grad_w', 'grad_w_out': 'grad_w', 'grad_norm2_w': 'grad_w', 'grad_w_gate': 'grad_w', 'grad_w_up': 'grad_w', 'grad_w_down': 'grad_w', 'grad_norm_f_w': 'grad_w', 'delta_norm1_w': 'delta_w', 'delta_w_in': 'delta_w', 'delta_conv_w': 'delta_w', 'delta_a_log': 'delta_w', 'delta_dt_bias': 'delta_w', 'delta_gdn_norm_w': 'delta_w', 'delta_ret_norm_w': 'delta_w', 'delta_w_out': 'delta_w', 'delta_norm2_w': 'delta_w', 'delta_w_gate': 'delta_w', 'delta_w_up': 'delta_w', 'delta_w_down': 'delta_w', 'delta_norm_f_w': 'delta_w', 'new_m_norm1_w': 'new_m', 'new_m_w_in': 'new_m', 'new_m_conv_w': 'new_m', 'new_m_a_log': 'new_m', 'new_m_dt_bias': 'new_m', 'new_m_gdn_norm_w': 'new_m', 'new_m_ret_norm_w': 'new_m', 'new_m_w_out': 'new_m', 'new_m_norm2_w': 'new_m', 'new_m_w_gate': 'new_m', 'new_m_w_up': 'new_m', 'new_m_w_down': 'new_m', 'new_m_norm_f_w': 'new_m', 'new_v_norm1_w': 'new_v', 'new_v_w_in': 'new_v', 'new_v_conv_w': 'new_v', 'new_v_a_log': 'new_v', 'new_v_dt_bias': 'new_v', 'new_v_gdn_norm_w': 'new_v', 'new_v_ret_norm_w': 'new_v', 'new_v_w_out': 'new_v', 'new_v_norm2_w': 'new_v', 'new_v_w_gate': 'new_v', 'new_v_w_up': 'new_v', 'new_v_w_down': 'new_v', 'new_v_norm_f_w': 'new_v'}


def _forward(args):
    return _fwd_reference(*[args[k] for k in FWD_PARAMS])


def _output_shape():
    def fwd():
        inp = _fwd_setup_inputs(0)
        return _fwd_reference(*[inp[k] for k in FWD_PARAMS])
    out = _jax.eval_shape(fwd)
    return out.shape, out.dtype

N_MICROBATCH = 1
ADAM_LR = 0.001
ADAM_B1 = 0.9
ADAM_B2 = 0.999
ADAM_EPS = 1e-08
ADAM_WD = 0.01
ADAM_STEP = 10
PER_EXAMPLE_BATCH_AXIS = {'x': 0, 'loss_target': 0}
SHARED_INPUTS = []
_WEIGHT_DTYPES = {'norm1_w': _jnp.float32, 'w_in': _jnp.float32, 'conv_w': _jnp.float32, 'a_log': _jnp.float32, 'dt_bias': _jnp.float32, 'gdn_norm_w': _jnp.float32, 'ret_norm_w': _jnp.float32, 'w_out': _jnp.float32, 'norm2_w': _jnp.float32, 'w_gate': _jnp.float32, 'w_up': _jnp.float32, 'w_down': _jnp.float32, 'norm_f_w': _jnp.float32}
MOMENT_SCALE = {'norm1_w': 1.117017e-01, 'w_in': 3.875264e-02, 'conv_w': 3.722139e-02, 'a_log': 3.093816e-01, 'dt_bias': 2.931174e-01, 'gdn_norm_w': 1.209666e-01, 'ret_norm_w': 4.269199e-02, 'w_out': 5.773238e-02, 'norm2_w': 8.627858e-02, 'w_gate': 3.719093e-02, 'w_up': 3.605402e-02, 'w_down': 5.979873e-02, 'norm_f_w': 3.197881e+01}


def _to_microbatches(a, axis):
    t = _jnp.moveaxis(a, axis, 0)
    t = t.reshape((N_MICROBATCH, t.shape[0] // N_MICROBATCH) + t.shape[1:])
    return _jnp.moveaxis(t, 1, axis + 1)


def setup_inputs(seed: int = 0) -> dict:
    inp = _fwd_setup_inputs(seed)
    key = _jax.random.fold_in(_jax.random.key(seed), 7919)
    shape, _ = _output_shape()
    out = dict(inp)
    out["loss_target"] = _jax.random.normal(_jax.random.fold_in(key, 0), shape, _jnp.float32)
    for i, name in enumerate(TWIN_WEIGHTS):
        w = inp[name].astype(_jnp.float32)
        if MOMENT_SCALE is None:
            s = _jnp.sqrt(_jnp.mean(_jnp.square(w)) + 1e-30)
        else:
            s = MOMENT_SCALE[name]
        km, kv = _jax.random.split(_jax.random.fold_in(key, i + 1))
        out[name] = w
        out["m_" + name] = s * _jax.random.normal(km, w.shape, _jnp.float32)
        out["v_" + name] = (s * s) * _jax.random.uniform(kv, w.shape, _jnp.float32, 0.5, 1.5)
    if N_MICROBATCH > 1:
        for name, axis in PER_EXAMPLE_BATCH_AXIS.items():
            out[name] = _to_microbatches(out[name], axis)
    return {'x': out['x'], 'norm1_w': out['norm1_w'], 'w_in': out['w_in'], 'conv_w': out['conv_w'], 'a_log': out['a_log'], 'dt_bias': out['dt_bias'], 'gdn_norm_w': out['gdn_norm_w'], 'ret_norm_w': out['ret_norm_w'], 'w_out': out['w_out'], 'norm2_w': out['norm2_w'], 'w_gate': out['w_gate'], 'w_up': out['w_up'], 'w_down': out['w_down'], 'norm_f_w': out['norm_f_w'], 'loss_target': out['loss_target'], 'm_norm1_w': out['m_norm1_w'], 'm_w_in': out['m_w_in'], 'm_conv_w': out['m_conv_w'], 'm_a_log': out['m_a_log'], 'm_dt_bias': out['m_dt_bias'], 'm_gdn_norm_w': out['m_gdn_norm_w'], 'm_ret_norm_w': out['m_ret_norm_w'], 'm_w_out': out['m_w_out'], 'm_norm2_w': out['m_norm2_w'], 'm_w_gate': out['m_w_gate'], 'm_w_up': out['m_w_up'], 'm_w_down': out['m_w_down'], 'm_norm_f_w': out['m_norm_f_w'], 'v_norm1_w': out['v_norm1_w'], 'v_w_in': out['v_w_in'], 'v_conv_w': out['v_conv_w'], 'v_a_log': out['v_a_log'], 'v_dt_bias': out['v_dt_bias'], 'v_gdn_norm_w': out['v_gdn_norm_w'], 'v_ret_norm_w': out['v_ret_norm_w'], 'v_w_out': out['v_w_out'], 'v_norm2_w': out['v_norm2_w'], 'v_w_gate': out['v_w_gate'], 'v_w_up': out['v_w_up'], 'v_w_down': out['v_w_down'], 'v_norm_f_w': out['v_norm_f_w']}


def _loss(weights, diff, rest, loss_target):
    with _jax.named_scope("forward"):
        args = {**rest, TWIN_DIFF_INPUT: diff, **{k: w.astype(_WEIGHT_DTYPES[k]) for k, w in weights.items()}}
        y = _forward(args)
    with _jax.named_scope("loss_head"):
        err = _jnp.square(y.astype(_jnp.float32) - loss_target)
        return 0.5 * _jnp.sum(_jnp.mean(err, axis=-1)) if err.ndim else 0.5 * err


def _adamw(w, g, m, v):
    m = ADAM_B1 * m + (1.0 - ADAM_B1) * g
    v = ADAM_B2 * v + (1.0 - ADAM_B2) * _jnp.square(g)
    m_hat = m / (1.0 - ADAM_B1 ** ADAM_STEP)
    v_hat = v / (1.0 - ADAM_B2 ** ADAM_STEP)
    delta = -ADAM_LR * (m_hat / (_jnp.sqrt(v_hat) + ADAM_EPS) + ADAM_WD * w)
    return delta, m, v


def reference(x, norm1_w, w_in, conv_w, a_log, dt_bias, gdn_norm_w, ret_norm_w, w_out, norm2_w, w_gate, w_up, w_down, norm_f_w, loss_target, m_norm1_w, m_w_in, m_conv_w, m_a_log, m_dt_bias, m_gdn_norm_w, m_ret_norm_w, m_w_out, m_norm2_w, m_w_gate, m_w_up, m_w_down, m_norm_f_w, v_norm1_w, v_w_in, v_conv_w, v_a_log, v_dt_bias, v_gdn_norm_w, v_ret_norm_w, v_w_out, v_norm2_w, v_w_gate, v_w_up, v_w_down, v_norm_f_w):
    given = dict(x=x, norm1_w=norm1_w, w_in=w_in, conv_w=conv_w, a_log=a_log, dt_bias=dt_bias, gdn_norm_w=gdn_norm_w, ret_norm_w=ret_norm_w, w_out=w_out, norm2_w=norm2_w, w_gate=w_gate, w_up=w_up, w_down=w_down, norm_f_w=norm_f_w, loss_target=loss_target, m_norm1_w=m_norm1_w, m_w_in=m_w_in, m_conv_w=m_conv_w, m_a_log=m_a_log, m_dt_bias=m_dt_bias, m_gdn_norm_w=m_gdn_norm_w, m_ret_norm_w=m_ret_norm_w, m_w_out=m_w_out, m_norm2_w=m_norm2_w, m_w_gate=m_w_gate, m_w_up=m_w_up, m_w_down=m_w_down, m_norm_f_w=m_norm_f_w, v_norm1_w=v_norm1_w, v_w_in=v_w_in, v_conv_w=v_conv_w, v_a_log=v_a_log, v_dt_bias=v_dt_bias, v_gdn_norm_w=v_gdn_norm_w, v_ret_norm_w=v_ret_norm_w, v_w_out=v_w_out, v_norm2_w=v_norm2_w, v_w_gate=v_w_gate, v_w_up=v_w_up, v_w_down=v_w_down, v_norm_f_w=v_norm_f_w)
    weights = {n: given[n] for n in TWIN_WEIGHTS}
    shared = {n: given[n] for n in SHARED_INPUTS}
    per_example = {n: given[n] for n in ['x']}
    grad_fn = _jax.value_and_grad(_loss, argnums=(0, 1))

    def one_microbatch(ex, loss_target):
        ex = dict(ex)
        diff = ex.pop(TWIN_DIFF_INPUT)
        return grad_fn(weights, diff, {**shared, **ex}, loss_target)

    if N_MICROBATCH == 1:
        loss, (grad_w, grad_x) = one_microbatch(per_example, given["loss_target"])
    else:
        def body(carry, xs):
            loss_sum, grad_sum = carry
            l_k, (gw_k, gx_k) = one_microbatch(xs[0], xs[1])
            with _jax.named_scope("update"):
                return (loss_sum + l_k, _jax.tree.map(_jnp.add, grad_sum, gw_k)), gx_k

        init = (_jnp.zeros((), _jnp.float32), _jax.tree.map(_jnp.zeros_like, weights))
        (loss, grad_w), grad_x = _jax.lax.scan(body, init, (per_example, given["loss_target"]))
    with _jax.named_scope("update"):
        delta_w, new_m, new_v = {}, {}, {}
        for n in TWIN_WEIGHTS:
            delta_w[n], new_m[n], new_v[n] = _adamw(weights[n], grad_w[n], given["m_" + n], given["v_" + n])
    return (loss, grad_x, *[grad_w[n] for n in TWIN_WEIGHTS], *[delta_w[n] for n in TWIN_WEIGHTS],
            *[new_m[n] for n in TWIN_WEIGHTS], *[new_v[n] for n in TWIN_WEIGHTS])
```

```python
import functools
import numpy as np
import jax
import jax.numpy as jnp
from jax import lax
from jax.experimental import pallas as pl
from jax.experimental.pallas import tpu as pltpu

F32, BF16 = jnp.float32, jnp.bfloat16
HI = lax.Precision.HIGHEST
MESH = pl.DeviceIdType.MESH
ANY = pl.BlockSpec(memory_space=pl.ANY)
VMEM_FULL = pl.BlockSpec(memory_space=pltpu.VMEM)

NDEV = 8
D = 2048
H = 8
DK = 128
DV = 256
C = 64
CPB = 4
EPS = 1e-6
ROPE_BASE = 10000.0
N_IN = 16400
O_Z, O_BA, O_RQ, O_RK, O_RV, O_RG, O_GA, O_GB, P_IN = 4096, 6144, 6400, 7424, 8448, 10496, 12544, 14592, 16640
BA_END = 6160
LR, B1, B2, EPS_ADAM, WD, STEP = 0.001, 0.9, 0.999, 1e-08, 0.01, 10
VMEM_CAP = 60 * 1024 * 1024

NN = ((1,), (0,))
NT = ((1,), (1,))
TN = ((0,), (0,))


def _params(sem=None, est=None):
    kw = {}
    if sem is not None:
        kw["dimension_semantics"] = sem
    if est is not None:
        kw["vmem_limit_bytes"] = int(min(VMEM_CAP, max(32 * 1024 * 1024, est * 5 // 4 + (4 << 20))))
    return pltpu.CompilerParams(**kw)


def _sds(shape, dt=F32):
    return jax.ShapeDtypeStruct(tuple(shape), dt)


def _row_tile(rows, limit):
    return max(t for t in range(8, min(rows, limit) + 1, 8) if rows % t == 0)


def _bdot(a, b, dims):
    return lax.dot_general(a.astype(BF16), b.astype(BF16), (dims, ((), ())), preferred_element_type=F32)


def _hdot(a, b, dims):
    return lax.dot_general(a, b, (dims, ((), ())), precision=HI, preferred_element_type=F32)


def _split2(a):
    hi = a.astype(BF16)
    lo = (a - hi.astype(F32)).astype(BF16)
    return hi, lo


def _dot3(a, b, dims):
    ah, al = _split2(a)
    bh, bl = _split2(b)
    dn = (dims, ((), ()))
    return (lax.dot_general(ah, bh, dn, preferred_element_type=F32)
            + lax.dot_general(ah, bl, dn, preferred_element_type=F32)
            + lax.dot_general(al, bh, dn, preferred_element_type=F32))


def _silu(x):
    return x * jax.nn.sigmoid(x)


def _rms(x, w):
    return x * lax.rsqrt(jnp.mean(x * x, axis=-1, keepdims=True) + EPS) * w


def _matmul(a, b, *, ta=False, tb=False, tm, tn, tk, out_dtype=F32, res=None, name):
    m = a.shape[1] if ta else a.shape[0]
    k = a.shape[0] if ta else a.shape[1]
    n = b.shape[0] if tb else b.shape[1]
    assert k == (b.shape[1] if tb else b.shape[0])
    tm, tn, tk = min(tm, m), min(tn, n), min(tk, k)
    assert m % tm == 0 and n % tn == 0 and k % tk == 0, (name, m, n, k, tm, tn, tk)
    nk = k // tk
    dims = ((0 if ta else 1,), (1 if tb else 0,))
    has_res = res is not None

    def body(*refs):
        a_ref, b_ref = refs[0], refs[1]
        r_ref = refs[2] if has_res else None
        o_ref = refs[3] if has_res else refs[2]

        def finish(acc):
            if has_res:
                acc = acc + r_ref[...].astype(F32)
            o_ref[...] = acc.astype(out_dtype)

        part = _bdot(a_ref[...], b_ref[...], dims)
        if nk == 1:
            finish(part)
        else:
            acc_ref = refs[-1]
            kk = pl.program_id(2)

            @pl.when(kk == 0)
            def _():
                acc_ref[...] = part

            @pl.when(kk > 0)
            def _():
                acc_ref[...] += part

            @pl.when(kk == nk - 1)
            def _():
                finish(acc_ref[...])

    a_spec = pl.BlockSpec((tk, tm), lambda i, j, kk: (kk, i)) if ta else pl.BlockSpec((tm, tk), lambda i, j, kk: (i, kk))
    b_spec = pl.BlockSpec((tn, tk), lambda i, j, kk: (j, kk)) if tb else pl.BlockSpec((tk, tn), lambda i, j, kk: (kk, j))
    o_spec = pl.BlockSpec((tm, tn), lambda i, j, kk: (i, j))
    in_specs = [a_spec, b_spec] + ([o_spec] if has_res else [])
    est = 2 * (tm * tk * a.dtype.itemsize + tk * tn * b.dtype.itemsize + tm * tn * jnp.dtype(out_dtype).itemsize)
    est += 2 * tm * tn * 4 * (1 if has_res else 0) + (tm * tn * 4 if nk > 1 else 0) + 2 * tm * tn * 4
    args = (a, b) + ((res,) if has_res else ())
    return pl.pallas_call(
        body, grid=(m // tm, n // tn, nk), in_specs=in_specs, out_specs=o_spec, out_shape=_sds((m, n), out_dtype),
        scratch_shapes=[pltpu.VMEM((tm, tn), F32)] if nk > 1 else [], name=name,
        compiler_params=_params(("parallel", "parallel", "arbitrary"), est))(*args)


def _rms_fwd(x, w, ts, name):
    s = x.shape[0]

    def body(x_ref, w_ref, o_ref):
        o_ref[...] = _rms(x_ref[...], w_ref[...]).astype(BF16)

    row = pl.BlockSpec((ts, D), lambda i: (i, 0))
    return pl.pallas_call(body, grid=(s // ts,), in_specs=[row, pl.BlockSpec((1, D), lambda i: (0, 0))], out_specs=row,
                          out_shape=_sds((s, D), BF16), name=name, compiler_params=_params(("parallel",)))(x, w)


def _rms_bwd(x, w, du, dres, ts, name):
    s = x.shape[0]

    def body(x_ref, w_ref, du_ref, dres_ref, dx_ref, dw_ref):
        _, vjp = jax.vjp(_rms, x_ref[...], w_ref[...])
        dx, dw = vjp(du_ref[...].astype(F32))
        dx_ref[...] = dx + dres_ref[...]

        @pl.when(pl.program_id(0) == 0)
        def _():
            dw_ref[...] = jnp.zeros_like(dw_ref)

        dw_ref[...] += dw

    row = pl.BlockSpec((ts, D), lambda i: (i, 0))
    vec = pl.BlockSpec((1, D), lambda i: (0, 0))
    return pl.pallas_call(body, grid=(s // ts,), in_specs=[row, vec, row, row], out_specs=[row, vec],
                          out_shape=[_sds((s, D)), _sds((1, D))], name=name,
                          compiler_params=_params(("arbitrary",), 12 * ts * D * 4))(x, w, du, dres)


def _conv_taps(xx, w, base, ts):
    acc = xx[base:base + ts] * w[0:1, :]
    for j in range(1, 4):
        acc = acc + xx[base + j:base + j + ts] * w[j:j + 1, :]
    return acc


def _causal_conv(prev8, cur, w, first):
    xx = jnp.concatenate([jnp.where(first, 0.0, prev8), cur], axis=0)
    return _conv_taps(xx, w, 5, cur.shape[0])


def _qk_post(c, scale):
    s = _silu(c)
    return s * lax.rsqrt(jnp.sum(s * s, axis=-1, keepdims=True) + EPS) * scale


def _conv_specs(ts, cw, col0):
    cur = pl.BlockSpec((ts, cw), lambda j, i: (i, col0 + j))
    prev = pl.BlockSpec((8, cw), lambda j, i: (jnp.maximum(i * (ts // 8) - 1, 0), col0 + j))
    wsp = pl.BlockSpec((4, cw), lambda j, i: (0, col0 + j))
    return cur, prev, wsp


def _gdn_qkv_fwd(proj, conv_w, ts):
    s = proj.shape[0]

    def qk_body(cur_ref, prev_ref, w_ref, o_ref):
        c = _causal_conv(prev_ref[...], cur_ref[...], w_ref[...], pl.program_id(1) == 0)
        scale = jnp.where(pl.program_id(0) < H, DK ** -0.5, 1.0).astype(F32)
        o_ref[...] = _qk_post(c, scale)

    cur, prev, wsp = _conv_specs(ts, DK, 0)
    qk = pl.pallas_call(qk_body, grid=(2 * H, s // ts), in_specs=[cur, prev, wsp],
                        out_specs=pl.BlockSpec((ts, DK), lambda j, i: (i, j)), out_shape=_sds((s, 2 * H * DK)),
                        name="gdn_qk_prep", compiler_params=_params(("parallel", "parallel")))(proj, proj, conv_w)

    def v_body(cur_ref, prev_ref, w_ref, o_ref):
        o_ref[...] = _silu(_causal_conv(prev_ref[...], cur_ref[...], w_ref[...], pl.program_id(1) == 0))

    cw = 512
    cur, prev, wsp = _conv_specs(ts, cw, 2 * H * DK // cw)
    v = pl.pallas_call(v_body, grid=(H * DV // cw, s // ts), in_specs=[cur, prev, wsp],
                       out_specs=pl.BlockSpec((ts, cw), lambda j, i: (i, j)), out_shape=_sds((s, H * DV)),
                       name="gdn_v_prep", compiler_params=_params(("parallel", "parallel")))(proj, proj, conv_w)
    return qk, v


def _gdn_qkv_bwd(proj, conv_w, dqk, dv, ts):
    s = proj.shape[0]
    nt = s // ts

    def qk_body(cur_ref, prev_ref, w_ref, d_ref, o_ref):
        c = _causal_conv(prev_ref[...], cur_ref[...], w_ref[...], pl.program_id(1) == 0)
        scale = jnp.where(pl.program_id(0) < H, DK ** -0.5, 1.0).astype(F32)
        _, vjp = jax.vjp(lambda cc: _qk_post(cc, scale), c)
        o_ref[...] = vjp(d_ref[...])[0]

    cur, prev, wsp = _conv_specs(ts, DK, 0)
    dc_qk = pl.pallas_call(qk_body, grid=(2 * H, nt), in_specs=[cur, prev, wsp, pl.BlockSpec((ts, DK), lambda j, i: (i, j))],
                           out_specs=pl.BlockSpec((ts, DK), lambda j, i: (i, j)), out_shape=_sds((s, 2 * H * DK)),
                           name="gdn_qk_prep_bwd", compiler_params=_params(("parallel", "parallel")))(proj, proj, conv_w, dqk)

    def v_body(cur_ref, prev_ref, w_ref, d_ref, o_ref):
        c = _causal_conv(prev_ref[...], cur_ref[...], w_ref[...], pl.program_id(1) == 0)
        _, vjp = jax.vjp(_silu, c)
        o_ref[...] = vjp(d_ref[...])[0]

    cw = 512
    cur, prev, wsp = _conv_specs(ts, cw, 2 * H * DK // cw)
    dc_v = pl.pallas_call(v_body, grid=(H * DV // cw, nt), in_specs=[cur, prev, wsp, pl.BlockSpec((ts, cw), lambda j, i: (i, j))],
                          out_specs=pl.BlockSpec((ts, cw), lambda j, i: (i, j)), out_shape=_sds((s, H * DV)),
                          name="gdn_v_prep_bwd", compiler_params=_params(("parallel", "parallel")))(proj, proj, conv_w, dv)

    def conv_bwd(dc, col0, dcol0, ncols, name):
        def body(x_ref, xprev_ref, w_ref, dc_ref, dcnext_ref, da_ref, dw_ref):
            i = pl.program_id(1)
            w = w_ref[...]
            dcur = dc_ref[...]
            dd = jnp.concatenate([dcur, jnp.where(i == nt - 1, 0.0, dcnext_ref[...])], axis=0)
            acc = dd[3:3 + ts] * w[0:1, :]
            for j in range(1, 4):
                acc = acc + dd[3 - j:3 - j + ts] * w[j:j + 1, :]
            da_ref[...] = acc.astype(BF16)
            xx = jnp.concatenate([jnp.where(i == 0, 0.0, xprev_ref[...]), x_ref[...]], axis=0)

            @pl.when(i == 0)
            def _():
                dw_ref[...] = jnp.zeros_like(dw_ref)

            for j in range(4):
                dw_ref[j:j + 1, :] += jnp.sum(dcur * xx[5 + j:5 + j + ts], axis=0, keepdims=True)

        cur, prev, wsp = _conv_specs(ts, cw, col0)
        dcur = pl.BlockSpec((ts, cw), lambda j, i: (i, j))
        dnext = pl.BlockSpec((8, cw), lambda j, i: (jnp.minimum((i + 1) * (ts // 8), s // 8 - 1), j))
        return pl.pallas_call(body, grid=(ncols // cw, nt), in_specs=[cur, prev, wsp, dcur, dnext],
                              out_specs=[pl.BlockSpec((ts, cw), lambda j, i: (i, j)), pl.BlockSpec((4, cw), lambda j, i: (0, j))],
                              out_shape=[_sds((s, ncols), BF16), _sds((4, ncols))], name=name,
                              compiler_params=_params(("parallel", "arbitrary")))(proj, proj, conv_w, dc, dc)

    da_qk, dw_qk = conv_bwd(dc_qk, 0, 0, 2 * H * DK, "conv_bwd_qk")
    da_v, dw_v = conv_bwd(dc_v, 2 * H * DK // cw, 0, H * DV, "conv_bwd_v")
    return da_qk, da_v, jnp.concatenate([dw_qk, dw_v], axis=1)


def _bg(b, a, alog, dtb):
    return jax.nn.sigmoid(b), -jnp.exp(alog) * jax.nn.softplus(a + dtb)


def _bg_fwd(proj, alog, dtb, ts):
    s = proj.shape[0]

    def body(ba_ref, alog_ref, dtb_ref, beta_ref, g_ref):
        beta, g = _bg(ba_ref[:, 0:H], ba_ref[:, H:2 * H], alog_ref[...], dtb_ref[...])
        beta_ref[...] = beta
        g_ref[...] = g

    small = pl.BlockSpec((1, H), lambda i: (0, 0))
    out = pl.BlockSpec((ts, H), lambda i: (i, 0))
    return pl.pallas_call(body, grid=(s // ts,), in_specs=[pl.BlockSpec((ts, 256), lambda i: (i, O_BA // 256)), small, small],
                          out_specs=[out, out], out_shape=[_sds((s, H)), _sds((s, H))], name="gdn_bg_prep",
                          compiler_params=_params(("parallel",)))(proj, alog, dtb)


def _bg_bwd(proj, alog, dtb, dbeta_h, dg_h, ts):
    s = proj.shape[0]

    def body(ba_ref, alog_ref, dtb_ref, dbeta_ref, dg_ref, dba_ref, dalog_ref, ddtb_ref):
        _, vjp = jax.vjp(_bg, ba_ref[:, 0:H], ba_ref[:, H:2 * H], alog_ref[...], dtb_ref[...])
        db, da, dalog, ddtb = vjp((jnp.sum(dbeta_ref[...], axis=0), jnp.sum(dg_ref[...], axis=0)))
        dba_ref[...] = jnp.zeros_like(dba_ref)
        dba_ref[:, 0:H] = db.astype(BF16)
        dba_ref[:, H:2 * H] = da.astype(BF16)

        @pl.when(pl.program_id(0) == 0)
        def _():
            dalog_ref[...] = jnp.zeros_like(dalog_ref)
            ddtb_ref[...] = jnp.zeros_like(ddtb_ref)

        dalog_ref[...] += dalog
        ddtb_ref[...] += ddtb

    small = pl.BlockSpec((1, H), lambda i: (0, 0))
    per_head = pl.BlockSpec((H, ts, H), lambda i: (0, i, 0))
    return pl.pallas_call(body, grid=(s // ts,),
                          in_specs=[pl.BlockSpec((ts, 256), lambda i: (i, O_BA // 256)), small, small, per_head, per_head],
                          out_specs=[pl.BlockSpec((ts, 256), lambda i: (i, 0)), small, small],
                          out_shape=[_sds((s, 256), BF16), _sds((1, H)), _sds((1, H))], name="gdn_bg_prep_bwd",
                          compiler_params=_params(("arbitrary",)))(proj, alog, dtb, dbeta_h, dg_h)


@jax.custom_vjp
def _inv_unit_lower(a):
    row = lax.broadcasted_iota(jnp.int32, a.shape, 0)
    col = lax.broadcasted_iota(jnp.int32, a.shape, 1)
    x = jnp.where(row == col, 1.0, 0.0).astype(F32) - a
    p = _dot3(a, a, NN)
    n = 2
    while True:
        x = x + _dot3(x, p, NN)
        n *= 2
        if n >= a.shape[0]:
            return x
        p = _dot3(p, p, NN)


def _inv_fwd(a):
    t = _inv_unit_lower(a)
    return t, t


def _inv_bwd(t, dt):
    return (-_dot3(_dot3(t, dt, TN), t, NT),)


_inv_unit_lower.defvjp(_inv_fwd, _inv_bwd)


def _gdn_chunk(q, k, v, bfull, gfull, state, hmask):
    n = q.shape[0]
    beta = jnp.sum(bfull * hmask, axis=1, keepdims=True)
    g = jnp.sum(gfull * hmask, axis=1, keepdims=True)
    row = lax.broadcasted_iota(jnp.int32, (n, n), 0)
    col = lax.broadcasted_iota(jnp.int32, (n, n), 1)
    incl, strict = row >= col, row > col
    gb = g * jnp.ones((1, n), F32)
    g_i = _hdot(incl.astype(F32), gb, NN)
    g_j = _hdot(jnp.ones((n, n), F32), jnp.where(row <= col, gb, 0.0), NN)
    gc = jnp.sum(jnp.where(col == 0, g_i, 0.0), axis=1, keepdims=True)
    decay = jnp.exp(jnp.where(incl, g_i - g_j, -jnp.inf))
    kb = k * beta
    a = jnp.where(strict, _bdot(kb, k, NT) * decay, 0.0)
    tinv = _inv_unit_lower(a)
    u = _bdot(tinv, v * beta, NN)
    w = _bdot(tinv, kb * jnp.exp(gc), NN)
    attn = _bdot(q, k, NT) * decay
    v_new = u - _bdot(w, state, NN)
    o = _bdot(q * jnp.exp(gc), state, NN) + _bdot(attn, v_new, NN)
    last = lax.broadcasted_iota(jnp.int32, (n, 1), 0) == n - 1
    g_last = jnp.sum(jnp.where(last, gc, 0.0), axis=0, keepdims=True)
    new_state = state * jnp.exp(g_last) + _bdot(k * jnp.exp(g_last - gc), v_new, TN)
    return o, new_state


def _head_mask(h):
    return (lax.broadcasted_iota(jnp.int32, (1, H), 1) == h).astype(F32)


def _gdn_scan_fwd(qk, v, beta, g):
    s = qk.shape[0]
    r = CPB * C
    nb = s // r

    def body(q_ref, k_ref, v_ref, b_ref, g_ref, o_ref, st_ref, state_ref):
        @pl.when(pl.program_id(1) == 0)
        def _():
            state_ref[...] = jnp.zeros_like(state_ref)

        hmask = _head_mask(pl.program_id(0))
        state = state_ref[...]
        for i in range(CPB):
            rs = slice(i * C, (i + 1) * C)
            st_ref[0, i] = state
            o, state = _gdn_chunk(q_ref[rs, :], k_ref[rs, :], v_ref[rs, :], b_ref[rs, :], g_ref[rs, :], state, hmask)
            o_ref[rs, :] = o
        state_ref[...] = state

    in_specs = [pl.BlockSpec((r, DK), lambda h, c: (c, h)), pl.BlockSpec((r, DK), lambda h, c: (c, H + h)),
                pl.BlockSpec((r, DV), lambda h, c: (c, h)), pl.BlockSpec((r, H), lambda h, c: (c, 0)),
                pl.BlockSpec((r, H), lambda h, c: (c, 0))]
    out_specs = [pl.BlockSpec((r, DV), lambda h, c: (c, h)), pl.BlockSpec((1, CPB, DK, DV), lambda h, c: (h, c, 0, 0))]
    return pl.pallas_call(body, grid=(H, nb), in_specs=in_specs, out_specs=out_specs,
                          out_shape=[_sds((s, H * DV)), _sds((H, s // C, DK, DV))],
                          scratch_shapes=[pltpu.VMEM((DK, DV), F32)], name="gdn_scan_fwd",
                          compiler_params=_params(("parallel", "arbitrary")))(qk, qk, v, beta, g)


def _gdn_scan_bwd(qk, v, beta, g, states, do):
    s = qk.shape[0]
    r = CPB * C
    nb = s // r

    def body(q_ref, k_ref, v_ref, b_ref, g_ref, st_ref, do_ref, dq_ref, dk_ref, dv_ref, db_ref, dg_ref, dstate_ref):
        @pl.when(pl.program_id(1) == 0)
        def _():
            dstate_ref[...] = jnp.zeros_like(dstate_ref)

        hmask = _head_mask(pl.program_id(0))
        dstate = dstate_ref[...]
        for i in reversed(range(CPB)):
            rs = slice(i * C, (i + 1) * C)
            _, vjp = jax.vjp(lambda q, k, v, b, g, st: _gdn_chunk(q, k, v, b, g, st, hmask),
                             q_ref[rs, :], k_ref[rs, :], v_ref[rs, :], b_ref[rs, :], g_ref[rs, :], st_ref[0, i])
            dq, dk, dv, db, dg, dstate = vjp((do_ref[rs, :], dstate))
            dq_ref[rs, :] = dq
            dk_ref[rs, :] = dk
            dv_ref[rs, :] = dv
            db_ref[0, rs, :] = db
            dg_ref[0, rs, :] = dg
        dstate_ref[...] = dstate

    rev = lambda c: nb - 1 - c
    in_specs = [pl.BlockSpec((r, DK), lambda h, c: (rev(c), h)), pl.BlockSpec((r, DK), lambda h, c: (rev(c), H + h)),
                pl.BlockSpec((r, DV), lambda h, c: (rev(c), h)), pl.BlockSpec((r, H), lambda h, c: (rev(c), 0)),
                pl.BlockSpec((r, H), lambda h, c: (rev(c), 0)),
                pl.BlockSpec((1, CPB, DK, DV), lambda h, c: (h, rev(c), 0, 0)),
                pl.BlockSpec((r, DV), lambda h, c: (rev(c), h))]
    out_specs = [pl.BlockSpec((r, DK), lambda h, c: (rev(c), h)), pl.BlockSpec((r, DK), lambda h, c: (rev(c), h)),
                 pl.BlockSpec((r, DV), lambda h, c: (rev(c), h)),
                 pl.BlockSpec((1, r, H), lambda h, c: (h, rev(c), 0)), pl.BlockSpec((1, r, H), lambda h, c: (h, rev(c), 0))]
    dq, dk, dv, db, dg = pl.pallas_call(
        body, grid=(H, nb), in_specs=in_specs, out_specs=out_specs,
        out_shape=[_sds((s, H * DK)), _sds((s, H * DK)), _sds((s, H * DV)), _sds((H, s, H)), _sds((H, s, H))],
        scratch_shapes=[pltpu.VMEM((DK, DV), F32)], name="gdn_scan_bwd",
        compiler_params=_params(("parallel", "arbitrary")))(qk, qk, v, beta, g, states, do)
    return jnp.concatenate([dq, dk], axis=1), dv, db, dg


def _rot(x, cs, sn):
    return x * cs + pltpu.roll(x, DK // 2, 1) * sn


def _rot_t(d, cs, sn):
    return d * cs - pltpu.roll(d, DK // 2, 1) * sn


def _ret_chunk(q, k, v, state, lg):
    n = q.shape[0]
    row = lax.broadcasted_iota(jnp.int32, (n, n), 0)
    col = lax.broadcasted_iota(jnp.int32, (n, n), 1)
    dist = (row - col).astype(F32)
    dmat = jnp.exp(jnp.where(dist >= 0, dist * lg, -jnp.inf))
    scores = _bdot(q, k, NT) * dmat
    pos = lax.broadcasted_iota(jnp.int32, (n, 1), 0).astype(F32)
    xi = jnp.exp((pos + 1.0) * lg)
    zeta = jnp.exp((n - 1.0 - pos) * lg)
    o = _bdot(scores, v, NN) + _bdot(q, state, NN) * xi
    new_state = state * jnp.exp(n * lg) + _bdot(k * zeta, v, TN)
    return o, new_state


def _ret_specs(r, order):
    return [pl.BlockSpec((r, DK), lambda h, c: (order(c), O_RQ // DK + h)), pl.BlockSpec((r, DK), lambda h, c: (order(c), O_RK // DK + h)),
            pl.BlockSpec((r, DV), lambda h, c: (order(c), O_RV // DV + h)), pl.BlockSpec((r, DK), lambda h, c: (order(c), 0)),
            pl.BlockSpec((r, DK), lambda h, c: (order(c), 0)), pl.BlockSpec((1, H), lambda h, c: (0, 0))]


def _ret_scan_fwd(proj, cs, sn, lgtab):
    s = proj.shape[0]
    r = CPB * C
    nb = s // r

    def body(q_ref, k_ref, v_ref, cs_ref, sn_ref, lg_ref, o_ref, st_ref, state_ref):
        @pl.when(pl.program_id(1) == 0)
        def _():
            state_ref[...] = jnp.zeros_like(state_ref)

        lg = jnp.sum(lg_ref[...] * _head_mask(pl.program_id(0)), axis=1, keepdims=True)
        state = state_ref[...]
        for i in range(CPB):
            rs = slice(i * C, (i + 1) * C)
            st_ref[0, i] = state
            q = _rot(q_ref[rs, :], cs_ref[rs, :], sn_ref[rs, :])
            k = _rot(k_ref[rs, :], cs_ref[rs, :], sn_ref[rs, :]) * DK ** -0.5
            o, state = _ret_chunk(q, k, v_ref[rs, :], state, lg)
            o_ref[rs, :] = o
        state_ref[...] = state

    out_specs = [pl.BlockSpec((r, DV), lambda h, c: (c, h)), pl.BlockSpec((1, CPB, DK, DV), lambda h, c: (h, c, 0, 0))]
    return pl.pallas_call(body, grid=(H, nb), in_specs=_ret_specs(r, lambda c: c), out_specs=out_specs,
                          out_shape=[_sds((s, H * DV)), _sds((H, s // C, DK, DV))],
                          scratch_shapes=[pltpu.VMEM((DK, DV), F32)], name="ret_scan_fwd",
                          compiler_params=_params(("parallel", "arbitrary")))(proj, proj, proj, cs, sn, lgtab)


def _ret_scan_bwd(proj, cs, sn, lgtab, states, do):
    s = proj.shape[0]
    r = CPB * C
    nb = s // r

    def body(q_ref, k_ref, v_ref, cs_ref, sn_ref, lg_ref, st_ref, do_ref, dq_ref, dk_ref, dv_ref, dstate_ref):
        @pl.when(pl.program_id(1) == 0)
        def _():
            dstate_ref[...] = jnp.zeros_like(dstate_ref)

        lg = jnp.sum(lg_ref[...] * _head_mask(pl.program_id(0)), axis=1, keepdims=True)
        dstate = dstate_ref[...]
        for i in reversed(range(CPB)):
            rs = slice(i * C, (i + 1) * C)
            cs_, sn_ = cs_ref[rs, :], sn_ref[rs, :]
            q = _rot(q_ref[rs, :], cs_, sn_)
            k = _rot(k_ref[rs, :], cs_, sn_) * DK ** -0.5
            _, vjp = jax.vjp(lambda q, k, v, st: _ret_chunk(q, k, v, st, lg), q, k, v_ref[rs, :], st_ref[0, i])
            dq, dk, dv, dstate = vjp((do_ref[rs, :], dstate))
            dq_ref[rs, :] = _rot_t(dq, cs_, sn_).astype(BF16)
            dk_ref[rs, :] = _rot_t(dk * DK ** -0.5, cs_, sn_).astype(BF16)
            dv_ref[rs, :] = dv.astype(BF16)
        dstate_ref[...] = dstate

    rev = lambda c: nb - 1 - c
    in_specs = _ret_specs(r, rev) + [pl.BlockSpec((1, CPB, DK, DV), lambda h, c: (h, rev(c), 0, 0)),
                                     pl.BlockSpec((r, DV), lambda h, c: (rev(c), h))]
    out_specs = [pl.BlockSpec((r, DK), lambda h, c: (rev(c), h)), pl.BlockSpec((r, DK), lambda h, c: (rev(c), h)),
                 pl.BlockSpec((r, DV), lambda h, c: (rev(c), h))]
    dq, dk, dv = pl.pallas_call(
        body, grid=(H, nb), in_specs=in_specs, out_specs=out_specs,
        out_shape=[_sds((s, H * DK), BF16), _sds((s, H * DK), BF16), _sds((s, H * DV), BF16)],
        scratch_shapes=[pltpu.VMEM((DK, DV), F32)], name="ret_scan_bwd",
        compiler_params=_params(("parallel", "arbitrary")))(proj, proj, proj, cs, sn, lgtab, states, do)
    return dq, dk, dv


def _merge(oa, z, ob, rg, ga, gb, wa, wb):
    ya = oa * lax.rsqrt(jnp.mean(oa * oa, axis=-1, keepdims=True) + EPS) * wa * _silu(z)
    mu = jnp.mean(ob, axis=-1, keepdims=True)
    var = jnp.mean(jnp.square(ob - mu), axis=-1, keepdims=True)
    yb = (ob - mu) * lax.rsqrt(var + EPS) * wb * _silu(rg)
    return jax.nn.sigmoid(ga) * ya + jax.nn.sigmoid(gb) * yb


def _merge_specs(ts):
    blk = lambda off: pl.BlockSpec((ts, DV), lambda h, i: (i, off // DV + h))
    return [blk(0), blk(O_Z), blk(0), blk(O_RG), blk(O_GA), blk(O_GB),
            pl.BlockSpec((1, DV), lambda h, i: (0, 0)), pl.BlockSpec((1, DV), lambda h, i: (0, h))]


def _merge_fwd(oa, ob, proj, wa, wb, ts):
    s = oa.shape[0]

    def body(oa_ref, z_ref, ob_ref, rg_ref, ga_ref, gb_ref, wa_ref, wb_ref, o_ref):
        o_ref[...] = _merge(oa_ref[...], z_ref[...], ob_ref[...], rg_ref[...], ga_ref[...], gb_ref[...],
                            wa_ref[...], wb_ref[...]).astype(BF16)

    return pl.pallas_call(body, grid=(H, s // ts), in_specs=_merge_specs(ts), out_specs=pl.BlockSpec((ts, DV), lambda h, i: (i, h)),
                          out_shape=_sds((s, H * DV), BF16), name="merge_fwd",
                          compiler_params=_params(("parallel", "parallel")))(oa, proj, ob, proj, proj, proj, wa, wb)


def _merge_bwd(oa, ob, proj, wa, wb, dmixed, ts):
    s = oa.shape[0]

    def body(oa_ref, z_ref, ob_ref, rg_ref, ga_ref, gb_ref, wa_ref, wb_ref, dm_ref,
             doa_ref, dob_ref, dz_ref, drg_ref, dga_ref, dgb_ref, dwa_ref, dwb_ref):
        _, vjp = jax.vjp(_merge, oa_ref[...], z_ref[...], ob_ref[...], rg_ref[...], ga_ref[...], gb_ref[...],
                         wa_ref[...], wb_ref[...])
        doa, dz, dob, drg, dga, dgb, dwa, dwb = vjp(dm_ref[...].astype(F32))
        doa_ref[...] = doa
        dob_ref[...] = dob
        dz_ref[...] = dz.astype(BF16)
        drg_ref[...] = drg.astype(BF16)
        dga_ref[...] = dga.astype(BF16)
        dgb_ref[...] = dgb.astype(BF16)
        first_tile = pl.program_id(1) == 0

        @pl.when(first_tile & (pl.program_id(0) == 0))
        def _():
            dwa_ref[...] = jnp.zeros_like(dwa_ref)

        @pl.when(first_tile)
        def _():
            dwb_ref[...] = jnp.zeros_like(dwb_ref)

        dwa_ref[...] += dwa
        dwb_ref[...] += dwb

    blk = pl.BlockSpec((ts, DV), lambda h, i: (i, h))
    out_specs = [blk] * 6 + [pl.BlockSpec((1, DV), lambda h, i: (0, 0)), pl.BlockSpec((1, DV), lambda h, i: (0, h))]
    out_shape = [_sds((s, H * DV)), _sds((s, H * DV))] + [_sds((s, H * DV), BF16)] * 4 + [_sds((1, DV)), _sds((1, H * DV))]
    return pl.pallas_call(body, grid=(H, s // ts), in_specs=_merge_specs(ts) + [blk], out_specs=out_specs, out_shape=out_shape,
                          name="merge_bwd", compiler_params=_params(("arbitrary", "arbitrary"), 40 * ts * DV * 4))(
                              oa, proj, ob, proj, proj, proj, wa, wb, dmixed)


def _act(hg, hu):
    return _silu(hg) * hu


def _act_fwd(hg, hu, ts, tc):
    s, f = hg.shape
    blk = pl.BlockSpec((ts, tc), lambda i, j: (i, j))

    def body(hg_ref, hu_ref, o_ref):
        o_ref[...] = _act(hg_ref[...], hu_ref[...]).astype(BF16)

    return pl.pallas_call(body, grid=(s // ts, f // tc), in_specs=[blk, blk], out_specs=blk, out_shape=_sds((s, f), BF16),
                          name="swiglu_fwd", compiler_params=_params(("parallel", "parallel")))(hg, hu)


def _act_bwd(hg, hu, dact, ts, tc):
    s, f = hg.shape
    blk = pl.BlockSpec((ts, tc), lambda i, j: (i, j))

    def body(hg_ref, hu_ref, d_ref, dhg_ref, dhu_ref):
        _, vjp = jax.vjp(_act, hg_ref[...], hu_ref[...])
        dhg, dhu = vjp(d_ref[...].astype(F32))
        dhg_ref[...] = dhg.astype(BF16)
        dhu_ref[...] = dhu.astype(BF16)

    return pl.pallas_call(body, grid=(s // ts, f // tc), in_specs=[blk, blk, blk], out_specs=[blk, blk],
                          out_shape=[_sds((s, f), BF16), _sds((s, f), BF16)], name="swiglu_bwd",
                          compiler_params=_params(("parallel", "parallel")))(hg, hu, dact)


def _loss_rows(h2, wf, tgt):
    err = _rms(h2, wf) - tgt
    return 0.5 * jnp.sum(jnp.mean(err * err, axis=-1, keepdims=True), keepdims=True)


def _loss_fwd_bwd(h2, wf, tgt, ts):
    s = h2.shape[0]

    def body(h_ref, w_ref, t_ref, loss_ref, dh_ref, dw_ref):
        loss, vjp = jax.vjp(_loss_rows, h_ref[...], w_ref[...], t_ref[...])
        dh, dw, _ = vjp(jnp.ones((1, 1), F32))
        dh_ref[...] = dh

        @pl.when(pl.program_id(0) == 0)
        def _():
            loss_ref[...] = jnp.zeros_like(loss_ref)
            dw_ref[...] = jnp.zeros_like(dw_ref)

        loss_ref[...] += loss
        dw_ref[...] += dw

    row = pl.BlockSpec((ts, D), lambda i: (i, 0))
    vec = pl.BlockSpec((1, D), lambda i: (0, 0))
    tile = pl.BlockSpec((8, 128), lambda i: (0, 0))
    return pl.pallas_call(body, grid=(s // ts,), in_specs=[row, vec, row], out_specs=[tile, row, vec],
                          out_shape=[_sds((8, 128)), _sds((s, D)), _sds((1, D))], name="final_norm_loss",
                          compiler_params=_params(("arbitrary",), 12 * ts * D * 4))(h2, wf, tgt)


def _rope_tables(s):
    inv = ROPE_BASE ** (-jnp.arange(0, DK, 2, dtype=F32) / DK)
    ang = jnp.arange(s, dtype=F32)[:, None] * inv[None, :]
    cos, sin = jnp.cos(ang), jnp.sin(ang)
    return jnp.concatenate([cos, cos], axis=1), jnp.concatenate([-sin, sin], axis=1)


def _local_step(x, tgt, w_in, w_out, w_gate, w_up, w_down, norm1_w, conv_w, a_log, dt_bias, gdn_norm_w, ret_norm_w, norm2_w, norm_f_w):
    s = x.shape[0]
    ts = min(512, s)
    cs, sn = _rope_tables(s)
    lgtab = jnp.log1p(-jnp.exp2(-5.0 - jnp.arange(H, dtype=F32))).reshape(1, H)

    u = _rms_fwd(x, norm1_w, ts, "norm1_fwd")
    proj = _matmul(u, w_in, tm=1024, tn=1280, tk=D, name="in_proj")
    qk, va = _gdn_qkv_fwd(proj, conv_w, ts)
    beta, g = _bg_fwd(proj, a_log, dt_bias, ts)
    oa, st_a = _gdn_scan_fwd(qk, va, beta, g)
    ob, st_b = _ret_scan_fwd(proj, cs, sn, lgtab)
    mixed = _merge_fwd(oa, ob, proj, gdn_norm_w, ret_norm_w, ts)
    h1 = _matmul(mixed, w_out, tm=1024, tn=1024, tk=D, res=x, name="out_proj")
    hn = _rms_fwd(h1, norm2_w, ts, "norm2_fwd")
    hg = _matmul(hn, w_gate, tm=1024, tn=1408, tk=D, name="ffn_gate")
    hu = _matmul(hn, w_up, tm=1024, tn=1408, tk=D, name="ffn_up")
    act = _act_fwd(hg, hu, min(256, s), 1408)
    h2 = _matmul(act, w_down, tm=1024, tn=1024, tk=1408, res=h1, name="ffn_down")
    loss, dh2, d_norm_f = _loss_fwd_bwd(h2, norm_f_w, tgt, ts)

    dact = _matmul(dh2, w_down, tb=True, tm=1024, tn=1408, tk=D, out_dtype=BF16, name="ffn_down_dx")
    g_down = _matmul(act, dh2, ta=True, tm=1408, tn=1024, tk=512, name="ffn_down_dw")
    dhg, dhu = _act_bwd(hg, hu, dact, min(256, s), 1408)
    g_gate = _matmul(hn, dhg, ta=True, tm=1024, tn=1408, tk=512, name="ffn_gate_dw")
    g_up = _matmul(hn, dhu, ta=True, tm=1024, tn=1408, tk=512, name="ffn_up_dw")
    dhn = _matmul(dhg, w_gate, tb=True, tm=1024, tn=1024, tk=1408, name="ffn_gate_dx")
    dhn = _matmul(dhu, w_up, tb=True, tm=1024, tn=1024, tk=1408, res=dhn, name="ffn_up_dx")
    dh1, d_norm2 = _rms_bwd(h1, norm2_w, dhn, dh2, ts, "norm2_bwd")

    dmixed = _matmul(dh1, w_out, tb=True, tm=1024, tn=1024, tk=D, out_dtype=BF16, name="out_proj_dx")
    g_out = _matmul(mixed, dh1, ta=True, tm=1024, tn=1024, tk=512, name="out_proj_dw")
    doa, dob, dz, drg, dga, dgb, d_gdn_norm, d_ret_norm = _merge_bwd(oa, ob, proj, gdn_norm_w, ret_norm_w, dmixed, ts)

    drq, drk, drv = _ret_scan_bwd(proj, cs, sn, lgtab, st_b, dob)
    dqk, dva, dbeta_h, dg_h = _gdn_scan_bwd(qk, va, beta, g, st_a, doa)
    da_qk, da_v, d_conv = _gdn_qkv_bwd(proj, conv_w, dqk, dva, ts)
    dba, d_a_log, d_dt_bias = _bg_bwd(proj, a_log, dt_bias, dbeta_h, dg_h, ts)

    dproj = jnp.concatenate([da_qk, da_v, dz, dba, drq, drk, drv, drg, dga, dgb], axis=1)
    du = _matmul(dproj, w_in, tb=True, tm=1024, tn=1024, tk=1280, name="in_proj_dx")
    g_in = _matmul(u, dproj, ta=True, tm=1024, tn=1280, tk=512, name="in_proj_dw")
    dx, d_norm1 = _rms_bwd(x, norm1_w, du, dh1, ts, "norm1_bwd")

    big = dict(w_in=g_in, w_out=g_out, w_gate=g_gate, w_up=g_up, w_down=g_down)
    small = dict(norm1_w=d_norm1, conv_w=d_conv, a_log=d_a_log, dt_bias=d_dt_bias, gdn_norm_w=d_gdn_norm,
                 ret_norm_w=d_ret_norm, norm2_w=d_norm2, norm_f_w=d_norm_f)
    return loss, dx, big, small


def _coords():
    return lax.axis_index("x"), lax.axis_index("y"), lax.axis_index("c")


def _allgather(shards, name):
    n = len(shards)

    def body(*refs):
        ins, outs = refs[:n], refs[n:2 * n]
        send_sems, recv_sems, local_sems = refs[2 * n:]
        x, y, c = _coords()
        me, sibling = (x, y, c), (x, y, 1 - c)
        chips = [(1 - x, y), (x, 1 - y), (1 - x, 1 - y)]

        def copy(a, k, block, to, src=None):
            px, py, pc = block
            dst = outs[a].at[4 * px + 2 * py + pc]
            return pltpu.make_async_remote_copy(src_ref=dst if src is None else src, dst_ref=dst, send_sem=send_sems.at[a, k],
                                                recv_sem=recv_sems.at[a, k], device_id=to, device_id_type=MESH)

        mine = [pltpu.make_async_copy(ins[a], outs[a].at[4 * x + 2 * y + c], local_sems.at[a]) for a in range(n)]
        for cp in mine:
            cp.start()
        first = []
        for a in range(n):
            first.append(copy(a, 0, me, sibling, src=ins[a]))
            first += [copy(a, 1 + j, me, (*chip, c), src=ins[a]) for j, chip in enumerate(chips)]
        for cp in first:
            cp.start()
        passed = []
        for j, chip in enumerate(chips):
            for a in range(n):
                copy(a, 1 + j, (*chip, c), me).wait_recv()
                fwd = copy(a, 4 + j, (*chip, c), sibling)
                fwd.start()
                passed.append(fwd)
        for a in range(n):
            copy(a, 0, sibling, me).wait_recv()
            for j, chip in enumerate(chips):
                copy(a, 4 + j, (*chip, 1 - c), me).wait_recv()
        for cp in first + passed:
            cp.wait_send()
        for cp in mine:
            cp.wait()

    return pl.pallas_call(
        body, in_specs=[ANY] * n, out_specs=[ANY] * n, out_shape=[_sds((NDEV,) + a.shape, a.dtype) for a in shards],
        scratch_shapes=[pltpu.SemaphoreType.DMA((n, 7)), pltpu.SemaphoreType.DMA((n, 7)), pltpu.SemaphoreType.DMA((n,))],
        name=name)(*shards)


def _exchange_sibling(slots, name):
    n = len(slots)

    def body(*refs):
        ins, outs = refs[:n], refs[n:2 * n]
        send_sems, recv_sems = refs[2 * n:]
        x, y, c = _coords()
        copies = []
        for a in range(n):
            for j in range(4):
                copies.append(pltpu.make_async_remote_copy(
                    src_ref=ins[a].at[2 * j + (1 - c)], dst_ref=outs[a].at[j], send_sem=send_sems.at[a, j],
                    recv_sem=recv_sems.at[a, j], device_id=(x, y, 1 - c), device_id_type=MESH))
        for cp in copies:
            cp.start()
        for cp in copies:
            cp.wait()

    return pl.pallas_call(
        body, in_specs=[ANY] * n, out_specs=[ANY] * n, out_shape=[_sds((4,) + a.shape[1:], a.dtype) for a in slots],
        scratch_shapes=[pltpu.SemaphoreType.DMA((n, 4)), pltpu.SemaphoreType.DMA((n, 4))], name=name)(*slots)


def _exchange_chips(parts, name):
    n = len(parts)

    def body(*refs):
        ins, outs = refs[:n], refs[n:2 * n]
        send_sems, recv_sems = refs[2 * n:]
        x, y, c = _coords()
        chips = [(1 - x, y), (x, 1 - y), (1 - x, 1 - y)]
        copies = []
        for a in range(n):
            for k, (px, py) in enumerate(chips):
                copies.append(pltpu.make_async_remote_copy(
                    src_ref=ins[a].at[2 * px + py], dst_ref=outs[a].at[k], send_sem=send_sems.at[a, k],
                    recv_sem=recv_sems.at[a, k], device_id=(px, py, c), device_id_type=MESH))
        for cp in copies:
            cp.start()
        for cp in copies:
            cp.wait()

    return pl.pallas_call(
        body, in_specs=[ANY] * n, out_specs=[ANY] * n, out_shape=[_sds((3,) + a.shape[1:], a.dtype) for a in parts],
        scratch_shapes=[pltpu.SemaphoreType.DMA((n, 3)), pltpu.SemaphoreType.DMA((n, 3))], name=name)(*parts)


def _allreduce_small(pack, name):
    rows, cols = pack.shape

    def body(in_ref, out_ref, buf_ref, send_sems, recv_sems):
        x, y, c = _coords()
        mine = 4 * x + 2 * y + c
        buf_ref[mine] = in_ref[...]
        copies = []
        for r in range(1, NDEV):
            peer = (x ^ (r >> 2), y ^ ((r >> 1) & 1), c ^ (r & 1))
            copies.append(pltpu.make_async_remote_copy(
                src_ref=in_ref, dst_ref=buf_ref.at[mine], send_sem=send_sems.at[r - 1], recv_sem=recv_sems.at[r - 1],
                device_id=peer, device_id_type=MESH))
        for cp in copies:
            cp.start()
        for r in range(1, NDEV):
            pltpu.make_async_remote_copy(
                src_ref=in_ref, dst_ref=buf_ref.at[mine ^ r], send_sem=send_sems.at[r - 1], recv_sem=recv_sems.at[r - 1],
                device_id=(x, y, c), device_id_type=MESH).wait_recv()
        for cp in copies:
            cp.wait_send()
        acc = buf_ref[0]
        for d in range(1, NDEV):
            acc = acc + buf_ref[d]
        out_ref[...] = acc

    return pl.pallas_call(
        body, in_specs=[VMEM_FULL], out_specs=VMEM_FULL, out_shape=_sds((rows, cols)),
        scratch_shapes=[pltpu.VMEM((NDEV, rows, cols), F32), pltpu.SemaphoreType.DMA((NDEV - 1,)), pltpu.SemaphoreType.DMA((NDEV - 1,))],
        name=name)(pack)


def _add_sibling(slots, recv, core, tr, name):
    _, rows, cols = slots.shape
    tr = _row_tile(rows, tr)

    def body(c_ref, a_ref, b_ref, o_ref):
        o_ref[...] = a_ref[...] + b_ref[...]

    gs = pltpu.PrefetchScalarGridSpec(
        num_scalar_prefetch=1, grid=(4, rows // tr),
        in_specs=[pl.BlockSpec((None, tr, cols), lambda j, i, cr: (2 * j + cr[0], i, 0)),
                  pl.BlockSpec((None, tr, cols), lambda j, i, cr: (j, i, 0))],
        out_specs=pl.BlockSpec((None, tr, cols), lambda j, i, cr: (j, i, 0)))
    return pl.pallas_call(body, grid_spec=gs, out_shape=_sds((4, rows, cols)), name=name,
                          compiler_params=_params(("parallel", "parallel"), 6 * tr * cols * 4))(core, slots, recv)


def _adam_math(w, g, m, v):
    m2 = B1 * m + (1.0 - B1) * g
    v2 = B2 * v + (1.0 - B2) * jnp.square(g)
    m_hat = m2 / (1.0 - B1 ** STEP)
    v_hat = v2 / (1.0 - B2 ** STEP)
    return -LR * (m_hat / (jnp.sqrt(v_hat) + EPS_ADAM) + WD * w), m2, v2


def _adamw_reduced(part, recv, chip, w, m, v, tr, name):
    rows, cols = w.shape
    tr = _row_tile(rows, tr)

    def body(j_ref, p_ref, r0_ref, r1_ref, r2_ref, w_ref, m_ref, v_ref, g_ref, d_ref, nm_ref, nv_ref):
        g = p_ref[...] + r0_ref[...] + r1_ref[...] + r2_ref[...]
        d, m2, v2 = _adam_math(w_ref[...], g, m_ref[...], v_ref[...])
        g_ref[...] = g
        d_ref[...] = d
        nm_ref[...] = m2
        nv_ref[...] = v2

    flat = pl.BlockSpec((tr, cols), lambda i, jr: (i, 0))
    gs = pltpu.PrefetchScalarGridSpec(
        num_scalar_prefetch=1, grid=(rows // tr,),
        in_specs=[pl.BlockSpec((None, tr, cols), lambda i, jr: (jr[0], i, 0))]
        + [pl.BlockSpec((None, tr, cols), functools.partial(lambda i, jr, k: (k, i, 0), k=k)) for k in range(3)] + [flat] * 3,
        out_specs=[flat] * 4)
    return pl.pallas_call(body, grid_spec=gs, out_shape=[_sds((rows, cols))] * 4, name=name,
                          compiler_params=_params(("parallel",), 22 * tr * cols * 4))(chip, part, recv, recv, recv, w, m, v)


def _adamw_plain(w, g, m, v, name):
    def body(w_ref, g_ref, m_ref, v_ref, d_ref, nm_ref, nv_ref):
        d, m2, v2 = _adam_math(w_ref[...], g_ref[...], m_ref[...], v_ref[...])
        d_ref[...] = d
        nm_ref[...] = m2
        nv_ref[...] = v2

    return pl.pallas_call(body, out_shape=[_sds(w.shape)] * 3, name=name)(w, g, m, v)


def _pack_small(norm1_w, conv_w, a_log, dt_bias, gdn_norm_w, ret_norm_w, norm2_w, norm_f_w):
    misc = jnp.concatenate([gdn_norm_w.reshape(1, DV), a_log.reshape(1, H), dt_bias.reshape(1, H),
                            jnp.zeros((1, D - DV - 2 * H), F32)], axis=1)
    return jnp.concatenate([norm1_w.reshape(1, D), ret_norm_w.reshape(1, D), norm2_w.reshape(1, D), norm_f_w.reshape(1, D),
                            conv_w.reshape(8, D), misc, jnp.zeros((3, D), F32)], axis=0)


def _unpack_small(pack):
    return dict(norm1_w=pack[0:1], ret_norm_w=pack[1:2], norm2_w=pack[2:3], norm_f_w=pack[3], conv_w=pack[4:12].reshape(4, 2 * D),
                gdn_norm_w=pack[12:13, 0:DV], a_log=pack[12:13, DV:DV + H], dt_bias=pack[12:13, DV + H:DV + 2 * H])


def _regroup_w_in(w):
    return jnp.concatenate([w[:, :BA_END], jnp.zeros((w.shape[0], P_IN - N_IN), w.dtype), w[:, BA_END:]], axis=1)


def _ungroup_w_in(g):
    return jnp.concatenate([g[:, :BA_END], g[:, BA_END + P_IN - N_IN:]], axis=1)


def _to_slots_cols(g):
    rows, cols = g.shape
    return g.reshape(rows, NDEV, cols // NDEV).transpose(1, 0, 2)


def _from_slots_cols(a):
    n, rows, cols = a.shape
    return a.transpose(1, 0, 2).reshape(rows, n * cols)


WEIGHT_ORDER = ["norm1_w", "w_in", "conv_w", "a_log", "dt_bias", "gdn_norm_w", "ret_norm_w", "w_out", "norm2_w", "w_gate", "w_up",
                "w_down", "norm_f_w"]


def kernel(x, norm1_w, w_in, conv_w, a_log, dt_bias, gdn_norm_w, ret_norm_w, w_out, norm2_w, w_gate, w_up, w_down, norm_f_w, loss_target, m_norm1_w, m_w_in, m_conv_w, m_a_log, m_dt_bias, m_gdn_norm_w, m_ret_norm_w, m_w_out, m_norm2_w, m_w_gate, m_w_up, m_w_down, m_norm_f_w, v_norm1_w, v_w_in, v_conv_w, v_a_log, v_dt_bias, v_gdn_norm_w, v_ret_norm_w, v_w_out, v_norm2_w, v_w_gate, v_w_up, v_w_down, v_norm_f_w):
    ax, ay, ac = _coords()
    me = 4 * ax + 2 * ay + ac
    core = jnp.reshape(ac, (1,)).astype(jnp.int32)
    chip = jnp.reshape(2 * ax + ay, (1,)).astype(jnp.int32)
    w = dict(norm1_w=norm1_w, w_in=w_in[0], conv_w=conv_w[0], a_log=a_log, dt_bias=dt_bias, gdn_norm_w=gdn_norm_w,
             ret_norm_w=ret_norm_w, w_out=w_out[0], norm2_w=norm2_w, w_gate=w_gate[0], w_up=w_up[0], w_down=w_down[0],
             norm_f_w=norm_f_w)
    m = dict(norm1_w=m_norm1_w, w_in=m_w_in[0], conv_w=m_conv_w[0], a_log=m_a_log, dt_bias=m_dt_bias, gdn_norm_w=m_gdn_norm_w,
             ret_norm_w=m_ret_norm_w, w_out=m_w_out[0], norm2_w=m_norm2_w, w_gate=m_w_gate[0], w_up=m_w_up[0], w_down=m_w_down[0],
             norm_f_w=m_norm_f_w)
    v = dict(norm1_w=v_norm1_w, w_in=v_w_in[0], conv_w=v_conv_w[0], a_log=v_a_log, dt_bias=v_dt_bias, gdn_norm_w=v_gdn_norm_w,
             ret_norm_w=v_ret_norm_w, w_out=v_w_out[0], norm2_w=v_norm2_w, w_gate=v_w_gate[0], w_up=v_w_up[0], w_down=v_w_down[0],
             norm_f_w=v_norm_f_w)
    big_names = ["w_in", "w_out", "w_gate", "w_up", "w_down"]

    gathered = _allgather([w[k].astype(BF16) for k in big_names] + [w["conv_w"]], "weights_allgather")
    full = dict(zip(big_names, gathered[:5]))
    conv_full = _from_slots_cols(gathered[5])
    w_in_full = _regroup_w_in(_from_slots_cols(full["w_in"]))
    w_gate_full = _from_slots_cols(full["w_gate"])
    w_up_full = _from_slots_cols(full["w_up"])
    w_out_full = full["w_out"].reshape(D, D)
    w_down_full = full["w_down"].reshape(-1, D)

    loss_tile, dx, big, small = _local_step(
        x[0], loss_target[0], w_in_full, w_out_full, w_gate_full, w_up_full, w_down_full, norm1_w, conv_full, a_log, dt_bias,
        gdn_norm_w, ret_norm_w, norm2_w, norm_f_w.reshape(1, D))
    loss = lax.psum(loss_tile[0, 0], ("x", "y", "c"))

    slots = dict(w_in=_to_slots_cols(_ungroup_w_in(big["w_in"])), w_gate=_to_slots_cols(big["w_gate"]),
                 w_up=_to_slots_cols(big["w_up"]), w_out=big["w_out"].reshape(NDEV, D // NDEV, D),
                 w_down=big["w_down"].reshape(NDEV, -1, D))
    from_sibling = _exchange_sibling([slots[k] for k in big_names], "grads_to_sibling")
    parts = [_add_sibling(slots[k], r, core, 128, "grads_add_" + k) for k, r in zip(big_names, from_sibling)]
    from_chips = _exchange_chips(parts, "grads_to_chips")
    out = {}
    for k, part, recv in zip(big_names, parts, from_chips):
        out[k] = _adamw_reduced(part, recv, chip, w[k], m[k], v[k], 128, "adamw_" + k)

    g_small = _unpack_small(_allreduce_small(_pack_small(**small), "small_grads_allreduce"))
    g_small["conv_w"] = lax.dynamic_slice_in_dim(g_small["conv_w"], me * (2 * D // NDEV), 2 * D // NDEV, axis=1)
    small_names = [k for k in WEIGHT_ORDER if k not in big_names]
    pad_conv = lambda a: jnp.pad(a, ((0, 0), (0, 2 * D - a.shape[1])))
    packs = []
    for src in (w, g_small, m, v):
        args = {k: (pad_conv(src[k]) if k == "conv_w" else src[k]) for k in small_names}
        packs.append(_pack_small(**args))
    d_pack, m_pack, v_pack = _adamw_plain(*packs[0:1], packs[1], packs[2], packs[3], name="adamw_small")
    cut_conv = lambda dct: {**dct, "conv_w": dct["conv_w"][:, :2 * D // NDEV]}
    d_small, m_small, v_small = (cut_conv(_unpack_small(p)) for p in (d_pack, m_pack, v_pack))

    def shaped(k, a):
        return a.reshape(w_shapes[k])

    w_shapes = dict(norm1_w=norm1_w.shape, w_in=w_in.shape, conv_w=conv_w.shape, a_log=a_log.shape, dt_bias=dt_bias.shape,
                    gdn_norm_w=gdn_norm_w.shape, ret_norm_w=ret_norm_w.shape, w_out=w_out.shape, norm2_w=norm2_w.shape,
                    w_gate=w_gate.shape, w_up=w_up.shape, w_down=w_down.shape, norm_f_w=norm_f_w.shape)
    grads, deltas, new_m, new_v = [], [], [], []
    for k in WEIGHT_ORDER:
        if k in big_names:
            g_, d_, m_, v_ = out[k]
        else:
            g_, d_, m_, v_ = g_small[k], d_small[k], m_small[k], v_small[k]
        grads.append(shaped(k, g_))
        deltas.append(shaped(k, d_))
        new_m.append(shaped(k, m_))
        new_v.append(shaped(k, v_))
    return (loss, dx[None], *grads, *deltas, *new_m, *new_v)
```

```python
import functools
import numpy as np
import jax
import jax.numpy as jnp
from jax import lax
from jax.experimental import pallas as pl
from jax.experimental.pallas import tpu as pltpu

F32, BF16 = jnp.float32, jnp.bfloat16
HI = lax.Precision.HIGHEST
MESH = pl.DeviceIdType.MESH
ANY = pl.BlockSpec(memory_space=pl.ANY)
VMEM_FULL = pl.BlockSpec(memory_space=pltpu.VMEM)

NDEV = 8
D = 2048
H = 8
DK = 128
DV = 256
C = 64
CPB = 4
EPS = 1e-6
ROPE_BASE = 10000.0
N_IN = 16400
O_Z, O_BA, O_RQ, O_RK, O_RV, O_RG, O_GA, O_GB, P_IN = 4096, 6144, 6400, 7424, 8448, 10496, 12544, 14592, 16640
BA_END = 6160
LR, B1, B2, EPS_ADAM, WD, STEP = 0.001, 0.9, 0.999, 1e-08, 0.01, 10
VMEM_CAP = 60 * 1024 * 1024

NN = ((1,), (0,))
NT = ((1,), (1,))
TN = ((0,), (0,))


def _params(sem=None, est=None):
    kw = {}
    if sem is not None:
        kw["dimension_semantics"] = sem
    if est is not None:
        kw["vmem_limit_bytes"] = int(min(VMEM_CAP, max(32 * 1024 * 1024, est * 5 // 4 + (4 << 20))))
    return pltpu.CompilerParams(**kw)


def _sds(shape, dt=F32):
    return jax.ShapeDtypeStruct(tuple(shape), dt)


def _row_tile(rows, limit):
    return max(t for t in range(16, min(rows, limit) + 1, 16) if rows % t == 0)


def _bdot(a, b, dims):
    return lax.dot_general(a.astype(BF16), b.astype(BF16), (dims, ((), ())), preferred_element_type=F32)


def _hdot(a, b, dims):
    return lax.dot_general(a, b, (dims, ((), ())), precision=HI, preferred_element_type=F32)


def _silu(x):
    return x * jax.nn.sigmoid(x)


def _rms(x, w):
    return x * lax.rsqrt(jnp.mean(x * x, axis=-1, keepdims=True) + EPS) * w


def _matmul(a, b, *, ta=False, tb=False, tm, tn, tk, out_dtype=F32, res=None, name):
    m = a.shape[1] if ta else a.shape[0]
    k = a.shape[0] if ta else a.shape[1]
    n = b.shape[0] if tb else b.shape[1]
    assert k == (b.shape[1] if tb else b.shape[0])
    tm, tn, tk = min(tm, m), min(tn, n), min(tk, k)
    assert m % tm == 0 and n % tn == 0 and k % tk == 0, (name, m, n, k, tm, tn, tk)
    nk = k // tk
    dims = ((0 if ta else 1,), (1 if tb else 0,))
    has_res = res is not None

    def body(*refs):
        a_ref, b_ref = refs[0], refs[1]
        r_ref = refs[2] if has_res else None
        o_ref = refs[3] if has_res else refs[2]

        def finish(acc):
            if has_res:
                acc = acc + r_ref[...].astype(F32)
            o_ref[...] = acc.astype(out_dtype)

        part = _bdot(a_ref[...], b_ref[...], dims)
        if nk == 1:
            finish(part)
        else:
            acc_ref = refs[-1]
            kk = pl.program_id(2)

            @pl.when(kk == 0)
            def _():
                acc_ref[...] = part

            @pl.when(kk > 0)
            def _():
                acc_ref[...] += part

            @pl.when(kk == nk - 1)
            def _():
                finish(acc_ref[...])

    a_spec = pl.BlockSpec((tk, tm), lambda i, j, kk: (kk, i)) if ta else pl.BlockSpec((tm, tk), lambda i, j, kk: (i, kk))
    b_spec = pl.BlockSpec((tn, tk), lambda i, j, kk: (j, kk)) if tb else pl.BlockSpec((tk, tn), lambda i, j, kk: (kk, j))
    o_spec = pl.BlockSpec((tm, tn), lambda i, j, kk: (i, j))
    in_specs = [a_spec, b_spec] + ([o_spec] if has_res else [])
    est = 2 * (tm * tk * a.dtype.itemsize + tk * tn * b.dtype.itemsize + tm * tn * jnp.dtype(out_dtype).itemsize)
    est += 2 * tm * tn * 4 * (1 if has_res else 0) + (tm * tn * 4 if nk > 1 else 0) + 2 * tm * tn * 4
    args = (a, b) + ((res,) if has_res else ())
    return pl.pallas_call(
        body, grid=(m // tm, n // tn, nk), in_specs=in_specs, out_specs=o_spec, out_shape=_sds((m, n), out_dtype),
        scratch_shapes=[pltpu.VMEM((tm, tn), F32)] if nk > 1 else [], name=name,
        compiler_params=_params(("parallel", "parallel", "arbitrary"), est))(*args)


def _rms_fwd(x, w, ts, name):
    s = x.shape[0]

    def body(x_ref, w_ref, o_ref, ot_ref):
        y = _rms(x_ref[...], w_ref[...]).astype(BF16)
        o_ref[...] = y
        ot_ref[...] = y.T

    row = pl.BlockSpec((ts, D), lambda i: (i, 0))
    return pl.pallas_call(body, grid=(s // ts,), in_specs=[row, pl.BlockSpec((1, D), lambda i: (0, 0))],
                          out_specs=[row, pl.BlockSpec((D, ts), lambda i: (0, i))],
                          out_shape=[_sds((s, D), BF16), _sds((D, s), BF16)], name=name,
                          compiler_params=_params(("parallel",)))(x, w)


def _rms_bwd(x, w, du, dres, ts, name):
    s = x.shape[0]

    def body(x_ref, w_ref, du_ref, dres_ref, dx_ref, dw_ref):
        _, vjp = jax.vjp(_rms, x_ref[...], w_ref[...])
        dx, dw = vjp(du_ref[...].astype(F32))
        dx_ref[...] = dx + dres_ref[...]

        @pl.when(pl.program_id(0) == 0)
        def _():
            dw_ref[...] = jnp.zeros_like(dw_ref)

        dw_ref[...] += dw

    row = pl.BlockSpec((ts, D), lambda i: (i, 0))
    vec = pl.BlockSpec((1, D), lambda i: (0, 0))
    return pl.pallas_call(body, grid=(s // ts,), in_specs=[row, vec, row, row], out_specs=[row, vec],
                          out_shape=[_sds((s, D)), _sds((1, D))], name=name,
                          compiler_params=_params(("arbitrary",), 12 * ts * D * 4))(x, w, du, dres)


def _conv_taps(xx, w, base, ts):
    acc = xx[base:base + ts] * w[0:1, :]
    for j in range(1, 4):
        acc = acc + xx[base + j:base + j + ts] * w[j:j + 1, :]
    return acc


def _causal_conv(prev8, cur, w, first):
    xx = jnp.concatenate([jnp.where(first, 0.0, prev8), cur], axis=0)
    return _conv_taps(xx, w, 5, cur.shape[0])


def _qk_post(c, scale):
    s = _silu(c)
    return s * lax.rsqrt(jnp.sum(s * s, axis=-1, keepdims=True) + EPS) * scale


def _conv_specs(ts, cw, col0):
    cur = pl.BlockSpec((ts, cw), lambda j, i: (i, col0 + j))
    prev = pl.BlockSpec((8, cw), lambda j, i: (jnp.maximum(i * (ts // 8) - 1, 0), col0 + j))
    wsp = pl.BlockSpec((4, cw), lambda j, i: (0, col0 + j))
    return cur, prev, wsp


def _gdn_qkv_fwd(proj, conv_w, ts):
    s = proj.shape[0]

    def qk_body(cur_ref, prev_ref, w_ref, o_ref):
        c = _causal_conv(prev_ref[...], cur_ref[...], w_ref[...], pl.program_id(1) == 0)
        scale = jnp.where(pl.program_id(0) < H, DK ** -0.5, 1.0).astype(F32)
        o_ref[...] = _qk_post(c, scale)

    cur, prev, wsp = _conv_specs(ts, DK, 0)
    qk = pl.pallas_call(qk_body, grid=(2 * H, s // ts), in_specs=[cur, prev, wsp],
                        out_specs=pl.BlockSpec((ts, DK), lambda j, i: (i, j)), out_shape=_sds((s, 2 * H * DK)),
                        name="gdn_qk_prep", compiler_params=_params(("parallel", "parallel")))(proj, proj, conv_w)

    def v_body(cur_ref, prev_ref, w_ref, o_ref):
        o_ref[...] = _silu(_causal_conv(prev_ref[...], cur_ref[...], w_ref[...], pl.program_id(1) == 0))

    cw = 512
    cur, prev, wsp = _conv_specs(ts, cw, 2 * H * DK // cw)
    v = pl.pallas_call(v_body, grid=(H * DV // cw, s // ts), in_specs=[cur, prev, wsp],
                       out_specs=pl.BlockSpec((ts, cw), lambda j, i: (i, j)), out_shape=_sds((s, H * DV)),
                       name="gdn_v_prep", compiler_params=_params(("parallel", "parallel")))(proj, proj, conv_w)
    return qk, v


def _gdn_qkv_bwd(proj, conv_w, dqk, dv, ts):
    s = proj.shape[0]
    nt = s // ts

    def qk_body(cur_ref, prev_ref, w_ref, d_ref, o_ref):
        c = _causal_conv(prev_ref[...], cur_ref[...], w_ref[...], pl.program_id(1) == 0)
        scale = jnp.where(pl.program_id(0) < H, DK ** -0.5, 1.0).astype(F32)
        _, vjp = jax.vjp(lambda cc: _qk_post(cc, scale), c)
        o_ref[...] = vjp(d_ref[...])[0]

    cur, prev, wsp = _conv_specs(ts, DK, 0)
    dc_qk = pl.pallas_call(qk_body, grid=(2 * H, nt), in_specs=[cur, prev, wsp, pl.BlockSpec((ts, DK), lambda j, i: (i, j))],
                           out_specs=pl.BlockSpec((ts, DK), lambda j, i: (i, j)), out_shape=_sds((s, 2 * H * DK)),
                           name="gdn_qk_prep_bwd", compiler_params=_params(("parallel", "parallel")))(proj, proj, conv_w, dqk)

    def v_body(cur_ref, prev_ref, w_ref, d_ref, o_ref):
        c = _causal_conv(prev_ref[...], cur_ref[...], w_ref[...], pl.program_id(1) == 0)
        _, vjp = jax.vjp(_silu, c)
        o_ref[...] = vjp(d_ref[...])[0]

    cw = 512
    cur, prev, wsp = _conv_specs(ts, cw, 2 * H * DK // cw)
    dc_v = pl.pallas_call(v_body, grid=(H * DV // cw, nt), in_specs=[cur, prev, wsp, pl.BlockSpec((ts, cw), lambda j, i: (i, j))],
                          out_specs=pl.BlockSpec((ts, cw), lambda j, i: (i, j)), out_shape=_sds((s, H * DV)),
                          name="gdn_v_prep_bwd", compiler_params=_params(("parallel", "parallel")))(proj, proj, conv_w, dv)

    def conv_bwd(dc, col0, dcol0, ncols, name):
        def body(x_ref, xprev_ref, w_ref, dc_ref, dcnext_ref, da_ref, dw_ref):
            i = pl.program_id(1)
            w = w_ref[...]
            dcur = dc_ref[...]
            dd = jnp.concatenate([dcur, jnp.where(i == nt - 1, 0.0, dcnext_ref[...])], axis=0)
            acc = dd[3:3 + ts] * w[0:1, :]
            for j in range(1, 4):
                acc = acc + dd[3 - j:3 - j + ts] * w[j:j + 1, :]
            da_ref[...] = acc.astype(BF16)
            xx = jnp.concatenate([jnp.where(i == 0, 0.0, xprev_ref[...]), x_ref[...]], axis=0)

            @pl.when(i == 0)
            def _():
                dw_ref[...] = jnp.zeros_like(dw_ref)

            for j in range(4):
                dw_ref[j:j + 1, :] += jnp.sum(dcur * xx[5 + j:5 + j + ts], axis=0, keepdims=True)

        cur, prev, wsp = _conv_specs(ts, cw, col0)
        dcur = pl.BlockSpec((ts, cw), lambda j, i: (i, j))
        dnext = pl.BlockSpec((8, cw), lambda j, i: (jnp.minimum((i + 1) * (ts // 8), s // 8 - 1), j))
        return pl.pallas_call(body, grid=(ncols // cw, nt), in_specs=[cur, prev, wsp, dcur, dnext],
                              out_specs=[pl.BlockSpec((ts, cw), lambda j, i: (i, j)), pl.BlockSpec((4, cw), lambda j, i: (0, j))],
                              out_shape=[_sds((s, ncols), BF16), _sds((4, ncols))], name=name,
                              compiler_params=_params(("parallel", "arbitrary")))(proj, proj, conv_w, dc, dc)

    da_qk, dw_qk = conv_bwd(dc_qk, 0, 0, 2 * H * DK, "conv_bwd_qk")
    da_v, dw_v = conv_bwd(dc_v, 2 * H * DK // cw, 0, H * DV, "conv_bwd_v")
    return da_qk, da_v, jnp.concatenate([dw_qk, dw_v], axis=1)


def _bg(b, a, alog, dtb):
    n = b.shape[0]
    g = -jnp.exp(alog) * jax.nn.softplus(a + dtb)
    row = lax.broadcasted_iota(jnp.int32, (n, n), 0)
    col = lax.broadcasted_iota(jnp.int32, (n, n), 1)
    shift = C.bit_length() - 1
    same = (row >> shift) == (col >> shift)
    return jax.nn.sigmoid(b), _hdot((same & (row >= col)).astype(F32), g, NN), _hdot(same.astype(F32), g, NN)


def _bg_fwd(proj, alog, dtb, ts):
    s = proj.shape[0]

    def body(ba_ref, alog_ref, dtb_ref, beta_ref, gc_ref, gl_ref):
        beta_ref[...], gc_ref[...], gl_ref[...] = _bg(ba_ref[:, 0:H], ba_ref[:, H:2 * H], alog_ref[...], dtb_ref[...])

    small = pl.BlockSpec((1, H), lambda i: (0, 0))
    out = pl.BlockSpec((ts, H), lambda i: (i, 0))
    return pl.pallas_call(body, grid=(s // ts,), in_specs=[pl.BlockSpec((ts, 256), lambda i: (i, O_BA // 256)), small, small],
                          out_specs=[out] * 3, out_shape=[_sds((s, H))] * 3, name="gdn_bg_prep",
                          compiler_params=_params(("parallel",)))(proj, alog, dtb)


def _bg_bwd(proj, alog, dtb, dbeta_h, dgc_h, dgl_h, ts):
    s = proj.shape[0]

    def body(ba_ref, alog_ref, dtb_ref, dbeta_ref, dgc_ref, dgl_ref, dba_ref, dalog_ref, ddtb_ref):
        _, vjp = jax.vjp(_bg, ba_ref[:, 0:H], ba_ref[:, H:2 * H], alog_ref[...], dtb_ref[...])
        db, da, dalog, ddtb = vjp((jnp.sum(dbeta_ref[...], axis=0), jnp.sum(dgc_ref[...], axis=0), jnp.sum(dgl_ref[...], axis=0)))
        dba_ref[...] = jnp.zeros_like(dba_ref)
        dba_ref[:, 0:H] = db.astype(BF16)
        dba_ref[:, H:2 * H] = da.astype(BF16)

        @pl.when(pl.program_id(0) == 0)
        def _():
            dalog_ref[...] = jnp.zeros_like(dalog_ref)
            ddtb_ref[...] = jnp.zeros_like(ddtb_ref)

        dalog_ref[...] += dalog
        ddtb_ref[...] += ddtb

    small = pl.BlockSpec((1, H), lambda i: (0, 0))
    per_head = pl.BlockSpec((H, ts, H), lambda i: (0, i, 0))
    return pl.pallas_call(body, grid=(s // ts,),
                          in_specs=[pl.BlockSpec((ts, 256), lambda i: (i, O_BA // 256)), small, small, per_head, per_head, per_head],
                          out_specs=[pl.BlockSpec((ts, 256), lambda i: (i, 0)), small, small],
                          out_shape=[_sds((s, 256), BF16), _sds((1, H)), _sds((1, H))], name="gdn_bg_prep_bwd",
                          compiler_params=_params(("arbitrary",)))(proj, alog, dtb, dbeta_h, dgc_h, dgl_h)


BLK = 4 * C
NNB, NTB, TNB = ((2,), (1,)), ((2,), (2,)), ((1,), (1,))


def _bdot_b(a, b, dims):
    return lax.dot_general(a.astype(BF16), b.astype(BF16), (dims, ((0,), (0,))), preferred_element_type=F32)


@jax.custom_vjp
def _inv_unit_lower(a):
    n = a.shape[-1]
    row = lax.broadcasted_iota(jnp.int32, (n, n), 0)
    col = lax.broadcasted_iota(jnp.int32, (n, n), 1)
    x = jnp.where(row == col, 1.0, 0.0).astype(F32) - a
    p = _bdot_b(a, a, NNB)
    power = 2
    while True:
        x = x + _bdot_b(x, p, NNB)
        power *= 2
        if power >= C:
            return x
        p = _bdot_b(p, p, NNB)


def _inv_fwd(a):
    t = _inv_unit_lower(a)
    return t, t


def _inv_bwd(t, dt):
    return (-_bdot_b(_bdot_b(t, dt, TNB), t, NTB),)


_inv_unit_lower.defvjp(_inv_fwd, _inv_bwd)


def _gdn_prep(q, k, v, bfull, gcfull, glfull, hmask):
    nb, n = q.shape[0], q.shape[1]
    beta = jnp.sum(bfull * hmask, axis=-1, keepdims=True)
    gc = jnp.sum(gcfull * hmask, axis=-1, keepdims=True)
    gl = jnp.sum(glfull * hmask, axis=-1, keepdims=True)
    row = lax.broadcasted_iota(jnp.int32, (n, n), 0)
    col = lax.broadcasted_iota(jnp.int32, (n, n), 1)
    shift = C.bit_length() - 1
    same = (row >> shift) == (col >> shift)
    incl, strict = same & (row >= col), same & (row > col)
    g_i = gc * jnp.ones((1, 1, n), F32)
    decay = jnp.exp(jnp.where(incl, g_i - jnp.swapaxes(g_i, 1, 2), -jnp.inf))
    kb = k * beta
    a = jnp.where(strict, _bdot_b(kb, k, NTB) * decay, 0.0)
    tinv = _inv_unit_lower(a)
    u = _bdot_b(tinv, v * beta, NNB)
    w = _bdot_b(tinv, kb * jnp.exp(gc), NNB)
    attn = _bdot_b(q, k, NTB) * decay
    fold = ((lax.broadcasted_iota(jnp.int32, (n, C), 0) & (C - 1)) == lax.broadcasted_iota(jnp.int32, (n, C), 1)).astype(F32)
    attn_c = _bdot(attn.reshape(nb * n, n), fold, NN).reshape(nb, n, C)
    return u, w, attn_c, q * jnp.exp(gc), k * jnp.exp(gl - gc), jnp.exp(gl)


def _gdn_step(u, w, attn, qg, kd, egl, state):
    v_new = u - _bdot(w, state, NN)
    o = _bdot(qg, state, NN) + _bdot(attn, v_new, NN)
    return o, state * egl + _bdot(kd, v_new, TN)


def _head_mask(h):
    return (lax.broadcasted_iota(jnp.int32, (1, H), 1) == h).astype(F32)


PREP_BLOCKS = 2


def _gdn_prep_specs(r):
    small = pl.BlockSpec((r, H), lambda h, c: (c, 0))
    return [pl.BlockSpec((r, DK), lambda h, c: (c, h)), pl.BlockSpec((r, DK), lambda h, c: (c, H + h)),
            pl.BlockSpec((r, DV), lambda h, c: (c, h)), small, small, small]


def _blocked(ref):
    x = ref[...]
    return x.reshape(PREP_BLOCKS, BLK, x.shape[-1])


def _gdn_inter_specs(r):
    col = pl.BlockSpec((r, DK), lambda h, c: (c, h))
    return [pl.BlockSpec((r, DV), lambda h, c: (c, h)), col, pl.BlockSpec((1, r, C), lambda h, c: (h, c, 0)), col, col,
            pl.BlockSpec((1, r // C, 8, 128), lambda h, c: (h, c, 0, 0))]


def _gdn_prep_fwd(qk, v, beta, gc, gl):
    s = qk.shape[0]
    r = PREP_BLOCKS * BLK

    def body(q_ref, k_ref, v_ref, b_ref, gc_ref, gl_ref, u_ref, w_ref, attn_ref, qg_ref, kd_ref, egl_ref):
        u, w, attn, qg, kd, egl = _gdn_prep(_blocked(q_ref), _blocked(k_ref), _blocked(v_ref), _blocked(b_ref), _blocked(gc_ref),
                                            _blocked(gl_ref), _head_mask(pl.program_id(0)))
        u_ref[...] = u.reshape(r, DV)
        w_ref[...] = w.reshape(r, DK).astype(BF16)
        attn_ref[0] = attn.reshape(r, C).astype(BF16)
        qg_ref[...] = qg.reshape(r, DK).astype(BF16)
        kd_ref[...] = kd.reshape(r, DK).astype(BF16)
        egl = egl.reshape(r, 1)
        for j in range(r // C):
            egl_ref[0, j] = egl[j * C:j * C + 1, :] * jnp.ones((8, 128), F32)

    out_shape = [_sds((s, H * DV)), _sds((s, H * DK), BF16), _sds((H, s, C), BF16), _sds((s, H * DK), BF16),
                 _sds((s, H * DK), BF16), _sds((H, s // C, 8, 128))]
    return pl.pallas_call(body, grid=(H, s // r), in_specs=_gdn_prep_specs(r), out_specs=_gdn_inter_specs(r), out_shape=out_shape,
                          name="gdn_prep_fwd", compiler_params=_params(("parallel", "parallel")))(qk, qk, v, beta, gc, gl)


def _gdn_prep_bwd(qk, v, beta, gc, gl, du, dw, dattn, dqg, dkd, degl):
    s = qk.shape[0]
    r = PREP_BLOCKS * BLK

    def body(q_ref, k_ref, v_ref, b_ref, gc_ref, gl_ref, du_ref, dw_ref, dattn_ref, dqg_ref, dkd_ref, degl_ref,
             dq_ref, dk_ref, dv_ref, db_ref, dgc_ref, dgl_ref):
        hmask = _head_mask(pl.program_id(0))
        _, vjp = jax.vjp(lambda q, k, v, b, gc, gl: _gdn_prep(q, k, v, b, gc, gl, hmask), _blocked(q_ref), _blocked(k_ref),
                         _blocked(v_ref), _blocked(b_ref), _blocked(gc_ref), _blocked(gl_ref))
        rowid = lax.broadcasted_iota(jnp.int32, (r, 1), 0)
        degl = jnp.zeros((r, 1), F32)
        for j in range(r // C):
            degl = jnp.where(rowid == j * C, degl_ref[0, j, 0:1, 0:1], degl)
        dq, dk, dv, db, dgc, dgl = vjp((_blocked(du_ref), _blocked(dw_ref), _blocked(dattn_ref.at[0]), _blocked(dqg_ref),
                                        _blocked(dkd_ref), degl.reshape(PREP_BLOCKS, BLK, 1)))
        dq_ref[...] = dq.reshape(r, DK)
        dk_ref[...] = dk.reshape(r, DK)
        dv_ref[...] = dv.reshape(r, DV)
        db_ref[0] = db.reshape(r, H)
        dgc_ref[0] = dgc.reshape(r, H)
        dgl_ref[0] = dgl.reshape(r, H)

    col = pl.BlockSpec((r, DK), lambda h, c: (c, h))
    piece = pl.BlockSpec((1, r, H), lambda h, c: (h, c, 0))
    dq, dk, dv, db, dgc, dgl = pl.pallas_call(
        body, grid=(H, s // r), in_specs=_gdn_prep_specs(r) + _gdn_inter_specs(r),
        out_specs=[col, col, pl.BlockSpec((r, DV), lambda h, c: (c, h)), piece, piece, piece],
        out_shape=[_sds((s, H * DK)), _sds((s, H * DK)), _sds((s, H * DV))] + [_sds((H, s, H))] * 3,
        name="gdn_prep_bwd", compiler_params=_params(("parallel", "parallel"), 24 << 20))(
            qk, qk, v, beta, gc, gl, du, dw, dattn, dqg, dkd, degl)
    return jnp.concatenate([dq, dk], axis=1), dv, db, dgc, dgl


def _gdn_scan_specs(r, order):
    wide = pl.BlockSpec((r, H * DK), lambda c: (order(c), 0))
    return [pl.BlockSpec((r, H * DV), lambda c: (order(c), 0)), wide, pl.BlockSpec((H, r, C), lambda c: (0, order(c), 0)), wide, wide,
            pl.BlockSpec((H, r // C, 8, 128), lambda c: (0, order(c), 0, 0))]


def _gdn_scan_fwd(u, w, attn, qg, kd, egl):
    s = u.shape[0]
    r = CPB * C
    nb = s // r

    def body(u_ref, w_ref, attn_ref, qg_ref, kd_ref, egl_ref, o_ref, st_ref, state_ref):
        @pl.when(pl.program_id(0) == 0)
        def _():
            state_ref[...] = jnp.zeros_like(state_ref)

        for h in range(H):
            kc, vc = slice(h * DK, (h + 1) * DK), slice(h * DV, (h + 1) * DV)
            state = state_ref[h]
            for i in range(CPB):
                rs = slice(i * C, (i + 1) * C)
                st_ref[h, i] = state
                o, state = _gdn_step(u_ref[rs, vc], w_ref[rs, kc], attn_ref[h, rs, :], qg_ref[rs, kc], kd_ref[rs, kc],
                                     egl_ref[h, i, 0:1, 0:1], state)
                o_ref[rs, vc] = o
            state_ref[h] = state

    out_specs = [pl.BlockSpec((r, H * DV), lambda c: (c, 0)), pl.BlockSpec((H, CPB, DK, DV), lambda c: (0, c, 0, 0))]
    return pl.pallas_call(body, grid=(nb,), in_specs=_gdn_scan_specs(r, lambda c: c), out_specs=out_specs,
                          out_shape=[_sds((s, H * DV)), _sds((H, s // C, DK, DV))],
                          scratch_shapes=[pltpu.VMEM((H, DK, DV), F32)], name="gdn_scan_fwd",
                          compiler_params=_params(("arbitrary",), 24 << 20))(u, w, attn, qg, kd, egl)


def _gdn_scan_bwd(u, w, attn, qg, kd, egl, states, do):
    s = u.shape[0]
    r = CPB * C
    nb = s // r

    def body(u_ref, w_ref, attn_ref, qg_ref, kd_ref, egl_ref, st_ref, do_ref,
             du_ref, dw_ref, dattn_ref, dqg_ref, dkd_ref, degl_ref, dstate_ref):
        @pl.when(pl.program_id(0) == 0)
        def _():
            dstate_ref[...] = jnp.zeros_like(dstate_ref)

        for h in range(H):
            kc, vc = slice(h * DK, (h + 1) * DK), slice(h * DV, (h + 1) * DV)
            dstate = dstate_ref[h]
            for i in reversed(range(CPB)):
                rs = slice(i * C, (i + 1) * C)
                _, vjp = jax.vjp(_gdn_step, u_ref[rs, vc], w_ref[rs, kc].astype(F32), attn_ref[h, rs, :].astype(F32),
                                 qg_ref[rs, kc].astype(F32), kd_ref[rs, kc].astype(F32), egl_ref[h, i, 0:1, 0:1], st_ref[h, i])
                du, dw, dattn, dqg, dkd, degl, dstate = vjp((do_ref[rs, vc], dstate))
                du_ref[rs, vc] = du
                dw_ref[rs, kc] = dw
                dattn_ref[h, rs, :] = dattn
                dqg_ref[rs, kc] = dqg
                dkd_ref[rs, kc] = dkd
                degl_ref[h, i] = degl * jnp.ones((8, 128), F32)
            dstate_ref[h] = dstate

    rev = lambda c: nb - 1 - c
    in_specs = _gdn_scan_specs(r, rev) + [pl.BlockSpec((H, CPB, DK, DV), lambda c: (0, rev(c), 0, 0)),
                                          pl.BlockSpec((r, H * DV), lambda c: (rev(c), 0))]
    return pl.pallas_call(
        body, grid=(nb,), in_specs=in_specs, out_specs=_gdn_scan_specs(r, rev),
        out_shape=[_sds((s, H * DV)), _sds((s, H * DK)), _sds((H, s, C)), _sds((s, H * DK)), _sds((s, H * DK)),
                   _sds((H, s // C, 8, 128))],
        scratch_shapes=[pltpu.VMEM((H, DK, DV), F32)], name="gdn_scan_bwd",
        compiler_params=_params(("arbitrary",), 40 << 20))(u, w, attn, qg, kd, egl, states, do)


def _rot(x, cs, sn):
    return x * cs + pltpu.roll(x, DK // 2, 1) * sn


def _rot_t(d, cs, sn):
    return d * cs - pltpu.roll(d, DK // 2, 1) * sn


def _ret_chunk(q, k, v, state, lg):
    n = q.shape[0]
    row = lax.broadcasted_iota(jnp.int32, (n, n), 0)
    col = lax.broadcasted_iota(jnp.int32, (n, n), 1)
    dist = (row - col).astype(F32)
    dmat = jnp.exp(jnp.where(dist >= 0, dist * lg, -jnp.inf))
    scores = _bdot(q, k, NT) * dmat
    pos = lax.broadcasted_iota(jnp.int32, (n, 1), 0).astype(F32)
    xi = jnp.exp((pos + 1.0) * lg)
    zeta = jnp.exp((n - 1.0 - pos) * lg)
    o = _bdot(scores, v, NN) + _bdot(q, state, NN) * xi
    new_state = state * jnp.exp(n * lg) + _bdot(k * zeta, v, TN)
    return o, new_state


def _ret_specs(r, order):
    return [pl.BlockSpec((r, DK), lambda h, c: (order(c), O_RQ // DK + h)), pl.BlockSpec((r, DK), lambda h, c: (order(c), O_RK // DK + h)),
            pl.BlockSpec((r, DV), lambda h, c: (order(c), O_RV // DV + h)), pl.BlockSpec((r, DK), lambda h, c: (order(c), 0)),
            pl.BlockSpec((r, DK), lambda h, c: (order(c), 0)), pl.BlockSpec((1, H), lambda h, c: (0, 0))]


def _ret_scan_fwd(proj, cs, sn, lgtab):
    s = proj.shape[0]
    r = CPB * C
    nb = s // r

    def body(q_ref, k_ref, v_ref, cs_ref, sn_ref, lg_ref, o_ref, st_ref, state_ref):
        @pl.when(pl.program_id(1) == 0)
        def _():
            state_ref[...] = jnp.zeros_like(state_ref)

        lg = jnp.sum(lg_ref[...] * _head_mask(pl.program_id(0)), axis=1, keepdims=True)
        state = state_ref[...]
        for i in range(CPB):
            rs = slice(i * C, (i + 1) * C)
            st_ref[0, i] = state
            q = _rot(q_ref[rs, :], cs_ref[rs, :], sn_ref[rs, :])
            k = _rot(k_ref[rs, :], cs_ref[rs, :], sn_ref[rs, :]) * DK ** -0.5
            o, state = _ret_chunk(q, k, v_ref[rs, :], state, lg)
            o_ref[rs, :] = o
        state_ref[...] = state

    out_specs = [pl.BlockSpec((r, DV), lambda h, c: (c, h)), pl.BlockSpec((1, CPB, DK, DV), lambda h, c: (h, c, 0, 0))]
    return pl.pallas_call(body, grid=(H, nb), in_specs=_ret_specs(r, lambda c: c), out_specs=out_specs,
                          out_shape=[_sds((s, H * DV)), _sds((H, s // C, DK, DV))],
                          scratch_shapes=[pltpu.VMEM((DK, DV), F32)], name="ret_scan_fwd",
                          compiler_params=_params(("parallel", "arbitrary")))(proj, proj, proj, cs, sn, lgtab)


def _ret_scan_bwd(proj, cs, sn, lgtab, states, do):
    s = proj.shape[0]
    r = CPB * C
    nb = s // r

    def body(q_ref, k_ref, v_ref, cs_ref, sn_ref, lg_ref, st_ref, do_ref, dq_ref, dk_ref, dv_ref, dstate_ref):
        @pl.when(pl.program_id(1) == 0)
        def _():
            dstate_ref[...] = jnp.zeros_like(dstate_ref)

        lg = jnp.sum(lg_ref[...] * _head_mask(pl.program_id(0)), axis=1, keepdims=True)
        dstate = dstate_ref[...]
        for i in reversed(range(CPB)):
            rs = slice(i * C, (i + 1) * C)
            cs_, sn_ = cs_ref[rs, :], sn_ref[rs, :]
            q = _rot(q_ref[rs, :], cs_, sn_)
            k = _rot(k_ref[rs, :], cs_, sn_) * DK ** -0.5
            _, vjp = jax.vjp(lambda q, k, v, st: _ret_chunk(q, k, v, st, lg), q, k, v_ref[rs, :], st_ref[0, i])
            dq, dk, dv, dstate = vjp((do_ref[rs, :], dstate))
            dq_ref[rs, :] = _rot_t(dq, cs_, sn_).astype(BF16)
            dk_ref[rs, :] = _rot_t(dk * DK ** -0.5, cs_, sn_).astype(BF16)
            dv_ref[rs, :] = dv.astype(BF16)
        dstate_ref[...] = dstate

    rev = lambda c: nb - 1 - c
    in_specs = _ret_specs(r, rev) + [pl.BlockSpec((1, CPB, DK, DV), lambda h, c: (h, rev(c), 0, 0)),
                                     pl.BlockSpec((r, DV), lambda h, c: (rev(c), h))]
    out_specs = [pl.BlockSpec((r, DK), lambda h, c: (rev(c), h)), pl.BlockSpec((r, DK), lambda h, c: (rev(c), h)),
                 pl.BlockSpec((r, DV), lambda h, c: (rev(c), h))]
    dq, dk, dv = pl.pallas_call(
        body, grid=(H, nb), in_specs=in_specs, out_specs=out_specs,
        out_shape=[_sds((s, H * DK), BF16), _sds((s, H * DK), BF16), _sds((s, H * DV), BF16)],
        scratch_shapes=[pltpu.VMEM((DK, DV), F32)], name="ret_scan_bwd",
        compiler_params=_params(("parallel", "arbitrary")))(proj, proj, proj, cs, sn, lgtab, states, do)
    return dq, dk, dv


def _merge(oa, z, ob, rg, ga, gb, wa, wb):
    ya = oa * lax.rsqrt(jnp.mean(oa * oa, axis=-1, keepdims=True) + EPS) * wa * _silu(z)
    mu = jnp.mean(ob, axis=-1, keepdims=True)
    var = jnp.mean(jnp.square(ob - mu), axis=-1, keepdims=True)
    yb = (ob - mu) * lax.rsqrt(var + EPS) * wb * _silu(rg)
    return jax.nn.sigmoid(ga) * ya + jax.nn.sigmoid(gb) * yb


def _merge_specs(ts):
    blk = lambda off: pl.BlockSpec((ts, DV), lambda h, i: (i, off // DV + h))
    return [blk(0), blk(O_Z), blk(0), blk(O_RG), blk(O_GA), blk(O_GB),
            pl.BlockSpec((1, DV), lambda h, i: (0, 0)), pl.BlockSpec((1, DV), lambda h, i: (0, h))]


def _merge_fwd(oa, ob, proj, wa, wb, ts):
    s = oa.shape[0]

    def body(oa_ref, z_ref, ob_ref, rg_ref, ga_ref, gb_ref, wa_ref, wb_ref, o_ref, ot_ref):
        y = _merge(oa_ref[...], z_ref[...], ob_ref[...], rg_ref[...], ga_ref[...], gb_ref[...],
                   wa_ref[...], wb_ref[...]).astype(BF16)
        o_ref[...] = y
        ot_ref[...] = y.T

    return pl.pallas_call(body, grid=(H, s // ts), in_specs=_merge_specs(ts),
                          out_specs=[pl.BlockSpec((ts, DV), lambda h, i: (i, h)), pl.BlockSpec((DV, ts), lambda h, i: (h, i))],
                          out_shape=[_sds((s, H * DV), BF16), _sds((H * DV, s), BF16)], name="merge_fwd",
                          compiler_params=_params(("parallel", "parallel")))(oa, proj, ob, proj, proj, proj, wa, wb)


def _merge_bwd(oa, ob, proj, wa, wb, dmixed, ts):
    s = oa.shape[0]

    def body(oa_ref, z_ref, ob_ref, rg_ref, ga_ref, gb_ref, wa_ref, wb_ref, dm_ref,
             doa_ref, dob_ref, dz_ref, drg_ref, dga_ref, dgb_ref, dwa_ref, dwb_ref):
        _, vjp = jax.vjp(_merge, oa_ref[...], z_ref[...], ob_ref[...], rg_ref[...], ga_ref[...], gb_ref[...],
                         wa_ref[...], wb_ref[...])
        doa, dz, dob, drg, dga, dgb, dwa, dwb = vjp(dm_ref[...].astype(F32))
        doa_ref[...] = doa
        dob_ref[...] = dob
        dz_ref[...] = dz.astype(BF16)
        drg_ref[...] = drg.astype(BF16)
        dga_ref[...] = dga.astype(BF16)
        dgb_ref[...] = dgb.astype(BF16)
        first_tile = pl.program_id(1) == 0

        @pl.when(first_tile & (pl.program_id(0) == 0))
        def _():
            dwa_ref[...] = jnp.zeros_like(dwa_ref)

        @pl.when(first_tile)
        def _():
            dwb_ref[...] = jnp.zeros_like(dwb_ref)

        dwa_ref[...] += dwa
        dwb_ref[...] += dwb

    blk = pl.BlockSpec((ts, DV), lambda h, i: (i, h))
    out_specs = [blk] * 6 + [pl.BlockSpec((1, DV), lambda h, i: (0, 0)), pl.BlockSpec((1, DV), lambda h, i: (0, h))]
    out_shape = [_sds((s, H * DV)), _sds((s, H * DV))] + [_sds((s, H * DV), BF16)] * 4 + [_sds((1, DV)), _sds((1, H * DV))]
    return pl.pallas_call(body, grid=(H, s // ts), in_specs=_merge_specs(ts) + [blk], out_specs=out_specs, out_shape=out_shape,
                          name="merge_bwd", compiler_params=_params(("arbitrary", "arbitrary"), 40 * ts * DV * 4))(
                              oa, proj, ob, proj, proj, proj, wa, wb, dmixed)


def _act(hg, hu):
    return _silu(hg) * hu


def _act_fwd(hg, hu, ts, tc):
    s, f = hg.shape
    blk = pl.BlockSpec((ts, tc), lambda i, j: (i, j))

    def body(hg_ref, hu_ref, o_ref, ot_ref):
        y = _act(hg_ref[...], hu_ref[...]).astype(BF16)
        o_ref[...] = y
        ot_ref[...] = y.T

    return pl.pallas_call(body, grid=(s // ts, f // tc), in_specs=[blk, blk],
                          out_specs=[blk, pl.BlockSpec((tc, ts), lambda i, j: (j, i))],
                          out_shape=[_sds((s, f), BF16), _sds((f, s), BF16)],
                          name="swiglu_fwd", compiler_params=_params(("parallel", "parallel")))(hg, hu)


def _act_bwd(hg, hu, dact, ts, tc):
    s, f = hg.shape
    blk = pl.BlockSpec((ts, tc), lambda i, j: (i, j))

    def body(hg_ref, hu_ref, d_ref, dhg_ref, dhu_ref):
        _, vjp = jax.vjp(_act, hg_ref[...], hu_ref[...])
        dhg, dhu = vjp(d_ref[...].astype(F32))
        dhg_ref[...] = dhg.astype(BF16)
        dhu_ref[...] = dhu.astype(BF16)

    return pl.pallas_call(body, grid=(s // ts, f // tc), in_specs=[blk, blk, blk], out_specs=[blk, blk],
                          out_shape=[_sds((s, f), BF16), _sds((s, f), BF16)], name="swiglu_bwd",
                          compiler_params=_params(("parallel", "parallel")))(hg, hu, dact)


def _loss_rows(h2, wf, tgt):
    err = _rms(h2, wf) - tgt
    return 0.5 * jnp.sum(jnp.mean(err * err, axis=-1, keepdims=True), keepdims=True)


def _loss_fwd_bwd(h2, wf, tgt, ts):
    s = h2.shape[0]

    def body(h_ref, w_ref, t_ref, loss_ref, dh_ref, dw_ref):
        loss, vjp = jax.vjp(_loss_rows, h_ref[...], w_ref[...], t_ref[...])
        dh, dw, _ = vjp(jnp.ones((1, 1), F32))
        dh_ref[...] = dh

        @pl.when(pl.program_id(0) == 0)
        def _():
            loss_ref[...] = jnp.zeros_like(loss_ref)
            dw_ref[...] = jnp.zeros_like(dw_ref)

        loss_ref[...] += loss
        dw_ref[...] += dw

    row = pl.BlockSpec((ts, D), lambda i: (i, 0))
    vec = pl.BlockSpec((1, D), lambda i: (0, 0))
    tile = pl.BlockSpec((8, 128), lambda i: (0, 0))
    return pl.pallas_call(body, grid=(s // ts,), in_specs=[row, vec, row], out_specs=[tile, row, vec],
                          out_shape=[_sds((8, 128)), _sds((s, D)), _sds((1, D))], name="final_norm_loss",
                          compiler_params=_params(("arbitrary",), 12 * ts * D * 4))(h2, wf, tgt)


def _rope_tables(s):
    inv = ROPE_BASE ** (-jnp.arange(0, DK, 2, dtype=F32) / DK)
    ang = jnp.arange(s, dtype=F32)[:, None] * inv[None, :]
    cos, sin = jnp.cos(ang), jnp.sin(ang)
    return jnp.concatenate([cos, cos], axis=1), jnp.concatenate([-sin, sin], axis=1)


def _local_step(x, tgt, w_in, w_out, w_gate, w_up, w_down, norm1_w, conv_w, a_log, dt_bias, gdn_norm_w, ret_norm_w, norm2_w, norm_f_w):
    s = x.shape[0]
    ts = min(512, s)
    cs, sn = _rope_tables(s)
    lgtab = jnp.log1p(-jnp.exp2(-5.0 - jnp.arange(H, dtype=F32))).reshape(1, H)

    u, u_t = _rms_fwd(x, norm1_w, ts, "norm1_fwd")
    proj = _matmul(u, w_in, tm=1024, tn=1280, tk=D, name="in_proj")
    qk, va = _gdn_qkv_fwd(proj, conv_w, ts)
    beta, gc, gl = _bg_fwd(proj, a_log, dt_bias, ts)
    inter = _gdn_prep_fwd(qk, va, beta, gc, gl)
    oa, st_a = _gdn_scan_fwd(*inter)
    ob, st_b = _ret_scan_fwd(proj, cs, sn, lgtab)
    mixed, mixed_t = _merge_fwd(oa, ob, proj, gdn_norm_w, ret_norm_w, ts)
    h1 = _matmul(mixed, w_out, tm=1024, tn=1024, tk=D, res=x, name="out_proj")
    hn, hn_t = _rms_fwd(h1, norm2_w, ts, "norm2_fwd")
    hg = _matmul(hn, w_gate, tm=1024, tn=1408, tk=D, name="ffn_gate")
    hu = _matmul(hn, w_up, tm=1024, tn=1408, tk=D, name="ffn_up")
    act, act_t = _act_fwd(hg, hu, min(256, s), 1408)
    h2 = _matmul(act, w_down, tm=1024, tn=1024, tk=1408, res=h1, name="ffn_down")
    loss, dh2, d_norm_f = _loss_fwd_bwd(h2, norm_f_w, tgt, ts)

    dact = _matmul(dh2, w_down, tb=True, tm=1024, tn=1408, tk=D, out_dtype=BF16, name="ffn_down_dx")
    g_down = _matmul(act_t, dh2, tm=1408, tn=1024, tk=512, out_dtype=BF16, name="ffn_down_dw")
    dhg, dhu = _act_bwd(hg, hu, dact, min(256, s), 1408)
    g_gate = _matmul(hn_t, dhg, tm=1024, tn=1408, tk=512, out_dtype=BF16, name="ffn_gate_dw")
    g_up = _matmul(hn_t, dhu, tm=1024, tn=1408, tk=512, out_dtype=BF16, name="ffn_up_dw")
    dhn = _matmul(dhg, w_gate, tb=True, tm=1024, tn=1024, tk=1408, name="ffn_gate_dx")
    dhn = _matmul(dhu, w_up, tb=True, tm=1024, tn=1024, tk=1408, res=dhn, name="ffn_up_dx")
    dh1, d_norm2 = _rms_bwd(h1, norm2_w, dhn, dh2, ts, "norm2_bwd")

    dmixed = _matmul(dh1, w_out, tb=True, tm=1024, tn=1024, tk=D, out_dtype=BF16, name="out_proj_dx")
    g_out = _matmul(mixed_t, dh1, tm=1024, tn=1024, tk=512, out_dtype=BF16, name="out_proj_dw")
    doa, dob, dz, drg, dga, dgb, d_gdn_norm, d_ret_norm = _merge_bwd(oa, ob, proj, gdn_norm_w, ret_norm_w, dmixed, ts)

    drq, drk, drv = _ret_scan_bwd(proj, cs, sn, lgtab, st_b, dob)
    d_inter = _gdn_scan_bwd(*inter, st_a, doa)
    dqk, dva, dbeta_h, dgc_h, dgl_h = _gdn_prep_bwd(qk, va, beta, gc, gl, *d_inter)
    da_qk, da_v, d_conv = _gdn_qkv_bwd(proj, conv_w, dqk, dva, ts)
    dba, d_a_log, d_dt_bias = _bg_bwd(proj, a_log, dt_bias, dbeta_h, dgc_h, dgl_h, ts)

    dproj = jnp.concatenate([da_qk, da_v, dz, dba, drq, drk, drv, drg, dga, dgb], axis=1)
    du = _matmul(dproj, w_in, tb=True, tm=1024, tn=1024, tk=1280, name="in_proj_dx")
    g_in = _matmul(u_t, dproj, tm=1024, tn=1280, tk=512, out_dtype=BF16, name="in_proj_dw")
    dx, d_norm1 = _rms_bwd(x, norm1_w, du, dh1, ts, "norm1_bwd")

    big = dict(w_in=g_in, w_out=g_out, w_gate=g_gate, w_up=g_up, w_down=g_down)
    small = dict(norm1_w=d_norm1, conv_w=d_conv, a_log=d_a_log, dt_bias=d_dt_bias, gdn_norm_w=d_gdn_norm,
                 ret_norm_w=d_ret_norm, norm2_w=d_norm2, norm_f_w=d_norm_f)
    return loss, dx, big, small


def _coords():
    return lax.axis_index("x"), lax.axis_index("y"), lax.axis_index("c")


def _allgather(shards, name):
    n = len(shards)

    def body(*refs):
        ins, outs = refs[:n], refs[n:2 * n]
        send_sems, recv_sems, local_sems = refs[2 * n:]
        x, y, c = _coords()
        me, sibling = (x, y, c), (x, y, 1 - c)
        chips = [(1 - x, y), (x, 1 - y), (1 - x, 1 - y)]

        def copy(a, k, block, to, src=None):
            px, py, pc = block
            dst = outs[a].at[4 * px + 2 * py + pc]
            return pltpu.make_async_remote_copy(src_ref=dst if src is None else src, dst_ref=dst, send_sem=send_sems.at[a, k],
                                                recv_sem=recv_sems.at[a, k], device_id=to, device_id_type=MESH)

        mine = [pltpu.make_async_copy(ins[a], outs[a].at[4 * x + 2 * y + c], local_sems.at[a]) for a in range(n)]
        for cp in mine:
            cp.start()
        first = []
        for a in range(n):
            first.append(copy(a, 0, me, sibling, src=ins[a]))
            first += [copy(a, 1 + j, me, (*chip, c), src=ins[a]) for j, chip in enumerate(chips)]
        for cp in first:
            cp.start()
        passed = []
        for j, chip in enumerate(chips):
            for a in range(n):
                copy(a, 1 + j, (*chip, c), me).wait_recv()
                fwd = copy(a, 4 + j, (*chip, c), sibling)
                fwd.start()
                passed.append(fwd)
        for a in range(n):
            copy(a, 0, sibling, me).wait_recv()
            for j, chip in enumerate(chips):
                copy(a, 4 + j, (*chip, 1 - c), me).wait_recv()
        for cp in first + passed:
            cp.wait_send()
        for cp in mine:
            cp.wait()

    return pl.pallas_call(
        body, in_specs=[ANY] * n, out_specs=[ANY] * n, out_shape=[_sds((NDEV,) + a.shape, a.dtype) for a in shards],
        scratch_shapes=[pltpu.SemaphoreType.DMA((n, 7)), pltpu.SemaphoreType.DMA((n, 7)), pltpu.SemaphoreType.DMA((n,))],
        name=name)(*shards)


def _exchange_sibling(slots, name):
    n = len(slots)

    def body(*refs):
        ins, outs = refs[:n], refs[n:2 * n]
        send_sems, recv_sems = refs[2 * n:]
        x, y, c = _coords()
        copies = []
        for a in range(n):
            for j in range(4):
                copies.append(pltpu.make_async_remote_copy(
                    src_ref=ins[a].at[2 * j + (1 - c)], dst_ref=outs[a].at[j], send_sem=send_sems.at[a, j],
                    recv_sem=recv_sems.at[a, j], device_id=(x, y, 1 - c), device_id_type=MESH))
        for cp in copies:
            cp.start()
        for cp in copies:
            cp.wait()

    return pl.pallas_call(
        body, in_specs=[ANY] * n, out_specs=[ANY] * n, out_shape=[_sds((4,) + a.shape[1:], a.dtype) for a in slots],
        scratch_shapes=[pltpu.SemaphoreType.DMA((n, 4)), pltpu.SemaphoreType.DMA((n, 4))], name=name)(*slots)


def _exchange_chips(parts, name):
    n = len(parts)

    def body(*refs):
        ins, outs = refs[:n], refs[n:2 * n]
        send_sems, recv_sems = refs[2 * n:]
        x, y, c = _coords()
        chips = [(1 - x, y), (x, 1 - y), (1 - x, 1 - y)]
        copies = []
        for a in range(n):
            for k, (px, py) in enumerate(chips):
                copies.append(pltpu.make_async_remote_copy(
                    src_ref=ins[a].at[2 * px + py], dst_ref=outs[a].at[k], send_sem=send_sems.at[a, k],
                    recv_sem=recv_sems.at[a, k], device_id=(px, py, c), device_id_type=MESH))
        for cp in copies:
            cp.start()
        for cp in copies:
            cp.wait()

    return pl.pallas_call(
        body, in_specs=[ANY] * n, out_specs=[ANY] * n, out_shape=[_sds((3,) + a.shape[1:], a.dtype) for a in parts],
        scratch_shapes=[pltpu.SemaphoreType.DMA((n, 3)), pltpu.SemaphoreType.DMA((n, 3))], name=name)(*parts)


def _allreduce_small(pack, name):
    rows, cols = pack.shape

    def body(in_ref, out_ref, buf_ref, send_sems, recv_sems):
        x, y, c = _coords()
        mine = 4 * x + 2 * y + c
        buf_ref[mine] = in_ref[...]
        copies = []
        for r in range(1, NDEV):
            peer = (x ^ (r >> 2), y ^ ((r >> 1) & 1), c ^ (r & 1))
            copies.append(pltpu.make_async_remote_copy(
                src_ref=in_ref, dst_ref=buf_ref.at[mine], send_sem=send_sems.at[r - 1], recv_sem=recv_sems.at[r - 1],
                device_id=peer, device_id_type=MESH))
        for cp in copies:
            cp.start()
        for r in range(1, NDEV):
            pltpu.make_async_remote_copy(
                src_ref=in_ref, dst_ref=buf_ref.at[mine ^ r], send_sem=send_sems.at[r - 1], recv_sem=recv_sems.at[r - 1],
                device_id=(x, y, c), device_id_type=MESH).wait_recv()
        for cp in copies:
            cp.wait_send()
        acc = buf_ref[0]
        for d in range(1, NDEV):
            acc = acc + buf_ref[d]
        out_ref[...] = acc

    return pl.pallas_call(
        body, in_specs=[VMEM_FULL], out_specs=VMEM_FULL, out_shape=_sds((rows, cols)),
        scratch_shapes=[pltpu.VMEM((NDEV, rows, cols), F32), pltpu.SemaphoreType.DMA((NDEV - 1,)), pltpu.SemaphoreType.DMA((NDEV - 1,))],
        name=name)(pack)


def _add_sibling(slots, recv, core, tr, name):
    _, rows, cols = slots.shape
    tr = _row_tile(rows, tr)

    def body(c_ref, a_ref, b_ref, o_ref):
        o_ref[...] = (a_ref[...].astype(F32) + b_ref[...].astype(F32)).astype(BF16)

    gs = pltpu.PrefetchScalarGridSpec(
        num_scalar_prefetch=1, grid=(4, rows // tr),
        in_specs=[pl.BlockSpec((None, tr, cols), lambda j, i, cr: (2 * j + cr[0], i, 0)),
                  pl.BlockSpec((None, tr, cols), lambda j, i, cr: (j, i, 0))],
        out_specs=pl.BlockSpec((None, tr, cols), lambda j, i, cr: (j, i, 0)))
    return pl.pallas_call(body, grid_spec=gs, out_shape=_sds((4, rows, cols), BF16), name=name,
                          compiler_params=_params(("parallel", "parallel"), 6 * tr * cols * 4))(core, slots, recv)


def _adam_math(w, g, m, v):
    m2 = B1 * m + (1.0 - B1) * g
    v2 = B2 * v + (1.0 - B2) * jnp.square(g)
    m_hat = m2 / (1.0 - B1 ** STEP)
    v_hat = v2 / (1.0 - B2 ** STEP)
    return -LR * (m_hat / (jnp.sqrt(v_hat) + EPS_ADAM) + WD * w), m2, v2


def _adamw_reduced(part, recv, chip, w, m, v, tr, name):
    rows, cols = w.shape
    tr = _row_tile(rows, tr)

    def body(j_ref, p_ref, r0_ref, r1_ref, r2_ref, w_ref, m_ref, v_ref, g_ref, d_ref, nm_ref, nv_ref):
        g = p_ref[...].astype(F32) + r0_ref[...].astype(F32) + r1_ref[...].astype(F32) + r2_ref[...].astype(F32)
        d, m2, v2 = _adam_math(w_ref[...], g, m_ref[...], v_ref[...])
        g_ref[...] = g
        d_ref[...] = d
        nm_ref[...] = m2
        nv_ref[...] = v2

    flat = pl.BlockSpec((tr, cols), lambda i, jr: (i, 0))
    gs = pltpu.PrefetchScalarGridSpec(
        num_scalar_prefetch=1, grid=(rows // tr,),
        in_specs=[pl.BlockSpec((None, tr, cols), lambda i, jr: (jr[0], i, 0))]
        + [pl.BlockSpec((None, tr, cols), functools.partial(lambda i, jr, k: (k, i, 0), k=k)) for k in range(3)] + [flat] * 3,
        out_specs=[flat] * 4)
    return pl.pallas_call(body, grid_spec=gs, out_shape=[_sds((rows, cols))] * 4, name=name,
                          compiler_params=_params(("parallel",), 22 * tr * cols * 4))(chip, part, recv, recv, recv, w, m, v)


def _adamw_plain(w, g, m, v, name):
    def body(w_ref, g_ref, m_ref, v_ref, d_ref, nm_ref, nv_ref):
        d, m2, v2 = _adam_math(w_ref[...], g_ref[...], m_ref[...], v_ref[...])
        d_ref[...] = d
        nm_ref[...] = m2
        nv_ref[...] = v2

    return pl.pallas_call(body, out_shape=[_sds(w.shape)] * 3, name=name)(w, g, m, v)


def _pack_small(norm1_w, conv_w, a_log, dt_bias, gdn_norm_w, ret_norm_w, norm2_w, norm_f_w):
    misc = jnp.concatenate([gdn_norm_w.reshape(1, DV), a_log.reshape(1, H), dt_bias.reshape(1, H),
                            jnp.zeros((1, D - DV - 2 * H), F32)], axis=1)
    return jnp.concatenate([norm1_w.reshape(1, D), ret_norm_w.reshape(1, D), norm2_w.reshape(1, D), norm_f_w.reshape(1, D),
                            conv_w.reshape(8, D), misc, jnp.zeros((3, D), F32)], axis=0)


def _unpack_small(pack):
    return dict(norm1_w=pack[0:1], ret_norm_w=pack[1:2], norm2_w=pack[2:3], norm_f_w=pack[3], conv_w=pack[4:12].reshape(4, 2 * D),
                gdn_norm_w=pack[12:13, 0:DV], a_log=pack[12:13, DV:DV + H], dt_bias=pack[12:13, DV + H:DV + 2 * H])


def _regroup_w_in(w):
    return jnp.concatenate([w[:, :BA_END], jnp.zeros((w.shape[0], P_IN - N_IN), w.dtype), w[:, BA_END:]], axis=1)


def _ungroup_w_in(g):
    return jnp.concatenate([g[:, :BA_END], g[:, BA_END + P_IN - N_IN:]], axis=1)


def _to_slots_cols(g):
    rows, cols = g.shape
    return g.reshape(rows, NDEV, cols // NDEV).transpose(1, 0, 2)


def _from_slots_cols(a):
    n, rows, cols = a.shape
    return a.transpose(1, 0, 2).reshape(rows, n * cols)


WEIGHT_ORDER = ["norm1_w", "w_in", "conv_w", "a_log", "dt_bias", "gdn_norm_w", "ret_norm_w", "w_out", "norm2_w", "w_gate", "w_up",
                "w_down", "norm_f_w"]


def kernel(x, norm1_w, w_in, conv_w, a_log, dt_bias, gdn_norm_w, ret_norm_w, w_out, norm2_w, w_gate, w_up, w_down, norm_f_w, loss_target, m_norm1_w, m_w_in, m_conv_w, m_a_log, m_dt_bias, m_gdn_norm_w, m_ret_norm_w, m_w_out, m_norm2_w, m_w_gate, m_w_up, m_w_down, m_norm_f_w, v_norm1_w, v_w_in, v_conv_w, v_a_log, v_dt_bias, v_gdn_norm_w, v_ret_norm_w, v_w_out, v_norm2_w, v_w_gate, v_w_up, v_w_down, v_norm_f_w):
    ax, ay, ac = _coords()
    me = 4 * ax + 2 * ay + ac
    core = jnp.reshape(ac, (1,)).astype(jnp.int32)
    chip = jnp.reshape(2 * ax + ay, (1,)).astype(jnp.int32)
    w = dict(norm1_w=norm1_w, w_in=w_in[0], conv_w=conv_w[0], a_log=a_log, dt_bias=dt_bias, gdn_norm_w=gdn_norm_w,
             ret_norm_w=ret_norm_w, w_out=w_out[0], norm2_w=norm2_w, w_gate=w_gate[0], w_up=w_up[0], w_down=w_down[0],
             norm_f_w=norm_f_w)
    m = dict(norm1_w=m_norm1_w, w_in=m_w_in[0], conv_w=m_conv_w[0], a_log=m_a_log, dt_bias=m_dt_bias, gdn_norm_w=m_gdn_norm_w,
             ret_norm_w=m_ret_norm_w, w_out=m_w_out[0], norm2_w=m_norm2_w, w_gate=m_w_gate[0], w_up=m_w_up[0], w_down=m_w_down[0],
             norm_f_w=m_norm_f_w)
    v = dict(norm1_w=v_norm1_w, w_in=v_w_in[0], conv_w=v_conv_w[0], a_log=v_a_log, dt_bias=v_dt_bias, gdn_norm_w=v_gdn_norm_w,
             ret_norm_w=v_ret_norm_w, w_out=v_w_out[0], norm2_w=v_norm2_w, w_gate=v_w_gate[0], w_up=v_w_up[0], w_down=v_w_down[0],
             norm_f_w=v_norm_f_w)
    big_names = ["w_in", "w_out", "w_gate", "w_up", "w_down"]

    gathered = _allgather([w[k].astype(BF16) for k in big_names] + [w["conv_w"]], "weights_allgather")
    full = dict(zip(big_names, gathered[:5]))
    conv_full = _from_slots_cols(gathered[5])
    w_in_full = _regroup_w_in(_from_slots_cols(full["w_in"]))
    w_gate_full = _from_slots_cols(full["w_gate"])
    w_up_full = _from_slots_cols(full["w_up"])
    w_out_full = full["w_out"].reshape(D, D)
    w_down_full = full["w_down"].reshape(-1, D)

    loss_tile, dx, big, small = _local_step(
        x[0], loss_target[0], w_in_full, w_out_full, w_gate_full, w_up_full, w_down_full, norm1_w, conv_full, a_log, dt_bias,
        gdn_norm_w, ret_norm_w, norm2_w, norm_f_w.reshape(1, D))
    loss = lax.psum(loss_tile[0, 0], ("x", "y", "c"))

    slots = dict(w_in=_to_slots_cols(_ungroup_w_in(big["w_in"])), w_gate=_to_slots_cols(big["w_gate"]),
                 w_up=_to_slots_cols(big["w_up"]), w_out=big["w_out"].reshape(NDEV, D // NDEV, D),
                 w_down=big["w_down"].reshape(NDEV, -1, D))
    from_sibling = _exchange_sibling([slots[k] for k in big_names], "grads_to_sibling")
    parts = [_add_sibling(slots[k], r, core, 128, "grads_add_" + k) for k, r in zip(big_names, from_sibling)]
    from_chips = _exchange_chips(parts, "grads_to_chips")
    out = {}
    for k, part, recv in zip(big_names, parts, from_chips):
        out[k] = _adamw_reduced(part, recv, chip, w[k], m[k], v[k], 128, "adamw_" + k)

    g_small = _unpack_small(_allreduce_small(_pack_small(**small), "small_grads_allreduce"))
    g_small["conv_w"] = lax.dynamic_slice_in_dim(g_small["conv_w"], me * (2 * D // NDEV), 2 * D // NDEV, axis=1)
    small_names = [k for k in WEIGHT_ORDER if k not in big_names]
    pad_conv = lambda a: jnp.pad(a, ((0, 0), (0, 2 * D - a.shape[1])))
    packs = []
    for src in (w, g_small, m, v):
        args = {k: (pad_conv(src[k]) if k == "conv_w" else src[k]) for k in small_names}
        packs.append(_pack_small(**args))
    d_pack, m_pack, v_pack = _adamw_plain(*packs[0:1], packs[1], packs[2], packs[3], name="adamw_small")
    cut_conv = lambda dct: {**dct, "conv_w": dct["conv_w"][:, :2 * D // NDEV]}
    d_small, m_small, v_small = (cut_conv(_unpack_small(p)) for p in (d_pack, m_pack, v_pack))

    def shaped(k, a):
        return a.reshape(w_shapes[k])

    w_shapes = dict(norm1_w=norm1_w.shape, w_in=w_in.shape, conv_w=conv_w.shape, a_log=a_log.shape, dt_bias=dt_bias.shape,
                    gdn_norm_w=gdn_norm_w.shape, ret_norm_w=ret_norm_w.shape, w_out=w_out.shape, norm2_w=norm2_w.shape,
                    w_gate=w_gate.shape, w_up=w_up.shape, w_down=w_down.shape, norm_f_w=norm_f_w.shape)
    grads, deltas, new_m, new_v = [], [], [], []
    for k in WEIGHT_ORDER:
        if k in big_names:
            g_, d_, m_, v_ = out[k]
        else:
            g_, d_, m_, v_ = g_small[k], d_small[k], m_small[k], v_small[k]
        grads.append(shaped(k, g_))
        deltas.append(shaped(k, d_))
        new_m.append(shaped(k, m_))
        new_v.append(shaped(k, v_))
    return (loss, dx[None], *grads, *deltas, *new_m, *new_v)
```

```python
import functools
import numpy as np
import jax
import jax.numpy as jnp
from jax import lax
from jax.experimental import pallas as pl
from jax.experimental.pallas import tpu as pltpu

F32, BF16 = jnp.float32, jnp.bfloat16
HI = lax.Precision.HIGHEST
MESH = pl.DeviceIdType.MESH
ANY = pl.BlockSpec(memory_space=pl.ANY)
VMEM_FULL = pl.BlockSpec(memory_space=pltpu.VMEM)

NDEV = 8
D = 2048
H = 8
DK = 128
DV = 256
C = 64
CPB = 4
EPS = 1e-6
ROPE_BASE = 10000.0
N_IN = 16400
O_Z, O_BA, O_RQ, O_RK, O_RV, O_RG, O_GA, O_GB, P_IN = 4096, 6144, 6400, 7424, 8448, 10496, 12544, 14592, 16640
BA_END = 6160
LR, B1, B2, EPS_ADAM, WD, STEP = 0.001, 0.9, 0.999, 1e-08, 0.01, 10
VMEM_CAP = 60 * 1024 * 1024

NN = ((1,), (0,))
NT = ((1,), (1,))
TN = ((0,), (0,))


def _params(sem=None, est=None):
    kw = {}
    if sem is not None:
        kw["dimension_semantics"] = sem
    if est is not None:
        kw["vmem_limit_bytes"] = int(min(VMEM_CAP, max(32 * 1024 * 1024, est * 5 // 4 + (4 << 20))))
    return pltpu.CompilerParams(**kw)


def _sds(shape, dt=F32):
    return jax.ShapeDtypeStruct(tuple(shape), dt)


def _row_tile(rows, limit):
    return max(t for t in range(16, min(rows, limit) + 1, 16) if rows % t == 0)


def _bdot(a, b, dims):
    return lax.dot_general(a.astype(BF16), b.astype(BF16), (dims, ((), ())), preferred_element_type=F32)


def _hdot(a, b, dims):
    return lax.dot_general(a, b, (dims, ((), ())), precision=HI, preferred_element_type=F32)


def _silu(x):
    return x * jax.nn.sigmoid(x)


def _rms(x, w):
    return x * lax.rsqrt(jnp.mean(x * x, axis=-1, keepdims=True) + EPS) * w


class _Side:
    def __init__(self, ins, out_shapes, sems, start, finish):
        self.ins, self.out_shapes, self.sems, self.start, self.finish = list(ins), list(out_shapes), list(sems), start, finish


def _run_side(side, name):
    ni, no = len(side.ins), len(side.out_shapes)

    def body(*refs):
        ins, outs, sems = refs[:ni], refs[ni:ni + no], refs[ni + no:]
        side.start(ins, outs, *sems)
        side.finish(ins, outs, *sems)

    return pl.pallas_call(body, in_specs=[ANY] * ni, out_specs=[ANY] * no, out_shape=side.out_shapes,
                          scratch_shapes=[pltpu.SemaphoreType.DMA(s) for s in side.sems], name=name)(*side.ins)


def _matmul(a, b, *, ta=False, tb=False, tm, tn, tk, out_dtype=F32, res=None, side=None, name):
    m = a.shape[1] if ta else a.shape[0]
    k = a.shape[0] if ta else a.shape[1]
    n = b.shape[0] if tb else b.shape[1]
    assert k == (b.shape[1] if tb else b.shape[0])
    tm, tn, tk = min(tm, m), min(tn, n), min(tk, k)
    assert m % tm == 0 and n % tn == 0 and k % tk == 0, (name, m, n, k, tm, tn, tk)
    nk = k // tk
    dims = ((0 if ta else 1,), (1 if tb else 0,))
    has_res = res is not None
    n_in = 3 if has_res else 2
    n_side_in = len(side.ins) if side else 0
    n_side_out = len(side.out_shapes) if side else 0
    grid = (m // tm, n // tn, nk)

    def body(*refs):
        a_ref, b_ref = refs[0], refs[1]
        r_ref = refs[2] if has_res else None
        o_ref = refs[n_in + n_side_in]
        if side:
            side_ins = refs[n_in:n_in + n_side_in]
            side_outs = refs[n_in + n_side_in + 1:n_in + n_side_in + 1 + n_side_out]
            side_sems = refs[len(refs) - len(side.sems):]
            step = (pl.program_id(0) * grid[1] + pl.program_id(1)) * grid[2] + pl.program_id(2)

            @pl.when(step == 0)
            def _():
                side.start(side_ins, side_outs, *side_sems)

        def finish(acc):
            if has_res:
                acc = acc + r_ref[...].astype(F32)
            o_ref[...] = acc.astype(out_dtype)

        part = _bdot(a_ref[...], b_ref[...], dims)
        if nk == 1:
            finish(part)
        else:
            acc_ref = refs[n_in + n_side_in + 1 + n_side_out]
            kk = pl.program_id(2)

            @pl.when(kk == 0)
            def _():
                acc_ref[...] = part

            @pl.when(kk > 0)
            def _():
                acc_ref[...] += part

            @pl.when(kk == nk - 1)
            def _():
                finish(acc_ref[...])

        if side:
            @pl.when(step == grid[0] * grid[1] * grid[2] - 1)
            def _():
                side.finish(side_ins, side_outs, *side_sems)

    a_spec = pl.BlockSpec((tk, tm), lambda i, j, kk: (kk, i)) if ta else pl.BlockSpec((tm, tk), lambda i, j, kk: (i, kk))
    b_spec = pl.BlockSpec((tn, tk), lambda i, j, kk: (j, kk)) if tb else pl.BlockSpec((tk, tn), lambda i, j, kk: (kk, j))
    o_spec = pl.BlockSpec((tm, tn), lambda i, j, kk: (i, j))
    in_specs = [a_spec, b_spec] + ([o_spec] if has_res else []) + [ANY] * n_side_in
    est = 2 * (tm * tk * a.dtype.itemsize + tk * tn * b.dtype.itemsize + tm * tn * jnp.dtype(out_dtype).itemsize)
    est += 2 * tm * tn * 4 * (1 if has_res else 0) + (tm * tn * 4 if nk > 1 else 0) + 2 * tm * tn * 4
    args = (a, b) + ((res,) if has_res else ()) + (tuple(side.ins) if side else ())
    scratch = ([pltpu.VMEM((tm, tn), F32)] if nk > 1 else []) + ([pltpu.SemaphoreType.DMA(s) for s in side.sems] if side else [])
    sem = ("arbitrary",) * 3 if side else ("parallel", "parallel", "arbitrary")
    out = pl.pallas_call(
        body, grid=grid, in_specs=in_specs, out_specs=[o_spec] + [ANY] * n_side_out,
        out_shape=[_sds((m, n), out_dtype)] + (side.out_shapes if side else []),
        scratch_shapes=scratch, name=name, compiler_params=_params(sem, est))(*args)
    return (out[0], out[1:]) if side else out[0]


def _rms_fwd(x, w, ts, name):
    s = x.shape[0]

    def body(x_ref, w_ref, o_ref, ot_ref):
        y = _rms(x_ref[...], w_ref[...]).astype(BF16)
        o_ref[...] = y
        ot_ref[...] = y.T

    row = pl.BlockSpec((ts, D), lambda i: (i, 0))
    return pl.pallas_call(body, grid=(s // ts,), in_specs=[row, pl.BlockSpec((1, D), lambda i: (0, 0))],
                          out_specs=[row, pl.BlockSpec((D, ts), lambda i: (0, i))],
                          out_shape=[_sds((s, D), BF16), _sds((D, s), BF16)], name=name,
                          compiler_params=_params(("parallel",)))(x, w)


def _rms_bwd(x, w, du, dres, ts, name):
    s = x.shape[0]

    def body(x_ref, w_ref, du_ref, dres_ref, dx_ref, dw_ref):
        _, vjp = jax.vjp(_rms, x_ref[...], w_ref[...])
        dx, dw = vjp(du_ref[...].astype(F32))
        dx_ref[...] = dx + dres_ref[...]

        @pl.when(pl.program_id(0) == 0)
        def _():
            dw_ref[...] = jnp.zeros_like(dw_ref)

        dw_ref[...] += dw

    row = pl.BlockSpec((ts, D), lambda i: (i, 0))
    vec = pl.BlockSpec((1, D), lambda i: (0, 0))
    return pl.pallas_call(body, grid=(s // ts,), in_specs=[row, vec, row, row], out_specs=[row, vec],
                          out_shape=[_sds((s, D)), _sds((1, D))], name=name,
                          compiler_params=_params(("arbitrary",), 12 * ts * D * 4))(x, w, du, dres)


def _conv_taps(xx, w, base, ts):
    acc = xx[base:base + ts] * w[0:1, :]
    for j in range(1, 4):
        acc = acc + xx[base + j:base + j + ts] * w[j:j + 1, :]
    return acc


def _causal_conv(prev8, cur, w, first):
    xx = jnp.concatenate([jnp.where(first, 0.0, prev8), cur], axis=0)
    return _conv_taps(xx, w, 5, cur.shape[0])


def _qk_post(c, scale):
    s = _silu(c)
    return s * lax.rsqrt(jnp.sum(s * s, axis=-1, keepdims=True) + EPS) * scale


def _conv_specs(ts, cw, col0):
    cur = pl.BlockSpec((ts, cw), lambda j, i: (i, col0 + j))
    prev = pl.BlockSpec((8, cw), lambda j, i: (jnp.maximum(i * (ts // 8) - 1, 0), col0 + j))
    wsp = pl.BlockSpec((4, cw), lambda j, i: (0, col0 + j))
    return cur, prev, wsp


def _gdn_qkv_fwd(proj, conv_w, ts):
    s = proj.shape[0]

    def qk_body(cur_ref, prev_ref, w_ref, o_ref):
        c = _causal_conv(prev_ref[...], cur_ref[...], w_ref[...], pl.program_id(1) == 0)
        scale = jnp.where(pl.program_id(0) < H, DK ** -0.5, 1.0).astype(F32)
        o_ref[...] = _qk_post(c, scale)

    cur, prev, wsp = _conv_specs(ts, DK, 0)
    qk = pl.pallas_call(qk_body, grid=(2 * H, s // ts), in_specs=[cur, prev, wsp],
                        out_specs=pl.BlockSpec((ts, DK), lambda j, i: (i, j)), out_shape=_sds((s, 2 * H * DK)),
                        name="gdn_qk_prep", compiler_params=_params(("parallel", "parallel")))(proj, proj, conv_w)

    def v_body(cur_ref, prev_ref, w_ref, o_ref):
        o_ref[...] = _silu(_causal_conv(prev_ref[...], cur_ref[...], w_ref[...], pl.program_id(1) == 0))

    cw = 512
    cur, prev, wsp = _conv_specs(ts, cw, 2 * H * DK // cw)
    v = pl.pallas_call(v_body, grid=(H * DV // cw, s // ts), in_specs=[cur, prev, wsp],
                       out_specs=pl.BlockSpec((ts, cw), lambda j, i: (i, j)), out_shape=_sds((s, H * DV)),
                       name="gdn_v_prep", compiler_params=_params(("parallel", "parallel")))(proj, proj, conv_w)
    return qk, v


def _gdn_qkv_bwd(proj, conv_w, dqk, dv, ts):
    s = proj.shape[0]
    nt = s // ts

    def qk_body(cur_ref, prev_ref, w_ref, d_ref, o_ref):
        c = _causal_conv(prev_ref[...], cur_ref[...], w_ref[...], pl.program_id(1) == 0)
        scale = jnp.where(pl.program_id(0) < H, DK ** -0.5, 1.0).astype(F32)
        _, vjp = jax.vjp(lambda cc: _qk_post(cc, scale), c)
        o_ref[...] = vjp(d_ref[...])[0]

    cur, prev, wsp = _conv_specs(ts, DK, 0)
    dc_qk = pl.pallas_call(qk_body, grid=(2 * H, nt), in_specs=[cur, prev, wsp, pl.BlockSpec((ts, DK), lambda j, i: (i, j))],
                           out_specs=pl.BlockSpec((ts, DK), lambda j, i: (i, j)), out_shape=_sds((s, 2 * H * DK)),
                           name="gdn_qk_prep_bwd", compiler_params=_params(("parallel", "parallel")))(proj, proj, conv_w, dqk)

    def v_body(cur_ref, prev_ref, w_ref, d_ref, o_ref):
        c = _causal_conv(prev_ref[...], cur_ref[...], w_ref[...], pl.program_id(1) == 0)
        _, vjp = jax.vjp(_silu, c)
        o_ref[...] = vjp(d_ref[...])[0]

    cw = 512
    cur, prev, wsp = _conv_specs(ts, cw, 2 * H * DK // cw)
    dc_v = pl.pallas_call(v_body, grid=(H * DV // cw, nt), in_specs=[cur, prev, wsp, pl.BlockSpec((ts, cw), lambda j, i: (i, j))],
                          out_specs=pl.BlockSpec((ts, cw), lambda j, i: (i, j)), out_shape=_sds((s, H * DV)),
                          name="gdn_v_prep_bwd", compiler_params=_params(("parallel", "parallel")))(proj, proj, conv_w, dv)

    def conv_bwd(dc, col0, dcol0, ncols, name):
        def body(x_ref, xprev_ref, w_ref, dc_ref, dcnext_ref, da_ref, dw_ref):
            i = pl.program_id(1)
            w = w_ref[...]
            dcur = dc_ref[...]
            dd = jnp.concatenate([dcur, jnp.where(i == nt - 1, 0.0, dcnext_ref[...])], axis=0)
            acc = dd[3:3 + ts] * w[0:1, :]
            for j in range(1, 4):
                acc = acc + dd[3 - j:3 - j + ts] * w[j:j + 1, :]
            da_ref[...] = acc.astype(BF16)
            xx = jnp.concatenate([jnp.where(i == 0, 0.0, xprev_ref[...]), x_ref[...]], axis=0)

            @pl.when(i == 0)
            def _():
                dw_ref[...] = jnp.zeros_like(dw_ref)

            for j in range(4):
                dw_ref[j:j + 1, :] += jnp.sum(dcur * xx[5 + j:5 + j + ts], axis=0, keepdims=True)

        cur, prev, wsp = _conv_specs(ts, cw, col0)
        dcur = pl.BlockSpec((ts, cw), lambda j, i: (i, j))
        dnext = pl.BlockSpec((8, cw), lambda j, i: (jnp.minimum((i + 1) * (ts // 8), s // 8 - 1), j))
        return pl.pallas_call(body, grid=(ncols // cw, nt), in_specs=[cur, prev, wsp, dcur, dnext],
                              out_specs=[pl.BlockSpec((ts, cw), lambda j, i: (i, j)), pl.BlockSpec((4, cw), lambda j, i: (0, j))],
                              out_shape=[_sds((s, ncols), BF16), _sds((4, ncols))], name=name,
                              compiler_params=_params(("parallel", "arbitrary")))(proj, proj, conv_w, dc, dc)

    da_qk, dw_qk = conv_bwd(dc_qk, 0, 0, 2 * H * DK, "conv_bwd_qk")
    da_v, dw_v = conv_bwd(dc_v, 2 * H * DK // cw, 0, H * DV, "conv_bwd_v")
    return da_qk, da_v, jnp.concatenate([dw_qk, dw_v], axis=1)


def _bg(b, a, alog, dtb):
    n = b.shape[0]
    g = -jnp.exp(alog) * jax.nn.softplus(a + dtb)
    row = lax.broadcasted_iota(jnp.int32, (n, n), 0)
    col = lax.broadcasted_iota(jnp.int32, (n, n), 1)
    shift = C.bit_length() - 1
    same = (row >> shift) == (col >> shift)
    return jax.nn.sigmoid(b), _hdot((same & (row >= col)).astype(F32), g, NN), _hdot(same.astype(F32), g, NN)


def _bg_fwd(proj, alog, dtb, ts):
    s = proj.shape[0]

    def body(ba_ref, alog_ref, dtb_ref, beta_ref, gc_ref, gl_ref):
        beta_ref[...], gc_ref[...], gl_ref[...] = _bg(ba_ref[:, 0:H], ba_ref[:, H:2 * H], alog_ref[...], dtb_ref[...])

    small = pl.BlockSpec((1, H), lambda i: (0, 0))
    out = pl.BlockSpec((ts, H), lambda i: (i, 0))
    return pl.pallas_call(body, grid=(s // ts,), in_specs=[pl.BlockSpec((ts, 256), lambda i: (i, O_BA // 256)), small, small],
                          out_specs=[out] * 3, out_shape=[_sds((s, H))] * 3, name="gdn_bg_prep",
                          compiler_params=_params(("parallel",)))(proj, alog, dtb)


def _bg_bwd(proj, alog, dtb, dbeta_h, dgc_h, dgl_h, ts):
    s = proj.shape[0]

    def body(ba_ref, alog_ref, dtb_ref, dbeta_ref, dgc_ref, dgl_ref, dba_ref, dalog_ref, ddtb_ref):
        _, vjp = jax.vjp(_bg, ba_ref[:, 0:H], ba_ref[:, H:2 * H], alog_ref[...], dtb_ref[...])
        db, da, dalog, ddtb = vjp((jnp.sum(dbeta_ref[...], axis=0), jnp.sum(dgc_ref[...], axis=0), jnp.sum(dgl_ref[...], axis=0)))
        dba_ref[...] = jnp.zeros_like(dba_ref)
        dba_ref[:, 0:H] = db.astype(BF16)
        dba_ref[:, H:2 * H] = da.astype(BF16)

        @pl.when(pl.program_id(0) == 0)
        def _():
            dalog_ref[...] = jnp.zeros_like(dalog_ref)
            ddtb_ref[...] = jnp.zeros_like(ddtb_ref)

        dalog_ref[...] += dalog
        ddtb_ref[...] += ddtb

    small = pl.BlockSpec((1, H), lambda i: (0, 0))
    per_head = pl.BlockSpec((H, ts, H), lambda i: (0, i, 0))
    return pl.pallas_call(body, grid=(s // ts,),
                          in_specs=[pl.BlockSpec((ts, 256), lambda i: (i, O_BA // 256)), small, small, per_head, per_head, per_head],
                          out_specs=[pl.BlockSpec((ts, 256), lambda i: (i, 0)), small, small],
                          out_shape=[_sds((s, 256), BF16), _sds((1, H)), _sds((1, H))], name="gdn_bg_prep_bwd",
                          compiler_params=_params(("arbitrary",)))(proj, alog, dtb, dbeta_h, dgc_h, dgl_h)


BLK = 4 * C
NNB, NTB, TNB = ((2,), (1,)), ((2,), (2,)), ((1,), (1,))


def _bdot_b(a, b, dims):
    return lax.dot_general(a.astype(BF16), b.astype(BF16), (dims, ((0,), (0,))), preferred_element_type=F32)


@jax.custom_vjp
def _inv_unit_lower(a):
    n = a.shape[-1]
    row = lax.broadcasted_iota(jnp.int32, (n, n), 0)
    col = lax.broadcasted_iota(jnp.int32, (n, n), 1)
    x = jnp.where(row == col, 1.0, 0.0).astype(F32) - a
    p = _bdot_b(a, a, NNB)
    power = 2
    while True:
        x = x + _bdot_b(x, p, NNB)
        power *= 2
        if power >= C:
            return x
        p = _bdot_b(p, p, NNB)


def _inv_fwd(a):
    t = _inv_unit_lower(a)
    return t, t


def _inv_bwd(t, dt):
    return (-_bdot_b(_bdot_b(t, dt, TNB), t, NTB),)


_inv_unit_lower.defvjp(_inv_fwd, _inv_bwd)


def _gdn_prep(q, k, v, bfull, gcfull, glfull, hmask):
    nb, n = q.shape[0], q.shape[1]
    beta = jnp.sum(bfull * hmask, axis=-1, keepdims=True)
    gc = jnp.sum(gcfull * hmask, axis=-1, keepdims=True)
    gl = jnp.sum(glfull * hmask, axis=-1, keepdims=True)
    row = lax.broadcasted_iota(jnp.int32, (n, n), 0)
    col = lax.broadcasted_iota(jnp.int32, (n, n), 1)
    shift = C.bit_length() - 1
    same = (row >> shift) == (col >> shift)
    incl, strict = same & (row >= col), same & (row > col)
    g_i = gc * jnp.ones((1, 1, n), F32)
    decay = jnp.exp(jnp.where(incl, g_i - jnp.swapaxes(g_i, 1, 2), -jnp.inf))
    kb = k * beta
    a = jnp.where(strict, _bdot_b(kb, k, NTB) * decay, 0.0)
    tinv = _inv_unit_lower(a)
    u = _bdot_b(tinv, v * beta, NNB)
    w = _bdot_b(tinv, kb * jnp.exp(gc), NNB)
    attn = _bdot_b(q, k, NTB) * decay
    fold = ((lax.broadcasted_iota(jnp.int32, (n, C), 0) & (C - 1)) == lax.broadcasted_iota(jnp.int32, (n, C), 1)).astype(F32)
    attn_c = _bdot(attn.reshape(nb * n, n), fold, NN).reshape(nb, n, C)
    return u, w, attn_c, q * jnp.exp(gc), k * jnp.exp(gl - gc), jnp.exp(gl)


def _gdn_step(u, w, attn, qg, kd, egl, state):
    v_new = u - _bdot(w, state, NN)
    o = _bdot(qg, state, NN) + _bdot(attn, v_new, NN)
    return o, state * egl + _bdot(kd, v_new, TN)


def _head_mask(h):
    return (lax.broadcasted_iota(jnp.int32, (1, H), 1) == h).astype(F32)


PREP_BLOCKS = 2


def _gdn_prep_specs(r):
    small = pl.BlockSpec((r, H), lambda h, c: (c, 0))
    return [pl.BlockSpec((r, DK), lambda h, c: (c, h)), pl.BlockSpec((r, DK), lambda h, c: (c, H + h)),
            pl.BlockSpec((r, DV), lambda h, c: (c, h)), small, small, small]


def _blocked(ref):
    x = ref[...]
    return x.reshape(PREP_BLOCKS, BLK, x.shape[-1])


def _gdn_inter_specs(r):
    col = pl.BlockSpec((r, DK), lambda h, c: (c, h))
    return [pl.BlockSpec((r, DV), lambda h, c: (c, h)), col, pl.BlockSpec((1, r, C), lambda h, c: (h, c, 0)), col, col,
            pl.BlockSpec((1, r // C, 8, 128), lambda h, c: (h, c, 0, 0))]


def _gdn_prep_fwd(qk, v, beta, gc, gl):
    s = qk.shape[0]
    r = PREP_BLOCKS * BLK

    def body(q_ref, k_ref, v_ref, b_ref, gc_ref, gl_ref, u_ref, w_ref, attn_ref, qg_ref, kd_ref, egl_ref):
        u, w, attn, qg, kd, egl = _gdn_prep(_blocked(q_ref), _blocked(k_ref), _blocked(v_ref), _blocked(b_ref), _blocked(gc_ref),
                                            _blocked(gl_ref), _head_mask(pl.program_id(0)))
        u_ref[...] = u.reshape(r, DV)
        w_ref[...] = w.reshape(r, DK).astype(BF16)
        attn_ref[0] = attn.reshape(r, C).astype(BF16)
        qg_ref[...] = qg.reshape(r, DK).astype(BF16)
        kd_ref[...] = kd.reshape(r, DK).astype(BF16)
        egl = egl.reshape(r, 1)
        for j in range(r // C):
            egl_ref[0, j] = egl[j * C:j * C + 1, :] * jnp.ones((8, 128), F32)

    out_shape = [_sds((s, H * DV)), _sds((s, H * DK), BF16), _sds((H, s, C), BF16), _sds((s, H * DK), BF16),
                 _sds((s, H * DK), BF16), _sds((H, s // C, 8, 128))]
    return pl.pallas_call(body, grid=(H, s // r), in_specs=_gdn_prep_specs(r), out_specs=_gdn_inter_specs(r), out_shape=out_shape,
                          name="gdn_prep_fwd", compiler_params=_params(("parallel", "parallel")))(qk, qk, v, beta, gc, gl)


def _gdn_prep_bwd(qk, v, beta, gc, gl, du, dw, dattn, dqg, dkd, degl):
    s = qk.shape[0]
    r = PREP_BLOCKS * BLK

    def body(q_ref, k_ref, v_ref, b_ref, gc_ref, gl_ref, du_ref, dw_ref, dattn_ref, dqg_ref, dkd_ref, degl_ref,
             dq_ref, dk_ref, dv_ref, db_ref, dgc_ref, dgl_ref):
        hmask = _head_mask(pl.program_id(0))
        _, vjp = jax.vjp(lambda q, k, v, b, gc, gl: _gdn_prep(q, k, v, b, gc, gl, hmask), _blocked(q_ref), _blocked(k_ref),
                         _blocked(v_ref), _blocked(b_ref), _blocked(gc_ref), _blocked(gl_ref))
        rowid = lax.broadcasted_iota(jnp.int32, (r, 1), 0)
        degl = jnp.zeros((r, 1), F32)
        for j in range(r // C):
            degl = jnp.where(rowid == j * C, degl_ref[0, j, 0:1, 0:1], degl)
        dq, dk, dv, db, dgc, dgl = vjp((_blocked(du_ref), _blocked(dw_ref), _blocked(dattn_ref.at[0]), _blocked(dqg_ref),
                                        _blocked(dkd_ref), degl.reshape(PREP_BLOCKS, BLK, 1)))
        dq_ref[...] = dq.reshape(r, DK)
        dk_ref[...] = dk.reshape(r, DK)
        dv_ref[...] = dv.reshape(r, DV)
        db_ref[0] = db.reshape(r, H)
        dgc_ref[0] = dgc.reshape(r, H)
        dgl_ref[0] = dgl.reshape(r, H)

    col = pl.BlockSpec((r, DK), lambda h, c: (c, h))
    piece = pl.BlockSpec((1, r, H), lambda h, c: (h, c, 0))
    dq, dk, dv, db, dgc, dgl = pl.pallas_call(
        body, grid=(H, s // r), in_specs=_gdn_prep_specs(r) + _gdn_inter_specs(r),
        out_specs=[col, col, pl.BlockSpec((r, DV), lambda h, c: (c, h)), piece, piece, piece],
        out_shape=[_sds((s, H * DK)), _sds((s, H * DK)), _sds((s, H * DV))] + [_sds((H, s, H))] * 3,
        name="gdn_prep_bwd", compiler_params=_params(("parallel", "parallel"), 24 << 20))(
            qk, qk, v, beta, gc, gl, du, dw, dattn, dqg, dkd, degl)
    return jnp.concatenate([dq, dk], axis=1), dv, db, dgc, dgl


def _gdn_scan_specs(r, order):
    wide = pl.BlockSpec((r, H * DK), lambda c: (order(c), 0))
    return [pl.BlockSpec((r, H * DV), lambda c: (order(c), 0)), wide, pl.BlockSpec((H, r, C), lambda c: (0, order(c), 0)), wide, wide,
            pl.BlockSpec((H, r // C, 8, 128), lambda c: (0, order(c), 0, 0))]


def _gdn_scan_fwd(u, w, attn, qg, kd, egl):
    s = u.shape[0]
    r = CPB * C
    nb = s // r

    def body(u_ref, w_ref, attn_ref, qg_ref, kd_ref, egl_ref, o_ref, st_ref, state_ref):
        @pl.when(pl.program_id(0) == 0)
        def _():
            state_ref[...] = jnp.zeros_like(state_ref)

        for h in range(H):
            kc, vc = slice(h * DK, (h + 1) * DK), slice(h * DV, (h + 1) * DV)
            state = state_ref[h]
            for i in range(CPB):
                rs = slice(i * C, (i + 1) * C)
                st_ref[h, i] = state
                o, state = _gdn_step(u_ref[rs, vc], w_ref[rs, kc], attn_ref[h, rs, :], qg_ref[rs, kc], kd_ref[rs, kc],
                                     egl_ref[h, i, 0:1, 0:1], state)
                o_ref[rs, vc] = o
            state_ref[h] = state

    out_specs = [pl.BlockSpec((r, H * DV), lambda c: (c, 0)), pl.BlockSpec((H, CPB, DK, DV), lambda c: (0, c, 0, 0))]
    return pl.pallas_call(body, grid=(nb,), in_specs=_gdn_scan_specs(r, lambda c: c), out_specs=out_specs,
                          out_shape=[_sds((s, H * DV)), _sds((H, s // C, DK, DV))],
                          scratch_shapes=[pltpu.VMEM((H, DK, DV), F32)], name="gdn_scan_fwd",
                          compiler_params=_params(("arbitrary",), 24 << 20))(u, w, attn, qg, kd, egl)


def _gdn_scan_bwd(u, w, attn, qg, kd, egl, states, do):
    s = u.shape[0]
    r = CPB * C
    nb = s // r

    def body(u_ref, w_ref, attn_ref, qg_ref, kd_ref, egl_ref, st_ref, do_ref,
             du_ref, dw_ref, dattn_ref, dqg_ref, dkd_ref, degl_ref, dstate_ref):
        @pl.when(pl.program_id(0) == 0)
        def _():
            dstate_ref[...] = jnp.zeros_like(dstate_ref)

        for h in range(H):
            kc, vc = slice(h * DK, (h + 1) * DK), slice(h * DV, (h + 1) * DV)
            dstate = dstate_ref[h]
            for i in reversed(range(CPB)):
                rs = slice(i * C, (i + 1) * C)
                _, vjp = jax.vjp(_gdn_step, u_ref[rs, vc], w_ref[rs, kc].astype(F32), attn_ref[h, rs, :].astype(F32),
                                 qg_ref[rs, kc].astype(F32), kd_ref[rs, kc].astype(F32), egl_ref[h, i, 0:1, 0:1], st_ref[h, i])
                du, dw, dattn, dqg, dkd, degl, dstate = vjp((do_ref[rs, vc], dstate))
                du_ref[rs, vc] = du
                dw_ref[rs, kc] = dw
                dattn_ref[h, rs, :] = dattn
                dqg_ref[rs, kc] = dqg
                dkd_ref[rs, kc] = dkd
                degl_ref[h, i] = degl * jnp.ones((8, 128), F32)
            dstate_ref[h] = dstate

    rev = lambda c: nb - 1 - c
    in_specs = _gdn_scan_specs(r, rev) + [pl.BlockSpec((H, CPB, DK, DV), lambda c: (0, rev(c), 0, 0)),
                                          pl.BlockSpec((r, H * DV), lambda c: (rev(c), 0))]
    return pl.pallas_call(
        body, grid=(nb,), in_specs=in_specs, out_specs=_gdn_scan_specs(r, rev),
        out_shape=[_sds((s, H * DV)), _sds((s, H * DK)), _sds((H, s, C)), _sds((s, H * DK)), _sds((s, H * DK)),
                   _sds((H, s // C, 8, 128))],
        scratch_shapes=[pltpu.VMEM((H, DK, DV), F32)], name="gdn_scan_bwd",
        compiler_params=_params(("arbitrary",), 40 << 20))(u, w, attn, qg, kd, egl, states, do)


def _rot(x, cs, sn):
    return x * cs + pltpu.roll(x, DK // 2, 1) * sn


def _rot_t(d, cs, sn):
    return d * cs - pltpu.roll(d, DK // 2, 1) * sn


def _ret_chunk(q, k, v, state, lg):
    n = q.shape[0]
    row = lax.broadcasted_iota(jnp.int32, (n, n), 0)
    col = lax.broadcasted_iota(jnp.int32, (n, n), 1)
    dist = (row - col).astype(F32)
    dmat = jnp.exp(jnp.where(dist >= 0, dist * lg, -jnp.inf))
    scores = _bdot(q, k, NT) * dmat
    pos = lax.broadcasted_iota(jnp.int32, (n, 1), 0).astype(F32)
    xi = jnp.exp((pos + 1.0) * lg)
    zeta = jnp.exp((n - 1.0 - pos) * lg)
    o = _bdot(scores, v, NN) + _bdot(q, state, NN) * xi
    new_state = state * jnp.exp(n * lg) + _bdot(k * zeta, v, TN)
    return o, new_state


def _ret_specs(r, order):
    return [pl.BlockSpec((r, DK), lambda h, c: (order(c), O_RQ // DK + h)), pl.BlockSpec((r, DK), lambda h, c: (order(c), O_RK // DK + h)),
            pl.BlockSpec((r, DV), lambda h, c: (order(c), O_RV // DV + h)), pl.BlockSpec((r, DK), lambda h, c: (order(c), 0)),
            pl.BlockSpec((r, DK), lambda h, c: (order(c), 0)), pl.BlockSpec((1, H), lambda h, c: (0, 0))]


def _ret_scan_fwd(proj, cs, sn, lgtab):
    s = proj.shape[0]
    r = CPB * C
    nb = s // r

    def body(q_ref, k_ref, v_ref, cs_ref, sn_ref, lg_ref, o_ref, st_ref, state_ref):
        @pl.when(pl.program_id(1) == 0)
        def _():
            state_ref[...] = jnp.zeros_like(state_ref)

        lg = jnp.sum(lg_ref[...] * _head_mask(pl.program_id(0)), axis=1, keepdims=True)
        state = state_ref[...]
        for i in range(CPB):
            rs = slice(i * C, (i + 1) * C)
            st_ref[0, i] = state
            q = _rot(q_ref[rs, :], cs_ref[rs, :], sn_ref[rs, :])
            k = _rot(k_ref[rs, :], cs_ref[rs, :], sn_ref[rs, :]) * DK ** -0.5
            o, state = _ret_chunk(q, k, v_ref[rs, :], state, lg)
            o_ref[rs, :] = o
        state_ref[...] = state

    out_specs = [pl.BlockSpec((r, DV), lambda h, c: (c, h)), pl.BlockSpec((1, CPB, DK, DV), lambda h, c: (h, c, 0, 0))]
    return pl.pallas_call(body, grid=(H, nb), in_specs=_ret_specs(r, lambda c: c), out_specs=out_specs,
                          out_shape=[_sds((s, H * DV)), _sds((H, s // C, DK, DV))],
                          scratch_shapes=[pltpu.VMEM((DK, DV), F32)], name="ret_scan_fwd",
                          compiler_params=_params(("parallel", "arbitrary")))(proj, proj, proj, cs, sn, lgtab)


def _ret_scan_bwd(proj, cs, sn, lgtab, states, do):
    s = proj.shape[0]
    r = CPB * C
    nb = s // r

    def body(q_ref, k_ref, v_ref, cs_ref, sn_ref, lg_ref, st_ref, do_ref, dq_ref, dk_ref, dv_ref, dstate_ref):
        @pl.when(pl.program_id(1) == 0)
        def _():
            dstate_ref[...] = jnp.zeros_like(dstate_ref)

        lg = jnp.sum(lg_ref[...] * _head_mask(pl.program_id(0)), axis=1, keepdims=True)
        dstate = dstate_ref[...]
        for i in reversed(range(CPB)):
            rs = slice(i * C, (i + 1) * C)
            cs_, sn_ = cs_ref[rs, :], sn_ref[rs, :]
            q = _rot(q_ref[rs, :], cs_, sn_)
            k = _rot(k_ref[rs, :], cs_, sn_) * DK ** -0.5
            _, vjp = jax.vjp(lambda q, k, v, st: _ret_chunk(q, k, v, st, lg), q, k, v_ref[rs, :], st_ref[0, i])
            dq, dk, dv, dstate = vjp((do_ref[rs, :], dstate))
            dq_ref[rs, :] = _rot_t(dq, cs_, sn_).astype(BF16)
            dk_ref[rs, :] = _rot_t(dk * DK ** -0.5, cs_, sn_).astype(BF16)
            dv_ref[rs, :] = dv.astype(BF16)
        dstate_ref[...] = dstate

    rev = lambda c: nb - 1 - c
    in_specs = _ret_specs(r, rev) + [pl.BlockSpec((1, CPB, DK, DV), lambda h, c: (h, rev(c), 0, 0)),
                                     pl.BlockSpec((r, DV), lambda h, c: (rev(c), h))]
    out_specs = [pl.BlockSpec((r, DK), lambda h, c: (rev(c), h)), pl.BlockSpec((r, DK), lambda h, c: (rev(c), h)),
                 pl.BlockSpec((r, DV), lambda h, c: (rev(c), h))]
    dq, dk, dv = pl.pallas_call(
        body, grid=(H, nb), in_specs=in_specs, out_specs=out_specs,
        out_shape=[_sds((s, H * DK), BF16), _sds((s, H * DK), BF16), _sds((s, H * DV), BF16)],
        scratch_shapes=[pltpu.VMEM((DK, DV), F32)], name="ret_scan_bwd",
        compiler_params=_params(("parallel", "arbitrary")))(proj, proj, proj, cs, sn, lgtab, states, do)
    return dq, dk, dv


def _merge(oa, z, ob, rg, ga, gb, wa, wb):
    ya = oa * lax.rsqrt(jnp.mean(oa * oa, axis=-1, keepdims=True) + EPS) * wa * _silu(z)
    mu = jnp.mean(ob, axis=-1, keepdims=True)
    var = jnp.mean(jnp.square(ob - mu), axis=-1, keepdims=True)
    yb = (ob - mu) * lax.rsqrt(var + EPS) * wb * _silu(rg)
    return jax.nn.sigmoid(ga) * ya + jax.nn.sigmoid(gb) * yb


def _merge_specs(ts):
    blk = lambda off: pl.BlockSpec((ts, DV), lambda h, i: (i, off // DV + h))
    return [blk(0), blk(O_Z), blk(0), blk(O_RG), blk(O_GA), blk(O_GB),
            pl.BlockSpec((1, DV), lambda h, i: (0, 0)), pl.BlockSpec((1, DV), lambda h, i: (0, h))]


def _merge_fwd(oa, ob, proj, wa, wb, ts):
    s = oa.shape[0]

    def body(oa_ref, z_ref, ob_ref, rg_ref, ga_ref, gb_ref, wa_ref, wb_ref, o_ref, ot_ref):
        y = _merge(oa_ref[...], z_ref[...], ob_ref[...], rg_ref[...], ga_ref[...], gb_ref[...],
                   wa_ref[...], wb_ref[...]).astype(BF16)
        o_ref[...] = y
        ot_ref[...] = y.T

    return pl.pallas_call(body, grid=(H, s // ts), in_specs=_merge_specs(ts),
                          out_specs=[pl.BlockSpec((ts, DV), lambda h, i: (i, h)), pl.BlockSpec((DV, ts), lambda h, i: (h, i))],
                          out_shape=[_sds((s, H * DV), BF16), _sds((H * DV, s), BF16)], name="merge_fwd",
                          compiler_params=_params(("parallel", "parallel")))(oa, proj, ob, proj, proj, proj, wa, wb)


def _merge_bwd(oa, ob, proj, wa, wb, dmixed, ts):
    s = oa.shape[0]

    def body(oa_ref, z_ref, ob_ref, rg_ref, ga_ref, gb_ref, wa_ref, wb_ref, dm_ref,
             doa_ref, dob_ref, dz_ref, drg_ref, dga_ref, dgb_ref, dwa_ref, dwb_ref):
        _, vjp = jax.vjp(_merge, oa_ref[...], z_ref[...], ob_ref[...], rg_ref[...], ga_ref[...], gb_ref[...],
                         wa_ref[...], wb_ref[...])
        doa, dz, dob, drg, dga, dgb, dwa, dwb = vjp(dm_ref[...].astype(F32))
        doa_ref[...] = doa
        dob_ref[...] = dob
        dz_ref[...] = dz.astype(BF16)
        drg_ref[...] = drg.astype(BF16)
        dga_ref[...] = dga.astype(BF16)
        dgb_ref[...] = dgb.astype(BF16)
        first_tile = pl.program_id(1) == 0

        @pl.when(first_tile & (pl.program_id(0) == 0))
        def _():
            dwa_ref[...] = jnp.zeros_like(dwa_ref)

        @pl.when(first_tile)
        def _():
            dwb_ref[...] = jnp.zeros_like(dwb_ref)

        dwa_ref[...] += dwa
        dwb_ref[...] += dwb

    blk = pl.BlockSpec((ts, DV), lambda h, i: (i, h))
    out_specs = [blk] * 6 + [pl.BlockSpec((1, DV), lambda h, i: (0, 0)), pl.BlockSpec((1, DV), lambda h, i: (0, h))]
    out_shape = [_sds((s, H * DV)), _sds((s, H * DV))] + [_sds((s, H * DV), BF16)] * 4 + [_sds((1, DV)), _sds((1, H * DV))]
    return pl.pallas_call(body, grid=(H, s // ts), in_specs=_merge_specs(ts) + [blk], out_specs=out_specs, out_shape=out_shape,
                          name="merge_bwd", compiler_params=_params(("arbitrary", "arbitrary"), 40 * ts * DV * 4))(
                              oa, proj, ob, proj, proj, proj, wa, wb, dmixed)


def _act(hg, hu):
    return _silu(hg) * hu


def _act_fwd(hg, hu, ts, tc):
    s, f = hg.shape
    blk = pl.BlockSpec((ts, tc), lambda i, j: (i, j))

    def body(hg_ref, hu_ref, o_ref, ot_ref):
        y = _act(hg_ref[...], hu_ref[...]).astype(BF16)
        o_ref[...] = y
        ot_ref[...] = y.T

    return pl.pallas_call(body, grid=(s // ts, f // tc), in_specs=[blk, blk],
                          out_specs=[blk, pl.BlockSpec((tc, ts), lambda i, j: (j, i))],
                          out_shape=[_sds((s, f), BF16), _sds((f, s), BF16)],
                          name="swiglu_fwd", compiler_params=_params(("parallel", "parallel")))(hg, hu)


def _act_bwd(hg, hu, dact, ts, tc):
    s, f = hg.shape
    blk = pl.BlockSpec((ts, tc), lambda i, j: (i, j))

    def body(hg_ref, hu_ref, d_ref, dhg_ref, dhu_ref):
        _, vjp = jax.vjp(_act, hg_ref[...], hu_ref[...])
        dhg, dhu = vjp(d_ref[...].astype(F32))
        dhg_ref[...] = dhg.astype(BF16)
        dhu_ref[...] = dhu.astype(BF16)

    return pl.pallas_call(body, grid=(s // ts, f // tc), in_specs=[blk, blk, blk], out_specs=[blk, blk],
                          out_shape=[_sds((s, f), BF16), _sds((s, f), BF16)], name="swiglu_bwd",
                          compiler_params=_params(("parallel", "parallel")))(hg, hu, dact)


def _loss_rows(h2, wf, tgt):
    err = _rms(h2, wf) - tgt
    return 0.5 * jnp.sum(jnp.mean(err * err, axis=-1, keepdims=True), keepdims=True)


def _loss_fwd_bwd(h2, wf, tgt, ts):
    s = h2.shape[0]

    def body(h_ref, w_ref, t_ref, loss_ref, dh_ref, dw_ref):
        loss, vjp = jax.vjp(_loss_rows, h_ref[...], w_ref[...], t_ref[...])
        dh, dw, _ = vjp(jnp.ones((1, 1), F32))
        dh_ref[...] = dh

        @pl.when(pl.program_id(0) == 0)
        def _():
            loss_ref[...] = jnp.zeros_like(loss_ref)
            dw_ref[...] = jnp.zeros_like(dw_ref)

        loss_ref[...] += loss
        dw_ref[...] += dw

    row = pl.BlockSpec((ts, D), lambda i: (i, 0))
    vec = pl.BlockSpec((1, D), lambda i: (0, 0))
    tile = pl.BlockSpec((8, 128), lambda i: (0, 0))
    return pl.pallas_call(body, grid=(s // ts,), in_specs=[row, vec, row], out_specs=[tile, row, vec],
                          out_shape=[_sds((8, 128)), _sds((s, D)), _sds((1, D))], name="final_norm_loss",
                          compiler_params=_params(("arbitrary",), 12 * ts * D * 4))(h2, wf, tgt)


def _rope_tables(s):
    inv = ROPE_BASE ** (-jnp.arange(0, DK, 2, dtype=F32) / DK)
    ang = jnp.arange(s, dtype=F32)[:, None] * inv[None, :]
    cos, sin = jnp.cos(ang), jnp.sin(ang)
    return jnp.concatenate([cos, cos], axis=1), jnp.concatenate([-sin, sin], axis=1)


def _local_step(x, tgt, w_in, w_out, w_gate, w_up, w_down, norm1_w, conv_w, a_log, dt_bias, gdn_norm_w, ret_norm_w, norm2_w, norm_f_w,
                dist=None):
    s = x.shape[0]
    ts = min(512, s)
    cs, sn = _rope_tables(s)
    lgtab = jnp.log1p(-jnp.exp2(-5.0 - jnp.arange(H, dtype=F32))).reshape(1, H)

    u, u_t = _rms_fwd(x, norm1_w, ts, "norm1_fwd")
    if dist is None:
        proj = _matmul(u, w_in, tm=1024, tn=1280, tk=D, name="in_proj")
    else:
        proj, gathered = _matmul(u, w_in, tm=1024, tn=1280, tk=D, side=_gather_side(dist["shards"]), name="in_proj")
        w_out, w_gate, w_up, w_down = (gathered[0].reshape(D, D), _from_slots_cols(gathered[1]), _from_slots_cols(gathered[2]),
                                       gathered[3].reshape(-1, D))
    qk, va = _gdn_qkv_fwd(proj, conv_w, ts)
    beta, gc, gl = _bg_fwd(proj, a_log, dt_bias, ts)
    inter = _gdn_prep_fwd(qk, va, beta, gc, gl)
    oa, st_a = _gdn_scan_fwd(*inter)
    ob, st_b = _ret_scan_fwd(proj, cs, sn, lgtab)
    mixed, mixed_t = _merge_fwd(oa, ob, proj, gdn_norm_w, ret_norm_w, ts)
    h1 = _matmul(mixed, w_out, tm=1024, tn=1024, tk=D, res=x, name="out_proj")
    hn, hn_t = _rms_fwd(h1, norm2_w, ts, "norm2_fwd")
    hg = _matmul(hn, w_gate, tm=1024, tn=1408, tk=D, name="ffn_gate")
    hu = _matmul(hn, w_up, tm=1024, tn=1408, tk=D, name="ffn_up")
    act, act_t = _act_fwd(hg, hu, min(256, s), 1408)
    h2 = _matmul(act, w_down, tm=1024, tn=1024, tk=1408, res=h1, name="ffn_down")
    loss, dh2, d_norm_f = _loss_fwd_bwd(h2, norm_f_w, tgt, ts)

    dact = _matmul(dh2, w_down, tb=True, tm=1024, tn=1408, tk=D, out_dtype=BF16, name="ffn_down_dx")
    g_down = _matmul(act_t, dh2, tm=1408, tn=1024, tk=D, out_dtype=BF16, name="ffn_down_dw")
    dhg, dhu = _act_bwd(hg, hu, dact, min(256, s), 1408)
    g_gate = _matmul(hn_t, dhg, tm=1024, tn=1408, tk=D, out_dtype=BF16, name="ffn_gate_dw")
    g_up = _matmul(hn_t, dhu, tm=1024, tn=1408, tk=D, out_dtype=BF16, name="ffn_up_dw")
    dhn = _matmul(dhg, w_gate, tb=True, tm=1024, tn=1024, tk=1408, name="ffn_gate_dx")
    dhn = _matmul(dhu, w_up, tb=True, tm=1024, tn=1024, tk=1408, res=dhn, name="ffn_up_dx")
    dh1, d_norm2 = _rms_bwd(h1, norm2_w, dhn, dh2, ts, "norm2_bwd")

    g_out = _matmul(mixed_t, dh1, tm=1024, tn=1024, tk=D, out_dtype=BF16, name="out_proj_dw")
    early = ["w_out", "w_gate", "w_up", "w_down"]
    if dist is None:
        dmixed = _matmul(dh1, w_out, tb=True, tm=1024, tn=1024, tk=D, out_dtype=BF16, name="out_proj_dx")
    else:
        slots = dict(w_out=g_out.reshape(NDEV, D // NDEV, D), w_gate=_to_slots_cols(g_gate), w_up=_to_slots_cols(g_up),
                     w_down=g_down.reshape(NDEV, -1, D))
        dmixed, from_sibling = _matmul(dh1, w_out, tb=True, tm=1024, tn=1024, tk=D, out_dtype=BF16,
                                       side=_sibling_side([slots[k] for k in early]), name="out_proj_dx")
        parts = [_add_sibling(slots[k], r, dist["core"], 128, "grads_add_" + k) for k, r in zip(early, from_sibling)]
    doa, dob, dz, drg, dga, dgb, d_gdn_norm, d_ret_norm = _merge_bwd(oa, ob, proj, gdn_norm_w, ret_norm_w, dmixed, ts)

    drq, drk, drv = _ret_scan_bwd(proj, cs, sn, lgtab, st_b, dob)
    d_inter = _gdn_scan_bwd(*inter, st_a, doa)
    dqk, dva, dbeta_h, dgc_h, dgl_h = _gdn_prep_bwd(qk, va, beta, gc, gl, *d_inter)
    da_qk, da_v, d_conv = _gdn_qkv_bwd(proj, conv_w, dqk, dva, ts)
    dba, d_a_log, d_dt_bias = _bg_bwd(proj, a_log, dt_bias, dbeta_h, dgc_h, dgl_h, ts)

    dproj = jnp.concatenate([da_qk, da_v, dz, dba, drq, drk, drv, drg, dga, dgb], axis=1)
    if dist is None:
        g_in = _matmul(u_t, dproj, tm=1024, tn=1280, tk=D, out_dtype=BF16, name="in_proj_dw")
        du = _matmul(dproj, w_in, tb=True, tm=1024, tn=1024, tk=1664, name="in_proj_dx")
        big = dict(w_in=g_in, w_out=g_out, w_gate=g_gate, w_up=g_up, w_down=g_down)
    else:
        g_in, from_chips = _matmul(u_t, dproj, tm=1024, tn=1280, tk=D, out_dtype=BF16, side=_chips_side(parts), name="in_proj_dw")
        du, (from_all,) = _matmul(dproj, w_in, tb=True, tm=1024, tn=1024, tk=1664,
                                  side=_all_to_all_side(_to_slots_cols(_ungroup_w_in(g_in))), name="in_proj_dx")
        big = dict(w_in=from_all, **{k: (p, r) for k, p, r in zip(early, parts, from_chips)})
    dx, d_norm1 = _rms_bwd(x, norm1_w, du, dh1, ts, "norm1_bwd")

    small = dict(norm1_w=d_norm1, conv_w=d_conv, a_log=d_a_log, dt_bias=d_dt_bias, gdn_norm_w=d_gdn_norm,
                 ret_norm_w=d_ret_norm, norm2_w=d_norm2, norm_f_w=d_norm_f)
    return loss, dx, big, small


def _coords():
    return lax.axis_index("x"), lax.axis_index("y"), lax.axis_index("c")


def _gather_side(shards):
    n = len(shards)

    def plan(ins, outs, send_sems, recv_sems, local_sems):
        x, y, c = _coords()
        me, sibling = (x, y, c), (x, y, 1 - c)
        chips = [(1 - x, y), (x, 1 - y), (1 - x, 1 - y)]

        def copy(a, k, block, to, src=None):
            px, py, pc = block
            dst = outs[a].at[4 * px + 2 * py + pc]
            return pltpu.make_async_remote_copy(src_ref=dst if src is None else src, dst_ref=dst, send_sem=send_sems.at[a, k],
                                                recv_sem=recv_sems.at[a, k], device_id=to, device_id_type=MESH)

        mine = [pltpu.make_async_copy(ins[a], outs[a].at[4 * x + 2 * y + c], local_sems.at[a]) for a in range(n)]
        first = []
        for a in range(n):
            first.append(copy(a, 0, me, sibling, src=ins[a]))
            first += [copy(a, 1 + j, me, (*chip, c), src=ins[a]) for j, chip in enumerate(chips)]
        return c, me, sibling, chips, copy, mine, first

    def start(ins, outs, *sems):
        *_, mine, first = plan(ins, outs, *sems)
        for cp in mine + first:
            cp.start()

    def finish(ins, outs, *sems):
        c, me, sibling, chips, copy, mine, first = plan(ins, outs, *sems)
        passed = []
        for j, chip in enumerate(chips):
            for a in range(n):
                copy(a, 1 + j, (*chip, c), me).wait_recv()
                fwd = copy(a, 4 + j, (*chip, c), sibling)
                fwd.start()
                passed.append(fwd)
        for a in range(n):
            copy(a, 0, sibling, me).wait_recv()
            for j, chip in enumerate(chips):
                copy(a, 4 + j, (*chip, 1 - c), me).wait_recv()
        for cp in first + passed:
            cp.wait_send()
        for cp in mine:
            cp.wait()

    return _Side(shards, [_sds((NDEV,) + a.shape, a.dtype) for a in shards], [(n, 7), (n, 7), (n,)], start, finish)


def _exchange_side(ins, n_out, copies_of):
    def start(in_refs, out_refs, *sems):
        for cp in copies_of(in_refs, out_refs, *sems):
            cp.start()

    def finish(in_refs, out_refs, *sems):
        for cp in copies_of(in_refs, out_refs, *sems):
            cp.wait()

    n = len(ins)
    return _Side(ins, [_sds((n_out,) + a.shape[1:], a.dtype) for a in ins], [(n, n_out), (n, n_out)], start, finish)


def _sibling_side(slots):
    def copies_of(ins, outs, send_sems, recv_sems):
        x, y, c = _coords()
        return [pltpu.make_async_remote_copy(
            src_ref=ins[a].at[2 * j + (1 - c)], dst_ref=outs[a].at[j], send_sem=send_sems.at[a, j], recv_sem=recv_sems.at[a, j],
            device_id=(x, y, 1 - c), device_id_type=MESH) for a in range(len(slots)) for j in range(4)]

    return _exchange_side(slots, 4, copies_of)


def _chips_side(parts):
    def copies_of(ins, outs, send_sems, recv_sems):
        x, y, c = _coords()
        chips = [(1 - x, y), (x, 1 - y), (1 - x, 1 - y)]
        return [pltpu.make_async_remote_copy(
            src_ref=ins[a].at[2 * px + py], dst_ref=outs[a].at[k], send_sem=send_sems.at[a, k], recv_sem=recv_sems.at[a, k],
            device_id=(px, py, c), device_id_type=MESH) for a in range(len(parts)) for k, (px, py) in enumerate(chips)]

    return _exchange_side(parts, 3, copies_of)


def _all_to_all_side(slots):
    def plan(ins, outs, send_sems, recv_sems, local_sems):
        x, y, c = _coords()
        mine = 4 * x + 2 * y + c
        own = pltpu.make_async_copy(ins[0].at[mine], outs[0].at[mine], local_sems.at[0])
        remote = []
        for r in range(1, NDEV):
            peer = (x ^ (r >> 2), y ^ ((r >> 1) & 1), c ^ (r & 1))
            remote.append(pltpu.make_async_remote_copy(
                src_ref=ins[0].at[mine ^ r], dst_ref=outs[0].at[mine], send_sem=send_sems.at[r - 1], recv_sem=recv_sems.at[r - 1],
                device_id=peer, device_id_type=MESH))
        return own, remote

    def start(ins, outs, *sems):
        own, remote = plan(ins, outs, *sems)
        for cp in [own] + remote:
            cp.start()

    def finish(ins, outs, *sems):
        own, remote = plan(ins, outs, *sems)
        for cp in remote:
            cp.wait()
        own.wait()

    return _Side([slots], [_sds(slots.shape, slots.dtype)], [(NDEV - 1,), (NDEV - 1,), (1,)], start, finish)


def _allreduce_small(pack, name):
    rows, cols = pack.shape

    def body(in_ref, out_ref, buf_ref, send_sems, recv_sems):
        x, y, c = _coords()
        mine = 4 * x + 2 * y + c
        buf_ref[mine] = in_ref[...]
        copies = []
        for r in range(1, NDEV):
            peer = (x ^ (r >> 2), y ^ ((r >> 1) & 1), c ^ (r & 1))
            copies.append(pltpu.make_async_remote_copy(
                src_ref=in_ref, dst_ref=buf_ref.at[mine], send_sem=send_sems.at[r - 1], recv_sem=recv_sems.at[r - 1],
                device_id=peer, device_id_type=MESH))
        for cp in copies:
            cp.start()
        for r in range(1, NDEV):
            pltpu.make_async_remote_copy(
                src_ref=in_ref, dst_ref=buf_ref.at[mine ^ r], send_sem=send_sems.at[r - 1], recv_sem=recv_sems.at[r - 1],
                device_id=(x, y, c), device_id_type=MESH).wait_recv()
        for cp in copies:
            cp.wait_send()
        acc = buf_ref[0]
        for d in range(1, NDEV):
            acc = acc + buf_ref[d]
        out_ref[...] = acc

    return pl.pallas_call(
        body, in_specs=[VMEM_FULL], out_specs=VMEM_FULL, out_shape=_sds((rows, cols)),
        scratch_shapes=[pltpu.VMEM((NDEV, rows, cols), F32), pltpu.SemaphoreType.DMA((NDEV - 1,)), pltpu.SemaphoreType.DMA((NDEV - 1,))],
        name=name)(pack)


def _add_sibling(slots, recv, core, tr, name):
    _, rows, cols = slots.shape
    tr = _row_tile(rows, tr)

    def body(c_ref, a_ref, b_ref, o_ref):
        o_ref[...] = (a_ref[...].astype(F32) + b_ref[...].astype(F32)).astype(BF16)

    gs = pltpu.PrefetchScalarGridSpec(
        num_scalar_prefetch=1, grid=(4, rows // tr),
        in_specs=[pl.BlockSpec((None, tr, cols), lambda j, i, cr: (2 * j + cr[0], i, 0)),
                  pl.BlockSpec((None, tr, cols), lambda j, i, cr: (j, i, 0))],
        out_specs=pl.BlockSpec((None, tr, cols), lambda j, i, cr: (j, i, 0)))
    return pl.pallas_call(body, grid_spec=gs, out_shape=_sds((4, rows, cols), BF16), name=name,
                          compiler_params=_params(("parallel", "parallel"), 6 * tr * cols * 4))(core, slots, recv)


def _adam_math(w, g, m, v):
    m2 = B1 * m + (1.0 - B1) * g
    v2 = B2 * v + (1.0 - B2) * jnp.square(g)
    m_hat = m2 / (1.0 - B1 ** STEP)
    v_hat = v2 / (1.0 - B2 ** STEP)
    return -LR * (m_hat / (jnp.sqrt(v_hat) + EPS_ADAM) + WD * w), m2, v2


def _adamw_reduced(part, recv, chip, w, m, v, tr, name):
    rows, cols = w.shape
    tr = _row_tile(rows, tr)

    def body(j_ref, p_ref, r0_ref, r1_ref, r2_ref, w_ref, m_ref, v_ref, g_ref, d_ref, nm_ref, nv_ref):
        g = p_ref[...].astype(F32) + r0_ref[...].astype(F32) + r1_ref[...].astype(F32) + r2_ref[...].astype(F32)
        d, m2, v2 = _adam_math(w_ref[...], g, m_ref[...], v_ref[...])
        g_ref[...] = g
        d_ref[...] = d
        nm_ref[...] = m2
        nv_ref[...] = v2

    flat = pl.BlockSpec((tr, cols), lambda i, jr: (i, 0))
    gs = pltpu.PrefetchScalarGridSpec(
        num_scalar_prefetch=1, grid=(rows // tr,),
        in_specs=[pl.BlockSpec((None, tr, cols), lambda i, jr: (jr[0], i, 0))]
        + [pl.BlockSpec((None, tr, cols), functools.partial(lambda i, jr, k: (k, i, 0), k=k)) for k in range(3)] + [flat] * 3,
        out_specs=[flat] * 4)
    return pl.pallas_call(body, grid_spec=gs, out_shape=[_sds((rows, cols))] * 4, name=name,
                          compiler_params=_params(("parallel",), 22 * tr * cols * 4))(chip, part, recv, recv, recv, w, m, v)


def _adamw_summed(recv, w, m, v, tr, name):
    rows, cols = w.shape
    tr = _row_tile(rows, tr)

    def body(*refs):
        parts, (w_ref, m_ref, v_ref), (g_ref, d_ref, nm_ref, nv_ref) = refs[:NDEV], refs[NDEV:NDEV + 3], refs[NDEV + 3:]
        g = parts[0][...].astype(F32)
        for p_ref in parts[1:]:
            g = g + p_ref[...].astype(F32)
        d, m2, v2 = _adam_math(w_ref[...], g, m_ref[...], v_ref[...])
        g_ref[...] = g
        d_ref[...] = d
        nm_ref[...] = m2
        nv_ref[...] = v2

    flat = pl.BlockSpec((tr, cols), lambda i: (i, 0))
    slot = [pl.BlockSpec((None, tr, cols), functools.partial(lambda i, k: (k, i, 0), k=k)) for k in range(NDEV)]
    return pl.pallas_call(body, grid=(rows // tr,), in_specs=slot + [flat] * 3, out_specs=[flat] * 4,
                          out_shape=[_sds((rows, cols))] * 4, name=name,
                          compiler_params=_params(("parallel",), 24 * tr * cols * 4))(*([recv] * NDEV), w, m, v)


def _adamw_plain(w, g, m, v, name):
    def body(w_ref, g_ref, m_ref, v_ref, d_ref, nm_ref, nv_ref):
        d, m2, v2 = _adam_math(w_ref[...], g_ref[...], m_ref[...], v_ref[...])
        d_ref[...] = d
        nm_ref[...] = m2
        nv_ref[...] = v2

    return pl.pallas_call(body, out_shape=[_sds(w.shape)] * 3, name=name)(w, g, m, v)


def _pack_small(norm1_w, conv_w, a_log, dt_bias, gdn_norm_w, ret_norm_w, norm2_w, norm_f_w):
    misc = jnp.concatenate([gdn_norm_w.reshape(1, DV), a_log.reshape(1, H), dt_bias.reshape(1, H),
                            jnp.zeros((1, D - DV - 2 * H), F32)], axis=1)
    return jnp.concatenate([norm1_w.reshape(1, D), ret_norm_w.reshape(1, D), norm2_w.reshape(1, D), norm_f_w.reshape(1, D),
                            conv_w.reshape(8, D), misc, jnp.zeros((3, D), F32)], axis=0)


def _unpack_small(pack):
    return dict(norm1_w=pack[0:1], ret_norm_w=pack[1:2], norm2_w=pack[2:3], norm_f_w=pack[3], conv_w=pack[4:12].reshape(4, 2 * D),
                gdn_norm_w=pack[12:13, 0:DV], a_log=pack[12:13, DV:DV + H], dt_bias=pack[12:13, DV + H:DV + 2 * H])


def _regroup_w_in(w):
    return jnp.concatenate([w[:, :BA_END], jnp.zeros((w.shape[0], P_IN - N_IN), w.dtype), w[:, BA_END:]], axis=1)


def _ungroup_w_in(g):
    return jnp.concatenate([g[:, :BA_END], g[:, BA_END + P_IN - N_IN:]], axis=1)


def _to_slots_cols(g):
    rows, cols = g.shape
    return g.reshape(rows, NDEV, cols // NDEV).transpose(1, 0, 2)


def _from_slots_cols(a):
    n, rows, cols = a.shape
    return a.transpose(1, 0, 2).reshape(rows, n * cols)


WEIGHT_ORDER = ["norm1_w", "w_in", "conv_w", "a_log", "dt_bias", "gdn_norm_w", "ret_norm_w", "w_out", "norm2_w", "w_gate", "w_up",
                "w_down", "norm_f_w"]


def kernel(x, norm1_w, w_in, conv_w, a_log, dt_bias, gdn_norm_w, ret_norm_w, w_out, norm2_w, w_gate, w_up, w_down, norm_f_w, loss_target, m_norm1_w, m_w_in, m_conv_w, m_a_log, m_dt_bias, m_gdn_norm_w, m_ret_norm_w, m_w_out, m_norm2_w, m_w_gate, m_w_up, m_w_down, m_norm_f_w, v_norm1_w, v_w_in, v_conv_w, v_a_log, v_dt_bias, v_gdn_norm_w, v_ret_norm_w, v_w_out, v_norm2_w, v_w_gate, v_w_up, v_w_down, v_norm_f_w):
    ax, ay, ac = _coords()
    me = 4 * ax + 2 * ay + ac
    core = jnp.reshape(ac, (1,)).astype(jnp.int32)
    chip = jnp.reshape(2 * ax + ay, (1,)).astype(jnp.int32)
    w = dict(norm1_w=norm1_w, w_in=w_in[0], conv_w=conv_w[0], a_log=a_log, dt_bias=dt_bias, gdn_norm_w=gdn_norm_w,
             ret_norm_w=ret_norm_w, w_out=w_out[0], norm2_w=norm2_w, w_gate=w_gate[0], w_up=w_up[0], w_down=w_down[0],
             norm_f_w=norm_f_w)
    m = dict(norm1_w=m_norm1_w, w_in=m_w_in[0], conv_w=m_conv_w[0], a_log=m_a_log, dt_bias=m_dt_bias, gdn_norm_w=m_gdn_norm_w,
             ret_norm_w=m_ret_norm_w, w_out=m_w_out[0], norm2_w=m_norm2_w, w_gate=m_w_gate[0], w_up=m_w_up[0], w_down=m_w_down[0],
             norm_f_w=m_norm_f_w)
    v = dict(norm1_w=v_norm1_w, w_in=v_w_in[0], conv_w=v_conv_w[0], a_log=v_a_log, dt_bias=v_dt_bias, gdn_norm_w=v_gdn_norm_w,
             ret_norm_w=v_ret_norm_w, w_out=v_w_out[0], norm2_w=v_norm2_w, w_gate=v_w_gate[0], w_up=v_w_up[0], w_down=v_w_down[0],
             norm_f_w=v_norm_f_w)
    big_names = ["w_in", "w_out", "w_gate", "w_up", "w_down"]

    w_in_all, conv_all = _run_side(_gather_side([w["w_in"].astype(BF16), w["conv_w"]]), "w_in_allgather")
    w_in_full = _regroup_w_in(_from_slots_cols(w_in_all))
    conv_full = _from_slots_cols(conv_all)
    dist = dict(core=core, shards=[w[k].astype(BF16) for k in ("w_out", "w_gate", "w_up", "w_down")])

    loss_tile, dx, big, small = _local_step(
        x[0], loss_target[0], w_in_full, None, None, None, None, norm1_w, conv_full, a_log, dt_bias,
        gdn_norm_w, ret_norm_w, norm2_w, norm_f_w.reshape(1, D), dist=dist)
    loss = lax.psum(loss_tile[0, 0], ("x", "y", "c"))

    out = {"w_in": _adamw_summed(big["w_in"], w["w_in"], m["w_in"], v["w_in"], 64, "adamw_w_in")}
    for k in ("w_out", "w_gate", "w_up", "w_down"):
        part, recv = big[k]
        out[k] = _adamw_reduced(part, recv, chip, w[k], m[k], v[k], 128, "adamw_" + k)

    g_small = _unpack_small(_allreduce_small(_pack_small(**small), "small_grads_allreduce"))
    g_small["conv_w"] = lax.dynamic_slice_in_dim(g_small["conv_w"], me * (2 * D // NDEV), 2 * D // NDEV, axis=1)
    small_names = [k for k in WEIGHT_ORDER if k not in big_names]
    pad_conv = lambda a: jnp.pad(a, ((0, 0), (0, 2 * D - a.shape[1])))
    packs = []
    for src in (w, g_small, m, v):
        args = {k: (pad_conv(src[k]) if k == "conv_w" else src[k]) for k in small_names}
        packs.append(_pack_small(**args))
    d_pack, m_pack, v_pack = _adamw_plain(*packs[0:1], packs[1], packs[2], packs[3], name="adamw_small")
    cut_conv = lambda dct: {**dct, "conv_w": dct["conv_w"][:, :2 * D // NDEV]}
    d_small, m_small, v_small = (cut_conv(_unpack_small(p)) for p in (d_pack, m_pack, v_pack))

    def shaped(k, a):
        return a.reshape(w_shapes[k])

    w_shapes = dict(norm1_w=norm1_w.shape, w_in=w_in.shape, conv_w=conv_w.shape, a_log=a_log.shape, dt_bias=dt_bias.shape,
                    gdn_norm_w=gdn_norm_w.shape, ret_norm_w=ret_norm_w.shape, w_out=w_out.shape, norm2_w=norm2_w.shape,
                    w_gate=w_gate.shape, w_up=w_up.shape, w_down=w_down.shape, norm_f_w=norm_f_w.shape)
    grads, deltas, new_m, new_v = [], [], [], []
    for k in WEIGHT_ORDER:
        if k in big_names:
            g_, d_, m_, v_ = out[k]
        else:
            g_, d_, m_, v_ = g_small[k], d_small[k], m_small[k], v_small[k]
        grads.append(shaped(k, g_))
        deltas.append(shaped(k, d_))
        new_m.append(shaped(k, m_))
        new_v.append(shaped(k, v_))
    return (loss, dx[None], *grads, *deltas, *new_m, *new_v)
```

```python
import functools
import numpy as np
import jax
import jax.numpy as jnp
from jax import lax
from jax.experimental import pallas as pl
from jax.experimental.pallas import tpu as pltpu

F32, BF16 = jnp.float32, jnp.bfloat16
HI = lax.Precision.HIGHEST
MESH = pl.DeviceIdType.MESH
ANY = pl.BlockSpec(memory_space=pl.ANY)
VMEM_FULL = pl.BlockSpec(memory_space=pltpu.VMEM)

NDEV = 8
D = 2048
H = 8
DK = 128
DV = 256
C = 64
CPB = 4
EPS = 1e-6
ROPE_BASE = 10000.0
N_IN = 16400
O_MERGE, O_QKV, O_RET, O_BA, P_IN = 0, 8192, 12288, 16384, 16640
MERGE_W, RET_W = 4 * DV, 2 * DK + DV
LR, B1, B2, EPS_ADAM, WD, STEP = 0.001, 0.9, 0.999, 1e-08, 0.01, 10
VMEM_CAP = 60 * 1024 * 1024

NN = ((1,), (0,))
NT = ((1,), (1,))
TN = ((0,), (0,))


def _params(sem=None, est=None):
    kw = {}
    if sem is not None:
        kw["dimension_semantics"] = sem
    if est is not None:
        kw["vmem_limit_bytes"] = int(min(VMEM_CAP, max(32 * 1024 * 1024, est * 5 // 4 + (4 << 20))))
    return pltpu.CompilerParams(**kw)


def _sds(shape, dt=F32):
    return jax.ShapeDtypeStruct(tuple(shape), dt)


def _row_tile(rows, limit):
    return max(t for t in range(16, min(rows, limit) + 1, 16) if rows % t == 0)


def _bdot(a, b, dims):
    return lax.dot_general(a.astype(BF16), b.astype(BF16), (dims, ((), ())), preferred_element_type=F32)


def _hdot(a, b, dims):
    return lax.dot_general(a, b, (dims, ((), ())), precision=HI, preferred_element_type=F32)


def _silu(x):
    return x * jax.nn.sigmoid(x)


def _rms(x, w):
    return x * lax.rsqrt(jnp.mean(x * x, axis=-1, keepdims=True) + EPS) * w


class _Side:
    def __init__(self, ins, out_shapes, sems, start, finish):
        self.ins, self.out_shapes, self.sems, self.start, self.finish = list(ins), list(out_shapes), list(sems), start, finish


def _run_side(side, name):
    ni, no = len(side.ins), len(side.out_shapes)

    def body(*refs):
        ins, outs, sems = refs[:ni], refs[ni:ni + no], refs[ni + no:]
        side.start(ins, outs, *sems)
        side.finish(ins, outs, *sems)

    return pl.pallas_call(body, in_specs=[ANY] * ni, out_specs=[ANY] * no, out_shape=side.out_shapes,
                          scratch_shapes=[pltpu.SemaphoreType.DMA(s) for s in side.sems], name=name)(*side.ins)


def _matmul(a, b, *, ta=False, tb=False, tm, tn, tk, out_dtype=F32, res=None, side=None, name):
    m = a.shape[1] if ta else a.shape[0]
    k = a.shape[0] if ta else a.shape[1]
    n = b.shape[0] if tb else b.shape[1]
    assert k == (b.shape[1] if tb else b.shape[0])
    tm, tn, tk = min(tm, m), min(tn, n), min(tk, k)
    assert m % tm == 0 and n % tn == 0 and k % tk == 0, (name, m, n, k, tm, tn, tk)
    nk = k // tk
    dims = ((0 if ta else 1,), (1 if tb else 0,))
    has_res = res is not None
    n_in = 3 if has_res else 2
    n_side_in = len(side.ins) if side else 0
    n_side_out = len(side.out_shapes) if side else 0
    grid = (m // tm, n // tn, nk)

    def body(*refs):
        a_ref, b_ref = refs[0], refs[1]
        r_ref = refs[2] if has_res else None
        o_ref = refs[n_in + n_side_in]
        if side:
            side_ins = refs[n_in:n_in + n_side_in]
            side_outs = refs[n_in + n_side_in + 1:n_in + n_side_in + 1 + n_side_out]
            side_sems = refs[len(refs) - len(side.sems):]
            step = (pl.program_id(0) * grid[1] + pl.program_id(1)) * grid[2] + pl.program_id(2)

            @pl.when(step == 0)
            def _():
                side.start(side_ins, side_outs, *side_sems)

        def finish(acc):
            if has_res:
                acc = acc + r_ref[...].astype(F32)
            o_ref[...] = acc.astype(out_dtype)

        part = _bdot(a_ref[...], b_ref[...], dims)
        if nk == 1:
            finish(part)
        else:
            acc_ref = refs[n_in + n_side_in + 1 + n_side_out]
            kk = pl.program_id(2)

            @pl.when(kk == 0)
            def _():
                acc_ref[...] = part

            @pl.when(kk > 0)
            def _():
                acc_ref[...] += part

            @pl.when(kk == nk - 1)
            def _():
                finish(acc_ref[...])

        if side:
            @pl.when(step == grid[0] * grid[1] * grid[2] - 1)
            def _():
                side.finish(side_ins, side_outs, *side_sems)

    a_spec = pl.BlockSpec((tk, tm), lambda i, j, kk: (kk, i)) if ta else pl.BlockSpec((tm, tk), lambda i, j, kk: (i, kk))
    b_spec = pl.BlockSpec((tn, tk), lambda i, j, kk: (j, kk)) if tb else pl.BlockSpec((tk, tn), lambda i, j, kk: (kk, j))
    o_spec = pl.BlockSpec((tm, tn), lambda i, j, kk: (i, j))
    in_specs = [a_spec, b_spec] + ([o_spec] if has_res else []) + [ANY] * n_side_in
    est = 2 * (tm * tk * a.dtype.itemsize + tk * tn * b.dtype.itemsize + tm * tn * jnp.dtype(out_dtype).itemsize)
    est += 2 * tm * tn * 4 * (1 if has_res else 0) + (tm * tn * 4 if nk > 1 else 0) + 2 * tm * tn * 4
    args = (a, b) + ((res,) if has_res else ()) + (tuple(side.ins) if side else ())
    scratch = ([pltpu.VMEM((tm, tn), F32)] if nk > 1 else []) + ([pltpu.SemaphoreType.DMA(s) for s in side.sems] if side else [])
    sem = ("arbitrary",) * 3 if side else ("parallel", "parallel", "arbitrary")
    out = pl.pallas_call(
        body, grid=grid, in_specs=in_specs, out_specs=[o_spec] + [ANY] * n_side_out,
        out_shape=[_sds((m, n), out_dtype)] + (side.out_shapes if side else []),
        scratch_shapes=scratch, name=name, compiler_params=_params(sem, est))(*args)
    return (out[0], out[1:]) if side else out[0]


def _rms_fwd(x, w, ts, name):
    s = x.shape[0]

    def body(x_ref, w_ref, o_ref, ot_ref):
        y = _rms(x_ref[...], w_ref[...]).astype(BF16)
        o_ref[...] = y
        ot_ref[...] = y.T

    row = pl.BlockSpec((ts, D), lambda i: (i, 0))
    return pl.pallas_call(body, grid=(s // ts,), in_specs=[row, pl.BlockSpec((1, D), lambda i: (0, 0))],
                          out_specs=[row, pl.BlockSpec((D, ts), lambda i: (0, i))],
                          out_shape=[_sds((s, D), BF16), _sds((D, s), BF16)], name=name,
                          compiler_params=_params(("parallel",)))(x, w)


def _rms_bwd(x, w, du, dres, ts, name):
    s = x.shape[0]

    def body(x_ref, w_ref, du_ref, dres_ref, dx_ref, dw_ref):
        _, vjp = jax.vjp(_rms, x_ref[...], w_ref[...])
        dx, dw = vjp(du_ref[...].astype(F32))
        dx_ref[...] = dx + dres_ref[...]

        @pl.when(pl.program_id(0) == 0)
        def _():
            dw_ref[...] = jnp.zeros_like(dw_ref)

        dw_ref[...] += dw

    row = pl.BlockSpec((ts, D), lambda i: (i, 0))
    vec = pl.BlockSpec((1, D), lambda i: (0, 0))
    return pl.pallas_call(body, grid=(s // ts,), in_specs=[row, vec, row, row], out_specs=[row, vec],
                          out_shape=[_sds((s, D)), _sds((1, D))], name=name,
                          compiler_params=_params(("arbitrary",), 12 * ts * D * 4))(x, w, du, dres)


def _conv_taps(xx, w, base, ts):
    acc = xx[base:base + ts] * w[0:1, :]
    for j in range(1, 4):
        acc = acc + xx[base + j:base + j + ts] * w[j:j + 1, :]
    return acc


def _causal_conv(prev8, cur, w, first):
    xx = jnp.concatenate([jnp.where(first, 0.0, prev8), cur], axis=0)
    return _conv_taps(xx, w, 5, cur.shape[0])


def _qk_post(c, scale):
    s = _silu(c)
    return s * lax.rsqrt(jnp.sum(s * s, axis=-1, keepdims=True) + EPS) * scale


def _conv_specs(ts, cw, col0):
    pcol = O_QKV // cw + col0
    cur = pl.BlockSpec((ts, cw), lambda j, i: (i, pcol + j))
    prev = pl.BlockSpec((8, cw), lambda j, i: (jnp.maximum(i * (ts // 8) - 1, 0), pcol + j))
    wsp = pl.BlockSpec((4, cw), lambda j, i: (0, col0 + j))
    return cur, prev, wsp


def _gdn_qkv_fwd(proj, conv_w, ts):
    s = proj.shape[0]

    def qk_body(cur_ref, prev_ref, w_ref, o_ref):
        c = _causal_conv(prev_ref[...], cur_ref[...], w_ref[...], pl.program_id(1) == 0)
        scale = jnp.where(pl.program_id(0) < H, DK ** -0.5, 1.0).astype(F32)
        o_ref[...] = _qk_post(c, scale)

    cur, prev, wsp = _conv_specs(ts, DK, 0)
    qk = pl.pallas_call(qk_body, grid=(2 * H, s // ts), in_specs=[cur, prev, wsp],
                        out_specs=pl.BlockSpec((ts, DK), lambda j, i: (i, j)), out_shape=_sds((s, 2 * H * DK)),
                        name="gdn_qk_prep", compiler_params=_params(("parallel", "parallel")))(proj, proj, conv_w)

    def v_body(cur_ref, prev_ref, w_ref, o_ref):
        o_ref[...] = _silu(_causal_conv(prev_ref[...], cur_ref[...], w_ref[...], pl.program_id(1) == 0))

    cw = 512
    cur, prev, wsp = _conv_specs(ts, cw, 2 * H * DK // cw)
    v = pl.pallas_call(v_body, grid=(H * DV // cw, s // ts), in_specs=[cur, prev, wsp],
                       out_specs=pl.BlockSpec((ts, cw), lambda j, i: (i, j)), out_shape=_sds((s, H * DV)),
                       name="gdn_v_prep", compiler_params=_params(("parallel", "parallel")))(proj, proj, conv_w)
    return qk, v


def _gdn_qkv_bwd(proj, conv_w, dqk, dv, dproj, ts):
    s = proj.shape[0]
    nt = s // ts

    def qk_body(cur_ref, prev_ref, w_ref, d_ref, o_ref):
        c = _causal_conv(prev_ref[...], cur_ref[...], w_ref[...], pl.program_id(1) == 0)
        scale = jnp.where(pl.program_id(0) < H, DK ** -0.5, 1.0).astype(F32)
        _, vjp = jax.vjp(lambda cc: _qk_post(cc, scale), c)
        o_ref[...] = vjp(d_ref[...])[0]

    cur, prev, wsp = _conv_specs(ts, DK, 0)
    dc_qk = pl.pallas_call(qk_body, grid=(2 * H, nt), in_specs=[cur, prev, wsp, pl.BlockSpec((ts, DK), lambda j, i: (i, j))],
                           out_specs=pl.BlockSpec((ts, DK), lambda j, i: (i, j)), out_shape=_sds((s, 2 * H * DK)),
                           name="gdn_qk_prep_bwd", compiler_params=_params(("parallel", "parallel")))(proj, proj, conv_w, dqk)

    def v_body(cur_ref, prev_ref, w_ref, d_ref, o_ref):
        c = _causal_conv(prev_ref[...], cur_ref[...], w_ref[...], pl.program_id(1) == 0)
        _, vjp = jax.vjp(_silu, c)
        o_ref[...] = vjp(d_ref[...])[0]

    cw = 512
    cur, prev, wsp = _conv_specs(ts, cw, 2 * H * DK // cw)
    dc_v = pl.pallas_call(v_body, grid=(H * DV // cw, nt), in_specs=[cur, prev, wsp, pl.BlockSpec((ts, cw), lambda j, i: (i, j))],
                          out_specs=pl.BlockSpec((ts, cw), lambda j, i: (i, j)), out_shape=_sds((s, H * DV)),
                          name="gdn_v_prep_bwd", compiler_params=_params(("parallel", "parallel")))(proj, proj, conv_w, dv)

    def conv_bwd(dc, dproj, col0, ncols, name):
        def body(x_ref, xprev_ref, w_ref, dc_ref, dcnext_ref, _, da_ref, dw_ref):
            i = pl.program_id(1)
            w = w_ref[...]
            dcur = dc_ref[...]
            dd = jnp.concatenate([dcur, jnp.where(i == nt - 1, 0.0, dcnext_ref[...])], axis=0)
            acc = dd[3:3 + ts] * w[0:1, :]
            for j in range(1, 4):
                acc = acc + dd[3 - j:3 - j + ts] * w[j:j + 1, :]
            da_ref[...] = acc.astype(BF16)
            xx = jnp.concatenate([jnp.where(i == 0, 0.0, xprev_ref[...]), x_ref[...]], axis=0)

            @pl.when(i == 0)
            def _():
                dw_ref[...] = jnp.zeros_like(dw_ref)

            for j in range(4):
                dw_ref[j:j + 1, :] += jnp.sum(dcur * xx[5 + j:5 + j + ts], axis=0, keepdims=True)

        cur, prev, wsp = _conv_specs(ts, cw, col0)
        dcur = pl.BlockSpec((ts, cw), lambda j, i: (i, j))
        dnext = pl.BlockSpec((8, cw), lambda j, i: (jnp.minimum((i + 1) * (ts // 8), s // 8 - 1), j))
        pcol = O_QKV // cw + col0
        return pl.pallas_call(body, grid=(ncols // cw, nt), in_specs=[cur, prev, wsp, dcur, dnext, ANY],
                              out_specs=[pl.BlockSpec((ts, cw), lambda j, i: (i, pcol + j)), pl.BlockSpec((4, cw), lambda j, i: (0, j))],
                              out_shape=[_sds(dproj.shape, BF16), _sds((4, ncols))], input_output_aliases={5: 0}, name=name,
                              compiler_params=_params(("parallel", "arbitrary")))(proj, proj, conv_w, dc, dc, dproj)

    dproj, dw_qk = conv_bwd(dc_qk, dproj, 0, 2 * H * DK, "conv_bwd_qk")
    dproj, dw_v = conv_bwd(dc_v, dproj, 2 * H * DK // cw, H * DV, "conv_bwd_v")
    return dproj, jnp.concatenate([dw_qk, dw_v], axis=1)


def _bg(b, a, alog, dtb):
    n = b.shape[0]
    g = -jnp.exp(alog) * jax.nn.softplus(a + dtb)
    row = lax.broadcasted_iota(jnp.int32, (n, n), 0)
    col = lax.broadcasted_iota(jnp.int32, (n, n), 1)
    shift = C.bit_length() - 1
    same = (row >> shift) == (col >> shift)
    return jax.nn.sigmoid(b), _hdot((same & (row >= col)).astype(F32), g, NN), _hdot(same.astype(F32), g, NN)


def _bg_fwd(proj, alog, dtb, ts):
    s = proj.shape[0]

    def body(ba_ref, alog_ref, dtb_ref, beta_ref, gc_ref, gl_ref):
        beta_ref[...], gc_ref[...], gl_ref[...] = _bg(ba_ref[:, 0:H], ba_ref[:, H:2 * H], alog_ref[...], dtb_ref[...])

    small = pl.BlockSpec((1, H), lambda i: (0, 0))
    out = pl.BlockSpec((ts, H), lambda i: (i, 0))
    return pl.pallas_call(body, grid=(s // ts,), in_specs=[pl.BlockSpec((ts, 256), lambda i: (i, O_BA // 256)), small, small],
                          out_specs=[out] * 3, out_shape=[_sds((s, H))] * 3, name="gdn_bg_prep",
                          compiler_params=_params(("parallel",)))(proj, alog, dtb)


def _bg_bwd(proj, alog, dtb, dbeta_h, dgc_h, dgl_h, dproj, ts):
    s = proj.shape[0]

    def body(ba_ref, alog_ref, dtb_ref, dbeta_ref, dgc_ref, dgl_ref, _, dba_ref, dalog_ref, ddtb_ref):
        _, vjp = jax.vjp(_bg, ba_ref[:, 0:H], ba_ref[:, H:2 * H], alog_ref[...], dtb_ref[...])
        db, da, dalog, ddtb = vjp((jnp.sum(dbeta_ref[...], axis=0), jnp.sum(dgc_ref[...], axis=0), jnp.sum(dgl_ref[...], axis=0)))
        dba_ref[...] = jnp.zeros_like(dba_ref)
        dba_ref[:, 0:H] = db.astype(BF16)
        dba_ref[:, H:2 * H] = da.astype(BF16)

        @pl.when(pl.program_id(0) == 0)
        def _():
            dalog_ref[...] = jnp.zeros_like(dalog_ref)
            ddtb_ref[...] = jnp.zeros_like(ddtb_ref)

        dalog_ref[...] += dalog
        ddtb_ref[...] += ddtb

    small = pl.BlockSpec((1, H), lambda i: (0, 0))
    per_head = pl.BlockSpec((H, ts, H), lambda i: (0, i, 0))
    return pl.pallas_call(body, grid=(s // ts,),
                          in_specs=[pl.BlockSpec((ts, 256), lambda i: (i, O_BA // 256)), small, small, per_head, per_head, per_head, ANY],
                          out_specs=[pl.BlockSpec((ts, 256), lambda i: (i, O_BA // 256)), small, small],
                          out_shape=[_sds(dproj.shape, BF16), _sds((1, H)), _sds((1, H))], input_output_aliases={6: 0},
                          name="gdn_bg_prep_bwd", compiler_params=_params(("arbitrary",)))(proj, alog, dtb, dbeta_h, dgc_h, dgl_h, dproj)


BLK = 4 * C
NNB, NTB, TNB = ((2,), (1,)), ((2,), (2,)), ((1,), (1,))


def _bdot_b(a, b, dims):
    return lax.dot_general(a.astype(BF16), b.astype(BF16), (dims, ((0,), (0,))), preferred_element_type=F32)


@jax.custom_vjp
def _inv_unit_lower(a):
    n = a.shape[-1]
    row = lax.broadcasted_iota(jnp.int32, (n, n), 0)
    col = lax.broadcasted_iota(jnp.int32, (n, n), 1)
    x = jnp.where(row == col, 1.0, 0.0).astype(F32) - a
    p = _bdot_b(a, a, NNB)
    power = 2
    while True:
        x = x + _bdot_b(x, p, NNB)
        power *= 2
        if power >= C:
            return x
        p = _bdot_b(p, p, NNB)


def _inv_fwd(a):
    t = _inv_unit_lower(a)
    return t, t


def _inv_bwd(t, dt):
    return (-_bdot_b(_bdot_b(t, dt, TNB), t, NTB),)


_inv_unit_lower.defvjp(_inv_fwd, _inv_bwd)


def _gdn_prep(q, k, v, bfull, gcfull, glfull, hmask):
    nb, n = q.shape[0], q.shape[1]
    beta = jnp.sum(bfull * hmask, axis=-1, keepdims=True)
    gc = jnp.sum(gcfull * hmask, axis=-1, keepdims=True)
    gl = jnp.sum(glfull * hmask, axis=-1, keepdims=True)
    row = lax.broadcasted_iota(jnp.int32, (n, n), 0)
    col = lax.broadcasted_iota(jnp.int32, (n, n), 1)
    shift = C.bit_length() - 1
    same = (row >> shift) == (col >> shift)
    incl, strict = same & (row >= col), same & (row > col)
    g_i = gc * jnp.ones((1, 1, n), F32)
    decay = jnp.exp(jnp.where(incl, g_i - jnp.swapaxes(g_i, 1, 2), -jnp.inf))
    kb = k * beta
    a = jnp.where(strict, _bdot_b(kb, k, NTB) * decay, 0.0)
    tinv = _inv_unit_lower(a)
    u = _bdot_b(tinv, v * beta, NNB)
    w = _bdot_b(tinv, kb * jnp.exp(gc), NNB)
    attn = _bdot_b(q, k, NTB) * decay
    fold = ((lax.broadcasted_iota(jnp.int32, (n, C), 0) & (C - 1)) == lax.broadcasted_iota(jnp.int32, (n, C), 1)).astype(F32)
    attn_c = _bdot(attn.reshape(nb * n, n), fold, NN).reshape(nb, n, C)
    return u, w, attn_c, q * jnp.exp(gc), k * jnp.exp(gl - gc), jnp.exp(gl)


def _gdn_step(u, w, attn, qg, kd, egl, state):
    v_new = u - _bdot(w, state, NN)
    o = _bdot(qg, state, NN) + _bdot(attn, v_new, NN)
    return o, state * egl + _bdot(kd, v_new, TN)


def _head_mask(h):
    return (lax.broadcasted_iota(jnp.int32, (1, H), 1) == h).astype(F32)


PREP_BLOCKS = 2


def _gdn_prep_specs(r):
    small = pl.BlockSpec((r, H), lambda h, c: (c, 0))
    return [pl.BlockSpec((r, DK), lambda h, c: (c, h)), pl.BlockSpec((r, DK), lambda h, c: (c, H + h)),
            pl.BlockSpec((r, DV), lambda h, c: (c, h)), small, small, small]


def _blocked(ref):
    x = ref[...]
    return x.reshape(PREP_BLOCKS, BLK, x.shape[-1])


def _gdn_inter_specs(r):
    col = pl.BlockSpec((r, DK), lambda h, c: (c, h))
    return [pl.BlockSpec((r, DV), lambda h, c: (c, h)), col, pl.BlockSpec((1, r, C), lambda h, c: (h, c, 0)), col, col,
            pl.BlockSpec((1, r // C, 8, 128), lambda h, c: (h, c, 0, 0))]


def _gdn_prep_fwd(qk, v, beta, gc, gl):
    s = qk.shape[0]
    r = PREP_BLOCKS * BLK

    def body(q_ref, k_ref, v_ref, b_ref, gc_ref, gl_ref, u_ref, w_ref, attn_ref, qg_ref, kd_ref, egl_ref):
        u, w, attn, qg, kd, egl = _gdn_prep(_blocked(q_ref), _blocked(k_ref), _blocked(v_ref), _blocked(b_ref), _blocked(gc_ref),
                                            _blocked(gl_ref), _head_mask(pl.program_id(0)))
        u_ref[...] = u.reshape(r, DV)
        w_ref[...] = w.reshape(r, DK).astype(BF16)
        attn_ref[0] = attn.reshape(r, C).astype(BF16)
        qg_ref[...] = qg.reshape(r, DK).astype(BF16)
        kd_ref[...] = kd.reshape(r, DK).astype(BF16)
        egl = egl.reshape(r, 1)
        for j in range(r // C):
            egl_ref[0, j] = egl[j * C:j * C + 1, :] * jnp.ones((8, 128), F32)

    out_shape = [_sds((s, H * DV)), _sds((s, H * DK), BF16), _sds((H, s, C), BF16), _sds((s, H * DK), BF16),
                 _sds((s, H * DK), BF16), _sds((H, s // C, 8, 128))]
    return pl.pallas_call(body, grid=(H, s // r), in_specs=_gdn_prep_specs(r), out_specs=_gdn_inter_specs(r), out_shape=out_shape,
                          name="gdn_prep_fwd", compiler_params=_params(("parallel", "parallel")))(qk, qk, v, beta, gc, gl)


def _gdn_prep_bwd(qk, v, beta, gc, gl, du, dw, dattn, dqg, dkd, degl):
    s = qk.shape[0]
    r = PREP_BLOCKS * BLK

    def body(q_ref, k_ref, v_ref, b_ref, gc_ref, gl_ref, du_ref, dw_ref, dattn_ref, dqg_ref, dkd_ref, degl_ref,
             dq_ref, dk_ref, dv_ref, db_ref, dgc_ref, dgl_ref):
        hmask = _head_mask(pl.program_id(0))
        _, vjp = jax.vjp(lambda q, k, v, b, gc, gl: _gdn_prep(q, k, v, b, gc, gl, hmask), _blocked(q_ref), _blocked(k_ref),
                         _blocked(v_ref), _blocked(b_ref), _blocked(gc_ref), _blocked(gl_ref))
        rowid = lax.broadcasted_iota(jnp.int32, (r, 1), 0)
        degl = jnp.zeros((r, 1), F32)
        for j in range(r // C):
            degl = jnp.where(rowid == j * C, degl_ref[0, j, 0:1, 0:1], degl)
        dq, dk, dv, db, dgc, dgl = vjp((_blocked(du_ref), _blocked(dw_ref), _blocked(dattn_ref.at[0]), _blocked(dqg_ref),
                                        _blocked(dkd_ref), degl.reshape(PREP_BLOCKS, BLK, 1)))
        dq_ref[...] = dq.reshape(r, DK)
        dk_ref[...] = dk.reshape(r, DK)
        dv_ref[...] = dv.reshape(r, DV)
        db_ref[0] = db.reshape(r, H)
        dgc_ref[0] = dgc.reshape(r, H)
        dgl_ref[0] = dgl.reshape(r, H)

    col = pl.BlockSpec((r, DK), lambda h, c: (c, h))
    piece = pl.BlockSpec((1, r, H), lambda h, c: (h, c, 0))
    dq, dk, dv, db, dgc, dgl = pl.pallas_call(
        body, grid=(H, s // r), in_specs=_gdn_prep_specs(r) + _gdn_inter_specs(r),
        out_specs=[col, col, pl.BlockSpec((r, DV), lambda h, c: (c, h)), piece, piece, piece],
        out_shape=[_sds((s, H * DK)), _sds((s, H * DK)), _sds((s, H * DV))] + [_sds((H, s, H))] * 3,
        name="gdn_prep_bwd", compiler_params=_params(("parallel", "parallel"), 24 << 20))(
            qk, qk, v, beta, gc, gl, du, dw, dattn, dqg, dkd, degl)
    return jnp.concatenate([dq, dk], axis=1), dv, db, dgc, dgl


def _gdn_scan_specs(r, order):
    wide = pl.BlockSpec((r, H * DK), lambda c: (order(c), 0))
    return [pl.BlockSpec((r, H * DV), lambda c: (order(c), 0)), wide, pl.BlockSpec((H, r, C), lambda c: (0, order(c), 0)), wide, wide,
            pl.BlockSpec((H, r // C, 8, 128), lambda c: (0, order(c), 0, 0))]


def _gdn_scan_fwd(u, w, attn, qg, kd, egl):
    s = u.shape[0]
    r = CPB * C
    nb = s // r

    def body(u_ref, w_ref, attn_ref, qg_ref, kd_ref, egl_ref, o_ref, st_ref, state_ref):
        @pl.when(pl.program_id(0) == 0)
        def _():
            state_ref[...] = jnp.zeros_like(state_ref)

        for h in range(H):
            kc, vc = slice(h * DK, (h + 1) * DK), slice(h * DV, (h + 1) * DV)
            state = state_ref[h]
            for i in range(CPB):
                rs = slice(i * C, (i + 1) * C)
                st_ref[h, i] = state
                o, state = _gdn_step(u_ref[rs, vc], w_ref[rs, kc], attn_ref[h, rs, :], qg_ref[rs, kc], kd_ref[rs, kc],
                                     egl_ref[h, i, 0:1, 0:1], state)
                o_ref[rs, vc] = o
            state_ref[h] = state

    out_specs = [pl.BlockSpec((r, H * DV), lambda c: (c, 0)), pl.BlockSpec((H, CPB, DK, DV), lambda c: (0, c, 0, 0))]
    return pl.pallas_call(body, grid=(nb,), in_specs=_gdn_scan_specs(r, lambda c: c), out_specs=out_specs,
                          out_shape=[_sds((s, H * DV)), _sds((H, s // C, DK, DV))],
                          scratch_shapes=[pltpu.VMEM((H, DK, DV), F32)], name="gdn_scan_fwd",
                          compiler_params=_params(("arbitrary",), 24 << 20))(u, w, attn, qg, kd, egl)


def _gdn_scan_bwd(u, w, attn, qg, kd, egl, states, do):
    s = u.shape[0]
    r = CPB * C
    nb = s // r

    def body(u_ref, w_ref, attn_ref, qg_ref, kd_ref, egl_ref, st_ref, do_ref,
             du_ref, dw_ref, dattn_ref, dqg_ref, dkd_ref, degl_ref, dstate_ref):
        @pl.when(pl.program_id(0) == 0)
        def _():
            dstate_ref[...] = jnp.zeros_like(dstate_ref)

        for h in range(H):
            kc, vc = slice(h * DK, (h + 1) * DK), slice(h * DV, (h + 1) * DV)
            dstate = dstate_ref[h]
            for i in reversed(range(CPB)):
                rs = slice(i * C, (i + 1) * C)
                _, vjp = jax.vjp(_gdn_step, u_ref[rs, vc], w_ref[rs, kc].astype(F32), attn_ref[h, rs, :].astype(F32),
                                 qg_ref[rs, kc].astype(F32), kd_ref[rs, kc].astype(F32), egl_ref[h, i, 0:1, 0:1], st_ref[h, i])
                du, dw, dattn, dqg, dkd, degl, dstate = vjp((do_ref[rs, vc], dstate))
                du_ref[rs, vc] = du
                dw_ref[rs, kc] = dw
                dattn_ref[h, rs, :] = dattn
                dqg_ref[rs, kc] = dqg
                dkd_ref[rs, kc] = dkd
                degl_ref[h, i] = degl * jnp.ones((8, 128), F32)
            dstate_ref[h] = dstate

    rev = lambda c: nb - 1 - c
    in_specs = _gdn_scan_specs(r, rev) + [pl.BlockSpec((H, CPB, DK, DV), lambda c: (0, rev(c), 0, 0)),
                                          pl.BlockSpec((r, H * DV), lambda c: (rev(c), 0))]
    return pl.pallas_call(
        body, grid=(nb,), in_specs=in_specs, out_specs=_gdn_scan_specs(r, rev),
        out_shape=[_sds((s, H * DV)), _sds((s, H * DK)), _sds((H, s, C)), _sds((s, H * DK)), _sds((s, H * DK)),
                   _sds((H, s // C, 8, 128))],
        scratch_shapes=[pltpu.VMEM((H, DK, DV), F32)], name="gdn_scan_bwd",
        compiler_params=_params(("arbitrary",), 40 << 20))(u, w, attn, qg, kd, egl, states, do)


def _rot(x, cs, sn):
    return x * cs + pltpu.roll(x, DK // 2, 1) * sn


def _rot_t(d, cs, sn):
    return d * cs - pltpu.roll(d, DK // 2, 1) * sn


def _ret_chunk(q, k, v, state, lg):
    n = q.shape[0]
    row = lax.broadcasted_iota(jnp.int32, (n, n), 0)
    col = lax.broadcasted_iota(jnp.int32, (n, n), 1)
    dist = (row - col).astype(F32)
    dmat = jnp.exp(jnp.where(dist >= 0, dist * lg, -jnp.inf))
    scores = _bdot(q, k, NT) * dmat
    pos = lax.broadcasted_iota(jnp.int32, (n, 1), 0).astype(F32)
    xi = jnp.exp((pos + 1.0) * lg)
    zeta = jnp.exp((n - 1.0 - pos) * lg)
    o = _bdot(scores, v, NN) + _bdot(q, state, NN) * xi
    new_state = state * jnp.exp(n * lg) + _bdot(k * zeta, v, TN)
    return o, new_state


def _ret_specs(r, order):
    q0, v0 = O_RET // DK, (O_RET + 2 * DK) // DV
    return [pl.BlockSpec((r, DK), lambda h, c: (order(c), q0 + h * (RET_W // DK))),
            pl.BlockSpec((r, DK), lambda h, c: (order(c), q0 + 1 + h * (RET_W // DK))),
            pl.BlockSpec((r, DV), lambda h, c: (order(c), v0 + h * (RET_W // DV))), pl.BlockSpec((r, DK), lambda h, c: (order(c), 0)),
            pl.BlockSpec((r, DK), lambda h, c: (order(c), 0)), pl.BlockSpec((1, H), lambda h, c: (0, 0))]


RET_C = 256


def _ret_scan_fwd(proj, cs, sn, lgtab):
    s = proj.shape[0]
    r = min(RET_C, s)
    nb = s // r

    def body(q_ref, k_ref, v_ref, cs_ref, sn_ref, lg_ref, o_ref, st_ref, state_ref):
        @pl.when(pl.program_id(1) == 0)
        def _():
            state_ref[...] = jnp.zeros_like(state_ref)

        lg = jnp.sum(lg_ref[...] * _head_mask(pl.program_id(0)), axis=1, keepdims=True)
        state = state_ref[...]
        st_ref[0, 0] = state
        q = _rot(q_ref[...], cs_ref[...], sn_ref[...])
        k = _rot(k_ref[...], cs_ref[...], sn_ref[...]) * DK ** -0.5
        o_ref[...], state_ref[...] = _ret_chunk(q, k, v_ref[...], state, lg)

    out_specs = [pl.BlockSpec((r, DV), lambda h, c: (c, h)), pl.BlockSpec((1, 1, DK, DV), lambda h, c: (h, c, 0, 0))]
    return pl.pallas_call(body, grid=(H, nb), in_specs=_ret_specs(r, lambda c: c), out_specs=out_specs,
                          out_shape=[_sds((s, H * DV)), _sds((H, nb, DK, DV))],
                          scratch_shapes=[pltpu.VMEM((DK, DV), F32)], name="ret_scan_fwd",
                          compiler_params=_params(("parallel", "arbitrary")))(proj, proj, proj, cs, sn, lgtab)


def _ret_scan_bwd(proj, cs, sn, lgtab, states, do, dproj):
    s = proj.shape[0]
    r = min(RET_C, s)
    nb = s // r

    def body(q_ref, k_ref, v_ref, cs_ref, sn_ref, lg_ref, st_ref, do_ref, _, d_ref, dstate_ref):
        @pl.when(pl.program_id(1) == 0)
        def _():
            dstate_ref[...] = jnp.zeros_like(dstate_ref)

        lg = jnp.sum(lg_ref[...] * _head_mask(pl.program_id(0)), axis=1, keepdims=True)
        cs_, sn_ = cs_ref[...], sn_ref[...]
        q = _rot(q_ref[...], cs_, sn_)
        k = _rot(k_ref[...], cs_, sn_) * DK ** -0.5
        _, vjp = jax.vjp(lambda q, k, v, st: _ret_chunk(q, k, v, st, lg), q, k, v_ref[...], st_ref[0, 0])
        dq, dk, dv, dstate_ref[...] = vjp((do_ref[...], dstate_ref[...]))
        d_ref[:, 0:DK] = _rot_t(dq, cs_, sn_).astype(BF16)
        d_ref[:, DK:2 * DK] = _rot_t(dk * DK ** -0.5, cs_, sn_).astype(BF16)
        d_ref[:, 2 * DK:RET_W] = dv.astype(BF16)

    rev = lambda c: nb - 1 - c
    in_specs = _ret_specs(r, rev) + [pl.BlockSpec((1, 1, DK, DV), lambda h, c: (h, rev(c), 0, 0)),
                                     pl.BlockSpec((r, DV), lambda h, c: (rev(c), h)), ANY]
    return pl.pallas_call(
        body, grid=(H, nb), in_specs=in_specs, out_specs=pl.BlockSpec((r, RET_W), lambda h, c: (rev(c), O_RET // RET_W + h)),
        out_shape=_sds(dproj.shape, BF16), input_output_aliases={8: 0},
        scratch_shapes=[pltpu.VMEM((DK, DV), F32)], name="ret_scan_bwd",
        compiler_params=_params(("parallel", "arbitrary")))(proj, proj, proj, cs, sn, lgtab, states, do, dproj)


def _merge(oa, z, ob, rg, ga, gb, wa, wb):
    ya = oa * lax.rsqrt(jnp.mean(oa * oa, axis=-1, keepdims=True) + EPS) * wa * _silu(z)
    mu = jnp.mean(ob, axis=-1, keepdims=True)
    var = jnp.mean(jnp.square(ob - mu), axis=-1, keepdims=True)
    yb = (ob - mu) * lax.rsqrt(var + EPS) * wb * _silu(rg)
    return jax.nn.sigmoid(ga) * ya + jax.nn.sigmoid(gb) * yb


def _merge_specs(ts):
    own = pl.BlockSpec((ts, DV), lambda h, i: (i, h))
    grp = lambda k: pl.BlockSpec((ts, DV), lambda h, i: (i, O_MERGE // DV + 4 * h + k))
    return [own, grp(0), own, grp(1), grp(2), grp(3),
            pl.BlockSpec((1, DV), lambda h, i: (0, 0)), pl.BlockSpec((1, DV), lambda h, i: (0, h))]


def _merge_fwd(oa, ob, proj, wa, wb, ts):
    s = oa.shape[0]

    def body(oa_ref, z_ref, ob_ref, rg_ref, ga_ref, gb_ref, wa_ref, wb_ref, o_ref, ot_ref):
        y = _merge(oa_ref[...], z_ref[...], ob_ref[...], rg_ref[...], ga_ref[...], gb_ref[...],
                   wa_ref[...], wb_ref[...]).astype(BF16)
        o_ref[...] = y
        ot_ref[...] = y.T

    return pl.pallas_call(body, grid=(H, s // ts), in_specs=_merge_specs(ts),
                          out_specs=[pl.BlockSpec((ts, DV), lambda h, i: (i, h)), pl.BlockSpec((DV, ts), lambda h, i: (h, i))],
                          out_shape=[_sds((s, H * DV), BF16), _sds((H * DV, s), BF16)], name="merge_fwd",
                          compiler_params=_params(("parallel", "parallel")))(oa, proj, ob, proj, proj, proj, wa, wb)


def _merge_bwd(oa, ob, proj, wa, wb, dmixed, ts):
    s = oa.shape[0]

    def body(oa_ref, z_ref, ob_ref, rg_ref, ga_ref, gb_ref, wa_ref, wb_ref, dm_ref,
             doa_ref, dob_ref, dgrp_ref, dwa_ref, dwb_ref):
        _, vjp = jax.vjp(_merge, oa_ref[...], z_ref[...], ob_ref[...], rg_ref[...], ga_ref[...], gb_ref[...],
                         wa_ref[...], wb_ref[...])
        doa, dz, dob, drg, dga, dgb, dwa, dwb = vjp(dm_ref[...].astype(F32))
        doa_ref[...] = doa
        dob_ref[...] = dob
        for k, d in enumerate((dz, drg, dga, dgb)):
            dgrp_ref[:, k * DV:(k + 1) * DV] = d.astype(BF16)
        first_tile = pl.program_id(1) == 0

        @pl.when(first_tile & (pl.program_id(0) == 0))
        def _():
            dwa_ref[...] = jnp.zeros_like(dwa_ref)

        @pl.when(first_tile)
        def _():
            dwb_ref[...] = jnp.zeros_like(dwb_ref)

        dwa_ref[...] += dwa
        dwb_ref[...] += dwb

    blk = pl.BlockSpec((ts, DV), lambda h, i: (i, h))
    out_specs = [blk, blk, pl.BlockSpec((ts, MERGE_W), lambda h, i: (i, O_MERGE // MERGE_W + h)),
                 pl.BlockSpec((1, DV), lambda h, i: (0, 0)), pl.BlockSpec((1, DV), lambda h, i: (0, h))]
    out_shape = [_sds((s, H * DV)), _sds((s, H * DV)), _sds((s, P_IN), BF16), _sds((1, DV)), _sds((1, H * DV))]
    return pl.pallas_call(body, grid=(H, s // ts), in_specs=_merge_specs(ts) + [blk], out_specs=out_specs, out_shape=out_shape,
                          name="merge_bwd", compiler_params=_params(("arbitrary", "arbitrary"), 40 * ts * DV * 4))(
                              oa, proj, ob, proj, proj, proj, wa, wb, dmixed)


def _act(hg, hu):
    return _silu(hg) * hu


def _ffn_gate_up(hn, w_gate, w_up, tm, tn):
    s, f = hn.shape[0], w_gate.shape[1]
    tm, tn = min(tm, s), min(tn, f)
    assert s % tm == 0 and f % tn == 0 and tm % 256 == 0
    sub = tm // 2

    def body(a_ref, wg_ref, wu_ref, hg_ref, hu_ref, act_ref, actt_ref):
        for r0 in range(0, tm, sub):
            rs = slice(r0, r0 + sub)
            hg = _bdot(a_ref[rs, :], wg_ref[...], NN)
            hu = _bdot(a_ref[rs, :], wu_ref[...], NN)
            y = _act(hg, hu).astype(BF16)
            hg_ref[rs, :] = hg.astype(BF16)
            hu_ref[rs, :] = hu.astype(BF16)
            act_ref[rs, :] = y
            actt_ref[:, rs] = y.T

    wsp = pl.BlockSpec((D, tn), lambda i, j: (0, j))
    blk = pl.BlockSpec((tm, tn), lambda i, j: (i, j))
    est = 2 * (tm * D * 2 + 2 * D * tn * 2 + 4 * tm * tn * 2) + 4 * sub * tn * 4
    return pl.pallas_call(body, grid=(s // tm, f // tn), in_specs=[pl.BlockSpec((tm, D), lambda i, j: (i, 0)), wsp, wsp],
                          out_specs=[blk, blk, blk, pl.BlockSpec((tn, tm), lambda i, j: (j, i))],
                          out_shape=[_sds((s, f), BF16)] * 3 + [_sds((f, s), BF16)], name="ffn_gate_up",
                          compiler_params=_params(("parallel", "parallel"), est))(hn, w_gate, w_up)


def _ffn_down_dx(dh2, w_down, hg, hu, tm, tn):
    s, f = hg.shape
    tm, tn = min(tm, s), min(tn, f)
    assert s % tm == 0 and f % tn == 0 and tm % 256 == 0
    sub = tm // 2

    def body(d_ref, w_ref, hg_ref, hu_ref, dhg_ref, dhu_ref):
        for r0 in range(0, tm, sub):
            rs = slice(r0, r0 + sub)
            dact = _bdot(d_ref[rs, :], w_ref[...], NT)
            _, vjp = jax.vjp(_act, hg_ref[rs, :].astype(F32), hu_ref[rs, :].astype(F32))
            dhg, dhu = vjp(dact)
            dhg_ref[rs, :] = dhg.astype(BF16)
            dhu_ref[rs, :] = dhu.astype(BF16)

    blk = pl.BlockSpec((tm, tn), lambda i, j: (i, j))
    est = 2 * (tm * D * 4 + tn * D * 2 + 4 * tm * tn * 2) + 6 * sub * tn * 4
    return pl.pallas_call(body, grid=(s // tm, f // tn),
                          in_specs=[pl.BlockSpec((tm, D), lambda i, j: (i, 0)), pl.BlockSpec((tn, D), lambda i, j: (j, 0)), blk, blk],
                          out_specs=[blk, blk], out_shape=[_sds((s, f), BF16)] * 2, name="ffn_down_dx",
                          compiler_params=_params(("parallel", "parallel"), est))(dh2, w_down, hg, hu)


def _loss_rows(h2, wf, tgt):
    err = _rms(h2, wf) - tgt
    return 0.5 * jnp.sum(jnp.mean(err * err, axis=-1, keepdims=True), keepdims=True)


def _loss_fwd_bwd(h2, wf, tgt, ts):
    s = h2.shape[0]

    def body(h_ref, w_ref, t_ref, loss_ref, dh_ref, dw_ref):
        loss, vjp = jax.vjp(_loss_rows, h_ref[...], w_ref[...], t_ref[...])
        dh, dw, _ = vjp(jnp.ones((1, 1), F32))
        dh_ref[...] = dh

        @pl.when(pl.program_id(0) == 0)
        def _():
            loss_ref[...] = jnp.zeros_like(loss_ref)
            dw_ref[...] = jnp.zeros_like(dw_ref)

        loss_ref[...] += loss
        dw_ref[...] += dw

    row = pl.BlockSpec((ts, D), lambda i: (i, 0))
    vec = pl.BlockSpec((1, D), lambda i: (0, 0))
    tile = pl.BlockSpec((8, 128), lambda i: (0, 0))
    return pl.pallas_call(body, grid=(s // ts,), in_specs=[row, vec, row], out_specs=[tile, row, vec],
                          out_shape=[_sds((8, 128)), _sds((s, D)), _sds((1, D))], name="final_norm_loss",
                          compiler_params=_params(("arbitrary",), 12 * ts * D * 4))(h2, wf, tgt)


def _rope_tables(s):
    inv = ROPE_BASE ** (-jnp.arange(0, DK, 2, dtype=F32) / DK)
    ang = jnp.arange(s, dtype=F32)[:, None] * inv[None, :]
    cos, sin = jnp.cos(ang), jnp.sin(ang)
    return jnp.concatenate([cos, cos], axis=1), jnp.concatenate([-sin, sin], axis=1)


def _local_step(x, tgt, w_in, w_out, w_gate, w_up, w_down, norm1_w, conv_w, a_log, dt_bias, gdn_norm_w, ret_norm_w, norm2_w, norm_f_w,
                dist=None):
    s = x.shape[0]
    ts = min(512, s)
    cs, sn = _rope_tables(s)
    lgtab = jnp.log1p(-jnp.exp2(-5.0 - jnp.arange(H, dtype=F32))).reshape(1, H)

    u, u_t = _rms_fwd(x, norm1_w, ts, "norm1_fwd")
    if dist is None:
        proj = _matmul(u, w_in, tm=1024, tn=1280, tk=D, name="in_proj")
    else:
        proj, gathered = _matmul(u, w_in, tm=1024, tn=1280, tk=D, side=_gather_side(dist["shards"]), name="in_proj")
        w_out, w_gate, w_up, w_down = (gathered[0].reshape(D, D), _from_slots_cols(gathered[1]), _from_slots_cols(gathered[2]),
                                       gathered[3].reshape(-1, D))
    qk, va = _gdn_qkv_fwd(proj, conv_w, ts)
    beta, gc, gl = _bg_fwd(proj, a_log, dt_bias, ts)
    inter = _gdn_prep_fwd(qk, va, beta, gc, gl)
    oa, st_a = _gdn_scan_fwd(*inter)
    ob, st_b = _ret_scan_fwd(proj, cs, sn, lgtab)
    mixed, mixed_t = _merge_fwd(oa, ob, proj, gdn_norm_w, ret_norm_w, ts)
    h1 = _matmul(mixed, w_out, tm=1024, tn=1024, tk=D, res=x, name="out_proj")
    hn, hn_t = _rms_fwd(h1, norm2_w, ts, "norm2_fwd")
    hg, hu, act, act_t = _ffn_gate_up(hn, w_gate, w_up, 512, 1408)
    h2 = _matmul(act, w_down, tm=1024, tn=1024, tk=1408, res=h1, name="ffn_down")
    loss, dh2, d_norm_f = _loss_fwd_bwd(h2, norm_f_w, tgt, ts)

    dhg, dhu = _ffn_down_dx(dh2, w_down, hg, hu, 512, 1408)
    g_down = _matmul(act_t, dh2, tm=1408, tn=1024, tk=D, out_dtype=BF16, name="ffn_down_dw")
    g_gate = _matmul(hn_t, dhg, tm=1024, tn=1408, tk=D, out_dtype=BF16, name="ffn_gate_dw")
    g_up = _matmul(hn_t, dhu, tm=1024, tn=1408, tk=D, out_dtype=BF16, name="ffn_up_dw")
    dhn = _matmul(dhg, w_gate, tb=True, tm=1024, tn=1024, tk=1408, name="ffn_gate_dx")
    dhn = _matmul(dhu, w_up, tb=True, tm=1024, tn=1024, tk=1408, res=dhn, name="ffn_up_dx")
    dh1, d_norm2 = _rms_bwd(h1, norm2_w, dhn, dh2, ts, "norm2_bwd")

    g_out = _matmul(mixed_t, dh1, tm=1024, tn=1024, tk=D, out_dtype=BF16, name="out_proj_dw")
    early = ["w_out", "w_gate", "w_up", "w_down"]
    if dist is None:
        dmixed = _matmul(dh1, w_out, tb=True, tm=1024, tn=1024, tk=D, out_dtype=BF16, name="out_proj_dx")
    else:
        slots = dict(w_out=g_out.reshape(NDEV, D // NDEV, D), w_gate=_to_slots_cols(g_gate), w_up=_to_slots_cols(g_up),
                     w_down=g_down.reshape(NDEV, -1, D))
        dmixed, from_sibling = _matmul(dh1, w_out, tb=True, tm=1024, tn=1024, tk=D, out_dtype=BF16,
                                       side=_sibling_side([slots[k] for k in early]), name="out_proj_dx")
        parts = [_add_sibling(slots[k], r, dist["core"], 128, "grads_add_" + k) for k, r in zip(early, from_sibling)]
    doa, dob, dproj, d_gdn_norm, d_ret_norm = _merge_bwd(oa, ob, proj, gdn_norm_w, ret_norm_w, dmixed, ts)

    dproj = _ret_scan_bwd(proj, cs, sn, lgtab, st_b, dob, dproj)
    d_inter = _gdn_scan_bwd(*inter, st_a, doa)
    dqk, dva, dbeta_h, dgc_h, dgl_h = _gdn_prep_bwd(qk, va, beta, gc, gl, *d_inter)
    dproj, d_conv = _gdn_qkv_bwd(proj, conv_w, dqk, dva, dproj, ts)
    dproj, d_a_log, d_dt_bias = _bg_bwd(proj, a_log, dt_bias, dbeta_h, dgc_h, dgl_h, dproj, ts)

    if dist is None:
        g_in = _matmul(u_t, dproj, tm=1024, tn=1280, tk=D, out_dtype=BF16, name="in_proj_dw")
        du = _matmul(dproj, w_in, tb=True, tm=1024, tn=1024, tk=1664, name="in_proj_dx")
        big = dict(w_in=g_in, w_out=g_out, w_gate=g_gate, w_up=g_up, w_down=g_down)
    else:
        g_in, from_chips = _matmul(u_t, dproj, tm=1024, tn=1280, tk=D, out_dtype=BF16, side=_chips_side(parts), name="in_proj_dw")
        du, (from_all,) = _matmul(dproj, w_in, tb=True, tm=1024, tn=1024, tk=1664,
                                  side=_all_to_all_side(_to_slots_cols(_ungroup_w_in(g_in))), name="in_proj_dx")
        big = dict(w_in=from_all, **{k: (p, r) for k, p, r in zip(early, parts, from_chips)})
    dx, d_norm1 = _rms_bwd(x, norm1_w, du, dh1, ts, "norm1_bwd")

    small = dict(norm1_w=d_norm1, conv_w=d_conv, a_log=d_a_log, dt_bias=d_dt_bias, gdn_norm_w=d_gdn_norm,
                 ret_norm_w=d_ret_norm, norm2_w=d_norm2, norm_f_w=d_norm_f)
    return loss, dx, big, small


def _coords():
    return lax.axis_index("x"), lax.axis_index("y"), lax.axis_index("c")


def _gather_side(shards):
    n = len(shards)

    def plan(ins, outs, send_sems, recv_sems, local_sems):
        x, y, c = _coords()
        me, sibling = (x, y, c), (x, y, 1 - c)
        chips = [(1 - x, y), (x, 1 - y), (1 - x, 1 - y)]

        def copy(a, k, block, to, src=None):
            px, py, pc = block
            dst = outs[a].at[4 * px + 2 * py + pc]
            return pltpu.make_async_remote_copy(src_ref=dst if src is None else src, dst_ref=dst, send_sem=send_sems.at[a, k],
                                                recv_sem=recv_sems.at[a, k], device_id=to, device_id_type=MESH)

        mine = [pltpu.make_async_copy(ins[a], outs[a].at[4 * x + 2 * y + c], local_sems.at[a]) for a in range(n)]
        first = []
        for a in range(n):
            first.append(copy(a, 0, me, sibling, src=ins[a]))
            first += [copy(a, 1 + j, me, (*chip, c), src=ins[a]) for j, chip in enumerate(chips)]
        return c, me, sibling, chips, copy, mine, first

    def start(ins, outs, *sems):
        *_, mine, first = plan(ins, outs, *sems)
        for cp in mine + first:
            cp.start()

    def finish(ins, outs, *sems):
        c, me, sibling, chips, copy, mine, first = plan(ins, outs, *sems)
        passed = []
        for j, chip in enumerate(chips):
            for a in range(n):
                copy(a, 1 + j, (*chip, c), me).wait_recv()
                fwd = copy(a, 4 + j, (*chip, c), sibling)
                fwd.start()
                passed.append(fwd)
        for a in range(n):
            copy(a, 0, sibling, me).wait_recv()
            for j, chip in enumerate(chips):
                copy(a, 4 + j, (*chip, 1 - c), me).wait_recv()
        for cp in first + passed:
            cp.wait_send()
        for cp in mine:
            cp.wait()

    return _Side(shards, [_sds((NDEV,) + a.shape, a.dtype) for a in shards], [(n, 7), (n, 7), (n,)], start, finish)


def _exchange_side(ins, n_out, copies_of):
    def start(in_refs, out_refs, *sems):
        for cp in copies_of(in_refs, out_refs, *sems):
            cp.start()

    def finish(in_refs, out_refs, *sems):
        for cp in copies_of(in_refs, out_refs, *sems):
            cp.wait()

    n = len(ins)
    return _Side(ins, [_sds((n_out,) + a.shape[1:], a.dtype) for a in ins], [(n, n_out), (n, n_out)], start, finish)


def _sibling_side(slots):
    def copies_of(ins, outs, send_sems, recv_sems):
        x, y, c = _coords()
        return [pltpu.make_async_remote_copy(
            src_ref=ins[a].at[2 * j + (1 - c)], dst_ref=outs[a].at[j], send_sem=send_sems.at[a, j], recv_sem=recv_sems.at[a, j],
            device_id=(x, y, 1 - c), device_id_type=MESH) for a in range(len(slots)) for j in range(4)]

    return _exchange_side(slots, 4, copies_of)


def _chips_side(parts):
    def copies_of(ins, outs, send_sems, recv_sems):
        x, y, c = _coords()
        chips = [(1 - x, y), (x, 1 - y), (1 - x, 1 - y)]
        return [pltpu.make_async_remote_copy(
            src_ref=ins[a].at[2 * px + py], dst_ref=outs[a].at[k], send_sem=send_sems.at[a, k], recv_sem=recv_sems.at[a, k],
            device_id=(px, py, c), device_id_type=MESH) for a in range(len(parts)) for k, (px, py) in enumerate(chips)]

    return _exchange_side(parts, 3, copies_of)


def _all_to_all_side(slots):
    def plan(ins, outs, send_sems, recv_sems, local_sems):
        x, y, c = _coords()
        mine = 4 * x + 2 * y + c
        own = pltpu.make_async_copy(ins[0].at[mine], outs[0].at[mine], local_sems.at[0])
        remote = []
        for r in range(1, NDEV):
            peer = (x ^ (r >> 2), y ^ ((r >> 1) & 1), c ^ (r & 1))
            remote.append(pltpu.make_async_remote_copy(
                src_ref=ins[0].at[mine ^ r], dst_ref=outs[0].at[mine], send_sem=send_sems.at[r - 1], recv_sem=recv_sems.at[r - 1],
                device_id=peer, device_id_type=MESH))
        return own, remote

    def start(ins, outs, *sems):
        own, remote = plan(ins, outs, *sems)
        for cp in [own] + remote:
            cp.start()

    def finish(ins, outs, *sems):
        own, remote = plan(ins, outs, *sems)
        for cp in remote:
            cp.wait()
        own.wait()

    return _Side([slots], [_sds(slots.shape, slots.dtype)], [(NDEV - 1,), (NDEV - 1,), (1,)], start, finish)


def _allreduce_small(pack, name):
    rows, cols = pack.shape

    def body(in_ref, out_ref, buf_ref, send_sems, recv_sems):
        x, y, c = _coords()
        mine = 4 * x + 2 * y + c
        buf_ref[mine] = in_ref[...]
        copies = []
        for r in range(1, NDEV):
            peer = (x ^ (r >> 2), y ^ ((r >> 1) & 1), c ^ (r & 1))
            copies.append(pltpu.make_async_remote_copy(
                src_ref=in_ref, dst_ref=buf_ref.at[mine], send_sem=send_sems.at[r - 1], recv_sem=recv_sems.at[r - 1],
                device_id=peer, device_id_type=MESH))
        for cp in copies:
            cp.start()
        for r in range(1, NDEV):
            pltpu.make_async_remote_copy(
                src_ref=in_ref, dst_ref=buf_ref.at[mine ^ r], send_sem=send_sems.at[r - 1], recv_sem=recv_sems.at[r - 1],
                device_id=(x, y, c), device_id_type=MESH).wait_recv()
        for cp in copies:
            cp.wait_send()
        acc = buf_ref[0]
        for d in range(1, NDEV):
            acc = acc + buf_ref[d]
        out_ref[...] = acc

    return pl.pallas_call(
        body, in_specs=[VMEM_FULL], out_specs=VMEM_FULL, out_shape=_sds((rows, cols)),
        scratch_shapes=[pltpu.VMEM((NDEV, rows, cols), F32), pltpu.SemaphoreType.DMA((NDEV - 1,)), pltpu.SemaphoreType.DMA((NDEV - 1,))],
        name=name)(pack)


def _add_sibling(slots, recv, core, tr, name):
    _, rows, cols = slots.shape
    tr = _row_tile(rows, tr)

    def body(c_ref, a_ref, b_ref, o_ref):
        o_ref[...] = (a_ref[...].astype(F32) + b_ref[...].astype(F32)).astype(BF16)

    gs = pltpu.PrefetchScalarGridSpec(
        num_scalar_prefetch=1, grid=(4, rows // tr),
        in_specs=[pl.BlockSpec((None, tr, cols), lambda j, i, cr: (2 * j + cr[0], i, 0)),
                  pl.BlockSpec((None, tr, cols), lambda j, i, cr: (j, i, 0))],
        out_specs=pl.BlockSpec((None, tr, cols), lambda j, i, cr: (j, i, 0)))
    return pl.pallas_call(body, grid_spec=gs, out_shape=_sds((4, rows, cols), BF16), name=name,
                          compiler_params=_params(("parallel", "parallel"), 6 * tr * cols * 4))(core, slots, recv)


def _adam_math(w, g, m, v):
    m2 = B1 * m + (1.0 - B1) * g
    v2 = B2 * v + (1.0 - B2) * jnp.square(g)
    m_hat = m2 / (1.0 - B1 ** STEP)
    v_hat = v2 / (1.0 - B2 ** STEP)
    return -LR * (m_hat / (jnp.sqrt(v_hat) + EPS_ADAM) + WD * w), m2, v2


def _adamw_reduced(part, recv, chip, w, m, v, tr, name):
    rows, cols = w.shape
    tr = _row_tile(rows, tr)

    def body(j_ref, p_ref, r0_ref, r1_ref, r2_ref, w_ref, m_ref, v_ref, g_ref, d_ref, nm_ref, nv_ref):
        g = p_ref[...].astype(F32) + r0_ref[...].astype(F32) + r1_ref[...].astype(F32) + r2_ref[...].astype(F32)
        d, m2, v2 = _adam_math(w_ref[...], g, m_ref[...], v_ref[...])
        g_ref[...] = g
        d_ref[...] = d
        nm_ref[...] = m2
        nv_ref[...] = v2

    flat = pl.BlockSpec((tr, cols), lambda i, jr: (i, 0))
    gs = pltpu.PrefetchScalarGridSpec(
        num_scalar_prefetch=1, grid=(rows // tr,),
        in_specs=[pl.BlockSpec((None, tr, cols), lambda i, jr: (jr[0], i, 0))]
        + [pl.BlockSpec((None, tr, cols), functools.partial(lambda i, jr, k: (k, i, 0), k=k)) for k in range(3)] + [flat] * 3,
        out_specs=[flat] * 4)
    return pl.pallas_call(body, grid_spec=gs, out_shape=[_sds((rows, cols))] * 4, name=name,
                          compiler_params=_params(("parallel",), 22 * tr * cols * 4))(chip, part, recv, recv, recv, w, m, v)


def _adamw_summed(recv, w, m, v, tr, name):
    rows, cols = w.shape
    tr = _row_tile(rows, tr)

    def body(*refs):
        parts, (w_ref, m_ref, v_ref), (g_ref, d_ref, nm_ref, nv_ref) = refs[:NDEV], refs[NDEV:NDEV + 3], refs[NDEV + 3:]
        g = parts[0][...].astype(F32)
        for p_ref in parts[1:]:
            g = g + p_ref[...].astype(F32)
        d, m2, v2 = _adam_math(w_ref[...], g, m_ref[...], v_ref[...])
        g_ref[...] = g
        d_ref[...] = d
        nm_ref[...] = m2
        nv_ref[...] = v2

    flat = pl.BlockSpec((tr, cols), lambda i: (i, 0))
    slot = [pl.BlockSpec((None, tr, cols), functools.partial(lambda i, k: (k, i, 0), k=k)) for k in range(NDEV)]
    return pl.pallas_call(body, grid=(rows // tr,), in_specs=slot + [flat] * 3, out_specs=[flat] * 4,
                          out_shape=[_sds((rows, cols))] * 4, name=name,
                          compiler_params=_params(("parallel",), 24 * tr * cols * 4))(*([recv] * NDEV), w, m, v)


def _adamw_plain(w, g, m, v, name):
    def body(w_ref, g_ref, m_ref, v_ref, d_ref, nm_ref, nv_ref):
        d, m2, v2 = _adam_math(w_ref[...], g_ref[...], m_ref[...], v_ref[...])
        d_ref[...] = d
        nm_ref[...] = m2
        nv_ref[...] = v2

    return pl.pallas_call(body, out_shape=[_sds(w.shape)] * 3, name=name)(w, g, m, v)


def _pack_small(norm1_w, conv_w, a_log, dt_bias, gdn_norm_w, ret_norm_w, norm2_w, norm_f_w):
    misc = jnp.concatenate([gdn_norm_w.reshape(1, DV), a_log.reshape(1, H), dt_bias.reshape(1, H),
                            jnp.zeros((1, D - DV - 2 * H), F32)], axis=1)
    return jnp.concatenate([norm1_w.reshape(1, D), ret_norm_w.reshape(1, D), norm2_w.reshape(1, D), norm_f_w.reshape(1, D),
                            conv_w.reshape(8, D), misc, jnp.zeros((3, D), F32)], axis=0)


def _unpack_small(pack):
    return dict(norm1_w=pack[0:1], ret_norm_w=pack[1:2], norm2_w=pack[2:3], norm_f_w=pack[3], conv_w=pack[4:12].reshape(4, 2 * D),
                gdn_norm_w=pack[12:13, 0:DV], a_log=pack[12:13, DV:DV + H], dt_bias=pack[12:13, DV + H:DV + 2 * H])


IN_SPLITS = (4096, 2048, 8, 8, 1024, 1024, 2048, 2048, 2048, 2048)


def _regroup_w_in(w):
    rows = w.shape[0]
    offs = np.cumsum((0,) + IN_SPLITS)
    qkv, z, b, a, rq, rk, rv, rg, ga, gb = (w[:, offs[i]:offs[i + 1]] for i in range(len(IN_SPLITS)))
    merge = jnp.stack([t.reshape(rows, H, DV) for t in (z, rg, ga, gb)], axis=2).reshape(rows, H * MERGE_W)
    ret = jnp.concatenate([rq.reshape(rows, H, DK), rk.reshape(rows, H, DK), rv.reshape(rows, H, DV)], axis=2).reshape(rows, H * RET_W)
    return jnp.concatenate([merge, qkv, ret, b, a, jnp.zeros((rows, P_IN - O_BA - 2 * H), w.dtype)], axis=1)


def _ungroup_w_in(g):
    rows = g.shape[0]
    merge = g[:, O_MERGE:O_QKV].reshape(rows, H, 4, DV)
    z, rg, ga, gb = (merge[:, :, i, :].reshape(rows, H * DV) for i in range(4))
    ret = g[:, O_RET:O_BA].reshape(rows, H, RET_W)
    rq, rk, rv = (ret[:, :, lo:hi].reshape(rows, -1) for lo, hi in ((0, DK), (DK, 2 * DK), (2 * DK, RET_W)))
    return jnp.concatenate([g[:, O_QKV:O_RET], z, g[:, O_BA:O_BA + 2 * H], rq, rk, rv, rg, ga, gb], axis=1)


def _to_slots_cols(g):
    rows, cols = g.shape
    return g.reshape(rows, NDEV, cols // NDEV).transpose(1, 0, 2)


def _from_slots_cols(a):
    n, rows, cols = a.shape
    return a.transpose(1, 0, 2).reshape(rows, n * cols)


WEIGHT_ORDER = ["norm1_w", "w_in", "conv_w", "a_log", "dt_bias", "gdn_norm_w", "ret_norm_w", "w_out", "norm2_w", "w_gate", "w_up",
                "w_down", "norm_f_w"]


def kernel(x, norm1_w, w_in, conv_w, a_log, dt_bias, gdn_norm_w, ret_norm_w, w_out, norm2_w, w_gate, w_up, w_down, norm_f_w, loss_target, m_norm1_w, m_w_in, m_conv_w, m_a_log, m_dt_bias, m_gdn_norm_w, m_ret_norm_w, m_w_out, m_norm2_w, m_w_gate, m_w_up, m_w_down, m_norm_f_w, v_norm1_w, v_w_in, v_conv_w, v_a_log, v_dt_bias, v_gdn_norm_w, v_ret_norm_w, v_w_out, v_norm2_w, v_w_gate, v_w_up, v_w_down, v_norm_f_w):
    ax, ay, ac = _coords()
    me = 4 * ax + 2 * ay + ac
    core = jnp.reshape(ac, (1,)).astype(jnp.int32)
    chip = jnp.reshape(2 * ax + ay, (1,)).astype(jnp.int32)
    w = dict(norm1_w=norm1_w, w_in=w_in[0], conv_w=conv_w[0], a_log=a_log, dt_bias=dt_bias, gdn_norm_w=gdn_norm_w,
             ret_norm_w=ret_norm_w, w_out=w_out[0], norm2_w=norm2_w, w_gate=w_gate[0], w_up=w_up[0], w_down=w_down[0],
             norm_f_w=norm_f_w)
    m = dict(norm1_w=m_norm1_w, w_in=m_w_in[0], conv_w=m_conv_w[0], a_log=m_a_log, dt_bias=m_dt_bias, gdn_norm_w=m_gdn_norm_w,
             ret_norm_w=m_ret_norm_w, w_out=m_w_out[0], norm2_w=m_norm2_w, w_gate=m_w_gate[0], w_up=m_w_up[0], w_down=m_w_down[0],
             norm_f_w=m_norm_f_w)
    v = dict(norm1_w=v_norm1_w, w_in=v_w_in[0], conv_w=v_conv_w[0], a_log=v_a_log, dt_bias=v_dt_bias, gdn_norm_w=v_gdn_norm_w,
             ret_norm_w=v_ret_norm_w, w_out=v_w_out[0], norm2_w=v_norm2_w, w_gate=v_w_gate[0], w_up=v_w_up[0], w_down=v_w_down[0],
             norm_f_w=v_norm_f_w)
    big_names = ["w_in", "w_out", "w_gate", "w_up", "w_down"]

    w_in_all, conv_all = _run_side(_gather_side([w["w_in"].astype(BF16), w["conv_w"]]), "w_in_allgather")
    w_in_full = _regroup_w_in(_from_slots_cols(w_in_all))
    conv_full = _from_slots_cols(conv_all)
    dist = dict(core=core, shards=[w[k].astype(BF16) for k in ("w_out", "w_gate", "w_up", "w_down")])

    loss_tile, dx, big, small = _local_step(
        x[0], loss_target[0], w_in_full, None, None, None, None, norm1_w, conv_full, a_log, dt_bias,
        gdn_norm_w, ret_norm_w, norm2_w, norm_f_w.reshape(1, D), dist=dist)
    loss = lax.psum(loss_tile[0, 0], ("x", "y", "c"))

    out = {"w_in": _adamw_summed(big["w_in"], w["w_in"], m["w_in"], v["w_in"], 64, "adamw_w_in")}
    for k in ("w_out", "w_gate", "w_up", "w_down"):
        part, recv = big[k]
        out[k] = _adamw_reduced(part, recv, chip, w[k], m[k], v[k], 128, "adamw_" + k)

    g_small = _unpack_small(_allreduce_small(_pack_small(**small), "small_grads_allreduce"))
    g_small["conv_w"] = lax.dynamic_slice_in_dim(g_small["conv_w"], me * (2 * D // NDEV), 2 * D // NDEV, axis=1)
    small_names = [k for k in WEIGHT_ORDER if k not in big_names]
    pad_conv = lambda a: jnp.pad(a, ((0, 0), (0, 2 * D - a.shape[1])))
    packs = []
    for src in (w, g_small, m, v):
        args = {k: (pad_conv(src[k]) if k == "conv_w" else src[k]) for k in small_names}
        packs.append(_pack_small(**args))
    d_pack, m_pack, v_pack = _adamw_plain(*packs[0:1], packs[1], packs[2], packs[3], name="adamw_small")
    cut_conv = lambda dct: {**dct, "conv_w": dct["conv_w"][:, :2 * D // NDEV]}
    d_small, m_small, v_small = (cut_conv(_unpack_small(p)) for p in (d_pack, m_pack, v_pack))

    def shaped(k, a):
        return a.reshape(w_shapes[k])

    w_shapes = dict(norm1_w=norm1_w.shape, w_in=w_in.shape, conv_w=conv_w.shape, a_log=a_log.shape, dt_bias=dt_bias.shape,
                    gdn_norm_w=gdn_norm_w.shape, ret_norm_w=ret_norm_w.shape, w_out=w_out.shape, norm2_w=norm2_w.shape,
                    w_gate=w_gate.shape, w_up=w_up.shape, w_down=w_down.shape, norm_f_w=norm_f_w.shape)
    grads, deltas, new_m, new_v = [], [], [], []
    for k in WEIGHT_ORDER:
        if k in big_names:
            g_, d_, m_, v_ = out[k]
        else:
            g_, d_, m_, v_ = g_small[k], d_small[k], m_small[k], v_small[k]
        grads.append(shaped(k, g_))
        deltas.append(shaped(k, d_))
        new_m.append(shaped(k, m_))
        new_v.append(shaped(k, v_))
    return (loss, dx[None], *grads, *deltas, *new_m, *new_v)
```

```python
import functools
import numpy as np
import jax
import jax.numpy as jnp
from jax import lax
from jax.experimental import pallas as pl
from jax.experimental.pallas import tpu as pltpu

F32, BF16 = jnp.float32, jnp.bfloat16
HI = lax.Precision.HIGHEST
MESH = pl.DeviceIdType.MESH
ANY = pl.BlockSpec(memory_space=pl.ANY)
VMEM_FULL = pl.BlockSpec(memory_space=pltpu.VMEM)

NDEV = 8
D = 2048
H = 8
DK = 128
DV = 256
C = 64
CPB = 4
EPS = 1e-6
ROPE_BASE = 10000.0
N_IN = 16400
O_MERGE, O_QKV, O_RET, O_BA, P_IN = 0, 8192, 12288, 16384, 16640
MERGE_W, RET_W = 4 * DV, 2 * DK + DV
LR, B1, B2, EPS_ADAM, WD, STEP = 0.001, 0.9, 0.999, 1e-08, 0.01, 10
VMEM_CAP = 60 * 1024 * 1024

NN = ((1,), (0,))
NT = ((1,), (1,))
TN = ((0,), (0,))


def _params(sem=None, est=None):
    kw = {}
    if sem is not None:
        kw["dimension_semantics"] = sem
    if est is not None:
        kw["vmem_limit_bytes"] = int(min(VMEM_CAP, max(32 * 1024 * 1024, est * 5 // 4 + (4 << 20))))
    return pltpu.CompilerParams(**kw)


def _sds(shape, dt=F32):
    return jax.ShapeDtypeStruct(tuple(shape), dt)


def _row_tile(rows, limit):
    return max(t for t in range(16, min(rows, limit) + 1, 16) if rows % t == 0)


def _bdot(a, b, dims):
    return lax.dot_general(a.astype(BF16), b.astype(BF16), (dims, ((), ())), preferred_element_type=F32)


def _hdot(a, b, dims):
    return lax.dot_general(a, b, (dims, ((), ())), precision=HI, preferred_element_type=F32)


def _silu(x):
    return x * jax.nn.sigmoid(x)


def _rms(x, w):
    return x * lax.rsqrt(jnp.mean(x * x, axis=-1, keepdims=True) + EPS) * w


class _Side:
    def __init__(self, ins, out_shapes, sems, start, finish):
        self.ins, self.out_shapes, self.sems, self.start, self.finish = list(ins), list(out_shapes), list(sems), start, finish


def _run_side(side, name):
    ni, no = len(side.ins), len(side.out_shapes)

    def body(*refs):
        ins, outs, sems = refs[:ni], refs[ni:ni + no], refs[ni + no:]
        side.start(ins, outs, *sems)
        side.finish(ins, outs, *sems)

    return pl.pallas_call(body, in_specs=[ANY] * ni, out_specs=[ANY] * no, out_shape=side.out_shapes,
                          scratch_shapes=[pltpu.SemaphoreType.DMA(s) for s in side.sems], name=name)(*side.ins)


def _matmul(a, b, *, ta=False, tb=False, tm, tn, tk, out_dtype=F32, res=None, side=None, name):
    m = a.shape[1] if ta else a.shape[0]
    k = a.shape[0] if ta else a.shape[1]
    n = b.shape[0] if tb else b.shape[1]
    assert k == (b.shape[1] if tb else b.shape[0])
    tm, tn, tk = min(tm, m), min(tn, n), min(tk, k)
    assert m % tm == 0 and n % tn == 0 and k % tk == 0, (name, m, n, k, tm, tn, tk)
    nk = k // tk
    dims = ((0 if ta else 1,), (1 if tb else 0,))
    has_res = res is not None
    n_in = 3 if has_res else 2
    n_side_in = len(side.ins) if side else 0
    n_side_out = len(side.out_shapes) if side else 0
    grid = (m // tm, n // tn, nk)

    def body(*refs):
        a_ref, b_ref = refs[0], refs[1]
        r_ref = refs[2] if has_res else None
        o_ref = refs[n_in + n_side_in]
        if side:
            side_ins = refs[n_in:n_in + n_side_in]
            side_outs = refs[n_in + n_side_in + 1:n_in + n_side_in + 1 + n_side_out]
            side_sems = refs[len(refs) - len(side.sems):]
            step = (pl.program_id(0) * grid[1] + pl.program_id(1)) * grid[2] + pl.program_id(2)

            @pl.when(step == 0)
            def _():
                side.start(side_ins, side_outs, *side_sems)

        def finish(acc):
            if has_res:
                acc = acc + r_ref[...].astype(F32)
            o_ref[...] = acc.astype(out_dtype)

        part = _bdot(a_ref[...], b_ref[...], dims)
        if nk == 1:
            finish(part)
        else:
            acc_ref = refs[n_in + n_side_in + 1 + n_side_out]
            kk = pl.program_id(2)

            @pl.when(kk == 0)
            def _():
                acc_ref[...] = part

            @pl.when(kk > 0)
            def _():
                acc_ref[...] += part

            @pl.when(kk == nk - 1)
            def _():
                finish(acc_ref[...])

        if side:
            @pl.when(step == grid[0] * grid[1] * grid[2] - 1)
            def _():
                side.finish(side_ins, side_outs, *side_sems)

    a_spec = pl.BlockSpec((tk, tm), lambda i, j, kk: (kk, i)) if ta else pl.BlockSpec((tm, tk), lambda i, j, kk: (i, kk))
    b_spec = pl.BlockSpec((tn, tk), lambda i, j, kk: (j, kk)) if tb else pl.BlockSpec((tk, tn), lambda i, j, kk: (kk, j))
    o_spec = pl.BlockSpec((tm, tn), lambda i, j, kk: (i, j))
    in_specs = [a_spec, b_spec] + ([o_spec] if has_res else []) + [ANY] * n_side_in
    est = 2 * (tm * tk * a.dtype.itemsize + tk * tn * b.dtype.itemsize + tm * tn * jnp.dtype(out_dtype).itemsize)
    est += 2 * tm * tn * 4 * (1 if has_res else 0) + (tm * tn * 4 if nk > 1 else 0) + 2 * tm * tn * 4
    args = (a, b) + ((res,) if has_res else ()) + (tuple(side.ins) if side else ())
    scratch = ([pltpu.VMEM((tm, tn), F32)] if nk > 1 else []) + ([pltpu.SemaphoreType.DMA(s) for s in side.sems] if side else [])
    sem = ("arbitrary",) * 3 if side else ("parallel", "parallel", "arbitrary")
    out = pl.pallas_call(
        body, grid=grid, in_specs=in_specs, out_specs=[o_spec] + [ANY] * n_side_out,
        out_shape=[_sds((m, n), out_dtype)] + (side.out_shapes if side else []),
        scratch_shapes=scratch, name=name, compiler_params=_params(sem, est))(*args)
    return (out[0], out[1:]) if side else out[0]


def _rms_fwd(x, w, ts, name):
    s = x.shape[0]

    def body(x_ref, w_ref, o_ref, ot_ref):
        y = _rms(x_ref[...], w_ref[...]).astype(BF16)
        o_ref[...] = y
        ot_ref[...] = y.T

    row = pl.BlockSpec((ts, D), lambda i: (i, 0))
    return pl.pallas_call(body, grid=(s // ts,), in_specs=[row, pl.BlockSpec((1, D), lambda i: (0, 0))],
                          out_specs=[row, pl.BlockSpec((D, ts), lambda i: (0, i))],
                          out_shape=[_sds((s, D), BF16), _sds((D, s), BF16)], name=name,
                          compiler_params=_params(("parallel",)))(x, w)


def _rms_bwd(x, w, du, dres, ts, name):
    s = x.shape[0]

    def body(x_ref, w_ref, du_ref, dres_ref, dx_ref, dw_ref):
        _, vjp = jax.vjp(_rms, x_ref[...], w_ref[...])
        dx, dw = vjp(du_ref[...].astype(F32))
        dx_ref[...] = dx + dres_ref[...]

        @pl.when(pl.program_id(0) == 0)
        def _():
            dw_ref[...] = jnp.zeros_like(dw_ref)

        dw_ref[...] += dw

    row = pl.BlockSpec((ts, D), lambda i: (i, 0))
    vec = pl.BlockSpec((1, D), lambda i: (0, 0))
    return pl.pallas_call(body, grid=(s // ts,), in_specs=[row, vec, row, row], out_specs=[row, vec],
                          out_shape=[_sds((s, D)), _sds((1, D))], name=name,
                          compiler_params=_params(("arbitrary",), 12 * ts * D * 4))(x, w, du, dres)


def _conv_taps(xx, w, base, ts):
    acc = xx[base:base + ts] * w[0:1, :]
    for j in range(1, 4):
        acc = acc + xx[base + j:base + j + ts] * w[j:j + 1, :]
    return acc


def _causal_conv(prev8, cur, w, first):
    xx = jnp.concatenate([jnp.where(first, 0.0, prev8), cur], axis=0)
    return _conv_taps(xx, w, 5, cur.shape[0])


def _qk_post(c, scale):
    s = _silu(c)
    return s * lax.rsqrt(jnp.sum(s * s, axis=-1, keepdims=True) + EPS) * scale


def _conv_specs(ts, cw, col0):
    pcol = O_QKV // cw + col0
    cur = pl.BlockSpec((ts, cw), lambda j, i: (i, pcol + j))
    prev = pl.BlockSpec((8, cw), lambda j, i: (jnp.maximum(i * (ts // 8) - 1, 0), pcol + j))
    wsp = pl.BlockSpec((4, cw), lambda j, i: (0, col0 + j))
    return cur, prev, wsp


def _gdn_qkv_fwd(proj, conv_w, ts):
    s = proj.shape[0]

    def qk_body(cur_ref, prev_ref, w_ref, o_ref):
        c = _causal_conv(prev_ref[...], cur_ref[...], w_ref[...], pl.program_id(1) == 0)
        scale = jnp.where(pl.program_id(0) < H, DK ** -0.5, 1.0).astype(F32)
        o_ref[...] = _qk_post(c, scale)

    cur, prev, wsp = _conv_specs(ts, DK, 0)
    qk = pl.pallas_call(qk_body, grid=(2 * H, s // ts), in_specs=[cur, prev, wsp],
                        out_specs=pl.BlockSpec((ts, DK), lambda j, i: (i, j)), out_shape=_sds((s, 2 * H * DK)),
                        name="gdn_qk_prep", compiler_params=_params(("parallel", "parallel")))(proj, proj, conv_w)

    def v_body(cur_ref, prev_ref, w_ref, o_ref):
        o_ref[...] = _silu(_causal_conv(prev_ref[...], cur_ref[...], w_ref[...], pl.program_id(1) == 0))

    cw = 512
    cur, prev, wsp = _conv_specs(ts, cw, 2 * H * DK // cw)
    v = pl.pallas_call(v_body, grid=(H * DV // cw, s // ts), in_specs=[cur, prev, wsp],
                       out_specs=pl.BlockSpec((ts, cw), lambda j, i: (i, j)), out_shape=_sds((s, H * DV)),
                       name="gdn_v_prep", compiler_params=_params(("parallel", "parallel")))(proj, proj, conv_w)
    return qk, v


def _gdn_qkv_bwd(proj, conv_w, dqk, dv, dproj, ts):
    s = proj.shape[0]
    nt = s // ts

    def qk_body(cur_ref, prev_ref, w_ref, d_ref, o_ref):
        c = _causal_conv(prev_ref[...], cur_ref[...], w_ref[...], pl.program_id(1) == 0)
        scale = jnp.where(pl.program_id(0) < H, DK ** -0.5, 1.0).astype(F32)
        _, vjp = jax.vjp(lambda cc: _qk_post(cc, scale), c)
        o_ref[...] = vjp(d_ref[...])[0]

    cur, prev, wsp = _conv_specs(ts, DK, 0)
    dc_qk = pl.pallas_call(qk_body, grid=(2 * H, nt), in_specs=[cur, prev, wsp, pl.BlockSpec((ts, DK), lambda j, i: (i, j))],
                           out_specs=pl.BlockSpec((ts, DK), lambda j, i: (i, j)), out_shape=_sds((s, 2 * H * DK)),
                           name="gdn_qk_prep_bwd", compiler_params=_params(("parallel", "parallel")))(proj, proj, conv_w, dqk)

    def v_body(cur_ref, prev_ref, w_ref, d_ref, o_ref):
        c = _causal_conv(prev_ref[...], cur_ref[...], w_ref[...], pl.program_id(1) == 0)
        _, vjp = jax.vjp(_silu, c)
        o_ref[...] = vjp(d_ref[...])[0]

    cw = 512
    cur, prev, wsp = _conv_specs(ts, cw, 2 * H * DK // cw)
    dc_v = pl.pallas_call(v_body, grid=(H * DV // cw, nt), in_specs=[cur, prev, wsp, pl.BlockSpec((ts, cw), lambda j, i: (i, j))],
                          out_specs=pl.BlockSpec((ts, cw), lambda j, i: (i, j)), out_shape=_sds((s, H * DV)),
                          name="gdn_v_prep_bwd", compiler_params=_params(("parallel", "parallel")))(proj, proj, conv_w, dv)

    def conv_bwd(dc, dproj, col0, ncols, name):
        def body(x_ref, xprev_ref, w_ref, dc_ref, dcnext_ref, _, da_ref, dw_ref):
            i = pl.program_id(1)
            w = w_ref[...]
            dcur = dc_ref[...]
            dd = jnp.concatenate([dcur, jnp.where(i == nt - 1, 0.0, dcnext_ref[...])], axis=0)
            acc = dd[3:3 + ts] * w[0:1, :]
            for j in range(1, 4):
                acc = acc + dd[3 - j:3 - j + ts] * w[j:j + 1, :]
            da_ref[...] = acc.astype(BF16)
            xx = jnp.concatenate([jnp.where(i == 0, 0.0, xprev_ref[...]), x_ref[...]], axis=0)

            @pl.when(i == 0)
            def _():
                dw_ref[...] = jnp.zeros_like(dw_ref)

            for j in range(4):
                dw_ref[j:j + 1, :] += jnp.sum(dcur * xx[5 + j:5 + j + ts], axis=0, keepdims=True)

        cur, prev, wsp = _conv_specs(ts, cw, col0)
        dcur = pl.BlockSpec((ts, cw), lambda j, i: (i, j))
        dnext = pl.BlockSpec((8, cw), lambda j, i: (jnp.minimum((i + 1) * (ts // 8), s // 8 - 1), j))
        pcol = O_QKV // cw + col0
        return pl.pallas_call(body, grid=(ncols // cw, nt), in_specs=[cur, prev, wsp, dcur, dnext, ANY],
                              out_specs=[pl.BlockSpec((ts, cw), lambda j, i: (i, pcol + j)), pl.BlockSpec((4, cw), lambda j, i: (0, j))],
                              out_shape=[_sds(dproj.shape, BF16), _sds((4, ncols))], input_output_aliases={5: 0}, name=name,
                              compiler_params=_params(("parallel", "arbitrary")))(proj, proj, conv_w, dc, dc, dproj)

    dproj, dw_qk = conv_bwd(dc_qk, dproj, 0, 2 * H * DK, "conv_bwd_qk")
    dproj, dw_v = conv_bwd(dc_v, dproj, 2 * H * DK // cw, H * DV, "conv_bwd_v")
    return dproj, jnp.concatenate([dw_qk, dw_v], axis=1)


def _bg(b, a, alog, dtb):
    n = b.shape[0]
    g = -jnp.exp(alog) * jax.nn.softplus(a + dtb)
    row = lax.broadcasted_iota(jnp.int32, (n, n), 0)
    col = lax.broadcasted_iota(jnp.int32, (n, n), 1)
    shift = C.bit_length() - 1
    same = (row >> shift) == (col >> shift)
    return jax.nn.sigmoid(b), _hdot((same & (row >= col)).astype(F32), g, NN), _hdot(same.astype(F32), g, NN)


def _bg_fwd(proj, alog, dtb, ts):
    s = proj.shape[0]

    def body(ba_ref, alog_ref, dtb_ref, beta_ref, gc_ref, gl_ref):
        beta_ref[...], gc_ref[...], gl_ref[...] = _bg(ba_ref[:, 0:H], ba_ref[:, H:2 * H], alog_ref[...], dtb_ref[...])

    small = pl.BlockSpec((1, H), lambda i: (0, 0))
    out = pl.BlockSpec((ts, H), lambda i: (i, 0))
    return pl.pallas_call(body, grid=(s // ts,), in_specs=[pl.BlockSpec((ts, 256), lambda i: (i, O_BA // 256)), small, small],
                          out_specs=[out] * 3, out_shape=[_sds((s, H))] * 3, name="gdn_bg_prep",
                          compiler_params=_params(("parallel",)))(proj, alog, dtb)


def _bg_bwd(proj, alog, dtb, dbeta_h, dgc_h, dgl_h, dproj, ts):
    s = proj.shape[0]

    def body(ba_ref, alog_ref, dtb_ref, dbeta_ref, dgc_ref, dgl_ref, _, dba_ref, dalog_ref, ddtb_ref):
        _, vjp = jax.vjp(_bg, ba_ref[:, 0:H], ba_ref[:, H:2 * H], alog_ref[...], dtb_ref[...])
        db, da, dalog, ddtb = vjp((jnp.sum(dbeta_ref[...], axis=0), jnp.sum(dgc_ref[...], axis=0), jnp.sum(dgl_ref[...], axis=0)))
        dba_ref[...] = jnp.zeros_like(dba_ref)
        dba_ref[:, 0:H] = db.astype(BF16)
        dba_ref[:, H:2 * H] = da.astype(BF16)

        @pl.when(pl.program_id(0) == 0)
        def _():
            dalog_ref[...] = jnp.zeros_like(dalog_ref)
            ddtb_ref[...] = jnp.zeros_like(ddtb_ref)

        dalog_ref[...] += dalog
        ddtb_ref[...] += ddtb

    small = pl.BlockSpec((1, H), lambda i: (0, 0))
    per_head = pl.BlockSpec((H, ts, H), lambda i: (0, i, 0))
    return pl.pallas_call(body, grid=(s // ts,),
                          in_specs=[pl.BlockSpec((ts, 256), lambda i: (i, O_BA // 256)), small, small, per_head, per_head, per_head, ANY],
                          out_specs=[pl.BlockSpec((ts, 256), lambda i: (i, O_BA // 256)), small, small],
                          out_shape=[_sds(dproj.shape, BF16), _sds((1, H)), _sds((1, H))], input_output_aliases={6: 0},
                          name="gdn_bg_prep_bwd", compiler_params=_params(("arbitrary",)))(proj, alog, dtb, dbeta_h, dgc_h, dgl_h, dproj)


BLK = 4 * C
NNB, NTB, TNB = ((2,), (1,)), ((2,), (2,)), ((1,), (1,))


def _bdot_b(a, b, dims):
    return lax.dot_general(a.astype(BF16), b.astype(BF16), (dims, ((0,), (0,))), preferred_element_type=F32)


@jax.custom_vjp
def _inv_unit_lower(a):
    n = a.shape[-1]
    row = lax.broadcasted_iota(jnp.int32, (n, n), 0)
    col = lax.broadcasted_iota(jnp.int32, (n, n), 1)
    x = jnp.where(row == col, 1.0, 0.0).astype(F32) - a
    p = _bdot_b(a, a, NNB)
    power = 2
    while True:
        x = x + _bdot_b(x, p, NNB)
        power *= 2
        if power >= C:
            return x
        p = _bdot_b(p, p, NNB)


def _inv_fwd(a):
    t = _inv_unit_lower(a)
    return t, t


def _inv_bwd(t, dt):
    return (-_bdot_b(_bdot_b(t, dt, TNB), t, NTB),)


_inv_unit_lower.defvjp(_inv_fwd, _inv_bwd)


def _gdn_prep(q, k, v, bfull, gcfull, glfull, hmask):
    nb, n = q.shape[0], q.shape[1]
    beta = jnp.sum(bfull * hmask, axis=-1, keepdims=True)
    gc = jnp.sum(gcfull * hmask, axis=-1, keepdims=True)
    gl = jnp.sum(glfull * hmask, axis=-1, keepdims=True)
    row = lax.broadcasted_iota(jnp.int32, (n, n), 0)
    col = lax.broadcasted_iota(jnp.int32, (n, n), 1)
    shift = C.bit_length() - 1
    same = (row >> shift) == (col >> shift)
    incl, strict = same & (row >= col), same & (row > col)
    g_i = gc * jnp.ones((1, 1, n), F32)
    decay = jnp.exp(jnp.where(incl, g_i - jnp.swapaxes(g_i, 1, 2), -jnp.inf))
    kb = k * beta
    a = jnp.where(strict, _bdot_b(kb, k, NTB) * decay, 0.0)
    tinv = _inv_unit_lower(a)
    u = _bdot_b(tinv, v * beta, NNB)
    w = _bdot_b(tinv, kb * jnp.exp(gc), NNB)
    attn = _bdot_b(q, k, NTB) * decay
    fold = ((lax.broadcasted_iota(jnp.int32, (n, C), 0) & (C - 1)) == lax.broadcasted_iota(jnp.int32, (n, C), 1)).astype(F32)
    attn_c = _bdot(attn.reshape(nb * n, n), fold, NN).reshape(nb, n, C)
    return u, w, attn_c, q * jnp.exp(gc), k * jnp.exp(gl - gc), jnp.exp(gl)


def _gdn_step(u, w, attn, qg, kd, egl, state):
    v_new = u - _bdot(w, state, NN)
    o = _bdot(qg, state, NN) + _bdot(attn, v_new, NN)
    return o, state * egl + _bdot(kd, v_new, TN)


def _head_mask(h):
    return (lax.broadcasted_iota(jnp.int32, (1, H), 1) == h).astype(F32)


PREP_BLOCKS = 2


def _gdn_prep_specs(r):
    small = pl.BlockSpec((r, H), lambda h, c: (c, 0))
    return [pl.BlockSpec((r, DK), lambda h, c: (c, h)), pl.BlockSpec((r, DK), lambda h, c: (c, H + h)),
            pl.BlockSpec((r, DV), lambda h, c: (c, h)), small, small, small]


def _blocked(ref):
    x = ref[...]
    return x.reshape(PREP_BLOCKS, BLK, x.shape[-1])


def _gdn_inter_specs(r):
    col = pl.BlockSpec((r, DK), lambda h, c: (c, h))
    return [pl.BlockSpec((r, DV), lambda h, c: (c, h)), col, pl.BlockSpec((1, r, C), lambda h, c: (h, c, 0)), col, col,
            pl.BlockSpec((1, r // C, 8, 128), lambda h, c: (h, c, 0, 0))]


def _gdn_prep_fwd(qk, v, beta, gc, gl):
    s = qk.shape[0]
    r = PREP_BLOCKS * BLK

    def body(q_ref, k_ref, v_ref, b_ref, gc_ref, gl_ref, u_ref, w_ref, attn_ref, qg_ref, kd_ref, egl_ref):
        u, w, attn, qg, kd, egl = _gdn_prep(_blocked(q_ref), _blocked(k_ref), _blocked(v_ref), _blocked(b_ref), _blocked(gc_ref),
                                            _blocked(gl_ref), _head_mask(pl.program_id(0)))
        u_ref[...] = u.reshape(r, DV)
        w_ref[...] = w.reshape(r, DK).astype(BF16)
        attn_ref[0] = attn.reshape(r, C).astype(BF16)
        qg_ref[...] = qg.reshape(r, DK).astype(BF16)
        kd_ref[...] = kd.reshape(r, DK).astype(BF16)
        egl = egl.reshape(r, 1)
        for j in range(r // C):
            egl_ref[0, j] = egl[j * C:j * C + 1, :] * jnp.ones((8, 128), F32)

    out_shape = [_sds((s, H * DV)), _sds((s, H * DK), BF16), _sds((H, s, C), BF16), _sds((s, H * DK), BF16),
                 _sds((s, H * DK), BF16), _sds((H, s // C, 8, 128))]
    return pl.pallas_call(body, grid=(H, s // r), in_specs=_gdn_prep_specs(r), out_specs=_gdn_inter_specs(r), out_shape=out_shape,
                          name="gdn_prep_fwd", compiler_params=_params(("parallel", "parallel")))(qk, qk, v, beta, gc, gl)


def _gdn_prep_bwd(qk, v, beta, gc, gl, du, dw, dattn, dqg, dkd, degl):
    s = qk.shape[0]
    r = PREP_BLOCKS * BLK

    def body(q_ref, k_ref, v_ref, b_ref, gc_ref, gl_ref, du_ref, dw_ref, dattn_ref, dqg_ref, dkd_ref, degl_ref,
             dq_ref, dk_ref, dv_ref, db_ref, dgc_ref, dgl_ref):
        hmask = _head_mask(pl.program_id(0))
        _, vjp = jax.vjp(lambda q, k, v, b, gc, gl: _gdn_prep(q, k, v, b, gc, gl, hmask), _blocked(q_ref), _blocked(k_ref),
                         _blocked(v_ref), _blocked(b_ref), _blocked(gc_ref), _blocked(gl_ref))
        rowid = lax.broadcasted_iota(jnp.int32, (r, 1), 0)
        degl = jnp.zeros((r, 1), F32)
        for j in range(r // C):
            degl = jnp.where(rowid == j * C, degl_ref[0, j, 0:1, 0:1], degl)
        dq, dk, dv, db, dgc, dgl = vjp((_blocked(du_ref), _blocked(dw_ref), _blocked(dattn_ref.at[0]), _blocked(dqg_ref),
                                        _blocked(dkd_ref), degl.reshape(PREP_BLOCKS, BLK, 1)))
        dq_ref[...] = dq.reshape(r, DK)
        dk_ref[...] = dk.reshape(r, DK)
        dv_ref[...] = dv.reshape(r, DV)
        db_ref[0] = db.reshape(r, H)
        dgc_ref[0] = dgc.reshape(r, H)
        dgl_ref[0] = dgl.reshape(r, H)

    col = pl.BlockSpec((r, DK), lambda h, c: (c, h))
    piece = pl.BlockSpec((1, r, H), lambda h, c: (h, c, 0))
    dq, dk, dv, db, dgc, dgl = pl.pallas_call(
        body, grid=(H, s // r), in_specs=_gdn_prep_specs(r) + _gdn_inter_specs(r),
        out_specs=[col, col, pl.BlockSpec((r, DV), lambda h, c: (c, h)), piece, piece, piece],
        out_shape=[_sds((s, H * DK)), _sds((s, H * DK)), _sds((s, H * DV))] + [_sds((H, s, H))] * 3,
        name="gdn_prep_bwd", compiler_params=_params(("parallel", "parallel"), 24 << 20))(
            qk, qk, v, beta, gc, gl, du, dw, dattn, dqg, dkd, degl)
    return jnp.concatenate([dq, dk], axis=1), dv, db, dgc, dgl


def _gdn_scan_specs(r, order):
    wide = pl.BlockSpec((r, H * DK), lambda c: (order(c), 0))
    return [pl.BlockSpec((r, H * DV), lambda c: (order(c), 0)), wide, pl.BlockSpec((H, r, C), lambda c: (0, order(c), 0)), wide, wide,
            pl.BlockSpec((H, r // C, 8, 128), lambda c: (0, order(c), 0, 0))]


def _gdn_scan_fwd(u, w, attn, qg, kd, egl):
    s = u.shape[0]
    r = CPB * C
    nb = s // r

    def body(u_ref, w_ref, attn_ref, qg_ref, kd_ref, egl_ref, o_ref, st_ref, state_ref):
        @pl.when(pl.program_id(0) == 0)
        def _():
            state_ref[...] = jnp.zeros_like(state_ref)

        for h in range(H):
            kc, vc = slice(h * DK, (h + 1) * DK), slice(h * DV, (h + 1) * DV)
            state = state_ref[h]
            for i in range(CPB):
                rs = slice(i * C, (i + 1) * C)
                st_ref[h, i] = state
                o, state = _gdn_step(u_ref[rs, vc], w_ref[rs, kc], attn_ref[h, rs, :], qg_ref[rs, kc], kd_ref[rs, kc],
                                     egl_ref[h, i, 0:1, 0:1], state)
                o_ref[rs, vc] = o
            state_ref[h] = state

    out_specs = [pl.BlockSpec((r, H * DV), lambda c: (c, 0)), pl.BlockSpec((H, CPB, DK, DV), lambda c: (0, c, 0, 0))]
    return pl.pallas_call(body, grid=(nb,), in_specs=_gdn_scan_specs(r, lambda c: c), out_specs=out_specs,
                          out_shape=[_sds((s, H * DV)), _sds((H, s // C, DK, DV))],
                          scratch_shapes=[pltpu.VMEM((H, DK, DV), F32)], name="gdn_scan_fwd",
                          compiler_params=_params(("arbitrary",), 24 << 20))(u, w, attn, qg, kd, egl)


def _gdn_scan_bwd(u, w, attn, qg, kd, egl, states, do):
    s = u.shape[0]
    r = CPB * C
    nb = s // r

    def body(u_ref, w_ref, attn_ref, qg_ref, kd_ref, egl_ref, st_ref, do_ref,
             du_ref, dw_ref, dattn_ref, dqg_ref, dkd_ref, degl_ref, dstate_ref):
        @pl.when(pl.program_id(0) == 0)
        def _():
            dstate_ref[...] = jnp.zeros_like(dstate_ref)

        for h in range(H):
            kc, vc = slice(h * DK, (h + 1) * DK), slice(h * DV, (h + 1) * DV)
            dstate = dstate_ref[h]
            for i in reversed(range(CPB)):
                rs = slice(i * C, (i + 1) * C)
                _, vjp = jax.vjp(_gdn_step, u_ref[rs, vc], w_ref[rs, kc].astype(F32), attn_ref[h, rs, :].astype(F32),
                                 qg_ref[rs, kc].astype(F32), kd_ref[rs, kc].astype(F32), egl_ref[h, i, 0:1, 0:1], st_ref[h, i])
                du, dw, dattn, dqg, dkd, degl, dstate = vjp((do_ref[rs, vc], dstate))
                du_ref[rs, vc] = du
                dw_ref[rs, kc] = dw
                dattn_ref[h, rs, :] = dattn
                dqg_ref[rs, kc] = dqg
                dkd_ref[rs, kc] = dkd
                degl_ref[h, i] = degl * jnp.ones((8, 128), F32)
            dstate_ref[h] = dstate

    rev = lambda c: nb - 1 - c
    in_specs = _gdn_scan_specs(r, rev) + [pl.BlockSpec((H, CPB, DK, DV), lambda c: (0, rev(c), 0, 0)),
                                          pl.BlockSpec((r, H * DV), lambda c: (rev(c), 0))]
    return pl.pallas_call(
        body, grid=(nb,), in_specs=in_specs, out_specs=_gdn_scan_specs(r, rev),
        out_shape=[_sds((s, H * DV)), _sds((s, H * DK)), _sds((H, s, C)), _sds((s, H * DK)), _sds((s, H * DK)),
                   _sds((H, s // C, 8, 128))],
        scratch_shapes=[pltpu.VMEM((H, DK, DV), F32)], name="gdn_scan_bwd",
        compiler_params=_params(("arbitrary",), 40 << 20))(u, w, attn, qg, kd, egl, states, do)


def _rot(x, cs, sn):
    return x * cs + pltpu.roll(x, DK // 2, 1) * sn


def _rot_t(d, cs, sn):
    return d * cs - pltpu.roll(d, DK // 2, 1) * sn


def _ret_chunk(q, k, v, state, lg):
    n = q.shape[0]
    row = lax.broadcasted_iota(jnp.int32, (n, n), 0)
    col = lax.broadcasted_iota(jnp.int32, (n, n), 1)
    dist = (row - col).astype(F32)
    dmat = jnp.exp(jnp.where(dist >= 0, dist * lg, -jnp.inf))
    scores = _bdot(q, k, NT) * dmat
    pos = lax.broadcasted_iota(jnp.int32, (n, 1), 0).astype(F32)
    xi = jnp.exp((pos + 1.0) * lg)
    zeta = jnp.exp((n - 1.0 - pos) * lg)
    o = _bdot(scores, v, NN) + _bdot(q, state, NN) * xi
    new_state = state * jnp.exp(n * lg) + _bdot(k * zeta, v, TN)
    return o, new_state


def _ret_specs(r, order):
    q0, v0 = O_RET // DK, (O_RET + 2 * DK) // DV
    return [pl.BlockSpec((r, DK), lambda h, c: (order(c), q0 + h * (RET_W // DK))),
            pl.BlockSpec((r, DK), lambda h, c: (order(c), q0 + 1 + h * (RET_W // DK))),
            pl.BlockSpec((r, DV), lambda h, c: (order(c), v0 + h * (RET_W // DV))), pl.BlockSpec((r, DK), lambda h, c: (order(c), 0)),
            pl.BlockSpec((r, DK), lambda h, c: (order(c), 0)), pl.BlockSpec((1, H), lambda h, c: (0, 0))]


RET_C = 256


def _ret_scan_fwd(proj, cs, sn, lgtab):
    s = proj.shape[0]
    r = min(RET_C, s)
    nb = s // r

    def body(q_ref, k_ref, v_ref, cs_ref, sn_ref, lg_ref, o_ref, st_ref, state_ref):
        @pl.when(pl.program_id(1) == 0)
        def _():
            state_ref[...] = jnp.zeros_like(state_ref)

        lg = jnp.sum(lg_ref[...] * _head_mask(pl.program_id(0)), axis=1, keepdims=True)
        state = state_ref[...]
        st_ref[0, 0] = state
        q = _rot(q_ref[...], cs_ref[...], sn_ref[...])
        k = _rot(k_ref[...], cs_ref[...], sn_ref[...]) * DK ** -0.5
        o_ref[...], state_ref[...] = _ret_chunk(q, k, v_ref[...], state, lg)

    out_specs = [pl.BlockSpec((r, DV), lambda h, c: (c, h)), pl.BlockSpec((1, 1, DK, DV), lambda h, c: (h, c, 0, 0))]
    return pl.pallas_call(body, grid=(H, nb), in_specs=_ret_specs(r, lambda c: c), out_specs=out_specs,
                          out_shape=[_sds((s, H * DV)), _sds((H, nb, DK, DV))],
                          scratch_shapes=[pltpu.VMEM((DK, DV), F32)], name="ret_scan_fwd",
                          compiler_params=_params(("parallel", "arbitrary")))(proj, proj, proj, cs, sn, lgtab)


def _ret_scan_bwd(proj, cs, sn, lgtab, states, do, dproj):
    s = proj.shape[0]
    r = min(RET_C, s)
    nb = s // r

    def body(q_ref, k_ref, v_ref, cs_ref, sn_ref, lg_ref, st_ref, do_ref, _, d_ref, dstate_ref):
        @pl.when(pl.program_id(1) == 0)
        def _():
            dstate_ref[...] = jnp.zeros_like(dstate_ref)

        lg = jnp.sum(lg_ref[...] * _head_mask(pl.program_id(0)), axis=1, keepdims=True)
        cs_, sn_ = cs_ref[...], sn_ref[...]
        q = _rot(q_ref[...], cs_, sn_)
        k = _rot(k_ref[...], cs_, sn_) * DK ** -0.5
        _, vjp = jax.vjp(lambda q, k, v, st: _ret_chunk(q, k, v, st, lg), q, k, v_ref[...], st_ref[0, 0])
        dq, dk, dv, dstate_ref[...] = vjp((do_ref[...], dstate_ref[...]))
        d_ref[:, 0:DK] = _rot_t(dq, cs_, sn_).astype(BF16)
        d_ref[:, DK:2 * DK] = _rot_t(dk * DK ** -0.5, cs_, sn_).astype(BF16)
        d_ref[:, 2 * DK:RET_W] = dv.astype(BF16)

    rev = lambda c: nb - 1 - c
    in_specs = _ret_specs(r, rev) + [pl.BlockSpec((1, 1, DK, DV), lambda h, c: (h, rev(c), 0, 0)),
                                     pl.BlockSpec((r, DV), lambda h, c: (rev(c), h)), ANY]
    return pl.pallas_call(
        body, grid=(H, nb), in_specs=in_specs, out_specs=pl.BlockSpec((r, RET_W), lambda h, c: (rev(c), O_RET // RET_W + h)),
        out_shape=_sds(dproj.shape, BF16), input_output_aliases={8: 0},
        scratch_shapes=[pltpu.VMEM((DK, DV), F32)], name="ret_scan_bwd",
        compiler_params=_params(("parallel", "arbitrary")))(proj, proj, proj, cs, sn, lgtab, states, do, dproj)


def _merge(oa, z, ob, rg, ga, gb, wa, wb):
    ya = oa * lax.rsqrt(jnp.mean(oa * oa, axis=-1, keepdims=True) + EPS) * wa * _silu(z)
    mu = jnp.mean(ob, axis=-1, keepdims=True)
    var = jnp.mean(jnp.square(ob - mu), axis=-1, keepdims=True)
    yb = (ob - mu) * lax.rsqrt(var + EPS) * wb * _silu(rg)
    return jax.nn.sigmoid(ga) * ya + jax.nn.sigmoid(gb) * yb


def _merge_specs(ts):
    own = pl.BlockSpec((ts, DV), lambda h, i: (i, h))
    grp = lambda k: pl.BlockSpec((ts, DV), lambda h, i: (i, O_MERGE // DV + 4 * h + k))
    return [own, grp(0), own, grp(1), grp(2), grp(3),
            pl.BlockSpec((1, DV), lambda h, i: (0, 0)), pl.BlockSpec((1, DV), lambda h, i: (0, h))]


def _merge_fwd(oa, ob, proj, wa, wb, ts):
    s = oa.shape[0]

    def body(oa_ref, z_ref, ob_ref, rg_ref, ga_ref, gb_ref, wa_ref, wb_ref, o_ref, ot_ref):
        y = _merge(oa_ref[...], z_ref[...], ob_ref[...], rg_ref[...], ga_ref[...], gb_ref[...],
                   wa_ref[...], wb_ref[...]).astype(BF16)
        o_ref[...] = y
        ot_ref[...] = y.T

    return pl.pallas_call(body, grid=(H, s // ts), in_specs=_merge_specs(ts),
                          out_specs=[pl.BlockSpec((ts, DV), lambda h, i: (i, h)), pl.BlockSpec((DV, ts), lambda h, i: (h, i))],
                          out_shape=[_sds((s, H * DV), BF16), _sds((H * DV, s), BF16)], name="merge_fwd",
                          compiler_params=_params(("parallel", "parallel")))(oa, proj, ob, proj, proj, proj, wa, wb)


def _merge_bwd(oa, ob, proj, wa, wb, dmixed, ts):
    s = oa.shape[0]

    def body(oa_ref, z_ref, ob_ref, rg_ref, ga_ref, gb_ref, wa_ref, wb_ref, dm_ref,
             doa_ref, dob_ref, dgrp_ref, dwa_ref, dwb_ref):
        _, vjp = jax.vjp(_merge, oa_ref[...], z_ref[...], ob_ref[...], rg_ref[...], ga_ref[...], gb_ref[...],
                         wa_ref[...], wb_ref[...])
        doa, dz, dob, drg, dga, dgb, dwa, dwb = vjp(dm_ref[...].astype(F32))
        doa_ref[...] = doa
        dob_ref[...] = dob
        for k, d in enumerate((dz, drg, dga, dgb)):
            dgrp_ref[:, k * DV:(k + 1) * DV] = d.astype(BF16)
        first_tile = pl.program_id(1) == 0

        @pl.when(first_tile & (pl.program_id(0) == 0))
        def _():
            dwa_ref[...] = jnp.zeros_like(dwa_ref)

        @pl.when(first_tile)
        def _():
            dwb_ref[...] = jnp.zeros_like(dwb_ref)

        dwa_ref[...] += dwa
        dwb_ref[...] += dwb

    blk = pl.BlockSpec((ts, DV), lambda h, i: (i, h))
    out_specs = [blk, blk, pl.BlockSpec((ts, MERGE_W), lambda h, i: (i, O_MERGE // MERGE_W + h)),
                 pl.BlockSpec((1, DV), lambda h, i: (0, 0)), pl.BlockSpec((1, DV), lambda h, i: (0, h))]
    out_shape = [_sds((s, H * DV)), _sds((s, H * DV)), _sds((s, P_IN), BF16), _sds((1, DV)), _sds((1, H * DV))]
    return pl.pallas_call(body, grid=(H, s // ts), in_specs=_merge_specs(ts) + [blk], out_specs=out_specs, out_shape=out_shape,
                          name="merge_bwd", compiler_params=_params(("arbitrary", "arbitrary"), 40 * ts * DV * 4))(
                              oa, proj, ob, proj, proj, proj, wa, wb, dmixed)


def _act(hg, hu):
    return _silu(hg) * hu


def _ffn_gate_up(hn, w_gate, w_up, tm, tn):
    s, f = hn.shape[0], w_gate.shape[1]
    tm, tn = min(tm, s), min(tn, f)
    assert s % tm == 0 and f % tn == 0 and tm % 256 == 0
    sub = tm // 2

    def body(a_ref, wg_ref, wu_ref, hg_ref, hu_ref, act_ref, actt_ref):
        for r0 in range(0, tm, sub):
            rs = slice(r0, r0 + sub)
            hg = _bdot(a_ref[rs, :], wg_ref[...], NN)
            hu = _bdot(a_ref[rs, :], wu_ref[...], NN)
            y = _act(hg, hu).astype(BF16)
            hg_ref[rs, :] = hg.astype(BF16)
            hu_ref[rs, :] = hu.astype(BF16)
            act_ref[rs, :] = y
            actt_ref[:, rs] = y.T

    wsp = pl.BlockSpec((D, tn), lambda i, j: (0, j))
    blk = pl.BlockSpec((tm, tn), lambda i, j: (i, j))
    est = 2 * (tm * D * 2 + 2 * D * tn * 2 + 4 * tm * tn * 2) + 4 * sub * tn * 4
    return pl.pallas_call(body, grid=(s // tm, f // tn), in_specs=[pl.BlockSpec((tm, D), lambda i, j: (i, 0)), wsp, wsp],
                          out_specs=[blk, blk, blk, pl.BlockSpec((tn, tm), lambda i, j: (j, i))],
                          out_shape=[_sds((s, f), BF16)] * 3 + [_sds((f, s), BF16)], name="ffn_gate_up",
                          compiler_params=_params(("parallel", "parallel"), est))(hn, w_gate, w_up)


def _ffn_down_dx(dh2, w_down, hg, hu, tm, tn):
    s, f = hg.shape
    tm, tn = min(tm, s), min(tn, f)
    assert s % tm == 0 and f % tn == 0 and tm % 256 == 0
    sub = tm // 2

    def body(d_ref, w_ref, hg_ref, hu_ref, dhg_ref, dhu_ref):
        for r0 in range(0, tm, sub):
            rs = slice(r0, r0 + sub)
            dact = _bdot(d_ref[rs, :], w_ref[...], NT)
            _, vjp = jax.vjp(_act, hg_ref[rs, :].astype(F32), hu_ref[rs, :].astype(F32))
            dhg, dhu = vjp(dact)
            dhg_ref[rs, :] = dhg.astype(BF16)
            dhu_ref[rs, :] = dhu.astype(BF16)

    blk = pl.BlockSpec((tm, tn), lambda i, j: (i, j))
    est = 2 * (tm * D * 4 + tn * D * 2 + 4 * tm * tn * 2) + 6 * sub * tn * 4
    return pl.pallas_call(body, grid=(s // tm, f // tn),
                          in_specs=[pl.BlockSpec((tm, D), lambda i, j: (i, 0)), pl.BlockSpec((tn, D), lambda i, j: (j, 0)), blk, blk],
                          out_specs=[blk, blk], out_shape=[_sds((s, f), BF16)] * 2, name="ffn_down_dx",
                          compiler_params=_params(("parallel", "parallel"), est))(dh2, w_down, hg, hu)


def _loss_rows(h2, wf, tgt):
    err = _rms(h2, wf) - tgt
    return 0.5 * jnp.sum(jnp.mean(err * err, axis=-1, keepdims=True), keepdims=True)


def _loss_fwd_bwd(h2, wf, tgt, ts):
    s = h2.shape[0]

    def body(h_ref, w_ref, t_ref, loss_ref, dh_ref, dw_ref):
        loss, vjp = jax.vjp(_loss_rows, h_ref[...], w_ref[...], t_ref[...])
        dh, dw, _ = vjp(jnp.ones((1, 1), F32))
        dh_ref[...] = dh

        @pl.when(pl.program_id(0) == 0)
        def _():
            loss_ref[...] = jnp.zeros_like(loss_ref)
            dw_ref[...] = jnp.zeros_like(dw_ref)

        loss_ref[...] += loss
        dw_ref[...] += dw

    row = pl.BlockSpec((ts, D), lambda i: (i, 0))
    vec = pl.BlockSpec((1, D), lambda i: (0, 0))
    tile = pl.BlockSpec((8, 128), lambda i: (0, 0))
    return pl.pallas_call(body, grid=(s // ts,), in_specs=[row, vec, row], out_specs=[tile, row, vec],
                          out_shape=[_sds((8, 128)), _sds((s, D)), _sds((1, D))], name="final_norm_loss",
                          compiler_params=_params(("arbitrary",), 12 * ts * D * 4))(h2, wf, tgt)


def _rope_tables(s):
    inv = ROPE_BASE ** (-jnp.arange(0, DK, 2, dtype=F32) / DK)
    ang = jnp.arange(s, dtype=F32)[:, None] * inv[None, :]
    cos, sin = jnp.cos(ang), jnp.sin(ang)
    return jnp.concatenate([cos, cos], axis=1), jnp.concatenate([-sin, sin], axis=1)


def _local_step(x, tgt, w_in, w_out, w_gate, w_up, w_down, norm1_w, conv_w, a_log, dt_bias, gdn_norm_w, ret_norm_w, norm2_w, norm_f_w,
                dist=None):
    s = x.shape[0]
    ts = min(512, s)
    cs, sn = _rope_tables(s)
    lgtab = jnp.log1p(-jnp.exp2(-5.0 - jnp.arange(H, dtype=F32))).reshape(1, H)

    u, u_t = _rms_fwd(x, norm1_w, ts, "norm1_fwd")
    if dist is None:
        proj = _matmul(u, w_in, tm=1024, tn=1280, tk=D, name="in_proj")
    else:
        proj, gathered = _matmul(u, w_in, tm=1024, tn=1280, tk=D, side=_gather_side(dist["shards"]), name="in_proj")
        w_out, w_gate, w_up, w_down = (gathered[0].reshape(D, D), _from_slots_cols(gathered[1]), _from_slots_cols(gathered[2]),
                                       gathered[3].reshape(-1, D))
    qk, va = _gdn_qkv_fwd(proj, conv_w, ts)
    beta, gc, gl = _bg_fwd(proj, a_log, dt_bias, ts)
    inter = _gdn_prep_fwd(qk, va, beta, gc, gl)
    oa, st_a = _gdn_scan_fwd(*inter)
    ob, st_b = _ret_scan_fwd(proj, cs, sn, lgtab)
    mixed, mixed_t = _merge_fwd(oa, ob, proj, gdn_norm_w, ret_norm_w, ts)
    h1 = _matmul(mixed, w_out, tm=1024, tn=1024, tk=D, res=x, name="out_proj")
    hn, hn_t = _rms_fwd(h1, norm2_w, ts, "norm2_fwd")
    hg, hu, act, act_t = _ffn_gate_up(hn, w_gate, w_up, 512, 1408)
    h2 = _matmul(act, w_down, tm=1024, tn=1024, tk=1408, res=h1, name="ffn_down")
    loss, dh2, d_norm_f = _loss_fwd_bwd(h2, norm_f_w, tgt, ts)

    dhg, dhu = _ffn_down_dx(dh2, w_down, hg, hu, 512, 1408)
    g_down = _matmul(act_t, dh2, tm=1408, tn=1024, tk=D, out_dtype=BF16, name="ffn_down_dw")
    g_gate = _matmul(hn_t, dhg, tm=1024, tn=1408, tk=D, out_dtype=BF16, name="ffn_gate_dw")
    g_up = _matmul(hn_t, dhu, tm=1024, tn=1408, tk=D, out_dtype=BF16, name="ffn_up_dw")
    dhn = _matmul(dhg, w_gate, tb=True, tm=1024, tn=1024, tk=1408, name="ffn_gate_dx")
    dhn = _matmul(dhu, w_up, tb=True, tm=1024, tn=1024, tk=1408, res=dhn, name="ffn_up_dx")
    dh1, d_norm2 = _rms_bwd(h1, norm2_w, dhn, dh2, ts, "norm2_bwd")

    g_out = _matmul(mixed_t, dh1, tm=1024, tn=1024, tk=D, out_dtype=BF16, name="out_proj_dw")
    early = ["w_out", "w_gate", "w_up", "w_down"]
    if dist is None:
        dmixed = _matmul(dh1, w_out, tb=True, tm=1024, tn=1024, tk=D, out_dtype=BF16, name="out_proj_dx")
    else:
        slots = dict(w_out=g_out.reshape(NDEV, D // NDEV, D), w_gate=_to_slots_cols(g_gate), w_up=_to_slots_cols(g_up),
                     w_down=g_down.reshape(NDEV, -1, D))
        dmixed, from_sibling = _matmul(dh1, w_out, tb=True, tm=1024, tn=1024, tk=D, out_dtype=BF16,
                                       side=_sibling_side([slots[k] for k in early]), name="out_proj_dx")
        parts = [_add_sibling(slots[k], r, dist["core"], 128, "grads_add_" + k) for k, r in zip(early, from_sibling)]
    doa, dob, dproj, d_gdn_norm, d_ret_norm = _merge_bwd(oa, ob, proj, gdn_norm_w, ret_norm_w, dmixed, ts)

    dproj = _ret_scan_bwd(proj, cs, sn, lgtab, st_b, dob, dproj)
    d_inter = _gdn_scan_bwd(*inter, st_a, doa)
    dqk, dva, dbeta_h, dgc_h, dgl_h = _gdn_prep_bwd(qk, va, beta, gc, gl, *d_inter)
    dproj, d_conv = _gdn_qkv_bwd(proj, conv_w, dqk, dva, dproj, ts)
    dproj, d_a_log, d_dt_bias = _bg_bwd(proj, a_log, dt_bias, dbeta_h, dgc_h, dgl_h, dproj, ts)

    if dist is None:
        g_in = _matmul(u_t, dproj, tm=1024, tn=1280, tk=D, out_dtype=BF16, name="in_proj_dw")
        du = _matmul(dproj, w_in, tb=True, tm=1024, tn=1024, tk=1664, name="in_proj_dx")
        big = dict(w_in=g_in, w_out=g_out, w_gate=g_gate, w_up=g_up, w_down=g_down)
    else:
        g_in, from_chips = _matmul(u_t, dproj, tm=1024, tn=1280, tk=D, out_dtype=BF16, side=_chips_side(parts), name="in_proj_dw")
        du, (from_all,) = _matmul(dproj, w_in, tb=True, tm=1024, tn=1024, tk=1664,
                                  side=_all_to_all_side(_to_slots_cols(_ungroup_w_in(g_in))), name="in_proj_dx")
        big = dict(w_in=from_all, **{k: (p, r) for k, p, r in zip(early, parts, from_chips)})
    dx, d_norm1 = _rms_bwd(x, norm1_w, du, dh1, ts, "norm1_bwd")

    small = dict(norm1_w=d_norm1, conv_w=d_conv, a_log=d_a_log, dt_bias=d_dt_bias, gdn_norm_w=d_gdn_norm,
                 ret_norm_w=d_ret_norm, norm2_w=d_norm2, norm_f_w=d_norm_f)
    return loss, dx, big, small


def _coords():
    return lax.axis_index("x"), lax.axis_index("y"), lax.axis_index("c")


def _gather_side(shards):
    n = len(shards)

    def plan(ins, outs, send_sems, recv_sems, local_sems):
        x, y, c = _coords()
        me, sibling = (x, y, c), (x, y, 1 - c)
        chips = [(1 - x, y), (x, 1 - y), (1 - x, 1 - y)]

        def copy(a, k, block, to, src=None):
            px, py, pc = block
            dst = outs[a].at[4 * px + 2 * py + pc]
            return pltpu.make_async_remote_copy(src_ref=dst if src is None else src, dst_ref=dst, send_sem=send_sems.at[a, k],
                                                recv_sem=recv_sems.at[a, k], device_id=to, device_id_type=MESH)

        mine = [pltpu.make_async_copy(ins[a], outs[a].at[4 * x + 2 * y + c], local_sems.at[a]) for a in range(n)]
        first = []
        for a in range(n):
            first.append(copy(a, 0, me, sibling, src=ins[a]))
            first += [copy(a, 1 + j, me, (*chip, c), src=ins[a]) for j, chip in enumerate(chips)]
        return c, me, sibling, chips, copy, mine, first

    def start(ins, outs, *sems):
        *_, mine, first = plan(ins, outs, *sems)
        for cp in mine + first:
            cp.start()

    def finish(ins, outs, *sems):
        c, me, sibling, chips, copy, mine, first = plan(ins, outs, *sems)
        passed = []
        for j, chip in enumerate(chips):
            for a in range(n):
                copy(a, 1 + j, (*chip, c), me).wait_recv()
                fwd = copy(a, 4 + j, (*chip, c), sibling)
                fwd.start()
                passed.append(fwd)
        for a in range(n):
            copy(a, 0, sibling, me).wait_recv()
            for j, chip in enumerate(chips):
                copy(a, 4 + j, (*chip, 1 - c), me).wait_recv()
        for cp in first + passed:
            cp.wait_send()
        for cp in mine:
            cp.wait()

    return _Side(shards, [_sds((NDEV,) + a.shape, a.dtype) for a in shards], [(n, 7), (n, 7), (n,)], start, finish)


def _exchange_side(ins, n_out, copies_of):
    def start(in_refs, out_refs, *sems):
        for cp in copies_of(in_refs, out_refs, *sems):
            cp.start()

    def finish(in_refs, out_refs, *sems):
        for cp in copies_of(in_refs, out_refs, *sems):
            cp.wait()

    n = len(ins)
    return _Side(ins, [_sds((n_out,) + a.shape[1:], a.dtype) for a in ins], [(n, n_out), (n, n_out)], start, finish)


def _sibling_side(slots):
    def copies_of(ins, outs, send_sems, recv_sems):
        x, y, c = _coords()
        return [pltpu.make_async_remote_copy(
            src_ref=ins[a].at[2 * j + (1 - c)], dst_ref=outs[a].at[j], send_sem=send_sems.at[a, j], recv_sem=recv_sems.at[a, j],
            device_id=(x, y, 1 - c), device_id_type=MESH) for a in range(len(slots)) for j in range(4)]

    return _exchange_side(slots, 4, copies_of)


def _chips_side(parts):
    def copies_of(ins, outs, send_sems, recv_sems):
        x, y, c = _coords()
        chips = [(1 - x, y), (x, 1 - y), (1 - x, 1 - y)]
        return [pltpu.make_async_remote_copy(
            src_ref=ins[a].at[2 * px + py], dst_ref=outs[a].at[k], send_sem=send_sems.at[a, k], recv_sem=recv_sems.at[a, k],
            device_id=(px, py, c), device_id_type=MESH) for a in range(len(parts)) for k, (px, py) in enumerate(chips)]

    return _exchange_side(parts, 3, copies_of)


def _all_to_all_side(slots):
    def plan(ins, outs, send_sems, recv_sems, local_sems):
        x, y, c = _coords()
        mine = 4 * x + 2 * y + c
        own = pltpu.make_async_copy(ins[0].at[mine], outs[0].at[mine], local_sems.at[0])
        remote = []
        for r in range(1, NDEV):
            peer = (x ^ (r >> 2), y ^ ((r >> 1) & 1), c ^ (r & 1))
            remote.append(pltpu.make_async_remote_copy(
                src_ref=ins[0].at[mine ^ r], dst_ref=outs[0].at[mine], send_sem=send_sems.at[r - 1], recv_sem=recv_sems.at[r - 1],
                device_id=peer, device_id_type=MESH))
        return own, remote

    def start(ins, outs, *sems):
        own, remote = plan(ins, outs, *sems)
        for cp in [own] + remote:
            cp.start()

    def finish(ins, outs, *sems):
        own, remote = plan(ins, outs, *sems)
        for cp in remote:
            cp.wait()
        own.wait()

    return _Side([slots], [_sds(slots.shape, slots.dtype)], [(NDEV - 1,), (NDEV - 1,), (1,)], start, finish)


def _allreduce_small(pack, name):
    rows, cols = pack.shape

    def body(in_ref, out_ref, buf_ref, send_sems, recv_sems):
        x, y, c = _coords()
        mine = 4 * x + 2 * y + c
        buf_ref[mine] = in_ref[...]
        copies = []
        for r in range(1, NDEV):
            peer = (x ^ (r >> 2), y ^ ((r >> 1) & 1), c ^ (r & 1))
            copies.append(pltpu.make_async_remote_copy(
                src_ref=in_ref, dst_ref=buf_ref.at[mine], send_sem=send_sems.at[r - 1], recv_sem=recv_sems.at[r - 1],
                device_id=peer, device_id_type=MESH))
        for cp in copies:
            cp.start()
        for r in range(1, NDEV):
            pltpu.make_async_remote_copy(
                src_ref=in_ref, dst_ref=buf_ref.at[mine ^ r], send_sem=send_sems.at[r - 1], recv_sem=recv_sems.at[r - 1],
                device_id=(x, y, c), device_id_type=MESH).wait_recv()
        for cp in copies:
            cp.wait_send()
        acc = buf_ref[0]
        for d in range(1, NDEV):
            acc = acc + buf_ref[d]
        out_ref[...] = acc

    return pl.pallas_call(
        body, in_specs=[VMEM_FULL], out_specs=VMEM_FULL, out_shape=_sds((rows, cols)),
        scratch_shapes=[pltpu.VMEM((NDEV, rows, cols), F32), pltpu.SemaphoreType.DMA((NDEV - 1,)), pltpu.SemaphoreType.DMA((NDEV - 1,))],
        name=name)(pack)


def _add_sibling(slots, recv, core, tr, name):
    _, rows, cols = slots.shape
    tr = _row_tile(rows, tr)

    def body(c_ref, a_ref, b_ref, o_ref):
        o_ref[...] = (a_ref[...].astype(F32) + b_ref[...].astype(F32)).astype(BF16)

    gs = pltpu.PrefetchScalarGridSpec(
        num_scalar_prefetch=1, grid=(4, rows // tr),
        in_specs=[pl.BlockSpec((None, tr, cols), lambda j, i, cr: (2 * j + cr[0], i, 0)),
                  pl.BlockSpec((None, tr, cols), lambda j, i, cr: (j, i, 0))],
        out_specs=pl.BlockSpec((None, tr, cols), lambda j, i, cr: (j, i, 0)))
    return pl.pallas_call(body, grid_spec=gs, out_shape=_sds((4, rows, cols), BF16), name=name,
                          compiler_params=_params(("parallel", "parallel"), 6 * tr * cols * 4))(core, slots, recv)


def _adam_math(w, g, m, v):
    m2 = B1 * m + (1.0 - B1) * g
    v2 = B2 * v + (1.0 - B2) * jnp.square(g)
    m_hat = m2 / (1.0 - B1 ** STEP)
    v_hat = v2 / (1.0 - B2 ** STEP)
    return -LR * (m_hat / (jnp.sqrt(v_hat) + EPS_ADAM) + WD * w), m2, v2


def _adamw_reduced(part, recv, chip, w, m, v, tr, name):
    rows, cols = w.shape
    tr = _row_tile(rows, tr)

    def body(j_ref, p_ref, r0_ref, r1_ref, r2_ref, w_ref, m_ref, v_ref, g_ref, d_ref, nm_ref, nv_ref):
        g = p_ref[...].astype(F32) + r0_ref[...].astype(F32) + r1_ref[...].astype(F32) + r2_ref[...].astype(F32)
        d, m2, v2 = _adam_math(w_ref[...], g, m_ref[...], v_ref[...])
        g_ref[...] = g
        d_ref[...] = d
        nm_ref[...] = m2
        nv_ref[...] = v2

    flat = pl.BlockSpec((tr, cols), lambda i, jr: (i, 0))
    gs = pltpu.PrefetchScalarGridSpec(
        num_scalar_prefetch=1, grid=(rows // tr,),
        in_specs=[pl.BlockSpec((None, tr, cols), lambda i, jr: (jr[0], i, 0))]
        + [pl.BlockSpec((None, tr, cols), functools.partial(lambda i, jr, k: (k, i, 0), k=k)) for k in range(3)] + [flat] * 3,
        out_specs=[flat] * 4)
    return pl.pallas_call(body, grid_spec=gs, out_shape=[_sds((rows, cols))] * 4, name=name,
                          compiler_params=_params(("parallel",), 22 * tr * cols * 4))(chip, part, recv, recv, recv, w, m, v)


def _adamw_summed(recv, w, m, v, tr, name):
    rows, cols = w.shape
    tr = _row_tile(rows, tr)

    def body(*refs):
        parts, (w_ref, m_ref, v_ref), (g_ref, d_ref, nm_ref, nv_ref) = refs[:NDEV], refs[NDEV:NDEV + 3], refs[NDEV + 3:]
        g = parts[0][...].astype(F32)
        for p_ref in parts[1:]:
            g = g + p_ref[...].astype(F32)
        d, m2, v2 = _adam_math(w_ref[...], g, m_ref[...], v_ref[...])
        g_ref[...] = g
        d_ref[...] = d
        nm_ref[...] = m2
        nv_ref[...] = v2

    flat = pl.BlockSpec((tr, cols), lambda i: (i, 0))
    slot = [pl.BlockSpec((None, tr, cols), functools.partial(lambda i, k: (k, i, 0), k=k)) for k in range(NDEV)]
    return pl.pallas_call(body, grid=(rows // tr,), in_specs=slot + [flat] * 3, out_specs=[flat] * 4,
                          out_shape=[_sds((rows, cols))] * 4, name=name,
                          compiler_params=_params(("parallel",), 24 * tr * cols * 4))(*([recv] * NDEV), w, m, v)


def _adamw_plain(w, g, m, v, name):
    def body(w_ref, g_ref, m_ref, v_ref, d_ref, nm_ref, nv_ref):
        d, m2, v2 = _adam_math(w_ref[...], g_ref[...], m_ref[...], v_ref[...])
        d_ref[...] = d
        nm_ref[...] = m2
        nv_ref[...] = v2

    return pl.pallas_call(body, out_shape=[_sds(w.shape)] * 3, name=name)(w, g, m, v)


def _pack_small(norm1_w, conv_w, a_log, dt_bias, gdn_norm_w, ret_norm_w, norm2_w, norm_f_w):
    misc = jnp.concatenate([gdn_norm_w.reshape(1, DV), a_log.reshape(1, H), dt_bias.reshape(1, H),
                            jnp.zeros((1, D - DV - 2 * H), F32)], axis=1)
    return jnp.concatenate([norm1_w.reshape(1, D), ret_norm_w.reshape(1, D), norm2_w.reshape(1, D), norm_f_w.reshape(1, D),
                            conv_w.reshape(8, D), misc, jnp.zeros((3, D), F32)], axis=0)


def _unpack_small(pack):
    return dict(norm1_w=pack[0:1], ret_norm_w=pack[1:2], norm2_w=pack[2:3], norm_f_w=pack[3], conv_w=pack[4:12].reshape(4, 2 * D),
                gdn_norm_w=pack[12:13, 0:DV], a_log=pack[12:13, DV:DV + H], dt_bias=pack[12:13, DV + H:DV + 2 * H])


IN_SPLITS = (4096, 2048, 8, 8, 1024, 1024, 2048, 2048, 2048, 2048)


BA_END = sum(IN_SPLITS[:4])
LANES = 128


def _padded_order_blocks():
    z0, ba0, rq0, rk0, rv0, rg0, ga0, gb0 = 4096, 6144, 6400, 7424, 8448, 10496, 12544, 14592
    cols = []
    for h in range(H):
        for base in (z0, rg0, ga0, gb0):
            cols += [base + DV * h, base + DV * h + LANES]
    cols += list(range(0, z0, LANES))
    for h in range(H):
        cols += [rq0 + DK * h, rk0 + DK * h, rv0 + DV * h, rv0 + DV * h + LANES]
    cols += [ba0, ba0 + LANES]
    blocks = np.asarray(cols, np.int32) // LANES
    assert sorted(blocks.tolist()) == list(range(P_IN // LANES))
    return blocks


def _permute_blocks(x, blocks, name):
    rows, cols = x.shape

    def body(p_ref, x_ref, o_ref):
        o_ref[...] = x_ref[...]

    gs = pltpu.PrefetchScalarGridSpec(num_scalar_prefetch=1, grid=(cols // LANES,),
                                      in_specs=[pl.BlockSpec((rows, LANES), lambda j, p: (0, p[j]))],
                                      out_specs=pl.BlockSpec((rows, LANES), lambda j, p: (0, j)))
    return pl.pallas_call(body, grid_spec=gs, out_shape=_sds((rows, cols), x.dtype), name=name,
                          compiler_params=_params(("parallel",)))(jnp.asarray(blocks), x)


def _regroup_w_in(w):
    padded = jnp.concatenate([w[:, :BA_END], jnp.zeros((w.shape[0], P_IN - N_IN), w.dtype), w[:, BA_END:]], axis=1)
    return _permute_blocks(padded, _padded_order_blocks(), "w_in_to_layout")


def _ungroup_w_in(g):
    padded = _permute_blocks(g, np.argsort(_padded_order_blocks()).astype(np.int32), "w_in_grad_from_layout")
    return jnp.concatenate([padded[:, :BA_END], padded[:, BA_END + P_IN - N_IN:]], axis=1)


def _to_slots_cols(g):
    rows, cols = g.shape
    return g.reshape(rows, NDEV, cols // NDEV).transpose(1, 0, 2)


def _from_slots_cols(a):
    n, rows, cols = a.shape
    return a.transpose(1, 0, 2).reshape(rows, n * cols)


WEIGHT_ORDER = ["norm1_w", "w_in", "conv_w", "a_log", "dt_bias", "gdn_norm_w", "ret_norm_w", "w_out", "norm2_w", "w_gate", "w_up",
                "w_down", "norm_f_w"]


def kernel(x, norm1_w, w_in, conv_w, a_log, dt_bias, gdn_norm_w, ret_norm_w, w_out, norm2_w, w_gate, w_up, w_down, norm_f_w, loss_target, m_norm1_w, m_w_in, m_conv_w, m_a_log, m_dt_bias, m_gdn_norm_w, m_ret_norm_w, m_w_out, m_norm2_w, m_w_gate, m_w_up, m_w_down, m_norm_f_w, v_norm1_w, v_w_in, v_conv_w, v_a_log, v_dt_bias, v_gdn_norm_w, v_ret_norm_w, v_w_out, v_norm2_w, v_w_gate, v_w_up, v_w_down, v_norm_f_w):
    ax, ay, ac = _coords()
    me = 4 * ax + 2 * ay + ac
    core = jnp.reshape(ac, (1,)).astype(jnp.int32)
    chip = jnp.reshape(2 * ax + ay, (1,)).astype(jnp.int32)
    w = dict(norm1_w=norm1_w, w_in=w_in[0], conv_w=conv_w[0], a_log=a_log, dt_bias=dt_bias, gdn_norm_w=gdn_norm_w,
             ret_norm_w=ret_norm_w, w_out=w_out[0], norm2_w=norm2_w, w_gate=w_gate[0], w_up=w_up[0], w_down=w_down[0],
             norm_f_w=norm_f_w)
    m = dict(norm1_w=m_norm1_w, w_in=m_w_in[0], conv_w=m_conv_w[0], a_log=m_a_log, dt_bias=m_dt_bias, gdn_norm_w=m_gdn_norm_w,
             ret_norm_w=m_ret_norm_w, w_out=m_w_out[0], norm2_w=m_norm2_w, w_gate=m_w_gate[0], w_up=m_w_up[0], w_down=m_w_down[0],
             norm_f_w=m_norm_f_w)
    v = dict(norm1_w=v_norm1_w, w_in=v_w_in[0], conv_w=v_conv_w[0], a_log=v_a_log, dt_bias=v_dt_bias, gdn_norm_w=v_gdn_norm_w,
             ret_norm_w=v_ret_norm_w, w_out=v_w_out[0], norm2_w=v_norm2_w, w_gate=v_w_gate[0], w_up=v_w_up[0], w_down=v_w_down[0],
             norm_f_w=v_norm_f_w)
    big_names = ["w_in", "w_out", "w_gate", "w_up", "w_down"]

    w_in_all, conv_all = _run_side(_gather_side([w["w_in"].astype(BF16), w["conv_w"]]), "w_in_allgather")
    w_in_full = _regroup_w_in(_from_slots_cols(w_in_all))
    conv_full = _from_slots_cols(conv_all)
    dist = dict(core=core, shards=[w[k].astype(BF16) for k in ("w_out", "w_gate", "w_up", "w_down")])

    loss_tile, dx, big, small = _local_step(
        x[0], loss_target[0], w_in_full, None, None, None, None, norm1_w, conv_full, a_log, dt_bias,
        gdn_norm_w, ret_norm_w, norm2_w, norm_f_w.reshape(1, D), dist=dist)
    loss = lax.psum(loss_tile[0, 0], ("x", "y", "c"))

    out = {"w_in": _adamw_summed(big["w_in"], w["w_in"], m["w_in"], v["w_in"], 64, "adamw_w_in")}
    for k in ("w_out", "w_gate", "w_up", "w_down"):
        part, recv = big[k]
        out[k] = _adamw_reduced(part, recv, chip, w[k], m[k], v[k], 128, "adamw_" + k)

    g_small = _unpack_small(_allreduce_small(_pack_small(**small), "small_grads_allreduce"))
    g_small["conv_w"] = lax.dynamic_slice_in_dim(g_small["conv_w"], me * (2 * D // NDEV), 2 * D // NDEV, axis=1)
    small_names = [k for k in WEIGHT_ORDER if k not in big_names]
    pad_conv = lambda a: jnp.pad(a, ((0, 0), (0, 2 * D - a.shape[1])))
    packs = []
    for src in (w, g_small, m, v):
        args = {k: (pad_conv(src[k]) if k == "conv_w" else src[k]) for k in small_names}
        packs.append(_pack_small(**args))
    d_pack, m_pack, v_pack = _adamw_plain(*packs[0:1], packs[1], packs[2], packs[3], name="adamw_small")
    cut_conv = lambda dct: {**dct, "conv_w": dct["conv_w"][:, :2 * D // NDEV]}
    d_small, m_small, v_small = (cut_conv(_unpack_small(p)) for p in (d_pack, m_pack, v_pack))

    def shaped(k, a):
        return a.reshape(w_shapes[k])

    w_shapes = dict(norm1_w=norm1_w.shape, w_in=w_in.shape, conv_w=conv_w.shape, a_log=a_log.shape, dt_bias=dt_bias.shape,
                    gdn_norm_w=gdn_norm_w.shape, ret_norm_w=ret_norm_w.shape, w_out=w_out.shape, norm2_w=norm2_w.shape,
                    w_gate=w_gate.shape, w_up=w_up.shape, w_down=w_down.shape, norm_f_w=norm_f_w.shape)
    grads, deltas, new_m, new_v = [], [], [], []
    for k in WEIGHT_ORDER:
        if k in big_names:
            g_, d_, m_, v_ = out[k]
        else:
            g_, d_, m_, v_ = g_small[k], d_small[k], m_small[k], v_small[k]
        grads.append(shaped(k, g_))
        deltas.append(shaped(k, d_))
        new_m.append(shaped(k, m_))
        new_v.append(shaped(k, v_))
    return (loss, dx[None], *grads, *deltas, *new_m, *new_v)
```

```python
import functools
import numpy as np
import jax
import jax.numpy as jnp
from jax import lax
from jax.experimental import pallas as pl
from jax.experimental.pallas import tpu as pltpu

F32, BF16 = jnp.float32, jnp.bfloat16
HI = lax.Precision.HIGHEST
MESH = pl.DeviceIdType.MESH
ANY = pl.BlockSpec(memory_space=pl.ANY)
VMEM_FULL = pl.BlockSpec(memory_space=pltpu.VMEM)

NDEV = 8
D = 2048
H = 8
DK = 128
DV = 256
C = 64
CPB = 4
EPS = 1e-6
ROPE_BASE = 10000.0
N_IN = 16400
O_MERGE, O_QKV, O_RET, O_BA, P_IN = 0, 8192, 12288, 16384, 16640
MERGE_W, RET_W = 4 * DV, 2 * DK + DV
LR, B1, B2, EPS_ADAM, WD, STEP = 0.001, 0.9, 0.999, 1e-08, 0.01, 10
VMEM_CAP = 60 * 1024 * 1024

NN = ((1,), (0,))
NT = ((1,), (1,))
TN = ((0,), (0,))


def _params(sem=None, est=None):
    kw = {}
    if sem is not None:
        kw["dimension_semantics"] = sem
    if est is not None:
        kw["vmem_limit_bytes"] = int(min(VMEM_CAP, max(32 * 1024 * 1024, est * 5 // 4 + (4 << 20))))
    return pltpu.CompilerParams(**kw)


def _sds(shape, dt=F32):
    return jax.ShapeDtypeStruct(tuple(shape), dt)


def _row_tile(rows, limit):
    return max(t for t in range(16, min(rows, limit) + 1, 16) if rows % t == 0)


def _bdot(a, b, dims):
    return lax.dot_general(a.astype(BF16), b.astype(BF16), (dims, ((), ())), preferred_element_type=F32)


def _hdot(a, b, dims):
    return lax.dot_general(a, b, (dims, ((), ())), precision=HI, preferred_element_type=F32)


def _silu(x):
    return x * jax.nn.sigmoid(x)


def _rms(x, w):
    return x * lax.rsqrt(jnp.mean(x * x, axis=-1, keepdims=True) + EPS) * w


class _Side:
    def __init__(self, ins, out_shapes, sems, start, finish):
        self.ins, self.out_shapes, self.sems, self.start, self.finish = list(ins), list(out_shapes), list(sems), start, finish


def _run_side(side, name):
    ni, no = len(side.ins), len(side.out_shapes)

    def body(*refs):
        ins, outs, sems = refs[:ni], refs[ni:ni + no], refs[ni + no:]
        side.start(ins, outs, *sems)
        side.finish(ins, outs, *sems)

    return pl.pallas_call(body, in_specs=[ANY] * ni, out_specs=[ANY] * no, out_shape=side.out_shapes,
                          scratch_shapes=[pltpu.SemaphoreType.DMA(s) for s in side.sems], name=name)(*side.ins)


def _matmul(a, b, *, ta=False, tb=False, tm, tn, tk, out_dtype=F32, res=None, side=None, name):
    m = a.shape[1] if ta else a.shape[0]
    k = a.shape[0] if ta else a.shape[1]
    n = b.shape[0] if tb else b.shape[1]
    assert k == (b.shape[1] if tb else b.shape[0])
    tm, tn, tk = min(tm, m), min(tn, n), min(tk, k)
    assert m % tm == 0 and n % tn == 0 and k % tk == 0, (name, m, n, k, tm, tn, tk)
    nk = k // tk
    dims = ((0 if ta else 1,), (1 if tb else 0,))
    has_res = res is not None
    n_in = 3 if has_res else 2
    n_side_in = len(side.ins) if side else 0
    n_side_out = len(side.out_shapes) if side else 0
    grid = (m // tm, n // tn, nk)

    def body(*refs):
        a_ref, b_ref = refs[0], refs[1]
        r_ref = refs[2] if has_res else None
        o_ref = refs[n_in + n_side_in]
        if side:
            side_ins = refs[n_in:n_in + n_side_in]
            side_outs = refs[n_in + n_side_in + 1:n_in + n_side_in + 1 + n_side_out]
            side_sems = refs[len(refs) - len(side.sems):]
            step = (pl.program_id(0) * grid[1] + pl.program_id(1)) * grid[2] + pl.program_id(2)

            @pl.when(step == 0)
            def _():
                side.start(side_ins, side_outs, *side_sems)

        def finish(acc):
            if has_res:
                acc = acc + r_ref[...].astype(F32)
            o_ref[...] = acc.astype(out_dtype)

        part = _bdot(a_ref[...], b_ref[...], dims)
        if nk == 1:
            finish(part)
        else:
            acc_ref = refs[n_in + n_side_in + 1 + n_side_out]
            kk = pl.program_id(2)

            @pl.when(kk == 0)
            def _():
                acc_ref[...] = part

            @pl.when(kk > 0)
            def _():
                acc_ref[...] += part

            @pl.when(kk == nk - 1)
            def _():
                finish(acc_ref[...])

        if side:
            @pl.when(step == grid[0] * grid[1] * grid[2] - 1)
            def _():
                side.finish(side_ins, side_outs, *side_sems)

    a_spec = pl.BlockSpec((tk, tm), lambda i, j, kk: (kk, i)) if ta else pl.BlockSpec((tm, tk), lambda i, j, kk: (i, kk))
    b_spec = pl.BlockSpec((tn, tk), lambda i, j, kk: (j, kk)) if tb else pl.BlockSpec((tk, tn), lambda i, j, kk: (kk, j))
    o_spec = pl.BlockSpec((tm, tn), lambda i, j, kk: (i, j))
    in_specs = [a_spec, b_spec] + ([o_spec] if has_res else []) + [ANY] * n_side_in
    est = 2 * (tm * tk * a.dtype.itemsize + tk * tn * b.dtype.itemsize + tm * tn * jnp.dtype(out_dtype).itemsize)
    est += 2 * tm * tn * 4 * (1 if has_res else 0) + (tm * tn * 4 if nk > 1 else 0) + 2 * tm * tn * 4
    args = (a, b) + ((res,) if has_res else ()) + (tuple(side.ins) if side else ())
    scratch = ([pltpu.VMEM((tm, tn), F32)] if nk > 1 else []) + ([pltpu.SemaphoreType.DMA(s) for s in side.sems] if side else [])
    sem = ("arbitrary",) * 3 if side else ("parallel", "parallel", "arbitrary")
    out = pl.pallas_call(
        body, grid=grid, in_specs=in_specs, out_specs=[o_spec] + [ANY] * n_side_out,
        out_shape=[_sds((m, n), out_dtype)] + (side.out_shapes if side else []),
        scratch_shapes=scratch, name=name, compiler_params=_params(sem, est))(*args)
    return (out[0], out[1:]) if side else out[0]


def _rms_fwd(x, w, ts, name):
    s = x.shape[0]

    def body(x_ref, w_ref, o_ref, ot_ref):
        y = _rms(x_ref[...], w_ref[...]).astype(BF16)
        o_ref[...] = y
        ot_ref[...] = y.T

    row = pl.BlockSpec((ts, D), lambda i: (i, 0))
    return pl.pallas_call(body, grid=(s // ts,), in_specs=[row, pl.BlockSpec((1, D), lambda i: (0, 0))],
                          out_specs=[row, pl.BlockSpec((D, ts), lambda i: (0, i))],
                          out_shape=[_sds((s, D), BF16), _sds((D, s), BF16)], name=name,
                          compiler_params=_params(("parallel",)))(x, w)


def _rms_bwd(x, w, du, dres, ts, name):
    s = x.shape[0]

    def body(x_ref, w_ref, du_ref, dres_ref, dx_ref, dw_ref):
        _, vjp = jax.vjp(_rms, x_ref[...], w_ref[...])
        dx, dw = vjp(du_ref[...].astype(F32))
        dx_ref[...] = dx + dres_ref[...]

        @pl.when(pl.program_id(0) == 0)
        def _():
            dw_ref[...] = jnp.zeros_like(dw_ref)

        dw_ref[...] += dw

    row = pl.BlockSpec((ts, D), lambda i: (i, 0))
    vec = pl.BlockSpec((1, D), lambda i: (0, 0))
    return pl.pallas_call(body, grid=(s // ts,), in_specs=[row, vec, row, row], out_specs=[row, vec],
                          out_shape=[_sds((s, D)), _sds((1, D))], name=name,
                          compiler_params=_params(("arbitrary",), 12 * ts * D * 4))(x, w, du, dres)


def _conv_taps(xx, w, base, ts):
    acc = xx[base:base + ts] * w[0:1, :]
    for j in range(1, 4):
        acc = acc + xx[base + j:base + j + ts] * w[j:j + 1, :]
    return acc


def _causal_conv(prev8, cur, w, first):
    xx = jnp.concatenate([jnp.where(first, 0.0, prev8), cur], axis=0)
    return _conv_taps(xx, w, 5, cur.shape[0])


def _qk_post(c, scale):
    s = _silu(c)
    return s * lax.rsqrt(jnp.sum(s * s, axis=-1, keepdims=True) + EPS) * scale


def _conv_specs(ts, cw, col0):
    pcol = O_QKV // cw + col0
    cur = pl.BlockSpec((ts, cw), lambda j, i: (i, pcol + j))
    prev = pl.BlockSpec((8, cw), lambda j, i: (jnp.maximum(i * (ts // 8) - 1, 0), pcol + j))
    wsp = pl.BlockSpec((4, cw), lambda j, i: (0, col0 + j))
    return cur, prev, wsp


def _gdn_qkv_fwd(proj, conv_w, ts):
    s = proj.shape[0]

    def qk_body(cur_ref, prev_ref, w_ref, o_ref):
        c = _causal_conv(prev_ref[...], cur_ref[...], w_ref[...], pl.program_id(1) == 0)
        scale = jnp.where(pl.program_id(0) < H, DK ** -0.5, 1.0).astype(F32)
        o_ref[...] = _qk_post(c, scale)

    cur, prev, wsp = _conv_specs(ts, DK, 0)
    qk = pl.pallas_call(qk_body, grid=(2 * H, s // ts), in_specs=[cur, prev, wsp],
                        out_specs=pl.BlockSpec((ts, DK), lambda j, i: (i, j)), out_shape=_sds((s, 2 * H * DK)),
                        name="gdn_qk_prep", compiler_params=_params(("parallel", "parallel")))(proj, proj, conv_w)

    def v_body(cur_ref, prev_ref, w_ref, o_ref):
        o_ref[...] = _silu(_causal_conv(prev_ref[...], cur_ref[...], w_ref[...], pl.program_id(1) == 0))

    cw = 512
    cur, prev, wsp = _conv_specs(ts, cw, 2 * H * DK // cw)
    v = pl.pallas_call(v_body, grid=(H * DV // cw, s // ts), in_specs=[cur, prev, wsp],
                       out_specs=pl.BlockSpec((ts, cw), lambda j, i: (i, j)), out_shape=_sds((s, H * DV)),
                       name="gdn_v_prep", compiler_params=_params(("parallel", "parallel")))(proj, proj, conv_w)
    return qk, v


def _gdn_qkv_bwd(proj, conv_w, dqk, dv, dproj, ts):
    s = proj.shape[0]
    nt = s // ts

    def qk_body(cur_ref, prev_ref, w_ref, d_ref, o_ref):
        c = _causal_conv(prev_ref[...], cur_ref[...], w_ref[...], pl.program_id(1) == 0)
        scale = jnp.where(pl.program_id(0) < H, DK ** -0.5, 1.0).astype(F32)
        _, vjp = jax.vjp(lambda cc: _qk_post(cc, scale), c)
        o_ref[...] = vjp(d_ref[...])[0]

    cur, prev, wsp = _conv_specs(ts, DK, 0)
    dc_qk = pl.pallas_call(qk_body, grid=(2 * H, nt), in_specs=[cur, prev, wsp, pl.BlockSpec((ts, DK), lambda j, i: (i, j))],
                           out_specs=pl.BlockSpec((ts, DK), lambda j, i: (i, j)), out_shape=_sds((s, 2 * H * DK)),
                           name="gdn_qk_prep_bwd", compiler_params=_params(("parallel", "parallel")))(proj, proj, conv_w, dqk)

    def v_body(cur_ref, prev_ref, w_ref, d_ref, o_ref):
        c = _causal_conv(prev_ref[...], cur_ref[...], w_ref[...], pl.program_id(1) == 0)
        _, vjp = jax.vjp(_silu, c)
        o_ref[...] = vjp(d_ref[...])[0]

    cw = 512
    cur, prev, wsp = _conv_specs(ts, cw, 2 * H * DK // cw)
    dc_v = pl.pallas_call(v_body, grid=(H * DV // cw, nt), in_specs=[cur, prev, wsp, pl.BlockSpec((ts, cw), lambda j, i: (i, j))],
                          out_specs=pl.BlockSpec((ts, cw), lambda j, i: (i, j)), out_shape=_sds((s, H * DV)),
                          name="gdn_v_prep_bwd", compiler_params=_params(("parallel", "parallel")))(proj, proj, conv_w, dv)

    def conv_bwd(dc, dproj, col0, ncols, name):
        def body(x_ref, xprev_ref, w_ref, dc_ref, dcnext_ref, _, da_ref, dw_ref):
            i = pl.program_id(1)
            w = w_ref[...]
            dcur = dc_ref[...]
            dd = jnp.concatenate([dcur, jnp.where(i == nt - 1, 0.0, dcnext_ref[...])], axis=0)
            acc = dd[3:3 + ts] * w[0:1, :]
            for j in range(1, 4):
                acc = acc + dd[3 - j:3 - j + ts] * w[j:j + 1, :]
            da_ref[...] = acc.astype(BF16)
            xx = jnp.concatenate([jnp.where(i == 0, 0.0, xprev_ref[...]), x_ref[...]], axis=0)

            @pl.when(i == 0)
            def _():
                dw_ref[...] = jnp.zeros_like(dw_ref)

            for j in range(4):
                dw_ref[j:j + 1, :] += jnp.sum(dcur * xx[5 + j:5 + j + ts], axis=0, keepdims=True)

        cur, prev, wsp = _conv_specs(ts, cw, col0)
        dcur = pl.BlockSpec((ts, cw), lambda j, i: (i, j))
        dnext = pl.BlockSpec((8, cw), lambda j, i: (jnp.minimum((i + 1) * (ts // 8), s // 8 - 1), j))
        pcol = O_QKV // cw + col0
        return pl.pallas_call(body, grid=(ncols // cw, nt), in_specs=[cur, prev, wsp, dcur, dnext, ANY],
                              out_specs=[pl.BlockSpec((ts, cw), lambda j, i: (i, pcol + j)), pl.BlockSpec((4, cw), lambda j, i: (0, j))],
                              out_shape=[_sds(dproj.shape, BF16), _sds((4, ncols))], input_output_aliases={5: 0}, name=name,
                              compiler_params=_params(("parallel", "arbitrary")))(proj, proj, conv_w, dc, dc, dproj)

    dproj, dw_qk = conv_bwd(dc_qk, dproj, 0, 2 * H * DK, "conv_bwd_qk")
    dproj, dw_v = conv_bwd(dc_v, dproj, 2 * H * DK // cw, H * DV, "conv_bwd_v")
    return dproj, jnp.concatenate([dw_qk, dw_v], axis=1)


def _bg(b, a, alog, dtb):
    n = b.shape[0]
    g = -jnp.exp(alog) * jax.nn.softplus(a + dtb)
    row = lax.broadcasted_iota(jnp.int32, (n, n), 0)
    col = lax.broadcasted_iota(jnp.int32, (n, n), 1)
    shift = C.bit_length() - 1
    same = (row >> shift) == (col >> shift)
    return jax.nn.sigmoid(b), _hdot((same & (row >= col)).astype(F32), g, NN), _hdot(same.astype(F32), g, NN)


def _bg_fwd(proj, alog, dtb, ts):
    s = proj.shape[0]

    def body(ba_ref, alog_ref, dtb_ref, beta_ref, gc_ref, gl_ref):
        beta_ref[...], gc_ref[...], gl_ref[...] = _bg(ba_ref[:, 0:H], ba_ref[:, H:2 * H], alog_ref[...], dtb_ref[...])

    small = pl.BlockSpec((1, H), lambda i: (0, 0))
    out = pl.BlockSpec((ts, H), lambda i: (i, 0))
    return pl.pallas_call(body, grid=(s // ts,), in_specs=[pl.BlockSpec((ts, 256), lambda i: (i, O_BA // 256)), small, small],
                          out_specs=[out] * 3, out_shape=[_sds((s, H))] * 3, name="gdn_bg_prep",
                          compiler_params=_params(("parallel",)))(proj, alog, dtb)


def _bg_bwd(proj, alog, dtb, dbeta_h, dgc_h, dgl_h, dproj, ts):
    s = proj.shape[0]

    def body(ba_ref, alog_ref, dtb_ref, dbeta_ref, dgc_ref, dgl_ref, _, dba_ref, dalog_ref, ddtb_ref):
        _, vjp = jax.vjp(_bg, ba_ref[:, 0:H], ba_ref[:, H:2 * H], alog_ref[...], dtb_ref[...])
        db, da, dalog, ddtb = vjp((jnp.sum(dbeta_ref[...], axis=0), jnp.sum(dgc_ref[...], axis=0), jnp.sum(dgl_ref[...], axis=0)))
        dba_ref[...] = jnp.zeros_like(dba_ref)
        dba_ref[:, 0:H] = db.astype(BF16)
        dba_ref[:, H:2 * H] = da.astype(BF16)

        @pl.when(pl.program_id(0) == 0)
        def _():
            dalog_ref[...] = jnp.zeros_like(dalog_ref)
            ddtb_ref[...] = jnp.zeros_like(ddtb_ref)

        dalog_ref[...] += dalog
        ddtb_ref[...] += ddtb

    small = pl.BlockSpec((1, H), lambda i: (0, 0))
    per_head = pl.BlockSpec((H, ts, H), lambda i: (0, i, 0))
    return pl.pallas_call(body, grid=(s // ts,),
                          in_specs=[pl.BlockSpec((ts, 256), lambda i: (i, O_BA // 256)), small, small, per_head, per_head, per_head, ANY],
                          out_specs=[pl.BlockSpec((ts, 256), lambda i: (i, O_BA // 256)), small, small],
                          out_shape=[_sds(dproj.shape, BF16), _sds((1, H)), _sds((1, H))], input_output_aliases={6: 0},
                          name="gdn_bg_prep_bwd", compiler_params=_params(("arbitrary",)))(proj, alog, dtb, dbeta_h, dgc_h, dgl_h, dproj)


BLK = 4 * C
NNB, NTB, TNB = ((2,), (1,)), ((2,), (2,)), ((1,), (1,))


def _bdot_b(a, b, dims):
    return lax.dot_general(a.astype(BF16), b.astype(BF16), (dims, ((0,), (0,))), preferred_element_type=F32)


@jax.custom_vjp
def _inv_unit_lower(a):
    n = a.shape[-1]
    row = lax.broadcasted_iota(jnp.int32, (n, n), 0)
    col = lax.broadcasted_iota(jnp.int32, (n, n), 1)
    x = jnp.where(row == col, 1.0, 0.0).astype(F32) - a
    p = _bdot_b(a, a, NNB)
    power = 2
    while True:
        x = x + _bdot_b(x, p, NNB)
        power *= 2
        if power >= C:
            return x
        p = _bdot_b(p, p, NNB)


def _inv_fwd(a):
    t = _inv_unit_lower(a)
    return t, t


def _inv_bwd(t, dt):
    return (-_bdot_b(_bdot_b(t, dt, TNB), t, NTB),)


_inv_unit_lower.defvjp(_inv_fwd, _inv_bwd)


def _gdn_prep(q, k, v, bfull, gcfull, glfull, hmask):
    nb, n = q.shape[0], q.shape[1]
    beta = jnp.sum(bfull * hmask, axis=-1, keepdims=True)
    gc = jnp.sum(gcfull * hmask, axis=-1, keepdims=True)
    gl = jnp.sum(glfull * hmask, axis=-1, keepdims=True)
    row = lax.broadcasted_iota(jnp.int32, (n, n), 0)
    col = lax.broadcasted_iota(jnp.int32, (n, n), 1)
    shift = C.bit_length() - 1
    same = (row >> shift) == (col >> shift)
    incl, strict = same & (row >= col), same & (row > col)
    g_i = gc * jnp.ones((1, 1, n), F32)
    decay = jnp.exp(jnp.where(incl, g_i - jnp.swapaxes(g_i, 1, 2), -jnp.inf))
    kb = k * beta
    a = jnp.where(strict, _bdot_b(kb, k, NTB) * decay, 0.0)
    tinv = _inv_unit_lower(a)
    u = _bdot_b(tinv, v * beta, NNB)
    w = _bdot_b(tinv, kb * jnp.exp(gc), NNB)
    attn = _bdot_b(q, k, NTB) * decay
    fold = ((lax.broadcasted_iota(jnp.int32, (n, C), 0) & (C - 1)) == lax.broadcasted_iota(jnp.int32, (n, C), 1)).astype(F32)
    attn_c = _bdot(attn.reshape(nb * n, n), fold, NN).reshape(nb, n, C)
    return u, w, attn_c, q * jnp.exp(gc), k * jnp.exp(gl - gc), jnp.exp(gl)


def _gdn_step(u, w, attn, qg, kd, egl, state):
    v_new = u - _bdot_b(w, state, NNB)
    o = _bdot_b(qg, state, NNB) + _bdot_b(attn, v_new, NNB)
    return o, state * egl + _bdot_b(kd, v_new, TNB)


def _heads(ref, rs, width):
    return jnp.stack([ref[rs, h * width:(h + 1) * width] for h in range(H)])


def _head_mask(h):
    return (lax.broadcasted_iota(jnp.int32, (1, H), 1) == h).astype(F32)


PREP_BLOCKS = 2


def _gdn_prep_specs(r):
    small = pl.BlockSpec((r, H), lambda h, c: (c, 0))
    return [pl.BlockSpec((r, DK), lambda h, c: (c, h)), pl.BlockSpec((r, DK), lambda h, c: (c, H + h)),
            pl.BlockSpec((r, DV), lambda h, c: (c, h)), small, small, small]


def _blocked(ref):
    x = ref[...]
    return x.reshape(PREP_BLOCKS, BLK, x.shape[-1])


def _gdn_inter_specs(r):
    col = pl.BlockSpec((r, DK), lambda h, c: (c, h))
    return [pl.BlockSpec((r, DV), lambda h, c: (c, h)), col, pl.BlockSpec((1, r, C), lambda h, c: (h, c, 0)), col, col,
            pl.BlockSpec((1, r // C, 8, 128), lambda h, c: (h, c, 0, 0))]


def _gdn_prep_fwd(qk, v, beta, gc, gl):
    s = qk.shape[0]
    r = PREP_BLOCKS * BLK

    def body(q_ref, k_ref, v_ref, b_ref, gc_ref, gl_ref, u_ref, w_ref, attn_ref, qg_ref, kd_ref, egl_ref):
        u, w, attn, qg, kd, egl = _gdn_prep(_blocked(q_ref), _blocked(k_ref), _blocked(v_ref), _blocked(b_ref), _blocked(gc_ref),
                                            _blocked(gl_ref), _head_mask(pl.program_id(0)))
        u_ref[...] = u.reshape(r, DV)
        w_ref[...] = w.reshape(r, DK).astype(BF16)
        attn_ref[0] = attn.reshape(r, C).astype(BF16)
        qg_ref[...] = qg.reshape(r, DK).astype(BF16)
        kd_ref[...] = kd.reshape(r, DK).astype(BF16)
        egl = egl.reshape(r, 1)
        for j in range(r // C):
            egl_ref[0, j] = egl[j * C:j * C + 1, :] * jnp.ones((8, 128), F32)

    out_shape = [_sds((s, H * DV)), _sds((s, H * DK), BF16), _sds((H, s, C), BF16), _sds((s, H * DK), BF16),
                 _sds((s, H * DK), BF16), _sds((H, s // C, 8, 128))]
    return pl.pallas_call(body, grid=(H, s // r), in_specs=_gdn_prep_specs(r), out_specs=_gdn_inter_specs(r), out_shape=out_shape,
                          name="gdn_prep_fwd", compiler_params=_params(("parallel", "parallel")))(qk, qk, v, beta, gc, gl)


def _gdn_prep_bwd(qk, v, beta, gc, gl, du, dw, dattn, dqg, dkd, degl):
    s = qk.shape[0]
    r = PREP_BLOCKS * BLK

    def body(q_ref, k_ref, v_ref, b_ref, gc_ref, gl_ref, du_ref, dw_ref, dattn_ref, dqg_ref, dkd_ref, degl_ref,
             dq_ref, dk_ref, dv_ref, db_ref, dgc_ref, dgl_ref):
        hmask = _head_mask(pl.program_id(0))
        _, vjp = jax.vjp(lambda q, k, v, b, gc, gl: _gdn_prep(q, k, v, b, gc, gl, hmask), _blocked(q_ref), _blocked(k_ref),
                         _blocked(v_ref), _blocked(b_ref), _blocked(gc_ref), _blocked(gl_ref))
        rowid = lax.broadcasted_iota(jnp.int32, (r, 1), 0)
        degl = jnp.zeros((r, 1), F32)
        for j in range(r // C):
            degl = jnp.where(rowid == j * C, degl_ref[0, j, 0:1, 0:1], degl)
        dq, dk, dv, db, dgc, dgl = vjp((_blocked(du_ref), _blocked(dw_ref), _blocked(dattn_ref.at[0]), _blocked(dqg_ref),
                                        _blocked(dkd_ref), degl.reshape(PREP_BLOCKS, BLK, 1)))
        dq_ref[...] = dq.reshape(r, DK)
        dk_ref[...] = dk.reshape(r, DK)
        dv_ref[...] = dv.reshape(r, DV)
        db_ref[0] = db.reshape(r, H)
        dgc_ref[0] = dgc.reshape(r, H)
        dgl_ref[0] = dgl.reshape(r, H)

    col = pl.BlockSpec((r, DK), lambda h, c: (c, h))
    piece = pl.BlockSpec((1, r, H), lambda h, c: (h, c, 0))
    dq, dk, dv, db, dgc, dgl = pl.pallas_call(
        body, grid=(H, s // r), in_specs=_gdn_prep_specs(r) + _gdn_inter_specs(r),
        out_specs=[col, col, pl.BlockSpec((r, DV), lambda h, c: (c, h)), piece, piece, piece],
        out_shape=[_sds((s, H * DK)), _sds((s, H * DK)), _sds((s, H * DV))] + [_sds((H, s, H))] * 3,
        name="gdn_prep_bwd", compiler_params=_params(("parallel", "parallel"), 24 << 20))(
            qk, qk, v, beta, gc, gl, du, dw, dattn, dqg, dkd, degl)
    return jnp.concatenate([dq, dk], axis=1), dv, db, dgc, dgl


def _gdn_scan_specs(r, order):
    wide = pl.BlockSpec((r, H * DK), lambda c: (order(c), 0))
    return [pl.BlockSpec((r, H * DV), lambda c: (order(c), 0)), wide, pl.BlockSpec((H, r, C), lambda c: (0, order(c), 0)), wide, wide,
            pl.BlockSpec((H, r // C, 8, 128), lambda c: (0, order(c), 0, 0))]


def _gdn_scan_fwd(u, w, attn, qg, kd, egl):
    s = u.shape[0]
    r = CPB * C
    nb = s // r

    def body(u_ref, w_ref, attn_ref, qg_ref, kd_ref, egl_ref, o_ref, st_ref, state_ref):
        @pl.when(pl.program_id(0) == 0)
        def _():
            state_ref[...] = jnp.zeros_like(state_ref)

        state = state_ref[...]
        for i in range(CPB):
            rs = slice(i * C, (i + 1) * C)
            st_ref[:, i] = state
            o, state = _gdn_step(_heads(u_ref, rs, DV), _heads(w_ref, rs, DK), attn_ref[:, rs, :], _heads(qg_ref, rs, DK),
                                 _heads(kd_ref, rs, DK), egl_ref[:, i, 0:1, 0:1], state)
            for h in range(H):
                o_ref[rs, h * DV:(h + 1) * DV] = o[h]
        state_ref[...] = state

    out_specs = [pl.BlockSpec((r, H * DV), lambda c: (c, 0)), pl.BlockSpec((H, CPB, DK, DV), lambda c: (0, c, 0, 0))]
    return pl.pallas_call(body, grid=(nb,), in_specs=_gdn_scan_specs(r, lambda c: c), out_specs=out_specs,
                          out_shape=[_sds((s, H * DV)), _sds((H, s // C, DK, DV))],
                          scratch_shapes=[pltpu.VMEM((H, DK, DV), F32)], name="gdn_scan_fwd",
                          compiler_params=_params(("arbitrary",), 24 << 20))(u, w, attn, qg, kd, egl)


def _gdn_scan_bwd(u, w, attn, qg, kd, egl, states, do):
    s = u.shape[0]
    r = CPB * C
    nb = s // r

    def body(u_ref, w_ref, attn_ref, qg_ref, kd_ref, egl_ref, st_ref, do_ref,
             du_ref, dw_ref, dattn_ref, dqg_ref, dkd_ref, degl_ref, dstate_ref):
        @pl.when(pl.program_id(0) == 0)
        def _():
            dstate_ref[...] = jnp.zeros_like(dstate_ref)

        dstate = dstate_ref[...]
        for i in reversed(range(CPB)):
            rs = slice(i * C, (i + 1) * C)
            _, vjp = jax.vjp(_gdn_step, _heads(u_ref, rs, DV), _heads(w_ref, rs, DK).astype(F32), attn_ref[:, rs, :].astype(F32),
                             _heads(qg_ref, rs, DK).astype(F32), _heads(kd_ref, rs, DK).astype(F32), egl_ref[:, i, 0:1, 0:1],
                             st_ref[:, i])
            du, dw, dattn, dqg, dkd, degl, dstate = vjp((_heads(do_ref, rs, DV), dstate))
            dattn_ref[:, rs, :] = dattn
            degl_ref[:, i] = degl * jnp.ones((1, 8, 128), F32)
            for h in range(H):
                du_ref[rs, h * DV:(h + 1) * DV] = du[h]
                dw_ref[rs, h * DK:(h + 1) * DK] = dw[h]
                dqg_ref[rs, h * DK:(h + 1) * DK] = dqg[h]
                dkd_ref[rs, h * DK:(h + 1) * DK] = dkd[h]
        dstate_ref[...] = dstate

    rev = lambda c: nb - 1 - c
    in_specs = _gdn_scan_specs(r, rev) + [pl.BlockSpec((H, CPB, DK, DV), lambda c: (0, rev(c), 0, 0)),
                                          pl.BlockSpec((r, H * DV), lambda c: (rev(c), 0))]
    return pl.pallas_call(
        body, grid=(nb,), in_specs=in_specs, out_specs=_gdn_scan_specs(r, rev),
        out_shape=[_sds((s, H * DV)), _sds((s, H * DK)), _sds((H, s, C)), _sds((s, H * DK)), _sds((s, H * DK)),
                   _sds((H, s // C, 8, 128))],
        scratch_shapes=[pltpu.VMEM((H, DK, DV), F32)], name="gdn_scan_bwd",
        compiler_params=_params(("arbitrary",), 40 << 20))(u, w, attn, qg, kd, egl, states, do)


def _rot(x, cs, sn):
    return x * cs + pltpu.roll(x, DK // 2, 1) * sn


def _rot_t(d, cs, sn):
    return d * cs - pltpu.roll(d, DK // 2, 1) * sn


def _ret_chunk(q, k, v, state, lg):
    n = q.shape[0]
    row = lax.broadcasted_iota(jnp.int32, (n, n), 0)
    col = lax.broadcasted_iota(jnp.int32, (n, n), 1)
    dist = (row - col).astype(F32)
    dmat = jnp.exp(jnp.where(dist >= 0, dist * lg, -jnp.inf))
    scores = _bdot(q, k, NT) * dmat
    pos = lax.broadcasted_iota(jnp.int32, (n, 1), 0).astype(F32)
    xi = jnp.exp((pos + 1.0) * lg)
    zeta = jnp.exp((n - 1.0 - pos) * lg)
    o = _bdot(scores, v, NN) + _bdot(q, state, NN) * xi
    new_state = state * jnp.exp(n * lg) + _bdot(k * zeta, v, TN)
    return o, new_state


def _ret_specs(r, order):
    q0, v0 = O_RET // DK, (O_RET + 2 * DK) // DV
    return [pl.BlockSpec((r, DK), lambda h, c: (order(c), q0 + h * (RET_W // DK))),
            pl.BlockSpec((r, DK), lambda h, c: (order(c), q0 + 1 + h * (RET_W // DK))),
            pl.BlockSpec((r, DV), lambda h, c: (order(c), v0 + h * (RET_W // DV))), pl.BlockSpec((r, DK), lambda h, c: (order(c), 0)),
            pl.BlockSpec((r, DK), lambda h, c: (order(c), 0)), pl.BlockSpec((1, H), lambda h, c: (0, 0))]


RET_C = 256


def _ret_scan_fwd(proj, cs, sn, lgtab):
    s = proj.shape[0]
    r = min(RET_C, s)
    nb = s // r

    def body(q_ref, k_ref, v_ref, cs_ref, sn_ref, lg_ref, o_ref, st_ref, state_ref):
        @pl.when(pl.program_id(1) == 0)
        def _():
            state_ref[...] = jnp.zeros_like(state_ref)

        lg = jnp.sum(lg_ref[...] * _head_mask(pl.program_id(0)), axis=1, keepdims=True)
        state = state_ref[...]
        st_ref[0, 0] = state
        q = _rot(q_ref[...], cs_ref[...], sn_ref[...])
        k = _rot(k_ref[...], cs_ref[...], sn_ref[...]) * DK ** -0.5
        o_ref[...], state_ref[...] = _ret_chunk(q, k, v_ref[...], state, lg)

    out_specs = [pl.BlockSpec((r, DV), lambda h, c: (c, h)), pl.BlockSpec((1, 1, DK, DV), lambda h, c: (h, c, 0, 0))]
    return pl.pallas_call(body, grid=(H, nb), in_specs=_ret_specs(r, lambda c: c), out_specs=out_specs,
                          out_shape=[_sds((s, H * DV)), _sds((H, nb, DK, DV))],
                          scratch_shapes=[pltpu.VMEM((DK, DV), F32)], name="ret_scan_fwd",
                          compiler_params=_params(("parallel", "arbitrary")))(proj, proj, proj, cs, sn, lgtab)


def _ret_scan_bwd(proj, cs, sn, lgtab, states, do, dproj):
    s = proj.shape[0]
    r = min(RET_C, s)
    nb = s // r

    def body(q_ref, k_ref, v_ref, cs_ref, sn_ref, lg_ref, st_ref, do_ref, _, d_ref, dstate_ref):
        @pl.when(pl.program_id(1) == 0)
        def _():
            dstate_ref[...] = jnp.zeros_like(dstate_ref)

        lg = jnp.sum(lg_ref[...] * _head_mask(pl.program_id(0)), axis=1, keepdims=True)
        cs_, sn_ = cs_ref[...], sn_ref[...]
        q = _rot(q_ref[...], cs_, sn_)
        k = _rot(k_ref[...], cs_, sn_) * DK ** -0.5
        _, vjp = jax.vjp(lambda q, k, v, st: _ret_chunk(q, k, v, st, lg), q, k, v_ref[...], st_ref[0, 0])
        dq, dk, dv, dstate_ref[...] = vjp((do_ref[...], dstate_ref[...]))
        d_ref[:, 0:DK] = _rot_t(dq, cs_, sn_).astype(BF16)
        d_ref[:, DK:2 * DK] = _rot_t(dk * DK ** -0.5, cs_, sn_).astype(BF16)
        d_ref[:, 2 * DK:RET_W] = dv.astype(BF16)

    rev = lambda c: nb - 1 - c
    in_specs = _ret_specs(r, rev) + [pl.BlockSpec((1, 1, DK, DV), lambda h, c: (h, rev(c), 0, 0)),
                                     pl.BlockSpec((r, DV), lambda h, c: (rev(c), h)), ANY]
    return pl.pallas_call(
        body, grid=(H, nb), in_specs=in_specs, out_specs=pl.BlockSpec((r, RET_W), lambda h, c: (rev(c), O_RET // RET_W + h)),
        out_shape=_sds(dproj.shape, BF16), input_output_aliases={8: 0},
        scratch_shapes=[pltpu.VMEM((DK, DV), F32)], name="ret_scan_bwd",
        compiler_params=_params(("parallel", "arbitrary")))(proj, proj, proj, cs, sn, lgtab, states, do, dproj)


def _merge(oa, z, ob, rg, ga, gb, wa, wb):
    ya = oa * lax.rsqrt(jnp.mean(oa * oa, axis=-1, keepdims=True) + EPS) * wa * _silu(z)
    mu = jnp.mean(ob, axis=-1, keepdims=True)
    var = jnp.mean(jnp.square(ob - mu), axis=-1, keepdims=True)
    yb = (ob - mu) * lax.rsqrt(var + EPS) * wb * _silu(rg)
    return jax.nn.sigmoid(ga) * ya + jax.nn.sigmoid(gb) * yb


def _merge_specs(ts):
    own = pl.BlockSpec((ts, DV), lambda h, i: (i, h))
    grp = lambda k: pl.BlockSpec((ts, DV), lambda h, i: (i, O_MERGE // DV + 4 * h + k))
    return [own, grp(0), own, grp(1), grp(2), grp(3),
            pl.BlockSpec((1, DV), lambda h, i: (0, 0)), pl.BlockSpec((1, DV), lambda h, i: (0, h))]


def _merge_fwd(oa, ob, proj, wa, wb, ts):
    s = oa.shape[0]

    def body(oa_ref, z_ref, ob_ref, rg_ref, ga_ref, gb_ref, wa_ref, wb_ref, o_ref, ot_ref):
        y = _merge(oa_ref[...], z_ref[...], ob_ref[...], rg_ref[...], ga_ref[...], gb_ref[...],
                   wa_ref[...], wb_ref[...]).astype(BF16)
        o_ref[...] = y
        ot_ref[...] = y.T

    return pl.pallas_call(body, grid=(H, s // ts), in_specs=_merge_specs(ts),
                          out_specs=[pl.BlockSpec((ts, DV), lambda h, i: (i, h)), pl.BlockSpec((DV, ts), lambda h, i: (h, i))],
                          out_shape=[_sds((s, H * DV), BF16), _sds((H * DV, s), BF16)], name="merge_fwd",
                          compiler_params=_params(("parallel", "parallel")))(oa, proj, ob, proj, proj, proj, wa, wb)


def _merge_bwd(oa, ob, proj, wa, wb, dmixed, ts):
    s = oa.shape[0]

    def body(oa_ref, z_ref, ob_ref, rg_ref, ga_ref, gb_ref, wa_ref, wb_ref, dm_ref,
             doa_ref, dob_ref, dgrp_ref, dwa_ref, dwb_ref):
        _, vjp = jax.vjp(_merge, oa_ref[...], z_ref[...], ob_ref[...], rg_ref[...], ga_ref[...], gb_ref[...],
                         wa_ref[...], wb_ref[...])
        doa, dz, dob, drg, dga, dgb, dwa, dwb = vjp(dm_ref[...].astype(F32))
        doa_ref[...] = doa
        dob_ref[...] = dob
        for k, d in enumerate((dz, drg, dga, dgb)):
            dgrp_ref[:, k * DV:(k + 1) * DV] = d.astype(BF16)
        first_tile = pl.program_id(1) == 0

        @pl.when(first_tile & (pl.program_id(0) == 0))
        def _():
            dwa_ref[...] = jnp.zeros_like(dwa_ref)

        @pl.when(first_tile)
        def _():
            dwb_ref[...] = jnp.zeros_like(dwb_ref)

        dwa_ref[...] += dwa
        dwb_ref[...] += dwb

    blk = pl.BlockSpec((ts, DV), lambda h, i: (i, h))
    out_specs = [blk, blk, pl.BlockSpec((ts, MERGE_W), lambda h, i: (i, O_MERGE // MERGE_W + h)),
                 pl.BlockSpec((1, DV), lambda h, i: (0, 0)), pl.BlockSpec((1, DV), lambda h, i: (0, h))]
    out_shape = [_sds((s, H * DV)), _sds((s, H * DV)), _sds((s, P_IN), BF16), _sds((1, DV)), _sds((1, H * DV))]
    return pl.pallas_call(body, grid=(H, s // ts), in_specs=_merge_specs(ts) + [blk], out_specs=out_specs, out_shape=out_shape,
                          name="merge_bwd", compiler_params=_params(("arbitrary", "arbitrary"), 40 * ts * DV * 4))(
                              oa, proj, ob, proj, proj, proj, wa, wb, dmixed)


def _act(hg, hu):
    return _silu(hg) * hu


def _ffn_gate_up(hn, w_gate, w_up, tm, tn):
    s, f = hn.shape[0], w_gate.shape[1]
    tm, tn = min(tm, s), min(tn, f)
    assert s % tm == 0 and f % tn == 0 and tm % 256 == 0
    sub = tm // 2

    def body(a_ref, wg_ref, wu_ref, hg_ref, hu_ref, act_ref, actt_ref):
        for r0 in range(0, tm, sub):
            rs = slice(r0, r0 + sub)
            hg = _bdot(a_ref[rs, :], wg_ref[...], NN)
            hu = _bdot(a_ref[rs, :], wu_ref[...], NN)
            y = _act(hg, hu).astype(BF16)
            hg_ref[rs, :] = hg.astype(BF16)
            hu_ref[rs, :] = hu.astype(BF16)
            act_ref[rs, :] = y
            actt_ref[:, rs] = y.T

    wsp = pl.BlockSpec((D, tn), lambda i, j: (0, j))
    blk = pl.BlockSpec((tm, tn), lambda i, j: (i, j))
    est = 2 * (tm * D * 2 + 2 * D * tn * 2 + 4 * tm * tn * 2) + 4 * sub * tn * 4
    return pl.pallas_call(body, grid=(s // tm, f // tn), in_specs=[pl.BlockSpec((tm, D), lambda i, j: (i, 0)), wsp, wsp],
                          out_specs=[blk, blk, blk, pl.BlockSpec((tn, tm), lambda i, j: (j, i))],
                          out_shape=[_sds((s, f), BF16)] * 3 + [_sds((f, s), BF16)], name="ffn_gate_up",
                          compiler_params=_params(("parallel", "parallel"), est))(hn, w_gate, w_up)


def _ffn_down_dx(dh2, w_down, hg, hu, tm, tn):
    s, f = hg.shape
    tm, tn = min(tm, s), min(tn, f)
    assert s % tm == 0 and f % tn == 0 and tm % 256 == 0
    sub = tm // 2

    def body(d_ref, w_ref, hg_ref, hu_ref, dhg_ref, dhu_ref):
        for r0 in range(0, tm, sub):
            rs = slice(r0, r0 + sub)
            dact = _bdot(d_ref[rs, :], w_ref[...], NT)
            _, vjp = jax.vjp(_act, hg_ref[rs, :].astype(F32), hu_ref[rs, :].astype(F32))
            dhg, dhu = vjp(dact)
            dhg_ref[rs, :] = dhg.astype(BF16)
            dhu_ref[rs, :] = dhu.astype(BF16)

    blk = pl.BlockSpec((tm, tn), lambda i, j: (i, j))
    est = 2 * (tm * D * 4 + tn * D * 2 + 4 * tm * tn * 2) + 6 * sub * tn * 4
    return pl.pallas_call(body, grid=(s // tm, f // tn),
                          in_specs=[pl.BlockSpec((tm, D), lambda i, j: (i, 0)), pl.BlockSpec((tn, D), lambda i, j: (j, 0)), blk, blk],
                          out_specs=[blk, blk], out_shape=[_sds((s, f), BF16)] * 2, name="ffn_down_dx",
                          compiler_params=_params(("parallel", "parallel"), est))(dh2, w_down, hg, hu)


def _loss_rows(h2, wf, tgt):
    err = _rms(h2, wf) - tgt
    return 0.5 * jnp.sum(jnp.mean(err * err, axis=-1, keepdims=True), keepdims=True)


def _loss_fwd_bwd(h2, wf, tgt, ts):
    s = h2.shape[0]

    def body(h_ref, w_ref, t_ref, loss_ref, dh_ref, dw_ref):
        loss, vjp = jax.vjp(_loss_rows, h_ref[...], w_ref[...], t_ref[...])
        dh, dw, _ = vjp(jnp.ones((1, 1), F32))
        dh_ref[...] = dh

        @pl.when(pl.program_id(0) == 0)
        def _():
            loss_ref[...] = jnp.zeros_like(loss_ref)
            dw_ref[...] = jnp.zeros_like(dw_ref)

        loss_ref[...] += loss
        dw_ref[...] += dw

    row = pl.BlockSpec((ts, D), lambda i: (i, 0))
    vec = pl.BlockSpec((1, D), lambda i: (0, 0))
    tile = pl.BlockSpec((8, 128), lambda i: (0, 0))
    return pl.pallas_call(body, grid=(s // ts,), in_specs=[row, vec, row], out_specs=[tile, row, vec],
                          out_shape=[_sds((8, 128)), _sds((s, D)), _sds((1, D))], name="final_norm_loss",
                          compiler_params=_params(("arbitrary",), 12 * ts * D * 4))(h2, wf, tgt)


def _rope_tables(s):
    inv = ROPE_BASE ** (-jnp.arange(0, DK, 2, dtype=F32) / DK)
    ang = jnp.arange(s, dtype=F32)[:, None] * inv[None, :]
    cos, sin = jnp.cos(ang), jnp.sin(ang)
    return jnp.concatenate([cos, cos], axis=1), jnp.concatenate([-sin, sin], axis=1)


def _local_step(x, tgt, w_in, w_out, w_gate, w_up, w_down, norm1_w, conv_w, a_log, dt_bias, gdn_norm_w, ret_norm_w, norm2_w, norm_f_w,
                dist=None):
    s = x.shape[0]
    ts = min(512, s)
    cs, sn = _rope_tables(s)
    lgtab = jnp.log1p(-jnp.exp2(-5.0 - jnp.arange(H, dtype=F32))).reshape(1, H)

    u, u_t = _rms_fwd(x, norm1_w, ts, "norm1_fwd")
    if dist is None:
        proj = _matmul(u, w_in, tm=1024, tn=1280, tk=D, name="in_proj")
    else:
        proj, gathered = _matmul(u, w_in, tm=1024, tn=1280, tk=D, side=_gather_side(dist["shards"]), name="in_proj")
        w_out, w_gate, w_up, w_down = (gathered[0].reshape(D, D), _from_slots_cols(gathered[1]), _from_slots_cols(gathered[2]),
                                       gathered[3].reshape(-1, D))
    qk, va = _gdn_qkv_fwd(proj, conv_w, ts)
    beta, gc, gl = _bg_fwd(proj, a_log, dt_bias, ts)
    inter = _gdn_prep_fwd(qk, va, beta, gc, gl)
    oa, st_a = _gdn_scan_fwd(*inter)
    ob, st_b = _ret_scan_fwd(proj, cs, sn, lgtab)
    mixed, mixed_t = _merge_fwd(oa, ob, proj, gdn_norm_w, ret_norm_w, ts)
    h1 = _matmul(mixed, w_out, tm=1024, tn=1024, tk=D, res=x, name="out_proj")
    hn, hn_t = _rms_fwd(h1, norm2_w, ts, "norm2_fwd")
    hg, hu, act, act_t = _ffn_gate_up(hn, w_gate, w_up, 512, 1408)
    h2 = _matmul(act, w_down, tm=1024, tn=1024, tk=1408, res=h1, name="ffn_down")
    loss, dh2, d_norm_f = _loss_fwd_bwd(h2, norm_f_w, tgt, ts)

    dhg, dhu = _ffn_down_dx(dh2, w_down, hg, hu, 512, 1408)
    g_down = _matmul(act_t, dh2, tm=1408, tn=1024, tk=D, out_dtype=BF16, name="ffn_down_dw")
    g_gate = _matmul(hn_t, dhg, tm=1024, tn=1408, tk=D, out_dtype=BF16, name="ffn_gate_dw")
    g_up = _matmul(hn_t, dhu, tm=1024, tn=1408, tk=D, out_dtype=BF16, name="ffn_up_dw")
    dhn = _matmul(dhg, w_gate, tb=True, tm=1024, tn=1024, tk=1408, name="ffn_gate_dx")
    dhn = _matmul(dhu, w_up, tb=True, tm=1024, tn=1024, tk=1408, res=dhn, name="ffn_up_dx")
    dh1, d_norm2 = _rms_bwd(h1, norm2_w, dhn, dh2, ts, "norm2_bwd")

    g_out = _matmul(mixed_t, dh1, tm=1024, tn=1024, tk=D, out_dtype=BF16, name="out_proj_dw")
    early = ["w_out", "w_gate", "w_up", "w_down"]
    if dist is None:
        dmixed = _matmul(dh1, w_out, tb=True, tm=1024, tn=1024, tk=D, out_dtype=BF16, name="out_proj_dx")
    else:
        slots = dict(w_out=g_out.reshape(NDEV, D // NDEV, D), w_gate=_to_slots_cols(g_gate), w_up=_to_slots_cols(g_up),
                     w_down=g_down.reshape(NDEV, -1, D))
        dmixed, from_sibling = _matmul(dh1, w_out, tb=True, tm=1024, tn=1024, tk=D, out_dtype=BF16,
                                       side=_sibling_side([slots[k] for k in early]), name="out_proj_dx")
        parts = [_add_sibling(slots[k], r, dist["core"], 128, "grads_add_" + k) for k, r in zip(early, from_sibling)]
    doa, dob, dproj, d_gdn_norm, d_ret_norm = _merge_bwd(oa, ob, proj, gdn_norm_w, ret_norm_w, dmixed, ts)

    dproj = _ret_scan_bwd(proj, cs, sn, lgtab, st_b, dob, dproj)
    d_inter = _gdn_scan_bwd(*inter, st_a, doa)
    dqk, dva, dbeta_h, dgc_h, dgl_h = _gdn_prep_bwd(qk, va, beta, gc, gl, *d_inter)
    dproj, d_conv = _gdn_qkv_bwd(proj, conv_w, dqk, dva, dproj, ts)
    dproj, d_a_log, d_dt_bias = _bg_bwd(proj, a_log, dt_bias, dbeta_h, dgc_h, dgl_h, dproj, ts)

    if dist is None:
        g_in = _matmul(u_t, dproj, tm=1024, tn=1280, tk=D, out_dtype=BF16, name="in_proj_dw")
        du = _matmul(dproj, w_in, tb=True, tm=1024, tn=1024, tk=1664, name="in_proj_dx")
        big = dict(w_in=g_in, w_out=g_out, w_gate=g_gate, w_up=g_up, w_down=g_down)
    else:
        g_in, from_chips = _matmul(u_t, dproj, tm=1024, tn=1280, tk=D, out_dtype=BF16, side=_chips_side(parts), name="in_proj_dw")
        du, (from_all,) = _matmul(dproj, w_in, tb=True, tm=1024, tn=1024, tk=1664,
                                  side=_all_to_all_side(_windows_from_layout(g_in)), name="in_proj_dx")
        big = dict(w_in=from_all, **{k: (p, r) for k, p, r in zip(early, parts, from_chips)})
    dx, d_norm1 = _rms_bwd(x, norm1_w, du, dh1, ts, "norm1_bwd")

    small = dict(norm1_w=d_norm1, conv_w=d_conv, a_log=d_a_log, dt_bias=d_dt_bias, gdn_norm_w=d_gdn_norm,
                 ret_norm_w=d_ret_norm, norm2_w=d_norm2, norm_f_w=d_norm_f)
    return loss, dx, big, small


def _coords():
    return lax.axis_index("x"), lax.axis_index("y"), lax.axis_index("c")


def _gather_side(shards):
    n = len(shards)

    def plan(ins, outs, send_sems, recv_sems, local_sems):
        x, y, c = _coords()
        me, sibling = (x, y, c), (x, y, 1 - c)
        chips = [(1 - x, y), (x, 1 - y), (1 - x, 1 - y)]

        def copy(a, k, block, to, src=None):
            px, py, pc = block
            dst = outs[a].at[4 * px + 2 * py + pc]
            return pltpu.make_async_remote_copy(src_ref=dst if src is None else src, dst_ref=dst, send_sem=send_sems.at[a, k],
                                                recv_sem=recv_sems.at[a, k], device_id=to, device_id_type=MESH)

        mine = [pltpu.make_async_copy(ins[a], outs[a].at[4 * x + 2 * y + c], local_sems.at[a]) for a in range(n)]
        first = []
        for a in range(n):
            first.append(copy(a, 0, me, sibling, src=ins[a]))
            first += [copy(a, 1 + j, me, (*chip, c), src=ins[a]) for j, chip in enumerate(chips)]
        return c, me, sibling, chips, copy, mine, first

    def start(ins, outs, *sems):
        *_, mine, first = plan(ins, outs, *sems)
        for cp in mine + first:
            cp.start()

    def finish(ins, outs, *sems):
        c, me, sibling, chips, copy, mine, first = plan(ins, outs, *sems)
        passed = []
        for j, chip in enumerate(chips):
            for a in range(n):
                copy(a, 1 + j, (*chip, c), me).wait_recv()
                fwd = copy(a, 4 + j, (*chip, c), sibling)
                fwd.start()
                passed.append(fwd)
        for a in range(n):
            copy(a, 0, sibling, me).wait_recv()
            for j, chip in enumerate(chips):
                copy(a, 4 + j, (*chip, 1 - c), me).wait_recv()
        for cp in first + passed:
            cp.wait_send()
        for cp in mine:
            cp.wait()

    return _Side(shards, [_sds((NDEV,) + a.shape, a.dtype) for a in shards], [(n, 7), (n, 7), (n,)], start, finish)


def _exchange_side(ins, n_out, copies_of):
    def start(in_refs, out_refs, *sems):
        for cp in copies_of(in_refs, out_refs, *sems):
            cp.start()

    def finish(in_refs, out_refs, *sems):
        for cp in copies_of(in_refs, out_refs, *sems):
            cp.wait()

    n = len(ins)
    return _Side(ins, [_sds((n_out,) + a.shape[1:], a.dtype) for a in ins], [(n, n_out), (n, n_out)], start, finish)


def _sibling_side(slots):
    def copies_of(ins, outs, send_sems, recv_sems):
        x, y, c = _coords()
        return [pltpu.make_async_remote_copy(
            src_ref=ins[a].at[2 * j + (1 - c)], dst_ref=outs[a].at[j], send_sem=send_sems.at[a, j], recv_sem=recv_sems.at[a, j],
            device_id=(x, y, 1 - c), device_id_type=MESH) for a in range(len(slots)) for j in range(4)]

    return _exchange_side(slots, 4, copies_of)


def _chips_side(parts):
    def copies_of(ins, outs, send_sems, recv_sems):
        x, y, c = _coords()
        chips = [(1 - x, y), (x, 1 - y), (1 - x, 1 - y)]
        return [pltpu.make_async_remote_copy(
            src_ref=ins[a].at[2 * px + py], dst_ref=outs[a].at[k], send_sem=send_sems.at[a, k], recv_sem=recv_sems.at[a, k],
            device_id=(px, py, c), device_id_type=MESH) for a in range(len(parts)) for k, (px, py) in enumerate(chips)]

    return _exchange_side(parts, 3, copies_of)


def _all_to_all_side(slots):
    def plan(ins, outs, send_sems, recv_sems, local_sems):
        x, y, c = _coords()
        mine = 4 * x + 2 * y + c
        own = pltpu.make_async_copy(ins[0].at[mine], outs[0].at[mine], local_sems.at[0])
        remote = []
        for r in range(1, NDEV):
            peer = (x ^ (r >> 2), y ^ ((r >> 1) & 1), c ^ (r & 1))
            remote.append(pltpu.make_async_remote_copy(
                src_ref=ins[0].at[mine ^ r], dst_ref=outs[0].at[mine], send_sem=send_sems.at[r - 1], recv_sem=recv_sems.at[r - 1],
                device_id=peer, device_id_type=MESH))
        return own, remote

    def start(ins, outs, *sems):
        own, remote = plan(ins, outs, *sems)
        for cp in [own] + remote:
            cp.start()

    def finish(ins, outs, *sems):
        own, remote = plan(ins, outs, *sems)
        for cp in remote:
            cp.wait()
        own.wait()

    return _Side([slots], [_sds(slots.shape, slots.dtype)], [(NDEV - 1,), (NDEV - 1,), (1,)], start, finish)


def _allreduce_small(pack, name):
    rows, cols = pack.shape

    def body(in_ref, out_ref, buf_ref, send_sems, recv_sems):
        x, y, c = _coords()
        mine = 4 * x + 2 * y + c
        buf_ref[mine] = in_ref[...]
        copies = []
        for r in range(1, NDEV):
            peer = (x ^ (r >> 2), y ^ ((r >> 1) & 1), c ^ (r & 1))
            copies.append(pltpu.make_async_remote_copy(
                src_ref=in_ref, dst_ref=buf_ref.at[mine], send_sem=send_sems.at[r - 1], recv_sem=recv_sems.at[r - 1],
                device_id=peer, device_id_type=MESH))
        for cp in copies:
            cp.start()
        for r in range(1, NDEV):
            pltpu.make_async_remote_copy(
                src_ref=in_ref, dst_ref=buf_ref.at[mine ^ r], send_sem=send_sems.at[r - 1], recv_sem=recv_sems.at[r - 1],
                device_id=(x, y, c), device_id_type=MESH).wait_recv()
        for cp in copies:
            cp.wait_send()
        acc = buf_ref[0]
        for d in range(1, NDEV):
            acc = acc + buf_ref[d]
        out_ref[...] = acc

    return pl.pallas_call(
        body, in_specs=[VMEM_FULL], out_specs=VMEM_FULL, out_shape=_sds((rows, cols)),
        scratch_shapes=[pltpu.VMEM((NDEV, rows, cols), F32), pltpu.SemaphoreType.DMA((NDEV - 1,)), pltpu.SemaphoreType.DMA((NDEV - 1,))],
        name=name)(pack)


def _add_sibling(slots, recv, core, tr, name):
    _, rows, cols = slots.shape
    tr = _row_tile(rows, tr)

    def body(c_ref, a_ref, b_ref, o_ref):
        o_ref[...] = (a_ref[...].astype(F32) + b_ref[...].astype(F32)).astype(BF16)

    gs = pltpu.PrefetchScalarGridSpec(
        num_scalar_prefetch=1, grid=(4, rows // tr),
        in_specs=[pl.BlockSpec((None, tr, cols), lambda j, i, cr: (2 * j + cr[0], i, 0)),
                  pl.BlockSpec((None, tr, cols), lambda j, i, cr: (j, i, 0))],
        out_specs=pl.BlockSpec((None, tr, cols), lambda j, i, cr: (j, i, 0)))
    return pl.pallas_call(body, grid_spec=gs, out_shape=_sds((4, rows, cols), BF16), name=name,
                          compiler_params=_params(("parallel", "parallel"), 6 * tr * cols * 4))(core, slots, recv)


def _adam_math(w, g, m, v):
    m2 = B1 * m + (1.0 - B1) * g
    v2 = B2 * v + (1.0 - B2) * jnp.square(g)
    m_hat = m2 / (1.0 - B1 ** STEP)
    v_hat = v2 / (1.0 - B2 ** STEP)
    return -LR * (m_hat / (jnp.sqrt(v_hat) + EPS_ADAM) + WD * w), m2, v2


def _adamw_reduced(part, recv, chip, w, m, v, tr, name):
    rows, cols = w.shape
    tr = _row_tile(rows, tr)

    def body(j_ref, p_ref, r0_ref, r1_ref, r2_ref, w_ref, m_ref, v_ref, g_ref, d_ref, nm_ref, nv_ref):
        g = p_ref[...].astype(F32) + r0_ref[...].astype(F32) + r1_ref[...].astype(F32) + r2_ref[...].astype(F32)
        d, m2, v2 = _adam_math(w_ref[...], g, m_ref[...], v_ref[...])
        g_ref[...] = g
        d_ref[...] = d
        nm_ref[...] = m2
        nv_ref[...] = v2

    flat = pl.BlockSpec((tr, cols), lambda i, jr: (i, 0))
    gs = pltpu.PrefetchScalarGridSpec(
        num_scalar_prefetch=1, grid=(rows // tr,),
        in_specs=[pl.BlockSpec((None, tr, cols), lambda i, jr: (jr[0], i, 0))]
        + [pl.BlockSpec((None, tr, cols), functools.partial(lambda i, jr, k: (k, i, 0), k=k)) for k in range(3)] + [flat] * 3,
        out_specs=[flat] * 4)
    return pl.pallas_call(body, grid_spec=gs, out_shape=[_sds((rows, cols))] * 4, name=name,
                          compiler_params=_params(("parallel",), 22 * tr * cols * 4))(chip, part, recv, recv, recv, w, m, v)


def _sum_slots(recv, tr, name):
    _, rows, cols = recv.shape
    tr = _row_tile(rows, tr)

    def body(*refs):
        acc = refs[0][...].astype(F32)
        for p_ref in refs[1:NDEV]:
            acc = acc + p_ref[...].astype(F32)
        refs[NDEV][...] = acc

    slot = [pl.BlockSpec((None, tr, cols), functools.partial(lambda i, k: (k, i, 0), k=k)) for k in range(NDEV)]
    return pl.pallas_call(body, grid=(rows // tr,), in_specs=slot, out_specs=pl.BlockSpec((tr, cols), lambda i: (i, 0)),
                          out_shape=_sds((rows, cols)), name=name, compiler_params=_params(("parallel",)))(*([recv] * NDEV))


def _adamw_rows(w, g, m, v, tr, name):
    rows, cols = w.shape
    tr = _row_tile(rows, tr)

    def body(w_ref, g_ref, m_ref, v_ref, d_ref, nm_ref, nv_ref):
        d_ref[...], nm_ref[...], nv_ref[...] = _adam_math(w_ref[...], g_ref[...], m_ref[...], v_ref[...])

    flat = pl.BlockSpec((tr, cols), lambda i: (i, 0))
    return pl.pallas_call(body, grid=(rows // tr,), in_specs=[flat] * 4, out_specs=[flat] * 3, out_shape=[_sds((rows, cols))] * 3,
                          name=name, compiler_params=_params(("parallel",)))(w, g, m, v)


def _adamw_plain(w, g, m, v, name):
    def body(w_ref, g_ref, m_ref, v_ref, d_ref, nm_ref, nv_ref):
        d, m2, v2 = _adam_math(w_ref[...], g_ref[...], m_ref[...], v_ref[...])
        d_ref[...] = d
        nm_ref[...] = m2
        nv_ref[...] = v2

    return pl.pallas_call(body, out_shape=[_sds(w.shape)] * 3, name=name)(w, g, m, v)


def _pack_small(norm1_w, conv_w, a_log, dt_bias, gdn_norm_w, ret_norm_w, norm2_w, norm_f_w):
    misc = jnp.concatenate([gdn_norm_w.reshape(1, DV), a_log.reshape(1, H), dt_bias.reshape(1, H),
                            jnp.zeros((1, D - DV - 2 * H), F32)], axis=1)
    return jnp.concatenate([norm1_w.reshape(1, D), ret_norm_w.reshape(1, D), norm2_w.reshape(1, D), norm_f_w.reshape(1, D),
                            conv_w.reshape(8, D), misc, jnp.zeros((3, D), F32)], axis=0)


def _unpack_small(pack):
    return dict(norm1_w=pack[0:1], ret_norm_w=pack[1:2], norm2_w=pack[2:3], norm_f_w=pack[3], conv_w=pack[4:12].reshape(4, 2 * D),
                gdn_norm_w=pack[12:13, 0:DV], a_log=pack[12:13, DV:DV + H], dt_bias=pack[12:13, DV + H:DV + 2 * H])


IN_SPLITS = (4096, 2048, 8, 8, 1024, 1024, 2048, 2048, 2048, 2048)


BA_END = sum(IN_SPLITS[:4])
LANES = 128


def _padded_order_blocks():
    z0, ba0, rq0, rk0, rv0, rg0, ga0, gb0 = 4096, 6144, 6400, 7424, 8448, 10496, 12544, 14592
    cols = []
    for h in range(H):
        for base in (z0, rg0, ga0, gb0):
            cols += [base + DV * h, base + DV * h + LANES]
    cols += list(range(0, z0, LANES))
    for h in range(H):
        cols += [rq0 + DK * h, rk0 + DK * h, rv0 + DV * h, rv0 + DV * h + LANES]
    cols += [ba0, ba0 + LANES]
    blocks = np.asarray(cols, np.int32) // LANES
    assert sorted(blocks.tolist()) == list(range(P_IN // LANES))
    return blocks


def _permute_blocks(x, blocks, name):
    rows, cols = x.shape

    def body(p_ref, x_ref, o_ref):
        o_ref[...] = x_ref[...]

    gs = pltpu.PrefetchScalarGridSpec(num_scalar_prefetch=1, grid=(cols // LANES,),
                                      in_specs=[pl.BlockSpec((rows, LANES), lambda j, p: (0, p[j]))],
                                      out_specs=pl.BlockSpec((rows, LANES), lambda j, p: (0, j)))
    return pl.pallas_call(body, grid_spec=gs, out_shape=_sds((rows, cols), x.dtype), name=name,
                          compiler_params=_params(("parallel",)))(jnp.asarray(blocks), x)


def _regroup_w_in(w):
    padded = jnp.concatenate([w[:, :BA_END], jnp.zeros((w.shape[0], P_IN - N_IN), w.dtype), w[:, BA_END:]], axis=1)
    return _permute_blocks(padded, _padded_order_blocks(), "w_in_to_layout")


def _ungroup_w_in(g):
    padded = _permute_blocks(g, np.argsort(_padded_order_blocks()).astype(np.int32), "w_in_grad_from_layout")
    return jnp.concatenate([padded[:, :BA_END], padded[:, BA_END + P_IN - N_IN:]], axis=1)


SHARD_W = N_IN // NDEV
GAP = P_IN - N_IN
WIN = 2304


def _padded_col(c):
    return c + (GAP if c >= BA_END else 0)


WIN_START = [min(_padded_col(SHARD_W * d) // LANES * LANES, P_IN - WIN) for d in range(NDEV)]
WIN_OFF = [_padded_col(SHARD_W * d) - WIN_START[d] for d in range(NDEV)]
STRADDLER = BA_END // SHARD_W
STRADDLE_AT = BA_END - STRADDLER * SHARD_W
assert all(WIN_OFF[d] + SHARD_W + (GAP if d == STRADDLER else 0) <= WIN for d in range(NDEV))


def _win_off(me):
    off = jnp.int32(0)
    for d in range(NDEV):
        off = jnp.where(me == d, jnp.int32(WIN_OFF[d]), off)
    return off


def _window_of_shard(shard, me):
    rows = shard.shape[0]
    zeros = lambda n: jnp.zeros((rows, n), shard.dtype)
    plain = lax.dynamic_update_slice(zeros(WIN), shard, (0, _win_off(me)))
    o = WIN_OFF[STRADDLER]
    split = jnp.concatenate([zeros(o), shard[:, :STRADDLE_AT], zeros(GAP), shard[:, STRADDLE_AT:], zeros(WIN - o - GAP - SHARD_W)], axis=1)
    return jnp.where(me == STRADDLER, split, plain)


def _shard_of_window(win, me):
    plain = lax.dynamic_slice(win, (0, _win_off(me)), (win.shape[0], SHARD_W))
    o = WIN_OFF[STRADDLER]
    split = jnp.concatenate([win[:, o:o + STRADDLE_AT], win[:, o + STRADDLE_AT + GAP:o + GAP + SHARD_W]], axis=1)
    return jnp.where(me == STRADDLER, split, plain)


def _layout_from_windows(wins):
    _, rows, _ = wins.shape
    data = []
    for d in range(NDEV):
        lo = _padded_col(SHARD_W * d)
        data.append([(lo, lo + STRADDLE_AT), (lo + STRADDLE_AT + GAP, lo + GAP + SHARD_W)] if d == STRADDLER else [(lo, lo + SHARD_W)])
    zero_block = (0, WIN // LANES - 1)
    table = []
    for p in _padded_order_blocks():
        src = [(d, int(p) - WIN_START[d] // LANES) for d in range(NDEV)
               if any(lo < (p + 1) * LANES and hi > p * LANES for lo, hi in data[d])]
        assert len(src) <= 2 and all(0 <= b < WIN // LANES for _, b in src)
        src += [zero_block] * (2 - len(src))
        table.append([src[0][0], src[0][1], src[1][0], src[1][1]])
    table = np.asarray(table, np.int32).T.copy()

    def body(t_ref, a_ref, b_ref, o_ref):
        o_ref[...] = a_ref[...] + b_ref[...]

    gs = pltpu.PrefetchScalarGridSpec(
        num_scalar_prefetch=1, grid=(P_IN // LANES,),
        in_specs=[pl.BlockSpec((None, rows, LANES), lambda j, t: (t[0, j], 0, t[1, j])),
                  pl.BlockSpec((None, rows, LANES), lambda j, t: (t[2, j], 0, t[3, j]))],
        out_specs=pl.BlockSpec((rows, LANES), lambda j, t: (0, j)))
    return pl.pallas_call(body, grid_spec=gs, out_shape=_sds((rows, P_IN), wins.dtype), name="w_in_from_windows",
                          compiler_params=_params(("parallel",)))(jnp.asarray(table), wins, wins)


def _windows_from_layout(g):
    rows = g.shape[0]
    where = np.argsort(_padded_order_blocks())
    nb = WIN // LANES
    table = np.asarray([where[WIN_START[d] // LANES + b] for d in range(NDEV) for b in range(nb)], np.int32)

    def body(t_ref, x_ref, o_ref):
        o_ref[...] = x_ref[...]

    gs = pltpu.PrefetchScalarGridSpec(num_scalar_prefetch=1, grid=(NDEV, nb),
                                      in_specs=[pl.BlockSpec((rows, LANES), lambda d, b, t: (0, t[d * nb + b]))],
                                      out_specs=pl.BlockSpec((None, rows, LANES), lambda d, b, t: (d, 0, b)))
    return pl.pallas_call(body, grid_spec=gs, out_shape=_sds((NDEV, rows, WIN), g.dtype), name="w_in_grad_windows",
                          compiler_params=_params(("parallel", "parallel")))(jnp.asarray(table), g)


def _to_slots_cols(g):
    rows, cols = g.shape
    return g.reshape(rows, NDEV, cols // NDEV).transpose(1, 0, 2)


def _from_slots_cols(a):
    n, rows, cols = a.shape
    return a.transpose(1, 0, 2).reshape(rows, n * cols)


WEIGHT_ORDER = ["norm1_w", "w_in", "conv_w", "a_log", "dt_bias", "gdn_norm_w", "ret_norm_w", "w_out", "norm2_w", "w_gate", "w_up",
                "w_down", "norm_f_w"]


def kernel(x, norm1_w, w_in, conv_w, a_log, dt_bias, gdn_norm_w, ret_norm_w, w_out, norm2_w, w_gate, w_up, w_down, norm_f_w, loss_target, m_norm1_w, m_w_in, m_conv_w, m_a_log, m_dt_bias, m_gdn_norm_w, m_ret_norm_w, m_w_out, m_norm2_w, m_w_gate, m_w_up, m_w_down, m_norm_f_w, v_norm1_w, v_w_in, v_conv_w, v_a_log, v_dt_bias, v_gdn_norm_w, v_ret_norm_w, v_w_out, v_norm2_w, v_w_gate, v_w_up, v_w_down, v_norm_f_w):
    ax, ay, ac = _coords()
    me = 4 * ax + 2 * ay + ac
    core = jnp.reshape(ac, (1,)).astype(jnp.int32)
    chip = jnp.reshape(2 * ax + ay, (1,)).astype(jnp.int32)
    w = dict(norm1_w=norm1_w, w_in=w_in[0], conv_w=conv_w[0], a_log=a_log, dt_bias=dt_bias, gdn_norm_w=gdn_norm_w,
             ret_norm_w=ret_norm_w, w_out=w_out[0], norm2_w=norm2_w, w_gate=w_gate[0], w_up=w_up[0], w_down=w_down[0],
             norm_f_w=norm_f_w)
    m = dict(norm1_w=m_norm1_w, w_in=m_w_in[0], conv_w=m_conv_w[0], a_log=m_a_log, dt_bias=m_dt_bias, gdn_norm_w=m_gdn_norm_w,
             ret_norm_w=m_ret_norm_w, w_out=m_w_out[0], norm2_w=m_norm2_w, w_gate=m_w_gate[0], w_up=m_w_up[0], w_down=m_w_down[0],
             norm_f_w=m_norm_f_w)
    v = dict(norm1_w=v_norm1_w, w_in=v_w_in[0], conv_w=v_conv_w[0], a_log=v_a_log, dt_bias=v_dt_bias, gdn_norm_w=v_gdn_norm_w,
             ret_norm_w=v_ret_norm_w, w_out=v_w_out[0], norm2_w=v_norm2_w, w_gate=v_w_gate[0], w_up=v_w_up[0], w_down=v_w_down[0],
             norm_f_w=v_norm_f_w)
    big_names = ["w_in", "w_out", "w_gate", "w_up", "w_down"]

    w_in_wins, conv_all = _run_side(_gather_side([_window_of_shard(w["w_in"].astype(BF16), me), w["conv_w"]]), "w_in_allgather")
    w_in_full = _layout_from_windows(w_in_wins)
    conv_full = _from_slots_cols(conv_all)
    dist = dict(core=core, shards=[w[k].astype(BF16) for k in ("w_out", "w_gate", "w_up", "w_down")])

    loss_tile, dx, big, small = _local_step(
        x[0], loss_target[0], w_in_full, None, None, None, None, norm1_w, conv_full, a_log, dt_bias,
        gdn_norm_w, ret_norm_w, norm2_w, norm_f_w.reshape(1, D), dist=dist)
    loss = lax.psum(loss_tile[0, 0], ("x", "y", "c"))

    g_w_in = _shard_of_window(_sum_slots(big["w_in"], 64, "w_in_grad_sum"), me)
    out = {"w_in": (g_w_in, *_adamw_rows(w["w_in"], g_w_in, m["w_in"], v["w_in"], 64, "adamw_w_in"))}
    for k in ("w_out", "w_gate", "w_up", "w_down"):
        part, recv = big[k]
        out[k] = _adamw_reduced(part, recv, chip, w[k], m[k], v[k], 128, "adamw_" + k)

    g_small = _unpack_small(_allreduce_small(_pack_small(**small), "small_grads_allreduce"))
    g_small["conv_w"] = lax.dynamic_slice_in_dim(g_small["conv_w"], me * (2 * D // NDEV), 2 * D // NDEV, axis=1)
    small_names = [k for k in WEIGHT_ORDER if k not in big_names]
    pad_conv = lambda a: jnp.pad(a, ((0, 0), (0, 2 * D - a.shape[1])))
    packs = []
    for src in (w, g_small, m, v):
        args = {k: (pad_conv(src[k]) if k == "conv_w" else src[k]) for k in small_names}
        packs.append(_pack_small(**args))
    d_pack, m_pack, v_pack = _adamw_plain(*packs[0:1], packs[1], packs[2], packs[3], name="adamw_small")
    cut_conv = lambda dct: {**dct, "conv_w": dct["conv_w"][:, :2 * D // NDEV]}
    d_small, m_small, v_small = (cut_conv(_unpack_small(p)) for p in (d_pack, m_pack, v_pack))

    def shaped(k, a):
        return a.reshape(w_shapes[k])

    w_shapes = dict(norm1_w=norm1_w.shape, w_in=w_in.shape, conv_w=conv_w.shape, a_log=a_log.shape, dt_bias=dt_bias.shape,
                    gdn_norm_w=gdn_norm_w.shape, ret_norm_w=ret_norm_w.shape, w_out=w_out.shape, norm2_w=norm2_w.shape,
                    w_gate=w_gate.shape, w_up=w_up.shape, w_down=w_down.shape, norm_f_w=norm_f_w.shape)
    grads, deltas, new_m, new_v = [], [], [], []
    for k in WEIGHT_ORDER:
        if k in big_names:
            g_, d_, m_, v_ = out[k]
        else:
            g_, d_, m_, v_ = g_small[k], d_small[k], m_small[k], v_small[k]
        grads.append(shaped(k, g_))
        deltas.append(shaped(k, d_))
        new_m.append(shaped(k, m_))
        new_v.append(shaped(k, v_))
    return (loss, dx[None], *grads, *deltas, *new_m, *new_v)
```

```python
import functools
import numpy as np
import jax
import jax.numpy as jnp
from jax import lax
from jax.experimental import pallas as pl
from jax.experimental.pallas import tpu as pltpu

F32, BF16 = jnp.float32, jnp.bfloat16
HI = lax.Precision.HIGHEST
MESH = pl.DeviceIdType.MESH
ANY = pl.BlockSpec(memory_space=pl.ANY)
VMEM_FULL = pl.BlockSpec(memory_space=pltpu.VMEM)

NDEV = 8
D = 2048
H = 8
DK = 128
DV = 256
C = 64
CPB = 4
EPS = 1e-6
ROPE_BASE = 10000.0
N_IN = 16400
O_MERGE, O_QKV, O_RET, O_BA, P_IN = 0, 8192, 12288, 16384, 16640
MERGE_W, RET_W = 4 * DV, 2 * DK + DV
LR, B1, B2, EPS_ADAM, WD, STEP = 0.001, 0.9, 0.999, 1e-08, 0.01, 10
VMEM_CAP = 60 * 1024 * 1024

NN = ((1,), (0,))
NT = ((1,), (1,))
TN = ((0,), (0,))


def _params(sem=None, est=None):
    kw = {}
    if sem is not None:
        kw["dimension_semantics"] = sem
    if est is not None:
        kw["vmem_limit_bytes"] = int(min(VMEM_CAP, max(32 * 1024 * 1024, est * 5 // 4 + (4 << 20))))
    return pltpu.CompilerParams(**kw)


def _sds(shape, dt=F32):
    return jax.ShapeDtypeStruct(tuple(shape), dt)


def _row_tile(rows, limit):
    return max(t for t in range(16, min(rows, limit) + 1, 16) if rows % t == 0)


def _bdot(a, b, dims):
    return lax.dot_general(a.astype(BF16), b.astype(BF16), (dims, ((), ())), preferred_element_type=F32)


def _hdot(a, b, dims):
    return lax.dot_general(a, b, (dims, ((), ())), precision=HI, preferred_element_type=F32)


_sigmoid = jax.nn.sigmoid


def _silu(x):
    return x * _sigmoid(x)


def _rms(x, w):
    return x * lax.rsqrt(jnp.mean(x * x, axis=-1, keepdims=True) + EPS) * w


class _Side:
    def __init__(self, ins, out_shapes, sems, start, finish):
        self.ins, self.out_shapes, self.sems, self.start, self.finish = list(ins), list(out_shapes), list(sems), start, finish


def _run_side(side, name):
    ni, no = len(side.ins), len(side.out_shapes)

    def body(*refs):
        ins, outs, sems = refs[:ni], refs[ni:ni + no], refs[ni + no:]
        side.start(ins, outs, *sems)
        side.finish(ins, outs, *sems)

    return pl.pallas_call(body, in_specs=[ANY] * ni, out_specs=[ANY] * no, out_shape=side.out_shapes,
                          scratch_shapes=[pltpu.SemaphoreType.DMA(s) for s in side.sems], name=name)(*side.ins)


def _matmul(a, b, *, ta=False, tb=False, tm, tn, tk, out_dtype=F32, res=None, side=None, name):
    m = a.shape[1] if ta else a.shape[0]
    k = a.shape[0] if ta else a.shape[1]
    n = b.shape[0] if tb else b.shape[1]
    assert k == (b.shape[1] if tb else b.shape[0])
    tm, tn, tk = min(tm, m), min(tn, n), min(tk, k)
    assert m % tm == 0 and n % tn == 0 and k % tk == 0, (name, m, n, k, tm, tn, tk)
    nk = k // tk
    dims = ((0 if ta else 1,), (1 if tb else 0,))
    has_res = res is not None
    n_in = 3 if has_res else 2
    n_side_in = len(side.ins) if side else 0
    n_side_out = len(side.out_shapes) if side else 0
    grid = (m // tm, n // tn, nk)

    def body(*refs):
        a_ref, b_ref = refs[0], refs[1]
        r_ref = refs[2] if has_res else None
        o_ref = refs[n_in + n_side_in]
        if side:
            side_ins = refs[n_in:n_in + n_side_in]
            side_outs = refs[n_in + n_side_in + 1:n_in + n_side_in + 1 + n_side_out]
            side_sems = refs[len(refs) - len(side.sems):]
            step = (pl.program_id(0) * grid[1] + pl.program_id(1)) * grid[2] + pl.program_id(2)

            @pl.when(step == 0)
            def _():
                side.start(side_ins, side_outs, *side_sems)

        def finish(acc):
            if has_res:
                acc = acc + r_ref[...].astype(F32)
            o_ref[...] = acc.astype(out_dtype)

        part = _bdot(a_ref[...], b_ref[...], dims)
        if nk == 1:
            finish(part)
        else:
            acc_ref = refs[n_in + n_side_in + 1 + n_side_out]
            kk = pl.program_id(2)

            @pl.when(kk == 0)
            def _():
                acc_ref[...] = part

            @pl.when(kk > 0)
            def _():
                acc_ref[...] += part

            @pl.when(kk == nk - 1)
            def _():
                finish(acc_ref[...])

        if side:
            @pl.when(step == grid[0] * grid[1] * grid[2] - 1)
            def _():
                side.finish(side_ins, side_outs, *side_sems)

    a_spec = pl.BlockSpec((tk, tm), lambda i, j, kk: (kk, i)) if ta else pl.BlockSpec((tm, tk), lambda i, j, kk: (i, kk))
    b_spec = pl.BlockSpec((tn, tk), lambda i, j, kk: (j, kk)) if tb else pl.BlockSpec((tk, tn), lambda i, j, kk: (kk, j))
    o_spec = pl.BlockSpec((tm, tn), lambda i, j, kk: (i, j))
    in_specs = [a_spec, b_spec] + ([o_spec] if has_res else []) + [ANY] * n_side_in
    est = 2 * (tm * tk * a.dtype.itemsize + tk * tn * b.dtype.itemsize + tm * tn * jnp.dtype(out_dtype).itemsize)
    est += 2 * tm * tn * 4 * (1 if has_res else 0) + (tm * tn * 4 if nk > 1 else 0) + 2 * tm * tn * 4
    args = (a, b) + ((res,) if has_res else ()) + (tuple(side.ins) if side else ())
    scratch = ([pltpu.VMEM((tm, tn), F32)] if nk > 1 else []) + ([pltpu.SemaphoreType.DMA(s) for s in side.sems] if side else [])
    sem = ("arbitrary",) * 3 if side else ("parallel", "parallel", "arbitrary")
    out = pl.pallas_call(
        body, grid=grid, in_specs=in_specs, out_specs=[o_spec] + [ANY] * n_side_out,
        out_shape=[_sds((m, n), out_dtype)] + (side.out_shapes if side else []),
        scratch_shapes=scratch, name=name, compiler_params=_params(sem, est))(*args)
    return (out[0], out[1:]) if side else out[0]


def _rms_fwd(x, w, ts, name):
    s = x.shape[0]

    def body(x_ref, w_ref, o_ref, ot_ref):
        y = _rms(x_ref[...], w_ref[...]).astype(BF16)
        o_ref[...] = y
        ot_ref[...] = y.T

    row = pl.BlockSpec((ts, D), lambda i: (i, 0))
    return pl.pallas_call(body, grid=(s // ts,), in_specs=[row, pl.BlockSpec((1, D), lambda i: (0, 0))],
                          out_specs=[row, pl.BlockSpec((D, ts), lambda i: (0, i))],
                          out_shape=[_sds((s, D), BF16), _sds((D, s), BF16)], name=name,
                          compiler_params=_params(("parallel",)))(x, w)


def _rms_bwd(x, w, du, dres, ts, name):
    s = x.shape[0]

    def body(x_ref, w_ref, du_ref, dres_ref, dx_ref, dw_ref):
        _, vjp = jax.vjp(_rms, x_ref[...], w_ref[...])
        dx, dw = vjp(du_ref[...].astype(F32))
        dx_ref[...] = dx + dres_ref[...]

        @pl.when(pl.program_id(0) == 0)
        def _():
            dw_ref[...] = jnp.zeros_like(dw_ref)

        dw_ref[...] += dw

    row = pl.BlockSpec((ts, D), lambda i: (i, 0))
    vec = pl.BlockSpec((1, D), lambda i: (0, 0))
    return pl.pallas_call(body, grid=(s // ts,), in_specs=[row, vec, row, row], out_specs=[row, vec],
                          out_shape=[_sds((s, D)), _sds((1, D))], name=name,
                          compiler_params=_params(("arbitrary",), 12 * ts * D * 4))(x, w, du, dres)


def _conv_taps(xx, w, base, ts):
    acc = xx[base:base + ts] * w[0:1, :]
    for j in range(1, 4):
        acc = acc + xx[base + j:base + j + ts] * w[j:j + 1, :]
    return acc


def _causal_conv(prev8, cur, w, first):
    xx = jnp.concatenate([jnp.where(first, 0.0, prev8), cur], axis=0)
    return _conv_taps(xx, w, 5, cur.shape[0])


def _qk_post(c, scale):
    s = _silu(c)
    return s * lax.rsqrt(jnp.sum(s * s, axis=-1, keepdims=True) + EPS) * scale


def _conv_specs(ts, cw, col0):
    pcol = O_QKV // cw + col0
    cur = pl.BlockSpec((ts, cw), lambda j, i: (i, pcol + j))
    prev = pl.BlockSpec((8, cw), lambda j, i: (jnp.maximum(i * (ts // 8) - 1, 0), pcol + j))
    wsp = pl.BlockSpec((4, cw), lambda j, i: (0, col0 + j))
    return cur, prev, wsp


def _gdn_qkv_fwd(proj, conv_w, ts):
    s = proj.shape[0]

    def qk_body(cur_ref, prev_ref, w_ref, o_ref, c_ref):
        c = _causal_conv(prev_ref[...], cur_ref[...], w_ref[...], pl.program_id(1) == 0)
        scale = jnp.where(pl.program_id(0) < H, DK ** -0.5, 1.0).astype(F32)
        c_ref[...] = c
        o_ref[...] = _qk_post(c, scale)

    cur, prev, wsp = _conv_specs(ts, DK, 0)
    out = pl.BlockSpec((ts, DK), lambda j, i: (i, j))
    qk, c_qk = pl.pallas_call(qk_body, grid=(2 * H, s // ts), in_specs=[cur, prev, wsp], out_specs=[out, out],
                              out_shape=[_sds((s, 2 * H * DK))] * 2, name="gdn_qk_prep",
                              compiler_params=_params(("parallel", "parallel")))(proj, proj, conv_w)

    def v_body(cur_ref, prev_ref, w_ref, o_ref, c_ref):
        c = _causal_conv(prev_ref[...], cur_ref[...], w_ref[...], pl.program_id(1) == 0)
        c_ref[...] = c
        o_ref[...] = _silu(c)

    cw = 512
    cur, prev, wsp = _conv_specs(ts, cw, 2 * H * DK // cw)
    out = pl.BlockSpec((ts, cw), lambda j, i: (i, j))
    v, c_v = pl.pallas_call(v_body, grid=(H * DV // cw, s // ts), in_specs=[cur, prev, wsp], out_specs=[out, out],
                            out_shape=[_sds((s, H * DV))] * 2, name="gdn_v_prep",
                            compiler_params=_params(("parallel", "parallel")))(proj, proj, conv_w)
    return qk, v, c_qk, c_v


def _gdn_qkv_bwd(proj, conv_w, c_qk, c_v, dqk, dv, dproj, ts):
    s = proj.shape[0]
    nt = s // ts

    def qk_body(c_ref, d_ref, o_ref):
        scale = jnp.where(pl.program_id(0) < H, DK ** -0.5, 1.0).astype(F32)
        _, vjp = jax.vjp(lambda cc: _qk_post(cc, scale), c_ref[...])
        o_ref[...] = vjp(d_ref[...])[0]

    blk = pl.BlockSpec((ts, DK), lambda j, i: (i, j))
    dc_qk = pl.pallas_call(qk_body, grid=(2 * H, nt), in_specs=[blk, blk], out_specs=blk, out_shape=_sds((s, 2 * H * DK)),
                           name="gdn_qk_prep_bwd", compiler_params=_params(("parallel", "parallel")))(c_qk, dqk)

    def v_body(c_ref, d_ref, o_ref):
        _, vjp = jax.vjp(_silu, c_ref[...])
        o_ref[...] = vjp(d_ref[...])[0]

    cw = 512
    blk = pl.BlockSpec((ts, cw), lambda j, i: (i, j))
    dc_v = pl.pallas_call(v_body, grid=(H * DV // cw, nt), in_specs=[blk, blk], out_specs=blk, out_shape=_sds((s, H * DV)),
                          name="gdn_v_prep_bwd", compiler_params=_params(("parallel", "parallel")))(c_v, dv)

    def conv_bwd(dc, dproj, col0, ncols, name):
        def body(x_ref, xprev_ref, w_ref, dc_ref, dcnext_ref, _, da_ref, dw_ref):
            i = pl.program_id(1)
            w = w_ref[...]
            dcur = dc_ref[...]
            dd = jnp.concatenate([dcur, jnp.where(i == nt - 1, 0.0, dcnext_ref[...])], axis=0)
            acc = dd[3:3 + ts] * w[0:1, :]
            for j in range(1, 4):
                acc = acc + dd[3 - j:3 - j + ts] * w[j:j + 1, :]
            da_ref[...] = acc.astype(BF16)
            xx = jnp.concatenate([jnp.where(i == 0, 0.0, xprev_ref[...]), x_ref[...]], axis=0)

            @pl.when(i == 0)
            def _():
                dw_ref[...] = jnp.zeros_like(dw_ref)

            for j in range(4):
                dw_ref[j:j + 1, :] += jnp.sum(dcur * xx[5 + j:5 + j + ts], axis=0, keepdims=True)

        cur, prev, wsp = _conv_specs(ts, cw, col0)
        dcur = pl.BlockSpec((ts, cw), lambda j, i: (i, j))
        dnext = pl.BlockSpec((8, cw), lambda j, i: (jnp.minimum((i + 1) * (ts // 8), s // 8 - 1), j))
        pcol = O_QKV // cw + col0
        return pl.pallas_call(body, grid=(ncols // cw, nt), in_specs=[cur, prev, wsp, dcur, dnext, ANY],
                              out_specs=[pl.BlockSpec((ts, cw), lambda j, i: (i, pcol + j)), pl.BlockSpec((4, cw), lambda j, i: (0, j))],
                              out_shape=[_sds(dproj.shape, BF16), _sds((4, ncols))], input_output_aliases={5: 0}, name=name,
                              compiler_params=_params(("parallel", "arbitrary")))(proj, proj, conv_w, dc, dc, dproj)

    dproj, dw_qk = conv_bwd(dc_qk, dproj, 0, 2 * H * DK, "conv_bwd_qk")
    dproj, dw_v = conv_bwd(dc_v, dproj, 2 * H * DK // cw, H * DV, "conv_bwd_v")
    return dproj, jnp.concatenate([dw_qk, dw_v], axis=1)


def _bg(b, a, alog, dtb):
    n = b.shape[0]
    g = -jnp.exp(alog) * jax.nn.softplus(a + dtb)
    row = lax.broadcasted_iota(jnp.int32, (n, n), 0)
    col = lax.broadcasted_iota(jnp.int32, (n, n), 1)
    shift = C.bit_length() - 1
    same = (row >> shift) == (col >> shift)
    return _sigmoid(b), _hdot((same & (row >= col)).astype(F32), g, NN), _hdot(same.astype(F32), g, NN)


def _bg_fwd(proj, alog, dtb, ts):
    s = proj.shape[0]

    def body(ba_ref, alog_ref, dtb_ref, beta_ref, gc_ref, gl_ref):
        beta_ref[...], gc_ref[...], gl_ref[...] = _bg(ba_ref[:, 0:H], ba_ref[:, H:2 * H], alog_ref[...], dtb_ref[...])

    small = pl.BlockSpec((1, H), lambda i: (0, 0))
    out = pl.BlockSpec((ts, H), lambda i: (i, 0))
    return pl.pallas_call(body, grid=(s // ts,), in_specs=[pl.BlockSpec((ts, 256), lambda i: (i, O_BA // 256)), small, small],
                          out_specs=[out] * 3, out_shape=[_sds((s, H))] * 3, name="gdn_bg_prep",
                          compiler_params=_params(("parallel",)))(proj, alog, dtb)


def _bg_bwd(proj, alog, dtb, dbeta_h, dgc_h, dgl_h, dproj, ts):
    s = proj.shape[0]

    def body(ba_ref, alog_ref, dtb_ref, dbeta_ref, dgc_ref, dgl_ref, _, dba_ref, dalog_ref, ddtb_ref):
        _, vjp = jax.vjp(_bg, ba_ref[:, 0:H], ba_ref[:, H:2 * H], alog_ref[...], dtb_ref[...])
        db, da, dalog, ddtb = vjp((jnp.sum(dbeta_ref[...], axis=0), jnp.sum(dgc_ref[...], axis=0), jnp.sum(dgl_ref[...], axis=0)))
        dba_ref[...] = jnp.zeros_like(dba_ref)
        dba_ref[:, 0:H] = db.astype(BF16)
        dba_ref[:, H:2 * H] = da.astype(BF16)

        @pl.when(pl.program_id(0) == 0)
        def _():
            dalog_ref[...] = jnp.zeros_like(dalog_ref)
            ddtb_ref[...] = jnp.zeros_like(ddtb_ref)

        dalog_ref[...] += dalog
        ddtb_ref[...] += ddtb

    small = pl.BlockSpec((1, H), lambda i: (0, 0))
    per_head = pl.BlockSpec((H, ts, H), lambda i: (0, i, 0))
    return pl.pallas_call(body, grid=(s // ts,),
                          in_specs=[pl.BlockSpec((ts, 256), lambda i: (i, O_BA // 256)), small, small, per_head, per_head, per_head, ANY],
                          out_specs=[pl.BlockSpec((ts, 256), lambda i: (i, O_BA // 256)), small, small],
                          out_shape=[_sds(dproj.shape, BF16), _sds((1, H)), _sds((1, H))], input_output_aliases={6: 0},
                          name="gdn_bg_prep_bwd", compiler_params=_params(("arbitrary",)))(proj, alog, dtb, dbeta_h, dgc_h, dgl_h, dproj)


BLK = 4 * C
NNB, NTB, TNB = ((2,), (1,)), ((2,), (2,)), ((1,), (1,))


def _bdot_b(a, b, dims):
    return lax.dot_general(a.astype(BF16), b.astype(BF16), (dims, ((0,), (0,))), preferred_element_type=F32)


@jax.custom_vjp
def _inv_unit_lower(a):
    n = a.shape[-1]
    row = lax.broadcasted_iota(jnp.int32, (n, n), 0)
    col = lax.broadcasted_iota(jnp.int32, (n, n), 1)
    x = jnp.where(row == col, 1.0, 0.0).astype(F32) - a
    p = _bdot_b(a, a, NNB)
    power = 2
    while True:
        x = x + _bdot_b(x, p, NNB)
        power *= 2
        if power >= C:
            return x
        p = _bdot_b(p, p, NNB)


def _inv_fwd(a):
    t = _inv_unit_lower(a)
    return t, t


def _inv_bwd(t, dt):
    return (-_bdot_b(_bdot_b(t, dt, TNB), t, NTB),)


_inv_unit_lower.defvjp(_inv_fwd, _inv_bwd)


def _gdn_prep(q, k, v, bfull, gcfull, glfull, hmask):
    nb, n = q.shape[0], q.shape[1]
    beta = jnp.sum(bfull * hmask, axis=-1, keepdims=True)
    gc = jnp.sum(gcfull * hmask, axis=-1, keepdims=True)
    gl = jnp.sum(glfull * hmask, axis=-1, keepdims=True)
    row = lax.broadcasted_iota(jnp.int32, (n, n), 0)
    col = lax.broadcasted_iota(jnp.int32, (n, n), 1)
    shift = C.bit_length() - 1
    same = (row >> shift) == (col >> shift)
    incl, strict = same & (row >= col), same & (row > col)
    g_i = gc * jnp.ones((1, 1, n), F32)
    decay = jnp.exp(jnp.where(incl, g_i - jnp.swapaxes(g_i, 1, 2), -jnp.inf))
    kb = k * beta
    a = jnp.where(strict, _bdot_b(kb, k, NTB) * decay, 0.0)
    tinv = _inv_unit_lower(a)
    u = _bdot_b(tinv, v * beta, NNB)
    w = _bdot_b(tinv, kb * jnp.exp(gc), NNB)
    attn = _bdot_b(q, k, NTB) * decay
    fold = ((lax.broadcasted_iota(jnp.int32, (n, C), 0) & (C - 1)) == lax.broadcasted_iota(jnp.int32, (n, C), 1)).astype(F32)
    attn_c = _bdot(attn.reshape(nb * n, n), fold, NN).reshape(nb, n, C)
    return u, w, attn_c, q * jnp.exp(gc), k * jnp.exp(gl - gc), jnp.exp(gl)


def _gdn_step(u, w, attn, qg, kd, egl, state):
    v_new = u - _bdot_b(w, state, NNB)
    o = _bdot_b(qg, state, NNB) + _bdot_b(attn, v_new, NNB)
    return o, state * egl + _bdot_b(kd, v_new, TNB)


def _heads(ref, rs, width):
    return jnp.stack([ref[rs, h * width:(h + 1) * width] for h in range(H)])


def _head_mask(h):
    return (lax.broadcasted_iota(jnp.int32, (1, H), 1) == h).astype(F32)


PREP_BLOCKS = 2


def _gdn_prep_specs(r):
    small = pl.BlockSpec((r, H), lambda h, c: (c, 0))
    return [pl.BlockSpec((r, DK), lambda h, c: (c, h)), pl.BlockSpec((r, DK), lambda h, c: (c, H + h)),
            pl.BlockSpec((r, DV), lambda h, c: (c, h)), small, small, small]


def _blocked(ref):
    x = ref[...]
    return x.reshape(PREP_BLOCKS, BLK, x.shape[-1])


def _gdn_inter_specs(r):
    col = pl.BlockSpec((r, DK), lambda h, c: (c, h))
    return [pl.BlockSpec((r, DV), lambda h, c: (c, h)), col, pl.BlockSpec((1, r, C), lambda h, c: (h, c, 0)), col, col,
            pl.BlockSpec((1, r // C, 8, 128), lambda h, c: (h, c, 0, 0))]


def _gdn_prep_fwd(qk, v, beta, gc, gl):
    s = qk.shape[0]
    r = PREP_BLOCKS * BLK

    def body(q_ref, k_ref, v_ref, b_ref, gc_ref, gl_ref, u_ref, w_ref, attn_ref, qg_ref, kd_ref, egl_ref):
        u, w, attn, qg, kd, egl = _gdn_prep(_blocked(q_ref), _blocked(k_ref), _blocked(v_ref), _blocked(b_ref), _blocked(gc_ref),
                                            _blocked(gl_ref), _head_mask(pl.program_id(0)))
        u_ref[...] = u.reshape(r, DV)
        w_ref[...] = w.reshape(r, DK).astype(BF16)
        attn_ref[0] = attn.reshape(r, C).astype(BF16)
        qg_ref[...] = qg.reshape(r, DK).astype(BF16)
        kd_ref[...] = kd.reshape(r, DK).astype(BF16)
        egl = egl.reshape(r, 1)
        for j in range(r // C):
            egl_ref[0, j] = egl[j * C:j * C + 1, :] * jnp.ones((8, 128), F32)

    out_shape = [_sds((s, H * DV)), _sds((s, H * DK), BF16), _sds((H, s, C), BF16), _sds((s, H * DK), BF16),
                 _sds((s, H * DK), BF16), _sds((H, s // C, 8, 128))]
    return pl.pallas_call(body, grid=(H, s // r), in_specs=_gdn_prep_specs(r), out_specs=_gdn_inter_specs(r), out_shape=out_shape,
                          name="gdn_prep_fwd", compiler_params=_params(("parallel", "parallel")))(qk, qk, v, beta, gc, gl)


def _gdn_prep_bwd(qk, v, beta, gc, gl, du, dw, dattn, dqg, dkd, degl):
    s = qk.shape[0]
    r = PREP_BLOCKS * BLK

    def body(q_ref, k_ref, v_ref, b_ref, gc_ref, gl_ref, du_ref, dw_ref, dattn_ref, dqg_ref, dkd_ref, degl_ref,
             dq_ref, dk_ref, dv_ref, db_ref, dgc_ref, dgl_ref):
        hmask = _head_mask(pl.program_id(0))
        _, vjp = jax.vjp(lambda q, k, v, b, gc, gl: _gdn_prep(q, k, v, b, gc, gl, hmask), _blocked(q_ref), _blocked(k_ref),
                         _blocked(v_ref), _blocked(b_ref), _blocked(gc_ref), _blocked(gl_ref))
        rowid = lax.broadcasted_iota(jnp.int32, (r, 1), 0)
        degl = jnp.zeros((r, 1), F32)
        for j in range(r // C):
            degl = jnp.where(rowid == j * C, degl_ref[0, j, 0:1, 0:1], degl)
        dq, dk, dv, db, dgc, dgl = vjp((_blocked(du_ref), _blocked(dw_ref), _blocked(dattn_ref.at[0]), _blocked(dqg_ref),
                                        _blocked(dkd_ref), degl.reshape(PREP_BLOCKS, BLK, 1)))
        dq_ref[...] = dq.reshape(r, DK)
        dk_ref[...] = dk.reshape(r, DK)
        dv_ref[...] = dv.reshape(r, DV)
        db_ref[0] = db.reshape(r, H)
        dgc_ref[0] = dgc.reshape(r, H)
        dgl_ref[0] = dgl.reshape(r, H)

    col = pl.BlockSpec((r, DK), lambda h, c: (c, h))
    piece = pl.BlockSpec((1, r, H), lambda h, c: (h, c, 0))
    dq, dk, dv, db, dgc, dgl = pl.pallas_call(
        body, grid=(H, s // r), in_specs=_gdn_prep_specs(r) + _gdn_inter_specs(r),
        out_specs=[col, col, pl.BlockSpec((r, DV), lambda h, c: (c, h)), piece, piece, piece],
        out_shape=[_sds((s, H * DK)), _sds((s, H * DK)), _sds((s, H * DV))] + [_sds((H, s, H))] * 3,
        name="gdn_prep_bwd", compiler_params=_params(("parallel", "parallel"), 24 << 20))(
            qk, qk, v, beta, gc, gl, du, dw, dattn, dqg, dkd, degl)
    return jnp.concatenate([dq, dk], axis=1), dv, db, dgc, dgl


def _gdn_scan_specs(r, order):
    wide = pl.BlockSpec((r, H * DK), lambda c: (order(c), 0))
    return [pl.BlockSpec((r, H * DV), lambda c: (order(c), 0)), wide, pl.BlockSpec((H, r, C), lambda c: (0, order(c), 0)), wide, wide,
            pl.BlockSpec((H, r // C, 8, 128), lambda c: (0, order(c), 0, 0))]


def _gdn_scan_fwd(u, w, attn, qg, kd, egl):
    s = u.shape[0]
    r = CPB * C
    nb = s // r

    def body(u_ref, w_ref, attn_ref, qg_ref, kd_ref, egl_ref, o_ref, st_ref, state_ref):
        @pl.when(pl.program_id(0) == 0)
        def _():
            state_ref[...] = jnp.zeros_like(state_ref)

        state = state_ref[...]
        for i in range(CPB):
            rs = slice(i * C, (i + 1) * C)
            st_ref[:, i] = state
            o, state = _gdn_step(_heads(u_ref, rs, DV), _heads(w_ref, rs, DK), attn_ref[:, rs, :], _heads(qg_ref, rs, DK),
                                 _heads(kd_ref, rs, DK), egl_ref[:, i, 0:1, 0:1], state)
            for h in range(H):
                o_ref[rs, h * DV:(h + 1) * DV] = o[h]
        state_ref[...] = state

    out_specs = [pl.BlockSpec((r, H * DV), lambda c: (c, 0)), pl.BlockSpec((H, CPB, DK, DV), lambda c: (0, c, 0, 0))]
    return pl.pallas_call(body, grid=(nb,), in_specs=_gdn_scan_specs(r, lambda c: c), out_specs=out_specs,
                          out_shape=[_sds((s, H * DV)), _sds((H, s // C, DK, DV))],
                          scratch_shapes=[pltpu.VMEM((H, DK, DV), F32)], name="gdn_scan_fwd",
                          compiler_params=_params(("arbitrary",), 24 << 20))(u, w, attn, qg, kd, egl)


def _gdn_scan_bwd(u, w, attn, qg, kd, egl, states, do):
    s = u.shape[0]
    r = CPB * C
    nb = s // r

    def body(u_ref, w_ref, attn_ref, qg_ref, kd_ref, egl_ref, st_ref, do_ref,
             du_ref, dw_ref, dattn_ref, dqg_ref, dkd_ref, degl_ref, dstate_ref):
        @pl.when(pl.program_id(0) == 0)
        def _():
            dstate_ref[...] = jnp.zeros_like(dstate_ref)

        dstate = dstate_ref[...]
        for i in reversed(range(CPB)):
            rs = slice(i * C, (i + 1) * C)
            _, vjp = jax.vjp(_gdn_step, _heads(u_ref, rs, DV), _heads(w_ref, rs, DK).astype(F32), attn_ref[:, rs, :].astype(F32),
                             _heads(qg_ref, rs, DK).astype(F32), _heads(kd_ref, rs, DK).astype(F32), egl_ref[:, i, 0:1, 0:1],
                             st_ref[:, i])
            du, dw, dattn, dqg, dkd, degl, dstate = vjp((_heads(do_ref, rs, DV), dstate))
            dattn_ref[:, rs, :] = dattn
            degl_ref[:, i] = degl * jnp.ones((1, 8, 128), F32)
            for h in range(H):
                du_ref[rs, h * DV:(h + 1) * DV] = du[h]
                dw_ref[rs, h * DK:(h + 1) * DK] = dw[h]
                dqg_ref[rs, h * DK:(h + 1) * DK] = dqg[h]
                dkd_ref[rs, h * DK:(h + 1) * DK] = dkd[h]
        dstate_ref[...] = dstate

    rev = lambda c: nb - 1 - c
    in_specs = _gdn_scan_specs(r, rev) + [pl.BlockSpec((H, CPB, DK, DV), lambda c: (0, rev(c), 0, 0)),
                                          pl.BlockSpec((r, H * DV), lambda c: (rev(c), 0))]
    return pl.pallas_call(
        body, grid=(nb,), in_specs=in_specs, out_specs=_gdn_scan_specs(r, rev),
        out_shape=[_sds((s, H * DV)), _sds((s, H * DK)), _sds((H, s, C)), _sds((s, H * DK)), _sds((s, H * DK)),
                   _sds((H, s // C, 8, 128))],
        scratch_shapes=[pltpu.VMEM((H, DK, DV), F32)], name="gdn_scan_bwd",
        compiler_params=_params(("arbitrary",), 40 << 20))(u, w, attn, qg, kd, egl, states, do)


def _rot(x, cs, sn):
    return x * cs + pltpu.roll(x, DK // 2, 1) * sn


def _rot_t(d, cs, sn):
    return d * cs - pltpu.roll(d, DK // 2, 1) * sn


def _ret_chunk(q, k, v, state, lg):
    n = q.shape[0]
    row = lax.broadcasted_iota(jnp.int32, (n, n), 0)
    col = lax.broadcasted_iota(jnp.int32, (n, n), 1)
    dist = (row - col).astype(F32)
    dmat = jnp.exp(jnp.where(dist >= 0, dist * lg, -jnp.inf))
    scores = _bdot(q, k, NT) * dmat
    pos = lax.broadcasted_iota(jnp.int32, (n, 1), 0).astype(F32)
    xi = jnp.exp((pos + 1.0) * lg)
    zeta = jnp.exp((n - 1.0 - pos) * lg)
    o = _bdot(scores, v, NN) + _bdot(q, state, NN) * xi
    new_state = state * jnp.exp(n * lg) + _bdot(k * zeta, v, TN)
    return o, new_state


def _ret_specs(r, order):
    q0, v0 = O_RET // DK, (O_RET + 2 * DK) // DV
    return [pl.BlockSpec((r, DK), lambda h, c: (order(c), q0 + h * (RET_W // DK))),
            pl.BlockSpec((r, DK), lambda h, c: (order(c), q0 + 1 + h * (RET_W // DK))),
            pl.BlockSpec((r, DV), lambda h, c: (order(c), v0 + h * (RET_W // DV))), pl.BlockSpec((r, DK), lambda h, c: (order(c), 0)),
            pl.BlockSpec((r, DK), lambda h, c: (order(c), 0)), pl.BlockSpec((1, H), lambda h, c: (0, 0))]


RET_C = 256


def _ret_scan_fwd(proj, cs, sn, lgtab):
    s = proj.shape[0]
    r = min(RET_C, s)
    nb = s // r

    def body(q_ref, k_ref, v_ref, cs_ref, sn_ref, lg_ref, o_ref, st_ref, state_ref):
        @pl.when(pl.program_id(1) == 0)
        def _():
            state_ref[...] = jnp.zeros_like(state_ref)

        lg = jnp.sum(lg_ref[...] * _head_mask(pl.program_id(0)), axis=1, keepdims=True)
        state = state_ref[...]
        st_ref[0, 0] = state
        q = _rot(q_ref[...], cs_ref[...], sn_ref[...])
        k = _rot(k_ref[...], cs_ref[...], sn_ref[...]) * DK ** -0.5
        o_ref[...], state_ref[...] = _ret_chunk(q, k, v_ref[...], state, lg)

    out_specs = [pl.BlockSpec((r, DV), lambda h, c: (c, h)), pl.BlockSpec((1, 1, DK, DV), lambda h, c: (h, c, 0, 0))]
    return pl.pallas_call(body, grid=(H, nb), in_specs=_ret_specs(r, lambda c: c), out_specs=out_specs,
                          out_shape=[_sds((s, H * DV)), _sds((H, nb, DK, DV))],
                          scratch_shapes=[pltpu.VMEM((DK, DV), F32)], name="ret_scan_fwd",
                          compiler_params=_params(("parallel", "arbitrary")))(proj, proj, proj, cs, sn, lgtab)


def _ret_scan_bwd(proj, cs, sn, lgtab, states, do, dproj):
    s = proj.shape[0]
    r = min(RET_C, s)
    nb = s // r

    def body(q_ref, k_ref, v_ref, cs_ref, sn_ref, lg_ref, st_ref, do_ref, _, d_ref, dstate_ref):
        @pl.when(pl.program_id(1) == 0)
        def _():
            dstate_ref[...] = jnp.zeros_like(dstate_ref)

        lg = jnp.sum(lg_ref[...] * _head_mask(pl.program_id(0)), axis=1, keepdims=True)
        cs_, sn_ = cs_ref[...], sn_ref[...]
        q = _rot(q_ref[...], cs_, sn_)
        k = _rot(k_ref[...], cs_, sn_) * DK ** -0.5
        _, vjp = jax.vjp(lambda q, k, v, st: _ret_chunk(q, k, v, st, lg), q, k, v_ref[...], st_ref[0, 0])
        dq, dk, dv, dstate_ref[...] = vjp((do_ref[...], dstate_ref[...]))
        d_ref[:, 0:DK] = _rot_t(dq, cs_, sn_).astype(BF16)
        d_ref[:, DK:2 * DK] = _rot_t(dk * DK ** -0.5, cs_, sn_).astype(BF16)
        d_ref[:, 2 * DK:RET_W] = dv.astype(BF16)

    rev = lambda c: nb - 1 - c
    in_specs = _ret_specs(r, rev) + [pl.BlockSpec((1, 1, DK, DV), lambda h, c: (h, rev(c), 0, 0)),
                                     pl.BlockSpec((r, DV), lambda h, c: (rev(c), h)), ANY]
    return pl.pallas_call(
        body, grid=(H, nb), in_specs=in_specs, out_specs=pl.BlockSpec((r, RET_W), lambda h, c: (rev(c), O_RET // RET_W + h)),
        out_shape=_sds(dproj.shape, BF16), input_output_aliases={8: 0},
        scratch_shapes=[pltpu.VMEM((DK, DV), F32)], name="ret_scan_bwd",
        compiler_params=_params(("parallel", "arbitrary")))(proj, proj, proj, cs, sn, lgtab, states, do, dproj)


def _merge(oa, z, ob, rg, ga, gb, wa, wb):
    ya = oa * lax.rsqrt(jnp.mean(oa * oa, axis=-1, keepdims=True) + EPS) * wa * _silu(z)
    mu = jnp.mean(ob, axis=-1, keepdims=True)
    var = jnp.mean(jnp.square(ob - mu), axis=-1, keepdims=True)
    yb = (ob - mu) * lax.rsqrt(var + EPS) * wb * _silu(rg)
    return _sigmoid(ga) * ya + _sigmoid(gb) * yb


def _merge_specs(ts):
    own = pl.BlockSpec((ts, DV), lambda h, i: (i, h))
    grp = lambda k: pl.BlockSpec((ts, DV), lambda h, i: (i, O_MERGE // DV + 4 * h + k))
    return [own, grp(0), own, grp(1), grp(2), grp(3),
            pl.BlockSpec((1, DV), lambda h, i: (0, 0)), pl.BlockSpec((1, DV), lambda h, i: (0, h))]


def _merge_fwd(oa, ob, proj, wa, wb, ts):
    s = oa.shape[0]

    def body(oa_ref, z_ref, ob_ref, rg_ref, ga_ref, gb_ref, wa_ref, wb_ref, o_ref, ot_ref):
        y = _merge(oa_ref[...], z_ref[...], ob_ref[...], rg_ref[...], ga_ref[...], gb_ref[...],
                   wa_ref[...], wb_ref[...]).astype(BF16)
        o_ref[...] = y
        ot_ref[...] = y.T

    return pl.pallas_call(body, grid=(H, s // ts), in_specs=_merge_specs(ts),
                          out_specs=[pl.BlockSpec((ts, DV), lambda h, i: (i, h)), pl.BlockSpec((DV, ts), lambda h, i: (h, i))],
                          out_shape=[_sds((s, H * DV), BF16), _sds((H * DV, s), BF16)], name="merge_fwd",
                          compiler_params=_params(("parallel", "parallel")))(oa, proj, ob, proj, proj, proj, wa, wb)


def _merge_bwd(oa, ob, proj, wa, wb, dmixed, ts):
    s = oa.shape[0]

    def body(oa_ref, z_ref, ob_ref, rg_ref, ga_ref, gb_ref, wa_ref, wb_ref, dm_ref,
             doa_ref, dob_ref, dgrp_ref, dwa_ref, dwb_ref):
        _, vjp = jax.vjp(_merge, oa_ref[...], z_ref[...], ob_ref[...], rg_ref[...], ga_ref[...], gb_ref[...],
                         wa_ref[...], wb_ref[...])
        doa, dz, dob, drg, dga, dgb, dwa, dwb = vjp(dm_ref[...].astype(F32))
        doa_ref[...] = doa
        dob_ref[...] = dob
        for k, d in enumerate((dz, drg, dga, dgb)):
            dgrp_ref[:, k * DV:(k + 1) * DV] = d.astype(BF16)
        first_tile = pl.program_id(1) == 0

        @pl.when(first_tile & (pl.program_id(0) == 0))
        def _():
            dwa_ref[...] = jnp.zeros_like(dwa_ref)

        @pl.when(first_tile)
        def _():
            dwb_ref[...] = jnp.zeros_like(dwb_ref)

        dwa_ref[...] += dwa
        dwb_ref[...] += dwb

    blk = pl.BlockSpec((ts, DV), lambda h, i: (i, h))
    out_specs = [blk, blk, pl.BlockSpec((ts, MERGE_W), lambda h, i: (i, O_MERGE // MERGE_W + h)),
                 pl.BlockSpec((1, DV), lambda h, i: (0, 0)), pl.BlockSpec((1, DV), lambda h, i: (0, h))]
    out_shape = [_sds((s, H * DV)), _sds((s, H * DV)), _sds((s, P_IN), BF16), _sds((1, DV)), _sds((1, H * DV))]
    return pl.pallas_call(body, grid=(H, s // ts), in_specs=_merge_specs(ts) + [blk], out_specs=out_specs, out_shape=out_shape,
                          name="merge_bwd", compiler_params=_params(("arbitrary", "arbitrary"), 40 * ts * DV * 4))(
                              oa, proj, ob, proj, proj, proj, wa, wb, dmixed)


def _act(hg, hu):
    return _silu(hg) * hu


def _ffn_gate_up(hn, w_gate, w_up, tm, tn):
    s, f = hn.shape[0], w_gate.shape[1]
    tm, tn = min(tm, s), min(tn, f)
    assert s % tm == 0 and f % tn == 0 and tm % 256 == 0
    sub = tm // 2

    def body(a_ref, wg_ref, wu_ref, hg_ref, hu_ref, act_ref, actt_ref):
        for r0 in range(0, tm, sub):
            rs = slice(r0, r0 + sub)
            hg = _bdot(a_ref[rs, :], wg_ref[...], NN)
            hu = _bdot(a_ref[rs, :], wu_ref[...], NN)
            y = _act(hg, hu).astype(BF16)
            hg_ref[rs, :] = hg.astype(BF16)
            hu_ref[rs, :] = hu.astype(BF16)
            act_ref[rs, :] = y
            actt_ref[:, rs] = y.T

    wsp = pl.BlockSpec((D, tn), lambda i, j: (0, j))
    blk = pl.BlockSpec((tm, tn), lambda i, j: (i, j))
    est = 2 * (tm * D * 2 + 2 * D * tn * 2 + 4 * tm * tn * 2) + 4 * sub * tn * 4
    return pl.pallas_call(body, grid=(s // tm, f // tn), in_specs=[pl.BlockSpec((tm, D), lambda i, j: (i, 0)), wsp, wsp],
                          out_specs=[blk, blk, blk, pl.BlockSpec((tn, tm), lambda i, j: (j, i))],
                          out_shape=[_sds((s, f), BF16)] * 3 + [_sds((f, s), BF16)], name="ffn_gate_up",
                          compiler_params=_params(("parallel", "parallel"), est))(hn, w_gate, w_up)


def _ffn_down_dx(dh2, w_down, hg, hu, tm, tn):
    s, f = hg.shape
    tm, tn = min(tm, s), min(tn, f)
    assert s % tm == 0 and f % tn == 0 and tm % 256 == 0
    sub = tm // 2

    def body(d_ref, w_ref, hg_ref, hu_ref, dhg_ref, dhu_ref):
        for r0 in range(0, tm, sub):
            rs = slice(r0, r0 + sub)
            dact = _bdot(d_ref[rs, :], w_ref[...], NT)
            _, vjp = jax.vjp(_act, hg_ref[rs, :].astype(F32), hu_ref[rs, :].astype(F32))
            dhg, dhu = vjp(dact)
            dhg_ref[rs, :] = dhg.astype(BF16)
            dhu_ref[rs, :] = dhu.astype(BF16)

    blk = pl.BlockSpec((tm, tn), lambda i, j: (i, j))
    est = 2 * (tm * D * 4 + tn * D * 2 + 4 * tm * tn * 2) + 6 * sub * tn * 4
    return pl.pallas_call(body, grid=(s // tm, f // tn),
                          in_specs=[pl.BlockSpec((tm, D), lambda i, j: (i, 0)), pl.BlockSpec((tn, D), lambda i, j: (j, 0)), blk, blk],
                          out_specs=[blk, blk], out_shape=[_sds((s, f), BF16)] * 2, name="ffn_down_dx",
                          compiler_params=_params(("parallel", "parallel"), est))(dh2, w_down, hg, hu)


def _loss_rows(h2, wf, tgt):
    err = _rms(h2, wf) - tgt
    return 0.5 * jnp.sum(jnp.mean(err * err, axis=-1, keepdims=True), keepdims=True)


def _loss_fwd_bwd(h2, wf, tgt, ts):
    s = h2.shape[0]

    def body(h_ref, w_ref, t_ref, loss_ref, dh_ref, dw_ref):
        loss, vjp = jax.vjp(_loss_rows, h_ref[...], w_ref[...], t_ref[...])
        dh, dw, _ = vjp(jnp.ones((1, 1), F32))
        dh_ref[...] = dh

        @pl.when(pl.program_id(0) == 0)
        def _():
            loss_ref[...] = jnp.zeros_like(loss_ref)
            dw_ref[...] = jnp.zeros_like(dw_ref)

        loss_ref[...] += loss
        dw_ref[...] += dw

    row = pl.BlockSpec((ts, D), lambda i: (i, 0))
    vec = pl.BlockSpec((1, D), lambda i: (0, 0))
    tile = pl.BlockSpec((8, 128), lambda i: (0, 0))
    return pl.pallas_call(body, grid=(s // ts,), in_specs=[row, vec, row], out_specs=[tile, row, vec],
                          out_shape=[_sds((8, 128)), _sds((s, D)), _sds((1, D))], name="final_norm_loss",
                          compiler_params=_params(("arbitrary",), 12 * ts * D * 4))(h2, wf, tgt)


def _rope_tables(s):
    inv = ROPE_BASE ** (-jnp.arange(0, DK, 2, dtype=F32) / DK)
    ang = jnp.arange(s, dtype=F32)[:, None] * inv[None, :]
    cos, sin = jnp.cos(ang), jnp.sin(ang)
    return jnp.concatenate([cos, cos], axis=1), jnp.concatenate([-sin, sin], axis=1)


def _local_step(x, tgt, w_in, w_out, w_gate, w_up, w_down, norm1_w, conv_w, a_log, dt_bias, gdn_norm_w, ret_norm_w, norm2_w, norm_f_w,
                dist=None):
    s = x.shape[0]
    ts = min(512, s)
    cs, sn = _rope_tables(s)
    lgtab = jnp.log1p(-jnp.exp2(-5.0 - jnp.arange(H, dtype=F32))).reshape(1, H)

    u, u_t = _rms_fwd(x, norm1_w, ts, "norm1_fwd")
    if dist is None:
        proj = _matmul(u, w_in, tm=1024, tn=1280, tk=D, name="in_proj")
    else:
        proj, gathered = _matmul(u, w_in, tm=1024, tn=1280, tk=D, side=_gather_side(dist["shards"]), name="in_proj")
        w_out, w_gate, w_up, w_down = (gathered[0].reshape(D, D), _from_slots_cols(gathered[1]), _from_slots_cols(gathered[2]),
                                       gathered[3].reshape(-1, D))
    qk, va, c_qk, c_v = _gdn_qkv_fwd(proj, conv_w, ts)
    beta, gc, gl = _bg_fwd(proj, a_log, dt_bias, ts)
    inter = _gdn_prep_fwd(qk, va, beta, gc, gl)
    oa, st_a = _gdn_scan_fwd(*inter)
    ob, st_b = _ret_scan_fwd(proj, cs, sn, lgtab)
    mixed, mixed_t = _merge_fwd(oa, ob, proj, gdn_norm_w, ret_norm_w, ts)
    h1 = _matmul(mixed, w_out, tm=1024, tn=1024, tk=D, res=x, name="out_proj")
    hn, hn_t = _rms_fwd(h1, norm2_w, ts, "norm2_fwd")
    hg, hu, act, act_t = _ffn_gate_up(hn, w_gate, w_up, 512, 1408)
    h2 = _matmul(act, w_down, tm=512, tn=1024, tk=5632, res=h1, name="ffn_down")
    loss, dh2, d_norm_f = _loss_fwd_bwd(h2, norm_f_w, tgt, ts)

    dhg, dhu = _ffn_down_dx(dh2, w_down, hg, hu, 512, 1408)
    g_down = _matmul(act_t, dh2, tm=1408, tn=1024, tk=D, out_dtype=BF16, name="ffn_down_dw")
    g_gate = _matmul(hn_t, dhg, tm=512, tn=512, tk=8192, out_dtype=BF16, name="ffn_gate_dw")
    g_up = _matmul(hn_t, dhu, tm=512, tn=512, tk=8192, out_dtype=BF16, name="ffn_up_dw")
    dhn = _matmul(dhg, w_gate, tb=True, tm=512, tn=1024, tk=5632, name="ffn_gate_dx")
    dhn = _matmul(dhu, w_up, tb=True, tm=512, tn=1024, tk=5632, res=dhn, name="ffn_up_dx")
    dh1, d_norm2 = _rms_bwd(h1, norm2_w, dhn, dh2, ts, "norm2_bwd")

    g_out = _matmul(mixed_t, dh1, tm=1024, tn=1024, tk=D, out_dtype=BF16, name="out_proj_dw")
    early = ["w_out", "w_gate", "w_up", "w_down"]
    if dist is None:
        dmixed = _matmul(dh1, w_out, tb=True, tm=1024, tn=1024, tk=D, out_dtype=BF16, name="out_proj_dx")
    else:
        slots = dict(w_out=g_out.reshape(NDEV, D // NDEV, D), w_gate=_to_slots_cols(g_gate), w_up=_to_slots_cols(g_up),
                     w_down=g_down.reshape(NDEV, -1, D))
        dmixed, from_sibling = _matmul(dh1, w_out, tb=True, tm=1024, tn=1024, tk=D, out_dtype=BF16,
                                       side=_sibling_side([slots[k] for k in early]), name="out_proj_dx")
        parts = [_add_sibling(slots[k], r, dist["core"], 128, "grads_add_" + k) for k, r in zip(early, from_sibling)]
    doa, dob, dproj, d_gdn_norm, d_ret_norm = _merge_bwd(oa, ob, proj, gdn_norm_w, ret_norm_w, dmixed, ts)

    dproj = _ret_scan_bwd(proj, cs, sn, lgtab, st_b, dob, dproj)
    d_inter = _gdn_scan_bwd(*inter, st_a, doa)
    dqk, dva, dbeta_h, dgc_h, dgl_h = _gdn_prep_bwd(qk, va, beta, gc, gl, *d_inter)
    dproj, d_conv = _gdn_qkv_bwd(proj, conv_w, c_qk, c_v, dqk, dva, dproj, ts)
    dproj, d_a_log, d_dt_bias = _bg_bwd(proj, a_log, dt_bias, dbeta_h, dgc_h, dgl_h, dproj, ts)

    if dist is None:
        g_in = _matmul(u_t, dproj, tm=512, tn=640, tk=8192, out_dtype=BF16, name="in_proj_dw")
        du = _matmul(dproj, w_in, tb=True, tm=1024, tn=1024, tk=1664, name="in_proj_dx")
        big = dict(w_in=g_in, w_out=g_out, w_gate=g_gate, w_up=g_up, w_down=g_down)
    else:
        g_in, from_chips = _matmul(u_t, dproj, tm=512, tn=640, tk=8192, out_dtype=BF16, side=_chips_side(parts), name="in_proj_dw")
        du, (from_all,) = _matmul(dproj, w_in, tb=True, tm=1024, tn=1024, tk=1664,
                                  side=_all_to_all_side(_windows_from_layout(g_in)), name="in_proj_dx")
        big = dict(w_in=from_all, **{k: (p, r) for k, p, r in zip(early, parts, from_chips)})
    dx, d_norm1 = _rms_bwd(x, norm1_w, du, dh1, ts, "norm1_bwd")

    small = dict(norm1_w=d_norm1, conv_w=d_conv, a_log=d_a_log, dt_bias=d_dt_bias, gdn_norm_w=d_gdn_norm,
                 ret_norm_w=d_ret_norm, norm2_w=d_norm2, norm_f_w=d_norm_f)
    return loss, dx, big, small


def _coords():
    return lax.axis_index("x"), lax.axis_index("y"), lax.axis_index("c")


def _gather_side(shards):
    n = len(shards)

    def plan(ins, outs, send_sems, recv_sems, local_sems):
        x, y, c = _coords()
        me, sibling = (x, y, c), (x, y, 1 - c)
        chips = [(1 - x, y), (x, 1 - y), (1 - x, 1 - y)]

        def copy(a, k, block, to, src=None):
            px, py, pc = block
            dst = outs[a].at[4 * px + 2 * py + pc]
            return pltpu.make_async_remote_copy(src_ref=dst if src is None else src, dst_ref=dst, send_sem=send_sems.at[a, k],
                                                recv_sem=recv_sems.at[a, k], device_id=to, device_id_type=MESH)

        mine = [pltpu.make_async_copy(ins[a], outs[a].at[4 * x + 2 * y + c], local_sems.at[a]) for a in range(n)]
        first = []
        for a in range(n):
            first.append(copy(a, 0, me, sibling, src=ins[a]))
            first += [copy(a, 1 + j, me, (*chip, c), src=ins[a]) for j, chip in enumerate(chips)]
        return c, me, sibling, chips, copy, mine, first

    def start(ins, outs, *sems):
        *_, mine, first = plan(ins, outs, *sems)
        for cp in mine + first:
            cp.start()

    def finish(ins, outs, *sems):
        c, me, sibling, chips, copy, mine, first = plan(ins, outs, *sems)
        passed = []
        for j, chip in enumerate(chips):
            for a in range(n):
                copy(a, 1 + j, (*chip, c), me).wait_recv()
                fwd = copy(a, 4 + j, (*chip, c), sibling)
                fwd.start()
                passed.append(fwd)
        for a in range(n):
            copy(a, 0, sibling, me).wait_recv()
            for j, chip in enumerate(chips):
                copy(a, 4 + j, (*chip, 1 - c), me).wait_recv()
        for cp in first + passed:
            cp.wait_send()
        for cp in mine:
            cp.wait()

    return _Side(shards, [_sds((NDEV,) + a.shape, a.dtype) for a in shards], [(n, 7), (n, 7), (n,)], start, finish)


def _exchange_side(ins, n_out, copies_of):
    def start(in_refs, out_refs, *sems):
        for cp in copies_of(in_refs, out_refs, *sems):
            cp.start()

    def finish(in_refs, out_refs, *sems):
        for cp in copies_of(in_refs, out_refs, *sems):
            cp.wait()

    n = len(ins)
    return _Side(ins, [_sds((n_out,) + a.shape[1:], a.dtype) for a in ins], [(n, n_out), (n, n_out)], start, finish)


def _sibling_side(slots):
    def copies_of(ins, outs, send_sems, recv_sems):
        x, y, c = _coords()
        return [pltpu.make_async_remote_copy(
            src_ref=ins[a].at[2 * j + (1 - c)], dst_ref=outs[a].at[j], send_sem=send_sems.at[a, j], recv_sem=recv_sems.at[a, j],
            device_id=(x, y, 1 - c), device_id_type=MESH) for a in range(len(slots)) for j in range(4)]

    return _exchange_side(slots, 4, copies_of)


def _chips_side(parts):
    def copies_of(ins, outs, send_sems, recv_sems):
        x, y, c = _coords()
        chips = [(1 - x, y), (x, 1 - y), (1 - x, 1 - y)]
        return [pltpu.make_async_remote_copy(
            src_ref=ins[a].at[2 * px + py], dst_ref=outs[a].at[k], send_sem=send_sems.at[a, k], recv_sem=recv_sems.at[a, k],
            device_id=(px, py, c), device_id_type=MESH) for a in range(len(parts)) for k, (px, py) in enumerate(chips)]

    return _exchange_side(parts, 3, copies_of)


def _all_to_all_side(slots):
    def plan(ins, outs, send_sems, recv_sems, local_sems):
        x, y, c = _coords()
        mine = 4 * x + 2 * y + c
        own = pltpu.make_async_copy(ins[0].at[mine], outs[0].at[mine], local_sems.at[0])
        remote = []
        for r in range(1, NDEV):
            peer = (x ^ (r >> 2), y ^ ((r >> 1) & 1), c ^ (r & 1))
            remote.append(pltpu.make_async_remote_copy(
                src_ref=ins[0].at[mine ^ r], dst_ref=outs[0].at[mine], send_sem=send_sems.at[r - 1], recv_sem=recv_sems.at[r - 1],
                device_id=peer, device_id_type=MESH))
        return own, remote

    def start(ins, outs, *sems):
        own, remote = plan(ins, outs, *sems)
        for cp in [own] + remote:
            cp.start()

    def finish(ins, outs, *sems):
        own, remote = plan(ins, outs, *sems)
        for cp in remote:
            cp.wait()
        own.wait()

    return _Side([slots], [_sds(slots.shape, slots.dtype)], [(NDEV - 1,), (NDEV - 1,), (1,)], start, finish)


def _allreduce_small(pack, name):
    rows, cols = pack.shape

    def body(in_ref, out_ref, buf_ref, send_sems, recv_sems):
        x, y, c = _coords()
        mine = 4 * x + 2 * y + c
        buf_ref[mine] = in_ref[...]
        copies = []
        for r in range(1, NDEV):
            peer = (x ^ (r >> 2), y ^ ((r >> 1) & 1), c ^ (r & 1))
            copies.append(pltpu.make_async_remote_copy(
                src_ref=in_ref, dst_ref=buf_ref.at[mine], send_sem=send_sems.at[r - 1], recv_sem=recv_sems.at[r - 1],
                device_id=peer, device_id_type=MESH))
        for cp in copies:
            cp.start()
        for r in range(1, NDEV):
            pltpu.make_async_remote_copy(
                src_ref=in_ref, dst_ref=buf_ref.at[mine ^ r], send_sem=send_sems.at[r - 1], recv_sem=recv_sems.at[r - 1],
                device_id=(x, y, c), device_id_type=MESH).wait_recv()
        for cp in copies:
            cp.wait_send()
        acc = buf_ref[0]
        for d in range(1, NDEV):
            acc = acc + buf_ref[d]
        out_ref[...] = acc

    return pl.pallas_call(
        body, in_specs=[VMEM_FULL], out_specs=VMEM_FULL, out_shape=_sds((rows, cols)),
        scratch_shapes=[pltpu.VMEM((NDEV, rows, cols), F32), pltpu.SemaphoreType.DMA((NDEV - 1,)), pltpu.SemaphoreType.DMA((NDEV - 1,))],
        name=name)(pack)


def _add_sibling(slots, recv, core, tr, name):
    _, rows, cols = slots.shape
    tr = _row_tile(rows, tr)

    def body(c_ref, a_ref, b_ref, o_ref):
        o_ref[...] = (a_ref[...].astype(F32) + b_ref[...].astype(F32)).astype(BF16)

    gs = pltpu.PrefetchScalarGridSpec(
        num_scalar_prefetch=1, grid=(4, rows // tr),
        in_specs=[pl.BlockSpec((None, tr, cols), lambda j, i, cr: (2 * j + cr[0], i, 0)),
                  pl.BlockSpec((None, tr, cols), lambda j, i, cr: (j, i, 0))],
        out_specs=pl.BlockSpec((None, tr, cols), lambda j, i, cr: (j, i, 0)))
    return pl.pallas_call(body, grid_spec=gs, out_shape=_sds((4, rows, cols), BF16), name=name,
                          compiler_params=_params(("parallel", "parallel"), 6 * tr * cols * 4))(core, slots, recv)


def _adam_math(w, g, m, v):
    m2 = B1 * m + (1.0 - B1) * g
    v2 = B2 * v + (1.0 - B2) * jnp.square(g)
    m_hat = m2 / (1.0 - B1 ** STEP)
    v_hat = v2 / (1.0 - B2 ** STEP)
    return -LR * (m_hat / (jnp.sqrt(v_hat) + EPS_ADAM) + WD * w), m2, v2


def _adamw_reduced(part, recv, chip, w, m, v, tr, name):
    rows, cols = w.shape
    tr = _row_tile(rows, tr)

    def body(j_ref, p_ref, r0_ref, r1_ref, r2_ref, w_ref, m_ref, v_ref, g_ref, d_ref, nm_ref, nv_ref):
        g = p_ref[...].astype(F32) + r0_ref[...].astype(F32) + r1_ref[...].astype(F32) + r2_ref[...].astype(F32)
        d, m2, v2 = _adam_math(w_ref[...], g, m_ref[...], v_ref[...])
        g_ref[...] = g
        d_ref[...] = d
        nm_ref[...] = m2
        nv_ref[...] = v2

    flat = pl.BlockSpec((tr, cols), lambda i, jr: (i, 0))
    gs = pltpu.PrefetchScalarGridSpec(
        num_scalar_prefetch=1, grid=(rows // tr,),
        in_specs=[pl.BlockSpec((None, tr, cols), lambda i, jr: (jr[0], i, 0))]
        + [pl.BlockSpec((None, tr, cols), functools.partial(lambda i, jr, k: (k, i, 0), k=k)) for k in range(3)] + [flat] * 3,
        out_specs=[flat] * 4)
    return pl.pallas_call(body, grid_spec=gs, out_shape=[_sds((rows, cols))] * 4, name=name,
                          compiler_params=_params(("parallel",), 22 * tr * cols * 4))(chip, part, recv, recv, recv, w, m, v)


def _sum_slots(recv, tr, name):
    _, rows, cols = recv.shape
    tr = _row_tile(rows, tr)

    def body(*refs):
        acc = refs[0][...].astype(F32)
        for p_ref in refs[1:NDEV]:
            acc = acc + p_ref[...].astype(F32)
        refs[NDEV][...] = acc

    slot = [pl.BlockSpec((None, tr, cols), functools.partial(lambda i, k: (k, i, 0), k=k)) for k in range(NDEV)]
    return pl.pallas_call(body, grid=(rows // tr,), in_specs=slot, out_specs=pl.BlockSpec((tr, cols), lambda i: (i, 0)),
                          out_shape=_sds((rows, cols)), name=name, compiler_params=_params(("parallel",)))(*([recv] * NDEV))


def _adamw_rows(w, g, m, v, tr, name):
    rows, cols = w.shape
    tr = _row_tile(rows, tr)

    def body(w_ref, g_ref, m_ref, v_ref, d_ref, nm_ref, nv_ref):
        d_ref[...], nm_ref[...], nv_ref[...] = _adam_math(w_ref[...], g_ref[...], m_ref[...], v_ref[...])

    flat = pl.BlockSpec((tr, cols), lambda i: (i, 0))
    return pl.pallas_call(body, grid=(rows // tr,), in_specs=[flat] * 4, out_specs=[flat] * 3, out_shape=[_sds((rows, cols))] * 3,
                          name=name, compiler_params=_params(("parallel",)))(w, g, m, v)


def _adamw_plain(w, g, m, v, name):
    def body(w_ref, g_ref, m_ref, v_ref, d_ref, nm_ref, nv_ref):
        d, m2, v2 = _adam_math(w_ref[...], g_ref[...], m_ref[...], v_ref[...])
        d_ref[...] = d
        nm_ref[...] = m2
        nv_ref[...] = v2

    return pl.pallas_call(body, out_shape=[_sds(w.shape)] * 3, name=name)(w, g, m, v)


def _pack_small(norm1_w, conv_w, a_log, dt_bias, gdn_norm_w, ret_norm_w, norm2_w, norm_f_w):
    misc = jnp.concatenate([gdn_norm_w.reshape(1, DV), a_log.reshape(1, H), dt_bias.reshape(1, H),
                            jnp.zeros((1, D - DV - 2 * H), F32)], axis=1)
    return jnp.concatenate([norm1_w.reshape(1, D), ret_norm_w.reshape(1, D), norm2_w.reshape(1, D), norm_f_w.reshape(1, D),
                            conv_w.reshape(8, D), misc, jnp.zeros((3, D), F32)], axis=0)


def _unpack_small(pack):
    return dict(norm1_w=pack[0:1], ret_norm_w=pack[1:2], norm2_w=pack[2:3], norm_f_w=pack[3], conv_w=pack[4:12].reshape(4, 2 * D),
                gdn_norm_w=pack[12:13, 0:DV], a_log=pack[12:13, DV:DV + H], dt_bias=pack[12:13, DV + H:DV + 2 * H])


IN_SPLITS = (4096, 2048, 8, 8, 1024, 1024, 2048, 2048, 2048, 2048)


BA_END = sum(IN_SPLITS[:4])
LANES = 128


def _padded_order_blocks():
    z0, ba0, rq0, rk0, rv0, rg0, ga0, gb0 = 4096, 6144, 6400, 7424, 8448, 10496, 12544, 14592
    cols = []
    for h in range(H):
        for base in (z0, rg0, ga0, gb0):
            cols += [base + DV * h, base + DV * h + LANES]
    cols += list(range(0, z0, LANES))
    for h in range(H):
        cols += [rq0 + DK * h, rk0 + DK * h, rv0 + DV * h, rv0 + DV * h + LANES]
    cols += [ba0, ba0 + LANES]
    blocks = np.asarray(cols, np.int32) // LANES
    assert sorted(blocks.tolist()) == list(range(P_IN // LANES))
    return blocks


def _permute_blocks(x, blocks, name):
    rows, cols = x.shape

    def body(p_ref, x_ref, o_ref):
        o_ref[...] = x_ref[...]

    gs = pltpu.PrefetchScalarGridSpec(num_scalar_prefetch=1, grid=(cols // LANES,),
                                      in_specs=[pl.BlockSpec((rows, LANES), lambda j, p: (0, p[j]))],
                                      out_specs=pl.BlockSpec((rows, LANES), lambda j, p: (0, j)))
    return pl.pallas_call(body, grid_spec=gs, out_shape=_sds((rows, cols), x.dtype), name=name,
                          compiler_params=_params(("parallel",)))(jnp.asarray(blocks), x)


def _regroup_w_in(w):
    padded = jnp.concatenate([w[:, :BA_END], jnp.zeros((w.shape[0], P_IN - N_IN), w.dtype), w[:, BA_END:]], axis=1)
    return _permute_blocks(padded, _padded_order_blocks(), "w_in_to_layout")


def _ungroup_w_in(g):
    padded = _permute_blocks(g, np.argsort(_padded_order_blocks()).astype(np.int32), "w_in_grad_from_layout")
    return jnp.concatenate([padded[:, :BA_END], padded[:, BA_END + P_IN - N_IN:]], axis=1)


SHARD_W = N_IN // NDEV
GAP = P_IN - N_IN
WIN = 2304


def _padded_col(c):
    return c + (GAP if c >= BA_END else 0)


WIN_START = [min(_padded_col(SHARD_W * d) // LANES * LANES, P_IN - WIN) for d in range(NDEV)]
WIN_OFF = [_padded_col(SHARD_W * d) - WIN_START[d] for d in range(NDEV)]
STRADDLER = BA_END // SHARD_W
STRADDLE_AT = BA_END - STRADDLER * SHARD_W
assert all(WIN_OFF[d] + SHARD_W + (GAP if d == STRADDLER else 0) <= WIN for d in range(NDEV))


def _win_off(me):
    off = jnp.int32(0)
    for d in range(NDEV):
        off = jnp.where(me == d, jnp.int32(WIN_OFF[d]), off)
    return off


def _window_of_shard(shard, me):
    rows = shard.shape[0]
    zeros = lambda n: jnp.zeros((rows, n), shard.dtype)
    plain = lax.dynamic_update_slice(zeros(WIN), shard, (0, _win_off(me)))
    o = WIN_OFF[STRADDLER]
    split = jnp.concatenate([zeros(o), shard[:, :STRADDLE_AT], zeros(GAP), shard[:, STRADDLE_AT:], zeros(WIN - o - GAP - SHARD_W)], axis=1)
    return jnp.where(me == STRADDLER, split, plain)


def _shard_of_window(win, me):
    plain = lax.dynamic_slice(win, (0, _win_off(me)), (win.shape[0], SHARD_W))
    o = WIN_OFF[STRADDLER]
    split = jnp.concatenate([win[:, o:o + STRADDLE_AT], win[:, o + STRADDLE_AT + GAP:o + GAP + SHARD_W]], axis=1)
    return jnp.where(me == STRADDLER, split, plain)


def _layout_from_windows(wins):
    _, rows, _ = wins.shape
    data = []
    for d in range(NDEV):
        lo = _padded_col(SHARD_W * d)
        data.append([(lo, lo + STRADDLE_AT), (lo + STRADDLE_AT + GAP, lo + GAP + SHARD_W)] if d == STRADDLER else [(lo, lo + SHARD_W)])
    zero_block = (0, WIN // LANES - 1)
    table = []
    for p in _padded_order_blocks():
        src = [(d, int(p) - WIN_START[d] // LANES) for d in range(NDEV)
               if any(lo < (p + 1) * LANES and hi > p * LANES for lo, hi in data[d])]
        assert len(src) <= 2 and all(0 <= b < WIN // LANES for _, b in src)
        src += [zero_block] * (2 - len(src))
        table.append([src[0][0], src[0][1], src[1][0], src[1][1]])
    table = np.asarray(table, np.int32).T.copy()

    def body(t_ref, a_ref, b_ref, o_ref):
        o_ref[...] = a_ref[...] + b_ref[...]

    gs = pltpu.PrefetchScalarGridSpec(
        num_scalar_prefetch=1, grid=(P_IN // LANES,),
        in_specs=[pl.BlockSpec((None, rows, LANES), lambda j, t: (t[0, j], 0, t[1, j])),
                  pl.BlockSpec((None, rows, LANES), lambda j, t: (t[2, j], 0, t[3, j]))],
        out_specs=pl.BlockSpec((rows, LANES), lambda j, t: (0, j)))
    return pl.pallas_call(body, grid_spec=gs, out_shape=_sds((rows, P_IN), wins.dtype), name="w_in_from_windows",
                          compiler_params=_params(("parallel",)))(jnp.asarray(table), wins, wins)


def _windows_from_layout(g):
    rows = g.shape[0]
    where = np.argsort(_padded_order_blocks())
    nb = WIN // LANES
    table = np.asarray([where[WIN_START[d] // LANES + b] for d in range(NDEV) for b in range(nb)], np.int32)

    def body(t_ref, x_ref, o_ref):
        o_ref[...] = x_ref[...]

    gs = pltpu.PrefetchScalarGridSpec(num_scalar_prefetch=1, grid=(NDEV, nb),
                                      in_specs=[pl.BlockSpec((rows, LANES), lambda d, b, t: (0, t[d * nb + b]))],
                                      out_specs=pl.BlockSpec((None, rows, LANES), lambda d, b, t: (d, 0, b)))
    return pl.pallas_call(body, grid_spec=gs, out_shape=_sds((NDEV, rows, WIN), g.dtype), name="w_in_grad_windows",
                          compiler_params=_params(("parallel", "parallel")))(jnp.asarray(table), g)


def _to_slots_cols(g):
    rows, cols = g.shape
    return g.reshape(rows, NDEV, cols // NDEV).transpose(1, 0, 2)


def _from_slots_cols(a):
    n, rows, cols = a.shape
    return a.transpose(1, 0, 2).reshape(rows, n * cols)


WEIGHT_ORDER = ["norm1_w", "w_in", "conv_w", "a_log", "dt_bias", "gdn_norm_w", "ret_norm_w", "w_out", "norm2_w", "w_gate", "w_up",
                "w_down", "norm_f_w"]


def kernel(x, norm1_w, w_in, conv_w, a_log, dt_bias, gdn_norm_w, ret_norm_w, w_out, norm2_w, w_gate, w_up, w_down, norm_f_w, loss_target, m_norm1_w, m_w_in, m_conv_w, m_a_log, m_dt_bias, m_gdn_norm_w, m_ret_norm_w, m_w_out, m_norm2_w, m_w_gate, m_w_up, m_w_down, m_norm_f_w, v_norm1_w, v_w_in, v_conv_w, v_a_log, v_dt_bias, v_gdn_norm_w, v_ret_norm_w, v_w_out, v_norm2_w, v_w_gate, v_w_up, v_w_down, v_norm_f_w):
    ax, ay, ac = _coords()
    me = 4 * ax + 2 * ay + ac
    core = jnp.reshape(ac, (1,)).astype(jnp.int32)
    chip = jnp.reshape(2 * ax + ay, (1,)).astype(jnp.int32)
    w = dict(norm1_w=norm1_w, w_in=w_in[0], conv_w=conv_w[0], a_log=a_log, dt_bias=dt_bias, gdn_norm_w=gdn_norm_w,
             ret_norm_w=ret_norm_w, w_out=w_out[0], norm2_w=norm2_w, w_gate=w_gate[0], w_up=w_up[0], w_down=w_down[0],
             norm_f_w=norm_f_w)
    m = dict(norm1_w=m_norm1_w, w_in=m_w_in[0], conv_w=m_conv_w[0], a_log=m_a_log, dt_bias=m_dt_bias, gdn_norm_w=m_gdn_norm_w,
             ret_norm_w=m_ret_norm_w, w_out=m_w_out[0], norm2_w=m_norm2_w, w_gate=m_w_gate[0], w_up=m_w_up[0], w_down=m_w_down[0],
             norm_f_w=m_norm_f_w)
    v = dict(norm1_w=v_norm1_w, w_in=v_w_in[0], conv_w=v_conv_w[0], a_log=v_a_log, dt_bias=v_dt_bias, gdn_norm_w=v_gdn_norm_w,
             ret_norm_w=v_ret_norm_w, w_out=v_w_out[0], norm2_w=v_norm2_w, w_gate=v_w_gate[0], w_up=v_w_up[0], w_down=v_w_down[0],
             norm_f_w=v_norm_f_w)
    big_names = ["w_in", "w_out", "w_gate", "w_up", "w_down"]

    w_in_wins, conv_all = _run_side(_gather_side([_window_of_shard(w["w_in"].astype(BF16), me), w["conv_w"]]), "w_in_allgather")
    w_in_full = _layout_from_windows(w_in_wins)
    conv_full = _from_slots_cols(conv_all)
    dist = dict(core=core, shards=[w[k].astype(BF16) for k in ("w_out", "w_gate", "w_up", "w_down")])

    loss_tile, dx, big, small = _local_step(
        x[0], loss_target[0], w_in_full, None, None, None, None, norm1_w, conv_full, a_log, dt_bias,
        gdn_norm_w, ret_norm_w, norm2_w, norm_f_w.reshape(1, D), dist=dist)
    loss = lax.psum(loss_tile[0, 0], ("x", "y", "c"))

    g_w_in = _shard_of_window(_sum_slots(big["w_in"], 64, "w_in_grad_sum"), me)
    out = {"w_in": (g_w_in, *_adamw_rows(w["w_in"], g_w_in, m["w_in"], v["w_in"], 64, "adamw_w_in"))}
    for k in ("w_out", "w_gate", "w_up", "w_down"):
        part, recv = big[k]
        out[k] = _adamw_reduced(part, recv, chip, w[k], m[k], v[k], 128, "adamw_" + k)

    g_small = _unpack_small(_allreduce_small(_pack_small(**small), "small_grads_allreduce"))
    g_small["conv_w"] = lax.dynamic_slice_in_dim(g_small["conv_w"], me * (2 * D // NDEV), 2 * D // NDEV, axis=1)
    small_names = [k for k in WEIGHT_ORDER if k not in big_names]
    pad_conv = lambda a: jnp.pad(a, ((0, 0), (0, 2 * D - a.shape[1])))
    packs = []
    for src in (w, g_small, m, v):
        args = {k: (pad_conv(src[k]) if k == "conv_w" else src[k]) for k in small_names}
        packs.append(_pack_small(**args))
    d_pack, m_pack, v_pack = _adamw_plain(*packs[0:1], packs[1], packs[2], packs[3], name="adamw_small")
    cut_conv = lambda dct: {**dct, "conv_w": dct["conv_w"][:, :2 * D // NDEV]}
    d_small, m_small, v_small = (cut_conv(_unpack_small(p)) for p in (d_pack, m_pack, v_pack))

    def shaped(k, a):
        return a.reshape(w_shapes[k])

    w_shapes = dict(norm1_w=norm1_w.shape, w_in=w_in.shape, conv_w=conv_w.shape, a_log=a_log.shape, dt_bias=dt_bias.shape,
                    gdn_norm_w=gdn_norm_w.shape, ret_norm_w=ret_norm_w.shape, w_out=w_out.shape, norm2_w=norm2_w.shape,
                    w_gate=w_gate.shape, w_up=w_up.shape, w_down=w_down.shape, norm_f_w=norm_f_w.shape)
    grads, deltas, new_m, new_v = [], [], [], []
    for k in WEIGHT_ORDER:
        if k in big_names:
            g_, d_, m_, v_ = out[k]
        else:
            g_, d_, m_, v_ = g_small[k], d_small[k], m_small[k], v_small[k]
        grads.append(shaped(k, g_))
        deltas.append(shaped(k, d_))
        new_m.append(shaped(k, m_))
        new_v.append(shaped(k, v_))
    return (loss, dx[None], *grads, *deltas, *new_m, *new_v)
```

```python
import functools
import numpy as np
import jax
import jax.numpy as jnp
from jax import lax
from jax.experimental import pallas as pl
from jax.experimental.pallas import tpu as pltpu

F32, BF16 = jnp.float32, jnp.bfloat16
HI = lax.Precision.HIGHEST
MESH = pl.DeviceIdType.MESH
ANY = pl.BlockSpec(memory_space=pl.ANY)
VMEM_FULL = pl.BlockSpec(memory_space=pltpu.VMEM)

NDEV = 8
D = 2048
H = 8
DK = 128
DV = 256
C = 64
CPB = 4
EPS = 1e-6
ROPE_BASE = 10000.0
N_IN = 16400
O_MERGE, O_QKV, O_RET, O_BA, P_IN = 0, 8192, 12288, 16384, 16640
MERGE_W, RET_W = 4 * DV, 2 * DK + DV
LR, B1, B2, EPS_ADAM, WD, STEP = 0.001, 0.9, 0.999, 1e-08, 0.01, 10
VMEM_CAP = 60 * 1024 * 1024

NN = ((1,), (0,))
NT = ((1,), (1,))
TN = ((0,), (0,))


def _params(sem=None, est=None):
    kw = {}
    if sem is not None:
        kw["dimension_semantics"] = sem
    if est is not None:
        kw["vmem_limit_bytes"] = int(min(VMEM_CAP, max(32 * 1024 * 1024, est * 5 // 4 + (4 << 20))))
    return pltpu.CompilerParams(**kw)


def _sds(shape, dt=F32):
    return jax.ShapeDtypeStruct(tuple(shape), dt)


def _row_tile(rows, limit):
    return max(t for t in range(16, min(rows, limit) + 1, 16) if rows % t == 0)


def _bdot(a, b, dims):
    return lax.dot_general(a.astype(BF16), b.astype(BF16), (dims, ((), ())), preferred_element_type=F32)


def _hdot(a, b, dims):
    return lax.dot_general(a, b, (dims, ((), ())), precision=HI, preferred_element_type=F32)


_sigmoid = jax.nn.sigmoid


def _silu(x):
    return x * _sigmoid(x)


def _rms(x, w):
    return x * lax.rsqrt(jnp.mean(x * x, axis=-1, keepdims=True) + EPS) * w


class _Side:
    def __init__(self, ins, out_shapes, sems, start, finish):
        self.ins, self.out_shapes, self.sems, self.start, self.finish = list(ins), list(out_shapes), list(sems), start, finish


def _run_side(side, name):
    ni, no = len(side.ins), len(side.out_shapes)

    def body(*refs):
        ins, outs, sems = refs[:ni], refs[ni:ni + no], refs[ni + no:]
        side.start(ins, outs, *sems)
        side.finish(ins, outs, *sems)

    return pl.pallas_call(body, in_specs=[ANY] * ni, out_specs=[ANY] * no, out_shape=side.out_shapes,
                          scratch_shapes=[pltpu.SemaphoreType.DMA(s) for s in side.sems], name=name)(*side.ins)


def _matmul(a, b, *, ta=False, tb=False, tm, tn, tk, out_dtype=F32, res=None, side=None, name):
    m = a.shape[1] if ta else a.shape[0]
    k = a.shape[0] if ta else a.shape[1]
    n = b.shape[0] if tb else b.shape[1]
    assert k == (b.shape[1] if tb else b.shape[0])
    tm, tn, tk = min(tm, m), min(tn, n), min(tk, k)
    assert m % tm == 0 and n % tn == 0 and k % tk == 0, (name, m, n, k, tm, tn, tk)
    nk = k // tk
    dims = ((0 if ta else 1,), (1 if tb else 0,))
    has_res = res is not None
    n_in = 3 if has_res else 2
    n_side_in = len(side.ins) if side else 0
    n_side_out = len(side.out_shapes) if side else 0
    grid = (m // tm, n // tn, nk)

    def body(*refs):
        a_ref, b_ref = refs[0], refs[1]
        r_ref = refs[2] if has_res else None
        o_ref = refs[n_in + n_side_in]
        if side:
            side_ins = refs[n_in:n_in + n_side_in]
            side_outs = refs[n_in + n_side_in + 1:n_in + n_side_in + 1 + n_side_out]
            side_sems = refs[len(refs) - len(side.sems):]
            step = (pl.program_id(0) * grid[1] + pl.program_id(1)) * grid[2] + pl.program_id(2)

            @pl.when(step == 0)
            def _():
                side.start(side_ins, side_outs, *side_sems)

        def finish(acc):
            if has_res:
                acc = acc + r_ref[...].astype(F32)
            o_ref[...] = acc.astype(out_dtype)

        part = _bdot(a_ref[...], b_ref[...], dims)
        if nk == 1:
            finish(part)
        else:
            acc_ref = refs[n_in + n_side_in + 1 + n_side_out]
            kk = pl.program_id(2)

            @pl.when(kk == 0)
            def _():
                acc_ref[...] = part

            @pl.when(kk > 0)
            def _():
                acc_ref[...] += part

            @pl.when(kk == nk - 1)
            def _():
                finish(acc_ref[...])

        if side:
            @pl.when(step == grid[0] * grid[1] * grid[2] - 1)
            def _():
                side.finish(side_ins, side_outs, *side_sems)

    a_spec = pl.BlockSpec((tk, tm), lambda i, j, kk: (kk, i)) if ta else pl.BlockSpec((tm, tk), lambda i, j, kk: (i, kk))
    b_spec = pl.BlockSpec((tn, tk), lambda i, j, kk: (j, kk)) if tb else pl.BlockSpec((tk, tn), lambda i, j, kk: (kk, j))
    o_spec = pl.BlockSpec((tm, tn), lambda i, j, kk: (i, j))
    in_specs = [a_spec, b_spec] + ([o_spec] if has_res else []) + [ANY] * n_side_in
    est = 2 * (tm * tk * a.dtype.itemsize + tk * tn * b.dtype.itemsize + tm * tn * jnp.dtype(out_dtype).itemsize)
    est += 2 * tm * tn * 4 * (1 if has_res else 0) + (tm * tn * 4 if nk > 1 else 0) + 2 * tm * tn * 4
    args = (a, b) + ((res,) if has_res else ()) + (tuple(side.ins) if side else ())
    scratch = ([pltpu.VMEM((tm, tn), F32)] if nk > 1 else []) + ([pltpu.SemaphoreType.DMA(s) for s in side.sems] if side else [])
    sem = ("arbitrary",) * 3 if side else ("parallel", "parallel", "arbitrary")
    out = pl.pallas_call(
        body, grid=grid, in_specs=in_specs, out_specs=[o_spec] + [ANY] * n_side_out,
        out_shape=[_sds((m, n), out_dtype)] + (side.out_shapes if side else []),
        scratch_shapes=scratch, name=name, compiler_params=_params(sem, est))(*args)
    return (out[0], out[1:]) if side else out[0]


def _rms_fwd(x, w, ts, name):
    s = x.shape[0]

    def body(x_ref, w_ref, o_ref, ot_ref):
        y = _rms(x_ref[...], w_ref[...]).astype(BF16)
        o_ref[...] = y
        ot_ref[...] = y.T

    row = pl.BlockSpec((ts, D), lambda i: (i, 0))
    return pl.pallas_call(body, grid=(s // ts,), in_specs=[row, pl.BlockSpec((1, D), lambda i: (0, 0))],
                          out_specs=[row, pl.BlockSpec((D, ts), lambda i: (0, i))],
                          out_shape=[_sds((s, D), BF16), _sds((D, s), BF16)], name=name,
                          compiler_params=_params(("parallel",)))(x, w)


def _rms_bwd(x, w, du, dres, ts, name):
    s = x.shape[0]

    def body(x_ref, w_ref, du_ref, dres_ref, dx_ref, dw_ref):
        _, vjp = jax.vjp(_rms, x_ref[...], w_ref[...])
        dx, dw = vjp(du_ref[...].astype(F32))
        dx_ref[...] = dx + dres_ref[...]

        @pl.when(pl.program_id(0) == 0)
        def _():
            dw_ref[...] = jnp.zeros_like(dw_ref)

        dw_ref[...] += dw

    row = pl.BlockSpec((ts, D), lambda i: (i, 0))
    vec = pl.BlockSpec((1, D), lambda i: (0, 0))
    return pl.pallas_call(body, grid=(s // ts,), in_specs=[row, vec, row, row], out_specs=[row, vec],
                          out_shape=[_sds((s, D)), _sds((1, D))], name=name,
                          compiler_params=_params(("arbitrary",), 12 * ts * D * 4))(x, w, du, dres)


def _conv_taps(xx, w, base, ts):
    acc = xx[base:base + ts] * w[0:1, :]
    for j in range(1, 4):
        acc = acc + xx[base + j:base + j + ts] * w[j:j + 1, :]
    return acc


def _causal_conv(prev8, cur, w, first):
    xx = jnp.concatenate([jnp.where(first, 0.0, prev8), cur], axis=0)
    return _conv_taps(xx, w, 5, cur.shape[0])


def _qk_post(c, scale):
    s = _silu(c)
    return s * lax.rsqrt(jnp.sum(s * s, axis=-1, keepdims=True) + EPS) * scale


def _conv_specs(ts, cw, col0):
    pcol = O_QKV // cw + col0
    cur = pl.BlockSpec((ts, cw), lambda j, i: (i, pcol + j))
    prev = pl.BlockSpec((8, cw), lambda j, i: (jnp.maximum(i * (ts // 8) - 1, 0), pcol + j))
    wsp = pl.BlockSpec((4, cw), lambda j, i: (0, col0 + j))
    return cur, prev, wsp


def _gdn_qkv_fwd(proj, conv_w, ts):
    s = proj.shape[0]

    def qk_body(cur_ref, prev_ref, w_ref, o_ref, c_ref):
        c = _causal_conv(prev_ref[...], cur_ref[...], w_ref[...], pl.program_id(1) == 0)
        scale = jnp.where(pl.program_id(0) < H, DK ** -0.5, 1.0).astype(F32)
        c_ref[...] = c
        o_ref[...] = _qk_post(c, scale)

    tq = min(2 * ts, s)
    cur, prev, wsp = _conv_specs(tq, DK, 0)
    out = pl.BlockSpec((tq, DK), lambda j, i: (i, j))
    qk, c_qk = pl.pallas_call(qk_body, grid=(2 * H, s // tq), in_specs=[cur, prev, wsp], out_specs=[out, out],
                              out_shape=[_sds((s, 2 * H * DK))] * 2, name="gdn_qk_prep",
                              compiler_params=_params(("parallel", "parallel")))(proj, proj, conv_w)

    def v_body(cur_ref, prev_ref, w_ref, o_ref, c_ref):
        c = _causal_conv(prev_ref[...], cur_ref[...], w_ref[...], pl.program_id(1) == 0)
        c_ref[...] = c
        o_ref[...] = _silu(c)

    cw = 512
    cur, prev, wsp = _conv_specs(ts, cw, 2 * H * DK // cw)
    out = pl.BlockSpec((ts, cw), lambda j, i: (i, j))
    v, c_v = pl.pallas_call(v_body, grid=(H * DV // cw, s // ts), in_specs=[cur, prev, wsp], out_specs=[out, out],
                            out_shape=[_sds((s, H * DV))] * 2, name="gdn_v_prep",
                            compiler_params=_params(("parallel", "parallel")))(proj, proj, conv_w)
    return qk, v, c_qk, c_v


def _gdn_qkv_bwd(proj, conv_w, c_qk, c_v, dqk, dv, dproj, ts):
    s = proj.shape[0]
    nt = s // ts

    def qk_body(c_ref, d_ref, o_ref):
        scale = jnp.where(pl.program_id(0) < H, DK ** -0.5, 1.0).astype(F32)
        _, vjp = jax.vjp(lambda cc: _qk_post(cc, scale), c_ref[...])
        o_ref[...] = vjp(d_ref[...])[0]

    tq = min(2 * ts, s)
    blk = pl.BlockSpec((tq, DK), lambda j, i: (i, j))
    dc_qk = pl.pallas_call(qk_body, grid=(2 * H, s // tq), in_specs=[blk, blk], out_specs=blk, out_shape=_sds((s, 2 * H * DK)),
                           name="gdn_qk_prep_bwd", compiler_params=_params(("parallel", "parallel")))(c_qk, dqk)

    def v_body(c_ref, d_ref, o_ref):
        _, vjp = jax.vjp(_silu, c_ref[...])
        o_ref[...] = vjp(d_ref[...])[0]

    cw = 512
    blk = pl.BlockSpec((ts, cw), lambda j, i: (i, j))
    dc_v = pl.pallas_call(v_body, grid=(H * DV // cw, nt), in_specs=[blk, blk], out_specs=blk, out_shape=_sds((s, H * DV)),
                          name="gdn_v_prep_bwd", compiler_params=_params(("parallel", "parallel")))(c_v, dv)

    def conv_bwd(dc, dproj, col0, ncols, name):
        def body(x_ref, xprev_ref, w_ref, dc_ref, dcnext_ref, _, da_ref, dw_ref):
            i = pl.program_id(1)
            w = w_ref[...]
            dcur = dc_ref[...]
            dd = jnp.concatenate([dcur, jnp.where(i == nt - 1, 0.0, dcnext_ref[...])], axis=0)
            acc = dd[3:3 + ts] * w[0:1, :]
            for j in range(1, 4):
                acc = acc + dd[3 - j:3 - j + ts] * w[j:j + 1, :]
            da_ref[...] = acc.astype(BF16)
            xx = jnp.concatenate([jnp.where(i == 0, 0.0, xprev_ref[...]), x_ref[...]], axis=0)

            @pl.when(i == 0)
            def _():
                dw_ref[...] = jnp.zeros_like(dw_ref)

            for j in range(4):
                dw_ref[j:j + 1, :] += jnp.sum(dcur * xx[5 + j:5 + j + ts], axis=0, keepdims=True)

        cur, prev, wsp = _conv_specs(ts, cw, col0)
        dcur = pl.BlockSpec((ts, cw), lambda j, i: (i, j))
        dnext = pl.BlockSpec((8, cw), lambda j, i: (jnp.minimum((i + 1) * (ts // 8), s // 8 - 1), j))
        pcol = O_QKV // cw + col0
        return pl.pallas_call(body, grid=(ncols // cw, nt), in_specs=[cur, prev, wsp, dcur, dnext, ANY],
                              out_specs=[pl.BlockSpec((ts, cw), lambda j, i: (i, pcol + j)), pl.BlockSpec((4, cw), lambda j, i: (0, j))],
                              out_shape=[_sds(dproj.shape, BF16), _sds((4, ncols))], input_output_aliases={5: 0}, name=name,
                              compiler_params=_params(("parallel", "arbitrary")))(proj, proj, conv_w, dc, dc, dproj)

    dproj, dw_qk = conv_bwd(dc_qk, dproj, 0, 2 * H * DK, "conv_bwd_qk")
    dproj, dw_v = conv_bwd(dc_v, dproj, 2 * H * DK // cw, H * DV, "conv_bwd_v")
    return dproj, jnp.concatenate([dw_qk, dw_v], axis=1)


def _bg(b, a, alog, dtb):
    n = b.shape[0]
    g = -jnp.exp(alog) * jax.nn.softplus(a + dtb)
    row = lax.broadcasted_iota(jnp.int32, (n, n), 0)
    col = lax.broadcasted_iota(jnp.int32, (n, n), 1)
    shift = C.bit_length() - 1
    same = (row >> shift) == (col >> shift)
    return _sigmoid(b), _hdot((same & (row >= col)).astype(F32), g, NN), _hdot(same.astype(F32), g, NN)


def _bg_fwd(proj, alog, dtb, ts):
    s = proj.shape[0]

    def body(ba_ref, alog_ref, dtb_ref, beta_ref, gc_ref, gl_ref):
        beta_ref[...], gc_ref[...], gl_ref[...] = _bg(ba_ref[:, 0:H], ba_ref[:, H:2 * H], alog_ref[...], dtb_ref[...])

    small = pl.BlockSpec((1, H), lambda i: (0, 0))
    out = pl.BlockSpec((ts, H), lambda i: (i, 0))
    return pl.pallas_call(body, grid=(s // ts,), in_specs=[pl.BlockSpec((ts, 256), lambda i: (i, O_BA // 256)), small, small],
                          out_specs=[out] * 3, out_shape=[_sds((s, H))] * 3, name="gdn_bg_prep",
                          compiler_params=_params(("parallel",)))(proj, alog, dtb)


def _bg_bwd(proj, alog, dtb, dbeta_h, dgc_h, dgl_h, dproj, ts):
    s = proj.shape[0]

    def body(ba_ref, alog_ref, dtb_ref, dbeta_ref, dgc_ref, dgl_ref, _, dba_ref, dalog_ref, ddtb_ref):
        _, vjp = jax.vjp(_bg, ba_ref[:, 0:H], ba_ref[:, H:2 * H], alog_ref[...], dtb_ref[...])
        db, da, dalog, ddtb = vjp((jnp.sum(dbeta_ref[...], axis=0), jnp.sum(dgc_ref[...], axis=0), jnp.sum(dgl_ref[...], axis=0)))
        dba_ref[...] = jnp.zeros_like(dba_ref)
        dba_ref[:, 0:H] = db.astype(BF16)
        dba_ref[:, H:2 * H] = da.astype(BF16)

        @pl.when(pl.program_id(0) == 0)
        def _():
            dalog_ref[...] = jnp.zeros_like(dalog_ref)
            ddtb_ref[...] = jnp.zeros_like(ddtb_ref)

        dalog_ref[...] += dalog
        ddtb_ref[...] += ddtb

    small = pl.BlockSpec((1, H), lambda i: (0, 0))
    per_head = pl.BlockSpec((H, ts, H), lambda i: (0, i, 0))
    return pl.pallas_call(body, grid=(s // ts,),
                          in_specs=[pl.BlockSpec((ts, 256), lambda i: (i, O_BA // 256)), small, small, per_head, per_head, per_head, ANY],
                          out_specs=[pl.BlockSpec((ts, 256), lambda i: (i, O_BA // 256)), small, small],
                          out_shape=[_sds(dproj.shape, BF16), _sds((1, H)), _sds((1, H))], input_output_aliases={6: 0},
                          name="gdn_bg_prep_bwd", compiler_params=_params(("arbitrary",)))(proj, alog, dtb, dbeta_h, dgc_h, dgl_h, dproj)


BLK = 4 * C
NNB, NTB, TNB = ((2,), (1,)), ((2,), (2,)), ((1,), (1,))


def _bdot_b(a, b, dims):
    return lax.dot_general(a.astype(BF16), b.astype(BF16), (dims, ((0,), (0,))), preferred_element_type=F32)


@jax.custom_vjp
def _inv_unit_lower(a):
    n = a.shape[-1]
    row = lax.broadcasted_iota(jnp.int32, (n, n), 0)
    col = lax.broadcasted_iota(jnp.int32, (n, n), 1)
    x = jnp.where(row == col, 1.0, 0.0).astype(F32) - a
    p = _bdot_b(a, a, NNB)
    power = 2
    while True:
        x = x + _bdot_b(x, p, NNB)
        power *= 2
        if power >= C:
            return x
        p = _bdot_b(p, p, NNB)


def _inv_fwd(a):
    t = _inv_unit_lower(a)
    return t, t


def _inv_bwd(t, dt):
    return (-_bdot_b(_bdot_b(t, dt, TNB), t, NTB),)


_inv_unit_lower.defvjp(_inv_fwd, _inv_bwd)


def _gdn_prep(q, k, v, bfull, gcfull, glfull, hmask):
    nb, n = q.shape[0], q.shape[1]
    beta = jnp.sum(bfull * hmask, axis=-1, keepdims=True)
    gc = jnp.sum(gcfull * hmask, axis=-1, keepdims=True)
    gl = jnp.sum(glfull * hmask, axis=-1, keepdims=True)
    row = lax.broadcasted_iota(jnp.int32, (n, n), 0)
    col = lax.broadcasted_iota(jnp.int32, (n, n), 1)
    shift = C.bit_length() - 1
    same = (row >> shift) == (col >> shift)
    incl, strict = same & (row >= col), same & (row > col)
    g_i = gc * jnp.ones((1, 1, n), F32)
    decay = jnp.exp(jnp.where(incl, g_i - jnp.swapaxes(g_i, 1, 2), -jnp.inf))
    kb = k * beta
    a = jnp.where(strict, _bdot_b(kb, k, NTB) * decay, 0.0)
    tinv = _inv_unit_lower(a)
    u = _bdot_b(tinv, v * beta, NNB)
    w = _bdot_b(tinv, kb * jnp.exp(gc), NNB)
    attn = _bdot_b(q, k, NTB) * decay
    fold = ((lax.broadcasted_iota(jnp.int32, (n, C), 0) & (C - 1)) == lax.broadcasted_iota(jnp.int32, (n, C), 1)).astype(F32)
    attn_c = _bdot(attn.reshape(nb * n, n), fold, NN).reshape(nb, n, C)
    return u, w, attn_c, q * jnp.exp(gc), k * jnp.exp(gl - gc), jnp.exp(gl)


def _gdn_step(u, w, attn, qg, kd, egl, state):
    v_new = u - _bdot_b(w, state, NNB)
    o = _bdot_b(qg, state, NNB) + _bdot_b(attn, v_new, NNB)
    return o, state * egl + _bdot_b(kd, v_new, TNB)


def _heads(ref, rs, width):
    return jnp.stack([ref[rs, h * width:(h + 1) * width] for h in range(H)])


def _head_mask(h):
    return (lax.broadcasted_iota(jnp.int32, (1, H), 1) == h).astype(F32)


PREP_BLOCKS = 2


def _gdn_prep_specs(r):
    small = pl.BlockSpec((r, H), lambda h, c: (c, 0))
    return [pl.BlockSpec((r, DK), lambda h, c: (c, h)), pl.BlockSpec((r, DK), lambda h, c: (c, H + h)),
            pl.BlockSpec((r, DV), lambda h, c: (c, h)), small, small, small]


def _blocked(ref):
    x = ref[...]
    return x.reshape(PREP_BLOCKS, BLK, x.shape[-1])


def _gdn_inter_specs(r):
    col = pl.BlockSpec((r, DK), lambda h, c: (c, h))
    return [pl.BlockSpec((r, DV), lambda h, c: (c, h)), col, pl.BlockSpec((1, r, C), lambda h, c: (h, c, 0)), col, col,
            pl.BlockSpec((1, r // C, 8, 128), lambda h, c: (h, c, 0, 0))]


def _gdn_prep_fwd(qk, v, beta, gc, gl):
    s = qk.shape[0]
    r = PREP_BLOCKS * BLK

    def body(q_ref, k_ref, v_ref, b_ref, gc_ref, gl_ref, u_ref, w_ref, attn_ref, qg_ref, kd_ref, egl_ref):
        u, w, attn, qg, kd, egl = _gdn_prep(_blocked(q_ref), _blocked(k_ref), _blocked(v_ref), _blocked(b_ref), _blocked(gc_ref),
                                            _blocked(gl_ref), _head_mask(pl.program_id(0)))
        u_ref[...] = u.reshape(r, DV)
        w_ref[...] = w.reshape(r, DK).astype(BF16)
        attn_ref[0] = attn.reshape(r, C).astype(BF16)
        qg_ref[...] = qg.reshape(r, DK).astype(BF16)
        kd_ref[...] = kd.reshape(r, DK).astype(BF16)
        egl = egl.reshape(r, 1)
        for j in range(r // C):
            egl_ref[0, j] = egl[j * C:j * C + 1, :] * jnp.ones((8, 128), F32)

    out_shape = [_sds((s, H * DV)), _sds((s, H * DK), BF16), _sds((H, s, C), BF16), _sds((s, H * DK), BF16),
                 _sds((s, H * DK), BF16), _sds((H, s // C, 8, 128))]
    return pl.pallas_call(body, grid=(H, s // r), in_specs=_gdn_prep_specs(r), out_specs=_gdn_inter_specs(r), out_shape=out_shape,
                          name="gdn_prep_fwd", compiler_params=_params(("parallel", "parallel")))(qk, qk, v, beta, gc, gl)


def _gdn_prep_bwd(qk, v, beta, gc, gl, du, dw, dattn, dqg, dkd, degl):
    s = qk.shape[0]
    r = PREP_BLOCKS * BLK

    def body(q_ref, k_ref, v_ref, b_ref, gc_ref, gl_ref, du_ref, dw_ref, dattn_ref, dqg_ref, dkd_ref, degl_ref,
             dq_ref, dk_ref, dv_ref, db_ref, dgc_ref, dgl_ref):
        hmask = _head_mask(pl.program_id(0))
        _, vjp = jax.vjp(lambda q, k, v, b, gc, gl: _gdn_prep(q, k, v, b, gc, gl, hmask), _blocked(q_ref), _blocked(k_ref),
                         _blocked(v_ref), _blocked(b_ref), _blocked(gc_ref), _blocked(gl_ref))
        rowid = lax.broadcasted_iota(jnp.int32, (r, 1), 0)
        degl = jnp.zeros((r, 1), F32)
        for j in range(r // C):
            degl = jnp.where(rowid == j * C, degl_ref[0, j, 0:1, 0:1], degl)
        dq, dk, dv, db, dgc, dgl = vjp((_blocked(du_ref), _blocked(dw_ref), _blocked(dattn_ref.at[0]), _blocked(dqg_ref),
                                        _blocked(dkd_ref), degl.reshape(PREP_BLOCKS, BLK, 1)))
        dq_ref[...] = dq.reshape(r, DK)
        dk_ref[...] = dk.reshape(r, DK)
        dv_ref[...] = dv.reshape(r, DV)
        db_ref[0] = db.reshape(r, H)
        dgc_ref[0] = dgc.reshape(r, H)
        dgl_ref[0] = dgl.reshape(r, H)

    col = pl.BlockSpec((r, DK), lambda h, c: (c, h))
    piece = pl.BlockSpec((1, r, H), lambda h, c: (h, c, 0))
    dq, dk, dv, db, dgc, dgl = pl.pallas_call(
        body, grid=(H, s // r), in_specs=_gdn_prep_specs(r) + _gdn_inter_specs(r),
        out_specs=[col, col, pl.BlockSpec((r, DV), lambda h, c: (c, h)), piece, piece, piece],
        out_shape=[_sds((s, H * DK)), _sds((s, H * DK)), _sds((s, H * DV))] + [_sds((H, s, H))] * 3,
        name="gdn_prep_bwd", compiler_params=_params(("parallel", "parallel"), 24 << 20))(
            qk, qk, v, beta, gc, gl, du, dw, dattn, dqg, dkd, degl)
    return jnp.concatenate([dq, dk], axis=1), dv, db, dgc, dgl


def _gdn_scan_specs(r, order):
    wide = pl.BlockSpec((r, H * DK), lambda c: (order(c), 0))
    return [pl.BlockSpec((r, H * DV), lambda c: (order(c), 0)), wide, pl.BlockSpec((H, r, C), lambda c: (0, order(c), 0)), wide, wide,
            pl.BlockSpec((H, r // C, 8, 128), lambda c: (0, order(c), 0, 0))]


def _gdn_scan_fwd(u, w, attn, qg, kd, egl):
    s = u.shape[0]
    r = CPB * C
    nb = s // r

    def body(u_ref, w_ref, attn_ref, qg_ref, kd_ref, egl_ref, o_ref, st_ref, state_ref):
        @pl.when(pl.program_id(0) == 0)
        def _():
            state_ref[...] = jnp.zeros_like(state_ref)

        state = state_ref[...]
        for i in range(CPB):
            rs = slice(i * C, (i + 1) * C)
            st_ref[:, i] = state
            o, state = _gdn_step(_heads(u_ref, rs, DV), _heads(w_ref, rs, DK), attn_ref[:, rs, :], _heads(qg_ref, rs, DK),
                                 _heads(kd_ref, rs, DK), egl_ref[:, i, 0:1, 0:1], state)
            for h in range(H):
                o_ref[rs, h * DV:(h + 1) * DV] = o[h]
        state_ref[...] = state

    out_specs = [pl.BlockSpec((r, H * DV), lambda c: (c, 0)), pl.BlockSpec((H, CPB, DK, DV), lambda c: (0, c, 0, 0))]
    return pl.pallas_call(body, grid=(nb,), in_specs=_gdn_scan_specs(r, lambda c: c), out_specs=out_specs,
                          out_shape=[_sds((s, H * DV)), _sds((H, s // C, DK, DV))],
                          scratch_shapes=[pltpu.VMEM((H, DK, DV), F32)], name="gdn_scan_fwd",
                          compiler_params=_params(("arbitrary",), 24 << 20))(u, w, attn, qg, kd, egl)


def _gdn_scan_bwd(u, w, attn, qg, kd, egl, states, do):
    s = u.shape[0]
    r = CPB * C
    nb = s // r

    def body(u_ref, w_ref, attn_ref, qg_ref, kd_ref, egl_ref, st_ref, do_ref,
             du_ref, dw_ref, dattn_ref, dqg_ref, dkd_ref, degl_ref, dstate_ref):
        @pl.when(pl.program_id(0) == 0)
        def _():
            dstate_ref[...] = jnp.zeros_like(dstate_ref)

        dstate = dstate_ref[...]
        for i in reversed(range(CPB)):
            rs = slice(i * C, (i + 1) * C)
            _, vjp = jax.vjp(_gdn_step, _heads(u_ref, rs, DV), _heads(w_ref, rs, DK).astype(F32), attn_ref[:, rs, :].astype(F32),
                             _heads(qg_ref, rs, DK).astype(F32), _heads(kd_ref, rs, DK).astype(F32), egl_ref[:, i, 0:1, 0:1],
                             st_ref[:, i])
            du, dw, dattn, dqg, dkd, degl, dstate = vjp((_heads(do_ref, rs, DV), dstate))
            dattn_ref[:, rs, :] = dattn
            degl_ref[:, i] = degl * jnp.ones((1, 8, 128), F32)
            for h in range(H):
                du_ref[rs, h * DV:(h + 1) * DV] = du[h]
                dw_ref[rs, h * DK:(h + 1) * DK] = dw[h]
                dqg_ref[rs, h * DK:(h + 1) * DK] = dqg[h]
                dkd_ref[rs, h * DK:(h + 1) * DK] = dkd[h]
        dstate_ref[...] = dstate

    rev = lambda c: nb - 1 - c
    in_specs = _gdn_scan_specs(r, rev) + [pl.BlockSpec((H, CPB, DK, DV), lambda c: (0, rev(c), 0, 0)),
                                          pl.BlockSpec((r, H * DV), lambda c: (rev(c), 0))]
    return pl.pallas_call(
        body, grid=(nb,), in_specs=in_specs, out_specs=_gdn_scan_specs(r, rev),
        out_shape=[_sds((s, H * DV)), _sds((s, H * DK)), _sds((H, s, C)), _sds((s, H * DK)), _sds((s, H * DK)),
                   _sds((H, s // C, 8, 128))],
        scratch_shapes=[pltpu.VMEM((H, DK, DV), F32)], name="gdn_scan_bwd",
        compiler_params=_params(("arbitrary",), 40 << 20))(u, w, attn, qg, kd, egl, states, do)


def _rot(x, cs, sn):
    return x * cs + pltpu.roll(x, DK // 2, 1) * sn


def _rot_t(d, cs, sn):
    return d * cs - pltpu.roll(d, DK // 2, 1) * sn


def _ret_chunk(q, k, v, state, lg):
    n = q.shape[1]
    row = lax.broadcasted_iota(jnp.int32, (n, n), 0)
    col = lax.broadcasted_iota(jnp.int32, (n, n), 1)
    dist = (row - col).astype(F32)
    dmat = jnp.exp(jnp.where(dist >= 0, dist * lg, -jnp.inf))
    scores = _bdot_b(q, k, NTB) * dmat
    pos = lax.broadcasted_iota(jnp.int32, (n, 1), 0).astype(F32)
    xi = jnp.exp((pos + 1.0) * lg)
    zeta = jnp.exp((n - 1.0 - pos) * lg)
    o = _bdot_b(scores, v, NNB) + _bdot_b(q, state, NNB) * xi
    new_state = state * jnp.exp(n * lg) + _bdot_b(k * zeta, v, TNB)
    return o, new_state


def _ret_specs(r, order):
    return [pl.BlockSpec((r, H * RET_W), lambda c: (order(c), O_RET // (H * RET_W))), pl.BlockSpec((r, DK), lambda c: (order(c), 0)),
            pl.BlockSpec((r, DK), lambda c: (order(c), 0)), pl.BlockSpec((H, 1, 1), lambda c: (0, 0, 0))]


def _ret_qkv(x_ref, cs, sn):
    q = jnp.stack([_rot(x_ref[:, h * RET_W:h * RET_W + DK], cs, sn) for h in range(H)])
    k = jnp.stack([_rot(x_ref[:, h * RET_W + DK:h * RET_W + 2 * DK], cs, sn) for h in range(H)]) * DK ** -0.5
    v = jnp.stack([x_ref[:, h * RET_W + 2 * DK:(h + 1) * RET_W] for h in range(H)])
    return q, k, v


RET_C = 256


def _ret_scan_fwd(proj, cs, sn, lgtab):
    s = proj.shape[0]
    r = min(RET_C, s)
    nb = s // r

    def body(x_ref, cs_ref, sn_ref, lg_ref, o_ref, st_ref, state_ref):
        @pl.when(pl.program_id(0) == 0)
        def _():
            state_ref[...] = jnp.zeros_like(state_ref)

        state = state_ref[...]
        st_ref[:, 0] = state
        q, k, v = _ret_qkv(x_ref, cs_ref[...], sn_ref[...])
        o, state_ref[...] = _ret_chunk(q, k, v, state, lg_ref[...])
        for h in range(H):
            o_ref[:, h * DV:(h + 1) * DV] = o[h]

    out_specs = [pl.BlockSpec((r, H * DV), lambda c: (c, 0)), pl.BlockSpec((H, 1, DK, DV), lambda c: (0, c, 0, 0))]
    return pl.pallas_call(body, grid=(nb,), in_specs=_ret_specs(r, lambda c: c), out_specs=out_specs,
                          out_shape=[_sds((s, H * DV)), _sds((H, nb, DK, DV))],
                          scratch_shapes=[pltpu.VMEM((H, DK, DV), F32)], name="ret_scan_fwd",
                          compiler_params=_params(("arbitrary",), 32 << 20))(proj, cs, sn, lgtab.reshape(H, 1, 1))


def _ret_scan_bwd(proj, cs, sn, lgtab, states, do, dproj):
    s = proj.shape[0]
    r = min(RET_C, s)
    nb = s // r

    def body(x_ref, cs_ref, sn_ref, lg_ref, st_ref, do_ref, _, d_ref, dstate_ref):
        @pl.when(pl.program_id(0) == 0)
        def _():
            dstate_ref[...] = jnp.zeros_like(dstate_ref)

        cs_, sn_ = cs_ref[...], sn_ref[...]
        lg = lg_ref[...]
        q, k, v = _ret_qkv(x_ref, cs_, sn_)
        _, vjp = jax.vjp(lambda q, k, v, st: _ret_chunk(q, k, v, st, lg), q, k, v, st_ref[:, 0])
        dq, dk, dv, dstate_ref[...] = vjp((_heads(do_ref, slice(None), DV), dstate_ref[...]))
        for h in range(H):
            d_ref[:, h * RET_W:h * RET_W + DK] = _rot_t(dq[h], cs_, sn_).astype(BF16)
            d_ref[:, h * RET_W + DK:h * RET_W + 2 * DK] = _rot_t(dk[h] * DK ** -0.5, cs_, sn_).astype(BF16)
            d_ref[:, h * RET_W + 2 * DK:(h + 1) * RET_W] = dv[h].astype(BF16)

    rev = lambda c: nb - 1 - c
    in_specs = _ret_specs(r, rev) + [pl.BlockSpec((H, 1, DK, DV), lambda c: (0, rev(c), 0, 0)),
                                     pl.BlockSpec((r, H * DV), lambda c: (rev(c), 0)), ANY]
    return pl.pallas_call(
        body, grid=(nb,), in_specs=in_specs, out_specs=pl.BlockSpec((r, H * RET_W), lambda c: (rev(c), O_RET // (H * RET_W))),
        out_shape=_sds(dproj.shape, BF16), input_output_aliases={6: 0},
        scratch_shapes=[pltpu.VMEM((H, DK, DV), F32)], name="ret_scan_bwd",
        compiler_params=_params(("arbitrary",), 48 << 20))(proj, cs, sn, lgtab.reshape(H, 1, 1), states, do, dproj)


def _merge(oa, z, ob, rg, ga, gb, wa, wb):
    ya = oa * lax.rsqrt(jnp.mean(oa * oa, axis=-1, keepdims=True) + EPS) * wa * _silu(z)
    mu = jnp.mean(ob, axis=-1, keepdims=True)
    var = jnp.mean(jnp.square(ob - mu), axis=-1, keepdims=True)
    yb = (ob - mu) * lax.rsqrt(var + EPS) * wb * _silu(rg)
    return _sigmoid(ga) * ya + _sigmoid(gb) * yb


def _merge_specs(ts):
    own = pl.BlockSpec((ts, DV), lambda h, i: (i, h))
    grp = lambda k: pl.BlockSpec((ts, DV), lambda h, i: (i, O_MERGE // DV + 4 * h + k))
    return [own, grp(0), own, grp(1), grp(2), grp(3),
            pl.BlockSpec((1, DV), lambda h, i: (0, 0)), pl.BlockSpec((1, DV), lambda h, i: (0, h))]


def _merge_fwd(oa, ob, proj, wa, wb, ts):
    s = oa.shape[0]

    def body(oa_ref, z_ref, ob_ref, rg_ref, ga_ref, gb_ref, wa_ref, wb_ref, o_ref, ot_ref):
        y = _merge(oa_ref[...], z_ref[...], ob_ref[...], rg_ref[...], ga_ref[...], gb_ref[...],
                   wa_ref[...], wb_ref[...]).astype(BF16)
        o_ref[...] = y
        ot_ref[...] = y.T

    return pl.pallas_call(body, grid=(H, s // ts), in_specs=_merge_specs(ts),
                          out_specs=[pl.BlockSpec((ts, DV), lambda h, i: (i, h)), pl.BlockSpec((DV, ts), lambda h, i: (h, i))],
                          out_shape=[_sds((s, H * DV), BF16), _sds((H * DV, s), BF16)], name="merge_fwd",
                          compiler_params=_params(("parallel", "parallel")))(oa, proj, ob, proj, proj, proj, wa, wb)


def _merge_bwd(oa, ob, proj, wa, wb, dmixed, ts):
    s = oa.shape[0]

    def body(oa_ref, z_ref, ob_ref, rg_ref, ga_ref, gb_ref, wa_ref, wb_ref, dm_ref,
             doa_ref, dob_ref, dgrp_ref, dwa_ref, dwb_ref):
        _, vjp = jax.vjp(_merge, oa_ref[...], z_ref[...], ob_ref[...], rg_ref[...], ga_ref[...], gb_ref[...],
                         wa_ref[...], wb_ref[...])
        doa, dz, dob, drg, dga, dgb, dwa, dwb = vjp(dm_ref[...].astype(F32))
        doa_ref[...] = doa
        dob_ref[...] = dob
        for k, d in enumerate((dz, drg, dga, dgb)):
            dgrp_ref[:, k * DV:(k + 1) * DV] = d.astype(BF16)
        first_tile = pl.program_id(1) == 0

        @pl.when(first_tile & (pl.program_id(0) == 0))
        def _():
            dwa_ref[...] = jnp.zeros_like(dwa_ref)

        @pl.when(first_tile)
        def _():
            dwb_ref[...] = jnp.zeros_like(dwb_ref)

        dwa_ref[...] += dwa
        dwb_ref[...] += dwb

    blk = pl.BlockSpec((ts, DV), lambda h, i: (i, h))
    out_specs = [blk, blk, pl.BlockSpec((ts, MERGE_W), lambda h, i: (i, O_MERGE // MERGE_W + h)),
                 pl.BlockSpec((1, DV), lambda h, i: (0, 0)), pl.BlockSpec((1, DV), lambda h, i: (0, h))]
    out_shape = [_sds((s, H * DV)), _sds((s, H * DV)), _sds((s, P_IN), BF16), _sds((1, DV)), _sds((1, H * DV))]
    return pl.pallas_call(body, grid=(H, s // ts), in_specs=_merge_specs(ts) + [blk], out_specs=out_specs, out_shape=out_shape,
                          name="merge_bwd", compiler_params=_params(("arbitrary", "arbitrary"), 40 * ts * DV * 4))(
                              oa, proj, ob, proj, proj, proj, wa, wb, dmixed)


def _act(hg, hu):
    return _silu(hg) * hu


def _ffn_gate_up(hn, w_gate, w_up, tm, tn):
    s, f = hn.shape[0], w_gate.shape[1]
    tm, tn = min(tm, s), min(tn, f)
    assert s % tm == 0 and f % tn == 0 and tm % 256 == 0
    sub = tm // 2

    def body(a_ref, wg_ref, wu_ref, hg_ref, hu_ref, act_ref, actt_ref):
        for r0 in range(0, tm, sub):
            rs = slice(r0, r0 + sub)
            hg = _bdot(a_ref[rs, :], wg_ref[...], NN)
            hu = _bdot(a_ref[rs, :], wu_ref[...], NN)
            y = _act(hg, hu).astype(BF16)
            hg_ref[rs, :] = hg.astype(BF16)
            hu_ref[rs, :] = hu.astype(BF16)
            act_ref[rs, :] = y
            actt_ref[:, rs] = y.T

    wsp = pl.BlockSpec((D, tn), lambda i, j: (0, j))
    blk = pl.BlockSpec((tm, tn), lambda i, j: (i, j))
    est = 2 * (tm * D * 2 + 2 * D * tn * 2 + 4 * tm * tn * 2) + 4 * sub * tn * 4
    return pl.pallas_call(body, grid=(s // tm, f // tn), in_specs=[pl.BlockSpec((tm, D), lambda i, j: (i, 0)), wsp, wsp],
                          out_specs=[blk, blk, blk, pl.BlockSpec((tn, tm), lambda i, j: (j, i))],
                          out_shape=[_sds((s, f), BF16)] * 3 + [_sds((f, s), BF16)], name="ffn_gate_up",
                          compiler_params=_params(("parallel", "parallel"), est))(hn, w_gate, w_up)


def _ffn_down_dx(dh2, w_down, hg, hu, tm, tn):
    s, f = hg.shape
    tm, tn = min(tm, s), min(tn, f)
    assert s % tm == 0 and f % tn == 0 and tm % 256 == 0
    sub = tm // 2

    def body(d_ref, w_ref, hg_ref, hu_ref, dhg_ref, dhu_ref):
        for r0 in range(0, tm, sub):
            rs = slice(r0, r0 + sub)
            dact = _bdot(d_ref[rs, :], w_ref[...], NT)
            _, vjp = jax.vjp(_act, hg_ref[rs, :].astype(F32), hu_ref[rs, :].astype(F32))
            dhg, dhu = vjp(dact)
            dhg_ref[rs, :] = dhg.astype(BF16)
            dhu_ref[rs, :] = dhu.astype(BF16)

    blk = pl.BlockSpec((tm, tn), lambda i, j: (i, j))
    est = 2 * (tm * D * 4 + tn * D * 2 + 4 * tm * tn * 2) + 6 * sub * tn * 4
    return pl.pallas_call(body, grid=(s // tm, f // tn),
                          in_specs=[pl.BlockSpec((tm, D), lambda i, j: (i, 0)), pl.BlockSpec((tn, D), lambda i, j: (j, 0)), blk, blk],
                          out_specs=[blk, blk], out_shape=[_sds((s, f), BF16)] * 2, name="ffn_down_dx",
                          compiler_params=_params(("parallel", "parallel"), est))(dh2, w_down, hg, hu)


def _loss_rows(h2, wf, tgt):
    err = _rms(h2, wf) - tgt
    return 0.5 * jnp.sum(jnp.mean(err * err, axis=-1, keepdims=True), keepdims=True)


def _loss_fwd_bwd(h2, wf, tgt, ts):
    s = h2.shape[0]

    def body(h_ref, w_ref, t_ref, loss_ref, dh_ref, dw_ref):
        loss, vjp = jax.vjp(_loss_rows, h_ref[...], w_ref[...], t_ref[...])
        dh, dw, _ = vjp(jnp.ones((1, 1), F32))
        dh_ref[...] = dh

        @pl.when(pl.program_id(0) == 0)
        def _():
            loss_ref[...] = jnp.zeros_like(loss_ref)
            dw_ref[...] = jnp.zeros_like(dw_ref)

        loss_ref[...] += loss
        dw_ref[...] += dw

    row = pl.BlockSpec((ts, D), lambda i: (i, 0))
    vec = pl.BlockSpec((1, D), lambda i: (0, 0))
    tile = pl.BlockSpec((8, 128), lambda i: (0, 0))
    return pl.pallas_call(body, grid=(s // ts,), in_specs=[row, vec, row], out_specs=[tile, row, vec],
                          out_shape=[_sds((8, 128)), _sds((s, D)), _sds((1, D))], name="final_norm_loss",
                          compiler_params=_params(("arbitrary",), 12 * ts * D * 4))(h2, wf, tgt)


def _rope_tables(s):
    inv = ROPE_BASE ** (-jnp.arange(0, DK, 2, dtype=F32) / DK)
    ang = jnp.arange(s, dtype=F32)[:, None] * inv[None, :]
    cos, sin = jnp.cos(ang), jnp.sin(ang)
    return jnp.concatenate([cos, cos], axis=1), jnp.concatenate([-sin, sin], axis=1)


def _local_step(x, tgt, w_in, w_out, w_gate, w_up, w_down, norm1_w, conv_w, a_log, dt_bias, gdn_norm_w, ret_norm_w, norm2_w, norm_f_w,
                dist=None):
    s = x.shape[0]
    ts = min(512, s)
    cs, sn = _rope_tables(s)
    lgtab = jnp.log1p(-jnp.exp2(-5.0 - jnp.arange(H, dtype=F32))).reshape(1, H)

    u, u_t = _rms_fwd(x, norm1_w, ts, "norm1_fwd")
    if dist is None:
        proj = _matmul(u, w_in, tm=1024, tn=1280, tk=D, name="in_proj")
    else:
        proj, gathered = _matmul(u, w_in, tm=1024, tn=1280, tk=D, side=_gather_side(dist["shards"]), name="in_proj")
        w_out, w_gate, w_up, w_down = (gathered[0].reshape(D, D), _from_slots_cols(gathered[1]), _from_slots_cols(gathered[2]),
                                       gathered[3].reshape(-1, D))
    qk, va, c_qk, c_v = _gdn_qkv_fwd(proj, conv_w, ts)
    beta, gc, gl = _bg_fwd(proj, a_log, dt_bias, ts)
    inter = _gdn_prep_fwd(qk, va, beta, gc, gl)
    oa, st_a = _gdn_scan_fwd(*inter)
    ob, st_b = _ret_scan_fwd(proj, cs, sn, lgtab)
    mixed, mixed_t = _merge_fwd(oa, ob, proj, gdn_norm_w, ret_norm_w, ts)
    h1 = _matmul(mixed, w_out, tm=1024, tn=1024, tk=D, res=x, name="out_proj")
    hn, hn_t = _rms_fwd(h1, norm2_w, ts, "norm2_fwd")
    hg, hu, act, act_t = _ffn_gate_up(hn, w_gate, w_up, 512, 1408)
    h2 = _matmul(act, w_down, tm=512, tn=1024, tk=5632, res=h1, name="ffn_down")
    loss, dh2, d_norm_f = _loss_fwd_bwd(h2, norm_f_w, tgt, ts)

    dhg, dhu = _ffn_down_dx(dh2, w_down, hg, hu, 512, 1408)
    g_down = _matmul(act_t, dh2, tm=1408, tn=1024, tk=D, out_dtype=BF16, name="ffn_down_dw")
    g_gate = _matmul(hn_t, dhg, tm=512, tn=512, tk=8192, out_dtype=BF16, name="ffn_gate_dw")
    g_up = _matmul(hn_t, dhu, tm=512, tn=512, tk=8192, out_dtype=BF16, name="ffn_up_dw")
    dhn = _matmul(dhg, w_gate, tb=True, tm=512, tn=1024, tk=5632, name="ffn_gate_dx")
    dhn = _matmul(dhu, w_up, tb=True, tm=512, tn=1024, tk=5632, res=dhn, name="ffn_up_dx")
    dh1, d_norm2 = _rms_bwd(h1, norm2_w, dhn, dh2, ts, "norm2_bwd")

    g_out = _matmul(mixed_t, dh1, tm=1024, tn=1024, tk=D, out_dtype=BF16, name="out_proj_dw")
    early = ["w_out", "w_gate", "w_up", "w_down"]
    if dist is None:
        dmixed = _matmul(dh1, w_out, tb=True, tm=1024, tn=1024, tk=D, out_dtype=BF16, name="out_proj_dx")
    else:
        slots = dict(w_out=g_out.reshape(NDEV, D // NDEV, D), w_gate=_to_slots_cols(g_gate), w_up=_to_slots_cols(g_up),
                     w_down=g_down.reshape(NDEV, -1, D))
        dmixed, from_sibling = _matmul(dh1, w_out, tb=True, tm=1024, tn=1024, tk=D, out_dtype=BF16,
                                       side=_sibling_side([slots[k] for k in early]), name="out_proj_dx")
        parts = [_add_sibling(slots[k], r, dist["core"], 128, "grads_add_" + k) for k, r in zip(early, from_sibling)]
    doa, dob, dproj, d_gdn_norm, d_ret_norm = _merge_bwd(oa, ob, proj, gdn_norm_w, ret_norm_w, dmixed, ts)

    dproj = _ret_scan_bwd(proj, cs, sn, lgtab, st_b, dob, dproj)
    d_inter = _gdn_scan_bwd(*inter, st_a, doa)
    dqk, dva, dbeta_h, dgc_h, dgl_h = _gdn_prep_bwd(qk, va, beta, gc, gl, *d_inter)
    dproj, d_conv = _gdn_qkv_bwd(proj, conv_w, c_qk, c_v, dqk, dva, dproj, ts)
    dproj, d_a_log, d_dt_bias = _bg_bwd(proj, a_log, dt_bias, dbeta_h, dgc_h, dgl_h, dproj, ts)

    if dist is None:
        g_in = _matmul(u_t, dproj, tm=1024, tn=1280, tk=D, out_dtype=BF16, name="in_proj_dw")
        du = _matmul(dproj, w_in, tb=True, tm=1024, tn=1024, tk=1664, name="in_proj_dx")
        big = dict(w_in=g_in, w_out=g_out, w_gate=g_gate, w_up=g_up, w_down=g_down)
    else:
        g_in, from_chips = _matmul(u_t, dproj, tm=1024, tn=1280, tk=D, out_dtype=BF16, side=_chips_side(parts), name="in_proj_dw")
        du, (from_all,) = _matmul(dproj, w_in, tb=True, tm=1024, tn=1024, tk=1664,
                                  side=_all_to_all_side(_windows_from_layout(g_in)), name="in_proj_dx")
        big = dict(w_in=from_all, **{k: (p, r) for k, p, r in zip(early, parts, from_chips)})
    dx, d_norm1 = _rms_bwd(x, norm1_w, du, dh1, ts, "norm1_bwd")

    small = dict(norm1_w=d_norm1, conv_w=d_conv, a_log=d_a_log, dt_bias=d_dt_bias, gdn_norm_w=d_gdn_norm,
                 ret_norm_w=d_ret_norm, norm2_w=d_norm2, norm_f_w=d_norm_f)
    return loss, dx, big, small


def _coords():
    return lax.axis_index("x"), lax.axis_index("y"), lax.axis_index("c")


def _gather_side(shards):
    n = len(shards)

    def plan(ins, outs, send_sems, recv_sems, local_sems):
        x, y, c = _coords()
        me, sibling = (x, y, c), (x, y, 1 - c)
        chips = [(1 - x, y), (x, 1 - y), (1 - x, 1 - y)]

        def copy(a, k, block, to, src=None):
            px, py, pc = block
            dst = outs[a].at[4 * px + 2 * py + pc]
            return pltpu.make_async_remote_copy(src_ref=dst if src is None else src, dst_ref=dst, send_sem=send_sems.at[a, k],
                                                recv_sem=recv_sems.at[a, k], device_id=to, device_id_type=MESH)

        mine = [pltpu.make_async_copy(ins[a], outs[a].at[4 * x + 2 * y + c], local_sems.at[a]) for a in range(n)]
        first = []
        for a in range(n):
            first.append(copy(a, 0, me, sibling, src=ins[a]))
            first += [copy(a, 1 + j, me, (*chip, c), src=ins[a]) for j, chip in enumerate(chips)]
        return c, me, sibling, chips, copy, mine, first

    def start(ins, outs, *sems):
        *_, mine, first = plan(ins, outs, *sems)
        for cp in mine + first:
            cp.start()

    def finish(ins, outs, *sems):
        c, me, sibling, chips, copy, mine, first = plan(ins, outs, *sems)
        passed = []
        for j, chip in enumerate(chips):
            for a in range(n):
                copy(a, 1 + j, (*chip, c), me).wait_recv()
                fwd = copy(a, 4 + j, (*chip, c), sibling)
                fwd.start()
                passed.append(fwd)
        for a in range(n):
            copy(a, 0, sibling, me).wait_recv()
            for j, chip in enumerate(chips):
                copy(a, 4 + j, (*chip, 1 - c), me).wait_recv()
        for cp in first + passed:
            cp.wait_send()
        for cp in mine:
            cp.wait()

    return _Side(shards, [_sds((NDEV,) + a.shape, a.dtype) for a in shards], [(n, 7), (n, 7), (n,)], start, finish)


def _exchange_side(ins, n_out, copies_of):
    def start(in_refs, out_refs, *sems):
        for cp in copies_of(in_refs, out_refs, *sems):
            cp.start()

    def finish(in_refs, out_refs, *sems):
        for cp in copies_of(in_refs, out_refs, *sems):
            cp.wait()

    n = len(ins)
    return _Side(ins, [_sds((n_out,) + a.shape[1:], a.dtype) for a in ins], [(n, n_out), (n, n_out)], start, finish)


def _sibling_side(slots):
    def copies_of(ins, outs, send_sems, recv_sems):
        x, y, c = _coords()
        return [pltpu.make_async_remote_copy(
            src_ref=ins[a].at[2 * j + (1 - c)], dst_ref=outs[a].at[j], send_sem=send_sems.at[a, j], recv_sem=recv_sems.at[a, j],
            device_id=(x, y, 1 - c), device_id_type=MESH) for a in range(len(slots)) for j in range(4)]

    return _exchange_side(slots, 4, copies_of)


def _chips_side(parts):
    def copies_of(ins, outs, send_sems, recv_sems):
        x, y, c = _coords()
        chips = [(1 - x, y), (x, 1 - y), (1 - x, 1 - y)]
        return [pltpu.make_async_remote_copy(
            src_ref=ins[a].at[2 * px + py], dst_ref=outs[a].at[k], send_sem=send_sems.at[a, k], recv_sem=recv_sems.at[a, k],
            device_id=(px, py, c), device_id_type=MESH) for a in range(len(parts)) for k, (px, py) in enumerate(chips)]

    return _exchange_side(parts, 3, copies_of)


def _all_to_all_side(slots):
    def plan(ins, outs, send_sems, recv_sems, local_sems):
        x, y, c = _coords()
        mine = 4 * x + 2 * y + c
        own = pltpu.make_async_copy(ins[0].at[mine], outs[0].at[mine], local_sems.at[0])
        remote = []
        for r in range(1, NDEV):
            peer = (x ^ (r >> 2), y ^ ((r >> 1) & 1), c ^ (r & 1))
            remote.append(pltpu.make_async_remote_copy(
                src_ref=ins[0].at[mine ^ r], dst_ref=outs[0].at[mine], send_sem=send_sems.at[r - 1], recv_sem=recv_sems.at[r - 1],
                device_id=peer, device_id_type=MESH))
        return own, remote

    def start(ins, outs, *sems):
        own, remote = plan(ins, outs, *sems)
        for cp in [own] + remote:
            cp.start()

    def finish(ins, outs, *sems):
        own, remote = plan(ins, outs, *sems)
        for cp in remote:
            cp.wait()
        own.wait()

    return _Side([slots], [_sds(slots.shape, slots.dtype)], [(NDEV - 1,), (NDEV - 1,), (1,)], start, finish)


def _allreduce_small(pack, name):
    rows, cols = pack.shape

    def body(in_ref, out_ref, buf_ref, send_sems, recv_sems):
        x, y, c = _coords()
        mine = 4 * x + 2 * y + c
        buf_ref[mine] = in_ref[...]
        copies = []
        for r in range(1, NDEV):
            peer = (x ^ (r >> 2), y ^ ((r >> 1) & 1), c ^ (r & 1))
            copies.append(pltpu.make_async_remote_copy(
                src_ref=in_ref, dst_ref=buf_ref.at[mine], send_sem=send_sems.at[r - 1], recv_sem=recv_sems.at[r - 1],
                device_id=peer, device_id_type=MESH))
        for cp in copies:
            cp.start()
        for r in range(1, NDEV):
            pltpu.make_async_remote_copy(
                src_ref=in_ref, dst_ref=buf_ref.at[mine ^ r], send_sem=send_sems.at[r - 1], recv_sem=recv_sems.at[r - 1],
                device_id=(x, y, c), device_id_type=MESH).wait_recv()
        for cp in copies:
            cp.wait_send()
        acc = buf_ref[0]
        for d in range(1, NDEV):
            acc = acc + buf_ref[d]
        out_ref[...] = acc

    return pl.pallas_call(
        body, in_specs=[VMEM_FULL], out_specs=VMEM_FULL, out_shape=_sds((rows, cols)),
        scratch_shapes=[pltpu.VMEM((NDEV, rows, cols), F32), pltpu.SemaphoreType.DMA((NDEV - 1,)), pltpu.SemaphoreType.DMA((NDEV - 1,))],
        name=name)(pack)


def _add_sibling(slots, recv, core, tr, name):
    _, rows, cols = slots.shape
    tr = _row_tile(rows, tr)

    def body(c_ref, a_ref, b_ref, o_ref):
        o_ref[...] = (a_ref[...].astype(F32) + b_ref[...].astype(F32)).astype(BF16)

    gs = pltpu.PrefetchScalarGridSpec(
        num_scalar_prefetch=1, grid=(4, rows // tr),
        in_specs=[pl.BlockSpec((None, tr, cols), lambda j, i, cr: (2 * j + cr[0], i, 0)),
                  pl.BlockSpec((None, tr, cols), lambda j, i, cr: (j, i, 0))],
        out_specs=pl.BlockSpec((None, tr, cols), lambda j, i, cr: (j, i, 0)))
    return pl.pallas_call(body, grid_spec=gs, out_shape=_sds((4, rows, cols), BF16), name=name,
                          compiler_params=_params(("parallel", "parallel"), 6 * tr * cols * 4))(core, slots, recv)


def _adam_math(w, g, m, v):
    m2 = B1 * m + (1.0 - B1) * g
    v2 = B2 * v + (1.0 - B2) * jnp.square(g)
    m_hat = m2 / (1.0 - B1 ** STEP)
    v_hat = v2 / (1.0 - B2 ** STEP)
    return -LR * (m_hat / (jnp.sqrt(v_hat) + EPS_ADAM) + WD * w), m2, v2


def _adamw_reduced(part, recv, chip, w, m, v, tr, name):
    rows, cols = w.shape
    tr = _row_tile(rows, tr)

    def body(j_ref, p_ref, r0_ref, r1_ref, r2_ref, w_ref, m_ref, v_ref, g_ref, d_ref, nm_ref, nv_ref):
        g = p_ref[...].astype(F32) + r0_ref[...].astype(F32) + r1_ref[...].astype(F32) + r2_ref[...].astype(F32)
        d, m2, v2 = _adam_math(w_ref[...], g, m_ref[...], v_ref[...])
        g_ref[...] = g
        d_ref[...] = d
        nm_ref[...] = m2
        nv_ref[...] = v2

    flat = pl.BlockSpec((tr, cols), lambda i, jr: (i, 0))
    gs = pltpu.PrefetchScalarGridSpec(
        num_scalar_prefetch=1, grid=(rows // tr,),
        in_specs=[pl.BlockSpec((None, tr, cols), lambda i, jr: (jr[0], i, 0))]
        + [pl.BlockSpec((None, tr, cols), functools.partial(lambda i, jr, k: (k, i, 0), k=k)) for k in range(3)] + [flat] * 3,
        out_specs=[flat] * 4)
    return pl.pallas_call(body, grid_spec=gs, out_shape=[_sds((rows, cols))] * 4, name=name,
                          compiler_params=_params(("parallel",), 22 * tr * cols * 4))(chip, part, recv, recv, recv, w, m, v)


def _sum_slots(recv, tr, name):
    _, rows, cols = recv.shape
    tr = _row_tile(rows, tr)

    def body(*refs):
        acc = refs[0][...].astype(F32)
        for p_ref in refs[1:NDEV]:
            acc = acc + p_ref[...].astype(F32)
        refs[NDEV][...] = acc

    slot = [pl.BlockSpec((None, tr, cols), functools.partial(lambda i, k: (k, i, 0), k=k)) for k in range(NDEV)]
    return pl.pallas_call(body, grid=(rows // tr,), in_specs=slot, out_specs=pl.BlockSpec((tr, cols), lambda i: (i, 0)),
                          out_shape=_sds((rows, cols)), name=name, compiler_params=_params(("parallel",)))(*([recv] * NDEV))


def _adamw_rows(w, g, m, v, tr, name):
    rows, cols = w.shape
    tr = _row_tile(rows, tr)

    def body(w_ref, g_ref, m_ref, v_ref, d_ref, nm_ref, nv_ref):
        d_ref[...], nm_ref[...], nv_ref[...] = _adam_math(w_ref[...], g_ref[...], m_ref[...], v_ref[...])

    flat = pl.BlockSpec((tr, cols), lambda i: (i, 0))
    return pl.pallas_call(body, grid=(rows // tr,), in_specs=[flat] * 4, out_specs=[flat] * 3, out_shape=[_sds((rows, cols))] * 3,
                          name=name, compiler_params=_params(("parallel",)))(w, g, m, v)


def _adamw_plain(w, g, m, v, name):
    def body(w_ref, g_ref, m_ref, v_ref, d_ref, nm_ref, nv_ref):
        d, m2, v2 = _adam_math(w_ref[...], g_ref[...], m_ref[...], v_ref[...])
        d_ref[...] = d
        nm_ref[...] = m2
        nv_ref[...] = v2

    return pl.pallas_call(body, out_shape=[_sds(w.shape)] * 3, name=name)(w, g, m, v)


def _pack_small(norm1_w, conv_w, a_log, dt_bias, gdn_norm_w, ret_norm_w, norm2_w, norm_f_w):
    misc = jnp.concatenate([gdn_norm_w.reshape(1, DV), a_log.reshape(1, H), dt_bias.reshape(1, H),
                            jnp.zeros((1, D - DV - 2 * H), F32)], axis=1)
    return jnp.concatenate([norm1_w.reshape(1, D), ret_norm_w.reshape(1, D), norm2_w.reshape(1, D), norm_f_w.reshape(1, D),
                            conv_w.reshape(8, D), misc, jnp.zeros((3, D), F32)], axis=0)


def _unpack_small(pack):
    return dict(norm1_w=pack[0:1], ret_norm_w=pack[1:2], norm2_w=pack[2:3], norm_f_w=pack[3], conv_w=pack[4:12].reshape(4, 2 * D),
                gdn_norm_w=pack[12:13, 0:DV], a_log=pack[12:13, DV:DV + H], dt_bias=pack[12:13, DV + H:DV + 2 * H])


IN_SPLITS = (4096, 2048, 8, 8, 1024, 1024, 2048, 2048, 2048, 2048)


BA_END = sum(IN_SPLITS[:4])
LANES = 128


def _padded_order_blocks():
    z0, ba0, rq0, rk0, rv0, rg0, ga0, gb0 = 4096, 6144, 6400, 7424, 8448, 10496, 12544, 14592
    cols = []
    for h in range(H):
        for base in (z0, rg0, ga0, gb0):
            cols += [base + DV * h, base + DV * h + LANES]
    cols += list(range(0, z0, LANES))
    for h in range(H):
        cols += [rq0 + DK * h, rk0 + DK * h, rv0 + DV * h, rv0 + DV * h + LANES]
    cols += [ba0, ba0 + LANES]
    blocks = np.asarray(cols, np.int32) // LANES
    assert sorted(blocks.tolist()) == list(range(P_IN // LANES))
    return blocks


def _permute_blocks(x, blocks, name):
    rows, cols = x.shape

    def body(p_ref, x_ref, o_ref):
        o_ref[...] = x_ref[...]

    gs = pltpu.PrefetchScalarGridSpec(num_scalar_prefetch=1, grid=(cols // LANES,),
                                      in_specs=[pl.BlockSpec((rows, LANES), lambda j, p: (0, p[j]))],
                                      out_specs=pl.BlockSpec((rows, LANES), lambda j, p: (0, j)))
    return pl.pallas_call(body, grid_spec=gs, out_shape=_sds((rows, cols), x.dtype), name=name,
                          compiler_params=_params(("parallel",)))(jnp.asarray(blocks), x)


def _regroup_w_in(w):
    padded = jnp.concatenate([w[:, :BA_END], jnp.zeros((w.shape[0], P_IN - N_IN), w.dtype), w[:, BA_END:]], axis=1)
    return _permute_blocks(padded, _padded_order_blocks(), "w_in_to_layout")


def _ungroup_w_in(g):
    padded = _permute_blocks(g, np.argsort(_padded_order_blocks()).astype(np.int32), "w_in_grad_from_layout")
    return jnp.concatenate([padded[:, :BA_END], padded[:, BA_END + P_IN - N_IN:]], axis=1)


SHARD_W = N_IN // NDEV
GAP = P_IN - N_IN
WIN = 2304


def _padded_col(c):
    return c + (GAP if c >= BA_END else 0)


WIN_START = [min(_padded_col(SHARD_W * d) // LANES * LANES, P_IN - WIN) for d in range(NDEV)]
WIN_OFF = [_padded_col(SHARD_W * d) - WIN_START[d] for d in range(NDEV)]
STRADDLER = BA_END // SHARD_W
STRADDLE_AT = BA_END - STRADDLER * SHARD_W
assert all(WIN_OFF[d] + SHARD_W + (GAP if d == STRADDLER else 0) <= WIN for d in range(NDEV))


def _win_off(me):
    off = jnp.int32(0)
    for d in range(NDEV):
        off = jnp.where(me == d, jnp.int32(WIN_OFF[d]), off)
    return off


def _window_of_shard(shard, me):
    rows = shard.shape[0]
    zeros = lambda n: jnp.zeros((rows, n), shard.dtype)
    plain = lax.dynamic_update_slice(zeros(WIN), shard, (0, _win_off(me)))
    o = WIN_OFF[STRADDLER]
    split = jnp.concatenate([zeros(o), shard[:, :STRADDLE_AT], zeros(GAP), shard[:, STRADDLE_AT:], zeros(WIN - o - GAP - SHARD_W)], axis=1)
    return jnp.where(me == STRADDLER, split, plain)


def _shard_of_window(win, me):
    plain = lax.dynamic_slice(win, (0, _win_off(me)), (win.shape[0], SHARD_W))
    o = WIN_OFF[STRADDLER]
    split = jnp.concatenate([win[:, o:o + STRADDLE_AT], win[:, o + STRADDLE_AT + GAP:o + GAP + SHARD_W]], axis=1)
    return jnp.where(me == STRADDLER, split, plain)


def _layout_from_windows(wins):
    _, rows, _ = wins.shape
    data = []
    for d in range(NDEV):
        lo = _padded_col(SHARD_W * d)
        data.append([(lo, lo + STRADDLE_AT), (lo + STRADDLE_AT + GAP, lo + GAP + SHARD_W)] if d == STRADDLER else [(lo, lo + SHARD_W)])
    zero_block = (0, WIN // LANES - 1)
    table = []
    for p in _padded_order_blocks():
        src = [(d, int(p) - WIN_START[d] // LANES) for d in range(NDEV)
               if any(lo < (p + 1) * LANES and hi > p * LANES for lo, hi in data[d])]
        assert len(src) <= 2 and all(0 <= b < WIN // LANES for _, b in src)
        src += [zero_block] * (2 - len(src))
        table.append([src[0][0], src[0][1], src[1][0], src[1][1]])
    table = np.asarray(table, np.int32).T.copy()

    def body(t_ref, a_ref, b_ref, o_ref):
        o_ref[...] = a_ref[...] + b_ref[...]

    gs = pltpu.PrefetchScalarGridSpec(
        num_scalar_prefetch=1, grid=(P_IN // LANES,),
        in_specs=[pl.BlockSpec((None, rows, LANES), lambda j, t: (t[0, j], 0, t[1, j])),
                  pl.BlockSpec((None, rows, LANES), lambda j, t: (t[2, j], 0, t[3, j]))],
        out_specs=pl.BlockSpec((rows, LANES), lambda j, t: (0, j)))
    return pl.pallas_call(body, grid_spec=gs, out_shape=_sds((rows, P_IN), wins.dtype), name="w_in_from_windows",
                          compiler_params=_params(("parallel",)))(jnp.asarray(table), wins, wins)


def _windows_from_layout(g):
    rows = g.shape[0]
    where = np.argsort(_padded_order_blocks())
    nb = WIN // LANES
    table = np.asarray([where[WIN_START[d] // LANES + b] for d in range(NDEV) for b in range(nb)], np.int32)

    def body(t_ref, x_ref, o_ref):
        o_ref[...] = x_ref[...]

    gs = pltpu.PrefetchScalarGridSpec(num_scalar_prefetch=1, grid=(NDEV, nb),
                                      in_specs=[pl.BlockSpec((rows, LANES), lambda d, b, t: (0, t[d * nb + b]))],
                                      out_specs=pl.BlockSpec((None, rows, LANES), lambda d, b, t: (d, 0, b)))
    return pl.pallas_call(body, grid_spec=gs, out_shape=_sds((NDEV, rows, WIN), g.dtype), name="w_in_grad_windows",
                          compiler_params=_params(("parallel", "parallel")))(jnp.asarray(table), g)


def _to_slots_cols(g):
    rows, cols = g.shape
    return g.reshape(rows, NDEV, cols // NDEV).transpose(1, 0, 2)


def _from_slots_cols(a):
    n, rows, cols = a.shape
    return a.transpose(1, 0, 2).reshape(rows, n * cols)


WEIGHT_ORDER = ["norm1_w", "w_in", "conv_w", "a_log", "dt_bias", "gdn_norm_w", "ret_norm_w", "w_out", "norm2_w", "w_gate", "w_up",
                "w_down", "norm_f_w"]


def kernel(x, norm1_w, w_in, conv_w, a_log, dt_bias, gdn_norm_w, ret_norm_w, w_out, norm2_w, w_gate, w_up, w_down, norm_f_w, loss_target, m_norm1_w, m_w_in, m_conv_w, m_a_log, m_dt_bias, m_gdn_norm_w, m_ret_norm_w, m_w_out, m_norm2_w, m_w_gate, m_w_up, m_w_down, m_norm_f_w, v_norm1_w, v_w_in, v_conv_w, v_a_log, v_dt_bias, v_gdn_norm_w, v_ret_norm_w, v_w_out, v_norm2_w, v_w_gate, v_w_up, v_w_down, v_norm_f_w):
    ax, ay, ac = _coords()
    me = 4 * ax + 2 * ay + ac
    core = jnp.reshape(ac, (1,)).astype(jnp.int32)
    chip = jnp.reshape(2 * ax + ay, (1,)).astype(jnp.int32)
    w = dict(norm1_w=norm1_w, w_in=w_in[0], conv_w=conv_w[0], a_log=a_log, dt_bias=dt_bias, gdn_norm_w=gdn_norm_w,
             ret_norm_w=ret_norm_w, w_out=w_out[0], norm2_w=norm2_w, w_gate=w_gate[0], w_up=w_up[0], w_down=w_down[0],
             norm_f_w=norm_f_w)
    m = dict(norm1_w=m_norm1_w, w_in=m_w_in[0], conv_w=m_conv_w[0], a_log=m_a_log, dt_bias=m_dt_bias, gdn_norm_w=m_gdn_norm_w,
             ret_norm_w=m_ret_norm_w, w_out=m_w_out[0], norm2_w=m_norm2_w, w_gate=m_w_gate[0], w_up=m_w_up[0], w_down=m_w_down[0],
             norm_f_w=m_norm_f_w)
    v = dict(norm1_w=v_norm1_w, w_in=v_w_in[0], conv_w=v_conv_w[0], a_log=v_a_log, dt_bias=v_dt_bias, gdn_norm_w=v_gdn_norm_w,
             ret_norm_w=v_ret_norm_w, w_out=v_w_out[0], norm2_w=v_norm2_w, w_gate=v_w_gate[0], w_up=v_w_up[0], w_down=v_w_down[0],
             norm_f_w=v_norm_f_w)
    big_names = ["w_in", "w_out", "w_gate", "w_up", "w_down"]

    w_in_wins, conv_all = _run_side(_gather_side([_window_of_shard(w["w_in"].astype(BF16), me), w["conv_w"]]), "w_in_allgather")
    w_in_full = _layout_from_windows(w_in_wins)
    conv_full = _from_slots_cols(conv_all)
    dist = dict(core=core, shards=[w[k].astype(BF16) for k in ("w_out", "w_gate", "w_up", "w_down")])

    loss_tile, dx, big, small = _local_step(
        x[0], loss_target[0], w_in_full, None, None, None, None, norm1_w, conv_full, a_log, dt_bias,
        gdn_norm_w, ret_norm_w, norm2_w, norm_f_w.reshape(1, D), dist=dist)
    loss = lax.psum(loss_tile[0, 0], ("x", "y", "c"))

    g_w_in = _shard_of_window(_sum_slots(big["w_in"], 64, "w_in_grad_sum"), me)
    out = {"w_in": (g_w_in, *_adamw_rows(w["w_in"], g_w_in, m["w_in"], v["w_in"], 64, "adamw_w_in"))}
    for k in ("w_out", "w_gate", "w_up", "w_down"):
        part, recv = big[k]
        out[k] = _adamw_reduced(part, recv, chip, w[k], m[k], v[k], 128, "adamw_" + k)

    g_small = _unpack_small(_allreduce_small(_pack_small(**small), "small_grads_allreduce"))
    g_small["conv_w"] = lax.dynamic_slice_in_dim(g_small["conv_w"], me * (2 * D // NDEV), 2 * D // NDEV, axis=1)
    small_names = [k for k in WEIGHT_ORDER if k not in big_names]
    pad_conv = lambda a: jnp.pad(a, ((0, 0), (0, 2 * D - a.shape[1])))
    packs = []
    for src in (w, g_small, m, v):
        args = {k: (pad_conv(src[k]) if k == "conv_w" else src[k]) for k in small_names}
        packs.append(_pack_small(**args))
    d_pack, m_pack, v_pack = _adamw_plain(*packs[0:1], packs[1], packs[2], packs[3], name="adamw_small")
    cut_conv = lambda dct: {**dct, "conv_w": dct["conv_w"][:, :2 * D // NDEV]}
    d_small, m_small, v_small = (cut_conv(_unpack_small(p)) for p in (d_pack, m_pack, v_pack))

    def shaped(k, a):
        return a.reshape(w_shapes[k])

    w_shapes = dict(norm1_w=norm1_w.shape, w_in=w_in.shape, conv_w=conv_w.shape, a_log=a_log.shape, dt_bias=dt_bias.shape,
                    gdn_norm_w=gdn_norm_w.shape, ret_norm_w=ret_norm_w.shape, w_out=w_out.shape, norm2_w=norm2_w.shape,
                    w_gate=w_gate.shape, w_up=w_up.shape, w_down=w_down.shape, norm_f_w=norm_f_w.shape)
    grads, deltas, new_m, new_v = [], [], [], []
    for k in WEIGHT_ORDER:
        if k in big_names:
            g_, d_, m_, v_ = out[k]
        else:
            g_, d_, m_, v_ = g_small[k], d_small[k], m_small[k], v_small[k]
        grads.append(shaped(k, g_))
        deltas.append(shaped(k, d_))
        new_m.append(shaped(k, m_))
        new_v.append(shaped(k, v_))
    return (loss, dx[None], *grads, *deltas, *new_m, *new_v)
```

```python
import functools
import numpy as np
import jax
import jax.numpy as jnp
from jax import lax
from jax.experimental import pallas as pl
from jax.experimental.pallas import tpu as pltpu

F32, BF16 = jnp.float32, jnp.bfloat16
HI = lax.Precision.HIGHEST
MESH = pl.DeviceIdType.MESH
ANY = pl.BlockSpec(memory_space=pl.ANY)
VMEM_FULL = pl.BlockSpec(memory_space=pltpu.VMEM)

NDEV = 8
D = 2048
H = 8
DK = 128
DV = 256
C = 64
CPB = 4
EPS = 1e-6
ROPE_BASE = 10000.0
N_IN = 16400
O_MERGE, O_QKV, O_RET, O_BA, P_IN = 0, 8192, 12288, 16384, 16640
MERGE_W, RET_W = 4 * DV, 2 * DK + DV
LR, B1, B2, EPS_ADAM, WD, STEP = 0.001, 0.9, 0.999, 1e-08, 0.01, 10
VMEM_CAP = 60 * 1024 * 1024

NN = ((1,), (0,))
NT = ((1,), (1,))
TN = ((0,), (0,))


def _params(sem=None, est=None):
    kw = {}
    if sem is not None:
        kw["dimension_semantics"] = sem
    if est is not None:
        kw["vmem_limit_bytes"] = int(min(VMEM_CAP, max(32 * 1024 * 1024, est * 5 // 4 + (4 << 20))))
    return pltpu.CompilerParams(**kw)


def _sds(shape, dt=F32):
    return jax.ShapeDtypeStruct(tuple(shape), dt)


def _row_tile(rows, limit):
    return max(t for t in range(16, min(rows, limit) + 1, 16) if rows % t == 0)


def _bdot(a, b, dims):
    return lax.dot_general(a.astype(BF16), b.astype(BF16), (dims, ((), ())), preferred_element_type=F32)


def _hdot(a, b, dims):
    return lax.dot_general(a, b, (dims, ((), ())), precision=HI, preferred_element_type=F32)


_sigmoid = jax.nn.sigmoid


def _silu(x):
    return x * _sigmoid(x)


def _rms(x, w):
    return x * lax.rsqrt(jnp.mean(x * x, axis=-1, keepdims=True) + EPS) * w


class _Side:
    def __init__(self, ins, out_shapes, sems, start, finish):
        self.ins, self.out_shapes, self.sems, self.start, self.finish = list(ins), list(out_shapes), list(sems), start, finish


def _run_side(side, name):
    ni, no = len(side.ins), len(side.out_shapes)

    def body(*refs):
        ins, outs, sems = refs[:ni], refs[ni:ni + no], refs[ni + no:]
        side.start(ins, outs, *sems)
        side.finish(ins, outs, *sems)

    return pl.pallas_call(body, in_specs=[ANY] * ni, out_specs=[ANY] * no, out_shape=side.out_shapes,
                          scratch_shapes=[pltpu.SemaphoreType.DMA(s) for s in side.sems], name=name)(*side.ins)


def _matmul(a, b, *, ta=False, tb=False, tm, tn, tk, out_dtype=F32, res=None, side=None, name):
    m = a.shape[1] if ta else a.shape[0]
    k = a.shape[0] if ta else a.shape[1]
    n = b.shape[0] if tb else b.shape[1]
    assert k == (b.shape[1] if tb else b.shape[0])
    tm, tn, tk = min(tm, m), min(tn, n), min(tk, k)
    assert m % tm == 0 and n % tn == 0 and k % tk == 0, (name, m, n, k, tm, tn, tk)
    nk = k // tk
    dims = ((0 if ta else 1,), (1 if tb else 0,))
    has_res = res is not None
    n_in = 3 if has_res else 2
    n_side_in = len(side.ins) if side else 0
    n_side_out = len(side.out_shapes) if side else 0
    grid = (m // tm, n // tn, nk)

    def body(*refs):
        a_ref, b_ref = refs[0], refs[1]
        r_ref = refs[2] if has_res else None
        o_ref = refs[n_in + n_side_in]
        if side:
            side_ins = refs[n_in:n_in + n_side_in]
            side_outs = refs[n_in + n_side_in + 1:n_in + n_side_in + 1 + n_side_out]
            side_sems = refs[len(refs) - len(side.sems):]
            step = (pl.program_id(0) * grid[1] + pl.program_id(1)) * grid[2] + pl.program_id(2)

            @pl.when(step == 0)
            def _():
                side.start(side_ins, side_outs, *side_sems)

        def finish(acc):
            if has_res:
                acc = acc + r_ref[...].astype(F32)
            o_ref[...] = acc.astype(out_dtype)

        part = _bdot(a_ref[...], b_ref[...], dims)
        if nk == 1:
            finish(part)
        else:
            acc_ref = refs[n_in + n_side_in + 1 + n_side_out]
            kk = pl.program_id(2)

            @pl.when(kk == 0)
            def _():
                acc_ref[...] = part

            @pl.when(kk > 0)
            def _():
                acc_ref[...] += part

            @pl.when(kk == nk - 1)
            def _():
                finish(acc_ref[...])

        if side:
            @pl.when(step == grid[0] * grid[1] * grid[2] - 1)
            def _():
                side.finish(side_ins, side_outs, *side_sems)

    a_spec = pl.BlockSpec((tk, tm), lambda i, j, kk: (kk, i)) if ta else pl.BlockSpec((tm, tk), lambda i, j, kk: (i, kk))
    b_spec = pl.BlockSpec((tn, tk), lambda i, j, kk: (j, kk)) if tb else pl.BlockSpec((tk, tn), lambda i, j, kk: (kk, j))
    o_spec = pl.BlockSpec((tm, tn), lambda i, j, kk: (i, j))
    in_specs = [a_spec, b_spec] + ([o_spec] if has_res else []) + [ANY] * n_side_in
    est = 2 * (tm * tk * a.dtype.itemsize + tk * tn * b.dtype.itemsize + tm * tn * jnp.dtype(out_dtype).itemsize)
    est += 2 * tm * tn * 4 * (1 if has_res else 0) + (tm * tn * 4 if nk > 1 else 0) + 2 * tm * tn * 4
    args = (a, b) + ((res,) if has_res else ()) + (tuple(side.ins) if side else ())
    scratch = ([pltpu.VMEM((tm, tn), F32)] if nk > 1 else []) + ([pltpu.SemaphoreType.DMA(s) for s in side.sems] if side else [])
    sem = ("arbitrary",) * 3 if side else ("parallel", "parallel", "arbitrary")
    out = pl.pallas_call(
        body, grid=grid, in_specs=in_specs, out_specs=[o_spec] + [ANY] * n_side_out,
        out_shape=[_sds((m, n), out_dtype)] + (side.out_shapes if side else []),
        scratch_shapes=scratch, name=name, compiler_params=_params(sem, est))(*args)
    return (out[0], out[1:]) if side else out[0]


def _rms_fwd(x, w, ts, name):
    s = x.shape[0]

    def body(x_ref, w_ref, o_ref, ot_ref):
        y = _rms(x_ref[...], w_ref[...]).astype(BF16)
        o_ref[...] = y
        ot_ref[...] = y.T

    row = pl.BlockSpec((ts, D), lambda i: (i, 0))
    return pl.pallas_call(body, grid=(s // ts,), in_specs=[row, pl.BlockSpec((1, D), lambda i: (0, 0))],
                          out_specs=[row, pl.BlockSpec((D, ts), lambda i: (0, i))],
                          out_shape=[_sds((s, D), BF16), _sds((D, s), BF16)], name=name,
                          compiler_params=_params(("parallel",)))(x, w)


def _rms_bwd(x, w, du, dres, ts, name):
    s = x.shape[0]

    def body(x_ref, w_ref, du_ref, dres_ref, dx_ref, dw_ref):
        _, vjp = jax.vjp(_rms, x_ref[...], w_ref[...])
        dx, dw = vjp(du_ref[...].astype(F32))
        dx_ref[...] = dx + dres_ref[...]

        @pl.when(pl.program_id(0) == 0)
        def _():
            dw_ref[...] = jnp.zeros_like(dw_ref)

        dw_ref[...] += dw

    row = pl.BlockSpec((ts, D), lambda i: (i, 0))
    vec = pl.BlockSpec((1, D), lambda i: (0, 0))
    return pl.pallas_call(body, grid=(s // ts,), in_specs=[row, vec, row, row], out_specs=[row, vec],
                          out_shape=[_sds((s, D)), _sds((1, D))], name=name,
                          compiler_params=_params(("arbitrary",), 12 * ts * D * 4))(x, w, du, dres)


def _conv_taps(xx, w, base, ts):
    acc = xx[base:base + ts] * w[0:1, :]
    for j in range(1, 4):
        acc = acc + xx[base + j:base + j + ts] * w[j:j + 1, :]
    return acc


def _causal_conv(prev8, cur, w, first):
    xx = jnp.concatenate([jnp.where(first, 0.0, prev8), cur], axis=0)
    return _conv_taps(xx, w, 5, cur.shape[0])


def _qk_post(c, scale):
    s = _silu(c)
    return s * lax.rsqrt(jnp.sum(s * s, axis=-1, keepdims=True) + EPS) * scale


def _conv_specs(ts, cw, col0):
    pcol = O_QKV // cw + col0
    cur = pl.BlockSpec((ts, cw), lambda j, i: (i, pcol + j))
    prev = pl.BlockSpec((8, cw), lambda j, i: (jnp.maximum(i * (ts // 8) - 1, 0), pcol + j))
    wsp = pl.BlockSpec((4, cw), lambda j, i: (0, col0 + j))
    return cur, prev, wsp


def _gdn_qkv_fwd(proj, conv_w, ts):
    s = proj.shape[0]

    def qk_body(cur_ref, prev_ref, w_ref, o_ref, c_ref):
        c = _causal_conv(prev_ref[...], cur_ref[...], w_ref[...], pl.program_id(1) == 0)
        scale = jnp.where(pl.program_id(0) < H, DK ** -0.5, 1.0).astype(F32)
        c_ref[...] = c
        o_ref[...] = _qk_post(c, scale)

    tq = min(2 * ts, s)
    cur, prev, wsp = _conv_specs(tq, DK, 0)
    out = pl.BlockSpec((tq, DK), lambda j, i: (i, j))
    qk, c_qk = pl.pallas_call(qk_body, grid=(2 * H, s // tq), in_specs=[cur, prev, wsp], out_specs=[out, out],
                              out_shape=[_sds((s, 2 * H * DK))] * 2, name="gdn_qk_prep",
                              compiler_params=_params(("parallel", "parallel")))(proj, proj, conv_w)

    def v_body(cur_ref, prev_ref, w_ref, o_ref, c_ref):
        c = _causal_conv(prev_ref[...], cur_ref[...], w_ref[...], pl.program_id(1) == 0)
        c_ref[...] = c
        o_ref[...] = _silu(c)

    cw = 512
    cur, prev, wsp = _conv_specs(ts, cw, 2 * H * DK // cw)
    out = pl.BlockSpec((ts, cw), lambda j, i: (i, j))
    v, c_v = pl.pallas_call(v_body, grid=(H * DV // cw, s // ts), in_specs=[cur, prev, wsp], out_specs=[out, out],
                            out_shape=[_sds((s, H * DV))] * 2, name="gdn_v_prep",
                            compiler_params=_params(("parallel", "parallel")))(proj, proj, conv_w)
    return qk, v, c_qk, c_v


def _gdn_qkv_bwd(proj, conv_w, c_qk, c_v, dqk, dv, dproj, ts):
    s = proj.shape[0]
    nt = s // ts

    def qk_body(c_ref, d_ref, o_ref):
        scale = jnp.where(pl.program_id(0) < H, DK ** -0.5, 1.0).astype(F32)
        _, vjp = jax.vjp(lambda cc: _qk_post(cc, scale), c_ref[...])
        o_ref[...] = vjp(d_ref[...])[0]

    tq = min(2 * ts, s)
    blk = pl.BlockSpec((tq, DK), lambda j, i: (i, j))
    dc_qk = pl.pallas_call(qk_body, grid=(2 * H, s // tq), in_specs=[blk, blk], out_specs=blk, out_shape=_sds((s, 2 * H * DK)),
                           name="gdn_qk_prep_bwd", compiler_params=_params(("parallel", "parallel")))(c_qk, dqk)

    def v_body(c_ref, d_ref, o_ref):
        _, vjp = jax.vjp(_silu, c_ref[...])
        o_ref[...] = vjp(d_ref[...])[0]

    cw = 512
    blk = pl.BlockSpec((ts, cw), lambda j, i: (i, j))
    dc_v = pl.pallas_call(v_body, grid=(H * DV // cw, nt), in_specs=[blk, blk], out_specs=blk, out_shape=_sds((s, H * DV)),
                          name="gdn_v_prep_bwd", compiler_params=_params(("parallel", "parallel")))(c_v, dv)

    def conv_bwd(dc, dproj, col0, ncols, name):
        def body(x_ref, xprev_ref, w_ref, dc_ref, dcnext_ref, _, da_ref, dw_ref):
            i = pl.program_id(1)
            w = w_ref[...]
            dcur = dc_ref[...]
            dd = jnp.concatenate([dcur, jnp.where(i == nt - 1, 0.0, dcnext_ref[...])], axis=0)
            acc = dd[3:3 + ts] * w[0:1, :]
            for j in range(1, 4):
                acc = acc + dd[3 - j:3 - j + ts] * w[j:j + 1, :]
            da_ref[...] = acc.astype(BF16)
            xx = jnp.concatenate([jnp.where(i == 0, 0.0, xprev_ref[...]), x_ref[...]], axis=0)

            @pl.when(i == 0)
            def _():
                dw_ref[...] = jnp.zeros_like(dw_ref)

            for j in range(4):
                dw_ref[j:j + 1, :] += jnp.sum(dcur * xx[5 + j:5 + j + ts], axis=0, keepdims=True)

        cur, prev, wsp = _conv_specs(ts, cw, col0)
        dcur = pl.BlockSpec((ts, cw), lambda j, i: (i, j))
        dnext = pl.BlockSpec((8, cw), lambda j, i: (jnp.minimum((i + 1) * (ts // 8), s // 8 - 1), j))
        pcol = O_QKV // cw + col0
        return pl.pallas_call(body, grid=(ncols // cw, nt), in_specs=[cur, prev, wsp, dcur, dnext, ANY],
                              out_specs=[pl.BlockSpec((ts, cw), lambda j, i: (i, pcol + j)), pl.BlockSpec((4, cw), lambda j, i: (0, j))],
                              out_shape=[_sds(dproj.shape, BF16), _sds((4, ncols))], input_output_aliases={5: 0}, name=name,
                              compiler_params=_params(("parallel", "arbitrary")))(proj, proj, conv_w, dc, dc, dproj)

    dproj, dw_qk = conv_bwd(dc_qk, dproj, 0, 2 * H * DK, "conv_bwd_qk")
    dproj, dw_v = conv_bwd(dc_v, dproj, 2 * H * DK // cw, H * DV, "conv_bwd_v")
    return dproj, jnp.concatenate([dw_qk, dw_v], axis=1)


def _bg(b, a, alog, dtb):
    n = b.shape[0]
    g = -jnp.exp(alog) * jax.nn.softplus(a + dtb)
    row = lax.broadcasted_iota(jnp.int32, (n, n), 0)
    col = lax.broadcasted_iota(jnp.int32, (n, n), 1)
    shift = C.bit_length() - 1
    same = (row >> shift) == (col >> shift)
    return _sigmoid(b), _hdot((same & (row >= col)).astype(F32), g, NN), _hdot(same.astype(F32), g, NN)


def _bg_fwd(proj, alog, dtb, ts):
    s = proj.shape[0]

    def body(ba_ref, alog_ref, dtb_ref, beta_ref, gc_ref, gl_ref):
        beta_ref[...], gc_ref[...], gl_ref[...] = _bg(ba_ref[:, 0:H], ba_ref[:, H:2 * H], alog_ref[...], dtb_ref[...])

    small = pl.BlockSpec((1, H), lambda i: (0, 0))
    out = pl.BlockSpec((ts, H), lambda i: (i, 0))
    return pl.pallas_call(body, grid=(s // ts,), in_specs=[pl.BlockSpec((ts, 256), lambda i: (i, O_BA // 256)), small, small],
                          out_specs=[out] * 3, out_shape=[_sds((s, H))] * 3, name="gdn_bg_prep",
                          compiler_params=_params(("parallel",)))(proj, alog, dtb)


def _bg_bwd(proj, alog, dtb, dbeta_h, dgc_h, dgl_h, dproj, ts):
    s = proj.shape[0]

    def body(ba_ref, alog_ref, dtb_ref, dbeta_ref, dgc_ref, dgl_ref, _, dba_ref, dalog_ref, ddtb_ref):
        _, vjp = jax.vjp(_bg, ba_ref[:, 0:H], ba_ref[:, H:2 * H], alog_ref[...], dtb_ref[...])
        db, da, dalog, ddtb = vjp((jnp.sum(dbeta_ref[...], axis=0), jnp.sum(dgc_ref[...], axis=0), jnp.sum(dgl_ref[...], axis=0)))
        dba_ref[...] = jnp.zeros_like(dba_ref)
        dba_ref[:, 0:H] = db.astype(BF16)
        dba_ref[:, H:2 * H] = da.astype(BF16)

        @pl.when(pl.program_id(0) == 0)
        def _():
            dalog_ref[...] = jnp.zeros_like(dalog_ref)
            ddtb_ref[...] = jnp.zeros_like(ddtb_ref)

        dalog_ref[...] += dalog
        ddtb_ref[...] += ddtb

    small = pl.BlockSpec((1, H), lambda i: (0, 0))
    per_head = pl.BlockSpec((H, ts, H), lambda i: (0, i, 0))
    return pl.pallas_call(body, grid=(s // ts,),
                          in_specs=[pl.BlockSpec((ts, 256), lambda i: (i, O_BA // 256)), small, small, per_head, per_head, per_head, ANY],
                          out_specs=[pl.BlockSpec((ts, 256), lambda i: (i, O_BA // 256)), small, small],
                          out_shape=[_sds(dproj.shape, BF16), _sds((1, H)), _sds((1, H))], input_output_aliases={6: 0},
                          name="gdn_bg_prep_bwd", compiler_params=_params(("arbitrary",)))(proj, alog, dtb, dbeta_h, dgc_h, dgl_h, dproj)


BLK = 4 * C
NNB, NTB, TNB = ((2,), (1,)), ((2,), (2,)), ((1,), (1,))


def _bdot_b(a, b, dims):
    return lax.dot_general(a.astype(BF16), b.astype(BF16), (dims, ((0,), (0,))), preferred_element_type=F32)


@jax.custom_vjp
def _inv_unit_lower(a):
    n = a.shape[-1]
    row = lax.broadcasted_iota(jnp.int32, (n, n), 0)
    col = lax.broadcasted_iota(jnp.int32, (n, n), 1)
    x = jnp.where(row == col, 1.0, 0.0).astype(F32) - a
    p = _bdot_b(a, a, NNB)
    power = 2
    while True:
        x = x + _bdot_b(x, p, NNB)
        power *= 2
        if power >= C:
            return x
        p = _bdot_b(p, p, NNB)


def _inv_fwd(a):
    t = _inv_unit_lower(a)
    return t, t


def _inv_bwd(t, dt):
    return (-_bdot_b(_bdot_b(t, dt, TNB), t, NTB),)


_inv_unit_lower.defvjp(_inv_fwd, _inv_bwd)


def _gdn_prep(q, k, v, bfull, gcfull, glfull, hmask):
    nb, n = q.shape[0], q.shape[1]
    beta = jnp.sum(bfull * hmask, axis=-1, keepdims=True)
    gc = jnp.sum(gcfull * hmask, axis=-1, keepdims=True)
    gl = jnp.sum(glfull * hmask, axis=-1, keepdims=True)
    row = lax.broadcasted_iota(jnp.int32, (n, n), 0)
    col = lax.broadcasted_iota(jnp.int32, (n, n), 1)
    shift = C.bit_length() - 1
    same = (row >> shift) == (col >> shift)
    incl, strict = same & (row >= col), same & (row > col)
    g_i = gc * jnp.ones((1, 1, n), F32)
    decay = jnp.exp(jnp.where(incl, g_i - jnp.swapaxes(g_i, 1, 2), -jnp.inf))
    kb = k * beta
    a = jnp.where(strict, _bdot_b(kb, k, NTB) * decay, 0.0)
    tinv = _inv_unit_lower(a)
    u = _bdot_b(tinv, v * beta, NNB)
    w = _bdot_b(tinv, kb * jnp.exp(gc), NNB)
    attn = _bdot_b(q, k, NTB) * decay
    fold = ((lax.broadcasted_iota(jnp.int32, (n, C), 0) & (C - 1)) == lax.broadcasted_iota(jnp.int32, (n, C), 1)).astype(F32)
    attn_c = _bdot(attn.reshape(nb * n, n), fold, NN).reshape(nb, n, C)
    return u, w, attn_c, q * jnp.exp(gc), k * jnp.exp(gl - gc), jnp.exp(gl)


def _gdn_step(u, w, attn, qg, kd, egl, state):
    v_new = u - _bdot_b(w, state, NNB)
    o = _bdot_b(qg, state, NNB) + _bdot_b(attn, v_new, NNB)
    return o, state * egl + _bdot_b(kd, v_new, TNB)


def _heads(ref, rs, width):
    return jnp.stack([ref[rs, h * width:(h + 1) * width] for h in range(H)])


def _head_mask(h):
    return (lax.broadcasted_iota(jnp.int32, (1, H), 1) == h).astype(F32)


PREP_BLOCKS = 2


def _gdn_prep_specs(r):
    small = pl.BlockSpec((r, H), lambda h, c: (c, 0))
    return [pl.BlockSpec((r, DK), lambda h, c: (c, h)), pl.BlockSpec((r, DK), lambda h, c: (c, H + h)),
            pl.BlockSpec((r, DV), lambda h, c: (c, h)), small, small, small]


def _blocked(ref):
    x = ref[...]
    return x.reshape(PREP_BLOCKS, BLK, x.shape[-1])


def _gdn_inter_specs(r):
    col = pl.BlockSpec((r, DK), lambda h, c: (c, h))
    return [pl.BlockSpec((r, DV), lambda h, c: (c, h)), col, pl.BlockSpec((1, r, C), lambda h, c: (h, c, 0)), col, col,
            pl.BlockSpec((1, r // C, 8, 128), lambda h, c: (h, c, 0, 0))]


def _gdn_prep_fwd(qk, v, beta, gc, gl):
    s = qk.shape[0]
    r = PREP_BLOCKS * BLK

    def body(q_ref, k_ref, v_ref, b_ref, gc_ref, gl_ref, u_ref, w_ref, attn_ref, qg_ref, kd_ref, egl_ref):
        u, w, attn, qg, kd, egl = _gdn_prep(_blocked(q_ref), _blocked(k_ref), _blocked(v_ref), _blocked(b_ref), _blocked(gc_ref),
                                            _blocked(gl_ref), _head_mask(pl.program_id(0)))
        u_ref[...] = u.reshape(r, DV)
        w_ref[...] = w.reshape(r, DK).astype(BF16)
        attn_ref[0] = attn.reshape(r, C).astype(BF16)
        qg_ref[...] = qg.reshape(r, DK).astype(BF16)
        kd_ref[...] = kd.reshape(r, DK).astype(BF16)
        egl = egl.reshape(r, 1)
        for j in range(r // C):
            egl_ref[0, j] = egl[j * C:j * C + 1, :] * jnp.ones((8, 128), F32)

    out_shape = [_sds((s, H * DV)), _sds((s, H * DK), BF16), _sds((H, s, C), BF16), _sds((s, H * DK), BF16),
                 _sds((s, H * DK), BF16), _sds((H, s // C, 8, 128))]
    return pl.pallas_call(body, grid=(H, s // r), in_specs=_gdn_prep_specs(r), out_specs=_gdn_inter_specs(r), out_shape=out_shape,
                          name="gdn_prep_fwd", compiler_params=_params(("parallel", "parallel")))(qk, qk, v, beta, gc, gl)


def _gdn_prep_bwd(qk, v, beta, gc, gl, du, dw, dattn, dqg, dkd, degl):
    s = qk.shape[0]
    r = PREP_BLOCKS * BLK

    def body(q_ref, k_ref, v_ref, b_ref, gc_ref, gl_ref, du_ref, dw_ref, dattn_ref, dqg_ref, dkd_ref, degl_ref,
             dq_ref, dk_ref, dv_ref, db_ref, dgc_ref, dgl_ref):
        hmask = _head_mask(pl.program_id(0))
        _, vjp = jax.vjp(lambda q, k, v, b, gc, gl: _gdn_prep(q, k, v, b, gc, gl, hmask), _blocked(q_ref), _blocked(k_ref),
                         _blocked(v_ref), _blocked(b_ref), _blocked(gc_ref), _blocked(gl_ref))
        rowid = lax.broadcasted_iota(jnp.int32, (r, 1), 0)
        degl = jnp.zeros((r, 1), F32)
        for j in range(r // C):
            degl = jnp.where(rowid == j * C, degl_ref[0, j, 0:1, 0:1], degl)
        dq, dk, dv, db, dgc, dgl = vjp((_blocked(du_ref), _blocked(dw_ref), _blocked(dattn_ref.at[0]), _blocked(dqg_ref),
                                        _blocked(dkd_ref), degl.reshape(PREP_BLOCKS, BLK, 1)))
        dq_ref[...] = dq.reshape(r, DK)
        dk_ref[...] = dk.reshape(r, DK)
        dv_ref[...] = dv.reshape(r, DV)
        db_ref[0] = db.reshape(r, H)
        dgc_ref[0] = dgc.reshape(r, H)
        dgl_ref[0] = dgl.reshape(r, H)

    col = pl.BlockSpec((r, DK), lambda h, c: (c, h))
    piece = pl.BlockSpec((1, r, H), lambda h, c: (h, c, 0))
    dq, dk, dv, db, dgc, dgl = pl.pallas_call(
        body, grid=(H, s // r), in_specs=_gdn_prep_specs(r) + _gdn_inter_specs(r),
        out_specs=[col, col, pl.BlockSpec((r, DV), lambda h, c: (c, h)), piece, piece, piece],
        out_shape=[_sds((s, H * DK)), _sds((s, H * DK)), _sds((s, H * DV))] + [_sds((H, s, H))] * 3,
        name="gdn_prep_bwd", compiler_params=_params(("parallel", "parallel"), 24 << 20))(
            qk, qk, v, beta, gc, gl, du, dw, dattn, dqg, dkd, degl)
    return jnp.concatenate([dq, dk], axis=1), dv, db, dgc, dgl


def _gdn_scan_specs(r, order):
    wide = pl.BlockSpec((r, H * DK), lambda c: (order(c), 0))
    return [pl.BlockSpec((r, H * DV), lambda c: (order(c), 0)), wide, pl.BlockSpec((H, r, C), lambda c: (0, order(c), 0)), wide, wide,
            pl.BlockSpec((H, r // C, 8, 128), lambda c: (0, order(c), 0, 0))]


def _gdn_scan_fwd(u, w, attn, qg, kd, egl):
    s = u.shape[0]
    r = CPB * C
    nb = s // r

    def body(u_ref, w_ref, attn_ref, qg_ref, kd_ref, egl_ref, o_ref, st_ref, state_ref):
        @pl.when(pl.program_id(0) == 0)
        def _():
            state_ref[...] = jnp.zeros_like(state_ref)

        state = state_ref[...]
        for i in range(CPB):
            rs = slice(i * C, (i + 1) * C)
            st_ref[:, i] = state
            o, state = _gdn_step(_heads(u_ref, rs, DV), _heads(w_ref, rs, DK), attn_ref[:, rs, :], _heads(qg_ref, rs, DK),
                                 _heads(kd_ref, rs, DK), egl_ref[:, i, 0:1, 0:1], state)
            for h in range(H):
                o_ref[rs, h * DV:(h + 1) * DV] = o[h]
        state_ref[...] = state

    out_specs = [pl.BlockSpec((r, H * DV), lambda c: (c, 0)), pl.BlockSpec((H, CPB, DK, DV), lambda c: (0, c, 0, 0))]
    return pl.pallas_call(body, grid=(nb,), in_specs=_gdn_scan_specs(r, lambda c: c), out_specs=out_specs,
                          out_shape=[_sds((s, H * DV)), _sds((H, s // C, DK, DV))],
                          scratch_shapes=[pltpu.VMEM((H, DK, DV), F32)], name="gdn_scan_fwd",
                          compiler_params=_params(("arbitrary",), 24 << 20))(u, w, attn, qg, kd, egl)


def _gdn_scan_bwd(u, w, attn, qg, kd, egl, states, do):
    s = u.shape[0]
    r = CPB * C
    nb = s // r

    def body(u_ref, w_ref, attn_ref, qg_ref, kd_ref, egl_ref, st_ref, do_ref,
             du_ref, dw_ref, dattn_ref, dqg_ref, dkd_ref, degl_ref, dstate_ref):
        @pl.when(pl.program_id(0) == 0)
        def _():
            dstate_ref[...] = jnp.zeros_like(dstate_ref)

        dstate = dstate_ref[...]
        for i in reversed(range(CPB)):
            rs = slice(i * C, (i + 1) * C)
            _, vjp = jax.vjp(_gdn_step, _heads(u_ref, rs, DV), _heads(w_ref, rs, DK).astype(F32), attn_ref[:, rs, :].astype(F32),
                             _heads(qg_ref, rs, DK).astype(F32), _heads(kd_ref, rs, DK).astype(F32), egl_ref[:, i, 0:1, 0:1],
                             st_ref[:, i])
            du, dw, dattn, dqg, dkd, degl, dstate = vjp((_heads(do_ref, rs, DV), dstate))
            dattn_ref[:, rs, :] = dattn
            degl_ref[:, i] = degl * jnp.ones((1, 8, 128), F32)
            for h in range(H):
                du_ref[rs, h * DV:(h + 1) * DV] = du[h]
                dw_ref[rs, h * DK:(h + 1) * DK] = dw[h]
                dqg_ref[rs, h * DK:(h + 1) * DK] = dqg[h]
                dkd_ref[rs, h * DK:(h + 1) * DK] = dkd[h]
        dstate_ref[...] = dstate

    rev = lambda c: nb - 1 - c
    in_specs = _gdn_scan_specs(r, rev) + [pl.BlockSpec((H, CPB, DK, DV), lambda c: (0, rev(c), 0, 0)),
                                          pl.BlockSpec((r, H * DV), lambda c: (rev(c), 0))]
    return pl.pallas_call(
        body, grid=(nb,), in_specs=in_specs, out_specs=_gdn_scan_specs(r, rev),
        out_shape=[_sds((s, H * DV)), _sds((s, H * DK)), _sds((H, s, C)), _sds((s, H * DK)), _sds((s, H * DK)),
                   _sds((H, s // C, 8, 128))],
        scratch_shapes=[pltpu.VMEM((H, DK, DV), F32)], name="gdn_scan_bwd",
        compiler_params=_params(("arbitrary",), 40 << 20))(u, w, attn, qg, kd, egl, states, do)


def _rot(x, cs, sn):
    return x * cs + pltpu.roll(x, DK // 2, 1) * sn


def _rot_t(d, cs, sn):
    return d * cs - pltpu.roll(d, DK // 2, 1) * sn


def _ret_chunk(q, k, v, state, lg):
    n = q.shape[1]
    row = lax.broadcasted_iota(jnp.int32, (n, n), 0)
    col = lax.broadcasted_iota(jnp.int32, (n, n), 1)
    dist = (row - col).astype(F32)
    dmat = jnp.exp(jnp.where(dist >= 0, dist * lg, -jnp.inf))
    scores = _bdot_b(q, k, NTB) * dmat
    pos = lax.broadcasted_iota(jnp.int32, (n, 1), 0).astype(F32)
    xi = jnp.exp((pos + 1.0) * lg)
    zeta = jnp.exp((n - 1.0 - pos) * lg)
    o = _bdot_b(scores, v, NNB) + _bdot_b(q, state, NNB) * xi
    new_state = state * jnp.exp(n * lg) + _bdot_b(k * zeta, v, TNB)
    return o, new_state


def _ret_specs(r, order):
    return [pl.BlockSpec((r, H * RET_W), lambda c: (order(c), O_RET // (H * RET_W))), pl.BlockSpec((r, DK), lambda c: (order(c), 0)),
            pl.BlockSpec((r, DK), lambda c: (order(c), 0)), pl.BlockSpec((H, 1, 1), lambda c: (0, 0, 0))]


def _ret_qkv(x_ref, cs, sn):
    q = jnp.stack([_rot(x_ref[:, h * RET_W:h * RET_W + DK], cs, sn) for h in range(H)])
    k = jnp.stack([_rot(x_ref[:, h * RET_W + DK:h * RET_W + 2 * DK], cs, sn) for h in range(H)]) * DK ** -0.5
    v = jnp.stack([x_ref[:, h * RET_W + 2 * DK:(h + 1) * RET_W] for h in range(H)])
    return q, k, v


RET_C = 256


def _ret_scan_fwd(proj, cs, sn, lgtab):
    s = proj.shape[0]
    r = min(RET_C, s)
    nb = s // r

    def body(x_ref, cs_ref, sn_ref, lg_ref, o_ref, st_ref, state_ref):
        @pl.when(pl.program_id(0) == 0)
        def _():
            state_ref[...] = jnp.zeros_like(state_ref)

        state = state_ref[...]
        st_ref[:, 0] = state
        q, k, v = _ret_qkv(x_ref, cs_ref[...], sn_ref[...])
        o, state_ref[...] = _ret_chunk(q, k, v, state, lg_ref[...])
        for h in range(H):
            o_ref[:, h * DV:(h + 1) * DV] = o[h]

    out_specs = [pl.BlockSpec((r, H * DV), lambda c: (c, 0)), pl.BlockSpec((H, 1, DK, DV), lambda c: (0, c, 0, 0))]
    return pl.pallas_call(body, grid=(nb,), in_specs=_ret_specs(r, lambda c: c), out_specs=out_specs,
                          out_shape=[_sds((s, H * DV)), _sds((H, nb, DK, DV))],
                          scratch_shapes=[pltpu.VMEM((H, DK, DV), F32)], name="ret_scan_fwd",
                          compiler_params=_params(("arbitrary",), 32 << 20))(proj, cs, sn, lgtab.reshape(H, 1, 1))


def _ret_scan_bwd(proj, cs, sn, lgtab, states, do, dproj):
    s = proj.shape[0]
    r = min(RET_C, s)
    nb = s // r

    def body(x_ref, cs_ref, sn_ref, lg_ref, st_ref, do_ref, _, d_ref, dstate_ref):
        @pl.when(pl.program_id(0) == 0)
        def _():
            dstate_ref[...] = jnp.zeros_like(dstate_ref)

        cs_, sn_ = cs_ref[...], sn_ref[...]
        lg = lg_ref[...]
        q, k, v = _ret_qkv(x_ref, cs_, sn_)
        _, vjp = jax.vjp(lambda q, k, v, st: _ret_chunk(q, k, v, st, lg), q, k, v, st_ref[:, 0])
        dq, dk, dv, dstate_ref[...] = vjp((_heads(do_ref, slice(None), DV), dstate_ref[...]))
        for h in range(H):
            d_ref[:, h * RET_W:h * RET_W + DK] = _rot_t(dq[h], cs_, sn_).astype(BF16)
            d_ref[:, h * RET_W + DK:h * RET_W + 2 * DK] = _rot_t(dk[h] * DK ** -0.5, cs_, sn_).astype(BF16)
            d_ref[:, h * RET_W + 2 * DK:(h + 1) * RET_W] = dv[h].astype(BF16)

    rev = lambda c: nb - 1 - c
    in_specs = _ret_specs(r, rev) + [pl.BlockSpec((H, 1, DK, DV), lambda c: (0, rev(c), 0, 0)),
                                     pl.BlockSpec((r, H * DV), lambda c: (rev(c), 0)), ANY]
    return pl.pallas_call(
        body, grid=(nb,), in_specs=in_specs, out_specs=pl.BlockSpec((r, H * RET_W), lambda c: (rev(c), O_RET // (H * RET_W))),
        out_shape=_sds(dproj.shape, BF16), input_output_aliases={6: 0},
        scratch_shapes=[pltpu.VMEM((H, DK, DV), F32)], name="ret_scan_bwd",
        compiler_params=_params(("arbitrary",), 48 << 20))(proj, cs, sn, lgtab.reshape(H, 1, 1), states, do, dproj)


def _merge(oa, z, ob, rg, ga, gb, wa, wb):
    ya = oa * lax.rsqrt(jnp.mean(oa * oa, axis=-1, keepdims=True) + EPS) * wa * _silu(z)
    mu = jnp.mean(ob, axis=-1, keepdims=True)
    var = jnp.mean(jnp.square(ob - mu), axis=-1, keepdims=True)
    yb = (ob - mu) * lax.rsqrt(var + EPS) * wb * _silu(rg)
    return _sigmoid(ga) * ya + _sigmoid(gb) * yb


def _merge_specs(ts):
    own = pl.BlockSpec((ts, DV), lambda h, i: (i, h))
    grp = lambda k: pl.BlockSpec((ts, DV), lambda h, i: (i, O_MERGE // DV + 4 * h + k))
    return [own, grp(0), own, grp(1), grp(2), grp(3),
            pl.BlockSpec((1, DV), lambda h, i: (0, 0)), pl.BlockSpec((1, DV), lambda h, i: (0, h))]


def _merge_fwd(oa, ob, proj, wa, wb, ts):
    s = oa.shape[0]

    def body(oa_ref, z_ref, ob_ref, rg_ref, ga_ref, gb_ref, wa_ref, wb_ref, o_ref, ot_ref):
        y = _merge(oa_ref[...], z_ref[...], ob_ref[...], rg_ref[...], ga_ref[...], gb_ref[...],
                   wa_ref[...], wb_ref[...]).astype(BF16)
        o_ref[...] = y
        ot_ref[...] = y.T

    return pl.pallas_call(body, grid=(H, s // ts), in_specs=_merge_specs(ts),
                          out_specs=[pl.BlockSpec((ts, DV), lambda h, i: (i, h)), pl.BlockSpec((DV, ts), lambda h, i: (h, i))],
                          out_shape=[_sds((s, H * DV), BF16), _sds((H * DV, s), BF16)], name="merge_fwd",
                          compiler_params=_params(("parallel", "parallel")))(oa, proj, ob, proj, proj, proj, wa, wb)


def _merge_bwd(oa, ob, proj, wa, wb, dmixed, ts):
    s = oa.shape[0]

    def body(oa_ref, z_ref, ob_ref, rg_ref, ga_ref, gb_ref, wa_ref, wb_ref, dm_ref,
             doa_ref, dob_ref, dgrp_ref, dwa_ref, dwb_ref):
        _, vjp = jax.vjp(_merge, oa_ref[...], z_ref[...], ob_ref[...], rg_ref[...], ga_ref[...], gb_ref[...],
                         wa_ref[...], wb_ref[...])
        doa, dz, dob, drg, dga, dgb, dwa, dwb = vjp(dm_ref[...].astype(F32))
        doa_ref[...] = doa
        dob_ref[...] = dob
        for k, d in enumerate((dz, drg, dga, dgb)):
            dgrp_ref[:, k * DV:(k + 1) * DV] = d.astype(BF16)
        first_tile = pl.program_id(1) == 0

        @pl.when(first_tile & (pl.program_id(0) == 0))
        def _():
            dwa_ref[...] = jnp.zeros_like(dwa_ref)

        @pl.when(first_tile)
        def _():
            dwb_ref[...] = jnp.zeros_like(dwb_ref)

        dwa_ref[...] += dwa
        dwb_ref[...] += dwb

    blk = pl.BlockSpec((ts, DV), lambda h, i: (i, h))
    out_specs = [blk, blk, pl.BlockSpec((ts, MERGE_W), lambda h, i: (i, O_MERGE // MERGE_W + h)),
                 pl.BlockSpec((1, DV), lambda h, i: (0, 0)), pl.BlockSpec((1, DV), lambda h, i: (0, h))]
    out_shape = [_sds((s, H * DV)), _sds((s, H * DV)), _sds((s, P_IN), BF16), _sds((1, DV)), _sds((1, H * DV))]
    return pl.pallas_call(body, grid=(H, s // ts), in_specs=_merge_specs(ts) + [blk], out_specs=out_specs, out_shape=out_shape,
                          name="merge_bwd", compiler_params=_params(("arbitrary", "arbitrary"), 40 * ts * DV * 4))(
                              oa, proj, ob, proj, proj, proj, wa, wb, dmixed)


def _act(hg, hu):
    return _silu(hg) * hu


def _ffn_gate_up(hn, w_gate, w_up, tm, tn):
    s, f = hn.shape[0], w_gate.shape[1]
    tm, tn = min(tm, s), min(tn, f)
    assert s % tm == 0 and f % tn == 0 and tm % 256 == 0
    sub = tm // 2

    def body(a_ref, wg_ref, wu_ref, hg_ref, hu_ref, act_ref, actt_ref):
        for r0 in range(0, tm, sub):
            rs = slice(r0, r0 + sub)
            hg = _bdot(a_ref[rs, :], wg_ref[...], NN)
            hu = _bdot(a_ref[rs, :], wu_ref[...], NN)
            y = _act(hg, hu).astype(BF16)
            hg_ref[rs, :] = hg.astype(BF16)
            hu_ref[rs, :] = hu.astype(BF16)
            act_ref[rs, :] = y
            actt_ref[:, rs] = y.T

    wsp = pl.BlockSpec((D, tn), lambda i, j: (0, j))
    blk = pl.BlockSpec((tm, tn), lambda i, j: (i, j))
    est = 2 * (tm * D * 2 + 2 * D * tn * 2 + 4 * tm * tn * 2) + 4 * sub * tn * 4
    return pl.pallas_call(body, grid=(s // tm, f // tn), in_specs=[pl.BlockSpec((tm, D), lambda i, j: (i, 0)), wsp, wsp],
                          out_specs=[blk, blk, blk, pl.BlockSpec((tn, tm), lambda i, j: (j, i))],
                          out_shape=[_sds((s, f), BF16)] * 3 + [_sds((f, s), BF16)], name="ffn_gate_up",
                          compiler_params=_params(("parallel", "parallel"), est))(hn, w_gate, w_up)


def _ffn_down_dx(dh2, w_down, hg, hu, tm, tn):
    s, f = hg.shape
    tm, tn = min(tm, s), min(tn, f)
    assert s % tm == 0 and f % tn == 0 and tm % 256 == 0
    sub = tm // 2

    def body(d_ref, w_ref, hg_ref, hu_ref, dhg_ref, dhu_ref):
        for r0 in range(0, tm, sub):
            rs = slice(r0, r0 + sub)
            dact = _bdot(d_ref[rs, :], w_ref[...], NT)
            _, vjp = jax.vjp(_act, hg_ref[rs, :].astype(F32), hu_ref[rs, :].astype(F32))
            dhg, dhu = vjp(dact)
            dhg_ref[rs, :] = dhg.astype(BF16)
            dhu_ref[rs, :] = dhu.astype(BF16)

    blk = pl.BlockSpec((tm, tn), lambda i, j: (i, j))
    est = 2 * (tm * D * 4 + tn * D * 2 + 4 * tm * tn * 2) + 6 * sub * tn * 4
    return pl.pallas_call(body, grid=(s // tm, f // tn),
                          in_specs=[pl.BlockSpec((tm, D), lambda i, j: (i, 0)), pl.BlockSpec((tn, D), lambda i, j: (j, 0)), blk, blk],
                          out_specs=[blk, blk], out_shape=[_sds((s, f), BF16)] * 2, name="ffn_down_dx",
                          compiler_params=_params(("parallel", "parallel"), est))(dh2, w_down, hg, hu)


def _loss_rows(h2, wf, tgt):
    err = _rms(h2, wf) - tgt
    return 0.5 * jnp.sum(jnp.mean(err * err, axis=-1, keepdims=True), keepdims=True)


def _loss_fwd_bwd(h2, wf, tgt, ts):
    s = h2.shape[0]

    def body(h_ref, w_ref, t_ref, loss_ref, dh_ref, dhb_ref, dw_ref):
        loss, vjp = jax.vjp(_loss_rows, h_ref[...], w_ref[...], t_ref[...])
        dh, dw, _ = vjp(jnp.ones((1, 1), F32))
        dh_ref[...] = dh
        dhb_ref[...] = dh.astype(BF16)

        @pl.when(pl.program_id(0) == 0)
        def _():
            loss_ref[...] = jnp.zeros_like(loss_ref)
            dw_ref[...] = jnp.zeros_like(dw_ref)

        loss_ref[...] += loss
        dw_ref[...] += dw

    row = pl.BlockSpec((ts, D), lambda i: (i, 0))
    vec = pl.BlockSpec((1, D), lambda i: (0, 0))
    tile = pl.BlockSpec((8, 128), lambda i: (0, 0))
    return pl.pallas_call(body, grid=(s // ts,), in_specs=[row, vec, row], out_specs=[tile, row, row, vec],
                          out_shape=[_sds((8, 128)), _sds((s, D)), _sds((s, D), BF16), _sds((1, D))], name="final_norm_loss",
                          compiler_params=_params(("arbitrary",), 12 * ts * D * 4))(h2, wf, tgt)


def _rope_tables(s):
    inv = ROPE_BASE ** (-jnp.arange(0, DK, 2, dtype=F32) / DK)
    ang = jnp.arange(s, dtype=F32)[:, None] * inv[None, :]
    cos, sin = jnp.cos(ang), jnp.sin(ang)
    return jnp.concatenate([cos, cos], axis=1), jnp.concatenate([-sin, sin], axis=1)


def _local_step(x, tgt, w_in, w_out, w_gate, w_up, w_down, norm1_w, conv_w, a_log, dt_bias, gdn_norm_w, ret_norm_w, norm2_w, norm_f_w,
                dist=None):
    s = x.shape[0]
    ts = min(512, s)
    cs, sn = _rope_tables(s)
    lgtab = jnp.log1p(-jnp.exp2(-5.0 - jnp.arange(H, dtype=F32))).reshape(1, H)

    u, u_t = _rms_fwd(x, norm1_w, ts, "norm1_fwd")
    if dist is None:
        proj = _matmul(u, w_in, tm=1024, tn=1280, tk=D, name="in_proj")
    else:
        proj, gathered = _matmul(u, w_in, tm=1024, tn=1280, tk=D, side=_gather_side(dist["shards"]), name="in_proj")
        w_out, w_gate, w_up, w_down = (gathered[0].reshape(D, D), _from_slots_cols(gathered[1]), _from_slots_cols(gathered[2]),
                                       gathered[3].reshape(-1, D))
    qk, va, c_qk, c_v = _gdn_qkv_fwd(proj, conv_w, ts)
    beta, gc, gl = _bg_fwd(proj, a_log, dt_bias, ts)
    inter = _gdn_prep_fwd(qk, va, beta, gc, gl)
    oa, st_a = _gdn_scan_fwd(*inter)
    ob, st_b = _ret_scan_fwd(proj, cs, sn, lgtab)
    mixed, mixed_t = _merge_fwd(oa, ob, proj, gdn_norm_w, ret_norm_w, min(2 * ts, s))
    h1 = _matmul(mixed, w_out, tm=1024, tn=1024, tk=D, res=x, name="out_proj")
    hn, hn_t = _rms_fwd(h1, norm2_w, ts, "norm2_fwd")
    hg, hu, act, act_t = _ffn_gate_up(hn, w_gate, w_up, 512, 1408)
    h2 = _matmul(act, w_down, tm=512, tn=1024, tk=5632, res=h1, name="ffn_down")
    loss, dh2, dh2_b, d_norm_f = _loss_fwd_bwd(h2, norm_f_w, tgt, ts)

    dhg, dhu = _ffn_down_dx(dh2_b, w_down, hg, hu, 512, 1408)
    g_down = _matmul(act_t, dh2_b, tm=512, tn=512, tk=8192, out_dtype=BF16, name="ffn_down_dw")
    g_gate = _matmul(hn_t, dhg, tm=512, tn=512, tk=8192, out_dtype=BF16, name="ffn_gate_dw")
    g_up = _matmul(hn_t, dhu, tm=512, tn=512, tk=8192, out_dtype=BF16, name="ffn_up_dw")
    dhn = _matmul(dhg, w_gate, tb=True, tm=512, tn=1024, tk=5632, name="ffn_gate_dx")
    dhn = _matmul(dhu, w_up, tb=True, tm=512, tn=1024, tk=5632, res=dhn, name="ffn_up_dx")
    dh1, d_norm2 = _rms_bwd(h1, norm2_w, dhn, dh2, ts, "norm2_bwd")

    g_out = _matmul(mixed_t, dh1, tm=1024, tn=1024, tk=D, out_dtype=BF16, name="out_proj_dw")
    early = ["w_out", "w_gate", "w_up", "w_down"]
    if dist is None:
        dmixed = _matmul(dh1, w_out, tb=True, tm=1024, tn=1024, tk=D, out_dtype=BF16, name="out_proj_dx")
    else:
        slots = dict(w_out=g_out.reshape(NDEV, D // NDEV, D), w_gate=_to_slots_cols(g_gate), w_up=_to_slots_cols(g_up),
                     w_down=g_down.reshape(NDEV, -1, D))
        dmixed, from_sibling = _matmul(dh1, w_out, tb=True, tm=1024, tn=1024, tk=D, out_dtype=BF16,
                                       side=_sibling_side([slots[k] for k in early]), name="out_proj_dx")
        parts = [_add_sibling(slots[k], r, dist["core"], 128, "grads_add_" + k) for k, r in zip(early, from_sibling)]
    doa, dob, dproj, d_gdn_norm, d_ret_norm = _merge_bwd(oa, ob, proj, gdn_norm_w, ret_norm_w, dmixed, ts)

    dproj = _ret_scan_bwd(proj, cs, sn, lgtab, st_b, dob, dproj)
    d_inter = _gdn_scan_bwd(*inter, st_a, doa)
    dqk, dva, dbeta_h, dgc_h, dgl_h = _gdn_prep_bwd(qk, va, beta, gc, gl, *d_inter)
    dproj, d_conv = _gdn_qkv_bwd(proj, conv_w, c_qk, c_v, dqk, dva, dproj, ts)
    dproj, d_a_log, d_dt_bias = _bg_bwd(proj, a_log, dt_bias, dbeta_h, dgc_h, dgl_h, dproj, ts)

    if dist is None:
        g_in = _matmul(u_t, dproj, tm=1024, tn=1280, tk=D, out_dtype=BF16, name="in_proj_dw")
        du = _matmul(dproj, w_in, tb=True, tm=1024, tn=1024, tk=1664, name="in_proj_dx")
        big = dict(w_in=g_in, w_out=g_out, w_gate=g_gate, w_up=g_up, w_down=g_down)
    else:
        g_in, from_chips = _matmul(u_t, dproj, tm=1024, tn=1280, tk=D, out_dtype=BF16, side=_chips_side(parts), name="in_proj_dw")
        du, (from_all,) = _matmul(dproj, w_in, tb=True, tm=1024, tn=1024, tk=1664,
                                  side=_all_to_all_side(_windows_from_layout(g_in)), name="in_proj_dx")
        big = dict(w_in=from_all, **{k: (p, r) for k, p, r in zip(early, parts, from_chips)})
    dx, d_norm1 = _rms_bwd(x, norm1_w, du, dh1, ts, "norm1_bwd")

    small = dict(norm1_w=d_norm1, conv_w=d_conv, a_log=d_a_log, dt_bias=d_dt_bias, gdn_norm_w=d_gdn_norm,
                 ret_norm_w=d_ret_norm, norm2_w=d_norm2, norm_f_w=d_norm_f)
    return loss, dx, big, small


def _coords():
    return lax.axis_index("x"), lax.axis_index("y"), lax.axis_index("c")


def _gather_side(shards):
    n = len(shards)

    def plan(ins, outs, send_sems, recv_sems, local_sems):
        x, y, c = _coords()
        me, sibling = (x, y, c), (x, y, 1 - c)
        chips = [(1 - x, y), (x, 1 - y), (1 - x, 1 - y)]

        def copy(a, k, block, to, src=None):
            px, py, pc = block
            dst = outs[a].at[4 * px + 2 * py + pc]
            return pltpu.make_async_remote_copy(src_ref=dst if src is None else src, dst_ref=dst, send_sem=send_sems.at[a, k],
                                                recv_sem=recv_sems.at[a, k], device_id=to, device_id_type=MESH)

        mine = [pltpu.make_async_copy(ins[a], outs[a].at[4 * x + 2 * y + c], local_sems.at[a]) for a in range(n)]
        first = []
        for a in range(n):
            first.append(copy(a, 0, me, sibling, src=ins[a]))
            first += [copy(a, 1 + j, me, (*chip, c), src=ins[a]) for j, chip in enumerate(chips)]
        return c, me, sibling, chips, copy, mine, first

    def start(ins, outs, *sems):
        *_, mine, first = plan(ins, outs, *sems)
        for cp in mine + first:
            cp.start()

    def finish(ins, outs, *sems):
        c, me, sibling, chips, copy, mine, first = plan(ins, outs, *sems)
        passed = []
        for j, chip in enumerate(chips):
            for a in range(n):
                copy(a, 1 + j, (*chip, c), me).wait_recv()
                fwd = copy(a, 4 + j, (*chip, c), sibling)
                fwd.start()
                passed.append(fwd)
        for a in range(n):
            copy(a, 0, sibling, me).wait_recv()
            for j, chip in enumerate(chips):
                copy(a, 4 + j, (*chip, 1 - c), me).wait_recv()
        for cp in first + passed:
            cp.wait_send()
        for cp in mine:
            cp.wait()

    return _Side(shards, [_sds((NDEV,) + a.shape, a.dtype) for a in shards], [(n, 7), (n, 7), (n,)], start, finish)


def _exchange_side(ins, n_out, copies_of):
    def start(in_refs, out_refs, *sems):
        for cp in copies_of(in_refs, out_refs, *sems):
            cp.start()

    def finish(in_refs, out_refs, *sems):
        for cp in copies_of(in_refs, out_refs, *sems):
            cp.wait()

    n = len(ins)
    return _Side(ins, [_sds((n_out,) + a.shape[1:], a.dtype) for a in ins], [(n, n_out), (n, n_out)], start, finish)


def _sibling_side(slots):
    def copies_of(ins, outs, send_sems, recv_sems):
        x, y, c = _coords()
        return [pltpu.make_async_remote_copy(
            src_ref=ins[a].at[2 * j + (1 - c)], dst_ref=outs[a].at[j], send_sem=send_sems.at[a, j], recv_sem=recv_sems.at[a, j],
            device_id=(x, y, 1 - c), device_id_type=MESH) for a in range(len(slots)) for j in range(4)]

    return _exchange_side(slots, 4, copies_of)


def _chips_side(parts):
    def copies_of(ins, outs, send_sems, recv_sems):
        x, y, c = _coords()
        chips = [(1 - x, y), (x, 1 - y), (1 - x, 1 - y)]
        return [pltpu.make_async_remote_copy(
            src_ref=ins[a].at[2 * px + py], dst_ref=outs[a].at[k], send_sem=send_sems.at[a, k], recv_sem=recv_sems.at[a, k],
            device_id=(px, py, c), device_id_type=MESH) for a in range(len(parts)) for k, (px, py) in enumerate(chips)]

    return _exchange_side(parts, 3, copies_of)


def _all_to_all_side(slots):
    def plan(ins, outs, send_sems, recv_sems, local_sems):
        x, y, c = _coords()
        mine = 4 * x + 2 * y + c
        own = pltpu.make_async_copy(ins[0].at[mine], outs[0].at[mine], local_sems.at[0])
        remote = []
        for r in range(1, NDEV):
            peer = (x ^ (r >> 2), y ^ ((r >> 1) & 1), c ^ (r & 1))
            remote.append(pltpu.make_async_remote_copy(
                src_ref=ins[0].at[mine ^ r], dst_ref=outs[0].at[mine], send_sem=send_sems.at[r - 1], recv_sem=recv_sems.at[r - 1],
                device_id=peer, device_id_type=MESH))
        return own, remote

    def start(ins, outs, *sems):
        own, remote = plan(ins, outs, *sems)
        for cp in [own] + remote:
            cp.start()

    def finish(ins, outs, *sems):
        own, remote = plan(ins, outs, *sems)
        for cp in remote:
            cp.wait()
        own.wait()

    return _Side([slots], [_sds(slots.shape, slots.dtype)], [(NDEV - 1,), (NDEV - 1,), (1,)], start, finish)


def _allreduce_small(pack, name):
    rows, cols = pack.shape

    def body(in_ref, out_ref, buf_ref, send_sems, recv_sems):
        x, y, c = _coords()
        mine = 4 * x + 2 * y + c
        buf_ref[mine] = in_ref[...]
        copies = []
        for r in range(1, NDEV):
            peer = (x ^ (r >> 2), y ^ ((r >> 1) & 1), c ^ (r & 1))
            copies.append(pltpu.make_async_remote_copy(
                src_ref=in_ref, dst_ref=buf_ref.at[mine], send_sem=send_sems.at[r - 1], recv_sem=recv_sems.at[r - 1],
                device_id=peer, device_id_type=MESH))
        for cp in copies:
            cp.start()
        for r in range(1, NDEV):
            pltpu.make_async_remote_copy(
                src_ref=in_ref, dst_ref=buf_ref.at[mine ^ r], send_sem=send_sems.at[r - 1], recv_sem=recv_sems.at[r - 1],
                device_id=(x, y, c), device_id_type=MESH).wait_recv()
        for cp in copies:
            cp.wait_send()
        acc = buf_ref[0]
        for d in range(1, NDEV):
            acc = acc + buf_ref[d]
        out_ref[...] = acc

    return pl.pallas_call(
        body, in_specs=[VMEM_FULL], out_specs=VMEM_FULL, out_shape=_sds((rows, cols)),
        scratch_shapes=[pltpu.VMEM((NDEV, rows, cols), F32), pltpu.SemaphoreType.DMA((NDEV - 1,)), pltpu.SemaphoreType.DMA((NDEV - 1,))],
        name=name)(pack)


def _add_sibling(slots, recv, core, tr, name):
    _, rows, cols = slots.shape
    tr = _row_tile(rows, tr)

    def body(c_ref, a_ref, b_ref, o_ref):
        o_ref[...] = (a_ref[...].astype(F32) + b_ref[...].astype(F32)).astype(BF16)

    gs = pltpu.PrefetchScalarGridSpec(
        num_scalar_prefetch=1, grid=(4, rows // tr),
        in_specs=[pl.BlockSpec((None, tr, cols), lambda j, i, cr: (2 * j + cr[0], i, 0)),
                  pl.BlockSpec((None, tr, cols), lambda j, i, cr: (j, i, 0))],
        out_specs=pl.BlockSpec((None, tr, cols), lambda j, i, cr: (j, i, 0)))
    return pl.pallas_call(body, grid_spec=gs, out_shape=_sds((4, rows, cols), BF16), name=name,
                          compiler_params=_params(("parallel", "parallel"), 6 * tr * cols * 4))(core, slots, recv)


def _adam_math(w, g, m, v):
    m2 = B1 * m + (1.0 - B1) * g
    v2 = B2 * v + (1.0 - B2) * jnp.square(g)
    m_hat = m2 / (1.0 - B1 ** STEP)
    v_hat = v2 / (1.0 - B2 ** STEP)
    return -LR * (m_hat / (jnp.sqrt(v_hat) + EPS_ADAM) + WD * w), m2, v2


def _adamw_reduced(part, recv, chip, w, m, v, tr, name):
    rows, cols = w.shape
    tr = _row_tile(rows, tr)

    def body(j_ref, p_ref, r0_ref, r1_ref, r2_ref, w_ref, m_ref, v_ref, g_ref, d_ref, nm_ref, nv_ref):
        g = p_ref[...].astype(F32) + r0_ref[...].astype(F32) + r1_ref[...].astype(F32) + r2_ref[...].astype(F32)
        d, m2, v2 = _adam_math(w_ref[...], g, m_ref[...], v_ref[...])
        g_ref[...] = g
        d_ref[...] = d
        nm_ref[...] = m2
        nv_ref[...] = v2

    flat = pl.BlockSpec((tr, cols), lambda i, jr: (i, 0))
    gs = pltpu.PrefetchScalarGridSpec(
        num_scalar_prefetch=1, grid=(rows // tr,),
        in_specs=[pl.BlockSpec((None, tr, cols), lambda i, jr: (jr[0], i, 0))]
        + [pl.BlockSpec((None, tr, cols), functools.partial(lambda i, jr, k: (k, i, 0), k=k)) for k in range(3)] + [flat] * 3,
        out_specs=[flat] * 4)
    return pl.pallas_call(body, grid_spec=gs, out_shape=[_sds((rows, cols))] * 4, name=name,
                          compiler_params=_params(("parallel",), 22 * tr * cols * 4))(chip, part, recv, recv, recv, w, m, v)


def _sum_slots(recv, tr, name):
    _, rows, cols = recv.shape
    tr = _row_tile(rows, tr)

    def body(*refs):
        acc = refs[0][...].astype(F32)
        for p_ref in refs[1:NDEV]:
            acc = acc + p_ref[...].astype(F32)
        refs[NDEV][...] = acc

    slot = [pl.BlockSpec((None, tr, cols), functools.partial(lambda i, k: (k, i, 0), k=k)) for k in range(NDEV)]
    return pl.pallas_call(body, grid=(rows // tr,), in_specs=slot, out_specs=pl.BlockSpec((tr, cols), lambda i: (i, 0)),
                          out_shape=_sds((rows, cols)), name=name, compiler_params=_params(("parallel",)))(*([recv] * NDEV))


def _adamw_rows(w, g, m, v, tr, name):
    rows, cols = w.shape
    tr = _row_tile(rows, tr)

    def body(w_ref, g_ref, m_ref, v_ref, d_ref, nm_ref, nv_ref):
        d_ref[...], nm_ref[...], nv_ref[...] = _adam_math(w_ref[...], g_ref[...], m_ref[...], v_ref[...])

    flat = pl.BlockSpec((tr, cols), lambda i: (i, 0))
    return pl.pallas_call(body, grid=(rows // tr,), in_specs=[flat] * 4, out_specs=[flat] * 3, out_shape=[_sds((rows, cols))] * 3,
                          name=name, compiler_params=_params(("parallel",)))(w, g, m, v)


def _adamw_plain(w, g, m, v, name):
    def body(w_ref, g_ref, m_ref, v_ref, d_ref, nm_ref, nv_ref):
        d, m2, v2 = _adam_math(w_ref[...], g_ref[...], m_ref[...], v_ref[...])
        d_ref[...] = d
        nm_ref[...] = m2
        nv_ref[...] = v2

    return pl.pallas_call(body, out_shape=[_sds(w.shape)] * 3, name=name)(w, g, m, v)


def _pack_small(norm1_w, conv_w, a_log, dt_bias, gdn_norm_w, ret_norm_w, norm2_w, norm_f_w):
    misc = jnp.concatenate([gdn_norm_w.reshape(1, DV), a_log.reshape(1, H), dt_bias.reshape(1, H),
                            jnp.zeros((1, D - DV - 2 * H), F32)], axis=1)
    return jnp.concatenate([norm1_w.reshape(1, D), ret_norm_w.reshape(1, D), norm2_w.reshape(1, D), norm_f_w.reshape(1, D),
                            conv_w.reshape(8, D), misc, jnp.zeros((3, D), F32)], axis=0)


def _unpack_small(pack):
    return dict(norm1_w=pack[0:1], ret_norm_w=pack[1:2], norm2_w=pack[2:3], norm_f_w=pack[3], conv_w=pack[4:12].reshape(4, 2 * D),
                gdn_norm_w=pack[12:13, 0:DV], a_log=pack[12:13, DV:DV + H], dt_bias=pack[12:13, DV + H:DV + 2 * H])


IN_SPLITS = (4096, 2048, 8, 8, 1024, 1024, 2048, 2048, 2048, 2048)


BA_END = sum(IN_SPLITS[:4])
LANES = 128


def _padded_order_blocks():
    z0, ba0, rq0, rk0, rv0, rg0, ga0, gb0 = 4096, 6144, 6400, 7424, 8448, 10496, 12544, 14592
    cols = []
    for h in range(H):
        for base in (z0, rg0, ga0, gb0):
            cols += [base + DV * h, base + DV * h + LANES]
    cols += list(range(0, z0, LANES))
    for h in range(H):
        cols += [rq0 + DK * h, rk0 + DK * h, rv0 + DV * h, rv0 + DV * h + LANES]
    cols += [ba0, ba0 + LANES]
    blocks = np.asarray(cols, np.int32) // LANES
    assert sorted(blocks.tolist()) == list(range(P_IN // LANES))
    return blocks


def _permute_blocks(x, blocks, name):
    rows, cols = x.shape

    def body(p_ref, x_ref, o_ref):
        o_ref[...] = x_ref[...]

    gs = pltpu.PrefetchScalarGridSpec(num_scalar_prefetch=1, grid=(cols // LANES,),
                                      in_specs=[pl.BlockSpec((rows, LANES), lambda j, p: (0, p[j]))],
                                      out_specs=pl.BlockSpec((rows, LANES), lambda j, p: (0, j)))
    return pl.pallas_call(body, grid_spec=gs, out_shape=_sds((rows, cols), x.dtype), name=name,
                          compiler_params=_params(("parallel",)))(jnp.asarray(blocks), x)


def _regroup_w_in(w):
    padded = jnp.concatenate([w[:, :BA_END], jnp.zeros((w.shape[0], P_IN - N_IN), w.dtype), w[:, BA_END:]], axis=1)
    return _permute_blocks(padded, _padded_order_blocks(), "w_in_to_layout")


def _ungroup_w_in(g):
    padded = _permute_blocks(g, np.argsort(_padded_order_blocks()).astype(np.int32), "w_in_grad_from_layout")
    return jnp.concatenate([padded[:, :BA_END], padded[:, BA_END + P_IN - N_IN:]], axis=1)


SHARD_W = N_IN // NDEV
GAP = P_IN - N_IN
WIN = 2304


def _padded_col(c):
    return c + (GAP if c >= BA_END else 0)


WIN_START = [min(_padded_col(SHARD_W * d) // LANES * LANES, P_IN - WIN) for d in range(NDEV)]
WIN_OFF = [_padded_col(SHARD_W * d) - WIN_START[d] for d in range(NDEV)]
STRADDLER = BA_END // SHARD_W
STRADDLE_AT = BA_END - STRADDLER * SHARD_W
assert all(WIN_OFF[d] + SHARD_W + (GAP if d == STRADDLER else 0) <= WIN for d in range(NDEV))


def _win_off(me):
    off = jnp.int32(0)
    for d in range(NDEV):
        off = jnp.where(me == d, jnp.int32(WIN_OFF[d]), off)
    return off


def _window_of_shard(shard, me):
    rows = shard.shape[0]
    zeros = lambda n: jnp.zeros((rows, n), shard.dtype)
    plain = lax.dynamic_update_slice(zeros(WIN), shard, (0, _win_off(me)))
    o = WIN_OFF[STRADDLER]
    split = jnp.concatenate([zeros(o), shard[:, :STRADDLE_AT], zeros(GAP), shard[:, STRADDLE_AT:], zeros(WIN - o - GAP - SHARD_W)], axis=1)
    return jnp.where(me == STRADDLER, split, plain)


def _shard_of_window(win, me):
    plain = lax.dynamic_slice(win, (0, _win_off(me)), (win.shape[0], SHARD_W))
    o = WIN_OFF[STRADDLER]
    split = jnp.concatenate([win[:, o:o + STRADDLE_AT], win[:, o + STRADDLE_AT + GAP:o + GAP + SHARD_W]], axis=1)
    return jnp.where(me == STRADDLER, split, plain)


def _layout_from_windows(wins):
    _, rows, _ = wins.shape
    data = []
    for d in range(NDEV):
        lo = _padded_col(SHARD_W * d)
        data.append([(lo, lo + STRADDLE_AT), (lo + STRADDLE_AT + GAP, lo + GAP + SHARD_W)] if d == STRADDLER else [(lo, lo + SHARD_W)])
    zero_block = (0, WIN // LANES - 1)
    table = []
    for p in _padded_order_blocks():
        src = [(d, int(p) - WIN_START[d] // LANES) for d in range(NDEV)
               if any(lo < (p + 1) * LANES and hi > p * LANES for lo, hi in data[d])]
        assert len(src) <= 2 and all(0 <= b < WIN // LANES for _, b in src)
        src += [zero_block] * (2 - len(src))
        table.append([src[0][0], src[0][1], src[1][0], src[1][1]])
    table = np.asarray(table, np.int32).T.copy()

    def body(t_ref, a0_ref, b0_ref, a1_ref, b1_ref, o_ref):
        o_ref[:, :LANES] = a0_ref[...] + b0_ref[...]
        o_ref[:, LANES:] = a1_ref[...] + b1_ref[...]

    src = lambda k, odd: pl.BlockSpec((None, rows, LANES), lambda j, t: (t[k, 2 * j + odd], 0, t[k + 1, 2 * j + odd]))
    gs = pltpu.PrefetchScalarGridSpec(
        num_scalar_prefetch=1, grid=(P_IN // (2 * LANES),), in_specs=[src(0, 0), src(2, 0), src(0, 1), src(2, 1)],
        out_specs=pl.BlockSpec((rows, 2 * LANES), lambda j, t: (0, j)))
    return pl.pallas_call(body, grid_spec=gs, out_shape=_sds((rows, P_IN), wins.dtype), name="w_in_from_windows",
                          compiler_params=_params(("parallel",)))(jnp.asarray(table), wins, wins, wins, wins)


def _windows_from_layout(g):
    rows = g.shape[0]
    where = np.argsort(_padded_order_blocks())
    nb = WIN // LANES
    table = np.asarray([where[WIN_START[d] // LANES + b] for d in range(NDEV) for b in range(nb)], np.int32)

    def body(t_ref, x0_ref, x1_ref, o_ref):
        o_ref[:, :LANES] = x0_ref[...]
        o_ref[:, LANES:] = x1_ref[...]

    src = lambda odd: pl.BlockSpec((rows, LANES), lambda d, b, t: (0, t[d * nb + 2 * b + odd]))
    gs = pltpu.PrefetchScalarGridSpec(num_scalar_prefetch=1, grid=(NDEV, nb // 2), in_specs=[src(0), src(1)],
                                      out_specs=pl.BlockSpec((None, rows, 2 * LANES), lambda d, b, t: (d, 0, b)))
    return pl.pallas_call(body, grid_spec=gs, out_shape=_sds((NDEV, rows, WIN), g.dtype), name="w_in_grad_windows",
                          compiler_params=_params(("parallel", "parallel")))(jnp.asarray(table), g, g)


def _to_slots_cols(g):
    rows, cols = g.shape
    return g.reshape(rows, NDEV, cols // NDEV).transpose(1, 0, 2)


def _from_slots_cols(a):
    n, rows, cols = a.shape
    return a.transpose(1, 0, 2).reshape(rows, n * cols)


WEIGHT_ORDER = ["norm1_w", "w_in", "conv_w", "a_log", "dt_bias", "gdn_norm_w", "ret_norm_w", "w_out", "norm2_w", "w_gate", "w_up",
                "w_down", "norm_f_w"]


def kernel(x, norm1_w, w_in, conv_w, a_log, dt_bias, gdn_norm_w, ret_norm_w, w_out, norm2_w, w_gate, w_up, w_down, norm_f_w, loss_target, m_norm1_w, m_w_in, m_conv_w, m_a_log, m_dt_bias, m_gdn_norm_w, m_ret_norm_w, m_w_out, m_norm2_w, m_w_gate, m_w_up, m_w_down, m_norm_f_w, v_norm1_w, v_w_in, v_conv_w, v_a_log, v_dt_bias, v_gdn_norm_w, v_ret_norm_w, v_w_out, v_norm2_w, v_w_gate, v_w_up, v_w_down, v_norm_f_w):
    ax, ay, ac = _coords()
    me = 4 * ax + 2 * ay + ac
    core = jnp.reshape(ac, (1,)).astype(jnp.int32)
    chip = jnp.reshape(2 * ax + ay, (1,)).astype(jnp.int32)
    w = dict(norm1_w=norm1_w, w_in=w_in[0], conv_w=conv_w[0], a_log=a_log, dt_bias=dt_bias, gdn_norm_w=gdn_norm_w,
             ret_norm_w=ret_norm_w, w_out=w_out[0], norm2_w=norm2_w, w_gate=w_gate[0], w_up=w_up[0], w_down=w_down[0],
             norm_f_w=norm_f_w)
    m = dict(norm1_w=m_norm1_w, w_in=m_w_in[0], conv_w=m_conv_w[0], a_log=m_a_log, dt_bias=m_dt_bias, gdn_norm_w=m_gdn_norm_w,
             ret_norm_w=m_ret_norm_w, w_out=m_w_out[0], norm2_w=m_norm2_w, w_gate=m_w_gate[0], w_up=m_w_up[0], w_down=m_w_down[0],
             norm_f_w=m_norm_f_w)
    v = dict(norm1_w=v_norm1_w, w_in=v_w_in[0], conv_w=v_conv_w[0], a_log=v_a_log, dt_bias=v_dt_bias, gdn_norm_w=v_gdn_norm_w,
             ret_norm_w=v_ret_norm_w, w_out=v_w_out[0], norm2_w=v_norm2_w, w_gate=v_w_gate[0], w_up=v_w_up[0], w_down=v_w_down[0],
             norm_f_w=v_norm_f_w)
    big_names = ["w_in", "w_out", "w_gate", "w_up", "w_down"]

    w_in_wins, conv_all = _run_side(_gather_side([_window_of_shard(w["w_in"].astype(BF16), me), w["conv_w"]]), "w_in_allgather")
    w_in_full = _layout_from_windows(w_in_wins)
    conv_full = _from_slots_cols(conv_all)
    dist = dict(core=core, shards=[w[k].astype(BF16) for k in ("w_out", "w_gate", "w_up", "w_down")])

    loss_tile, dx, big, small = _local_step(
        x[0], loss_target[0], w_in_full, None, None, None, None, norm1_w, conv_full, a_log, dt_bias,
        gdn_norm_w, ret_norm_w, norm2_w, norm_f_w.reshape(1, D), dist=dist)
    loss = lax.psum(loss_tile[0, 0], ("x", "y", "c"))

    g_w_in = _shard_of_window(_sum_slots(big["w_in"], 64, "w_in_grad_sum"), me)
    out = {"w_in": (g_w_in, *_adamw_rows(w["w_in"], g_w_in, m["w_in"], v["w_in"], 64, "adamw_w_in"))}
    for k in ("w_out", "w_gate", "w_up", "w_down"):
        part, recv = big[k]
        out[k] = _adamw_reduced(part, recv, chip, w[k], m[k], v[k], 128, "adamw_" + k)

    g_small = _unpack_small(_allreduce_small(_pack_small(**small), "small_grads_allreduce"))
    g_small["conv_w"] = lax.dynamic_slice_in_dim(g_small["conv_w"], me * (2 * D // NDEV), 2 * D // NDEV, axis=1)
    small_names = [k for k in WEIGHT_ORDER if k not in big_names]
    pad_conv = lambda a: jnp.pad(a, ((0, 0), (0, 2 * D - a.shape[1])))
    packs = []
    for src in (w, g_small, m, v):
        args = {k: (pad_conv(src[k]) if k == "conv_w" else src[k]) for k in small_names}
        packs.append(_pack_small(**args))
    d_pack, m_pack, v_pack = _adamw_plain(*packs[0:1], packs[1], packs[2], packs[3], name="adamw_small")
    cut_conv = lambda dct: {**dct, "conv_w": dct["conv_w"][:, :2 * D // NDEV]}
    d_small, m_small, v_small = (cut_conv(_unpack_small(p)) for p in (d_pack, m_pack, v_pack))

    def shaped(k, a):
        return a.reshape(w_shapes[k])

    w_shapes = dict(norm1_w=norm1_w.shape, w_in=w_in.shape, conv_w=conv_w.shape, a_log=a_log.shape, dt_bias=dt_bias.shape,
                    gdn_norm_w=gdn_norm_w.shape, ret_norm_w=ret_norm_w.shape, w_out=w_out.shape, norm2_w=norm2_w.shape,
                    w_gate=w_gate.shape, w_up=w_up.shape, w_down=w_down.shape, norm_f_w=norm_f_w.shape)
    grads, deltas, new_m, new_v = [], [], [], []
    for k in WEIGHT_ORDER:
        if k in big_names:
            g_, d_, m_, v_ = out[k]
        else:
            g_, d_, m_, v_ = g_small[k], d_small[k], m_small[k], v_small[k]
        grads.append(shaped(k, g_))
        deltas.append(shaped(k, d_))
        new_m.append(shaped(k, m_))
        new_v.append(shaped(k, v_))
    return (loss, dx[None], *grads, *deltas, *new_m, *new_v)
```

```python
import functools
import numpy as np
import jax
import jax.numpy as jnp
from jax import lax
from jax.experimental import pallas as pl
from jax.experimental.pallas import tpu as pltpu

F32, BF16 = jnp.float32, jnp.bfloat16
HI = lax.Precision.HIGHEST
MESH = pl.DeviceIdType.MESH
ANY = pl.BlockSpec(memory_space=pl.ANY)
VMEM_FULL = pl.BlockSpec(memory_space=pltpu.VMEM)

NDEV = 8
D = 2048
H = 8
DK = 128
DV = 256
C = 64
CPB = 4
EPS = 1e-6
ROPE_BASE = 10000.0
N_IN = 16400
O_MERGE, O_QKV, O_RET, O_BA, P_IN = 0, 8192, 12288, 16384, 16640
MERGE_W, RET_W = 4 * DV, 2 * DK + DV
LR, B1, B2, EPS_ADAM, WD, STEP = 0.001, 0.9, 0.999, 1e-08, 0.01, 10
VMEM_CAP = 60 * 1024 * 1024

NN = ((1,), (0,))
NT = ((1,), (1,))
TN = ((0,), (0,))


def _params(sem=None, est=None):
    kw = {}
    if sem is not None:
        kw["dimension_semantics"] = sem
    if est is not None:
        kw["vmem_limit_bytes"] = int(min(VMEM_CAP, max(32 * 1024 * 1024, est * 5 // 4 + (4 << 20))))
    return pltpu.CompilerParams(**kw)


def _sds(shape, dt=F32):
    return jax.ShapeDtypeStruct(tuple(shape), dt)


def _row_tile(rows, limit):
    return max(t for t in range(16, min(rows, limit) + 1, 16) if rows % t == 0)


def _bdot(a, b, dims):
    return lax.dot_general(a.astype(BF16), b.astype(BF16), (dims, ((), ())), preferred_element_type=F32)


def _hdot(a, b, dims):
    return lax.dot_general(a, b, (dims, ((), ())), precision=HI, preferred_element_type=F32)


_sigmoid = jax.nn.sigmoid


def _silu(x):
    return x * _sigmoid(x)


def _rms(x, w):
    return x * lax.rsqrt(jnp.mean(x * x, axis=-1, keepdims=True) + EPS) * w


class _Side:
    def __init__(self, ins, out_shapes, sems, start, finish):
        self.ins, self.out_shapes, self.sems, self.start, self.finish = list(ins), list(out_shapes), list(sems), start, finish


def _run_side(side, name):
    ni, no = len(side.ins), len(side.out_shapes)

    def body(*refs):
        ins, outs, sems = refs[:ni], refs[ni:ni + no], refs[ni + no:]
        side.start(ins, outs, *sems)
        side.finish(ins, outs, *sems)

    return pl.pallas_call(body, in_specs=[ANY] * ni, out_specs=[ANY] * no, out_shape=side.out_shapes,
                          scratch_shapes=[pltpu.SemaphoreType.DMA(s) for s in side.sems], name=name)(*side.ins)


def _matmul(a, b, *, ta=False, tb=False, tm, tn, tk, out_dtype=F32, res=None, side=None, name):
    m = a.shape[1] if ta else a.shape[0]
    k = a.shape[0] if ta else a.shape[1]
    n = b.shape[0] if tb else b.shape[1]
    assert k == (b.shape[1] if tb else b.shape[0])
    tm, tn, tk = min(tm, m), min(tn, n), min(tk, k)
    assert m % tm == 0 and n % tn == 0 and k % tk == 0, (name, m, n, k, tm, tn, tk)
    nk = k // tk
    dims = ((0 if ta else 1,), (1 if tb else 0,))
    has_res = res is not None
    n_in = 3 if has_res else 2
    n_side_in = len(side.ins) if side else 0
    n_side_out = len(side.out_shapes) if side else 0
    grid = (m // tm, n // tn, nk)

    def body(*refs):
        a_ref, b_ref = refs[0], refs[1]
        r_ref = refs[2] if has_res else None
        o_ref = refs[n_in + n_side_in]
        if side:
            side_ins = refs[n_in:n_in + n_side_in]
            side_outs = refs[n_in + n_side_in + 1:n_in + n_side_in + 1 + n_side_out]
            side_sems = refs[len(refs) - len(side.sems):]
            step = (pl.program_id(0) * grid[1] + pl.program_id(1)) * grid[2] + pl.program_id(2)

            @pl.when(step == 0)
            def _():
                side.start(side_ins, side_outs, *side_sems)

        def finish(acc):
            if has_res:
                acc = acc + r_ref[...].astype(F32)
            o_ref[...] = acc.astype(out_dtype)

        part = _bdot(a_ref[...], b_ref[...], dims)
        if nk == 1:
            finish(part)
        else:
            acc_ref = refs[n_in + n_side_in + 1 + n_side_out]
            kk = pl.program_id(2)

            @pl.when(kk == 0)
            def _():
                acc_ref[...] = part

            @pl.when(kk > 0)
            def _():
                acc_ref[...] += part

            @pl.when(kk == nk - 1)
            def _():
                finish(acc_ref[...])

        if side:
            @pl.when(step == grid[0] * grid[1] * grid[2] - 1)
            def _():
                side.finish(side_ins, side_outs, *side_sems)

    a_spec = pl.BlockSpec((tk, tm), lambda i, j, kk: (kk, i)) if ta else pl.BlockSpec((tm, tk), lambda i, j, kk: (i, kk))
    b_spec = pl.BlockSpec((tn, tk), lambda i, j, kk: (j, kk)) if tb else pl.BlockSpec((tk, tn), lambda i, j, kk: (kk, j))
    o_spec = pl.BlockSpec((tm, tn), lambda i, j, kk: (i, j))
    in_specs = [a_spec, b_spec] + ([o_spec] if has_res else []) + [ANY] * n_side_in
    est = 2 * (tm * tk * a.dtype.itemsize + tk * tn * b.dtype.itemsize + tm * tn * jnp.dtype(out_dtype).itemsize)
    est += 2 * tm * tn * 4 * (1 if has_res else 0) + (tm * tn * 4 if nk > 1 else 0) + 2 * tm * tn * 4
    args = (a, b) + ((res,) if has_res else ()) + (tuple(side.ins) if side else ())
    scratch = ([pltpu.VMEM((tm, tn), F32)] if nk > 1 else []) + ([pltpu.SemaphoreType.DMA(s) for s in side.sems] if side else [])
    sem = ("arbitrary",) * 3 if side else ("parallel", "parallel", "arbitrary")
    out = pl.pallas_call(
        body, grid=grid, in_specs=in_specs, out_specs=[o_spec] + [ANY] * n_side_out,
        out_shape=[_sds((m, n), out_dtype)] + (side.out_shapes if side else []),
        scratch_shapes=scratch, name=name, compiler_params=_params(sem, est))(*args)
    return (out[0], out[1:]) if side else out[0]


def _rms_fwd(x, w, ts, name):
    s = x.shape[0]

    def body(x_ref, w_ref, o_ref, ot_ref):
        y = _rms(x_ref[...], w_ref[...]).astype(BF16)
        o_ref[...] = y
        ot_ref[...] = y.T

    row = pl.BlockSpec((ts, D), lambda i: (i, 0))
    return pl.pallas_call(body, grid=(s // ts,), in_specs=[row, pl.BlockSpec((1, D), lambda i: (0, 0))],
                          out_specs=[row, pl.BlockSpec((D, ts), lambda i: (0, i))],
                          out_shape=[_sds((s, D), BF16), _sds((D, s), BF16)], name=name,
                          compiler_params=_params(("parallel",)))(x, w)


def _rms_bwd(x, w, du, dres, ts, name):
    s = x.shape[0]

    def body(x_ref, w_ref, du_ref, dres_ref, dx_ref, dw_ref):
        _, vjp = jax.vjp(_rms, x_ref[...], w_ref[...])
        dx, dw = vjp(du_ref[...].astype(F32))
        dx_ref[...] = dx + dres_ref[...]

        @pl.when(pl.program_id(0) == 0)
        def _():
            dw_ref[...] = jnp.zeros_like(dw_ref)

        dw_ref[...] += dw

    row = pl.BlockSpec((ts, D), lambda i: (i, 0))
    vec = pl.BlockSpec((1, D), lambda i: (0, 0))
    return pl.pallas_call(body, grid=(s // ts,), in_specs=[row, vec, row, row], out_specs=[row, vec],
                          out_shape=[_sds((s, D)), _sds((1, D))], name=name,
                          compiler_params=_params(("arbitrary",), 12 * ts * D * 4))(x, w, du, dres)


def _conv_taps(xx, w, base, ts):
    acc = xx[base:base + ts] * w[0:1, :]
    for j in range(1, 4):
        acc = acc + xx[base + j:base + j + ts] * w[j:j + 1, :]
    return acc


def _causal_conv(prev8, cur, w, first):
    xx = jnp.concatenate([jnp.where(first, 0.0, prev8), cur], axis=0)
    return _conv_taps(xx, w, 5, cur.shape[0])


def _qk_post(c, scale):
    s = _silu(c)
    return s * lax.rsqrt(jnp.sum(s * s, axis=-1, keepdims=True) + EPS) * scale


def _conv_specs(ts, cw, col0):
    pcol = O_QKV // cw + col0
    cur = pl.BlockSpec((ts, cw), lambda j, i: (i, pcol + j))
    prev = pl.BlockSpec((8, cw), lambda j, i: (jnp.maximum(i * (ts // 8) - 1, 0), pcol + j))
    wsp = pl.BlockSpec((4, cw), lambda j, i: (0, col0 + j))
    return cur, prev, wsp


def _gdn_qkv_fwd(proj, conv_w, ts):
    s = proj.shape[0]

    def qk_body(cur_ref, prev_ref, w_ref, o_ref, c_ref):
        c = _causal_conv(prev_ref[...], cur_ref[...], w_ref[...], pl.program_id(1) == 0)
        scale = jnp.where(pl.program_id(0) < H, DK ** -0.5, 1.0).astype(F32)
        c_ref[...] = c
        o_ref[...] = _qk_post(c, scale)

    tq = min(2 * ts, s)
    cur, prev, wsp = _conv_specs(tq, DK, 0)
    out = pl.BlockSpec((tq, DK), lambda j, i: (i, j))
    qk, c_qk = pl.pallas_call(qk_body, grid=(2 * H, s // tq), in_specs=[cur, prev, wsp], out_specs=[out, out],
                              out_shape=[_sds((s, 2 * H * DK))] * 2, name="gdn_qk_prep",
                              compiler_params=_params(("parallel", "parallel")))(proj, proj, conv_w)

    def v_body(cur_ref, prev_ref, w_ref, o_ref, c_ref):
        c = _causal_conv(prev_ref[...], cur_ref[...], w_ref[...], pl.program_id(1) == 0)
        c_ref[...] = c
        o_ref[...] = _silu(c)

    cw = 512
    cur, prev, wsp = _conv_specs(ts, cw, 2 * H * DK // cw)
    out = pl.BlockSpec((ts, cw), lambda j, i: (i, j))
    v, c_v = pl.pallas_call(v_body, grid=(H * DV // cw, s // ts), in_specs=[cur, prev, wsp], out_specs=[out, out],
                            out_shape=[_sds((s, H * DV))] * 2, name="gdn_v_prep",
                            compiler_params=_params(("parallel", "parallel")))(proj, proj, conv_w)
    return qk, v, c_qk, c_v


def _gdn_qkv_bwd(proj, conv_w, c_qk, c_v, dqk, dv, dproj, ts):
    s = proj.shape[0]
    nt = s // ts

    def qk_body(c_ref, d_ref, o_ref):
        scale = jnp.where(pl.program_id(0) < H, DK ** -0.5, 1.0).astype(F32)
        _, vjp = jax.vjp(lambda cc: _qk_post(cc, scale), c_ref[...])
        o_ref[...] = vjp(d_ref[...])[0]

    tq = min(2 * ts, s)
    blk = pl.BlockSpec((tq, DK), lambda j, i: (i, j))
    dc_qk = pl.pallas_call(qk_body, grid=(2 * H, s // tq), in_specs=[blk, blk], out_specs=blk, out_shape=_sds((s, 2 * H * DK)),
                           name="gdn_qk_prep_bwd", compiler_params=_params(("parallel", "parallel")))(c_qk, dqk)

    def v_body(c_ref, d_ref, o_ref):
        _, vjp = jax.vjp(_silu, c_ref[...])
        o_ref[...] = vjp(d_ref[...])[0]

    cw = 512
    blk = pl.BlockSpec((ts, cw), lambda j, i: (i, j))
    dc_v = pl.pallas_call(v_body, grid=(H * DV // cw, nt), in_specs=[blk, blk], out_specs=blk, out_shape=_sds((s, H * DV)),
                          name="gdn_v_prep_bwd", compiler_params=_params(("parallel", "parallel")))(c_v, dv)

    def conv_bwd(dc, dproj, col0, ncols, name):
        def body(x_ref, xprev_ref, w_ref, dc_ref, dcnext_ref, _, da_ref, dw_ref):
            i = pl.program_id(1)
            w = w_ref[...]
            dcur = dc_ref[...]
            dd = jnp.concatenate([dcur, jnp.where(i == nt - 1, 0.0, dcnext_ref[...])], axis=0)
            acc = dd[3:3 + ts] * w[0:1, :]
            for j in range(1, 4):
                acc = acc + dd[3 - j:3 - j + ts] * w[j:j + 1, :]
            da_ref[...] = acc.astype(BF16)
            xx = jnp.concatenate([jnp.where(i == 0, 0.0, xprev_ref[...]), x_ref[...]], axis=0)

            @pl.when(i == 0)
            def _():
                dw_ref[...] = jnp.zeros_like(dw_ref)

            for j in range(4):
                dw_ref[j:j + 1, :] += jnp.sum(dcur * xx[5 + j:5 + j + ts], axis=0, keepdims=True)

        cur, prev, wsp = _conv_specs(ts, cw, col0)
        dcur = pl.BlockSpec((ts, cw), lambda j, i: (i, j))
        dnext = pl.BlockSpec((8, cw), lambda j, i: (jnp.minimum((i + 1) * (ts // 8), s // 8 - 1), j))
        pcol = O_QKV // cw + col0
        return pl.pallas_call(body, grid=(ncols // cw, nt), in_specs=[cur, prev, wsp, dcur, dnext, ANY],
                              out_specs=[pl.BlockSpec((ts, cw), lambda j, i: (i, pcol + j)), pl.BlockSpec((4, cw), lambda j, i: (0, j))],
                              out_shape=[_sds(dproj.shape, BF16), _sds((4, ncols))], input_output_aliases={5: 0}, name=name,
                              compiler_params=_params(("parallel", "arbitrary")))(proj, proj, conv_w, dc, dc, dproj)

    dproj, dw_qk = conv_bwd(dc_qk, dproj, 0, 2 * H * DK, "conv_bwd_qk")
    dproj, dw_v = conv_bwd(dc_v, dproj, 2 * H * DK // cw, H * DV, "conv_bwd_v")
    return dproj, jnp.concatenate([dw_qk, dw_v], axis=1)


def _bg(b, a, alog, dtb):
    n = b.shape[0]
    g = -jnp.exp(alog) * jax.nn.softplus(a + dtb)
    row = lax.broadcasted_iota(jnp.int32, (n, n), 0)
    col = lax.broadcasted_iota(jnp.int32, (n, n), 1)
    shift = C.bit_length() - 1
    same = (row >> shift) == (col >> shift)
    return _sigmoid(b), _hdot((same & (row >= col)).astype(F32), g, NN), _hdot(same.astype(F32), g, NN)


def _bg_fwd(proj, alog, dtb, ts):
    s = proj.shape[0]

    def body(ba_ref, alog_ref, dtb_ref, beta_ref, gc_ref, gl_ref):
        beta_ref[...], gc_ref[...], gl_ref[...] = _bg(ba_ref[:, 0:H], ba_ref[:, H:2 * H], alog_ref[...], dtb_ref[...])

    small = pl.BlockSpec((1, H), lambda i: (0, 0))
    out = pl.BlockSpec((ts, H), lambda i: (i, 0))
    return pl.pallas_call(body, grid=(s // ts,), in_specs=[pl.BlockSpec((ts, 256), lambda i: (i, O_BA // 256)), small, small],
                          out_specs=[out] * 3, out_shape=[_sds((s, H))] * 3, name="gdn_bg_prep",
                          compiler_params=_params(("parallel",)))(proj, alog, dtb)


def _bg_bwd(proj, alog, dtb, dbeta_h, dgc_h, dgl_h, dproj, ts):
    s = proj.shape[0]

    def body(ba_ref, alog_ref, dtb_ref, dbeta_ref, dgc_ref, dgl_ref, _, dba_ref, dalog_ref, ddtb_ref):
        _, vjp = jax.vjp(_bg, ba_ref[:, 0:H], ba_ref[:, H:2 * H], alog_ref[...], dtb_ref[...])
        db, da, dalog, ddtb = vjp((jnp.sum(dbeta_ref[...], axis=0), jnp.sum(dgc_ref[...], axis=0), jnp.sum(dgl_ref[...], axis=0)))
        dba_ref[...] = jnp.zeros_like(dba_ref)
        dba_ref[:, 0:H] = db.astype(BF16)
        dba_ref[:, H:2 * H] = da.astype(BF16)

        @pl.when(pl.program_id(0) == 0)
        def _():
            dalog_ref[...] = jnp.zeros_like(dalog_ref)
            ddtb_ref[...] = jnp.zeros_like(ddtb_ref)

        dalog_ref[...] += dalog
        ddtb_ref[...] += ddtb

    small = pl.BlockSpec((1, H), lambda i: (0, 0))
    per_head = pl.BlockSpec((H, ts, H), lambda i: (0, i, 0))
    return pl.pallas_call(body, grid=(s // ts,),
                          in_specs=[pl.BlockSpec((ts, 256), lambda i: (i, O_BA // 256)), small, small, per_head, per_head, per_head, ANY],
                          out_specs=[pl.BlockSpec((ts, 256), lambda i: (i, O_BA // 256)), small, small],
                          out_shape=[_sds(dproj.shape, BF16), _sds((1, H)), _sds((1, H))], input_output_aliases={6: 0},
                          name="gdn_bg_prep_bwd", compiler_params=_params(("arbitrary",)))(proj, alog, dtb, dbeta_h, dgc_h, dgl_h, dproj)


BLK = 4 * C
NNB, NTB, TNB = ((2,), (1,)), ((2,), (2,)), ((1,), (1,))


def _bdot_b(a, b, dims):
    return lax.dot_general(a.astype(BF16), b.astype(BF16), (dims, ((0,), (0,))), preferred_element_type=F32)


@jax.custom_vjp
def _inv_unit_lower(a):
    n = a.shape[-1]
    row = lax.broadcasted_iota(jnp.int32, (n, n), 0)
    col = lax.broadcasted_iota(jnp.int32, (n, n), 1)
    x = jnp.where(row == col, 1.0, 0.0).astype(F32) - a
    p = _bdot_b(a, a, NNB)
    power = 2
    while True:
        x = x + _bdot_b(x, p, NNB)
        power *= 2
        if power >= C:
            return x
        p = _bdot_b(p, p, NNB)


def _inv_fwd(a):
    t = _inv_unit_lower(a)
    return t, t


def _inv_bwd(t, dt):
    return (-_bdot_b(_bdot_b(t, dt, TNB), t, NTB),)


_inv_unit_lower.defvjp(_inv_fwd, _inv_bwd)


def _gdn_prep(q, k, v, bfull, gcfull, glfull, hmask):
    nb, n = q.shape[0], q.shape[1]
    beta = jnp.sum(bfull * hmask, axis=-1, keepdims=True)
    gc = jnp.sum(gcfull * hmask, axis=-1, keepdims=True)
    gl = jnp.sum(glfull * hmask, axis=-1, keepdims=True)
    row = lax.broadcasted_iota(jnp.int32, (n, n), 0)
    col = lax.broadcasted_iota(jnp.int32, (n, n), 1)
    shift = C.bit_length() - 1
    same = (row >> shift) == (col >> shift)
    incl, strict = same & (row >= col), same & (row > col)
    g_i = gc * jnp.ones((1, 1, n), F32)
    decay = jnp.exp(jnp.where(incl, g_i - jnp.swapaxes(g_i, 1, 2), -jnp.inf))
    kb = k * beta
    a = jnp.where(strict, _bdot_b(kb, k, NTB) * decay, 0.0)
    tinv = _inv_unit_lower(a)
    u = _bdot_b(tinv, v * beta, NNB)
    w = _bdot_b(tinv, kb * jnp.exp(gc), NNB)
    attn = _bdot_b(q, k, NTB) * decay
    fold = ((lax.broadcasted_iota(jnp.int32, (n, C), 0) & (C - 1)) == lax.broadcasted_iota(jnp.int32, (n, C), 1)).astype(F32)
    attn_c = _bdot(attn.reshape(nb * n, n), fold, NN).reshape(nb, n, C)
    return u, w, attn_c, q * jnp.exp(gc), k * jnp.exp(gl - gc), jnp.exp(gl)


def _gdn_step(u, w, attn, qg, kd, egl, state):
    v_new = u - _bdot_b(w, state, NNB)
    o = _bdot_b(qg, state, NNB) + _bdot_b(attn, v_new, NNB)
    return o, state * egl + _bdot_b(kd, v_new, TNB)


def _heads(ref, rs, width):
    return jnp.stack([ref[rs, h * width:(h + 1) * width] for h in range(H)])


def _head_mask(h):
    return (lax.broadcasted_iota(jnp.int32, (1, H), 1) == h).astype(F32)


PREP_BLOCKS = 2


def _gdn_prep_specs(r):
    small = pl.BlockSpec((r, H), lambda h, c: (c, 0))
    return [pl.BlockSpec((r, DK), lambda h, c: (c, h)), pl.BlockSpec((r, DK), lambda h, c: (c, H + h)),
            pl.BlockSpec((r, DV), lambda h, c: (c, h)), small, small, small]


def _blocked(ref):
    x = ref[...]
    return x.reshape(PREP_BLOCKS, BLK, x.shape[-1])


def _gdn_inter_specs(r):
    col = pl.BlockSpec((r, DK), lambda h, c: (c, h))
    return [pl.BlockSpec((r, DV), lambda h, c: (c, h)), col, pl.BlockSpec((1, r, C), lambda h, c: (h, c, 0)), col, col,
            pl.BlockSpec((1, r // C, 8, 128), lambda h, c: (h, c, 0, 0))]


def _gdn_prep_fwd(qk, v, beta, gc, gl):
    s = qk.shape[0]
    r = PREP_BLOCKS * BLK

    def body(q_ref, k_ref, v_ref, b_ref, gc_ref, gl_ref, u_ref, w_ref, attn_ref, qg_ref, kd_ref, egl_ref):
        u, w, attn, qg, kd, egl = _gdn_prep(_blocked(q_ref), _blocked(k_ref), _blocked(v_ref), _blocked(b_ref), _blocked(gc_ref),
                                            _blocked(gl_ref), _head_mask(pl.program_id(0)))
        u_ref[...] = u.reshape(r, DV)
        w_ref[...] = w.reshape(r, DK).astype(BF16)
        attn_ref[0] = attn.reshape(r, C).astype(BF16)
        qg_ref[...] = qg.reshape(r, DK).astype(BF16)
        kd_ref[...] = kd.reshape(r, DK).astype(BF16)
        egl = egl.reshape(r, 1)
        for j in range(r // C):
            egl_ref[0, j] = egl[j * C:j * C + 1, :] * jnp.ones((8, 128), F32)

    out_shape = [_sds((s, H * DV)), _sds((s, H * DK), BF16), _sds((H, s, C), BF16), _sds((s, H * DK), BF16),
                 _sds((s, H * DK), BF16), _sds((H, s // C, 8, 128))]
    return pl.pallas_call(body, grid=(H, s // r), in_specs=_gdn_prep_specs(r), out_specs=_gdn_inter_specs(r), out_shape=out_shape,
                          name="gdn_prep_fwd", compiler_params=_params(("parallel", "parallel")))(qk, qk, v, beta, gc, gl)


def _gdn_prep_bwd(qk, v, beta, gc, gl, du, dw, dattn, dqg, dkd, degl):
    s = qk.shape[0]
    r = PREP_BLOCKS * BLK

    def body(q_ref, k_ref, v_ref, b_ref, gc_ref, gl_ref, du_ref, dw_ref, dattn_ref, dqg_ref, dkd_ref, degl_ref,
             dq_ref, dk_ref, dv_ref, db_ref, dgc_ref, dgl_ref):
        hmask = _head_mask(pl.program_id(0))
        _, vjp = jax.vjp(lambda q, k, v, b, gc, gl: _gdn_prep(q, k, v, b, gc, gl, hmask), _blocked(q_ref), _blocked(k_ref),
                         _blocked(v_ref), _blocked(b_ref), _blocked(gc_ref), _blocked(gl_ref))
        rowid = lax.broadcasted_iota(jnp.int32, (r, 1), 0)
        degl = jnp.zeros((r, 1), F32)
        for j in range(r // C):
            degl = jnp.where(rowid == j * C, degl_ref[0, j, 0:1, 0:1], degl)
        dq, dk, dv, db, dgc, dgl = vjp((_blocked(du_ref), _blocked(dw_ref), _blocked(dattn_ref.at[0]), _blocked(dqg_ref),
                                        _blocked(dkd_ref), degl.reshape(PREP_BLOCKS, BLK, 1)))
        dq_ref[...] = dq.reshape(r, DK)
        dk_ref[...] = dk.reshape(r, DK)
        dv_ref[...] = dv.reshape(r, DV)
        db_ref[0] = db.reshape(r, H)
        dgc_ref[0] = dgc.reshape(r, H)
        dgl_ref[0] = dgl.reshape(r, H)

    col = pl.BlockSpec((r, DK), lambda h, c: (c, h))
    piece = pl.BlockSpec((1, r, H), lambda h, c: (h, c, 0))
    dq, dk, dv, db, dgc, dgl = pl.pallas_call(
        body, grid=(H, s // r), in_specs=_gdn_prep_specs(r) + _gdn_inter_specs(r),
        out_specs=[col, col, pl.BlockSpec((r, DV), lambda h, c: (c, h)), piece, piece, piece],
        out_shape=[_sds((s, H * DK)), _sds((s, H * DK)), _sds((s, H * DV))] + [_sds((H, s, H))] * 3,
        name="gdn_prep_bwd", compiler_params=_params(("parallel", "parallel"), 24 << 20))(
            qk, qk, v, beta, gc, gl, du, dw, dattn, dqg, dkd, degl)
    return jnp.concatenate([dq, dk], axis=1), dv, db, dgc, dgl


def _gdn_scan_specs(r, order):
    wide = pl.BlockSpec((r, H * DK), lambda c: (order(c), 0))
    return [pl.BlockSpec((r, H * DV), lambda c: (order(c), 0)), wide, pl.BlockSpec((H, r, C), lambda c: (0, order(c), 0)), wide, wide,
            pl.BlockSpec((H, r // C, 8, 128), lambda c: (0, order(c), 0, 0))]


def _gdn_scan_fwd(u, w, attn, qg, kd, egl):
    s = u.shape[0]
    r = CPB * C
    nb = s // r

    def body(u_ref, w_ref, attn_ref, qg_ref, kd_ref, egl_ref, o_ref, st_ref, state_ref):
        @pl.when(pl.program_id(0) == 0)
        def _():
            state_ref[...] = jnp.zeros_like(state_ref)

        state = state_ref[...]
        for i in range(CPB):
            rs = slice(i * C, (i + 1) * C)
            st_ref[:, i] = state
            o, state = _gdn_step(_heads(u_ref, rs, DV), _heads(w_ref, rs, DK), attn_ref[:, rs, :], _heads(qg_ref, rs, DK),
                                 _heads(kd_ref, rs, DK), egl_ref[:, i, 0:1, 0:1], state)
            for h in range(H):
                o_ref[rs, h * DV:(h + 1) * DV] = o[h]
        state_ref[...] = state

    out_specs = [pl.BlockSpec((r, H * DV), lambda c: (c, 0)), pl.BlockSpec((H, CPB, DK, DV), lambda c: (0, c, 0, 0))]
    return pl.pallas_call(body, grid=(nb,), in_specs=_gdn_scan_specs(r, lambda c: c), out_specs=out_specs,
                          out_shape=[_sds((s, H * DV)), _sds((H, s // C, DK, DV))],
                          scratch_shapes=[pltpu.VMEM((H, DK, DV), F32)], name="gdn_scan_fwd",
                          compiler_params=_params(("arbitrary",), 24 << 20))(u, w, attn, qg, kd, egl)


def _gdn_scan_bwd(u, w, attn, qg, kd, egl, states, do):
    s = u.shape[0]
    r = CPB * C
    nb = s // r

    def body(u_ref, w_ref, attn_ref, qg_ref, kd_ref, egl_ref, st_ref, do_ref,
             du_ref, dw_ref, dattn_ref, dqg_ref, dkd_ref, degl_ref, dstate_ref):
        @pl.when(pl.program_id(0) == 0)
        def _():
            dstate_ref[...] = jnp.zeros_like(dstate_ref)

        dstate = dstate_ref[...]
        for i in reversed(range(CPB)):
            rs = slice(i * C, (i + 1) * C)
            _, vjp = jax.vjp(_gdn_step, _heads(u_ref, rs, DV), _heads(w_ref, rs, DK).astype(F32), attn_ref[:, rs, :].astype(F32),
                             _heads(qg_ref, rs, DK).astype(F32), _heads(kd_ref, rs, DK).astype(F32), egl_ref[:, i, 0:1, 0:1],
                             st_ref[:, i])
            du, dw, dattn, dqg, dkd, degl, dstate = vjp((_heads(do_ref, rs, DV), dstate))
            dattn_ref[:, rs, :] = dattn
            degl_ref[:, i] = degl * jnp.ones((1, 8, 128), F32)
            for h in range(H):
                du_ref[rs, h * DV:(h + 1) * DV] = du[h]
                dw_ref[rs, h * DK:(h + 1) * DK] = dw[h]
                dqg_ref[rs, h * DK:(h + 1) * DK] = dqg[h]
                dkd_ref[rs, h * DK:(h + 1) * DK] = dkd[h]
        dstate_ref[...] = dstate

    rev = lambda c: nb - 1 - c
    in_specs = _gdn_scan_specs(r, rev) + [pl.BlockSpec((H, CPB, DK, DV), lambda c: (0, rev(c), 0, 0)),
                                          pl.BlockSpec((r, H * DV), lambda c: (rev(c), 0))]
    return pl.pallas_call(
        body, grid=(nb,), in_specs=in_specs, out_specs=_gdn_scan_specs(r, rev),
        out_shape=[_sds((s, H * DV)), _sds((s, H * DK)), _sds((H, s, C)), _sds((s, H * DK)), _sds((s, H * DK)),
                   _sds((H, s // C, 8, 128))],
        scratch_shapes=[pltpu.VMEM((H, DK, DV), F32)], name="gdn_scan_bwd",
        compiler_params=_params(("arbitrary",), 40 << 20))(u, w, attn, qg, kd, egl, states, do)


def _rot(x, cs, sn):
    return x * cs + pltpu.roll(x, DK // 2, 1) * sn


def _rot_t(d, cs, sn):
    return d * cs - pltpu.roll(d, DK // 2, 1) * sn


def _ret_chunk(q, k, v, state, lg):
    n = q.shape[1]
    row = lax.broadcasted_iota(jnp.int32, (n, n), 0)
    col = lax.broadcasted_iota(jnp.int32, (n, n), 1)
    dist = (row - col).astype(F32)
    dmat = jnp.exp(jnp.where(dist >= 0, dist * lg, -jnp.inf))
    scores = _bdot_b(q, k, NTB) * dmat
    pos = lax.broadcasted_iota(jnp.int32, (n, 1), 0).astype(F32)
    xi = jnp.exp((pos + 1.0) * lg)
    zeta = jnp.exp((n - 1.0 - pos) * lg)
    o = _bdot_b(scores, v, NNB) + _bdot_b(q, state, NNB) * xi
    new_state = state * jnp.exp(n * lg) + _bdot_b(k * zeta, v, TNB)
    return o, new_state


def _ret_specs(r, order):
    return [pl.BlockSpec((r, H * RET_W), lambda c: (order(c), O_RET // (H * RET_W))), pl.BlockSpec((r, DK), lambda c: (order(c), 0)),
            pl.BlockSpec((r, DK), lambda c: (order(c), 0)), pl.BlockSpec((H, 1, 1), lambda c: (0, 0, 0))]


def _ret_qkv(x_ref, cs, sn):
    q = jnp.stack([_rot(x_ref[:, h * RET_W:h * RET_W + DK], cs, sn) for h in range(H)])
    k = jnp.stack([_rot(x_ref[:, h * RET_W + DK:h * RET_W + 2 * DK], cs, sn) for h in range(H)]) * DK ** -0.5
    v = jnp.stack([x_ref[:, h * RET_W + 2 * DK:(h + 1) * RET_W] for h in range(H)])
    return q, k, v


RET_C = 256


def _ret_scan_fwd(proj, cs, sn, lgtab):
    s = proj.shape[0]
    r = min(RET_C, s)
    nb = s // r

    def body(x_ref, cs_ref, sn_ref, lg_ref, o_ref, st_ref, state_ref):
        @pl.when(pl.program_id(0) == 0)
        def _():
            state_ref[...] = jnp.zeros_like(state_ref)

        state = state_ref[...]
        st_ref[:, 0] = state
        q, k, v = _ret_qkv(x_ref, cs_ref[...], sn_ref[...])
        o, state_ref[...] = _ret_chunk(q, k, v, state, lg_ref[...])
        for h in range(H):
            o_ref[:, h * DV:(h + 1) * DV] = o[h]

    out_specs = [pl.BlockSpec((r, H * DV), lambda c: (c, 0)), pl.BlockSpec((H, 1, DK, DV), lambda c: (0, c, 0, 0))]
    return pl.pallas_call(body, grid=(nb,), in_specs=_ret_specs(r, lambda c: c), out_specs=out_specs,
                          out_shape=[_sds((s, H * DV)), _sds((H, nb, DK, DV))],
                          scratch_shapes=[pltpu.VMEM((H, DK, DV), F32)], name="ret_scan_fwd",
                          compiler_params=_params(("arbitrary",), 32 << 20))(proj, cs, sn, lgtab.reshape(H, 1, 1))


def _ret_scan_bwd(proj, cs, sn, lgtab, states, do, dproj):
    s = proj.shape[0]
    r = min(RET_C, s)
    nb = s // r

    def body(x_ref, cs_ref, sn_ref, lg_ref, st_ref, do_ref, _, d_ref, dstate_ref):
        @pl.when(pl.program_id(0) == 0)
        def _():
            dstate_ref[...] = jnp.zeros_like(dstate_ref)

        cs_, sn_ = cs_ref[...], sn_ref[...]
        lg = lg_ref[...]
        q, k, v = _ret_qkv(x_ref, cs_, sn_)
        _, vjp = jax.vjp(lambda q, k, v, st: _ret_chunk(q, k, v, st, lg), q, k, v, st_ref[:, 0])
        dq, dk, dv, dstate_ref[...] = vjp((_heads(do_ref, slice(None), DV), dstate_ref[...]))
        for h in range(H):
            d_ref[:, h * RET_W:h * RET_W + DK] = _rot_t(dq[h], cs_, sn_).astype(BF16)
            d_ref[:, h * RET_W + DK:h * RET_W + 2 * DK] = _rot_t(dk[h] * DK ** -0.5, cs_, sn_).astype(BF16)
            d_ref[:, h * RET_W + 2 * DK:(h + 1) * RET_W] = dv[h].astype(BF16)

    rev = lambda c: nb - 1 - c
    in_specs = _ret_specs(r, rev) + [pl.BlockSpec((H, 1, DK, DV), lambda c: (0, rev(c), 0, 0)),
                                     pl.BlockSpec((r, H * DV), lambda c: (rev(c), 0)), ANY]
    return pl.pallas_call(
        body, grid=(nb,), in_specs=in_specs, out_specs=pl.BlockSpec((r, H * RET_W), lambda c: (rev(c), O_RET // (H * RET_W))),
        out_shape=_sds(dproj.shape, BF16), input_output_aliases={6: 0},
        scratch_shapes=[pltpu.VMEM((H, DK, DV), F32)], name="ret_scan_bwd",
        compiler_params=_params(("arbitrary",), 48 << 20))(proj, cs, sn, lgtab.reshape(H, 1, 1), states, do, dproj)


def _merge(oa, z, ob, rg, ga, gb, wa, wb):
    ya = oa * lax.rsqrt(jnp.mean(oa * oa, axis=-1, keepdims=True) + EPS) * wa * _silu(z)
    mu = jnp.mean(ob, axis=-1, keepdims=True)
    var = jnp.mean(jnp.square(ob - mu), axis=-1, keepdims=True)
    yb = (ob - mu) * lax.rsqrt(var + EPS) * wb * _silu(rg)
    return _sigmoid(ga) * ya + _sigmoid(gb) * yb


def _merge_specs(ts):
    own = pl.BlockSpec((ts, DV), lambda h, i: (i, h))
    grp = lambda k: pl.BlockSpec((ts, DV), lambda h, i: (i, O_MERGE // DV + 4 * h + k))
    return [own, grp(0), own, grp(1), grp(2), grp(3),
            pl.BlockSpec((1, DV), lambda h, i: (0, 0)), pl.BlockSpec((1, DV), lambda h, i: (0, h))]


def _merge_fwd(oa, ob, proj, wa, wb, ts):
    s = oa.shape[0]

    def body(oa_ref, z_ref, ob_ref, rg_ref, ga_ref, gb_ref, wa_ref, wb_ref, o_ref, ot_ref):
        y = _merge(oa_ref[...], z_ref[...], ob_ref[...], rg_ref[...], ga_ref[...], gb_ref[...],
                   wa_ref[...], wb_ref[...]).astype(BF16)
        o_ref[...] = y
        ot_ref[...] = y.T

    return pl.pallas_call(body, grid=(H, s // ts), in_specs=_merge_specs(ts),
                          out_specs=[pl.BlockSpec((ts, DV), lambda h, i: (i, h)), pl.BlockSpec((DV, ts), lambda h, i: (h, i))],
                          out_shape=[_sds((s, H * DV), BF16), _sds((H * DV, s), BF16)], name="merge_fwd",
                          compiler_params=_params(("parallel", "parallel")))(oa, proj, ob, proj, proj, proj, wa, wb)


def _merge_bwd(oa, ob, proj, wa, wb, dmixed, ts):
    s = oa.shape[0]

    def body(oa_ref, z_ref, ob_ref, rg_ref, ga_ref, gb_ref, wa_ref, wb_ref, dm_ref,
             doa_ref, dob_ref, dgrp_ref, dwa_ref, dwb_ref):
        _, vjp = jax.vjp(_merge, oa_ref[...], z_ref[...], ob_ref[...], rg_ref[...], ga_ref[...], gb_ref[...],
                         wa_ref[...], wb_ref[...])
        doa, dz, dob, drg, dga, dgb, dwa, dwb = vjp(dm_ref[...].astype(F32))
        doa_ref[...] = doa
        dob_ref[...] = dob
        for k, d in enumerate((dz, drg, dga, dgb)):
            dgrp_ref[:, k * DV:(k + 1) * DV] = d.astype(BF16)
        first_tile = pl.program_id(1) == 0

        @pl.when(first_tile & (pl.program_id(0) == 0))
        def _():
            dwa_ref[...] = jnp.zeros_like(dwa_ref)

        @pl.when(first_tile)
        def _():
            dwb_ref[...] = jnp.zeros_like(dwb_ref)

        dwa_ref[...] += dwa
        dwb_ref[...] += dwb

    blk = pl.BlockSpec((ts, DV), lambda h, i: (i, h))
    out_specs = [blk, blk, pl.BlockSpec((ts, MERGE_W), lambda h, i: (i, O_MERGE // MERGE_W + h)),
                 pl.BlockSpec((1, DV), lambda h, i: (0, 0)), pl.BlockSpec((1, DV), lambda h, i: (0, h))]
    out_shape = [_sds((s, H * DV)), _sds((s, H * DV)), _sds((s, P_IN), BF16), _sds((1, DV)), _sds((1, H * DV))]
    return pl.pallas_call(body, grid=(H, s // ts), in_specs=_merge_specs(ts) + [blk], out_specs=out_specs, out_shape=out_shape,
                          name="merge_bwd", compiler_params=_params(("arbitrary", "arbitrary"), 40 * ts * DV * 4))(
                              oa, proj, ob, proj, proj, proj, wa, wb, dmixed)


def _act(hg, hu):
    return _silu(hg) * hu


def _ffn_gate_up(hn, w_gate, w_up, tm, tn):
    s, f = hn.shape[0], w_gate.shape[1]
    tm, tn = min(tm, s), min(tn, f)
    assert s % tm == 0 and f % tn == 0 and tm % 256 == 0
    sub = tm // 2

    def body(a_ref, wg_ref, wu_ref, pg_ref, pu_ref, act_ref, actt_ref):
        for r0 in range(0, tm, sub):
            rs = slice(r0, r0 + sub)
            hg = _bdot(a_ref[rs, :], wg_ref[...], NN)
            hu = _bdot(a_ref[rs, :], wu_ref[...], NN)
            y, vjp = jax.vjp(_act, hg, hu)
            pg, pu = vjp(jnp.ones_like(y))
            y = y.astype(BF16)
            pg_ref[rs, :] = pg.astype(BF16)
            pu_ref[rs, :] = pu.astype(BF16)
            act_ref[rs, :] = y
            actt_ref[:, rs] = y.T

    wsp = pl.BlockSpec((D, tn), lambda i, j: (0, j))
    blk = pl.BlockSpec((tm, tn), lambda i, j: (i, j))
    est = 2 * (tm * D * 2 + 2 * D * tn * 2 + 4 * tm * tn * 2) + 4 * sub * tn * 4
    return pl.pallas_call(body, grid=(s // tm, f // tn), in_specs=[pl.BlockSpec((tm, D), lambda i, j: (i, 0)), wsp, wsp],
                          out_specs=[blk, blk, blk, pl.BlockSpec((tn, tm), lambda i, j: (j, i))],
                          out_shape=[_sds((s, f), BF16)] * 3 + [_sds((f, s), BF16)], name="ffn_gate_up",
                          compiler_params=_params(("parallel", "parallel"), est))(hn, w_gate, w_up)


def _ffn_down_dx(dh2, w_down, pg, pu, tm, tn):
    s, f = pg.shape
    tm, tn = min(tm, s), min(tn, f)
    assert s % tm == 0 and f % tn == 0 and tm % 256 == 0
    sub = tm // 2

    def body(d_ref, w_ref, pg_ref, pu_ref, dhg_ref, dhu_ref):
        for r0 in range(0, tm, sub):
            rs = slice(r0, r0 + sub)
            dact = _bdot(d_ref[rs, :], w_ref[...], NT)
            dhg_ref[rs, :] = (dact * pg_ref[rs, :].astype(F32)).astype(BF16)
            dhu_ref[rs, :] = (dact * pu_ref[rs, :].astype(F32)).astype(BF16)

    blk = pl.BlockSpec((tm, tn), lambda i, j: (i, j))
    est = 2 * (tm * D * 4 + tn * D * 2 + 4 * tm * tn * 2) + 6 * sub * tn * 4
    return pl.pallas_call(body, grid=(s // tm, f // tn),
                          in_specs=[pl.BlockSpec((tm, D), lambda i, j: (i, 0)), pl.BlockSpec((tn, D), lambda i, j: (j, 0)), blk, blk],
                          out_specs=[blk, blk], out_shape=[_sds((s, f), BF16)] * 2, name="ffn_down_dx",
                          compiler_params=_params(("parallel", "parallel"), est))(dh2, w_down, pg, pu)


def _loss_rows(h2, wf, tgt):
    err = _rms(h2, wf) - tgt
    return 0.5 * jnp.sum(jnp.mean(err * err, axis=-1, keepdims=True), keepdims=True)


def _loss_fwd_bwd(h2, wf, tgt, ts):
    s = h2.shape[0]

    def body(h_ref, w_ref, t_ref, loss_ref, dh_ref, dhb_ref, dw_ref):
        loss, vjp = jax.vjp(_loss_rows, h_ref[...], w_ref[...], t_ref[...])
        dh, dw, _ = vjp(jnp.ones((1, 1), F32))
        dh_ref[...] = dh
        dhb_ref[...] = dh.astype(BF16)

        @pl.when(pl.program_id(0) == 0)
        def _():
            loss_ref[...] = jnp.zeros_like(loss_ref)
            dw_ref[...] = jnp.zeros_like(dw_ref)

        loss_ref[...] += loss
        dw_ref[...] += dw

    row = pl.BlockSpec((ts, D), lambda i: (i, 0))
    vec = pl.BlockSpec((1, D), lambda i: (0, 0))
    tile = pl.BlockSpec((8, 128), lambda i: (0, 0))
    return pl.pallas_call(body, grid=(s // ts,), in_specs=[row, vec, row], out_specs=[tile, row, row, vec],
                          out_shape=[_sds((8, 128)), _sds((s, D)), _sds((s, D), BF16), _sds((1, D))], name="final_norm_loss",
                          compiler_params=_params(("arbitrary",), 12 * ts * D * 4))(h2, wf, tgt)


def _rope_tables(s):
    inv = ROPE_BASE ** (-jnp.arange(0, DK, 2, dtype=F32) / DK)
    ang = jnp.arange(s, dtype=F32)[:, None] * inv[None, :]
    cos, sin = jnp.cos(ang), jnp.sin(ang)
    return jnp.concatenate([cos, cos], axis=1), jnp.concatenate([-sin, sin], axis=1)


def _local_step(x, tgt, w_in, w_out, w_gate, w_up, w_down, norm1_w, conv_w, a_log, dt_bias, gdn_norm_w, ret_norm_w, norm2_w, norm_f_w,
                dist=None):
    s = x.shape[0]
    ts = min(512, s)
    cs, sn = _rope_tables(s)
    lgtab = jnp.log1p(-jnp.exp2(-5.0 - jnp.arange(H, dtype=F32))).reshape(1, H)

    u, u_t = _rms_fwd(x, norm1_w, ts, "norm1_fwd")
    if dist is None:
        proj = _matmul(u, w_in, tm=1024, tn=1280, tk=D, name="in_proj")
    else:
        proj, gathered = _matmul(u, w_in, tm=1024, tn=1280, tk=D, side=_gather_side(dist["shards"]), name="in_proj")
        w_out, w_gate, w_up, w_down = (gathered[0].reshape(D, D), _from_slots_cols(gathered[1]), _from_slots_cols(gathered[2]),
                                       gathered[3].reshape(-1, D))
    qk, va, c_qk, c_v = _gdn_qkv_fwd(proj, conv_w, ts)
    beta, gc, gl = _bg_fwd(proj, a_log, dt_bias, ts)
    inter = _gdn_prep_fwd(qk, va, beta, gc, gl)
    oa, st_a = _gdn_scan_fwd(*inter)
    ob, st_b = _ret_scan_fwd(proj, cs, sn, lgtab)
    mixed, mixed_t = _merge_fwd(oa, ob, proj, gdn_norm_w, ret_norm_w, min(2 * ts, s))
    h1 = _matmul(mixed, w_out, tm=1024, tn=1024, tk=D, res=x, name="out_proj")
    hn, hn_t = _rms_fwd(h1, norm2_w, ts, "norm2_fwd")
    act_dg, act_du, act, act_t = _ffn_gate_up(hn, w_gate, w_up, 512, 1408)
    h2 = _matmul(act, w_down, tm=512, tn=1024, tk=5632, res=h1, name="ffn_down")
    loss, dh2, dh2_b, d_norm_f = _loss_fwd_bwd(h2, norm_f_w, tgt, ts)

    dhg, dhu = _ffn_down_dx(dh2_b, w_down, act_dg, act_du, 512, 1408)
    g_down = _matmul(act_t, dh2_b, tm=512, tn=512, tk=8192, out_dtype=BF16, name="ffn_down_dw")
    g_gate = _matmul(hn_t, dhg, tm=512, tn=512, tk=8192, out_dtype=BF16, name="ffn_gate_dw")
    g_up = _matmul(hn_t, dhu, tm=512, tn=512, tk=8192, out_dtype=BF16, name="ffn_up_dw")
    dhn = _matmul(dhg, w_gate, tb=True, tm=512, tn=1024, tk=5632, name="ffn_gate_dx")
    dhn = _matmul(dhu, w_up, tb=True, tm=512, tn=1024, tk=5632, res=dhn, name="ffn_up_dx")
    dh1, d_norm2 = _rms_bwd(h1, norm2_w, dhn, dh2, ts, "norm2_bwd")

    g_out = _matmul(mixed_t, dh1, tm=1024, tn=1024, tk=D, out_dtype=BF16, name="out_proj_dw")
    early = ["w_out", "w_gate", "w_up", "w_down"]
    if dist is None:
        dmixed = _matmul(dh1, w_out, tb=True, tm=1024, tn=1024, tk=D, out_dtype=BF16, name="out_proj_dx")
    else:
        slots = dict(w_out=g_out.reshape(NDEV, D // NDEV, D), w_gate=_to_slots_cols(g_gate), w_up=_to_slots_cols(g_up),
                     w_down=g_down.reshape(NDEV, -1, D))
        dmixed, from_sibling = _matmul(dh1, w_out, tb=True, tm=1024, tn=1024, tk=D, out_dtype=BF16,
                                       side=_sibling_side([slots[k] for k in early]), name="out_proj_dx")
        parts = [_add_sibling(slots[k], r, dist["core"], 128, "grads_add_" + k) for k, r in zip(early, from_sibling)]
    doa, dob, dproj, d_gdn_norm, d_ret_norm = _merge_bwd(oa, ob, proj, gdn_norm_w, ret_norm_w, dmixed, ts)

    dproj = _ret_scan_bwd(proj, cs, sn, lgtab, st_b, dob, dproj)
    d_inter = _gdn_scan_bwd(*inter, st_a, doa)
    dqk, dva, dbeta_h, dgc_h, dgl_h = _gdn_prep_bwd(qk, va, beta, gc, gl, *d_inter)
    dproj, d_conv = _gdn_qkv_bwd(proj, conv_w, c_qk, c_v, dqk, dva, dproj, ts)
    dproj, d_a_log, d_dt_bias = _bg_bwd(proj, a_log, dt_bias, dbeta_h, dgc_h, dgl_h, dproj, ts)

    if dist is None:
        g_in = _matmul(u_t, dproj, tm=1024, tn=1280, tk=D, out_dtype=BF16, name="in_proj_dw")
        du = _matmul(dproj, w_in, tb=True, tm=1024, tn=1024, tk=1664, name="in_proj_dx")
        big = dict(w_in=g_in, w_out=g_out, w_gate=g_gate, w_up=g_up, w_down=g_down)
    else:
        g_in, from_chips = _matmul(u_t, dproj, tm=1024, tn=1280, tk=D, out_dtype=BF16, side=_chips_side(parts), name="in_proj_dw")
        du, (from_all,) = _matmul(dproj, w_in, tb=True, tm=1024, tn=1024, tk=1664,
                                  side=_all_to_all_side(_windows_from_layout(g_in)), name="in_proj_dx")
        big = dict(w_in=from_all, **{k: (p, r) for k, p, r in zip(early, parts, from_chips)})
    dx, d_norm1 = _rms_bwd(x, norm1_w, du, dh1, ts, "norm1_bwd")

    small = dict(norm1_w=d_norm1, conv_w=d_conv, a_log=d_a_log, dt_bias=d_dt_bias, gdn_norm_w=d_gdn_norm,
                 ret_norm_w=d_ret_norm, norm2_w=d_norm2, norm_f_w=d_norm_f)
    return loss, dx, big, small


def _coords():
    return lax.axis_index("x"), lax.axis_index("y"), lax.axis_index("c")


def _gather_side(shards):
    n = len(shards)

    def plan(ins, outs, send_sems, recv_sems, local_sems):
        x, y, c = _coords()
        me, sibling = (x, y, c), (x, y, 1 - c)
        chips = [(1 - x, y), (x, 1 - y), (1 - x, 1 - y)]

        def copy(a, k, block, to, src=None):
            px, py, pc = block
            dst = outs[a].at[4 * px + 2 * py + pc]
            return pltpu.make_async_remote_copy(src_ref=dst if src is None else src, dst_ref=dst, send_sem=send_sems.at[a, k],
                                                recv_sem=recv_sems.at[a, k], device_id=to, device_id_type=MESH)

        mine = [pltpu.make_async_copy(ins[a], outs[a].at[4 * x + 2 * y + c], local_sems.at[a]) for a in range(n)]
        first = []
        for a in range(n):
            first.append(copy(a, 0, me, sibling, src=ins[a]))
            first += [copy(a, 1 + j, me, (*chip, c), src=ins[a]) for j, chip in enumerate(chips)]
        return c, me, sibling, chips, copy, mine, first

    def start(ins, outs, *sems):
        *_, mine, first = plan(ins, outs, *sems)
        for cp in mine + first:
            cp.start()

    def finish(ins, outs, *sems):
        c, me, sibling, chips, copy, mine, first = plan(ins, outs, *sems)
        passed = []
        for j, chip in enumerate(chips):
            for a in range(n):
                copy(a, 1 + j, (*chip, c), me).wait_recv()
                fwd = copy(a, 4 + j, (*chip, c), sibling)
                fwd.start()
                passed.append(fwd)
        for a in range(n):
            copy(a, 0, sibling, me).wait_recv()
            for j, chip in enumerate(chips):
                copy(a, 4 + j, (*chip, 1 - c), me).wait_recv()
        for cp in first + passed:
            cp.wait_send()
        for cp in mine:
            cp.wait()

    return _Side(shards, [_sds((NDEV,) + a.shape, a.dtype) for a in shards], [(n, 7), (n, 7), (n,)], start, finish)


def _exchange_side(ins, n_out, copies_of):
    def start(in_refs, out_refs, *sems):
        for cp in copies_of(in_refs, out_refs, *sems):
            cp.start()

    def finish(in_refs, out_refs, *sems):
        for cp in copies_of(in_refs, out_refs, *sems):
            cp.wait()

    n = len(ins)
    return _Side(ins, [_sds((n_out,) + a.shape[1:], a.dtype) for a in ins], [(n, n_out), (n, n_out)], start, finish)


def _sibling_side(slots):
    def copies_of(ins, outs, send_sems, recv_sems):
        x, y, c = _coords()
        return [pltpu.make_async_remote_copy(
            src_ref=ins[a].at[2 * j + (1 - c)], dst_ref=outs[a].at[j], send_sem=send_sems.at[a, j], recv_sem=recv_sems.at[a, j],
            device_id=(x, y, 1 - c), device_id_type=MESH) for a in range(len(slots)) for j in range(4)]

    return _exchange_side(slots, 4, copies_of)


def _chips_side(parts):
    def copies_of(ins, outs, send_sems, recv_sems):
        x, y, c = _coords()
        chips = [(1 - x, y), (x, 1 - y), (1 - x, 1 - y)]
        return [pltpu.make_async_remote_copy(
            src_ref=ins[a].at[2 * px + py], dst_ref=outs[a].at[k], send_sem=send_sems.at[a, k], recv_sem=recv_sems.at[a, k],
            device_id=(px, py, c), device_id_type=MESH) for a in range(len(parts)) for k, (px, py) in enumerate(chips)]

    return _exchange_side(parts, 3, copies_of)


def _all_to_all_side(slots):
    def plan(ins, outs, send_sems, recv_sems, local_sems):
        x, y, c = _coords()
        mine = 4 * x + 2 * y + c
        own = pltpu.make_async_copy(ins[0].at[mine], outs[0].at[mine], local_sems.at[0])
        remote = []
        for r in range(1, NDEV):
            peer = (x ^ (r >> 2), y ^ ((r >> 1) & 1), c ^ (r & 1))
            remote.append(pltpu.make_async_remote_copy(
                src_ref=ins[0].at[mine ^ r], dst_ref=outs[0].at[mine], send_sem=send_sems.at[r - 1], recv_sem=recv_sems.at[r - 1],
                device_id=peer, device_id_type=MESH))
        return own, remote

    def start(ins, outs, *sems):
        own, remote = plan(ins, outs, *sems)
        for cp in [own] + remote:
            cp.start()

    def finish(ins, outs, *sems):
        own, remote = plan(ins, outs, *sems)
        for cp in remote:
            cp.wait()
        own.wait()

    return _Side([slots], [_sds(slots.shape, slots.dtype)], [(NDEV - 1,), (NDEV - 1,), (1,)], start, finish)


def _allreduce_small(pack, name):
    rows, cols = pack.shape

    def body(in_ref, out_ref, buf_ref, send_sems, recv_sems):
        x, y, c = _coords()
        mine = 4 * x + 2 * y + c
        buf_ref[mine] = in_ref[...]
        copies = []
        for r in range(1, NDEV):
            peer = (x ^ (r >> 2), y ^ ((r >> 1) & 1), c ^ (r & 1))
            copies.append(pltpu.make_async_remote_copy(
                src_ref=in_ref, dst_ref=buf_ref.at[mine], send_sem=send_sems.at[r - 1], recv_sem=recv_sems.at[r - 1],
                device_id=peer, device_id_type=MESH))
        for cp in copies:
            cp.start()
        for r in range(1, NDEV):
            pltpu.make_async_remote_copy(
                src_ref=in_ref, dst_ref=buf_ref.at[mine ^ r], send_sem=send_sems.at[r - 1], recv_sem=recv_sems.at[r - 1],
                device_id=(x, y, c), device_id_type=MESH).wait_recv()
        for cp in copies:
            cp.wait_send()
        acc = buf_ref[0]
        for d in range(1, NDEV):
            acc = acc + buf_ref[d]
        out_ref[...] = acc

    return pl.pallas_call(
        body, in_specs=[VMEM_FULL], out_specs=VMEM_FULL, out_shape=_sds((rows, cols)),
        scratch_shapes=[pltpu.VMEM((NDEV, rows, cols), F32), pltpu.SemaphoreType.DMA((NDEV - 1,)), pltpu.SemaphoreType.DMA((NDEV - 1,))],
        name=name)(pack)


def _add_sibling(slots, recv, core, tr, name):
    _, rows, cols = slots.shape
    tr = _row_tile(rows, tr)

    def body(c_ref, a_ref, b_ref, o_ref):
        o_ref[...] = (a_ref[...].astype(F32) + b_ref[...].astype(F32)).astype(BF16)

    gs = pltpu.PrefetchScalarGridSpec(
        num_scalar_prefetch=1, grid=(4, rows // tr),
        in_specs=[pl.BlockSpec((None, tr, cols), lambda j, i, cr: (2 * j + cr[0], i, 0)),
                  pl.BlockSpec((None, tr, cols), lambda j, i, cr: (j, i, 0))],
        out_specs=pl.BlockSpec((None, tr, cols), lambda j, i, cr: (j, i, 0)))
    return pl.pallas_call(body, grid_spec=gs, out_shape=_sds((4, rows, cols), BF16), name=name,
                          compiler_params=_params(("parallel", "parallel"), 6 * tr * cols * 4))(core, slots, recv)


def _adam_math(w, g, m, v):
    m2 = B1 * m + (1.0 - B1) * g
    v2 = B2 * v + (1.0 - B2) * jnp.square(g)
    m_hat = m2 / (1.0 - B1 ** STEP)
    v_hat = v2 / (1.0 - B2 ** STEP)
    return -LR * (m_hat / (jnp.sqrt(v_hat) + EPS_ADAM) + WD * w), m2, v2


def _adamw_reduced(part, recv, chip, w, m, v, tr, name):
    rows, cols = w.shape
    tr = _row_tile(rows, tr)

    def body(j_ref, p_ref, r0_ref, r1_ref, r2_ref, w_ref, m_ref, v_ref, g_ref, d_ref, nm_ref, nv_ref):
        g = p_ref[...].astype(F32) + r0_ref[...].astype(F32) + r1_ref[...].astype(F32) + r2_ref[...].astype(F32)
        d, m2, v2 = _adam_math(w_ref[...], g, m_ref[...], v_ref[...])
        g_ref[...] = g
        d_ref[...] = d
        nm_ref[...] = m2
        nv_ref[...] = v2

    flat = pl.BlockSpec((tr, cols), lambda i, jr: (i, 0))
    gs = pltpu.PrefetchScalarGridSpec(
        num_scalar_prefetch=1, grid=(rows // tr,),
        in_specs=[pl.BlockSpec((None, tr, cols), lambda i, jr: (jr[0], i, 0))]
        + [pl.BlockSpec((None, tr, cols), functools.partial(lambda i, jr, k: (k, i, 0), k=k)) for k in range(3)] + [flat] * 3,
        out_specs=[flat] * 4)
    return pl.pallas_call(body, grid_spec=gs, out_shape=[_sds((rows, cols))] * 4, name=name,
                          compiler_params=_params(("parallel",), 22 * tr * cols * 4))(chip, part, recv, recv, recv, w, m, v)


def _sum_slots(recv, tr, name):
    _, rows, cols = recv.shape
    tr = _row_tile(rows, tr)

    def body(*refs):
        acc = refs[0][...].astype(F32)
        for p_ref in refs[1:NDEV]:
            acc = acc + p_ref[...].astype(F32)
        refs[NDEV][...] = acc

    slot = [pl.BlockSpec((None, tr, cols), functools.partial(lambda i, k: (k, i, 0), k=k)) for k in range(NDEV)]
    return pl.pallas_call(body, grid=(rows // tr,), in_specs=slot, out_specs=pl.BlockSpec((tr, cols), lambda i: (i, 0)),
                          out_shape=_sds((rows, cols)), name=name, compiler_params=_params(("parallel",)))(*([recv] * NDEV))


def _adamw_rows(w, g, m, v, tr, name):
    rows, cols = w.shape
    tr = _row_tile(rows, tr)

    def body(w_ref, g_ref, m_ref, v_ref, d_ref, nm_ref, nv_ref):
        d_ref[...], nm_ref[...], nv_ref[...] = _adam_math(w_ref[...], g_ref[...], m_ref[...], v_ref[...])

    flat = pl.BlockSpec((tr, cols), lambda i: (i, 0))
    return pl.pallas_call(body, grid=(rows // tr,), in_specs=[flat] * 4, out_specs=[flat] * 3, out_shape=[_sds((rows, cols))] * 3,
                          name=name, compiler_params=_params(("parallel",)))(w, g, m, v)


def _adamw_plain(w, g, m, v, name):
    def body(w_ref, g_ref, m_ref, v_ref, d_ref, nm_ref, nv_ref):
        d, m2, v2 = _adam_math(w_ref[...], g_ref[...], m_ref[...], v_ref[...])
        d_ref[...] = d
        nm_ref[...] = m2
        nv_ref[...] = v2

    return pl.pallas_call(body, out_shape=[_sds(w.shape)] * 3, name=name)(w, g, m, v)


def _pack_small(norm1_w, conv_w, a_log, dt_bias, gdn_norm_w, ret_norm_w, norm2_w, norm_f_w):
    misc = jnp.concatenate([gdn_norm_w.reshape(1, DV), a_log.reshape(1, H), dt_bias.reshape(1, H),
                            jnp.zeros((1, D - DV - 2 * H), F32)], axis=1)
    return jnp.concatenate([norm1_w.reshape(1, D), ret_norm_w.reshape(1, D), norm2_w.reshape(1, D), norm_f_w.reshape(1, D),
                            conv_w.reshape(8, D), misc, jnp.zeros((3, D), F32)], axis=0)


def _unpack_small(pack):
    return dict(norm1_w=pack[0:1], ret_norm_w=pack[1:2], norm2_w=pack[2:3], norm_f_w=pack[3], conv_w=pack[4:12].reshape(4, 2 * D),
                gdn_norm_w=pack[12:13, 0:DV], a_log=pack[12:13, DV:DV + H], dt_bias=pack[12:13, DV + H:DV + 2 * H])


IN_SPLITS = (4096, 2048, 8, 8, 1024, 1024, 2048, 2048, 2048, 2048)


BA_END = sum(IN_SPLITS[:4])
LANES = 128


def _padded_order_blocks():
    z0, ba0, rq0, rk0, rv0, rg0, ga0, gb0 = 4096, 6144, 6400, 7424, 8448, 10496, 12544, 14592
    cols = []
    for h in range(H):
        for base in (z0, rg0, ga0, gb0):
            cols += [base + DV * h, base + DV * h + LANES]
    cols += list(range(0, z0, LANES))
    for h in range(H):
        cols += [rq0 + DK * h, rk0 + DK * h, rv0 + DV * h, rv0 + DV * h + LANES]
    cols += [ba0, ba0 + LANES]
    blocks = np.asarray(cols, np.int32) // LANES
    assert sorted(blocks.tolist()) == list(range(P_IN // LANES))
    return blocks


def _permute_blocks(x, blocks, name):
    rows, cols = x.shape

    def body(p_ref, x_ref, o_ref):
        o_ref[...] = x_ref[...]

    gs = pltpu.PrefetchScalarGridSpec(num_scalar_prefetch=1, grid=(cols // LANES,),
                                      in_specs=[pl.BlockSpec((rows, LANES), lambda j, p: (0, p[j]))],
                                      out_specs=pl.BlockSpec((rows, LANES), lambda j, p: (0, j)))
    return pl.pallas_call(body, grid_spec=gs, out_shape=_sds((rows, cols), x.dtype), name=name,
                          compiler_params=_params(("parallel",)))(jnp.asarray(blocks), x)


def _regroup_w_in(w):
    padded = jnp.concatenate([w[:, :BA_END], jnp.zeros((w.shape[0], P_IN - N_IN), w.dtype), w[:, BA_END:]], axis=1)
    return _permute_blocks(padded, _padded_order_blocks(), "w_in_to_layout")


def _ungroup_w_in(g):
    padded = _permute_blocks(g, np.argsort(_padded_order_blocks()).astype(np.int32), "w_in_grad_from_layout")
    return jnp.concatenate([padded[:, :BA_END], padded[:, BA_END + P_IN - N_IN:]], axis=1)


SHARD_W = N_IN // NDEV
GAP = P_IN - N_IN
WIN = 2304


def _padded_col(c):
    return c + (GAP if c >= BA_END else 0)


WIN_START = [min(_padded_col(SHARD_W * d) // LANES * LANES, P_IN - WIN) for d in range(NDEV)]
WIN_OFF = [_padded_col(SHARD_W * d) - WIN_START[d] for d in range(NDEV)]
STRADDLER = BA_END // SHARD_W
STRADDLE_AT = BA_END - STRADDLER * SHARD_W
assert all(WIN_OFF[d] + SHARD_W + (GAP if d == STRADDLER else 0) <= WIN for d in range(NDEV))


def _win_off(me):
    off = jnp.int32(0)
    for d in range(NDEV):
        off = jnp.where(me == d, jnp.int32(WIN_OFF[d]), off)
    return off


def _window_of_shard(shard, me):
    rows = shard.shape[0]
    zeros = lambda n: jnp.zeros((rows, n), shard.dtype)
    plain = lax.dynamic_update_slice(zeros(WIN), shard, (0, _win_off(me)))
    o = WIN_OFF[STRADDLER]
    split = jnp.concatenate([zeros(o), shard[:, :STRADDLE_AT], zeros(GAP), shard[:, STRADDLE_AT:], zeros(WIN - o - GAP - SHARD_W)], axis=1)
    return jnp.where(me == STRADDLER, split, plain)


def _shard_of_window(win, me):
    plain = lax.dynamic_slice(win, (0, _win_off(me)), (win.shape[0], SHARD_W))
    o = WIN_OFF[STRADDLER]
    split = jnp.concatenate([win[:, o:o + STRADDLE_AT], win[:, o + STRADDLE_AT + GAP:o + GAP + SHARD_W]], axis=1)
    return jnp.where(me == STRADDLER, split, plain)


def _layout_from_windows(wins):
    _, rows, _ = wins.shape
    data = []
    for d in range(NDEV):
        lo = _padded_col(SHARD_W * d)
        data.append([(lo, lo + STRADDLE_AT), (lo + STRADDLE_AT + GAP, lo + GAP + SHARD_W)] if d == STRADDLER else [(lo, lo + SHARD_W)])
    zero_block = (0, WIN // LANES - 1)
    table = []
    for p in _padded_order_blocks():
        src = [(d, int(p) - WIN_START[d] // LANES) for d in range(NDEV)
               if any(lo < (p + 1) * LANES and hi > p * LANES for lo, hi in data[d])]
        assert len(src) <= 2 and all(0 <= b < WIN // LANES for _, b in src)
        src += [zero_block] * (2 - len(src))
        table.append([src[0][0], src[0][1], src[1][0], src[1][1]])
    table = np.asarray(table, np.int32).T.copy()

    def body(t_ref, a0_ref, b0_ref, a1_ref, b1_ref, o_ref):
        o_ref[:, :LANES] = a0_ref[...] + b0_ref[...]
        o_ref[:, LANES:] = a1_ref[...] + b1_ref[...]

    src = lambda k, odd: pl.BlockSpec((None, rows, LANES), lambda j, t: (t[k, 2 * j + odd], 0, t[k + 1, 2 * j + odd]))
    gs = pltpu.PrefetchScalarGridSpec(
        num_scalar_prefetch=1, grid=(P_IN // (2 * LANES),), in_specs=[src(0, 0), src(2, 0), src(0, 1), src(2, 1)],
        out_specs=pl.BlockSpec((rows, 2 * LANES), lambda j, t: (0, j)))
    return pl.pallas_call(body, grid_spec=gs, out_shape=_sds((rows, P_IN), wins.dtype), name="w_in_from_windows",
                          compiler_params=_params(("parallel",)))(jnp.asarray(table), wins, wins, wins, wins)


def _windows_from_layout(g):
    rows = g.shape[0]
    where = np.argsort(_padded_order_blocks())
    nb = WIN // LANES
    table = np.asarray([where[WIN_START[d] // LANES + b] for d in range(NDEV) for b in range(nb)], np.int32)

    def body(t_ref, x0_ref, x1_ref, o_ref):
        o_ref[:, :LANES] = x0_ref[...]
        o_ref[:, LANES:] = x1_ref[...]

    src = lambda odd: pl.BlockSpec((rows, LANES), lambda d, b, t: (0, t[d * nb + 2 * b + odd]))
    gs = pltpu.PrefetchScalarGridSpec(num_scalar_prefetch=1, grid=(NDEV, nb // 2), in_specs=[src(0), src(1)],
                                      out_specs=pl.BlockSpec((None, rows, 2 * LANES), lambda d, b, t: (d, 0, b)))
    return pl.pallas_call(body, grid_spec=gs, out_shape=_sds((NDEV, rows, WIN), g.dtype), name="w_in_grad_windows",
                          compiler_params=_params(("parallel", "parallel")))(jnp.asarray(table), g, g)


def _to_slots_cols(g):
    rows, cols = g.shape
    return g.reshape(rows, NDEV, cols // NDEV).transpose(1, 0, 2)


def _from_slots_cols(a):
    n, rows, cols = a.shape
    return a.transpose(1, 0, 2).reshape(rows, n * cols)


WEIGHT_ORDER = ["norm1_w", "w_in", "conv_w", "a_log", "dt_bias", "gdn_norm_w", "ret_norm_w", "w_out", "norm2_w", "w_gate", "w_up",
                "w_down", "norm_f_w"]


def kernel(x, norm1_w, w_in, conv_w, a_log, dt_bias, gdn_norm_w, ret_norm_w, w_out, norm2_w, w_gate, w_up, w_down, norm_f_w, loss_target, m_norm1_w, m_w_in, m_conv_w, m_a_log, m_dt_bias, m_gdn_norm_w, m_ret_norm_w, m_w_out, m_norm2_w, m_w_gate, m_w_up, m_w_down, m_norm_f_w, v_norm1_w, v_w_in, v_conv_w, v_a_log, v_dt_bias, v_gdn_norm_w, v_ret_norm_w, v_w_out, v_norm2_w, v_w_gate, v_w_up, v_w_down, v_norm_f_w):
    ax, ay, ac = _coords()
    me = 4 * ax + 2 * ay + ac
    core = jnp.reshape(ac, (1,)).astype(jnp.int32)
    chip = jnp.reshape(2 * ax + ay, (1,)).astype(jnp.int32)
    w = dict(norm1_w=norm1_w, w_in=w_in[0], conv_w=conv_w[0], a_log=a_log, dt_bias=dt_bias, gdn_norm_w=gdn_norm_w,
             ret_norm_w=ret_norm_w, w_out=w_out[0], norm2_w=norm2_w, w_gate=w_gate[0], w_up=w_up[0], w_down=w_down[0],
             norm_f_w=norm_f_w)
    m = dict(norm1_w=m_norm1_w, w_in=m_w_in[0], conv_w=m_conv_w[0], a_log=m_a_log, dt_bias=m_dt_bias, gdn_norm_w=m_gdn_norm_w,
             ret_norm_w=m_ret_norm_w, w_out=m_w_out[0], norm2_w=m_norm2_w, w_gate=m_w_gate[0], w_up=m_w_up[0], w_down=m_w_down[0],
             norm_f_w=m_norm_f_w)
    v = dict(norm1_w=v_norm1_w, w_in=v_w_in[0], conv_w=v_conv_w[0], a_log=v_a_log, dt_bias=v_dt_bias, gdn_norm_w=v_gdn_norm_w,
             ret_norm_w=v_ret_norm_w, w_out=v_w_out[0], norm2_w=v_norm2_w, w_gate=v_w_gate[0], w_up=v_w_up[0], w_down=v_w_down[0],
             norm_f_w=v_norm_f_w)
    big_names = ["w_in", "w_out", "w_gate", "w_up", "w_down"]

    w_in_wins, conv_all = _run_side(_gather_side([_window_of_shard(w["w_in"].astype(BF16), me), w["conv_w"]]), "w_in_allgather")
    w_in_full = _layout_from_windows(w_in_wins)
    conv_full = _from_slots_cols(conv_all)
    dist = dict(core=core, shards=[w[k].astype(BF16) for k in ("w_out", "w_gate", "w_up", "w_down")])

    loss_tile, dx, big, small = _local_step(
        x[0], loss_target[0], w_in_full, None, None, None, None, norm1_w, conv_full, a_log, dt_bias,
        gdn_norm_w, ret_norm_w, norm2_w, norm_f_w.reshape(1, D), dist=dist)
    loss = lax.psum(loss_tile[0, 0], ("x", "y", "c"))

    g_w_in = _shard_of_window(_sum_slots(big["w_in"], 64, "w_in_grad_sum"), me)
    out = {"w_in": (g_w_in, *_adamw_rows(w["w_in"], g_w_in, m["w_in"], v["w_in"], 64, "adamw_w_in"))}
    for k in ("w_out", "w_gate", "w_up", "w_down"):
        part, recv = big[k]
        out[k] = _adamw_reduced(part, recv, chip, w[k], m[k], v[k], 128, "adamw_" + k)

    g_small = _unpack_small(_allreduce_small(_pack_small(**small), "small_grads_allreduce"))
    g_small["conv_w"] = lax.dynamic_slice_in_dim(g_small["conv_w"], me * (2 * D // NDEV), 2 * D // NDEV, axis=1)
    small_names = [k for k in WEIGHT_ORDER if k not in big_names]
    pad_conv = lambda a: jnp.pad(a, ((0, 0), (0, 2 * D - a.shape[1])))
    packs = []
    for src in (w, g_small, m, v):
        args = {k: (pad_conv(src[k]) if k == "conv_w" else src[k]) for k in small_names}
        packs.append(_pack_small(**args))
    d_pack, m_pack, v_pack = _adamw_plain(*packs[0:1], packs[1], packs[2], packs[3], name="adamw_small")
    cut_conv = lambda dct: {**dct, "conv_w": dct["conv_w"][:, :2 * D // NDEV]}
    d_small, m_small, v_small = (cut_conv(_unpack_small(p)) for p in (d_pack, m_pack, v_pack))

    def shaped(k, a):
        return a.reshape(w_shapes[k])

    w_shapes = dict(norm1_w=norm1_w.shape, w_in=w_in.shape, conv_w=conv_w.shape, a_log=a_log.shape, dt_bias=dt_bias.shape,
                    gdn_norm_w=gdn_norm_w.shape, ret_norm_w=ret_norm_w.shape, w_out=w_out.shape, norm2_w=norm2_w.shape,
                    w_gate=w_gate.shape, w_up=w_up.shape, w_down=w_down.shape, norm_f_w=norm_f_w.shape)
    grads, deltas, new_m, new_v = [], [], [], []
    for k in WEIGHT_ORDER:
        if k in big_names:
            g_, d_, m_, v_ = out[k]
        else:
            g_, d_, m_, v_ = g_small[k], d_small[k], m_small[k], v_small[k]
        grads.append(shaped(k, g_))
        deltas.append(shaped(k, d_))
        new_m.append(shaped(k, m_))
        new_v.append(shaped(k, v_))
    return (loss, dx[None], *grads, *deltas, *new_m, *new_v)
```

```python
import functools
import numpy as np
import jax
import jax.numpy as jnp
from jax import lax
from jax.experimental import pallas as pl
from jax.experimental.pallas import tpu as pltpu

F32, BF16 = jnp.float32, jnp.bfloat16
HI = lax.Precision.HIGHEST
MESH = pl.DeviceIdType.MESH
ANY = pl.BlockSpec(memory_space=pl.ANY)
VMEM_FULL = pl.BlockSpec(memory_space=pltpu.VMEM)

NDEV = 8
D = 2048
H = 8
DK = 128
DV = 256
C = 64
CPB = 4
EPS = 1e-6
ROPE_BASE = 10000.0
N_IN = 16400
O_MERGE, O_QKV, O_RET, O_BA, P_IN = 0, 8192, 12288, 16384, 16640
MERGE_W, RET_W = 4 * DV, 2 * DK + DV
LR, B1, B2, EPS_ADAM, WD, STEP = 0.001, 0.9, 0.999, 1e-08, 0.01, 10
VMEM_CAP = 60 * 1024 * 1024

NN = ((1,), (0,))
NT = ((1,), (1,))
TN = ((0,), (0,))


def _params(sem=None, est=None):
    kw = {}
    if sem is not None:
        kw["dimension_semantics"] = sem
    if est is not None:
        kw["vmem_limit_bytes"] = int(min(VMEM_CAP, max(32 * 1024 * 1024, est * 5 // 4 + (4 << 20))))
    return pltpu.CompilerParams(**kw)


def _sds(shape, dt=F32):
    return jax.ShapeDtypeStruct(tuple(shape), dt)


def _row_tile(rows, limit):
    return max(t for t in range(16, min(rows, limit) + 1, 16) if rows % t == 0)


def _bdot(a, b, dims):
    return lax.dot_general(a.astype(BF16), b.astype(BF16), (dims, ((), ())), preferred_element_type=F32)


def _hdot(a, b, dims):
    return lax.dot_general(a, b, (dims, ((), ())), precision=HI, preferred_element_type=F32)


_sigmoid = jax.nn.sigmoid


def _silu(x):
    return x * _sigmoid(x)


def _rms(x, w):
    return x * lax.rsqrt(jnp.mean(x * x, axis=-1, keepdims=True) + EPS) * w


class _Side:
    def __init__(self, ins, out_shapes, sems, start, finish):
        self.ins, self.out_shapes, self.sems, self.start, self.finish = list(ins), list(out_shapes), list(sems), start, finish


def _run_side(side, name):
    ni, no = len(side.ins), len(side.out_shapes)

    def body(*refs):
        ins, outs, sems = refs[:ni], refs[ni:ni + no], refs[ni + no:]
        side.start(ins, outs, *sems)
        side.finish(ins, outs, *sems)

    return pl.pallas_call(body, in_specs=[ANY] * ni, out_specs=[ANY] * no, out_shape=side.out_shapes,
                          scratch_shapes=[pltpu.SemaphoreType.DMA(s) for s in side.sems], name=name)(*side.ins)


def _matmul(a, b, *, ta=False, tb=False, tm, tn, tk, out_dtype=F32, res=None, side=None, name):
    m = a.shape[1] if ta else a.shape[0]
    k = a.shape[0] if ta else a.shape[1]
    n = b.shape[0] if tb else b.shape[1]
    assert k == (b.shape[1] if tb else b.shape[0])
    tm, tn, tk = min(tm, m), min(tn, n), min(tk, k)
    assert m % tm == 0 and n % tn == 0 and k % tk == 0, (name, m, n, k, tm, tn, tk)
    nk = k // tk
    dims = ((0 if ta else 1,), (1 if tb else 0,))
    has_res = res is not None
    n_in = 3 if has_res else 2
    n_side_in = len(side.ins) if side else 0
    n_side_out = len(side.out_shapes) if side else 0
    grid = (m // tm, n // tn, nk)

    def body(*refs):
        a_ref, b_ref = refs[0], refs[1]
        r_ref = refs[2] if has_res else None
        o_ref = refs[n_in + n_side_in]
        if side:
            side_ins = refs[n_in:n_in + n_side_in]
            side_outs = refs[n_in + n_side_in + 1:n_in + n_side_in + 1 + n_side_out]
            side_sems = refs[len(refs) - len(side.sems):]
            step = (pl.program_id(0) * grid[1] + pl.program_id(1)) * grid[2] + pl.program_id(2)

            @pl.when(step == 0)
            def _():
                side.start(side_ins, side_outs, *side_sems)

        def finish(acc):
            if has_res:
                acc = acc + r_ref[...].astype(F32)
            o_ref[...] = acc.astype(out_dtype)

        part = _bdot(a_ref[...], b_ref[...], dims)
        if nk == 1:
            finish(part)
        else:
            acc_ref = refs[n_in + n_side_in + 1 + n_side_out]
            kk = pl.program_id(2)

            @pl.when(kk == 0)
            def _():
                acc_ref[...] = part

            @pl.when(kk > 0)
            def _():
                acc_ref[...] += part

            @pl.when(kk == nk - 1)
            def _():
                finish(acc_ref[...])

        if side:
            @pl.when(step == grid[0] * grid[1] * grid[2] - 1)
            def _():
                side.finish(side_ins, side_outs, *side_sems)

    a_spec = pl.BlockSpec((tk, tm), lambda i, j, kk: (kk, i)) if ta else pl.BlockSpec((tm, tk), lambda i, j, kk: (i, kk))
    b_spec = pl.BlockSpec((tn, tk), lambda i, j, kk: (j, kk)) if tb else pl.BlockSpec((tk, tn), lambda i, j, kk: (kk, j))
    o_spec = pl.BlockSpec((tm, tn), lambda i, j, kk: (i, j))
    in_specs = [a_spec, b_spec] + ([o_spec] if has_res else []) + [ANY] * n_side_in
    est = 2 * (tm * tk * a.dtype.itemsize + tk * tn * b.dtype.itemsize + tm * tn * jnp.dtype(out_dtype).itemsize)
    est += 2 * tm * tn * 4 * (1 if has_res else 0) + (tm * tn * 4 if nk > 1 else 0) + 2 * tm * tn * 4
    args = (a, b) + ((res,) if has_res else ()) + (tuple(side.ins) if side else ())
    scratch = ([pltpu.VMEM((tm, tn), F32)] if nk > 1 else []) + ([pltpu.SemaphoreType.DMA(s) for s in side.sems] if side else [])
    sem = ("arbitrary",) * 3 if side else ("parallel", "parallel", "arbitrary")
    out = pl.pallas_call(
        body, grid=grid, in_specs=in_specs, out_specs=[o_spec] + [ANY] * n_side_out,
        out_shape=[_sds((m, n), out_dtype)] + (side.out_shapes if side else []),
        scratch_shapes=scratch, name=name, compiler_params=_params(sem, est))(*args)
    return (out[0], out[1:]) if side else out[0]


def _rms_fwd(x, w, ts, name):
    s = x.shape[0]

    def body(x_ref, w_ref, o_ref, ot_ref):
        y = _rms(x_ref[...], w_ref[...]).astype(BF16)
        o_ref[...] = y
        ot_ref[...] = y.T

    row = pl.BlockSpec((ts, D), lambda i: (i, 0))
    return pl.pallas_call(body, grid=(s // ts,), in_specs=[row, pl.BlockSpec((1, D), lambda i: (0, 0))],
                          out_specs=[row, pl.BlockSpec((D, ts), lambda i: (0, i))],
                          out_shape=[_sds((s, D), BF16), _sds((D, s), BF16)], name=name,
                          compiler_params=_params(("parallel",)))(x, w)


def _rms_bwd(x, w, du, dres, ts, name):
    s = x.shape[0]

    def body(x_ref, w_ref, du_ref, dres_ref, dx_ref, dw_ref):
        _, vjp = jax.vjp(_rms, x_ref[...], w_ref[...])
        dx, dw = vjp(du_ref[...].astype(F32))
        dx_ref[...] = dx + dres_ref[...]

        @pl.when(pl.program_id(0) == 0)
        def _():
            dw_ref[...] = jnp.zeros_like(dw_ref)

        dw_ref[...] += dw

    row = pl.BlockSpec((ts, D), lambda i: (i, 0))
    vec = pl.BlockSpec((1, D), lambda i: (0, 0))
    return pl.pallas_call(body, grid=(s // ts,), in_specs=[row, vec, row, row], out_specs=[row, vec],
                          out_shape=[_sds((s, D)), _sds((1, D))], name=name,
                          compiler_params=_params(("arbitrary",), 12 * ts * D * 4))(x, w, du, dres)


def _conv_taps(xx, w, base, ts):
    acc = xx[base:base + ts] * w[0:1, :]
    for j in range(1, 4):
        acc = acc + xx[base + j:base + j + ts] * w[j:j + 1, :]
    return acc


def _causal_conv(prev8, cur, w, first):
    xx = jnp.concatenate([jnp.where(first, 0.0, prev8), cur], axis=0)
    return _conv_taps(xx, w, 5, cur.shape[0])


def _qk_post(c, scale):
    s = _silu(c)
    return s * lax.rsqrt(jnp.sum(s * s, axis=-1, keepdims=True) + EPS) * scale


def _conv_specs(ts, cw, col0):
    pcol = O_QKV // cw + col0
    cur = pl.BlockSpec((ts, cw), lambda j, i: (i, pcol + j))
    prev = pl.BlockSpec((8, cw), lambda j, i: (jnp.maximum(i * (ts // 8) - 1, 0), pcol + j))
    wsp = pl.BlockSpec((4, cw), lambda j, i: (0, col0 + j))
    return cur, prev, wsp


def _gdn_qkv_fwd(proj, conv_w, ts):
    s = proj.shape[0]

    def qk_body(cur_ref, prev_ref, w_ref, o_ref, c_ref):
        c = _causal_conv(prev_ref[...], cur_ref[...], w_ref[...], pl.program_id(1) == 0)
        scale = jnp.where(pl.program_id(0) < H, DK ** -0.5, 1.0).astype(F32)
        c_ref[...] = c
        o_ref[...] = _qk_post(c, scale)

    tq = min(2 * ts, s)
    cur, prev, wsp = _conv_specs(tq, DK, 0)
    out = pl.BlockSpec((tq, DK), lambda j, i: (i, j))
    qk, c_qk = pl.pallas_call(qk_body, grid=(2 * H, s // tq), in_specs=[cur, prev, wsp], out_specs=[out, out],
                              out_shape=[_sds((s, 2 * H * DK))] * 2, name="gdn_qk_prep",
                              compiler_params=_params(("parallel", "parallel")))(proj, proj, conv_w)

    def v_body(cur_ref, prev_ref, w_ref, o_ref, c_ref):
        c = _causal_conv(prev_ref[...], cur_ref[...], w_ref[...], pl.program_id(1) == 0)
        c_ref[...] = c
        o_ref[...] = _silu(c)

    cw = 512
    cur, prev, wsp = _conv_specs(ts, cw, 2 * H * DK // cw)
    out = pl.BlockSpec((ts, cw), lambda j, i: (i, j))
    v, c_v = pl.pallas_call(v_body, grid=(H * DV // cw, s // ts), in_specs=[cur, prev, wsp], out_specs=[out, out],
                            out_shape=[_sds((s, H * DV))] * 2, name="gdn_v_prep",
                            compiler_params=_params(("parallel", "parallel")))(proj, proj, conv_w)
    return qk, v, c_qk, c_v


def _gdn_qkv_bwd(proj, conv_w, c_qk, c_v, dqk, dv, dproj, ts):
    s = proj.shape[0]
    nt = s // ts

    def qk_body(c_ref, d_ref, o_ref):
        scale = jnp.where(pl.program_id(0) < H, DK ** -0.5, 1.0).astype(F32)
        _, vjp = jax.vjp(lambda cc: _qk_post(cc, scale), c_ref[...])
        o_ref[...] = vjp(d_ref[...])[0]

    tq = min(2 * ts, s)
    blk = pl.BlockSpec((tq, DK), lambda j, i: (i, j))
    dc_qk = pl.pallas_call(qk_body, grid=(2 * H, s // tq), in_specs=[blk, blk], out_specs=blk, out_shape=_sds((s, 2 * H * DK)),
                           name="gdn_qk_prep_bwd", compiler_params=_params(("parallel", "parallel")))(c_qk, dqk)

    def v_body(c_ref, d_ref, o_ref):
        _, vjp = jax.vjp(_silu, c_ref[...])
        o_ref[...] = vjp(d_ref[...])[0]

    cw = 512
    blk = pl.BlockSpec((ts, cw), lambda j, i: (i, j))
    dc_v = pl.pallas_call(v_body, grid=(H * DV // cw, nt), in_specs=[blk, blk], out_specs=blk, out_shape=_sds((s, H * DV)),
                          name="gdn_v_prep_bwd", compiler_params=_params(("parallel", "parallel")))(c_v, dv)

    def conv_bwd(dc, dproj, col0, ncols, name):
        def body(x_ref, xprev_ref, w_ref, dc_ref, dcnext_ref, _, da_ref, dw_ref):
            i = pl.program_id(1)
            w = w_ref[...]
            dcur = dc_ref[...]
            dd = jnp.concatenate([dcur, jnp.where(i == nt - 1, 0.0, dcnext_ref[...])], axis=0)
            acc = dd[3:3 + ts] * w[0:1, :]
            for j in range(1, 4):
                acc = acc + dd[3 - j:3 - j + ts] * w[j:j + 1, :]
            da_ref[...] = acc.astype(BF16)
            xx = jnp.concatenate([jnp.where(i == 0, 0.0, xprev_ref[...]), x_ref[...]], axis=0)

            @pl.when(i == 0)
            def _():
                dw_ref[...] = jnp.zeros_like(dw_ref)

            for j in range(4):
                dw_ref[j:j + 1, :] += jnp.sum(dcur * xx[5 + j:5 + j + ts], axis=0, keepdims=True)

        cur, prev, wsp = _conv_specs(ts, cw, col0)
        dcur = pl.BlockSpec((ts, cw), lambda j, i: (i, j))
        dnext = pl.BlockSpec((8, cw), lambda j, i: (jnp.minimum((i + 1) * (ts // 8), s // 8 - 1), j))
        pcol = O_QKV // cw + col0
        return pl.pallas_call(body, grid=(ncols // cw, nt), in_specs=[cur, prev, wsp, dcur, dnext, ANY],
                              out_specs=[pl.BlockSpec((ts, cw), lambda j, i: (i, pcol + j)), pl.BlockSpec((4, cw), lambda j, i: (0, j))],
                              out_shape=[_sds(dproj.shape, BF16), _sds((4, ncols))], input_output_aliases={5: 0}, name=name,
                              compiler_params=_params(("parallel", "arbitrary")))(proj, proj, conv_w, dc, dc, dproj)

    dproj, dw_qk = conv_bwd(dc_qk, dproj, 0, 2 * H * DK, "conv_bwd_qk")
    dproj, dw_v = conv_bwd(dc_v, dproj, 2 * H * DK // cw, H * DV, "conv_bwd_v")
    return dproj, jnp.concatenate([dw_qk, dw_v], axis=1)


def _bg(b, a, alog, dtb):
    n = b.shape[0]
    g = -jnp.exp(alog) * jax.nn.softplus(a + dtb)
    row = lax.broadcasted_iota(jnp.int32, (n, n), 0)
    col = lax.broadcasted_iota(jnp.int32, (n, n), 1)
    shift = C.bit_length() - 1
    same = (row >> shift) == (col >> shift)
    return _sigmoid(b), _hdot((same & (row >= col)).astype(F32), g, NN), _hdot(same.astype(F32), g, NN)


def _bg_fwd(proj, alog, dtb, ts):
    s = proj.shape[0]

    def body(ba_ref, alog_ref, dtb_ref, beta_ref, gc_ref, gl_ref):
        beta_ref[...], gc_ref[...], gl_ref[...] = _bg(ba_ref[:, 0:H], ba_ref[:, H:2 * H], alog_ref[...], dtb_ref[...])

    small = pl.BlockSpec((1, H), lambda i: (0, 0))
    out = pl.BlockSpec((ts, H), lambda i: (i, 0))
    return pl.pallas_call(body, grid=(s // ts,), in_specs=[pl.BlockSpec((ts, 256), lambda i: (i, O_BA // 256)), small, small],
                          out_specs=[out] * 3, out_shape=[_sds((s, H))] * 3, name="gdn_bg_prep",
                          compiler_params=_params(("parallel",)))(proj, alog, dtb)


def _bg_bwd(proj, alog, dtb, dbeta_h, dgc_h, dgl_h, dproj, ts):
    s = proj.shape[0]

    def body(ba_ref, alog_ref, dtb_ref, dbeta_ref, dgc_ref, dgl_ref, _, dba_ref, dalog_ref, ddtb_ref):
        _, vjp = jax.vjp(_bg, ba_ref[:, 0:H], ba_ref[:, H:2 * H], alog_ref[...], dtb_ref[...])
        db, da, dalog, ddtb = vjp((jnp.sum(dbeta_ref[...], axis=0), jnp.sum(dgc_ref[...], axis=0), jnp.sum(dgl_ref[...], axis=0)))
        dba_ref[...] = jnp.zeros_like(dba_ref)
        dba_ref[:, 0:H] = db.astype(BF16)
        dba_ref[:, H:2 * H] = da.astype(BF16)

        @pl.when(pl.program_id(0) == 0)
        def _():
            dalog_ref[...] = jnp.zeros_like(dalog_ref)
            ddtb_ref[...] = jnp.zeros_like(ddtb_ref)

        dalog_ref[...] += dalog
        ddtb_ref[...] += ddtb

    small = pl.BlockSpec((1, H), lambda i: (0, 0))
    per_head = pl.BlockSpec((H, ts, H), lambda i: (0, i, 0))
    return pl.pallas_call(body, grid=(s // ts,),
                          in_specs=[pl.BlockSpec((ts, 256), lambda i: (i, O_BA // 256)), small, small, per_head, per_head, per_head, ANY],
                          out_specs=[pl.BlockSpec((ts, 256), lambda i: (i, O_BA // 256)), small, small],
                          out_shape=[_sds(dproj.shape, BF16), _sds((1, H)), _sds((1, H))], input_output_aliases={6: 0},
                          name="gdn_bg_prep_bwd", compiler_params=_params(("arbitrary",)))(proj, alog, dtb, dbeta_h, dgc_h, dgl_h, dproj)


BLK = 4 * C
NNB, NTB, TNB = ((2,), (1,)), ((2,), (2,)), ((1,), (1,))


def _bdot_b(a, b, dims):
    return lax.dot_general(a.astype(BF16), b.astype(BF16), (dims, ((0,), (0,))), preferred_element_type=F32)


@jax.custom_vjp
def _inv_unit_lower(a):
    n = a.shape[-1]
    row = lax.broadcasted_iota(jnp.int32, (n, n), 0)
    col = lax.broadcasted_iota(jnp.int32, (n, n), 1)
    x = jnp.where(row == col, 1.0, 0.0).astype(F32) - a
    p = _bdot_b(a, a, NNB)
    power = 2
    while True:
        x = x + _bdot_b(x, p, NNB)
        power *= 2
        if power >= C:
            return x
        p = _bdot_b(p, p, NNB)


def _inv_fwd(a):
    t = _inv_unit_lower(a)
    return t, t


def _inv_bwd(t, dt):
    return (-_bdot_b(_bdot_b(t, dt, TNB), t, NTB),)


_inv_unit_lower.defvjp(_inv_fwd, _inv_bwd)


def _gdn_prep(q, k, v, bfull, gcfull, glfull, hmask):
    nb, n = q.shape[0], q.shape[1]
    beta = jnp.sum(bfull * hmask, axis=-1, keepdims=True)
    gc = jnp.sum(gcfull * hmask, axis=-1, keepdims=True)
    gl = jnp.sum(glfull * hmask, axis=-1, keepdims=True)
    row = lax.broadcasted_iota(jnp.int32, (n, n), 0)
    col = lax.broadcasted_iota(jnp.int32, (n, n), 1)
    shift = C.bit_length() - 1
    same = (row >> shift) == (col >> shift)
    incl, strict = same & (row >= col), same & (row > col)
    g_i = gc * jnp.ones((1, 1, n), F32)
    decay = jnp.exp(jnp.where(incl, g_i - jnp.swapaxes(g_i, 1, 2), -jnp.inf))
    kb = k * beta
    a = jnp.where(strict, _bdot_b(kb, k, NTB) * decay, 0.0)
    tinv = _inv_unit_lower(a)
    u = _bdot_b(tinv, v * beta, NNB)
    w = _bdot_b(tinv, kb * jnp.exp(gc), NNB)
    attn = _bdot_b(q, k, NTB) * decay
    fold = ((lax.broadcasted_iota(jnp.int32, (n, C), 0) & (C - 1)) == lax.broadcasted_iota(jnp.int32, (n, C), 1)).astype(F32)
    attn_c = _bdot(attn.reshape(nb * n, n), fold, NN).reshape(nb, n, C)
    return u, w, attn_c, q * jnp.exp(gc), k * jnp.exp(gl - gc), jnp.exp(gl)


def _gdn_step(u, w, attn, qg, kd, egl, state):
    v_new = u - _bdot_b(w, state, NNB)
    o = _bdot_b(qg, state, NNB) + _bdot_b(attn, v_new, NNB)
    return o, state * egl + _bdot_b(kd, v_new, TNB)


def _heads(ref, rs, width):
    return jnp.stack([ref[rs, h * width:(h + 1) * width] for h in range(H)])


def _head_mask(h):
    return (lax.broadcasted_iota(jnp.int32, (1, H), 1) == h).astype(F32)


PREP_BLOCKS = 2


def _gdn_prep_specs(r):
    small = pl.BlockSpec((r, H), lambda h, c: (c, 0))
    return [pl.BlockSpec((r, DK), lambda h, c: (c, h)), pl.BlockSpec((r, DK), lambda h, c: (c, H + h)),
            pl.BlockSpec((r, DV), lambda h, c: (c, h)), small, small, small]


def _blocked(ref):
    x = ref[...]
    return x.reshape(PREP_BLOCKS, BLK, x.shape[-1])


def _gdn_inter_specs(r):
    col = pl.BlockSpec((r, DK), lambda h, c: (c, h))
    return [pl.BlockSpec((r, DV), lambda h, c: (c, h)), col, pl.BlockSpec((1, r, C), lambda h, c: (h, c, 0)), col, col,
            pl.BlockSpec((1, r // C, 8, 128), lambda h, c: (h, c, 0, 0))]


def _gdn_prep_fwd(qk, v, beta, gc, gl):
    s = qk.shape[0]
    r = PREP_BLOCKS * BLK

    def body(q_ref, k_ref, v_ref, b_ref, gc_ref, gl_ref, u_ref, w_ref, attn_ref, qg_ref, kd_ref, egl_ref):
        u, w, attn, qg, kd, egl = _gdn_prep(_blocked(q_ref), _blocked(k_ref), _blocked(v_ref), _blocked(b_ref), _blocked(gc_ref),
                                            _blocked(gl_ref), _head_mask(pl.program_id(0)))
        u_ref[...] = u.reshape(r, DV)
        w_ref[...] = w.reshape(r, DK).astype(BF16)
        attn_ref[0] = attn.reshape(r, C).astype(BF16)
        qg_ref[...] = qg.reshape(r, DK).astype(BF16)
        kd_ref[...] = kd.reshape(r, DK).astype(BF16)
        egl = egl.reshape(r, 1)
        for j in range(r // C):
            egl_ref[0, j] = egl[j * C:j * C + 1, :] * jnp.ones((8, 128), F32)

    out_shape = [_sds((s, H * DV)), _sds((s, H * DK), BF16), _sds((H, s, C), BF16), _sds((s, H * DK), BF16),
                 _sds((s, H * DK), BF16), _sds((H, s // C, 8, 128))]
    return pl.pallas_call(body, grid=(H, s // r), in_specs=_gdn_prep_specs(r), out_specs=_gdn_inter_specs(r), out_shape=out_shape,
                          name="gdn_prep_fwd", compiler_params=_params(("parallel", "parallel")))(qk, qk, v, beta, gc, gl)


def _gdn_prep_bwd(qk, v, beta, gc, gl, du, dw, dattn, dqg, dkd, degl):
    s = qk.shape[0]
    r = PREP_BLOCKS * BLK

    def body(q_ref, k_ref, v_ref, b_ref, gc_ref, gl_ref, du_ref, dw_ref, dattn_ref, dqg_ref, dkd_ref, degl_ref,
             dq_ref, dk_ref, dv_ref, db_ref, dgc_ref, dgl_ref):
        hmask = _head_mask(pl.program_id(0))
        _, vjp = jax.vjp(lambda q, k, v, b, gc, gl: _gdn_prep(q, k, v, b, gc, gl, hmask), _blocked(q_ref), _blocked(k_ref),
                         _blocked(v_ref), _blocked(b_ref), _blocked(gc_ref), _blocked(gl_ref))
        rowid = lax.broadcasted_iota(jnp.int32, (r, 1), 0)
        degl = jnp.zeros((r, 1), F32)
        for j in range(r // C):
            degl = jnp.where(rowid == j * C, degl_ref[0, j, 0:1, 0:1], degl)
        dq, dk, dv, db, dgc, dgl = vjp((_blocked(du_ref), _blocked(dw_ref), _blocked(dattn_ref.at[0]), _blocked(dqg_ref),
                                        _blocked(dkd_ref), degl.reshape(PREP_BLOCKS, BLK, 1)))
        dq_ref[...] = dq.reshape(r, DK)
        dk_ref[...] = dk.reshape(r, DK)
        dv_ref[...] = dv.reshape(r, DV)
        db_ref[0] = db.reshape(r, H)
        dgc_ref[0] = dgc.reshape(r, H)
        dgl_ref[0] = dgl.reshape(r, H)

    col = pl.BlockSpec((r, DK), lambda h, c: (c, h))
    piece = pl.BlockSpec((1, r, H), lambda h, c: (h, c, 0))
    dq, dk, dv, db, dgc, dgl = pl.pallas_call(
        body, grid=(H, s // r), in_specs=_gdn_prep_specs(r) + _gdn_inter_specs(r),
        out_specs=[col, col, pl.BlockSpec((r, DV), lambda h, c: (c, h)), piece, piece, piece],
        out_shape=[_sds((s, H * DK)), _sds((s, H * DK)), _sds((s, H * DV))] + [_sds((H, s, H))] * 3,
        name="gdn_prep_bwd", compiler_params=_params(("parallel", "parallel"), 24 << 20))(
            qk, qk, v, beta, gc, gl, du, dw, dattn, dqg, dkd, degl)
    return jnp.concatenate([dq, dk], axis=1), dv, db, dgc, dgl


def _gdn_scan_specs(r, order):
    wide = pl.BlockSpec((r, H * DK), lambda c: (order(c), 0))
    return [pl.BlockSpec((r, H * DV), lambda c: (order(c), 0)), wide, pl.BlockSpec((H, r, C), lambda c: (0, order(c), 0)), wide, wide,
            pl.BlockSpec((H, r // C, 8, 128), lambda c: (0, order(c), 0, 0))]


def _gdn_scan_fwd(u, w, attn, qg, kd, egl):
    s = u.shape[0]
    r = CPB * C
    nb = s // r

    def body(u_ref, w_ref, attn_ref, qg_ref, kd_ref, egl_ref, o_ref, st_ref, state_ref):
        @pl.when(pl.program_id(0) == 0)
        def _():
            state_ref[...] = jnp.zeros_like(state_ref)

        state = state_ref[...]
        for i in range(CPB):
            rs = slice(i * C, (i + 1) * C)
            st_ref[:, i] = state
            o, state = _gdn_step(_heads(u_ref, rs, DV), _heads(w_ref, rs, DK), attn_ref[:, rs, :], _heads(qg_ref, rs, DK),
                                 _heads(kd_ref, rs, DK), egl_ref[:, i, 0:1, 0:1], state)
            for h in range(H):
                o_ref[rs, h * DV:(h + 1) * DV] = o[h]
        state_ref[...] = state

    out_specs = [pl.BlockSpec((r, H * DV), lambda c: (c, 0)), pl.BlockSpec((H, CPB, DK, DV), lambda c: (0, c, 0, 0))]
    return pl.pallas_call(body, grid=(nb,), in_specs=_gdn_scan_specs(r, lambda c: c), out_specs=out_specs,
                          out_shape=[_sds((s, H * DV)), _sds((H, s // C, DK, DV))],
                          scratch_shapes=[pltpu.VMEM((H, DK, DV), F32)], name="gdn_scan_fwd",
                          compiler_params=_params(("arbitrary",), 24 << 20))(u, w, attn, qg, kd, egl)


def _gdn_scan_bwd(u, w, attn, qg, kd, egl, states, do):
    s = u.shape[0]
    r = CPB * C
    nb = s // r

    def body(u_ref, w_ref, attn_ref, qg_ref, kd_ref, egl_ref, st_ref, do_ref,
             du_ref, dw_ref, dattn_ref, dqg_ref, dkd_ref, degl_ref, dstate_ref):
        @pl.when(pl.program_id(0) == 0)
        def _():
            dstate_ref[...] = jnp.zeros_like(dstate_ref)

        dstate = dstate_ref[...]
        for i in reversed(range(CPB)):
            rs = slice(i * C, (i + 1) * C)
            _, vjp = jax.vjp(_gdn_step, _heads(u_ref, rs, DV), _heads(w_ref, rs, DK).astype(F32), attn_ref[:, rs, :].astype(F32),
                             _heads(qg_ref, rs, DK).astype(F32), _heads(kd_ref, rs, DK).astype(F32), egl_ref[:, i, 0:1, 0:1],
                             st_ref[:, i])
            du, dw, dattn, dqg, dkd, degl, dstate = vjp((_heads(do_ref, rs, DV), dstate))
            dattn_ref[:, rs, :] = dattn
            degl_ref[:, i] = degl * jnp.ones((1, 8, 128), F32)
            for h in range(H):
                du_ref[rs, h * DV:(h + 1) * DV] = du[h]
                dw_ref[rs, h * DK:(h + 1) * DK] = dw[h]
                dqg_ref[rs, h * DK:(h + 1) * DK] = dqg[h]
                dkd_ref[rs, h * DK:(h + 1) * DK] = dkd[h]
        dstate_ref[...] = dstate

    rev = lambda c: nb - 1 - c
    in_specs = _gdn_scan_specs(r, rev) + [pl.BlockSpec((H, CPB, DK, DV), lambda c: (0, rev(c), 0, 0)),
                                          pl.BlockSpec((r, H * DV), lambda c: (rev(c), 0))]
    return pl.pallas_call(
        body, grid=(nb,), in_specs=in_specs, out_specs=_gdn_scan_specs(r, rev),
        out_shape=[_sds((s, H * DV)), _sds((s, H * DK)), _sds((H, s, C)), _sds((s, H * DK)), _sds((s, H * DK)),
                   _sds((H, s // C, 8, 128))],
        scratch_shapes=[pltpu.VMEM((H, DK, DV), F32)], name="gdn_scan_bwd",
        compiler_params=_params(("arbitrary",), 40 << 20))(u, w, attn, qg, kd, egl, states, do)


def _rot(x, cs, sn):
    return x * cs + pltpu.roll(x, DK // 2, 1) * sn


def _rot_t(d, cs, sn):
    return d * cs - pltpu.roll(d, DK // 2, 1) * sn


def _ret_chunk(q, k, v, state, lg):
    n = q.shape[1]
    row = lax.broadcasted_iota(jnp.int32, (n, n), 0)
    col = lax.broadcasted_iota(jnp.int32, (n, n), 1)
    dist = (row - col).astype(F32)
    dmat = jnp.exp(jnp.where(dist >= 0, dist * lg, -jnp.inf))
    scores = _bdot_b(q, k, NTB) * dmat
    pos = lax.broadcasted_iota(jnp.int32, (n, 1), 0).astype(F32)
    xi = jnp.exp((pos + 1.0) * lg)
    zeta = jnp.exp((n - 1.0 - pos) * lg)
    o = _bdot_b(scores, v, NNB) + _bdot_b(q, state, NNB) * xi
    new_state = state * jnp.exp(n * lg) + _bdot_b(k * zeta, v, TNB)
    return o, new_state


def _ret_specs(r, order):
    return [pl.BlockSpec((r, H * RET_W), lambda c: (order(c), O_RET // (H * RET_W))), pl.BlockSpec((r, DK), lambda c: (order(c), 0)),
            pl.BlockSpec((r, DK), lambda c: (order(c), 0)), pl.BlockSpec((H, 1, 1), lambda c: (0, 0, 0))]


def _ret_qkv(x_ref, cs, sn):
    q = jnp.stack([_rot(x_ref[:, h * RET_W:h * RET_W + DK], cs, sn) for h in range(H)])
    k = jnp.stack([_rot(x_ref[:, h * RET_W + DK:h * RET_W + 2 * DK], cs, sn) for h in range(H)]) * DK ** -0.5
    v = jnp.stack([x_ref[:, h * RET_W + 2 * DK:(h + 1) * RET_W] for h in range(H)])
    return q, k, v


RET_C = 256


def _ret_scan_fwd(proj, cs, sn, lgtab):
    s = proj.shape[0]
    r = min(RET_C, s)
    nb = s // r

    def body(x_ref, cs_ref, sn_ref, lg_ref, o_ref, st_ref, state_ref):
        @pl.when(pl.program_id(0) == 0)
        def _():
            state_ref[...] = jnp.zeros_like(state_ref)

        state = state_ref[...]
        st_ref[:, 0] = state
        q, k, v = _ret_qkv(x_ref, cs_ref[...], sn_ref[...])
        o, state_ref[...] = _ret_chunk(q, k, v, state, lg_ref[...])
        for h in range(H):
            o_ref[:, h * DV:(h + 1) * DV] = o[h]

    out_specs = [pl.BlockSpec((r, H * DV), lambda c: (c, 0)), pl.BlockSpec((H, 1, DK, DV), lambda c: (0, c, 0, 0))]
    return pl.pallas_call(body, grid=(nb,), in_specs=_ret_specs(r, lambda c: c), out_specs=out_specs,
                          out_shape=[_sds((s, H * DV)), _sds((H, nb, DK, DV))],
                          scratch_shapes=[pltpu.VMEM((H, DK, DV), F32)], name="ret_scan_fwd",
                          compiler_params=_params(("arbitrary",), 32 << 20))(proj, cs, sn, lgtab.reshape(H, 1, 1))


def _ret_scan_bwd(proj, cs, sn, lgtab, states, do, dproj):
    s = proj.shape[0]
    r = min(RET_C, s)
    nb = s // r

    def body(x_ref, cs_ref, sn_ref, lg_ref, st_ref, do_ref, _, d_ref, dstate_ref):
        @pl.when(pl.program_id(0) == 0)
        def _():
            dstate_ref[...] = jnp.zeros_like(dstate_ref)

        cs_, sn_ = cs_ref[...], sn_ref[...]
        lg = lg_ref[...]
        q, k, v = _ret_qkv(x_ref, cs_, sn_)
        _, vjp = jax.vjp(lambda q, k, v, st: _ret_chunk(q, k, v, st, lg), q, k, v, st_ref[:, 0])
        dq, dk, dv, dstate_ref[...] = vjp((_heads(do_ref, slice(None), DV), dstate_ref[...]))
        for h in range(H):
            d_ref[:, h * RET_W:h * RET_W + DK] = _rot_t(dq[h], cs_, sn_).astype(BF16)
            d_ref[:, h * RET_W + DK:h * RET_W + 2 * DK] = _rot_t(dk[h] * DK ** -0.5, cs_, sn_).astype(BF16)
            d_ref[:, h * RET_W + 2 * DK:(h + 1) * RET_W] = dv[h].astype(BF16)

    rev = lambda c: nb - 1 - c
    in_specs = _ret_specs(r, rev) + [pl.BlockSpec((H, 1, DK, DV), lambda c: (0, rev(c), 0, 0)),
                                     pl.BlockSpec((r, H * DV), lambda c: (rev(c), 0)), ANY]
    return pl.pallas_call(
        body, grid=(nb,), in_specs=in_specs, out_specs=pl.BlockSpec((r, H * RET_W), lambda c: (rev(c), O_RET // (H * RET_W))),
        out_shape=_sds(dproj.shape, BF16), input_output_aliases={6: 0},
        scratch_shapes=[pltpu.VMEM((H, DK, DV), F32)], name="ret_scan_bwd",
        compiler_params=_params(("arbitrary",), 48 << 20))(proj, cs, sn, lgtab.reshape(H, 1, 1), states, do, dproj)


def _merge(oa, z, ob, rg, ga, gb, wa, wb):
    ya = oa * lax.rsqrt(jnp.mean(oa * oa, axis=-1, keepdims=True) + EPS) * wa * _silu(z)
    mu = jnp.mean(ob, axis=-1, keepdims=True)
    var = jnp.mean(jnp.square(ob - mu), axis=-1, keepdims=True)
    yb = (ob - mu) * lax.rsqrt(var + EPS) * wb * _silu(rg)
    return _sigmoid(ga) * ya + _sigmoid(gb) * yb


def _merge_specs(ts):
    own = pl.BlockSpec((ts, DV), lambda h, i: (i, h))
    grp = lambda k: pl.BlockSpec((ts, DV), lambda h, i: (i, O_MERGE // DV + 4 * h + k))
    return [own, grp(0), own, grp(1), grp(2), grp(3),
            pl.BlockSpec((1, DV), lambda h, i: (0, 0)), pl.BlockSpec((1, DV), lambda h, i: (0, h))]


def _merge_fwd(oa, ob, proj, wa, wb, ts):
    s = oa.shape[0]

    def body(oa_ref, z_ref, ob_ref, rg_ref, ga_ref, gb_ref, wa_ref, wb_ref, o_ref, ot_ref):
        y = _merge(oa_ref[...], z_ref[...], ob_ref[...], rg_ref[...], ga_ref[...], gb_ref[...],
                   wa_ref[...], wb_ref[...]).astype(BF16)
        o_ref[...] = y
        ot_ref[...] = y.T

    return pl.pallas_call(body, grid=(H, s // ts), in_specs=_merge_specs(ts),
                          out_specs=[pl.BlockSpec((ts, DV), lambda h, i: (i, h)), pl.BlockSpec((DV, ts), lambda h, i: (h, i))],
                          out_shape=[_sds((s, H * DV), BF16), _sds((H * DV, s), BF16)], name="merge_fwd",
                          compiler_params=_params(("parallel", "parallel")))(oa, proj, ob, proj, proj, proj, wa, wb)


def _merge_bwd(oa, ob, proj, wa, wb, dmixed, ts):
    s = oa.shape[0]

    def body(oa_ref, z_ref, ob_ref, rg_ref, ga_ref, gb_ref, wa_ref, wb_ref, dm_ref,
             doa_ref, dob_ref, dgrp_ref, dwa_ref, dwb_ref):
        _, vjp = jax.vjp(_merge, oa_ref[...], z_ref[...], ob_ref[...], rg_ref[...], ga_ref[...], gb_ref[...],
                         wa_ref[...], wb_ref[...])
        doa, dz, dob, drg, dga, dgb, dwa, dwb = vjp(dm_ref[...].astype(F32))
        doa_ref[...] = doa
        dob_ref[...] = dob
        for k, d in enumerate((dz, drg, dga, dgb)):
            dgrp_ref[:, k * DV:(k + 1) * DV] = d.astype(BF16)
        first_tile = pl.program_id(1) == 0

        @pl.when(first_tile & (pl.program_id(0) == 0))
        def _():
            dwa_ref[...] = jnp.zeros_like(dwa_ref)

        @pl.when(first_tile)
        def _():
            dwb_ref[...] = jnp.zeros_like(dwb_ref)

        dwa_ref[...] += dwa
        dwb_ref[...] += dwb

    blk = pl.BlockSpec((ts, DV), lambda h, i: (i, h))
    out_specs = [blk, blk, pl.BlockSpec((ts, MERGE_W), lambda h, i: (i, O_MERGE // MERGE_W + h)),
                 pl.BlockSpec((1, DV), lambda h, i: (0, 0)), pl.BlockSpec((1, DV), lambda h, i: (0, h))]
    out_shape = [_sds((s, H * DV)), _sds((s, H * DV)), _sds((s, P_IN), BF16), _sds((1, DV)), _sds((1, H * DV))]
    return pl.pallas_call(body, grid=(H, s // ts), in_specs=_merge_specs(ts) + [blk], out_specs=out_specs, out_shape=out_shape,
                          name="merge_bwd", compiler_params=_params(("arbitrary", "arbitrary"), 40 * ts * DV * 4))(
                              oa, proj, ob, proj, proj, proj, wa, wb, dmixed)


def _act(hg, hu):
    return _silu(hg) * hu


def _ffn_gate_up(hn, w_gate, w_up, tm, tn):
    s, f = hn.shape[0], w_gate.shape[1]
    tm, tn = min(tm, s), min(tn, f)
    assert s % tm == 0 and f % tn == 0 and tm % 256 == 0
    sub = tm // 2

    def body(a_ref, wg_ref, wu_ref, pg_ref, pu_ref, act_ref, actt_ref):
        for r0 in range(0, tm, sub):
            rs = slice(r0, r0 + sub)
            hg = _bdot(a_ref[rs, :], wg_ref[...], NN)
            hu = _bdot(a_ref[rs, :], wu_ref[...], NN)
            y, vjp = jax.vjp(_act, hg, hu)
            pg, pu = vjp(jnp.ones_like(y))
            y = y.astype(BF16)
            pg_ref[rs, :] = pg.astype(BF16)
            pu_ref[rs, :] = pu.astype(BF16)
            act_ref[rs, :] = y
            actt_ref[:, rs] = y.T

    wsp = pl.BlockSpec((D, tn), lambda i, j: (0, j))
    blk = pl.BlockSpec((tm, tn), lambda i, j: (i, j))
    est = 2 * (tm * D * 2 + 2 * D * tn * 2 + 4 * tm * tn * 2) + 4 * sub * tn * 4
    return pl.pallas_call(body, grid=(s // tm, f // tn), in_specs=[pl.BlockSpec((tm, D), lambda i, j: (i, 0)), wsp, wsp],
                          out_specs=[blk, blk, blk, pl.BlockSpec((tn, tm), lambda i, j: (j, i))],
                          out_shape=[_sds((s, f), BF16)] * 3 + [_sds((f, s), BF16)], name="ffn_gate_up",
                          compiler_params=_params(("parallel", "parallel"), est))(hn, w_gate, w_up)


def _ffn_down_dx(dh2, w_down, pg, pu, tm, tn):
    s, f = pg.shape
    tm, tn = min(tm, s), min(tn, f)
    assert s % tm == 0 and f % tn == 0 and tm % 256 == 0
    sub = tm // 2

    def body(d_ref, w_ref, pg_ref, pu_ref, dhg_ref, dhu_ref):
        for r0 in range(0, tm, sub):
            rs = slice(r0, r0 + sub)
            dact = _bdot(d_ref[rs, :], w_ref[...], NT)
            dhg_ref[rs, :] = (dact * pg_ref[rs, :].astype(F32)).astype(BF16)
            dhu_ref[rs, :] = (dact * pu_ref[rs, :].astype(F32)).astype(BF16)

    blk = pl.BlockSpec((tm, tn), lambda i, j: (i, j))
    est = 2 * (tm * D * 4 + tn * D * 2 + 4 * tm * tn * 2) + 6 * sub * tn * 4
    return pl.pallas_call(body, grid=(s // tm, f // tn),
                          in_specs=[pl.BlockSpec((tm, D), lambda i, j: (i, 0)), pl.BlockSpec((tn, D), lambda i, j: (j, 0)), blk, blk],
                          out_specs=[blk, blk], out_shape=[_sds((s, f), BF16)] * 2, name="ffn_down_dx",
                          compiler_params=_params(("parallel", "parallel"), est))(dh2, w_down, pg, pu)


def _loss_rows(h2, wf, tgt):
    err = _rms(h2, wf) - tgt
    return 0.5 * jnp.sum(jnp.mean(err * err, axis=-1, keepdims=True), keepdims=True)


def _loss_fwd_bwd(h2, wf, tgt, ts):
    s = h2.shape[0]

    def body(h_ref, w_ref, t_ref, loss_ref, dh_ref, dhb_ref, dw_ref):
        loss, vjp = jax.vjp(_loss_rows, h_ref[...], w_ref[...], t_ref[...])
        dh, dw, _ = vjp(jnp.ones((1, 1), F32))
        dh_ref[...] = dh
        dhb_ref[...] = dh.astype(BF16)

        @pl.when(pl.program_id(0) == 0)
        def _():
            loss_ref[...] = jnp.zeros_like(loss_ref)
            dw_ref[...] = jnp.zeros_like(dw_ref)

        loss_ref[...] += loss
        dw_ref[...] += dw

    row = pl.BlockSpec((ts, D), lambda i: (i, 0))
    vec = pl.BlockSpec((1, D), lambda i: (0, 0))
    tile = pl.BlockSpec((8, 128), lambda i: (0, 0))
    return pl.pallas_call(body, grid=(s // ts,), in_specs=[row, vec, row], out_specs=[tile, row, row, vec],
                          out_shape=[_sds((8, 128)), _sds((s, D)), _sds((s, D), BF16), _sds((1, D))], name="final_norm_loss",
                          compiler_params=_params(("arbitrary",), 12 * ts * D * 4))(h2, wf, tgt)


def _rope_tables(s):
    inv = ROPE_BASE ** (-jnp.arange(0, DK, 2, dtype=F32) / DK)
    ang = jnp.arange(s, dtype=F32)[:, None] * inv[None, :]
    cos, sin = jnp.cos(ang), jnp.sin(ang)
    return jnp.concatenate([cos, cos], axis=1), jnp.concatenate([-sin, sin], axis=1)


def _local_step(x, tgt, w_in, w_out, w_gate, w_up, w_down, norm1_w, conv_w, a_log, dt_bias, gdn_norm_w, ret_norm_w, norm2_w, norm_f_w,
                dist=None):
    s = x.shape[0]
    ts = min(512, s)
    cs, sn = _rope_tables(s)
    lgtab = jnp.log1p(-jnp.exp2(-5.0 - jnp.arange(H, dtype=F32))).reshape(1, H)

    u, u_t = _rms_fwd(x, norm1_w, ts, "norm1_fwd")
    if dist is None:
        proj = _matmul(u, w_in, tm=1024, tn=1280, tk=D, name="in_proj")
    else:
        proj, gathered = _matmul(u, w_in, tm=1024, tn=1280, tk=D, side=_gather_side(dist["shards"]), name="in_proj")
        w_out, w_gate, w_up, w_down = (gathered[0].reshape(D, D), _from_slots_cols(gathered[1]), _from_slots_cols(gathered[2]),
                                       gathered[3].reshape(-1, D))
    qk, va, c_qk, c_v = _gdn_qkv_fwd(proj, conv_w, ts)
    beta, gc, gl = _bg_fwd(proj, a_log, dt_bias, ts)
    inter = _gdn_prep_fwd(qk, va, beta, gc, gl)
    oa, st_a = _gdn_scan_fwd(*inter)
    ob, st_b = _ret_scan_fwd(proj, cs, sn, lgtab)
    mixed, mixed_t = _merge_fwd(oa, ob, proj, gdn_norm_w, ret_norm_w, min(2 * ts, s))
    h1 = _matmul(mixed, w_out, tm=1024, tn=1024, tk=D, res=x, name="out_proj")
    hn, hn_t = _rms_fwd(h1, norm2_w, ts, "norm2_fwd")
    act_dg, act_du, act, act_t = _ffn_gate_up(hn, w_gate, w_up, 1024, 512)
    h2 = _matmul(act, w_down, tm=512, tn=1024, tk=5632, res=h1, name="ffn_down")
    loss, dh2, dh2_b, d_norm_f = _loss_fwd_bwd(h2, norm_f_w, tgt, ts)

    dhg, dhu = _ffn_down_dx(dh2_b, w_down, act_dg, act_du, 1024, 512)
    g_down = _matmul(act_t, dh2_b, tm=512, tn=512, tk=8192, out_dtype=BF16, name="ffn_down_dw")
    g_gate = _matmul(hn_t, dhg, tm=512, tn=512, tk=8192, out_dtype=BF16, name="ffn_gate_dw")
    g_up = _matmul(hn_t, dhu, tm=512, tn=512, tk=8192, out_dtype=BF16, name="ffn_up_dw")
    dhn = _matmul(dhg, w_gate, tb=True, tm=512, tn=1024, tk=5632, name="ffn_gate_dx")
    dhn = _matmul(dhu, w_up, tb=True, tm=512, tn=1024, tk=5632, res=dhn, name="ffn_up_dx")
    dh1, d_norm2 = _rms_bwd(h1, norm2_w, dhn, dh2, ts, "norm2_bwd")

    g_out = _matmul(mixed_t, dh1, tm=1024, tn=1024, tk=D, out_dtype=BF16, name="out_proj_dw")
    early = ["w_out", "w_gate", "w_up", "w_down"]
    if dist is None:
        dmixed = _matmul(dh1, w_out, tb=True, tm=1024, tn=1024, tk=D, out_dtype=BF16, name="out_proj_dx")
    else:
        slots = dict(w_out=g_out.reshape(NDEV, D // NDEV, D), w_gate=_to_slots_cols(g_gate), w_up=_to_slots_cols(g_up),
                     w_down=g_down.reshape(NDEV, -1, D))
        dmixed, from_sibling = _matmul(dh1, w_out, tb=True, tm=1024, tn=1024, tk=D, out_dtype=BF16,
                                       side=_sibling_side([slots[k] for k in early]), name="out_proj_dx")
        parts = [_add_sibling(slots[k], r, dist["core"], 128, "grads_add_" + k) for k, r in zip(early, from_sibling)]
    doa, dob, dproj, d_gdn_norm, d_ret_norm = _merge_bwd(oa, ob, proj, gdn_norm_w, ret_norm_w, dmixed, ts)

    dproj = _ret_scan_bwd(proj, cs, sn, lgtab, st_b, dob, dproj)
    d_inter = _gdn_scan_bwd(*inter, st_a, doa)
    dqk, dva, dbeta_h, dgc_h, dgl_h = _gdn_prep_bwd(qk, va, beta, gc, gl, *d_inter)
    dproj, d_conv = _gdn_qkv_bwd(proj, conv_w, c_qk, c_v, dqk, dva, dproj, ts)
    dproj, d_a_log, d_dt_bias = _bg_bwd(proj, a_log, dt_bias, dbeta_h, dgc_h, dgl_h, dproj, ts)

    if dist is None:
        g_in = _matmul(u_t, dproj, tm=1024, tn=1280, tk=D, out_dtype=BF16, name="in_proj_dw")
        du = _matmul(dproj, w_in, tb=True, tm=1024, tn=1024, tk=1664, name="in_proj_dx")
        big = dict(w_in=g_in, w_out=g_out, w_gate=g_gate, w_up=g_up, w_down=g_down)
    else:
        g_in, from_chips = _matmul(u_t, dproj, tm=1024, tn=1280, tk=D, out_dtype=BF16, side=_chips_side(parts), name="in_proj_dw")
        du, (from_all,) = _matmul(dproj, w_in, tb=True, tm=1024, tn=1024, tk=1664,
                                  side=_all_to_all_side(_windows_from_layout(g_in)), name="in_proj_dx")
        big = dict(w_in=from_all, **{k: (p, r) for k, p, r in zip(early, parts, from_chips)})
    dx, d_norm1 = _rms_bwd(x, norm1_w, du, dh1, ts, "norm1_bwd")

    small = dict(norm1_w=d_norm1, conv_w=d_conv, a_log=d_a_log, dt_bias=d_dt_bias, gdn_norm_w=d_gdn_norm,
                 ret_norm_w=d_ret_norm, norm2_w=d_norm2, norm_f_w=d_norm_f)
    return loss, dx, big, small


def _coords():
    return lax.axis_index("x"), lax.axis_index("y"), lax.axis_index("c")


def _gather_side(shards):
    n = len(shards)

    def plan(ins, outs, send_sems, recv_sems, local_sems):
        x, y, c = _coords()
        me, sibling = (x, y, c), (x, y, 1 - c)
        chips = [(1 - x, y), (x, 1 - y), (1 - x, 1 - y)]

        def copy(a, k, block, to, src=None):
            px, py, pc = block
            dst = outs[a].at[4 * px + 2 * py + pc]
            return pltpu.make_async_remote_copy(src_ref=dst if src is None else src, dst_ref=dst, send_sem=send_sems.at[a, k],
                                                recv_sem=recv_sems.at[a, k], device_id=to, device_id_type=MESH)

        mine = [pltpu.make_async_copy(ins[a], outs[a].at[4 * x + 2 * y + c], local_sems.at[a]) for a in range(n)]
        first = []
        for a in range(n):
            first.append(copy(a, 0, me, sibling, src=ins[a]))
            first += [copy(a, 1 + j, me, (*chip, c), src=ins[a]) for j, chip in enumerate(chips)]
        return c, me, sibling, chips, copy, mine, first

    def start(ins, outs, *sems):
        *_, mine, first = plan(ins, outs, *sems)
        for cp in mine + first:
            cp.start()

    def finish(ins, outs, *sems):
        c, me, sibling, chips, copy, mine, first = plan(ins, outs, *sems)
        passed = []
        for j, chip in enumerate(chips):
            for a in range(n):
                copy(a, 1 + j, (*chip, c), me).wait_recv()
                fwd = copy(a, 4 + j, (*chip, c), sibling)
                fwd.start()
                passed.append(fwd)
        for a in range(n):
            copy(a, 0, sibling, me).wait_recv()
            for j, chip in enumerate(chips):
                copy(a, 4 + j, (*chip, 1 - c), me).wait_recv()
        for cp in first + passed:
            cp.wait_send()
        for cp in mine:
            cp.wait()

    return _Side(shards, [_sds((NDEV,) + a.shape, a.dtype) for a in shards], [(n, 7), (n, 7), (n,)], start, finish)


def _exchange_side(ins, n_out, copies_of):
    def start(in_refs, out_refs, *sems):
        for cp in copies_of(in_refs, out_refs, *sems):
            cp.start()

    def finish(in_refs, out_refs, *sems):
        for cp in copies_of(in_refs, out_refs, *sems):
            cp.wait()

    n = len(ins)
    return _Side(ins, [_sds((n_out,) + a.shape[1:], a.dtype) for a in ins], [(n, n_out), (n, n_out)], start, finish)


def _sibling_side(slots):
    def copies_of(ins, outs, send_sems, recv_sems):
        x, y, c = _coords()
        return [pltpu.make_async_remote_copy(
            src_ref=ins[a].at[2 * j + (1 - c)], dst_ref=outs[a].at[j], send_sem=send_sems.at[a, j], recv_sem=recv_sems.at[a, j],
            device_id=(x, y, 1 - c), device_id_type=MESH) for a in range(len(slots)) for j in range(4)]

    return _exchange_side(slots, 4, copies_of)


def _chips_side(parts):
    def copies_of(ins, outs, send_sems, recv_sems):
        x, y, c = _coords()
        chips = [(1 - x, y), (x, 1 - y), (1 - x, 1 - y)]
        return [pltpu.make_async_remote_copy(
            src_ref=ins[a].at[2 * px + py], dst_ref=outs[a].at[k], send_sem=send_sems.at[a, k], recv_sem=recv_sems.at[a, k],
            device_id=(px, py, c), device_id_type=MESH) for a in range(len(parts)) for k, (px, py) in enumerate(chips)]

    return _exchange_side(parts, 3, copies_of)


def _all_to_all_side(slots):
    def plan(ins, outs, send_sems, recv_sems, local_sems):
        x, y, c = _coords()
        mine = 4 * x + 2 * y + c
        own = pltpu.make_async_copy(ins[0].at[mine], outs[0].at[mine], local_sems.at[0])
        remote = []
        for r in range(1, NDEV):
            peer = (x ^ (r >> 2), y ^ ((r >> 1) & 1), c ^ (r & 1))
            remote.append(pltpu.make_async_remote_copy(
                src_ref=ins[0].at[mine ^ r], dst_ref=outs[0].at[mine], send_sem=send_sems.at[r - 1], recv_sem=recv_sems.at[r - 1],
                device_id=peer, device_id_type=MESH))
        return own, remote

    def start(ins, outs, *sems):
        own, remote = plan(ins, outs, *sems)
        for cp in [own] + remote:
            cp.start()

    def finish(ins, outs, *sems):
        own, remote = plan(ins, outs, *sems)
        for cp in remote:
            cp.wait()
        own.wait()

    return _Side([slots], [_sds(slots.shape, slots.dtype)], [(NDEV - 1,), (NDEV - 1,), (1,)], start, finish)


def _allreduce_small(pack, name):
    rows, cols = pack.shape

    def body(in_ref, out_ref, buf_ref, send_sems, recv_sems):
        x, y, c = _coords()
        mine = 4 * x + 2 * y + c
        buf_ref[mine] = in_ref[...]
        copies = []
        for r in range(1, NDEV):
            peer = (x ^ (r >> 2), y ^ ((r >> 1) & 1), c ^ (r & 1))
            copies.append(pltpu.make_async_remote_copy(
                src_ref=in_ref, dst_ref=buf_ref.at[mine], send_sem=send_sems.at[r - 1], recv_sem=recv_sems.at[r - 1],
                device_id=peer, device_id_type=MESH))
        for cp in copies:
            cp.start()
        for r in range(1, NDEV):
            pltpu.make_async_remote_copy(
                src_ref=in_ref, dst_ref=buf_ref.at[mine ^ r], send_sem=send_sems.at[r - 1], recv_sem=recv_sems.at[r - 1],
                device_id=(x, y, c), device_id_type=MESH).wait_recv()
        for cp in copies:
            cp.wait_send()
        acc = buf_ref[0]
        for d in range(1, NDEV):
            acc = acc + buf_ref[d]
        out_ref[...] = acc

    return pl.pallas_call(
        body, in_specs=[VMEM_FULL], out_specs=VMEM_FULL, out_shape=_sds((rows, cols)),
        scratch_shapes=[pltpu.VMEM((NDEV, rows, cols), F32), pltpu.SemaphoreType.DMA((NDEV - 1,)), pltpu.SemaphoreType.DMA((NDEV - 1,))],
        name=name)(pack)


def _add_sibling(slots, recv, core, tr, name):
    _, rows, cols = slots.shape
    tr = _row_tile(rows, tr)

    def body(c_ref, a_ref, b_ref, o_ref):
        o_ref[...] = (a_ref[...].astype(F32) + b_ref[...].astype(F32)).astype(BF16)

    gs = pltpu.PrefetchScalarGridSpec(
        num_scalar_prefetch=1, grid=(4, rows // tr),
        in_specs=[pl.BlockSpec((None, tr, cols), lambda j, i, cr: (2 * j + cr[0], i, 0)),
                  pl.BlockSpec((None, tr, cols), lambda j, i, cr: (j, i, 0))],
        out_specs=pl.BlockSpec((None, tr, cols), lambda j, i, cr: (j, i, 0)))
    return pl.pallas_call(body, grid_spec=gs, out_shape=_sds((4, rows, cols), BF16), name=name,
                          compiler_params=_params(("parallel", "parallel"), 6 * tr * cols * 4))(core, slots, recv)


def _adam_math(w, g, m, v):
    m2 = B1 * m + (1.0 - B1) * g
    v2 = B2 * v + (1.0 - B2) * jnp.square(g)
    m_hat = m2 / (1.0 - B1 ** STEP)
    v_hat = v2 / (1.0 - B2 ** STEP)
    return -LR * (m_hat / (jnp.sqrt(v_hat) + EPS_ADAM) + WD * w), m2, v2


def _adamw_reduced(part, recv, chip, w, m, v, tr, name):
    rows, cols = w.shape
    tr = _row_tile(rows, tr)

    def body(j_ref, p_ref, r0_ref, r1_ref, r2_ref, w_ref, m_ref, v_ref, g_ref, d_ref, nm_ref, nv_ref):
        g = p_ref[...].astype(F32) + r0_ref[...].astype(F32) + r1_ref[...].astype(F32) + r2_ref[...].astype(F32)
        d, m2, v2 = _adam_math(w_ref[...], g, m_ref[...], v_ref[...])
        g_ref[...] = g
        d_ref[...] = d
        nm_ref[...] = m2
        nv_ref[...] = v2

    flat = pl.BlockSpec((tr, cols), lambda i, jr: (i, 0))
    gs = pltpu.PrefetchScalarGridSpec(
        num_scalar_prefetch=1, grid=(rows // tr,),
        in_specs=[pl.BlockSpec((None, tr, cols), lambda i, jr: (jr[0], i, 0))]
        + [pl.BlockSpec((None, tr, cols), functools.partial(lambda i, jr, k: (k, i, 0), k=k)) for k in range(3)] + [flat] * 3,
        out_specs=[flat] * 4)
    return pl.pallas_call(body, grid_spec=gs, out_shape=[_sds((rows, cols))] * 4, name=name,
                          compiler_params=_params(("parallel",), 22 * tr * cols * 4))(chip, part, recv, recv, recv, w, m, v)


def _sum_slots(recv, tr, name):
    _, rows, cols = recv.shape
    tr = _row_tile(rows, tr)

    def body(*refs):
        acc = refs[0][...].astype(F32)
        for p_ref in refs[1:NDEV]:
            acc = acc + p_ref[...].astype(F32)
        refs[NDEV][...] = acc

    slot = [pl.BlockSpec((None, tr, cols), functools.partial(lambda i, k: (k, i, 0), k=k)) for k in range(NDEV)]
    return pl.pallas_call(body, grid=(rows // tr,), in_specs=slot, out_specs=pl.BlockSpec((tr, cols), lambda i: (i, 0)),
                          out_shape=_sds((rows, cols)), name=name, compiler_params=_params(("parallel",)))(*([recv] * NDEV))


def _adamw_rows(w, g, m, v, tr, name):
    rows, cols = w.shape
    tr = _row_tile(rows, tr)

    def body(w_ref, g_ref, m_ref, v_ref, d_ref, nm_ref, nv_ref):
        d_ref[...], nm_ref[...], nv_ref[...] = _adam_math(w_ref[...], g_ref[...], m_ref[...], v_ref[...])

    flat = pl.BlockSpec((tr, cols), lambda i: (i, 0))
    return pl.pallas_call(body, grid=(rows // tr,), in_specs=[flat] * 4, out_specs=[flat] * 3, out_shape=[_sds((rows, cols))] * 3,
                          name=name, compiler_params=_params(("parallel",)))(w, g, m, v)


def _adamw_plain(w, g, m, v, name):
    def body(w_ref, g_ref, m_ref, v_ref, d_ref, nm_ref, nv_ref):
        d, m2, v2 = _adam_math(w_ref[...], g_ref[...], m_ref[...], v_ref[...])
        d_ref[...] = d
        nm_ref[...] = m2
        nv_ref[...] = v2

    return pl.pallas_call(body, out_shape=[_sds(w.shape)] * 3, name=name)(w, g, m, v)


def _pack_small(norm1_w, conv_w, a_log, dt_bias, gdn_norm_w, ret_norm_w, norm2_w, norm_f_w):
    misc = jnp.concatenate([gdn_norm_w.reshape(1, DV), a_log.reshape(1, H), dt_bias.reshape(1, H),
                            jnp.zeros((1, D - DV - 2 * H), F32)], axis=1)
    return jnp.concatenate([norm1_w.reshape(1, D), ret_norm_w.reshape(1, D), norm2_w.reshape(1, D), norm_f_w.reshape(1, D),
                            conv_w.reshape(8, D), misc, jnp.zeros((3, D), F32)], axis=0)


def _unpack_small(pack):
    return dict(norm1_w=pack[0:1], ret_norm_w=pack[1:2], norm2_w=pack[2:3], norm_f_w=pack[3], conv_w=pack[4:12].reshape(4, 2 * D),
                gdn_norm_w=pack[12:13, 0:DV], a_log=pack[12:13, DV:DV + H], dt_bias=pack[12:13, DV + H:DV + 2 * H])


IN_SPLITS = (4096, 2048, 8, 8, 1024, 1024, 2048, 2048, 2048, 2048)


BA_END = sum(IN_SPLITS[:4])
LANES = 128


def _padded_order_blocks():
    z0, ba0, rq0, rk0, rv0, rg0, ga0, gb0 = 4096, 6144, 6400, 7424, 8448, 10496, 12544, 14592
    cols = []
    for h in range(H):
        for base in (z0, rg0, ga0, gb0):
            cols += [base + DV * h, base + DV * h + LANES]
    cols += list(range(0, z0, LANES))
    for h in range(H):
        cols += [rq0 + DK * h, rk0 + DK * h, rv0 + DV * h, rv0 + DV * h + LANES]
    cols += [ba0, ba0 + LANES]
    blocks = np.asarray(cols, np.int32) // LANES
    assert sorted(blocks.tolist()) == list(range(P_IN // LANES))
    return blocks


def _permute_blocks(x, blocks, name):
    rows, cols = x.shape

    def body(p_ref, x_ref, o_ref):
        o_ref[...] = x_ref[...]

    gs = pltpu.PrefetchScalarGridSpec(num_scalar_prefetch=1, grid=(cols // LANES,),
                                      in_specs=[pl.BlockSpec((rows, LANES), lambda j, p: (0, p[j]))],
                                      out_specs=pl.BlockSpec((rows, LANES), lambda j, p: (0, j)))
    return pl.pallas_call(body, grid_spec=gs, out_shape=_sds((rows, cols), x.dtype), name=name,
                          compiler_params=_params(("parallel",)))(jnp.asarray(blocks), x)


def _regroup_w_in(w):
    padded = jnp.concatenate([w[:, :BA_END], jnp.zeros((w.shape[0], P_IN - N_IN), w.dtype), w[:, BA_END:]], axis=1)
    return _permute_blocks(padded, _padded_order_blocks(), "w_in_to_layout")


def _ungroup_w_in(g):
    padded = _permute_blocks(g, np.argsort(_padded_order_blocks()).astype(np.int32), "w_in_grad_from_layout")
    return jnp.concatenate([padded[:, :BA_END], padded[:, BA_END + P_IN - N_IN:]], axis=1)


SHARD_W = N_IN // NDEV
GAP = P_IN - N_IN
WIN = 2304


def _padded_col(c):
    return c + (GAP if c >= BA_END else 0)


WIN_START = [min(_padded_col(SHARD_W * d) // LANES * LANES, P_IN - WIN) for d in range(NDEV)]
WIN_OFF = [_padded_col(SHARD_W * d) - WIN_START[d] for d in range(NDEV)]
STRADDLER = BA_END // SHARD_W
STRADDLE_AT = BA_END - STRADDLER * SHARD_W
assert all(WIN_OFF[d] + SHARD_W + (GAP if d == STRADDLER else 0) <= WIN for d in range(NDEV))


def _win_off(me):
    off = jnp.int32(0)
    for d in range(NDEV):
        off = jnp.where(me == d, jnp.int32(WIN_OFF[d]), off)
    return off


def _window_of_shard(shard, me):
    rows = shard.shape[0]
    zeros = lambda n: jnp.zeros((rows, n), shard.dtype)
    plain = lax.dynamic_update_slice(zeros(WIN), shard, (0, _win_off(me)))
    o = WIN_OFF[STRADDLER]
    split = jnp.concatenate([zeros(o), shard[:, :STRADDLE_AT], zeros(GAP), shard[:, STRADDLE_AT:], zeros(WIN - o - GAP - SHARD_W)], axis=1)
    return jnp.where(me == STRADDLER, split, plain)


def _shard_of_window(win, me):
    plain = lax.dynamic_slice(win, (0, _win_off(me)), (win.shape[0], SHARD_W))
    o = WIN_OFF[STRADDLER]
    split = jnp.concatenate([win[:, o:o + STRADDLE_AT], win[:, o + STRADDLE_AT + GAP:o + GAP + SHARD_W]], axis=1)
    return jnp.where(me == STRADDLER, split, plain)


def _layout_from_windows(wins):
    _, rows, _ = wins.shape
    data = []
    for d in range(NDEV):
        lo = _padded_col(SHARD_W * d)
        data.append([(lo, lo + STRADDLE_AT), (lo + STRADDLE_AT + GAP, lo + GAP + SHARD_W)] if d == STRADDLER else [(lo, lo + SHARD_W)])
    zero_block = (0, WIN // LANES - 1)
    table = []
    for p in _padded_order_blocks():
        src = [(d, int(p) - WIN_START[d] // LANES) for d in range(NDEV)
               if any(lo < (p + 1) * LANES and hi > p * LANES for lo, hi in data[d])]
        assert len(src) <= 2 and all(0 <= b < WIN // LANES for _, b in src)
        src += [zero_block] * (2 - len(src))
        table.append([src[0][0], src[0][1], src[1][0], src[1][1]])
    table = np.asarray(table, np.int32).T.copy()

    def body(t_ref, a0_ref, b0_ref, a1_ref, b1_ref, o_ref):
        o_ref[:, :LANES] = a0_ref[...] + b0_ref[...]
        o_ref[:, LANES:] = a1_ref[...] + b1_ref[...]

    src = lambda k, odd: pl.BlockSpec((None, rows, LANES), lambda j, t: (t[k, 2 * j + odd], 0, t[k + 1, 2 * j + odd]))
    gs = pltpu.PrefetchScalarGridSpec(
        num_scalar_prefetch=1, grid=(P_IN // (2 * LANES),), in_specs=[src(0, 0), src(2, 0), src(0, 1), src(2, 1)],
        out_specs=pl.BlockSpec((rows, 2 * LANES), lambda j, t: (0, j)))
    return pl.pallas_call(body, grid_spec=gs, out_shape=_sds((rows, P_IN), wins.dtype), name="w_in_from_windows",
                          compiler_params=_params(("parallel",)))(jnp.asarray(table), wins, wins, wins, wins)


def _windows_from_layout(g):
    rows = g.shape[0]
    where = np.argsort(_padded_order_blocks())
    nb = WIN // LANES
    table = np.asarray([where[WIN_START[d] // LANES + b] for d in range(NDEV) for b in range(nb)], np.int32)

    def body(t_ref, x0_ref, x1_ref, o_ref):
        o_ref[:, :LANES] = x0_ref[...]
        o_ref[:, LANES:] = x1_ref[...]

    src = lambda odd: pl.BlockSpec((rows, LANES), lambda d, b, t: (0, t[d * nb + 2 * b + odd]))
    gs = pltpu.PrefetchScalarGridSpec(num_scalar_prefetch=1, grid=(NDEV, nb // 2), in_specs=[src(0), src(1)],
                                      out_specs=pl.BlockSpec((None, rows, 2 * LANES), lambda d, b, t: (d, 0, b)))
    return pl.pallas_call(body, grid_spec=gs, out_shape=_sds((NDEV, rows, WIN), g.dtype), name="w_in_grad_windows",
                          compiler_params=_params(("parallel", "parallel")))(jnp.asarray(table), g, g)


def _to_slots_cols(g):
    rows, cols = g.shape
    return g.reshape(rows, NDEV, cols // NDEV).transpose(1, 0, 2)


def _from_slots_cols(a):
    n, rows, cols = a.shape
    return a.transpose(1, 0, 2).reshape(rows, n * cols)


WEIGHT_ORDER = ["norm1_w", "w_in", "conv_w", "a_log", "dt_bias", "gdn_norm_w", "ret_norm_w", "w_out", "norm2_w", "w_gate", "w_up",
                "w_down", "norm_f_w"]


def kernel(x, norm1_w, w_in, conv_w, a_log, dt_bias, gdn_norm_w, ret_norm_w, w_out, norm2_w, w_gate, w_up, w_down, norm_f_w, loss_target, m_norm1_w, m_w_in, m_conv_w, m_a_log, m_dt_bias, m_gdn_norm_w, m_ret_norm_w, m_w_out, m_norm2_w, m_w_gate, m_w_up, m_w_down, m_norm_f_w, v_norm1_w, v_w_in, v_conv_w, v_a_log, v_dt_bias, v_gdn_norm_w, v_ret_norm_w, v_w_out, v_norm2_w, v_w_gate, v_w_up, v_w_down, v_norm_f_w):
    ax, ay, ac = _coords()
    me = 4 * ax + 2 * ay + ac
    core = jnp.reshape(ac, (1,)).astype(jnp.int32)
    chip = jnp.reshape(2 * ax + ay, (1,)).astype(jnp.int32)
    w = dict(norm1_w=norm1_w, w_in=w_in[0], conv_w=conv_w[0], a_log=a_log, dt_bias=dt_bias, gdn_norm_w=gdn_norm_w,
             ret_norm_w=ret_norm_w, w_out=w_out[0], norm2_w=norm2_w, w_gate=w_gate[0], w_up=w_up[0], w_down=w_down[0],
             norm_f_w=norm_f_w)
    m = dict(norm1_w=m_norm1_w, w_in=m_w_in[0], conv_w=m_conv_w[0], a_log=m_a_log, dt_bias=m_dt_bias, gdn_norm_w=m_gdn_norm_w,
             ret_norm_w=m_ret_norm_w, w_out=m_w_out[0], norm2_w=m_norm2_w, w_gate=m_w_gate[0], w_up=m_w_up[0], w_down=m_w_down[0],
             norm_f_w=m_norm_f_w)
    v = dict(norm1_w=v_norm1_w, w_in=v_w_in[0], conv_w=v_conv_w[0], a_log=v_a_log, dt_bias=v_dt_bias, gdn_norm_w=v_gdn_norm_w,
             ret_norm_w=v_ret_norm_w, w_out=v_w_out[0], norm2_w=v_norm2_w, w_gate=v_w_gate[0], w_up=v_w_up[0], w_down=v_w_down[0],
             norm_f_w=v_norm_f_w)
    big_names = ["w_in", "w_out", "w_gate", "w_up", "w_down"]

    w_in_wins, conv_all = _run_side(_gather_side([_window_of_shard(w["w_in"].astype(BF16), me), w["conv_w"]]), "w_in_allgather")
    w_in_full = _layout_from_windows(w_in_wins)
    conv_full = _from_slots_cols(conv_all)
    dist = dict(core=core, shards=[w[k].astype(BF16) for k in ("w_out", "w_gate", "w_up", "w_down")])

    loss_tile, dx, big, small = _local_step(
        x[0], loss_target[0], w_in_full, None, None, None, None, norm1_w, conv_full, a_log, dt_bias,
        gdn_norm_w, ret_norm_w, norm2_w, norm_f_w.reshape(1, D), dist=dist)
    loss = lax.psum(loss_tile[0, 0], ("x", "y", "c"))

    g_w_in = _shard_of_window(_sum_slots(big["w_in"], 64, "w_in_grad_sum"), me)
    out = {"w_in": (g_w_in, *_adamw_rows(w["w_in"], g_w_in, m["w_in"], v["w_in"], 64, "adamw_w_in"))}
    for k in ("w_out", "w_gate", "w_up", "w_down"):
        part, recv = big[k]
        out[k] = _adamw_reduced(part, recv, chip, w[k], m[k], v[k], 128, "adamw_" + k)

    g_small = _unpack_small(_allreduce_small(_pack_small(**small), "small_grads_allreduce"))
    g_small["conv_w"] = lax.dynamic_slice_in_dim(g_small["conv_w"], me * (2 * D // NDEV), 2 * D // NDEV, axis=1)
    small_names = [k for k in WEIGHT_ORDER if k not in big_names]
    pad_conv = lambda a: jnp.pad(a, ((0, 0), (0, 2 * D - a.shape[1])))
    packs = []
    for src in (w, g_small, m, v):
        args = {k: (pad_conv(src[k]) if k == "conv_w" else src[k]) for k in small_names}
        packs.append(_pack_small(**args))
    d_pack, m_pack, v_pack = _adamw_plain(*packs[0:1], packs[1], packs[2], packs[3], name="adamw_small")
    cut_conv = lambda dct: {**dct, "conv_w": dct["conv_w"][:, :2 * D // NDEV]}
    d_small, m_small, v_small = (cut_conv(_unpack_small(p)) for p in (d_pack, m_pack, v_pack))

    def shaped(k, a):
        return a.reshape(w_shapes[k])

    w_shapes = dict(norm1_w=norm1_w.shape, w_in=w_in.shape, conv_w=conv_w.shape, a_log=a_log.shape, dt_bias=dt_bias.shape,
                    gdn_norm_w=gdn_norm_w.shape, ret_norm_w=ret_norm_w.shape, w_out=w_out.shape, norm2_w=norm2_w.shape,
                    w_gate=w_gate.shape, w_up=w_up.shape, w_down=w_down.shape, norm_f_w=norm_f_w.shape)
    grads, deltas, new_m, new_v = [], [], [], []
    for k in WEIGHT_ORDER:
        if k in big_names:
            g_, d_, m_, v_ = out[k]
        else:
            g_, d_, m_, v_ = g_small[k], d_small[k], m_small[k], v_small[k]
        grads.append(shaped(k, g_))
        deltas.append(shaped(k, d_))
        new_m.append(shaped(k, m_))
        new_v.append(shaped(k, v_))
    return (loss, dx[None], *grads, *deltas, *new_m, *new_v)
```

```python
import functools
import numpy as np
import jax
import jax.numpy as jnp
from jax import lax
from jax.experimental import pallas as pl
from jax.experimental.pallas import tpu as pltpu

F32, BF16 = jnp.float32, jnp.bfloat16
HI = lax.Precision.HIGHEST
MESH = pl.DeviceIdType.MESH
ANY = pl.BlockSpec(memory_space=pl.ANY)
VMEM_FULL = pl.BlockSpec(memory_space=pltpu.VMEM)

NDEV = 8
D = 2048
H = 8
DK = 128
DV = 256
C = 64
CPB = 4
EPS = 1e-6
ROPE_BASE = 10000.0
N_IN = 16400
O_MERGE, O_QKV, O_RET, O_BA, P_IN = 0, 8192, 12288, 16384, 16640
MERGE_W, RET_W = 4 * DV, 2 * DK + DV
LR, B1, B2, EPS_ADAM, WD, STEP = 0.001, 0.9, 0.999, 1e-08, 0.01, 10
VMEM_CAP = 60 * 1024 * 1024

NN = ((1,), (0,))
NT = ((1,), (1,))
TN = ((0,), (0,))


def _params(sem=None, est=None):
    kw = {}
    if sem is not None:
        kw["dimension_semantics"] = sem
    if est is not None:
        kw["vmem_limit_bytes"] = int(min(VMEM_CAP, max(32 * 1024 * 1024, est * 5 // 4 + (4 << 20))))
    return pltpu.CompilerParams(**kw)


def _sds(shape, dt=F32):
    return jax.ShapeDtypeStruct(tuple(shape), dt)


def _row_tile(rows, limit):
    return max(t for t in range(16, min(rows, limit) + 1, 16) if rows % t == 0)


def _bdot(a, b, dims):
    return lax.dot_general(a.astype(BF16), b.astype(BF16), (dims, ((), ())), preferred_element_type=F32)


def _hdot(a, b, dims):
    return lax.dot_general(a, b, (dims, ((), ())), precision=HI, preferred_element_type=F32)


_sigmoid = jax.nn.sigmoid


def _silu(x):
    return x * _sigmoid(x)


def _rms(x, w):
    return x * lax.rsqrt(jnp.mean(x * x, axis=-1, keepdims=True) + EPS) * w


class _Side:
    def __init__(self, ins, out_shapes, sems, start, finish):
        self.ins, self.out_shapes, self.sems, self.start, self.finish = list(ins), list(out_shapes), list(sems), start, finish


def _run_side(side, name):
    ni, no = len(side.ins), len(side.out_shapes)

    def body(*refs):
        ins, outs, sems = refs[:ni], refs[ni:ni + no], refs[ni + no:]
        side.start(ins, outs, *sems)
        side.finish(ins, outs, *sems)

    return pl.pallas_call(body, in_specs=[ANY] * ni, out_specs=[ANY] * no, out_shape=side.out_shapes,
                          scratch_shapes=[pltpu.SemaphoreType.DMA(s) for s in side.sems], name=name)(*side.ins)


def _matmul(a, b, *, ta=False, tb=False, tm, tn, tk, out_dtype=F32, res=None, side=None, name):
    m = a.shape[1] if ta else a.shape[0]
    k = a.shape[0] if ta else a.shape[1]
    n = b.shape[0] if tb else b.shape[1]
    assert k == (b.shape[1] if tb else b.shape[0])
    tm, tn, tk = min(tm, m), min(tn, n), min(tk, k)
    assert m % tm == 0 and n % tn == 0 and k % tk == 0, (name, m, n, k, tm, tn, tk)
    nk = k // tk
    dims = ((0 if ta else 1,), (1 if tb else 0,))
    has_res = res is not None
    n_in = 3 if has_res else 2
    n_side_in = len(side.ins) if side else 0
    n_side_out = len(side.out_shapes) if side else 0
    grid = (m // tm, n // tn, nk)

    def body(*refs):
        a_ref, b_ref = refs[0], refs[1]
        r_ref = refs[2] if has_res else None
        o_ref = refs[n_in + n_side_in]
        if side:
            side_ins = refs[n_in:n_in + n_side_in]
            side_outs = refs[n_in + n_side_in + 1:n_in + n_side_in + 1 + n_side_out]
            side_sems = refs[len(refs) - len(side.sems):]
            step = (pl.program_id(0) * grid[1] + pl.program_id(1)) * grid[2] + pl.program_id(2)

            @pl.when(step == 0)
            def _():
                side.start(side_ins, side_outs, *side_sems)

        def finish(acc):
            if has_res:
                acc = acc + r_ref[...].astype(F32)
            o_ref[...] = acc.astype(out_dtype)

        part = _bdot(a_ref[...], b_ref[...], dims)
        if nk == 1:
            finish(part)
        else:
            acc_ref = refs[n_in + n_side_in + 1 + n_side_out]
            kk = pl.program_id(2)

            @pl.when(kk == 0)
            def _():
                acc_ref[...] = part

            @pl.when(kk > 0)
            def _():
                acc_ref[...] += part

            @pl.when(kk == nk - 1)
            def _():
                finish(acc_ref[...])

        if side:
            @pl.when(step == grid[0] * grid[1] * grid[2] - 1)
            def _():
                side.finish(side_ins, side_outs, *side_sems)

    a_spec = pl.BlockSpec((tk, tm), lambda i, j, kk: (kk, i)) if ta else pl.BlockSpec((tm, tk), lambda i, j, kk: (i, kk))
    b_spec = pl.BlockSpec((tn, tk), lambda i, j, kk: (j, kk)) if tb else pl.BlockSpec((tk, tn), lambda i, j, kk: (kk, j))
    o_spec = pl.BlockSpec((tm, tn), lambda i, j, kk: (i, j))
    in_specs = [a_spec, b_spec] + ([o_spec] if has_res else []) + [ANY] * n_side_in
    est = 2 * (tm * tk * a.dtype.itemsize + tk * tn * b.dtype.itemsize + tm * tn * jnp.dtype(out_dtype).itemsize)
    est += 2 * tm * tn * 4 * (1 if has_res else 0) + (tm * tn * 4 if nk > 1 else 0) + 2 * tm * tn * 4
    args = (a, b) + ((res,) if has_res else ()) + (tuple(side.ins) if side else ())
    scratch = ([pltpu.VMEM((tm, tn), F32)] if nk > 1 else []) + ([pltpu.SemaphoreType.DMA(s) for s in side.sems] if side else [])
    sem = ("arbitrary",) * 3 if side else ("parallel", "parallel", "arbitrary")
    out = pl.pallas_call(
        body, grid=grid, in_specs=in_specs, out_specs=[o_spec] + [ANY] * n_side_out,
        out_shape=[_sds((m, n), out_dtype)] + (side.out_shapes if side else []),
        scratch_shapes=scratch, name=name, compiler_params=_params(sem, est))(*args)
    return (out[0], out[1:]) if side else out[0]


def _rms_fwd(x, w, ts, name):
    s = x.shape[0]

    def body(x_ref, w_ref, o_ref, ot_ref):
        y = _rms(x_ref[...], w_ref[...]).astype(BF16)
        o_ref[...] = y
        ot_ref[...] = y.T

    row = pl.BlockSpec((ts, D), lambda i: (i, 0))
    return pl.pallas_call(body, grid=(s // ts,), in_specs=[row, pl.BlockSpec((1, D), lambda i: (0, 0))],
                          out_specs=[row, pl.BlockSpec((D, ts), lambda i: (0, i))],
                          out_shape=[_sds((s, D), BF16), _sds((D, s), BF16)], name=name,
                          compiler_params=_params(("parallel",)))(x, w)


def _rms_bwd(x, w, du, dres, ts, name):
    s = x.shape[0]

    def body(x_ref, w_ref, du_ref, dres_ref, dx_ref, dw_ref):
        _, vjp = jax.vjp(_rms, x_ref[...], w_ref[...])
        dx, dw = vjp(du_ref[...].astype(F32))
        dx_ref[...] = dx + dres_ref[...]

        @pl.when(pl.program_id(0) == 0)
        def _():
            dw_ref[...] = jnp.zeros_like(dw_ref)

        dw_ref[...] += dw

    row = pl.BlockSpec((ts, D), lambda i: (i, 0))
    vec = pl.BlockSpec((1, D), lambda i: (0, 0))
    return pl.pallas_call(body, grid=(s // ts,), in_specs=[row, vec, row, row], out_specs=[row, vec],
                          out_shape=[_sds((s, D)), _sds((1, D))], name=name,
                          compiler_params=_params(("arbitrary",), 12 * ts * D * 4))(x, w, du, dres)


def _conv_taps(xx, w, base, ts):
    acc = xx[base:base + ts] * w[0:1, :]
    for j in range(1, 4):
        acc = acc + xx[base + j:base + j + ts] * w[j:j + 1, :]
    return acc


def _causal_conv(prev8, cur, w, first):
    xx = jnp.concatenate([jnp.where(first, 0.0, prev8), cur], axis=0)
    return _conv_taps(xx, w, 5, cur.shape[0])


def _qk_post(c, scale):
    s = _silu(c)
    return s * lax.rsqrt(jnp.sum(s * s, axis=-1, keepdims=True) + EPS) * scale


def _conv_specs(ts, cw, col0):
    pcol = O_QKV // cw + col0
    cur = pl.BlockSpec((ts, cw), lambda j, i: (i, pcol + j))
    prev = pl.BlockSpec((8, cw), lambda j, i: (jnp.maximum(i * (ts // 8) - 1, 0), pcol + j))
    wsp = pl.BlockSpec((4, cw), lambda j, i: (0, col0 + j))
    return cur, prev, wsp


def _gdn_qkv_fwd(proj, conv_w, ts):
    s = proj.shape[0]

    def qk_body(cur_ref, prev_ref, w_ref, o_ref, c_ref):
        c = _causal_conv(prev_ref[...], cur_ref[...], w_ref[...], pl.program_id(1) == 0)
        scale = jnp.where(pl.program_id(0) < H, DK ** -0.5, 1.0).astype(F32)
        c_ref[...] = c
        o_ref[...] = _qk_post(c, scale)

    tq = min(2 * ts, s)
    cur, prev, wsp = _conv_specs(tq, DK, 0)
    out = pl.BlockSpec((tq, DK), lambda j, i: (i, j))
    qk, c_qk = pl.pallas_call(qk_body, grid=(2 * H, s // tq), in_specs=[cur, prev, wsp], out_specs=[out, out],
                              out_shape=[_sds((s, 2 * H * DK))] * 2, name="gdn_qk_prep",
                              compiler_params=_params(("parallel", "parallel")))(proj, proj, conv_w)

    def v_body(cur_ref, prev_ref, w_ref, o_ref, c_ref):
        c = _causal_conv(prev_ref[...], cur_ref[...], w_ref[...], pl.program_id(1) == 0)
        c_ref[...] = c
        o_ref[...] = _silu(c)

    cw = 512
    cur, prev, wsp = _conv_specs(ts, cw, 2 * H * DK // cw)
    out = pl.BlockSpec((ts, cw), lambda j, i: (i, j))
    v, c_v = pl.pallas_call(v_body, grid=(H * DV // cw, s // ts), in_specs=[cur, prev, wsp], out_specs=[out, out],
                            out_shape=[_sds((s, H * DV))] * 2, name="gdn_v_prep",
                            compiler_params=_params(("parallel", "parallel")))(proj, proj, conv_w)
    return qk, v, c_qk, c_v


def _gdn_qkv_bwd(proj, conv_w, c_qk, c_v, dqk, dv, dproj, ts):
    s = proj.shape[0]
    nt = s // ts

    def qk_body(c_ref, d_ref, o_ref):
        scale = jnp.where(pl.program_id(0) < H, DK ** -0.5, 1.0).astype(F32)
        _, vjp = jax.vjp(lambda cc: _qk_post(cc, scale), c_ref[...])
        o_ref[...] = vjp(d_ref[...])[0]

    tq = min(2 * ts, s)
    blk = pl.BlockSpec((tq, DK), lambda j, i: (i, j))
    dc_qk = pl.pallas_call(qk_body, grid=(2 * H, s // tq), in_specs=[blk, blk], out_specs=blk, out_shape=_sds((s, 2 * H * DK)),
                           name="gdn_qk_prep_bwd", compiler_params=_params(("parallel", "parallel")))(c_qk, dqk)

    def v_body(c_ref, d_ref, o_ref):
        _, vjp = jax.vjp(_silu, c_ref[...])
        o_ref[...] = vjp(d_ref[...])[0]

    cw = 512
    blk = pl.BlockSpec((ts, cw), lambda j, i: (i, j))
    dc_v = pl.pallas_call(v_body, grid=(H * DV // cw, nt), in_specs=[blk, blk], out_specs=blk, out_shape=_sds((s, H * DV)),
                          name="gdn_v_prep_bwd", compiler_params=_params(("parallel", "parallel")))(c_v, dv)

    def conv_bwd(dc, dproj, col0, ncols, name):
        def body(x_ref, xprev_ref, w_ref, dc_ref, dcnext_ref, _, da_ref, dw_ref):
            i = pl.program_id(1)
            w = w_ref[...]
            dcur = dc_ref[...]
            dd = jnp.concatenate([dcur, jnp.where(i == nt - 1, 0.0, dcnext_ref[...])], axis=0)
            acc = dd[3:3 + ts] * w[0:1, :]
            for j in range(1, 4):
                acc = acc + dd[3 - j:3 - j + ts] * w[j:j + 1, :]
            da_ref[...] = acc.astype(BF16)
            xx = jnp.concatenate([jnp.where(i == 0, 0.0, xprev_ref[...]), x_ref[...]], axis=0)

            @pl.when(i == 0)
            def _():
                dw_ref[...] = jnp.zeros_like(dw_ref)

            for j in range(4):
                dw_ref[j:j + 1, :] += jnp.sum(dcur * xx[5 + j:5 + j + ts], axis=0, keepdims=True)

        cur, prev, wsp = _conv_specs(ts, cw, col0)
        dcur = pl.BlockSpec((ts, cw), lambda j, i: (i, j))
        dnext = pl.BlockSpec((8, cw), lambda j, i: (jnp.minimum((i + 1) * (ts // 8), s // 8 - 1), j))
        pcol = O_QKV // cw + col0
        return pl.pallas_call(body, grid=(ncols // cw, nt), in_specs=[cur, prev, wsp, dcur, dnext, ANY],
                              out_specs=[pl.BlockSpec((ts, cw), lambda j, i: (i, pcol + j)), pl.BlockSpec((4, cw), lambda j, i: (0, j))],
                              out_shape=[_sds(dproj.shape, BF16), _sds((4, ncols))], input_output_aliases={5: 0}, name=name,
                              compiler_params=_params(("parallel", "arbitrary")))(proj, proj, conv_w, dc, dc, dproj)

    dproj, dw_qk = conv_bwd(dc_qk, dproj, 0, 2 * H * DK, "conv_bwd_qk")
    dproj, dw_v = conv_bwd(dc_v, dproj, 2 * H * DK // cw, H * DV, "conv_bwd_v")
    return dproj, jnp.concatenate([dw_qk, dw_v], axis=1)


def _bg(b, a, alog, dtb):
    n = b.shape[0]
    g = -jnp.exp(alog) * jax.nn.softplus(a + dtb)
    row = lax.broadcasted_iota(jnp.int32, (n, n), 0)
    col = lax.broadcasted_iota(jnp.int32, (n, n), 1)
    shift = C.bit_length() - 1
    same = (row >> shift) == (col >> shift)
    return _sigmoid(b), _hdot((same & (row >= col)).astype(F32), g, NN), _hdot(same.astype(F32), g, NN)


def _bg_fwd(proj, alog, dtb, ts):
    s = proj.shape[0]

    def body(ba_ref, alog_ref, dtb_ref, beta_ref, gc_ref, gl_ref):
        beta_ref[...], gc_ref[...], gl_ref[...] = _bg(ba_ref[:, 0:H], ba_ref[:, H:2 * H], alog_ref[...], dtb_ref[...])

    small = pl.BlockSpec((1, H), lambda i: (0, 0))
    out = pl.BlockSpec((ts, H), lambda i: (i, 0))
    return pl.pallas_call(body, grid=(s // ts,), in_specs=[pl.BlockSpec((ts, 256), lambda i: (i, O_BA // 256)), small, small],
                          out_specs=[out] * 3, out_shape=[_sds((s, H))] * 3, name="gdn_bg_prep",
                          compiler_params=_params(("parallel",)))(proj, alog, dtb)


def _bg_bwd(proj, alog, dtb, dbeta_h, dgc_h, dgl_h, dproj, ts):
    s = proj.shape[0]

    def body(ba_ref, alog_ref, dtb_ref, dbeta_ref, dgc_ref, dgl_ref, _, dba_ref, dalog_ref, ddtb_ref):
        _, vjp = jax.vjp(_bg, ba_ref[:, 0:H], ba_ref[:, H:2 * H], alog_ref[...], dtb_ref[...])
        db, da, dalog, ddtb = vjp((jnp.sum(dbeta_ref[...], axis=0), jnp.sum(dgc_ref[...], axis=0), jnp.sum(dgl_ref[...], axis=0)))
        dba_ref[...] = jnp.zeros_like(dba_ref)
        dba_ref[:, 0:H] = db.astype(BF16)
        dba_ref[:, H:2 * H] = da.astype(BF16)

        @pl.when(pl.program_id(0) == 0)
        def _():
            dalog_ref[...] = jnp.zeros_like(dalog_ref)
            ddtb_ref[...] = jnp.zeros_like(ddtb_ref)

        dalog_ref[...] += dalog
        ddtb_ref[...] += ddtb

    small = pl.BlockSpec((1, H), lambda i: (0, 0))
    per_head = pl.BlockSpec((H, ts, H), lambda i: (0, i, 0))
    return pl.pallas_call(body, grid=(s // ts,),
                          in_specs=[pl.BlockSpec((ts, 256), lambda i: (i, O_BA // 256)), small, small, per_head, per_head, per_head, ANY],
                          out_specs=[pl.BlockSpec((ts, 256), lambda i: (i, O_BA // 256)), small, small],
                          out_shape=[_sds(dproj.shape, BF16), _sds((1, H)), _sds((1, H))], input_output_aliases={6: 0},
                          name="gdn_bg_prep_bwd", compiler_params=_params(("arbitrary",)))(proj, alog, dtb, dbeta_h, dgc_h, dgl_h, dproj)


BLK = 4 * C
NNB, NTB, TNB = ((2,), (1,)), ((2,), (2,)), ((1,), (1,))


def _bdot_b(a, b, dims):
    return lax.dot_general(a.astype(BF16), b.astype(BF16), (dims, ((0,), (0,))), preferred_element_type=F32)


@jax.custom_vjp
def _inv_unit_lower(a):
    n = a.shape[-1]
    row = lax.broadcasted_iota(jnp.int32, (n, n), 0)
    col = lax.broadcasted_iota(jnp.int32, (n, n), 1)
    x = jnp.where(row == col, 1.0, 0.0).astype(F32) - a
    p = _bdot_b(a, a, NNB)
    power = 2
    while True:
        x = x + _bdot_b(x, p, NNB)
        power *= 2
        if power >= C:
            return x
        p = _bdot_b(p, p, NNB)


def _inv_fwd(a):
    t = _inv_unit_lower(a)
    return t, t


def _inv_bwd(t, dt):
    return (-_bdot_b(_bdot_b(t, dt, TNB), t, NTB),)


_inv_unit_lower.defvjp(_inv_fwd, _inv_bwd)


def _gdn_prep(q, k, v, bfull, gcfull, glfull, hmask):
    nb, n = q.shape[0], q.shape[1]
    beta = jnp.sum(bfull * hmask, axis=-1, keepdims=True)
    gc = jnp.sum(gcfull * hmask, axis=-1, keepdims=True)
    gl = jnp.sum(glfull * hmask, axis=-1, keepdims=True)
    row = lax.broadcasted_iota(jnp.int32, (n, n), 0)
    col = lax.broadcasted_iota(jnp.int32, (n, n), 1)
    shift = C.bit_length() - 1
    same = (row >> shift) == (col >> shift)
    incl, strict = same & (row >= col), same & (row > col)
    g_i = gc * jnp.ones((1, 1, n), F32)
    decay = jnp.exp(jnp.where(incl, g_i - jnp.swapaxes(g_i, 1, 2), -jnp.inf))
    kb = k * beta
    a = jnp.where(strict, _bdot_b(kb, k, NTB) * decay, 0.0)
    tinv = _inv_unit_lower(a)
    u = _bdot_b(tinv, v * beta, NNB)
    w = _bdot_b(tinv, kb * jnp.exp(gc), NNB)
    attn = _bdot_b(q, k, NTB) * decay
    fold = ((lax.broadcasted_iota(jnp.int32, (n, C), 0) & (C - 1)) == lax.broadcasted_iota(jnp.int32, (n, C), 1)).astype(F32)
    attn_c = _bdot(attn.reshape(nb * n, n), fold, NN).reshape(nb, n, C)
    return u, w, attn_c, q * jnp.exp(gc), k * jnp.exp(gl - gc), jnp.exp(gl)


def _gdn_step(u, w, attn, qg, kd, egl, state):
    v_new = u - _bdot_b(w, state, NNB)
    o = _bdot_b(qg, state, NNB) + _bdot_b(attn, v_new, NNB)
    return o, state * egl + _bdot_b(kd, v_new, TNB)


def _heads(ref, rs, width):
    return jnp.stack([ref[rs, h * width:(h + 1) * width] for h in range(H)])


def _head_mask(h):
    return (lax.broadcasted_iota(jnp.int32, (1, H), 1) == h).astype(F32)


PREP_BLOCKS = 2


def _gdn_prep_specs(r):
    small = pl.BlockSpec((r, H), lambda h, c: (c, 0))
    return [pl.BlockSpec((r, DK), lambda h, c: (c, h)), pl.BlockSpec((r, DK), lambda h, c: (c, H + h)),
            pl.BlockSpec((r, DV), lambda h, c: (c, h)), small, small, small]


def _blocked(ref):
    x = ref[...]
    return x.reshape(PREP_BLOCKS, BLK, x.shape[-1])


def _gdn_inter_specs(r):
    col = pl.BlockSpec((r, DK), lambda h, c: (c, h))
    return [pl.BlockSpec((r, DV), lambda h, c: (c, h)), col, pl.BlockSpec((1, r, C), lambda h, c: (h, c, 0)), col, col,
            pl.BlockSpec((1, r // C, 8, 128), lambda h, c: (h, c, 0, 0))]


def _gdn_prep_fwd(qk, v, beta, gc, gl):
    s = qk.shape[0]
    r = PREP_BLOCKS * BLK

    def body(q_ref, k_ref, v_ref, b_ref, gc_ref, gl_ref, u_ref, w_ref, attn_ref, qg_ref, kd_ref, egl_ref):
        u, w, attn, qg, kd, egl = _gdn_prep(_blocked(q_ref), _blocked(k_ref), _blocked(v_ref), _blocked(b_ref), _blocked(gc_ref),
                                            _blocked(gl_ref), _head_mask(pl.program_id(0)))
        u_ref[...] = u.reshape(r, DV)
        w_ref[...] = w.reshape(r, DK).astype(BF16)
        attn_ref[0] = attn.reshape(r, C).astype(BF16)
        qg_ref[...] = qg.reshape(r, DK).astype(BF16)
        kd_ref[...] = kd.reshape(r, DK).astype(BF16)
        egl = egl.reshape(r, 1)
        for j in range(r // C):
            egl_ref[0, j] = egl[j * C:j * C + 1, :] * jnp.ones((8, 128), F32)

    out_shape = [_sds((s, H * DV)), _sds((s, H * DK), BF16), _sds((H, s, C), BF16), _sds((s, H * DK), BF16),
                 _sds((s, H * DK), BF16), _sds((H, s // C, 8, 128))]
    return pl.pallas_call(body, grid=(H, s // r), in_specs=_gdn_prep_specs(r), out_specs=_gdn_inter_specs(r), out_shape=out_shape,
                          name="gdn_prep_fwd", compiler_params=_params(("parallel", "parallel")))(qk, qk, v, beta, gc, gl)


def _gdn_prep_bwd(qk, v, beta, gc, gl, du, dw, dattn, dqg, dkd, degl):
    s = qk.shape[0]
    r = PREP_BLOCKS * BLK

    def body(q_ref, k_ref, v_ref, b_ref, gc_ref, gl_ref, du_ref, dw_ref, dattn_ref, dqg_ref, dkd_ref, degl_ref,
             dq_ref, dk_ref, dv_ref, db_ref, dgc_ref, dgl_ref):
        hmask = _head_mask(pl.program_id(0))
        _, vjp = jax.vjp(lambda q, k, v, b, gc, gl: _gdn_prep(q, k, v, b, gc, gl, hmask), _blocked(q_ref), _blocked(k_ref),
                         _blocked(v_ref), _blocked(b_ref), _blocked(gc_ref), _blocked(gl_ref))
        rowid = lax.broadcasted_iota(jnp.int32, (r, 1), 0)
        degl = jnp.zeros((r, 1), F32)
        for j in range(r // C):
            degl = jnp.where(rowid == j * C, degl_ref[0, j, 0:1, 0:1], degl)
        dq, dk, dv, db, dgc, dgl = vjp((_blocked(du_ref), _blocked(dw_ref), _blocked(dattn_ref.at[0]), _blocked(dqg_ref),
                                        _blocked(dkd_ref), degl.reshape(PREP_BLOCKS, BLK, 1)))
        dq_ref[...] = dq.reshape(r, DK)
        dk_ref[...] = dk.reshape(r, DK)
        dv_ref[...] = dv.reshape(r, DV)
        db_ref[0] = db.reshape(r, H)
        dgc_ref[0] = dgc.reshape(r, H)
        dgl_ref[0] = dgl.reshape(r, H)

    col = pl.BlockSpec((r, DK), lambda h, c: (c, h))
    piece = pl.BlockSpec((1, r, H), lambda h, c: (h, c, 0))
    dq, dk, dv, db, dgc, dgl = pl.pallas_call(
        body, grid=(H, s // r), in_specs=_gdn_prep_specs(r) + _gdn_inter_specs(r),
        out_specs=[col, col, pl.BlockSpec((r, DV), lambda h, c: (c, h)), piece, piece, piece],
        out_shape=[_sds((s, H * DK)), _sds((s, H * DK)), _sds((s, H * DV))] + [_sds((H, s, H))] * 3,
        name="gdn_prep_bwd", compiler_params=_params(("parallel", "parallel"), 24 << 20))(
            qk, qk, v, beta, gc, gl, du, dw, dattn, dqg, dkd, degl)
    return jnp.concatenate([dq, dk], axis=1), dv, db, dgc, dgl


def _gdn_scan_specs(r, order):
    wide = pl.BlockSpec((r, H * DK), lambda c: (order(c), 0))
    return [pl.BlockSpec((r, H * DV), lambda c: (order(c), 0)), wide, pl.BlockSpec((H, r, C), lambda c: (0, order(c), 0)), wide, wide,
            pl.BlockSpec((H, r // C, 8, 128), lambda c: (0, order(c), 0, 0))]


def _gdn_scan_fwd(u, w, attn, qg, kd, egl):
    s = u.shape[0]
    r = CPB * C
    nb = s // r

    def body(u_ref, w_ref, attn_ref, qg_ref, kd_ref, egl_ref, o_ref, st_ref, state_ref):
        @pl.when(pl.program_id(0) == 0)
        def _():
            state_ref[...] = jnp.zeros_like(state_ref)

        state = state_ref[...]
        for i in range(CPB):
            rs = slice(i * C, (i + 1) * C)
            st_ref[:, i] = state
            o, state = _gdn_step(_heads(u_ref, rs, DV), _heads(w_ref, rs, DK), attn_ref[:, rs, :], _heads(qg_ref, rs, DK),
                                 _heads(kd_ref, rs, DK), egl_ref[:, i, 0:1, 0:1], state)
            for h in range(H):
                o_ref[rs, h * DV:(h + 1) * DV] = o[h]
        state_ref[...] = state

    out_specs = [pl.BlockSpec((r, H * DV), lambda c: (c, 0)), pl.BlockSpec((H, CPB, DK, DV), lambda c: (0, c, 0, 0))]
    return pl.pallas_call(body, grid=(nb,), in_specs=_gdn_scan_specs(r, lambda c: c), out_specs=out_specs,
                          out_shape=[_sds((s, H * DV)), _sds((H, s // C, DK, DV))],
                          scratch_shapes=[pltpu.VMEM((H, DK, DV), F32)], name="gdn_scan_fwd",
                          compiler_params=_params(("arbitrary",), 24 << 20))(u, w, attn, qg, kd, egl)


def _gdn_scan_bwd(u, w, attn, qg, kd, egl, states, do):
    s = u.shape[0]
    r = CPB * C
    nb = s // r

    def body(u_ref, w_ref, attn_ref, qg_ref, kd_ref, egl_ref, st_ref, do_ref,
             du_ref, dw_ref, dattn_ref, dqg_ref, dkd_ref, degl_ref, dstate_ref):
        @pl.when(pl.program_id(0) == 0)
        def _():
            dstate_ref[...] = jnp.zeros_like(dstate_ref)

        dstate = dstate_ref[...]
        for i in reversed(range(CPB)):
            rs = slice(i * C, (i + 1) * C)
            _, vjp = jax.vjp(_gdn_step, _heads(u_ref, rs, DV), _heads(w_ref, rs, DK).astype(F32), attn_ref[:, rs, :].astype(F32),
                             _heads(qg_ref, rs, DK).astype(F32), _heads(kd_ref, rs, DK).astype(F32), egl_ref[:, i, 0:1, 0:1],
                             st_ref[:, i])
            du, dw, dattn, dqg, dkd, degl, dstate = vjp((_heads(do_ref, rs, DV), dstate))
            dattn_ref[:, rs, :] = dattn
            degl_ref[:, i] = degl * jnp.ones((1, 8, 128), F32)
            for h in range(H):
                du_ref[rs, h * DV:(h + 1) * DV] = du[h]
                dw_ref[rs, h * DK:(h + 1) * DK] = dw[h]
                dqg_ref[rs, h * DK:(h + 1) * DK] = dqg[h]
                dkd_ref[rs, h * DK:(h + 1) * DK] = dkd[h]
        dstate_ref[...] = dstate

    rev = lambda c: nb - 1 - c
    in_specs = _gdn_scan_specs(r, rev) + [pl.BlockSpec((H, CPB, DK, DV), lambda c: (0, rev(c), 0, 0)),
                                          pl.BlockSpec((r, H * DV), lambda c: (rev(c), 0))]
    return pl.pallas_call(
        body, grid=(nb,), in_specs=in_specs, out_specs=_gdn_scan_specs(r, rev),
        out_shape=[_sds((s, H * DV)), _sds((s, H * DK)), _sds((H, s, C)), _sds((s, H * DK)), _sds((s, H * DK)),
                   _sds((H, s // C, 8, 128))],
        scratch_shapes=[pltpu.VMEM((H, DK, DV), F32)], name="gdn_scan_bwd",
        compiler_params=_params(("arbitrary",), 40 << 20))(u, w, attn, qg, kd, egl, states, do)


def _rot(x, cs, sn):
    return x * cs + pltpu.roll(x, DK // 2, 1) * sn


def _rot_t(d, cs, sn):
    return d * cs - pltpu.roll(d, DK // 2, 1) * sn


def _ret_chunk(q, k, v, state, lg):
    n = q.shape[1]
    row = lax.broadcasted_iota(jnp.int32, (n, n), 0)
    col = lax.broadcasted_iota(jnp.int32, (n, n), 1)
    dist = (row - col).astype(F32)
    dmat = jnp.exp(jnp.where(dist >= 0, dist * lg, -jnp.inf))
    scores = _bdot_b(q, k, NTB) * dmat
    pos = lax.broadcasted_iota(jnp.int32, (n, 1), 0).astype(F32)
    xi = jnp.exp((pos + 1.0) * lg)
    zeta = jnp.exp((n - 1.0 - pos) * lg)
    o = _bdot_b(scores, v, NNB) + _bdot_b(q, state, NNB) * xi
    new_state = state * jnp.exp(n * lg) + _bdot_b(k * zeta, v, TNB)
    return o, new_state


def _ret_specs(r, order):
    return [pl.BlockSpec((r, H * RET_W), lambda c: (order(c), O_RET // (H * RET_W))), pl.BlockSpec((r, DK), lambda c: (order(c), 0)),
            pl.BlockSpec((r, DK), lambda c: (order(c), 0)), pl.BlockSpec((H, 1, 1), lambda c: (0, 0, 0))]


def _ret_qkv(x_ref, cs, sn):
    q = jnp.stack([_rot(x_ref[:, h * RET_W:h * RET_W + DK], cs, sn) for h in range(H)])
    k = jnp.stack([_rot(x_ref[:, h * RET_W + DK:h * RET_W + 2 * DK], cs, sn) for h in range(H)]) * DK ** -0.5
    v = jnp.stack([x_ref[:, h * RET_W + 2 * DK:(h + 1) * RET_W] for h in range(H)])
    return q, k, v


RET_C = 256


def _ret_scan_fwd(proj, cs, sn, lgtab):
    s = proj.shape[0]
    r = min(RET_C, s)
    nb = s // r

    def body(x_ref, cs_ref, sn_ref, lg_ref, o_ref, st_ref, state_ref):
        @pl.when(pl.program_id(0) == 0)
        def _():
            state_ref[...] = jnp.zeros_like(state_ref)

        state = state_ref[...]
        st_ref[:, 0] = state
        q, k, v = _ret_qkv(x_ref, cs_ref[...], sn_ref[...])
        o, state_ref[...] = _ret_chunk(q, k, v, state, lg_ref[...])
        for h in range(H):
            o_ref[:, h * DV:(h + 1) * DV] = o[h]

    out_specs = [pl.BlockSpec((r, H * DV), lambda c: (c, 0)), pl.BlockSpec((H, 1, DK, DV), lambda c: (0, c, 0, 0))]
    return pl.pallas_call(body, grid=(nb,), in_specs=_ret_specs(r, lambda c: c), out_specs=out_specs,
                          out_shape=[_sds((s, H * DV)), _sds((H, nb, DK, DV))],
                          scratch_shapes=[pltpu.VMEM((H, DK, DV), F32)], name="ret_scan_fwd",
                          compiler_params=_params(("arbitrary",), 32 << 20))(proj, cs, sn, lgtab.reshape(H, 1, 1))


def _ret_scan_bwd(proj, cs, sn, lgtab, states, do, dproj):
    s = proj.shape[0]
    r = min(RET_C, s)
    nb = s // r

    def body(x_ref, cs_ref, sn_ref, lg_ref, st_ref, do_ref, _, d_ref, dstate_ref):
        @pl.when(pl.program_id(0) == 0)
        def _():
            dstate_ref[...] = jnp.zeros_like(dstate_ref)

        cs_, sn_ = cs_ref[...], sn_ref[...]
        lg = lg_ref[...]
        q, k, v = _ret_qkv(x_ref, cs_, sn_)
        _, vjp = jax.vjp(lambda q, k, v, st: _ret_chunk(q, k, v, st, lg), q, k, v, st_ref[:, 0])
        dq, dk, dv, dstate_ref[...] = vjp((_heads(do_ref, slice(None), DV), dstate_ref[...]))
        for h in range(H):
            d_ref[:, h * RET_W:h * RET_W + DK] = _rot_t(dq[h], cs_, sn_).astype(BF16)
            d_ref[:, h * RET_W + DK:h * RET_W + 2 * DK] = _rot_t(dk[h] * DK ** -0.5, cs_, sn_).astype(BF16)
            d_ref[:, h * RET_W + 2 * DK:(h + 1) * RET_W] = dv[h].astype(BF16)

    rev = lambda c: nb - 1 - c
    in_specs = _ret_specs(r, rev) + [pl.BlockSpec((H, 1, DK, DV), lambda c: (0, rev(c), 0, 0)),
                                     pl.BlockSpec((r, H * DV), lambda c: (rev(c), 0)), ANY]
    return pl.pallas_call(
        body, grid=(nb,), in_specs=in_specs, out_specs=pl.BlockSpec((r, H * RET_W), lambda c: (rev(c), O_RET // (H * RET_W))),
        out_shape=_sds(dproj.shape, BF16), input_output_aliases={6: 0},
        scratch_shapes=[pltpu.VMEM((H, DK, DV), F32)], name="ret_scan_bwd",
        compiler_params=_params(("arbitrary",), 48 << 20))(proj, cs, sn, lgtab.reshape(H, 1, 1), states, do, dproj)


def _merge(oa, z, ob, rg, ga, gb, wa, wb):
    ya = oa * lax.rsqrt(jnp.mean(oa * oa, axis=-1, keepdims=True) + EPS) * wa * _silu(z)
    mu = jnp.mean(ob, axis=-1, keepdims=True)
    var = jnp.mean(jnp.square(ob - mu), axis=-1, keepdims=True)
    yb = (ob - mu) * lax.rsqrt(var + EPS) * wb * _silu(rg)
    return _sigmoid(ga) * ya + _sigmoid(gb) * yb


def _merge_specs(ts):
    own = pl.BlockSpec((ts, DV), lambda h, i: (i, h))
    grp = lambda k: pl.BlockSpec((ts, DV), lambda h, i: (i, O_MERGE // DV + 4 * h + k))
    return [own, grp(0), own, grp(1), grp(2), grp(3),
            pl.BlockSpec((1, DV), lambda h, i: (0, 0)), pl.BlockSpec((1, DV), lambda h, i: (0, h))]


def _merge_fwd(oa, ob, proj, wa, wb, ts):
    s = oa.shape[0]

    def body(oa_ref, z_ref, ob_ref, rg_ref, ga_ref, gb_ref, wa_ref, wb_ref, o_ref, ot_ref):
        y = _merge(oa_ref[...], z_ref[...], ob_ref[...], rg_ref[...], ga_ref[...], gb_ref[...],
                   wa_ref[...], wb_ref[...]).astype(BF16)
        o_ref[...] = y
        ot_ref[...] = y.T

    return pl.pallas_call(body, grid=(H, s // ts), in_specs=_merge_specs(ts),
                          out_specs=[pl.BlockSpec((ts, DV), lambda h, i: (i, h)), pl.BlockSpec((DV, ts), lambda h, i: (h, i))],
                          out_shape=[_sds((s, H * DV), BF16), _sds((H * DV, s), BF16)], name="merge_fwd",
                          compiler_params=_params(("parallel", "parallel")))(oa, proj, ob, proj, proj, proj, wa, wb)


def _merge_bwd(oa, ob, proj, wa, wb, dmixed, ts):
    s = oa.shape[0]

    def body(oa_ref, z_ref, ob_ref, rg_ref, ga_ref, gb_ref, wa_ref, wb_ref, dm_ref,
             doa_ref, dob_ref, dgrp_ref, dwa_ref, dwb_ref):
        _, vjp = jax.vjp(_merge, oa_ref[...], z_ref[...], ob_ref[...], rg_ref[...], ga_ref[...], gb_ref[...],
                         wa_ref[...], wb_ref[...])
        doa, dz, dob, drg, dga, dgb, dwa, dwb = vjp(dm_ref[...].astype(F32))
        doa_ref[...] = doa
        dob_ref[...] = dob
        for k, d in enumerate((dz, drg, dga, dgb)):
            dgrp_ref[:, k * DV:(k + 1) * DV] = d.astype(BF16)
        first_tile = pl.program_id(1) == 0

        @pl.when(first_tile & (pl.program_id(0) == 0))
        def _():
            dwa_ref[...] = jnp.zeros_like(dwa_ref)

        @pl.when(first_tile)
        def _():
            dwb_ref[...] = jnp.zeros_like(dwb_ref)

        dwa_ref[...] += dwa
        dwb_ref[...] += dwb

    blk = pl.BlockSpec((ts, DV), lambda h, i: (i, h))
    out_specs = [blk, blk, pl.BlockSpec((ts, MERGE_W), lambda h, i: (i, O_MERGE // MERGE_W + h)),
                 pl.BlockSpec((1, DV), lambda h, i: (0, 0)), pl.BlockSpec((1, DV), lambda h, i: (0, h))]
    out_shape = [_sds((s, H * DV)), _sds((s, H * DV)), _sds((s, P_IN), BF16), _sds((1, DV)), _sds((1, H * DV))]
    return pl.pallas_call(body, grid=(H, s // ts), in_specs=_merge_specs(ts) + [blk], out_specs=out_specs, out_shape=out_shape,
                          name="merge_bwd", compiler_params=_params(("arbitrary", "arbitrary"), 40 * ts * DV * 4))(
                              oa, proj, ob, proj, proj, proj, wa, wb, dmixed)


def _act(hg, hu):
    return _silu(hg) * hu


def _ffn_gate_up(hn, w_gate, w_up, tm, tn):
    s, f = hn.shape[0], w_gate.shape[1]
    tm, tn = min(tm, s), min(tn, f)
    assert s % tm == 0 and f % tn == 0 and tm % 256 == 0
    sub = tm // 2

    def body(a_ref, wg_ref, wu_ref, pg_ref, pu_ref, act_ref, actt_ref):
        for r0 in range(0, tm, sub):
            rs = slice(r0, r0 + sub)
            hg = _bdot(a_ref[rs, :], wg_ref[...], NN)
            hu = _bdot(a_ref[rs, :], wu_ref[...], NN)
            y, vjp = jax.vjp(_act, hg, hu)
            pg, pu = vjp(jnp.ones_like(y))
            y = y.astype(BF16)
            pg_ref[rs, :] = pg.astype(BF16)
            pu_ref[rs, :] = pu.astype(BF16)
            act_ref[rs, :] = y
            actt_ref[:, rs] = y.T

    wsp = pl.BlockSpec((D, tn), lambda i, j: (0, j))
    blk = pl.BlockSpec((tm, tn), lambda i, j: (i, j))
    est = 2 * (tm * D * 2 + 2 * D * tn * 2 + 4 * tm * tn * 2) + 4 * sub * tn * 4
    return pl.pallas_call(body, grid=(s // tm, f // tn), in_specs=[pl.BlockSpec((tm, D), lambda i, j: (i, 0)), wsp, wsp],
                          out_specs=[blk, blk, blk, pl.BlockSpec((tn, tm), lambda i, j: (j, i))],
                          out_shape=[_sds((s, f), BF16)] * 3 + [_sds((f, s), BF16)], name="ffn_gate_up",
                          compiler_params=_params(("parallel", "parallel"), est))(hn, w_gate, w_up)


def _ffn_down_dx(dh2, w_down, pg, pu, tm, tn):
    s, f = pg.shape
    tm, tn = min(tm, s), min(tn, f)
    assert s % tm == 0 and f % tn == 0 and tm % 256 == 0
    sub = tm // 2

    def body(d_ref, w_ref, pg_ref, pu_ref, dhg_ref, dhu_ref):
        for r0 in range(0, tm, sub):
            rs = slice(r0, r0 + sub)
            dact = _bdot(d_ref[rs, :], w_ref[...], NT)
            dhg_ref[rs, :] = (dact * pg_ref[rs, :].astype(F32)).astype(BF16)
            dhu_ref[rs, :] = (dact * pu_ref[rs, :].astype(F32)).astype(BF16)

    blk = pl.BlockSpec((tm, tn), lambda i, j: (i, j))
    est = 2 * (tm * D * 4 + tn * D * 2 + 4 * tm * tn * 2) + 6 * sub * tn * 4
    return pl.pallas_call(body, grid=(s // tm, f // tn),
                          in_specs=[pl.BlockSpec((tm, D), lambda i, j: (i, 0)), pl.BlockSpec((tn, D), lambda i, j: (j, 0)), blk, blk],
                          out_specs=[blk, blk], out_shape=[_sds((s, f), BF16)] * 2, name="ffn_down_dx",
                          compiler_params=_params(("parallel", "parallel"), est))(dh2, w_down, pg, pu)


def _loss_rows(h2, wf, tgt):
    err = _rms(h2, wf) - tgt
    return 0.5 * jnp.sum(jnp.mean(err * err, axis=-1, keepdims=True), keepdims=True)


def _loss_fwd_bwd(h2, wf, tgt, ts):
    s = h2.shape[0]

    def body(h_ref, w_ref, t_ref, loss_ref, dh_ref, dhb_ref, dw_ref):
        loss, vjp = jax.vjp(_loss_rows, h_ref[...], w_ref[...], t_ref[...])
        dh, dw, _ = vjp(jnp.ones((1, 1), F32))
        dh_ref[...] = dh
        dhb_ref[...] = dh.astype(BF16)

        @pl.when(pl.program_id(0) == 0)
        def _():
            loss_ref[...] = jnp.zeros_like(loss_ref)
            dw_ref[...] = jnp.zeros_like(dw_ref)

        loss_ref[...] += loss
        dw_ref[...] += dw

    row = pl.BlockSpec((ts, D), lambda i: (i, 0))
    vec = pl.BlockSpec((1, D), lambda i: (0, 0))
    tile = pl.BlockSpec((8, 128), lambda i: (0, 0))
    return pl.pallas_call(body, grid=(s // ts,), in_specs=[row, vec, row], out_specs=[tile, row, row, vec],
                          out_shape=[_sds((8, 128)), _sds((s, D)), _sds((s, D), BF16), _sds((1, D))], name="final_norm_loss",
                          compiler_params=_params(("arbitrary",), 12 * ts * D * 4))(h2, wf, tgt)


def _rope_tables(s):
    inv = ROPE_BASE ** (-jnp.arange(0, DK, 2, dtype=F32) / DK)
    ang = jnp.arange(s, dtype=F32)[:, None] * inv[None, :]
    cos, sin = jnp.cos(ang), jnp.sin(ang)
    return jnp.concatenate([cos, cos], axis=1), jnp.concatenate([-sin, sin], axis=1)


def _local_step(x, tgt, w_in, w_out, w_gate, w_up, w_down, norm1_w, conv_w, a_log, dt_bias, gdn_norm_w, ret_norm_w, norm2_w, norm_f_w,
                dist=None):
    s = x.shape[0]
    ts = min(512, s)
    cs, sn = _rope_tables(s)
    lgtab = jnp.log1p(-jnp.exp2(-5.0 - jnp.arange(H, dtype=F32))).reshape(1, H)

    u, u_t = _rms_fwd(x, norm1_w, ts, "norm1_fwd")
    if dist is None:
        proj = _matmul(u, w_in, tm=1024, tn=1280, tk=D, name="in_proj")
    else:
        proj, gathered = _matmul(u, w_in, tm=1024, tn=1280, tk=D, side=_gather_side(dist["shards"]), name="in_proj")
        w_out, w_gate, w_up, w_down = (gathered[0].reshape(D, D), _from_slots_cols(gathered[1]), _from_slots_cols(gathered[2]),
                                       gathered[3].reshape(-1, D))
    qk, va, c_qk, c_v = _gdn_qkv_fwd(proj, conv_w, ts)
    beta, gc, gl = _bg_fwd(proj, a_log, dt_bias, ts)
    inter = _gdn_prep_fwd(qk, va, beta, gc, gl)
    oa, st_a = _gdn_scan_fwd(*inter)
    ob, st_b = _ret_scan_fwd(proj, cs, sn, lgtab)
    mixed, mixed_t = _merge_fwd(oa, ob, proj, gdn_norm_w, ret_norm_w, min(2 * ts, s))
    h1 = _matmul(mixed, w_out, tm=1024, tn=1024, tk=D, res=x, name="out_proj")
    hn, hn_t = _rms_fwd(h1, norm2_w, ts, "norm2_fwd")
    act_dg, act_du, act, act_t = _ffn_gate_up(hn, w_gate, w_up, 1024, 512)
    h2 = _matmul(act, w_down, tm=512, tn=1024, tk=5632, res=h1, name="ffn_down")
    loss, dh2, dh2_b, d_norm_f = _loss_fwd_bwd(h2, norm_f_w, tgt, ts)

    dhg, dhu = _ffn_down_dx(dh2_b, w_down, act_dg, act_du, 1024, 512)
    g_down = _matmul(act_t, dh2_b, tm=512, tn=512, tk=8192, out_dtype=BF16, name="ffn_down_dw")
    g_gate = _matmul(hn_t, dhg, tm=512, tn=512, tk=8192, out_dtype=BF16, name="ffn_gate_dw")
    g_up = _matmul(hn_t, dhu, tm=512, tn=512, tk=8192, out_dtype=BF16, name="ffn_up_dw")
    dhn = _matmul(dhg, w_gate, tb=True, tm=512, tn=1024, tk=5632, name="ffn_gate_dx")
    dhn = _matmul(dhu, w_up, tb=True, tm=512, tn=1024, tk=5632, res=dhn, name="ffn_up_dx")
    dh1, d_norm2 = _rms_bwd(h1, norm2_w, dhn, dh2, ts, "norm2_bwd")

    g_out = _matmul(mixed_t, dh1, tm=1024, tn=1024, tk=D, out_dtype=BF16, name="out_proj_dw")
    early = ["w_out", "w_gate", "w_up", "w_down"]
    if dist is None:
        dmixed = _matmul(dh1, w_out, tb=True, tm=1024, tn=1024, tk=D, out_dtype=BF16, name="out_proj_dx")
    else:
        slots = dict(w_out=g_out.reshape(NDEV, D // NDEV, D), w_gate=_to_slots_cols(g_gate), w_up=_to_slots_cols(g_up),
                     w_down=g_down.reshape(NDEV, -1, D))
        dmixed, from_sibling = _matmul(dh1, w_out, tb=True, tm=1024, tn=1024, tk=D, out_dtype=BF16,
                                       side=_sibling_side([slots[k] for k in early]), name="out_proj_dx")
        parts = [_add_sibling(slots[k], r, dist["core"], 128, "grads_add_" + k) for k, r in zip(early, from_sibling)]
    doa, dob, dproj, d_gdn_norm, d_ret_norm = _merge_bwd(oa, ob, proj, gdn_norm_w, ret_norm_w, dmixed, ts)

    dproj = _ret_scan_bwd(proj, cs, sn, lgtab, st_b, dob, dproj)
    d_inter = _gdn_scan_bwd(*inter, st_a, doa)
    dqk, dva, dbeta_h, dgc_h, dgl_h = _gdn_prep_bwd(qk, va, beta, gc, gl, *d_inter)
    dproj, d_conv = _gdn_qkv_bwd(proj, conv_w, c_qk, c_v, dqk, dva, dproj, ts)
    dproj, d_a_log, d_dt_bias = _bg_bwd(proj, a_log, dt_bias, dbeta_h, dgc_h, dgl_h, dproj, ts)

    if dist is None:
        g_in = _matmul(u_t, dproj, tm=1024, tn=1280, tk=D, out_dtype=BF16, name="in_proj_dw")
        du = _matmul(dproj, w_in, tb=True, tm=1024, tn=1024, tk=1664, name="in_proj_dx")
        big = dict(w_in=g_in, w_out=g_out, w_gate=g_gate, w_up=g_up, w_down=g_down)
    else:
        g_in, from_chips = _matmul(u_t, dproj, tm=1024, tn=1280, tk=D, out_dtype=BF16, side=_chips_side(parts), name="in_proj_dw")
        du, (from_all,) = _matmul(dproj, w_in, tb=True, tm=1024, tn=1024, tk=1664,
                                  side=_all_to_all_side(_windows_from_layout(g_in)), name="in_proj_dx")
        big = dict(w_in=from_all, **{k: (p, r) for k, p, r in zip(early, parts, from_chips)})
    dx, d_norm1 = _rms_bwd(x, norm1_w, du, dh1, ts, "norm1_bwd")

    small = dict(norm1_w=d_norm1, conv_w=d_conv, a_log=d_a_log, dt_bias=d_dt_bias, gdn_norm_w=d_gdn_norm,
                 ret_norm_w=d_ret_norm, norm2_w=d_norm2, norm_f_w=d_norm_f)
    return loss, dx, big, small


def _coords():
    return lax.axis_index("x"), lax.axis_index("y"), lax.axis_index("c")


def _gather_side(shards):
    n = len(shards)

    def plan(ins, outs, send_sems, recv_sems, local_sems):
        x, y, c = _coords()
        me, sibling = (x, y, c), (x, y, 1 - c)
        chips = [(1 - x, y), (x, 1 - y), (1 - x, 1 - y)]

        def copy(a, k, block, to, src=None):
            px, py, pc = block
            dst = outs[a].at[4 * px + 2 * py + pc]
            return pltpu.make_async_remote_copy(src_ref=dst if src is None else src, dst_ref=dst, send_sem=send_sems.at[a, k],
                                                recv_sem=recv_sems.at[a, k], device_id=to, device_id_type=MESH)

        mine = [pltpu.make_async_copy(ins[a], outs[a].at[4 * x + 2 * y + c], local_sems.at[a]) for a in range(n)]
        first = []
        for a in range(n):
            first.append(copy(a, 0, me, sibling, src=ins[a]))
            first += [copy(a, 1 + j, me, (*chip, c), src=ins[a]) for j, chip in enumerate(chips)]
        return c, me, sibling, chips, copy, mine, first

    def start(ins, outs, *sems):
        *_, mine, first = plan(ins, outs, *sems)
        for cp in mine + first:
            cp.start()

    def finish(ins, outs, *sems):
        c, me, sibling, chips, copy, mine, first = plan(ins, outs, *sems)
        passed = []
        for j, chip in enumerate(chips):
            for a in range(n):
                copy(a, 1 + j, (*chip, c), me).wait_recv()
                fwd = copy(a, 4 + j, (*chip, c), sibling)
                fwd.start()
                passed.append(fwd)
        for a in range(n):
            copy(a, 0, sibling, me).wait_recv()
            for j, chip in enumerate(chips):
                copy(a, 4 + j, (*chip, 1 - c), me).wait_recv()
        for cp in first + passed:
            cp.wait_send()
        for cp in mine:
            cp.wait()

    return _Side(shards, [_sds((NDEV,) + a.shape, a.dtype) for a in shards], [(n, 7), (n, 7), (n,)], start, finish)


def _exchange_side(ins, n_out, copies_of):
    def start(in_refs, out_refs, *sems):
        for cp in copies_of(in_refs, out_refs, *sems):
            cp.start()

    def finish(in_refs, out_refs, *sems):
        for cp in copies_of(in_refs, out_refs, *sems):
            cp.wait()

    n = len(ins)
    return _Side(ins, [_sds((n_out,) + a.shape[1:], a.dtype) for a in ins], [(n, n_out), (n, n_out)], start, finish)


def _sibling_side(slots):
    def copies_of(ins, outs, send_sems, recv_sems):
        x, y, c = _coords()
        return [pltpu.make_async_remote_copy(
            src_ref=ins[a].at[2 * j + (1 - c)], dst_ref=outs[a].at[j], send_sem=send_sems.at[a, j], recv_sem=recv_sems.at[a, j],
            device_id=(x, y, 1 - c), device_id_type=MESH) for a in range(len(slots)) for j in range(4)]

    return _exchange_side(slots, 4, copies_of)


def _chips_side(parts):
    def copies_of(ins, outs, send_sems, recv_sems):
        x, y, c = _coords()
        chips = [(1 - x, y), (x, 1 - y), (1 - x, 1 - y)]
        return [pltpu.make_async_remote_copy(
            src_ref=ins[a].at[2 * px + py], dst_ref=outs[a].at[k], send_sem=send_sems.at[a, k], recv_sem=recv_sems.at[a, k],
            device_id=(px, py, c), device_id_type=MESH) for a in range(len(parts)) for k, (px, py) in enumerate(chips)]

    return _exchange_side(parts, 3, copies_of)


def _all_to_all_side(slots):
    def plan(ins, outs, send_sems, recv_sems, local_sems):
        x, y, c = _coords()
        mine = 4 * x + 2 * y + c
        own = pltpu.make_async_copy(ins[0].at[mine], outs[0].at[mine], local_sems.at[0])
        remote = []
        for r in range(1, NDEV):
            peer = (x ^ (r >> 2), y ^ ((r >> 1) & 1), c ^ (r & 1))
            remote.append(pltpu.make_async_remote_copy(
                src_ref=ins[0].at[mine ^ r], dst_ref=outs[0].at[mine], send_sem=send_sems.at[r - 1], recv_sem=recv_sems.at[r - 1],
                device_id=peer, device_id_type=MESH))
        return own, remote

    def start(ins, outs, *sems):
        own, remote = plan(ins, outs, *sems)
        for cp in [own] + remote:
            cp.start()

    def finish(ins, outs, *sems):
        own, remote = plan(ins, outs, *sems)
        for cp in remote:
            cp.wait()
        own.wait()

    return _Side([slots], [_sds(slots.shape, slots.dtype)], [(NDEV - 1,), (NDEV - 1,), (1,)], start, finish)


def _allreduce_small(pack, name):
    rows, cols = pack.shape

    def body(in_ref, out_ref, buf_ref, send_sems, recv_sems):
        x, y, c = _coords()
        mine = 4 * x + 2 * y + c
        buf_ref[mine] = in_ref[...]
        copies = []
        for r in range(1, NDEV):
            peer = (x ^ (r >> 2), y ^ ((r >> 1) & 1), c ^ (r & 1))
            copies.append(pltpu.make_async_remote_copy(
                src_ref=in_ref, dst_ref=buf_ref.at[mine], send_sem=send_sems.at[r - 1], recv_sem=recv_sems.at[r - 1],
                device_id=peer, device_id_type=MESH))
        for cp in copies:
            cp.start()
        for r in range(1, NDEV):
            pltpu.make_async_remote_copy(
                src_ref=in_ref, dst_ref=buf_ref.at[mine ^ r], send_sem=send_sems.at[r - 1], recv_sem=recv_sems.at[r - 1],
                device_id=(x, y, c), device_id_type=MESH).wait_recv()
        for cp in copies:
            cp.wait_send()
        acc = buf_ref[0]
        for d in range(1, NDEV):
            acc = acc + buf_ref[d]
        out_ref[...] = acc

    return pl.pallas_call(
        body, in_specs=[VMEM_FULL], out_specs=VMEM_FULL, out_shape=_sds((rows, cols)),
        scratch_shapes=[pltpu.VMEM((NDEV, rows, cols), F32), pltpu.SemaphoreType.DMA((NDEV - 1,)), pltpu.SemaphoreType.DMA((NDEV - 1,))],
        name=name)(pack)


def _add_sibling(slots, recv, core, tr, name):
    _, rows, cols = slots.shape
    tr = _row_tile(rows, tr)

    def body(c_ref, a_ref, b_ref, o_ref):
        o_ref[...] = (a_ref[...].astype(F32) + b_ref[...].astype(F32)).astype(BF16)

    gs = pltpu.PrefetchScalarGridSpec(
        num_scalar_prefetch=1, grid=(4, rows // tr),
        in_specs=[pl.BlockSpec((None, tr, cols), lambda j, i, cr: (2 * j + cr[0], i, 0)),
                  pl.BlockSpec((None, tr, cols), lambda j, i, cr: (j, i, 0))],
        out_specs=pl.BlockSpec((None, tr, cols), lambda j, i, cr: (j, i, 0)))
    return pl.pallas_call(body, grid_spec=gs, out_shape=_sds((4, rows, cols), BF16), name=name,
                          compiler_params=_params(("parallel", "parallel"), 6 * tr * cols * 4))(core, slots, recv)


def _adam_math(w, g, m, v):
    m2 = B1 * m + (1.0 - B1) * g
    v2 = B2 * v + (1.0 - B2) * jnp.square(g)
    m_hat = m2 / (1.0 - B1 ** STEP)
    v_hat = v2 / (1.0 - B2 ** STEP)
    return -LR * (m_hat / (jnp.sqrt(v_hat) + EPS_ADAM) + WD * w), m2, v2


def _adamw_reduced(part, recv, chip, w, m, v, tr, name):
    rows, cols = w.shape
    tr = _row_tile(rows, tr)

    def body(j_ref, p_ref, r0_ref, r1_ref, r2_ref, w_ref, m_ref, v_ref, g_ref, d_ref, nm_ref, nv_ref):
        g = p_ref[...].astype(F32) + r0_ref[...].astype(F32) + r1_ref[...].astype(F32) + r2_ref[...].astype(F32)
        d, m2, v2 = _adam_math(w_ref[...], g, m_ref[...], v_ref[...])
        g_ref[...] = g
        d_ref[...] = d
        nm_ref[...] = m2
        nv_ref[...] = v2

    flat = pl.BlockSpec((tr, cols), lambda i, jr: (i, 0))
    gs = pltpu.PrefetchScalarGridSpec(
        num_scalar_prefetch=1, grid=(rows // tr,),
        in_specs=[pl.BlockSpec((None, tr, cols), lambda i, jr: (jr[0], i, 0))]
        + [pl.BlockSpec((None, tr, cols), functools.partial(lambda i, jr, k: (k, i, 0), k=k)) for k in range(3)] + [flat] * 3,
        out_specs=[flat] * 4)
    return pl.pallas_call(body, grid_spec=gs, out_shape=[_sds((rows, cols))] * 4, name=name,
                          compiler_params=_params(("parallel",), 22 * tr * cols * 4))(chip, part, recv, recv, recv, w, m, v)


def _sum_slots(recv, tr, name):
    _, rows, cols = recv.shape
    tr = _row_tile(rows, tr)

    def body(*refs):
        acc = refs[0][...].astype(F32)
        for p_ref in refs[1:NDEV]:
            acc = acc + p_ref[...].astype(F32)
        refs[NDEV][...] = acc

    slot = [pl.BlockSpec((None, tr, cols), functools.partial(lambda i, k: (k, i, 0), k=k)) for k in range(NDEV)]
    return pl.pallas_call(body, grid=(rows // tr,), in_specs=slot, out_specs=pl.BlockSpec((tr, cols), lambda i: (i, 0)),
                          out_shape=_sds((rows, cols)), name=name, compiler_params=_params(("parallel",)))(*([recv] * NDEV))


def _adamw_rows(w, g, m, v, tr, name):
    rows, cols = w.shape
    tr = _row_tile(rows, tr)

    def body(w_ref, g_ref, m_ref, v_ref, d_ref, nm_ref, nv_ref):
        d_ref[...], nm_ref[...], nv_ref[...] = _adam_math(w_ref[...], g_ref[...], m_ref[...], v_ref[...])

    flat = pl.BlockSpec((tr, cols), lambda i: (i, 0))
    return pl.pallas_call(body, grid=(rows // tr,), in_specs=[flat] * 4, out_specs=[flat] * 3, out_shape=[_sds((rows, cols))] * 3,
                          name=name, compiler_params=_params(("parallel",)))(w, g, m, v)


def _adamw_plain(w, g, m, v, name):
    def body(w_ref, g_ref, m_ref, v_ref, d_ref, nm_ref, nv_ref):
        d, m2, v2 = _adam_math(w_ref[...], g_ref[...], m_ref[...], v_ref[...])
        d_ref[...] = d
        nm_ref[...] = m2
        nv_ref[...] = v2

    return pl.pallas_call(body, out_shape=[_sds(w.shape)] * 3, name=name)(w, g, m, v)


def _pack_small(norm1_w, conv_w, a_log, dt_bias, gdn_norm_w, ret_norm_w, norm2_w, norm_f_w):
    misc = jnp.concatenate([gdn_norm_w.reshape(1, DV), a_log.reshape(1, H), dt_bias.reshape(1, H),
                            jnp.zeros((1, D - DV - 2 * H), F32)], axis=1)
    return jnp.concatenate([norm1_w.reshape(1, D), ret_norm_w.reshape(1, D), norm2_w.reshape(1, D), norm_f_w.reshape(1, D),
                            conv_w.reshape(8, D), misc, jnp.zeros((3, D), F32)], axis=0)


def _unpack_small(pack):
    return dict(norm1_w=pack[0:1], ret_norm_w=pack[1:2], norm2_w=pack[2:3], norm_f_w=pack[3], conv_w=pack[4:12].reshape(4, 2 * D),
                gdn_norm_w=pack[12:13, 0:DV], a_log=pack[12:13, DV:DV + H], dt_bias=pack[12:13, DV + H:DV + 2 * H])


IN_SPLITS = (4096, 2048, 8, 8, 1024, 1024, 2048, 2048, 2048, 2048)


BA_END = sum(IN_SPLITS[:4])
LANES = 128


def _padded_order_blocks():
    z0, ba0, rq0, rk0, rv0, rg0, ga0, gb0 = 4096, 6144, 6400, 7424, 8448, 10496, 12544, 14592
    cols = []
    for h in range(H):
        for base in (z0, rg0, ga0, gb0):
            cols += [base + DV * h, base + DV * h + LANES]
    cols += list(range(0, z0, LANES))
    for h in range(H):
        cols += [rq0 + DK * h, rk0 + DK * h, rv0 + DV * h, rv0 + DV * h + LANES]
    cols += [ba0, ba0 + LANES]
    blocks = np.asarray(cols, np.int32) // LANES
    assert sorted(blocks.tolist()) == list(range(P_IN // LANES))
    return blocks


def _permute_blocks(x, blocks, name):
    rows, cols = x.shape

    def body(p_ref, x_ref, o_ref):
        o_ref[...] = x_ref[...]

    gs = pltpu.PrefetchScalarGridSpec(num_scalar_prefetch=1, grid=(cols // LANES,),
                                      in_specs=[pl.BlockSpec((rows, LANES), lambda j, p: (0, p[j]))],
                                      out_specs=pl.BlockSpec((rows, LANES), lambda j, p: (0, j)))
    return pl.pallas_call(body, grid_spec=gs, out_shape=_sds((rows, cols), x.dtype), name=name,
                          compiler_params=_params(("parallel",)))(jnp.asarray(blocks), x)


def _regroup_w_in(w):
    padded = jnp.concatenate([w[:, :BA_END], jnp.zeros((w.shape[0], P_IN - N_IN), w.dtype), w[:, BA_END:]], axis=1)
    return _permute_blocks(padded, _padded_order_blocks(), "w_in_to_layout")


def _ungroup_w_in(g):
    padded = _permute_blocks(g, np.argsort(_padded_order_blocks()).astype(np.int32), "w_in_grad_from_layout")
    return jnp.concatenate([padded[:, :BA_END], padded[:, BA_END + P_IN - N_IN:]], axis=1)


SHARD_W = N_IN // NDEV
GAP = P_IN - N_IN
WIN = 2304


def _padded_col(c):
    return c + (GAP if c >= BA_END else 0)


WIN_START = [min(_padded_col(SHARD_W * d) // LANES * LANES, P_IN - WIN) for d in range(NDEV)]
WIN_OFF = [_padded_col(SHARD_W * d) - WIN_START[d] for d in range(NDEV)]
STRADDLER = BA_END // SHARD_W
STRADDLE_AT = BA_END - STRADDLER * SHARD_W
assert all(WIN_OFF[d] + SHARD_W + (GAP if d == STRADDLER else 0) <= WIN for d in range(NDEV))


def _win_off(me):
    off = jnp.int32(0)
    for d in range(NDEV):
        off = jnp.where(me == d, jnp.int32(WIN_OFF[d]), off)
    return off


def _window_of_shard(shard, me):
    rows = shard.shape[0]
    zeros = lambda n: jnp.zeros((rows, n), shard.dtype)
    plain = lax.dynamic_update_slice(zeros(WIN), shard, (0, _win_off(me)))
    o = WIN_OFF[STRADDLER]
    split = jnp.concatenate([zeros(o), shard[:, :STRADDLE_AT], zeros(GAP), shard[:, STRADDLE_AT:], zeros(WIN - o - GAP - SHARD_W)], axis=1)
    return jnp.where(me == STRADDLER, split, plain)


def _shard_of_window(win, me):
    plain = lax.dynamic_slice(win, (0, _win_off(me)), (win.shape[0], SHARD_W))
    o = WIN_OFF[STRADDLER]
    split = jnp.concatenate([win[:, o:o + STRADDLE_AT], win[:, o + STRADDLE_AT + GAP:o + GAP + SHARD_W]], axis=1)
    return jnp.where(me == STRADDLER, split, plain)


def _layout_from_windows(wins):
    _, rows, _ = wins.shape
    data = []
    for d in range(NDEV):
        lo = _padded_col(SHARD_W * d)
        data.append([(lo, lo + STRADDLE_AT), (lo + STRADDLE_AT + GAP, lo + GAP + SHARD_W)] if d == STRADDLER else [(lo, lo + SHARD_W)])
    zero_block = (0, WIN // LANES - 1)
    table = []
    for p in _padded_order_blocks():
        src = [(d, int(p) - WIN_START[d] // LANES) for d in range(NDEV)
               if any(lo < (p + 1) * LANES and hi > p * LANES for lo, hi in data[d])]
        assert len(src) <= 2 and all(0 <= b < WIN // LANES for _, b in src)
        src += [zero_block] * (2 - len(src))
        table.append([src[0][0], src[0][1], src[1][0], src[1][1]])
    table = np.asarray(table, np.int32).T.copy()

    per = 5
    assert P_IN // LANES % per == 0

    def body(t_ref, *refs):
        o_ref = refs[-1]
        for i in range(per):
            o_ref[:, i * LANES:(i + 1) * LANES] = refs[2 * i][...] + refs[2 * i + 1][...]

    src = lambda k, i: pl.BlockSpec((None, rows, LANES), lambda j, t: (t[k, per * j + i], 0, t[k + 1, per * j + i]))
    gs = pltpu.PrefetchScalarGridSpec(
        num_scalar_prefetch=1, grid=(P_IN // (per * LANES),), in_specs=[src(k, i) for i in range(per) for k in (0, 2)],
        out_specs=pl.BlockSpec((rows, per * LANES), lambda j, t: (0, j)))
    return pl.pallas_call(body, grid_spec=gs, out_shape=_sds((rows, P_IN), wins.dtype), name="w_in_from_windows",
                          compiler_params=_params(("parallel",)))(jnp.asarray(table), *([wins] * (2 * per)))


def _windows_from_layout(g):
    rows = g.shape[0]
    where = np.argsort(_padded_order_blocks())
    nb = WIN // LANES
    table = np.asarray([where[WIN_START[d] // LANES + b] for d in range(NDEV) for b in range(nb)], np.int32)

    per = 6
    assert nb % per == 0

    def body(t_ref, *refs):
        o_ref = refs[-1]
        for i in range(per):
            o_ref[:, i * LANES:(i + 1) * LANES] = refs[i][...]

    src = lambda i: pl.BlockSpec((rows, LANES), lambda d, b, t: (0, t[d * nb + per * b + i]))
    gs = pltpu.PrefetchScalarGridSpec(num_scalar_prefetch=1, grid=(NDEV, nb // per), in_specs=[src(i) for i in range(per)],
                                      out_specs=pl.BlockSpec((None, rows, per * LANES), lambda d, b, t: (d, 0, b)))
    return pl.pallas_call(body, grid_spec=gs, out_shape=_sds((NDEV, rows, WIN), g.dtype), name="w_in_grad_windows",
                          compiler_params=_params(("parallel", "parallel")))(jnp.asarray(table), *([g] * per))


def _to_slots_cols(g):
    rows, cols = g.shape
    return g.reshape(rows, NDEV, cols // NDEV).transpose(1, 0, 2)


def _from_slots_cols(a):
    n, rows, cols = a.shape
    return a.transpose(1, 0, 2).reshape(rows, n * cols)


WEIGHT_ORDER = ["norm1_w", "w_in", "conv_w", "a_log", "dt_bias", "gdn_norm_w", "ret_norm_w", "w_out", "norm2_w", "w_gate", "w_up",
                "w_down", "norm_f_w"]


def kernel(x, norm1_w, w_in, conv_w, a_log, dt_bias, gdn_norm_w, ret_norm_w, w_out, norm2_w, w_gate, w_up, w_down, norm_f_w, loss_target, m_norm1_w, m_w_in, m_conv_w, m_a_log, m_dt_bias, m_gdn_norm_w, m_ret_norm_w, m_w_out, m_norm2_w, m_w_gate, m_w_up, m_w_down, m_norm_f_w, v_norm1_w, v_w_in, v_conv_w, v_a_log, v_dt_bias, v_gdn_norm_w, v_ret_norm_w, v_w_out, v_norm2_w, v_w_gate, v_w_up, v_w_down, v_norm_f_w):
    ax, ay, ac = _coords()
    me = 4 * ax + 2 * ay + ac
    core = jnp.reshape(ac, (1,)).astype(jnp.int32)
    chip = jnp.reshape(2 * ax + ay, (1,)).astype(jnp.int32)
    w = dict(norm1_w=norm1_w, w_in=w_in[0], conv_w=conv_w[0], a_log=a_log, dt_bias=dt_bias, gdn_norm_w=gdn_norm_w,
             ret_norm_w=ret_norm_w, w_out=w_out[0], norm2_w=norm2_w, w_gate=w_gate[0], w_up=w_up[0], w_down=w_down[0],
             norm_f_w=norm_f_w)
    m = dict(norm1_w=m_norm1_w, w_in=m_w_in[0], conv_w=m_conv_w[0], a_log=m_a_log, dt_bias=m_dt_bias, gdn_norm_w=m_gdn_norm_w,
             ret_norm_w=m_ret_norm_w, w_out=m_w_out[0], norm2_w=m_norm2_w, w_gate=m_w_gate[0], w_up=m_w_up[0], w_down=m_w_down[0],
             norm_f_w=m_norm_f_w)
    v = dict(norm1_w=v_norm1_w, w_in=v_w_in[0], conv_w=v_conv_w[0], a_log=v_a_log, dt_bias=v_dt_bias, gdn_norm_w=v_gdn_norm_w,
             ret_norm_w=v_ret_norm_w, w_out=v_w_out[0], norm2_w=v_norm2_w, w_gate=v_w_gate[0], w_up=v_w_up[0], w_down=v_w_down[0],
             norm_f_w=v_norm_f_w)
    big_names = ["w_in", "w_out", "w_gate", "w_up", "w_down"]

    w_in_wins, conv_all = _run_side(_gather_side([_window_of_shard(w["w_in"].astype(BF16), me), w["conv_w"]]), "w_in_allgather")
    w_in_full = _layout_from_windows(w_in_wins)
    conv_full = _from_slots_cols(conv_all)
    dist = dict(core=core, shards=[w[k].astype(BF16) for k in ("w_out", "w_gate", "w_up", "w_down")])

    loss_tile, dx, big, small = _local_step(
        x[0], loss_target[0], w_in_full, None, None, None, None, norm1_w, conv_full, a_log, dt_bias,
        gdn_norm_w, ret_norm_w, norm2_w, norm_f_w.reshape(1, D), dist=dist)
    loss = lax.psum(loss_tile[0, 0], ("x", "y", "c"))

    g_w_in = _shard_of_window(_sum_slots(big["w_in"], 64, "w_in_grad_sum"), me)
    out = {"w_in": (g_w_in, *_adamw_rows(w["w_in"], g_w_in, m["w_in"], v["w_in"], 64, "adamw_w_in"))}
    for k in ("w_out", "w_gate", "w_up", "w_down"):
        part, recv = big[k]
        out[k] = _adamw_reduced(part, recv, chip, w[k], m[k], v[k], 128, "adamw_" + k)

    g_small = _unpack_small(_allreduce_small(_pack_small(**small), "small_grads_allreduce"))
    g_small["conv_w"] = lax.dynamic_slice_in_dim(g_small["conv_w"], me * (2 * D // NDEV), 2 * D // NDEV, axis=1)
    small_names = [k for k in WEIGHT_ORDER if k not in big_names]
    pad_conv = lambda a: jnp.pad(a, ((0, 0), (0, 2 * D - a.shape[1])))
    packs = []
    for src in (w, g_small, m, v):
        args = {k: (pad_conv(src[k]) if k == "conv_w" else src[k]) for k in small_names}
        packs.append(_pack_small(**args))
    d_pack, m_pack, v_pack = _adamw_plain(*packs[0:1], packs[1], packs[2], packs[3], name="adamw_small")
    cut_conv = lambda dct: {**dct, "conv_w": dct["conv_w"][:, :2 * D // NDEV]}
    d_small, m_small, v_small = (cut_conv(_unpack_small(p)) for p in (d_pack, m_pack, v_pack))

    def shaped(k, a):
        return a.reshape(w_shapes[k])

    w_shapes = dict(norm1_w=norm1_w.shape, w_in=w_in.shape, conv_w=conv_w.shape, a_log=a_log.shape, dt_bias=dt_bias.shape,
                    gdn_norm_w=gdn_norm_w.shape, ret_norm_w=ret_norm_w.shape, w_out=w_out.shape, norm2_w=norm2_w.shape,
                    w_gate=w_gate.shape, w_up=w_up.shape, w_down=w_down.shape, norm_f_w=norm_f_w.shape)
    grads, deltas, new_m, new_v = [], [], [], []
    for k in WEIGHT_ORDER:
        if k in big_names:
            g_, d_, m_, v_ = out[k]
        else:
            g_, d_, m_, v_ = g_small[k], d_small[k], m_small[k], v_small[k]
        grads.append(shaped(k, g_))
        deltas.append(shaped(k, d_))
        new_m.append(shaped(k, m_))
        new_v.append(shaped(k, v_))
    return (loss, dx[None], *grads, *deltas, *new_m, *new_v)
```

```python
import functools
import numpy as np
import jax
import jax.numpy as jnp
from jax import lax
from jax.experimental import pallas as pl
from jax.experimental.pallas import tpu as pltpu

F32, BF16 = jnp.float32, jnp.bfloat16
HI = lax.Precision.HIGHEST
MESH = pl.DeviceIdType.MESH
ANY = pl.BlockSpec(memory_space=pl.ANY)
VMEM_FULL = pl.BlockSpec(memory_space=pltpu.VMEM)

NDEV = 8
D = 2048
H = 8
DK = 128
DV = 256
C = 64
CPB = 4
EPS = 1e-6
ROPE_BASE = 10000.0
N_IN = 16400
O_MERGE, O_QKV, O_RET, O_BA, P_IN = 0, 8192, 12288, 16384, 16640
MERGE_W, RET_W = 4 * DV, 2 * DK + DV
LR, B1, B2, EPS_ADAM, WD, STEP = 0.001, 0.9, 0.999, 1e-08, 0.01, 10
VMEM_CAP = 60 * 1024 * 1024

NN = ((1,), (0,))
NT = ((1,), (1,))
TN = ((0,), (0,))


def _params(sem=None, est=None):
    kw = {}
    if sem is not None:
        kw["dimension_semantics"] = sem
    if est is not None:
        kw["vmem_limit_bytes"] = int(min(VMEM_CAP, max(32 * 1024 * 1024, est * 5 // 4 + (4 << 20))))
    return pltpu.CompilerParams(**kw)


def _sds(shape, dt=F32):
    return jax.ShapeDtypeStruct(tuple(shape), dt)


def _row_tile(rows, limit):
    return max(t for t in range(16, min(rows, limit) + 1, 16) if rows % t == 0)


def _bdot(a, b, dims):
    return lax.dot_general(a.astype(BF16), b.astype(BF16), (dims, ((), ())), preferred_element_type=F32)


def _hdot(a, b, dims):
    return lax.dot_general(a, b, (dims, ((), ())), precision=HI, preferred_element_type=F32)


_sigmoid = jax.nn.sigmoid


def _silu(x):
    return x * _sigmoid(x)


def _rms(x, w):
    return x * lax.rsqrt(jnp.mean(x * x, axis=-1, keepdims=True) + EPS) * w


class _Side:
    def __init__(self, ins, out_shapes, sems, start, finish):
        self.ins, self.out_shapes, self.sems, self.start, self.finish = list(ins), list(out_shapes), list(sems), start, finish


def _run_side(side, name):
    ni, no = len(side.ins), len(side.out_shapes)

    def body(*refs):
        ins, outs, sems = refs[:ni], refs[ni:ni + no], refs[ni + no:]
        side.start(ins, outs, *sems)
        side.finish(ins, outs, *sems)

    return pl.pallas_call(body, in_specs=[ANY] * ni, out_specs=[ANY] * no, out_shape=side.out_shapes,
                          scratch_shapes=[pltpu.SemaphoreType.DMA(s) for s in side.sems], name=name)(*side.ins)


def _matmul(a, b, *, ta=False, tb=False, tm, tn, tk, out_dtype=F32, res=None, side=None, name):
    m = a.shape[1] if ta else a.shape[0]
    k = a.shape[0] if ta else a.shape[1]
    n = b.shape[0] if tb else b.shape[1]
    assert k == (b.shape[1] if tb else b.shape[0])
    tm, tn, tk = min(tm, m), min(tn, n), min(tk, k)
    assert m % tm == 0 and n % tn == 0 and k % tk == 0, (name, m, n, k, tm, tn, tk)
    nk = k // tk
    dims = ((0 if ta else 1,), (1 if tb else 0,))
    has_res = res is not None
    n_in = 3 if has_res else 2
    n_side_in = len(side.ins) if side else 0
    n_side_out = len(side.out_shapes) if side else 0
    grid = (m // tm, n // tn, nk)

    def body(*refs):
        a_ref, b_ref = refs[0], refs[1]
        r_ref = refs[2] if has_res else None
        o_ref = refs[n_in + n_side_in]
        if side:
            side_ins = refs[n_in:n_in + n_side_in]
            side_outs = refs[n_in + n_side_in + 1:n_in + n_side_in + 1 + n_side_out]
            side_sems = refs[len(refs) - len(side.sems):]
            step = (pl.program_id(0) * grid[1] + pl.program_id(1)) * grid[2] + pl.program_id(2)

            @pl.when(step == 0)
            def _():
                side.start(side_ins, side_outs, *side_sems)

        def finish(acc):
            if has_res:
                acc = acc + r_ref[...].astype(F32)
            o_ref[...] = acc.astype(out_dtype)

        part = _bdot(a_ref[...], b_ref[...], dims)
        if nk == 1:
            finish(part)
        else:
            acc_ref = refs[n_in + n_side_in + 1 + n_side_out]
            kk = pl.program_id(2)

            @pl.when(kk == 0)
            def _():
                acc_ref[...] = part

            @pl.when(kk > 0)
            def _():
                acc_ref[...] += part

            @pl.when(kk == nk - 1)
            def _():
                finish(acc_ref[...])

        if side:
            @pl.when(step == grid[0] * grid[1] * grid[2] - 1)
            def _():
                side.finish(side_ins, side_outs, *side_sems)

    a_spec = pl.BlockSpec((tk, tm), lambda i, j, kk: (kk, i)) if ta else pl.BlockSpec((tm, tk), lambda i, j, kk: (i, kk))
    b_spec = pl.BlockSpec((tn, tk), lambda i, j, kk: (j, kk)) if tb else pl.BlockSpec((tk, tn), lambda i, j, kk: (kk, j))
    o_spec = pl.BlockSpec((tm, tn), lambda i, j, kk: (i, j))
    in_specs = [a_spec, b_spec] + ([o_spec] if has_res else []) + [ANY] * n_side_in
    est = 2 * (tm * tk * a.dtype.itemsize + tk * tn * b.dtype.itemsize + tm * tn * jnp.dtype(out_dtype).itemsize)
    est += 2 * tm * tn * 4 * (1 if has_res else 0) + (tm * tn * 4 if nk > 1 else 0) + 2 * tm * tn * 4
    args = (a, b) + ((res,) if has_res else ()) + (tuple(side.ins) if side else ())
    scratch = ([pltpu.VMEM((tm, tn), F32)] if nk > 1 else []) + ([pltpu.SemaphoreType.DMA(s) for s in side.sems] if side else [])
    sem = ("arbitrary",) * 3 if side else ("parallel", "parallel", "arbitrary")
    out = pl.pallas_call(
        body, grid=grid, in_specs=in_specs, out_specs=[o_spec] + [ANY] * n_side_out,
        out_shape=[_sds((m, n), out_dtype)] + (side.out_shapes if side else []),
        scratch_shapes=scratch, name=name, compiler_params=_params(sem, est))(*args)
    return (out[0], out[1:]) if side else out[0]


def _rms_fwd(x, w, ts, name):
    s = x.shape[0]

    def body(x_ref, w_ref, o_ref, ot_ref):
        y = _rms(x_ref[...], w_ref[...]).astype(BF16)
        o_ref[...] = y
        ot_ref[...] = y.T

    row = pl.BlockSpec((ts, D), lambda i: (i, 0))
    return pl.pallas_call(body, grid=(s // ts,), in_specs=[row, pl.BlockSpec((1, D), lambda i: (0, 0))],
                          out_specs=[row, pl.BlockSpec((D, ts), lambda i: (0, i))],
                          out_shape=[_sds((s, D), BF16), _sds((D, s), BF16)], name=name,
                          compiler_params=_params(("parallel",)))(x, w)


def _rms_bwd(x, w, du, dres, ts, name):
    s = x.shape[0]

    def body(x_ref, w_ref, du_ref, dres_ref, dx_ref, dw_ref):
        _, vjp = jax.vjp(_rms, x_ref[...], w_ref[...])
        dx, dw = vjp(du_ref[...].astype(F32))
        dx_ref[...] = dx + dres_ref[...]

        @pl.when(pl.program_id(0) == 0)
        def _():
            dw_ref[...] = jnp.zeros_like(dw_ref)

        dw_ref[...] += dw

    row = pl.BlockSpec((ts, D), lambda i: (i, 0))
    vec = pl.BlockSpec((1, D), lambda i: (0, 0))
    return pl.pallas_call(body, grid=(s // ts,), in_specs=[row, vec, row, row], out_specs=[row, vec],
                          out_shape=[_sds((s, D)), _sds((1, D))], name=name,
                          compiler_params=_params(("arbitrary",), 12 * ts * D * 4))(x, w, du, dres)


def _conv_taps(xx, w, base, ts):
    acc = xx[base:base + ts] * w[0:1, :]
    for j in range(1, 4):
        acc = acc + xx[base + j:base + j + ts] * w[j:j + 1, :]
    return acc


def _causal_conv(prev8, cur, w, first):
    xx = jnp.concatenate([jnp.where(first, 0.0, prev8), cur], axis=0)
    return _conv_taps(xx, w, 5, cur.shape[0])


def _qk_post(c, scale):
    s = _silu(c)
    return s * lax.rsqrt(jnp.sum(s * s, axis=-1, keepdims=True) + EPS) * scale


def _conv_specs(ts, cw, col0):
    pcol = O_QKV // cw + col0
    cur = pl.BlockSpec((ts, cw), lambda j, i: (i, pcol + j))
    prev = pl.BlockSpec((8, cw), lambda j, i: (jnp.maximum(i * (ts // 8) - 1, 0), pcol + j))
    wsp = pl.BlockSpec((4, cw), lambda j, i: (0, col0 + j))
    return cur, prev, wsp


def _gdn_qkv_fwd(proj, conv_w, ts):
    s = proj.shape[0]

    def qk_body(cur_ref, prev_ref, w_ref, o_ref, c_ref):
        c = _causal_conv(prev_ref[...], cur_ref[...], w_ref[...], pl.program_id(1) == 0)
        scale = jnp.where(pl.program_id(0) < H, DK ** -0.5, 1.0).astype(F32)
        c_ref[...] = c
        o_ref[...] = _qk_post(c, scale)

    tq = min(2 * ts, s)
    cur, prev, wsp = _conv_specs(tq, DK, 0)
    out = pl.BlockSpec((tq, DK), lambda j, i: (i, j))
    qk, c_qk = pl.pallas_call(qk_body, grid=(2 * H, s // tq), in_specs=[cur, prev, wsp], out_specs=[out, out],
                              out_shape=[_sds((s, 2 * H * DK))] * 2, name="gdn_qk_prep",
                              compiler_params=_params(("parallel", "parallel")))(proj, proj, conv_w)

    def v_body(cur_ref, prev_ref, w_ref, o_ref, c_ref):
        c = _causal_conv(prev_ref[...], cur_ref[...], w_ref[...], pl.program_id(1) == 0)
        c_ref[...] = c
        o_ref[...] = _silu(c)

    cw = 512
    cur, prev, wsp = _conv_specs(ts, cw, 2 * H * DK // cw)
    out = pl.BlockSpec((ts, cw), lambda j, i: (i, j))
    v, c_v = pl.pallas_call(v_body, grid=(H * DV // cw, s // ts), in_specs=[cur, prev, wsp], out_specs=[out, out],
                            out_shape=[_sds((s, H * DV))] * 2, name="gdn_v_prep",
                            compiler_params=_params(("parallel", "parallel")))(proj, proj, conv_w)
    return qk, v, c_qk, c_v


def _gdn_qkv_bwd(proj, conv_w, c_qk, c_v, dqk, dv, dproj, ts):
    s = proj.shape[0]
    nt = s // ts

    def qk_body(c_ref, d_ref, o_ref):
        scale = jnp.where(pl.program_id(0) < H, DK ** -0.5, 1.0).astype(F32)
        _, vjp = jax.vjp(lambda cc: _qk_post(cc, scale), c_ref[...])
        o_ref[...] = vjp(d_ref[...])[0]

    tq = min(2 * ts, s)
    blk = pl.BlockSpec((tq, DK), lambda j, i: (i, j))
    dc_qk = pl.pallas_call(qk_body, grid=(2 * H, s // tq), in_specs=[blk, blk], out_specs=blk, out_shape=_sds((s, 2 * H * DK)),
                           name="gdn_qk_prep_bwd", compiler_params=_params(("parallel", "parallel")))(c_qk, dqk)

    def v_body(c_ref, d_ref, o_ref):
        _, vjp = jax.vjp(_silu, c_ref[...])
        o_ref[...] = vjp(d_ref[...])[0]

    cw = 512
    blk = pl.BlockSpec((ts, cw), lambda j, i: (i, j))
    dc_v = pl.pallas_call(v_body, grid=(H * DV // cw, nt), in_specs=[blk, blk], out_specs=blk, out_shape=_sds((s, H * DV)),
                          name="gdn_v_prep_bwd", compiler_params=_params(("parallel", "parallel")))(c_v, dv)

    def conv_bwd(dc, dproj, col0, ncols, name):
        def body(x_ref, xprev_ref, w_ref, dc_ref, dcnext_ref, _, da_ref, dw_ref):
            i = pl.program_id(1)
            w = w_ref[...]
            dcur = dc_ref[...]
            dd = jnp.concatenate([dcur, jnp.where(i == nt - 1, 0.0, dcnext_ref[...])], axis=0)
            acc = dd[3:3 + ts] * w[0:1, :]
            for j in range(1, 4):
                acc = acc + dd[3 - j:3 - j + ts] * w[j:j + 1, :]
            da_ref[...] = acc.astype(BF16)
            xx = jnp.concatenate([jnp.where(i == 0, 0.0, xprev_ref[...]), x_ref[...]], axis=0)

            @pl.when(i == 0)
            def _():
                dw_ref[...] = jnp.zeros_like(dw_ref)

            for j in range(4):
                dw_ref[j:j + 1, :] += jnp.sum(dcur * xx[5 + j:5 + j + ts], axis=0, keepdims=True)

        cur, prev, wsp = _conv_specs(ts, cw, col0)
        dcur = pl.BlockSpec((ts, cw), lambda j, i: (i, j))
        dnext = pl.BlockSpec((8, cw), lambda j, i: (jnp.minimum((i + 1) * (ts // 8), s // 8 - 1), j))
        pcol = O_QKV // cw + col0
        return pl.pallas_call(body, grid=(ncols // cw, nt), in_specs=[cur, prev, wsp, dcur, dnext, ANY],
                              out_specs=[pl.BlockSpec((ts, cw), lambda j, i: (i, pcol + j)), pl.BlockSpec((4, cw), lambda j, i: (0, j))],
                              out_shape=[_sds(dproj.shape, BF16), _sds((4, ncols))], input_output_aliases={5: 0}, name=name,
                              compiler_params=_params(("parallel", "arbitrary")))(proj, proj, conv_w, dc, dc, dproj)

    dproj, dw_qk = conv_bwd(dc_qk, dproj, 0, 2 * H * DK, "conv_bwd_qk")
    dproj, dw_v = conv_bwd(dc_v, dproj, 2 * H * DK // cw, H * DV, "conv_bwd_v")
    return dproj, jnp.concatenate([dw_qk, dw_v], axis=1)


def _bg(b, a, alog, dtb):
    n = b.shape[0]
    g = -jnp.exp(alog) * jax.nn.softplus(a + dtb)
    row = lax.broadcasted_iota(jnp.int32, (n, n), 0)
    col = lax.broadcasted_iota(jnp.int32, (n, n), 1)
    shift = C.bit_length() - 1
    same = (row >> shift) == (col >> shift)
    return _sigmoid(b), _hdot((same & (row >= col)).astype(F32), g, NN), _hdot(same.astype(F32), g, NN)


def _bg_fwd(proj, alog, dtb, ts):
    s = proj.shape[0]

    def body(ba_ref, alog_ref, dtb_ref, beta_ref, gc_ref, gl_ref):
        beta_ref[...], gc_ref[...], gl_ref[...] = _bg(ba_ref[:, 0:H], ba_ref[:, H:2 * H], alog_ref[...], dtb_ref[...])

    small = pl.BlockSpec((1, H), lambda i: (0, 0))
    out = pl.BlockSpec((ts, H), lambda i: (i, 0))
    return pl.pallas_call(body, grid=(s // ts,), in_specs=[pl.BlockSpec((ts, 256), lambda i: (i, O_BA // 256)), small, small],
                          out_specs=[out] * 3, out_shape=[_sds((s, H))] * 3, name="gdn_bg_prep",
                          compiler_params=_params(("parallel",)))(proj, alog, dtb)


def _bg_bwd(proj, alog, dtb, dbeta_h, dgc_h, dgl_h, dproj, ts):
    s = proj.shape[0]

    def body(ba_ref, alog_ref, dtb_ref, dbeta_ref, dgc_ref, dgl_ref, _, dba_ref, dalog_ref, ddtb_ref):
        _, vjp = jax.vjp(_bg, ba_ref[:, 0:H], ba_ref[:, H:2 * H], alog_ref[...], dtb_ref[...])
        db, da, dalog, ddtb = vjp((jnp.sum(dbeta_ref[...], axis=0), jnp.sum(dgc_ref[...], axis=0), jnp.sum(dgl_ref[...], axis=0)))
        dba_ref[...] = jnp.zeros_like(dba_ref)
        dba_ref[:, 0:H] = db.astype(BF16)
        dba_ref[:, H:2 * H] = da.astype(BF16)

        @pl.when(pl.program_id(0) == 0)
        def _():
            dalog_ref[...] = jnp.zeros_like(dalog_ref)
            ddtb_ref[...] = jnp.zeros_like(ddtb_ref)

        dalog_ref[...] += dalog
        ddtb_ref[...] += ddtb

    small = pl.BlockSpec((1, H), lambda i: (0, 0))
    per_head = pl.BlockSpec((H, ts, H), lambda i: (0, i, 0))
    return pl.pallas_call(body, grid=(s // ts,),
                          in_specs=[pl.BlockSpec((ts, 256), lambda i: (i, O_BA // 256)), small, small, per_head, per_head, per_head, ANY],
                          out_specs=[pl.BlockSpec((ts, 256), lambda i: (i, O_BA // 256)), small, small],
                          out_shape=[_sds(dproj.shape, BF16), _sds((1, H)), _sds((1, H))], input_output_aliases={6: 0},
                          name="gdn_bg_prep_bwd", compiler_params=_params(("arbitrary",)))(proj, alog, dtb, dbeta_h, dgc_h, dgl_h, dproj)


BLK = 4 * C
NNB, NTB, TNB = ((2,), (1,)), ((2,), (2,)), ((1,), (1,))


def _bdot_b(a, b, dims):
    return lax.dot_general(a.astype(BF16), b.astype(BF16), (dims, ((0,), (0,))), preferred_element_type=F32)


@jax.custom_vjp
def _inv_unit_lower(a):
    n = a.shape[-1]
    row = lax.broadcasted_iota(jnp.int32, (n, n), 0)
    col = lax.broadcasted_iota(jnp.int32, (n, n), 1)
    x = jnp.where(row == col, 1.0, 0.0).astype(F32) - a
    p = _bdot_b(a, a, NNB)
    power = 2
    while True:
        x = x + _bdot_b(x, p, NNB)
        power *= 2
        if power >= C:
            return x
        p = _bdot_b(p, p, NNB)


def _inv_fwd(a):
    t = _inv_unit_lower(a)
    return t, t


def _inv_bwd(t, dt):
    return (-_bdot_b(_bdot_b(t, dt, TNB), t, NTB),)


_inv_unit_lower.defvjp(_inv_fwd, _inv_bwd)


def _gdn_prep(q, k, v, bfull, gcfull, glfull, hmask):
    nb, n = q.shape[0], q.shape[1]
    beta = jnp.sum(bfull * hmask, axis=-1, keepdims=True)
    gc = jnp.sum(gcfull * hmask, axis=-1, keepdims=True)
    gl = jnp.sum(glfull * hmask, axis=-1, keepdims=True)
    row = lax.broadcasted_iota(jnp.int32, (n, n), 0)
    col = lax.broadcasted_iota(jnp.int32, (n, n), 1)
    shift = C.bit_length() - 1
    same = (row >> shift) == (col >> shift)
    incl, strict = same & (row >= col), same & (row > col)
    g_i = gc * jnp.ones((1, 1, n), F32)
    decay = jnp.exp(jnp.where(incl, g_i - jnp.swapaxes(g_i, 1, 2), -jnp.inf))
    kb = k * beta
    a = jnp.where(strict, _bdot_b(kb, k, NTB) * decay, 0.0)
    tinv = _inv_unit_lower(a)
    u = _bdot_b(tinv, v * beta, NNB)
    w = _bdot_b(tinv, kb * jnp.exp(gc), NNB)
    attn = _bdot_b(q, k, NTB) * decay
    fold = ((lax.broadcasted_iota(jnp.int32, (n, C), 0) & (C - 1)) == lax.broadcasted_iota(jnp.int32, (n, C), 1)).astype(F32)
    attn_c = _bdot(attn.reshape(nb * n, n), fold, NN).reshape(nb, n, C)
    return u, w, attn_c, q * jnp.exp(gc), k * jnp.exp(gl - gc), jnp.exp(gl)


def _gdn_step(u, w, attn, qg, kd, egl, state):
    v_new = u - _bdot_b(w, state, NNB)
    o = _bdot_b(qg, state, NNB) + _bdot_b(attn, v_new, NNB)
    return o, state * egl + _bdot_b(kd, v_new, TNB)


def _heads(ref, rs, width):
    return jnp.stack([ref[rs, h * width:(h + 1) * width] for h in range(H)])


def _head_mask(h):
    return (lax.broadcasted_iota(jnp.int32, (1, H), 1) == h).astype(F32)


PREP_BLOCKS = 4


def _gdn_prep_specs(r):
    small = pl.BlockSpec((r, H), lambda h, c: (c, 0))
    return [pl.BlockSpec((r, DK), lambda h, c: (c, h)), pl.BlockSpec((r, DK), lambda h, c: (c, H + h)),
            pl.BlockSpec((r, DV), lambda h, c: (c, h)), small, small, small]


def _blocked(ref):
    x = ref[...]
    return x.reshape(PREP_BLOCKS, BLK, x.shape[-1])


def _gdn_inter_specs(r):
    col = pl.BlockSpec((r, DK), lambda h, c: (c, h))
    return [pl.BlockSpec((r, DV), lambda h, c: (c, h)), col, pl.BlockSpec((1, r, C), lambda h, c: (h, c, 0)), col, col,
            pl.BlockSpec((1, r // C, 8, 128), lambda h, c: (h, c, 0, 0))]


def _gdn_prep_fwd(qk, v, beta, gc, gl):
    s = qk.shape[0]
    r = PREP_BLOCKS * BLK

    def body(q_ref, k_ref, v_ref, b_ref, gc_ref, gl_ref, u_ref, w_ref, attn_ref, qg_ref, kd_ref, egl_ref):
        u, w, attn, qg, kd, egl = _gdn_prep(_blocked(q_ref), _blocked(k_ref), _blocked(v_ref), _blocked(b_ref), _blocked(gc_ref),
                                            _blocked(gl_ref), _head_mask(pl.program_id(0)))
        u_ref[...] = u.reshape(r, DV)
        w_ref[...] = w.reshape(r, DK).astype(BF16)
        attn_ref[0] = attn.reshape(r, C).astype(BF16)
        qg_ref[...] = qg.reshape(r, DK).astype(BF16)
        kd_ref[...] = kd.reshape(r, DK).astype(BF16)
        egl = egl.reshape(r, 1)
        for j in range(r // C):
            egl_ref[0, j] = egl[j * C:j * C + 1, :] * jnp.ones((8, 128), F32)

    out_shape = [_sds((s, H * DV)), _sds((s, H * DK), BF16), _sds((H, s, C), BF16), _sds((s, H * DK), BF16),
                 _sds((s, H * DK), BF16), _sds((H, s // C, 8, 128))]
    return pl.pallas_call(body, grid=(H, s // r), in_specs=_gdn_prep_specs(r), out_specs=_gdn_inter_specs(r), out_shape=out_shape,
                          name="gdn_prep_fwd", compiler_params=_params(("parallel", "parallel")))(qk, qk, v, beta, gc, gl)


def _gdn_prep_bwd(qk, v, beta, gc, gl, du, dw, dattn, dqg, dkd, degl):
    s = qk.shape[0]
    r = PREP_BLOCKS * BLK

    def body(q_ref, k_ref, v_ref, b_ref, gc_ref, gl_ref, du_ref, dw_ref, dattn_ref, dqg_ref, dkd_ref, degl_ref,
             dq_ref, dk_ref, dv_ref, db_ref, dgc_ref, dgl_ref):
        hmask = _head_mask(pl.program_id(0))
        _, vjp = jax.vjp(lambda q, k, v, b, gc, gl: _gdn_prep(q, k, v, b, gc, gl, hmask), _blocked(q_ref), _blocked(k_ref),
                         _blocked(v_ref), _blocked(b_ref), _blocked(gc_ref), _blocked(gl_ref))
        rowid = lax.broadcasted_iota(jnp.int32, (r, 1), 0)
        degl = jnp.zeros((r, 1), F32)
        for j in range(r // C):
            degl = jnp.where(rowid == j * C, degl_ref[0, j, 0:1, 0:1], degl)
        dq, dk, dv, db, dgc, dgl = vjp((_blocked(du_ref), _blocked(dw_ref), _blocked(dattn_ref.at[0]), _blocked(dqg_ref),
                                        _blocked(dkd_ref), degl.reshape(PREP_BLOCKS, BLK, 1)))
        dq_ref[...] = dq.reshape(r, DK)
        dk_ref[...] = dk.reshape(r, DK)
        dv_ref[...] = dv.reshape(r, DV)
        db_ref[0] = db.reshape(r, H)
        dgc_ref[0] = dgc.reshape(r, H)
        dgl_ref[0] = dgl.reshape(r, H)

    col = pl.BlockSpec((r, DK), lambda h, c: (c, h))
    piece = pl.BlockSpec((1, r, H), lambda h, c: (h, c, 0))
    dq, dk, dv, db, dgc, dgl = pl.pallas_call(
        body, grid=(H, s // r), in_specs=_gdn_prep_specs(r) + _gdn_inter_specs(r),
        out_specs=[col, col, pl.BlockSpec((r, DV), lambda h, c: (c, h)), piece, piece, piece],
        out_shape=[_sds((s, H * DK)), _sds((s, H * DK)), _sds((s, H * DV))] + [_sds((H, s, H))] * 3,
        name="gdn_prep_bwd", compiler_params=_params(("parallel", "parallel"), 24 << 20))(
            qk, qk, v, beta, gc, gl, du, dw, dattn, dqg, dkd, degl)
    return jnp.concatenate([dq, dk], axis=1), dv, db, dgc, dgl


def _gdn_scan_specs(r, order):
    wide = pl.BlockSpec((r, H * DK), lambda c: (order(c), 0))
    return [pl.BlockSpec((r, H * DV), lambda c: (order(c), 0)), wide, pl.BlockSpec((H, r, C), lambda c: (0, order(c), 0)), wide, wide,
            pl.BlockSpec((H, r // C, 8, 128), lambda c: (0, order(c), 0, 0))]


def _gdn_scan_fwd(u, w, attn, qg, kd, egl):
    s = u.shape[0]
    r = CPB * C
    nb = s // r

    def body(u_ref, w_ref, attn_ref, qg_ref, kd_ref, egl_ref, o_ref, st_ref, state_ref):
        @pl.when(pl.program_id(0) == 0)
        def _():
            state_ref[...] = jnp.zeros_like(state_ref)

        state = state_ref[...]
        for i in range(CPB):
            rs = slice(i * C, (i + 1) * C)
            st_ref[:, i] = state
            o, state = _gdn_step(_heads(u_ref, rs, DV), _heads(w_ref, rs, DK), attn_ref[:, rs, :], _heads(qg_ref, rs, DK),
                                 _heads(kd_ref, rs, DK), egl_ref[:, i, 0:1, 0:1], state)
            for h in range(H):
                o_ref[rs, h * DV:(h + 1) * DV] = o[h]
        state_ref[...] = state

    out_specs = [pl.BlockSpec((r, H * DV), lambda c: (c, 0)), pl.BlockSpec((H, CPB, DK, DV), lambda c: (0, c, 0, 0))]
    return pl.pallas_call(body, grid=(nb,), in_specs=_gdn_scan_specs(r, lambda c: c), out_specs=out_specs,
                          out_shape=[_sds((s, H * DV)), _sds((H, s // C, DK, DV))],
                          scratch_shapes=[pltpu.VMEM((H, DK, DV), F32)], name="gdn_scan_fwd",
                          compiler_params=_params(("arbitrary",), 24 << 20))(u, w, attn, qg, kd, egl)


def _gdn_scan_bwd(u, w, attn, qg, kd, egl, states, do):
    s = u.shape[0]
    r = CPB * C
    nb = s // r

    def body(u_ref, w_ref, attn_ref, qg_ref, kd_ref, egl_ref, st_ref, do_ref,
             du_ref, dw_ref, dattn_ref, dqg_ref, dkd_ref, degl_ref, dstate_ref):
        @pl.when(pl.program_id(0) == 0)
        def _():
            dstate_ref[...] = jnp.zeros_like(dstate_ref)

        dstate = dstate_ref[...]
        for i in reversed(range(CPB)):
            rs = slice(i * C, (i + 1) * C)
            _, vjp = jax.vjp(_gdn_step, _heads(u_ref, rs, DV), _heads(w_ref, rs, DK).astype(F32), attn_ref[:, rs, :].astype(F32),
                             _heads(qg_ref, rs, DK).astype(F32), _heads(kd_ref, rs, DK).astype(F32), egl_ref[:, i, 0:1, 0:1],
                             st_ref[:, i])
            du, dw, dattn, dqg, dkd, degl, dstate = vjp((_heads(do_ref, rs, DV), dstate))
            dattn_ref[:, rs, :] = dattn
            degl_ref[:, i] = degl * jnp.ones((1, 8, 128), F32)
            for h in range(H):
                du_ref[rs, h * DV:(h + 1) * DV] = du[h]
                dw_ref[rs, h * DK:(h + 1) * DK] = dw[h]
                dqg_ref[rs, h * DK:(h + 1) * DK] = dqg[h]
                dkd_ref[rs, h * DK:(h + 1) * DK] = dkd[h]
        dstate_ref[...] = dstate

    rev = lambda c: nb - 1 - c
    in_specs = _gdn_scan_specs(r, rev) + [pl.BlockSpec((H, CPB, DK, DV), lambda c: (0, rev(c), 0, 0)),
                                          pl.BlockSpec((r, H * DV), lambda c: (rev(c), 0))]
    return pl.pallas_call(
        body, grid=(nb,), in_specs=in_specs, out_specs=_gdn_scan_specs(r, rev),
        out_shape=[_sds((s, H * DV)), _sds((s, H * DK)), _sds((H, s, C)), _sds((s, H * DK)), _sds((s, H * DK)),
                   _sds((H, s // C, 8, 128))],
        scratch_shapes=[pltpu.VMEM((H, DK, DV), F32)], name="gdn_scan_bwd",
        compiler_params=_params(("arbitrary",), 40 << 20))(u, w, attn, qg, kd, egl, states, do)


def _rot(x, cs, sn):
    return x * cs + pltpu.roll(x, DK // 2, 1) * sn


def _rot_t(d, cs, sn):
    return d * cs - pltpu.roll(d, DK // 2, 1) * sn


def _ret_chunk(q, k, v, state, lg):
    n = q.shape[1]
    row = lax.broadcasted_iota(jnp.int32, (n, n), 0)
    col = lax.broadcasted_iota(jnp.int32, (n, n), 1)
    dist = (row - col).astype(F32)
    dmat = jnp.exp(jnp.where(dist >= 0, dist * lg, -jnp.inf))
    scores = _bdot_b(q, k, NTB) * dmat
    pos = lax.broadcasted_iota(jnp.int32, (n, 1), 0).astype(F32)
    xi = jnp.exp((pos + 1.0) * lg)
    zeta = jnp.exp((n - 1.0 - pos) * lg)
    o = _bdot_b(scores, v, NNB) + _bdot_b(q, state, NNB) * xi
    new_state = state * jnp.exp(n * lg) + _bdot_b(k * zeta, v, TNB)
    return o, new_state


def _ret_specs(r, order):
    return [pl.BlockSpec((r, H * RET_W), lambda c: (order(c), O_RET // (H * RET_W))), pl.BlockSpec((r, DK), lambda c: (order(c), 0)),
            pl.BlockSpec((r, DK), lambda c: (order(c), 0)), pl.BlockSpec((H, 1, 1), lambda c: (0, 0, 0))]


def _ret_qkv(x_ref, cs, sn):
    q = jnp.stack([_rot(x_ref[:, h * RET_W:h * RET_W + DK], cs, sn) for h in range(H)])
    k = jnp.stack([_rot(x_ref[:, h * RET_W + DK:h * RET_W + 2 * DK], cs, sn) for h in range(H)]) * DK ** -0.5
    v = jnp.stack([x_ref[:, h * RET_W + 2 * DK:(h + 1) * RET_W] for h in range(H)])
    return q, k, v


RET_C = 256


def _ret_scan_fwd(proj, cs, sn, lgtab):
    s = proj.shape[0]
    r = min(RET_C, s)
    nb = s // r

    def body(x_ref, cs_ref, sn_ref, lg_ref, o_ref, st_ref, state_ref):
        @pl.when(pl.program_id(0) == 0)
        def _():
            state_ref[...] = jnp.zeros_like(state_ref)

        state = state_ref[...]
        st_ref[:, 0] = state
        q, k, v = _ret_qkv(x_ref, cs_ref[...], sn_ref[...])
        o, state_ref[...] = _ret_chunk(q, k, v, state, lg_ref[...])
        for h in range(H):
            o_ref[:, h * DV:(h + 1) * DV] = o[h]

    out_specs = [pl.BlockSpec((r, H * DV), lambda c: (c, 0)), pl.BlockSpec((H, 1, DK, DV), lambda c: (0, c, 0, 0))]
    return pl.pallas_call(body, grid=(nb,), in_specs=_ret_specs(r, lambda c: c), out_specs=out_specs,
                          out_shape=[_sds((s, H * DV)), _sds((H, nb, DK, DV))],
                          scratch_shapes=[pltpu.VMEM((H, DK, DV), F32)], name="ret_scan_fwd",
                          compiler_params=_params(("arbitrary",), 32 << 20))(proj, cs, sn, lgtab.reshape(H, 1, 1))


def _ret_scan_bwd(proj, cs, sn, lgtab, states, do, dproj):
    s = proj.shape[0]
    r = min(RET_C, s)
    nb = s // r

    def body(x_ref, cs_ref, sn_ref, lg_ref, st_ref, do_ref, _, d_ref, dstate_ref):
        @pl.when(pl.program_id(0) == 0)
        def _():
            dstate_ref[...] = jnp.zeros_like(dstate_ref)

        cs_, sn_ = cs_ref[...], sn_ref[...]
        lg = lg_ref[...]
        q, k, v = _ret_qkv(x_ref, cs_, sn_)
        _, vjp = jax.vjp(lambda q, k, v, st: _ret_chunk(q, k, v, st, lg), q, k, v, st_ref[:, 0])
        dq, dk, dv, dstate_ref[...] = vjp((_heads(do_ref, slice(None), DV), dstate_ref[...]))
        for h in range(H):
            d_ref[:, h * RET_W:h * RET_W + DK] = _rot_t(dq[h], cs_, sn_).astype(BF16)
            d_ref[:, h * RET_W + DK:h * RET_W + 2 * DK] = _rot_t(dk[h] * DK ** -0.5, cs_, sn_).astype(BF16)
            d_ref[:, h * RET_W + 2 * DK:(h + 1) * RET_W] = dv[h].astype(BF16)

    rev = lambda c: nb - 1 - c
    in_specs = _ret_specs(r, rev) + [pl.BlockSpec((H, 1, DK, DV), lambda c: (0, rev(c), 0, 0)),
                                     pl.BlockSpec((r, H * DV), lambda c: (rev(c), 0)), ANY]
    return pl.pallas_call(
        body, grid=(nb,), in_specs=in_specs, out_specs=pl.BlockSpec((r, H * RET_W), lambda c: (rev(c), O_RET // (H * RET_W))),
        out_shape=_sds(dproj.shape, BF16), input_output_aliases={6: 0},
        scratch_shapes=[pltpu.VMEM((H, DK, DV), F32)], name="ret_scan_bwd",
        compiler_params=_params(("arbitrary",), 48 << 20))(proj, cs, sn, lgtab.reshape(H, 1, 1), states, do, dproj)


def _merge(oa, z, ob, rg, ga, gb, wa, wb):
    ya = oa * lax.rsqrt(jnp.mean(oa * oa, axis=-1, keepdims=True) + EPS) * wa * _silu(z)
    mu = jnp.mean(ob, axis=-1, keepdims=True)
    var = jnp.mean(jnp.square(ob - mu), axis=-1, keepdims=True)
    yb = (ob - mu) * lax.rsqrt(var + EPS) * wb * _silu(rg)
    return _sigmoid(ga) * ya + _sigmoid(gb) * yb


def _merge_specs(ts):
    own = pl.BlockSpec((ts, DV), lambda h, i: (i, h))
    grp = lambda k: pl.BlockSpec((ts, DV), lambda h, i: (i, O_MERGE // DV + 4 * h + k))
    return [own, grp(0), own, grp(1), grp(2), grp(3),
            pl.BlockSpec((1, DV), lambda h, i: (0, 0)), pl.BlockSpec((1, DV), lambda h, i: (0, h))]


def _merge_fwd(oa, ob, proj, wa, wb, ts):
    s = oa.shape[0]

    def body(oa_ref, z_ref, ob_ref, rg_ref, ga_ref, gb_ref, wa_ref, wb_ref, o_ref, ot_ref):
        y = _merge(oa_ref[...], z_ref[...], ob_ref[...], rg_ref[...], ga_ref[...], gb_ref[...],
                   wa_ref[...], wb_ref[...]).astype(BF16)
        o_ref[...] = y
        ot_ref[...] = y.T

    return pl.pallas_call(body, grid=(H, s // ts), in_specs=_merge_specs(ts),
                          out_specs=[pl.BlockSpec((ts, DV), lambda h, i: (i, h)), pl.BlockSpec((DV, ts), lambda h, i: (h, i))],
                          out_shape=[_sds((s, H * DV), BF16), _sds((H * DV, s), BF16)], name="merge_fwd",
                          compiler_params=_params(("parallel", "parallel")))(oa, proj, ob, proj, proj, proj, wa, wb)


def _merge_bwd(oa, ob, proj, wa, wb, dmixed, ts):
    s = oa.shape[0]

    def body(oa_ref, z_ref, ob_ref, rg_ref, ga_ref, gb_ref, wa_ref, wb_ref, dm_ref,
             doa_ref, dob_ref, dgrp_ref, dwa_ref, dwb_ref):
        _, vjp = jax.vjp(_merge, oa_ref[...], z_ref[...], ob_ref[...], rg_ref[...], ga_ref[...], gb_ref[...],
                         wa_ref[...], wb_ref[...])
        doa, dz, dob, drg, dga, dgb, dwa, dwb = vjp(dm_ref[...].astype(F32))
        doa_ref[...] = doa
        dob_ref[...] = dob
        for k, d in enumerate((dz, drg, dga, dgb)):
            dgrp_ref[:, k * DV:(k + 1) * DV] = d.astype(BF16)
        first_tile = pl.program_id(1) == 0

        @pl.when(first_tile & (pl.program_id(0) == 0))
        def _():
            dwa_ref[...] = jnp.zeros_like(dwa_ref)

        @pl.when(first_tile)
        def _():
            dwb_ref[...] = jnp.zeros_like(dwb_ref)

        dwa_ref[...] += dwa
        dwb_ref[...] += dwb

    blk = pl.BlockSpec((ts, DV), lambda h, i: (i, h))
    out_specs = [blk, blk, pl.BlockSpec((ts, MERGE_W), lambda h, i: (i, O_MERGE // MERGE_W + h)),
                 pl.BlockSpec((1, DV), lambda h, i: (0, 0)), pl.BlockSpec((1, DV), lambda h, i: (0, h))]
    out_shape = [_sds((s, H * DV)), _sds((s, H * DV)), _sds((s, P_IN), BF16), _sds((1, DV)), _sds((1, H * DV))]
    return pl.pallas_call(body, grid=(H, s // ts), in_specs=_merge_specs(ts) + [blk], out_specs=out_specs, out_shape=out_shape,
                          name="merge_bwd", compiler_params=_params(("arbitrary", "arbitrary"), 40 * ts * DV * 4))(
                              oa, proj, ob, proj, proj, proj, wa, wb, dmixed)


def _act(hg, hu):
    return _silu(hg) * hu


def _ffn_gate_up(hn, w_gate, w_up, tm, tn):
    s, f = hn.shape[0], w_gate.shape[1]
    tm, tn = min(tm, s), min(tn, f)
    assert s % tm == 0 and f % tn == 0 and tm % 256 == 0
    sub = tm // 2

    def body(a_ref, wg_ref, wu_ref, pg_ref, pu_ref, act_ref, actt_ref):
        for r0 in range(0, tm, sub):
            rs = slice(r0, r0 + sub)
            hg = _bdot(a_ref[rs, :], wg_ref[...], NN)
            hu = _bdot(a_ref[rs, :], wu_ref[...], NN)
            y, vjp = jax.vjp(_act, hg, hu)
            pg, pu = vjp(jnp.ones_like(y))
            y = y.astype(BF16)
            pg_ref[rs, :] = pg.astype(BF16)
            pu_ref[rs, :] = pu.astype(BF16)
            act_ref[rs, :] = y
            actt_ref[:, rs] = y.T

    wsp = pl.BlockSpec((D, tn), lambda i, j: (0, j))
    blk = pl.BlockSpec((tm, tn), lambda i, j: (i, j))
    est = 2 * (tm * D * 2 + 2 * D * tn * 2 + 4 * tm * tn * 2) + 4 * sub * tn * 4
    return pl.pallas_call(body, grid=(s // tm, f // tn), in_specs=[pl.BlockSpec((tm, D), lambda i, j: (i, 0)), wsp, wsp],
                          out_specs=[blk, blk, blk, pl.BlockSpec((tn, tm), lambda i, j: (j, i))],
                          out_shape=[_sds((s, f), BF16)] * 3 + [_sds((f, s), BF16)], name="ffn_gate_up",
                          compiler_params=_params(("parallel", "parallel"), est))(hn, w_gate, w_up)


def _ffn_down_dx(dh2, w_down, pg, pu, tm, tn):
    s, f = pg.shape
    tm, tn = min(tm, s), min(tn, f)
    assert s % tm == 0 and f % tn == 0 and tm % 256 == 0
    sub = tm // 2

    def body(d_ref, w_ref, pg_ref, pu_ref, dhg_ref, dhu_ref):
        for r0 in range(0, tm, sub):
            rs = slice(r0, r0 + sub)
            dact = _bdot(d_ref[rs, :], w_ref[...], NT)
            dhg_ref[rs, :] = (dact * pg_ref[rs, :].astype(F32)).astype(BF16)
            dhu_ref[rs, :] = (dact * pu_ref[rs, :].astype(F32)).astype(BF16)

    blk = pl.BlockSpec((tm, tn), lambda i, j: (i, j))
    est = 2 * (tm * D * 4 + tn * D * 2 + 4 * tm * tn * 2) + 6 * sub * tn * 4
    return pl.pallas_call(body, grid=(s // tm, f // tn),
                          in_specs=[pl.BlockSpec((tm, D), lambda i, j: (i, 0)), pl.BlockSpec((tn, D), lambda i, j: (j, 0)), blk, blk],
                          out_specs=[blk, blk], out_shape=[_sds((s, f), BF16)] * 2, name="ffn_down_dx",
                          compiler_params=_params(("parallel", "parallel"), est))(dh2, w_down, pg, pu)


def _loss_rows(h2, wf, tgt):
    err = _rms(h2, wf) - tgt
    return 0.5 * jnp.sum(jnp.mean(err * err, axis=-1, keepdims=True), keepdims=True)


def _loss_fwd_bwd(h2, wf, tgt, ts):
    s = h2.shape[0]

    def body(h_ref, w_ref, t_ref, loss_ref, dh_ref, dhb_ref, dw_ref):
        loss, vjp = jax.vjp(_loss_rows, h_ref[...], w_ref[...], t_ref[...])
        dh, dw, _ = vjp(jnp.ones((1, 1), F32))
        dh_ref[...] = dh
        dhb_ref[...] = dh.astype(BF16)

        @pl.when(pl.program_id(0) == 0)
        def _():
            loss_ref[...] = jnp.zeros_like(loss_ref)
            dw_ref[...] = jnp.zeros_like(dw_ref)

        loss_ref[...] += loss
        dw_ref[...] += dw

    row = pl.BlockSpec((ts, D), lambda i: (i, 0))
    vec = pl.BlockSpec((1, D), lambda i: (0, 0))
    tile = pl.BlockSpec((8, 128), lambda i: (0, 0))
    return pl.pallas_call(body, grid=(s // ts,), in_specs=[row, vec, row], out_specs=[tile, row, row, vec],
                          out_shape=[_sds((8, 128)), _sds((s, D)), _sds((s, D), BF16), _sds((1, D))], name="final_norm_loss",
                          compiler_params=_params(("arbitrary",), 12 * ts * D * 4))(h2, wf, tgt)


def _rope_tables(s):
    inv = ROPE_BASE ** (-jnp.arange(0, DK, 2, dtype=F32) / DK)
    ang = jnp.arange(s, dtype=F32)[:, None] * inv[None, :]
    cos, sin = jnp.cos(ang), jnp.sin(ang)
    return jnp.concatenate([cos, cos], axis=1), jnp.concatenate([-sin, sin], axis=1)


def _local_step(x, tgt, w_in, w_out, w_gate, w_up, w_down, norm1_w, conv_w, a_log, dt_bias, gdn_norm_w, ret_norm_w, norm2_w, norm_f_w,
                dist=None):
    s = x.shape[0]
    ts = min(512, s)
    cs, sn = _rope_tables(s)
    lgtab = jnp.log1p(-jnp.exp2(-5.0 - jnp.arange(H, dtype=F32))).reshape(1, H)

    u, u_t = _rms_fwd(x, norm1_w, ts, "norm1_fwd")
    if dist is None:
        proj = _matmul(u, w_in, tm=1024, tn=1280, tk=D, name="in_proj")
    else:
        proj, gathered = _matmul(u, w_in, tm=1024, tn=1280, tk=D, side=_gather_side(dist["shards"]), name="in_proj")
        w_out, w_gate, w_up, w_down = (gathered[0].reshape(D, D), _from_slots_cols(gathered[1]), _from_slots_cols(gathered[2]),
                                       gathered[3].reshape(-1, D))
    qk, va, c_qk, c_v = _gdn_qkv_fwd(proj, conv_w, ts)
    beta, gc, gl = _bg_fwd(proj, a_log, dt_bias, ts)
    inter = _gdn_prep_fwd(qk, va, beta, gc, gl)
    oa, st_a = _gdn_scan_fwd(*inter)
    ob, st_b = _ret_scan_fwd(proj, cs, sn, lgtab)
    mixed, mixed_t = _merge_fwd(oa, ob, proj, gdn_norm_w, ret_norm_w, min(2 * ts, s))
    h1 = _matmul(mixed, w_out, tm=1024, tn=1024, tk=D, res=x, name="out_proj")
    hn, hn_t = _rms_fwd(h1, norm2_w, ts, "norm2_fwd")
    act_dg, act_du, act, act_t = _ffn_gate_up(hn, w_gate, w_up, 1024, 512)
    h2 = _matmul(act, w_down, tm=512, tn=1024, tk=5632, res=h1, name="ffn_down")
    loss, dh2, dh2_b, d_norm_f = _loss_fwd_bwd(h2, norm_f_w, tgt, ts)

    dhg, dhu = _ffn_down_dx(dh2_b, w_down, act_dg, act_du, 1024, 512)
    g_down = _matmul(act_t, dh2_b, tm=512, tn=512, tk=8192, out_dtype=BF16, name="ffn_down_dw")
    g_gate = _matmul(hn_t, dhg, tm=512, tn=512, tk=8192, out_dtype=BF16, name="ffn_gate_dw")
    g_up = _matmul(hn_t, dhu, tm=512, tn=512, tk=8192, out_dtype=BF16, name="ffn_up_dw")
    dhn = _matmul(dhg, w_gate, tb=True, tm=512, tn=1024, tk=5632, name="ffn_gate_dx")
    dhn = _matmul(dhu, w_up, tb=True, tm=512, tn=1024, tk=5632, res=dhn, name="ffn_up_dx")
    dh1, d_norm2 = _rms_bwd(h1, norm2_w, dhn, dh2, ts, "norm2_bwd")

    g_out = _matmul(mixed_t, dh1, tm=1024, tn=1024, tk=D, out_dtype=BF16, name="out_proj_dw")
    early = ["w_out", "w_gate", "w_up", "w_down"]
    if dist is None:
        dmixed = _matmul(dh1, w_out, tb=True, tm=1024, tn=1024, tk=D, out_dtype=BF16, name="out_proj_dx")
    else:
        slots = dict(w_out=g_out.reshape(NDEV, D // NDEV, D), w_gate=_to_slots_cols(g_gate), w_up=_to_slots_cols(g_up),
                     w_down=g_down.reshape(NDEV, -1, D))
        dmixed, from_sibling = _matmul(dh1, w_out, tb=True, tm=1024, tn=1024, tk=D, out_dtype=BF16,
                                       side=_sibling_side([slots[k] for k in early]), name="out_proj_dx")
        parts = [_add_sibling(slots[k], r, dist["core"], 128, "grads_add_" + k) for k, r in zip(early, from_sibling)]
    doa, dob, dproj, d_gdn_norm, d_ret_norm = _merge_bwd(oa, ob, proj, gdn_norm_w, ret_norm_w, dmixed, ts)

    dproj = _ret_scan_bwd(proj, cs, sn, lgtab, st_b, dob, dproj)
    d_inter = _gdn_scan_bwd(*inter, st_a, doa)
    dqk, dva, dbeta_h, dgc_h, dgl_h = _gdn_prep_bwd(qk, va, beta, gc, gl, *d_inter)
    dproj, d_conv = _gdn_qkv_bwd(proj, conv_w, c_qk, c_v, dqk, dva, dproj, ts)
    dproj, d_a_log, d_dt_bias = _bg_bwd(proj, a_log, dt_bias, dbeta_h, dgc_h, dgl_h, dproj, ts)

    if dist is None:
        g_in = _matmul(u_t, dproj, tm=1024, tn=1280, tk=D, out_dtype=BF16, name="in_proj_dw")
        du = _matmul(dproj, w_in, tb=True, tm=1024, tn=1024, tk=1664, name="in_proj_dx")
        big = dict(w_in=g_in, w_out=g_out, w_gate=g_gate, w_up=g_up, w_down=g_down)
    else:
        g_in, from_chips = _matmul(u_t, dproj, tm=1024, tn=1280, tk=D, out_dtype=BF16, side=_chips_side(parts), name="in_proj_dw")
        du, (from_all,) = _matmul(dproj, w_in, tb=True, tm=1024, tn=1024, tk=1664,
                                  side=_all_to_all_side(_windows_from_layout(g_in)), name="in_proj_dx")
        big = dict(w_in=from_all, **{k: (p, r) for k, p, r in zip(early, parts, from_chips)})
    dx, d_norm1 = _rms_bwd(x, norm1_w, du, dh1, ts, "norm1_bwd")

    small = dict(norm1_w=d_norm1, conv_w=d_conv, a_log=d_a_log, dt_bias=d_dt_bias, gdn_norm_w=d_gdn_norm,
                 ret_norm_w=d_ret_norm, norm2_w=d_norm2, norm_f_w=d_norm_f)
    return loss, dx, big, small


def _coords():
    return lax.axis_index("x"), lax.axis_index("y"), lax.axis_index("c")


def _gather_side(shards):
    n = len(shards)

    def plan(ins, outs, send_sems, recv_sems, local_sems):
        x, y, c = _coords()
        me, sibling = (x, y, c), (x, y, 1 - c)
        chips = [(1 - x, y), (x, 1 - y), (1 - x, 1 - y)]

        def copy(a, k, block, to, src=None):
            px, py, pc = block
            dst = outs[a].at[4 * px + 2 * py + pc]
            return pltpu.make_async_remote_copy(src_ref=dst if src is None else src, dst_ref=dst, send_sem=send_sems.at[a, k],
                                                recv_sem=recv_sems.at[a, k], device_id=to, device_id_type=MESH)

        mine = [pltpu.make_async_copy(ins[a], outs[a].at[4 * x + 2 * y + c], local_sems.at[a]) for a in range(n)]
        first = []
        for a in range(n):
            first.append(copy(a, 0, me, sibling, src=ins[a]))
            first += [copy(a, 1 + j, me, (*chip, c), src=ins[a]) for j, chip in enumerate(chips)]
        return c, me, sibling, chips, copy, mine, first

    def start(ins, outs, *sems):
        *_, mine, first = plan(ins, outs, *sems)
        for cp in mine + first:
            cp.start()

    def finish(ins, outs, *sems):
        c, me, sibling, chips, copy, mine, first = plan(ins, outs, *sems)
        passed = []
        for j, chip in enumerate(chips):
            for a in range(n):
                copy(a, 1 + j, (*chip, c), me).wait_recv()
                fwd = copy(a, 4 + j, (*chip, c), sibling)
                fwd.start()
                passed.append(fwd)
        for a in range(n):
            copy(a, 0, sibling, me).wait_recv()
            for j, chip in enumerate(chips):
                copy(a, 4 + j, (*chip, 1 - c), me).wait_recv()
        for cp in first + passed:
            cp.wait_send()
        for cp in mine:
            cp.wait()

    return _Side(shards, [_sds((NDEV,) + a.shape, a.dtype) for a in shards], [(n, 7), (n, 7), (n,)], start, finish)


def _exchange_side(ins, n_out, copies_of):
    def start(in_refs, out_refs, *sems):
        for cp in copies_of(in_refs, out_refs, *sems):
            cp.start()

    def finish(in_refs, out_refs, *sems):
        for cp in copies_of(in_refs, out_refs, *sems):
            cp.wait()

    n = len(ins)
    return _Side(ins, [_sds((n_out,) + a.shape[1:], a.dtype) for a in ins], [(n, n_out), (n, n_out)], start, finish)


def _sibling_side(slots):
    def copies_of(ins, outs, send_sems, recv_sems):
        x, y, c = _coords()
        return [pltpu.make_async_remote_copy(
            src_ref=ins[a].at[2 * j + (1 - c)], dst_ref=outs[a].at[j], send_sem=send_sems.at[a, j], recv_sem=recv_sems.at[a, j],
            device_id=(x, y, 1 - c), device_id_type=MESH) for a in range(len(slots)) for j in range(4)]

    return _exchange_side(slots, 4, copies_of)


def _chips_side(parts):
    def copies_of(ins, outs, send_sems, recv_sems):
        x, y, c = _coords()
        chips = [(1 - x, y), (x, 1 - y), (1 - x, 1 - y)]
        return [pltpu.make_async_remote_copy(
            src_ref=ins[a].at[2 * px + py], dst_ref=outs[a].at[k], send_sem=send_sems.at[a, k], recv_sem=recv_sems.at[a, k],
            device_id=(px, py, c), device_id_type=MESH) for a in range(len(parts)) for k, (px, py) in enumerate(chips)]

    return _exchange_side(parts, 3, copies_of)


def _all_to_all_side(slots):
    def plan(ins, outs, send_sems, recv_sems, local_sems):
        x, y, c = _coords()
        mine = 4 * x + 2 * y + c
        own = pltpu.make_async_copy(ins[0].at[mine], outs[0].at[mine], local_sems.at[0])
        remote = []
        for r in range(1, NDEV):
            peer = (x ^ (r >> 2), y ^ ((r >> 1) & 1), c ^ (r & 1))
            remote.append(pltpu.make_async_remote_copy(
                src_ref=ins[0].at[mine ^ r], dst_ref=outs[0].at[mine], send_sem=send_sems.at[r - 1], recv_sem=recv_sems.at[r - 1],
                device_id=peer, device_id_type=MESH))
        return own, remote

    def start(ins, outs, *sems):
        own, remote = plan(ins, outs, *sems)
        for cp in [own] + remote:
            cp.start()

    def finish(ins, outs, *sems):
        own, remote = plan(ins, outs, *sems)
        for cp in remote:
            cp.wait()
        own.wait()

    return _Side([slots], [_sds(slots.shape, slots.dtype)], [(NDEV - 1,), (NDEV - 1,), (1,)], start, finish)


def _allreduce_small(pack, name):
    rows, cols = pack.shape

    def body(in_ref, out_ref, buf_ref, send_sems, recv_sems):
        x, y, c = _coords()
        mine = 4 * x + 2 * y + c
        buf_ref[mine] = in_ref[...]
        copies = []
        for r in range(1, NDEV):
            peer = (x ^ (r >> 2), y ^ ((r >> 1) & 1), c ^ (r & 1))
            copies.append(pltpu.make_async_remote_copy(
                src_ref=in_ref, dst_ref=buf_ref.at[mine], send_sem=send_sems.at[r - 1], recv_sem=recv_sems.at[r - 1],
                device_id=peer, device_id_type=MESH))
        for cp in copies:
            cp.start()
        for r in range(1, NDEV):
            pltpu.make_async_remote_copy(
                src_ref=in_ref, dst_ref=buf_ref.at[mine ^ r], send_sem=send_sems.at[r - 1], recv_sem=recv_sems.at[r - 1],
                device_id=(x, y, c), device_id_type=MESH).wait_recv()
        for cp in copies:
            cp.wait_send()
        acc = buf_ref[0]
        for d in range(1, NDEV):
            acc = acc + buf_ref[d]
        out_ref[...] = acc

    return pl.pallas_call(
        body, in_specs=[VMEM_FULL], out_specs=VMEM_FULL, out_shape=_sds((rows, cols)),
        scratch_shapes=[pltpu.VMEM((NDEV, rows, cols), F32), pltpu.SemaphoreType.DMA((NDEV - 1,)), pltpu.SemaphoreType.DMA((NDEV - 1,))],
        name=name)(pack)


def _add_sibling(slots, recv, core, tr, name):
    _, rows, cols = slots.shape
    tr = _row_tile(rows, tr)

    def body(c_ref, a_ref, b_ref, o_ref):
        o_ref[...] = (a_ref[...].astype(F32) + b_ref[...].astype(F32)).astype(BF16)

    gs = pltpu.PrefetchScalarGridSpec(
        num_scalar_prefetch=1, grid=(4, rows // tr),
        in_specs=[pl.BlockSpec((None, tr, cols), lambda j, i, cr: (2 * j + cr[0], i, 0)),
                  pl.BlockSpec((None, tr, cols), lambda j, i, cr: (j, i, 0))],
        out_specs=pl.BlockSpec((None, tr, cols), lambda j, i, cr: (j, i, 0)))
    return pl.pallas_call(body, grid_spec=gs, out_shape=_sds((4, rows, cols), BF16), name=name,
                          compiler_params=_params(("parallel", "parallel"), 6 * tr * cols * 4))(core, slots, recv)


def _adam_math(w, g, m, v):
    m2 = B1 * m + (1.0 - B1) * g
    v2 = B2 * v + (1.0 - B2) * jnp.square(g)
    m_hat = m2 / (1.0 - B1 ** STEP)
    v_hat = v2 / (1.0 - B2 ** STEP)
    return -LR * (m_hat / (jnp.sqrt(v_hat) + EPS_ADAM) + WD * w), m2, v2


def _adamw_reduced(part, recv, chip, w, m, v, tr, name):
    rows, cols = w.shape
    tr = _row_tile(rows, tr)

    def body(j_ref, p_ref, r0_ref, r1_ref, r2_ref, w_ref, m_ref, v_ref, g_ref, d_ref, nm_ref, nv_ref):
        g = p_ref[...].astype(F32) + r0_ref[...].astype(F32) + r1_ref[...].astype(F32) + r2_ref[...].astype(F32)
        d, m2, v2 = _adam_math(w_ref[...], g, m_ref[...], v_ref[...])
        g_ref[...] = g
        d_ref[...] = d
        nm_ref[...] = m2
        nv_ref[...] = v2

    flat = pl.BlockSpec((tr, cols), lambda i, jr: (i, 0))
    gs = pltpu.PrefetchScalarGridSpec(
        num_scalar_prefetch=1, grid=(rows // tr,),
        in_specs=[pl.BlockSpec((None, tr, cols), lambda i, jr: (jr[0], i, 0))]
        + [pl.BlockSpec((None, tr, cols), functools.partial(lambda i, jr, k: (k, i, 0), k=k)) for k in range(3)] + [flat] * 3,
        out_specs=[flat] * 4)
    return pl.pallas_call(body, grid_spec=gs, out_shape=[_sds((rows, cols))] * 4, name=name,
                          compiler_params=_params(("parallel",), 22 * tr * cols * 4))(chip, part, recv, recv, recv, w, m, v)


def _sum_slots(recv, tr, name):
    _, rows, cols = recv.shape
    tr = _row_tile(rows, tr)

    def body(*refs):
        acc = refs[0][...].astype(F32)
        for p_ref in refs[1:NDEV]:
            acc = acc + p_ref[...].astype(F32)
        refs[NDEV][...] = acc

    slot = [pl.BlockSpec((None, tr, cols), functools.partial(lambda i, k: (k, i, 0), k=k)) for k in range(NDEV)]
    return pl.pallas_call(body, grid=(rows // tr,), in_specs=slot, out_specs=pl.BlockSpec((tr, cols), lambda i: (i, 0)),
                          out_shape=_sds((rows, cols)), name=name, compiler_params=_params(("parallel",)))(*([recv] * NDEV))


def _adamw_rows(w, g, m, v, tr, name):
    rows, cols = w.shape
    tr = _row_tile(rows, tr)

    def body(w_ref, g_ref, m_ref, v_ref, d_ref, nm_ref, nv_ref):
        d_ref[...], nm_ref[...], nv_ref[...] = _adam_math(w_ref[...], g_ref[...], m_ref[...], v_ref[...])

    flat = pl.BlockSpec((tr, cols), lambda i: (i, 0))
    return pl.pallas_call(body, grid=(rows // tr,), in_specs=[flat] * 4, out_specs=[flat] * 3, out_shape=[_sds((rows, cols))] * 3,
                          name=name, compiler_params=_params(("parallel",)))(w, g, m, v)


def _adamw_plain(w, g, m, v, name):
    def body(w_ref, g_ref, m_ref, v_ref, d_ref, nm_ref, nv_ref):
        d, m2, v2 = _adam_math(w_ref[...], g_ref[...], m_ref[...], v_ref[...])
        d_ref[...] = d
        nm_ref[...] = m2
        nv_ref[...] = v2

    return pl.pallas_call(body, out_shape=[_sds(w.shape)] * 3, name=name)(w, g, m, v)


def _pack_small(norm1_w, conv_w, a_log, dt_bias, gdn_norm_w, ret_norm_w, norm2_w, norm_f_w):
    misc = jnp.concatenate([gdn_norm_w.reshape(1, DV), a_log.reshape(1, H), dt_bias.reshape(1, H),
                            jnp.zeros((1, D - DV - 2 * H), F32)], axis=1)
    return jnp.concatenate([norm1_w.reshape(1, D), ret_norm_w.reshape(1, D), norm2_w.reshape(1, D), norm_f_w.reshape(1, D),
                            conv_w.reshape(8, D), misc, jnp.zeros((3, D), F32)], axis=0)


def _unpack_small(pack):
    return dict(norm1_w=pack[0:1], ret_norm_w=pack[1:2], norm2_w=pack[2:3], norm_f_w=pack[3], conv_w=pack[4:12].reshape(4, 2 * D),
                gdn_norm_w=pack[12:13, 0:DV], a_log=pack[12:13, DV:DV + H], dt_bias=pack[12:13, DV + H:DV + 2 * H])


IN_SPLITS = (4096, 2048, 8, 8, 1024, 1024, 2048, 2048, 2048, 2048)


BA_END = sum(IN_SPLITS[:4])
LANES = 128


def _padded_order_blocks():
    z0, ba0, rq0, rk0, rv0, rg0, ga0, gb0 = 4096, 6144, 6400, 7424, 8448, 10496, 12544, 14592
    cols = []
    for h in range(H):
        for base in (z0, rg0, ga0, gb0):
            cols += [base + DV * h, base + DV * h + LANES]
    cols += list(range(0, z0, LANES))
    for h in range(H):
        cols += [rq0 + DK * h, rk0 + DK * h, rv0 + DV * h, rv0 + DV * h + LANES]
    cols += [ba0, ba0 + LANES]
    blocks = np.asarray(cols, np.int32) // LANES
    assert sorted(blocks.tolist()) == list(range(P_IN // LANES))
    return blocks


def _permute_blocks(x, blocks, name):
    rows, cols = x.shape

    def body(p_ref, x_ref, o_ref):
        o_ref[...] = x_ref[...]

    gs = pltpu.PrefetchScalarGridSpec(num_scalar_prefetch=1, grid=(cols // LANES,),
                                      in_specs=[pl.BlockSpec((rows, LANES), lambda j, p: (0, p[j]))],
                                      out_specs=pl.BlockSpec((rows, LANES), lambda j, p: (0, j)))
    return pl.pallas_call(body, grid_spec=gs, out_shape=_sds((rows, cols), x.dtype), name=name,
                          compiler_params=_params(("parallel",)))(jnp.asarray(blocks), x)


def _regroup_w_in(w):
    padded = jnp.concatenate([w[:, :BA_END], jnp.zeros((w.shape[0], P_IN - N_IN), w.dtype), w[:, BA_END:]], axis=1)
    return _permute_blocks(padded, _padded_order_blocks(), "w_in_to_layout")


def _ungroup_w_in(g):
    padded = _permute_blocks(g, np.argsort(_padded_order_blocks()).astype(np.int32), "w_in_grad_from_layout")
    return jnp.concatenate([padded[:, :BA_END], padded[:, BA_END + P_IN - N_IN:]], axis=1)


SHARD_W = N_IN // NDEV
GAP = P_IN - N_IN
WIN = 2304


def _padded_col(c):
    return c + (GAP if c >= BA_END else 0)


WIN_START = [min(_padded_col(SHARD_W * d) // LANES * LANES, P_IN - WIN) for d in range(NDEV)]
WIN_OFF = [_padded_col(SHARD_W * d) - WIN_START[d] for d in range(NDEV)]
STRADDLER = BA_END // SHARD_W
STRADDLE_AT = BA_END - STRADDLER * SHARD_W
assert all(WIN_OFF[d] + SHARD_W + (GAP if d == STRADDLER else 0) <= WIN for d in range(NDEV))


def _win_off(me):
    off = jnp.int32(0)
    for d in range(NDEV):
        off = jnp.where(me == d, jnp.int32(WIN_OFF[d]), off)
    return off


def _window_of_shard(shard, me):
    rows = shard.shape[0]
    zeros = lambda n: jnp.zeros((rows, n), shard.dtype)
    plain = lax.dynamic_update_slice(zeros(WIN), shard, (0, _win_off(me)))
    o = WIN_OFF[STRADDLER]
    split = jnp.concatenate([zeros(o), shard[:, :STRADDLE_AT], zeros(GAP), shard[:, STRADDLE_AT:], zeros(WIN - o - GAP - SHARD_W)], axis=1)
    return jnp.where(me == STRADDLER, split, plain)


def _shard_of_window(win, me):
    plain = lax.dynamic_slice(win, (0, _win_off(me)), (win.shape[0], SHARD_W))
    o = WIN_OFF[STRADDLER]
    split = jnp.concatenate([win[:, o:o + STRADDLE_AT], win[:, o + STRADDLE_AT + GAP:o + GAP + SHARD_W]], axis=1)
    return jnp.where(me == STRADDLER, split, plain)


def _layout_from_windows(wins):
    _, rows, _ = wins.shape
    data = []
    for d in range(NDEV):
        lo = _padded_col(SHARD_W * d)
        data.append([(lo, lo + STRADDLE_AT), (lo + STRADDLE_AT + GAP, lo + GAP + SHARD_W)] if d == STRADDLER else [(lo, lo + SHARD_W)])
    zero_block = (0, WIN // LANES - 1)
    table = []
    for p in _padded_order_blocks():
        src = [(d, int(p) - WIN_START[d] // LANES) for d in range(NDEV)
               if any(lo < (p + 1) * LANES and hi > p * LANES for lo, hi in data[d])]
        assert len(src) <= 2 and all(0 <= b < WIN // LANES for _, b in src)
        src += [zero_block] * (2 - len(src))
        table.append([src[0][0], src[0][1], src[1][0], src[1][1]])
    table = np.asarray(table, np.int32).T.copy()

    per = 5
    assert P_IN // LANES % per == 0

    def body(t_ref, *refs):
        o_ref = refs[-1]
        for i in range(per):
            o_ref[:, i * LANES:(i + 1) * LANES] = refs[2 * i][...] + refs[2 * i + 1][...]

    src = lambda k, i: pl.BlockSpec((None, rows, LANES), lambda j, t: (t[k, per * j + i], 0, t[k + 1, per * j + i]))
    gs = pltpu.PrefetchScalarGridSpec(
        num_scalar_prefetch=1, grid=(P_IN // (per * LANES),), in_specs=[src(k, i) for i in range(per) for k in (0, 2)],
        out_specs=pl.BlockSpec((rows, per * LANES), lambda j, t: (0, j)))
    return pl.pallas_call(body, grid_spec=gs, out_shape=_sds((rows, P_IN), wins.dtype), name="w_in_from_windows",
                          compiler_params=_params(("parallel",)))(jnp.asarray(table), *([wins] * (2 * per)))


def _windows_from_layout(g):
    rows = g.shape[0]
    where = np.argsort(_padded_order_blocks())
    nb = WIN // LANES
    table = np.asarray([where[WIN_START[d] // LANES + b] for d in range(NDEV) for b in range(nb)], np.int32)

    per = 6
    assert nb % per == 0

    def body(t_ref, *refs):
        o_ref = refs[-1]
        for i in range(per):
            o_ref[:, i * LANES:(i + 1) * LANES] = refs[i][...]

    src = lambda i: pl.BlockSpec((rows, LANES), lambda d, b, t: (0, t[d * nb + per * b + i]))
    gs = pltpu.PrefetchScalarGridSpec(num_scalar_prefetch=1, grid=(NDEV, nb // per), in_specs=[src(i) for i in range(per)],
                                      out_specs=pl.BlockSpec((None, rows, per * LANES), lambda d, b, t: (d, 0, b)))
    return pl.pallas_call(body, grid_spec=gs, out_shape=_sds((NDEV, rows, WIN), g.dtype), name="w_in_grad_windows",
                          compiler_params=_params(("parallel", "parallel")))(jnp.asarray(table), *([g] * per))


def _to_slots_cols(g):
    rows, cols = g.shape
    return g.reshape(rows, NDEV, cols // NDEV).transpose(1, 0, 2)


def _from_slots_cols(a):
    n, rows, cols = a.shape
    return a.transpose(1, 0, 2).reshape(rows, n * cols)


WEIGHT_ORDER = ["norm1_w", "w_in", "conv_w", "a_log", "dt_bias", "gdn_norm_w", "ret_norm_w", "w_out", "norm2_w", "w_gate", "w_up",
                "w_down", "norm_f_w"]


def kernel(x, norm1_w, w_in, conv_w, a_log, dt_bias, gdn_norm_w, ret_norm_w, w_out, norm2_w, w_gate, w_up, w_down, norm_f_w, loss_target, m_norm1_w, m_w_in, m_conv_w, m_a_log, m_dt_bias, m_gdn_norm_w, m_ret_norm_w, m_w_out, m_norm2_w, m_w_gate, m_w_up, m_w_down, m_norm_f_w, v_norm1_w, v_w_in, v_conv_w, v_a_log, v_dt_bias, v_gdn_norm_w, v_ret_norm_w, v_w_out, v_norm2_w, v_w_gate, v_w_up, v_w_down, v_norm_f_w):
    ax, ay, ac = _coords()
    me = 4 * ax + 2 * ay + ac
    core = jnp.reshape(ac, (1,)).astype(jnp.int32)
    chip = jnp.reshape(2 * ax + ay, (1,)).astype(jnp.int32)
    w = dict(norm1_w=norm1_w, w_in=w_in[0], conv_w=conv_w[0], a_log=a_log, dt_bias=dt_bias, gdn_norm_w=gdn_norm_w,
             ret_norm_w=ret_norm_w, w_out=w_out[0], norm2_w=norm2_w, w_gate=w_gate[0], w_up=w_up[0], w_down=w_down[0],
             norm_f_w=norm_f_w)
    m = dict(norm1_w=m_norm1_w, w_in=m_w_in[0], conv_w=m_conv_w[0], a_log=m_a_log, dt_bias=m_dt_bias, gdn_norm_w=m_gdn_norm_w,
             ret_norm_w=m_ret_norm_w, w_out=m_w_out[0], norm2_w=m_norm2_w, w_gate=m_w_gate[0], w_up=m_w_up[0], w_down=m_w_down[0],
             norm_f_w=m_norm_f_w)
    v = dict(norm1_w=v_norm1_w, w_in=v_w_in[0], conv_w=v_conv_w[0], a_log=v_a_log, dt_bias=v_dt_bias, gdn_norm_w=v_gdn_norm_w,
             ret_norm_w=v_ret_norm_w, w_out=v_w_out[0], norm2_w=v_norm2_w, w_gate=v_w_gate[0], w_up=v_w_up[0], w_down=v_w_down[0],
             norm_f_w=v_norm_f_w)
    big_names = ["w_in", "w_out", "w_gate", "w_up", "w_down"]

    w_in_wins, conv_all = _run_side(_gather_side([_window_of_shard(w["w_in"].astype(BF16), me), w["conv_w"]]), "w_in_allgather")
    w_in_full = _layout_from_windows(w_in_wins)
    conv_full = _from_slots_cols(conv_all)
    dist = dict(core=core, shards=[w[k].astype(BF16) for k in ("w_out", "w_gate", "w_up", "w_down")])

    loss_tile, dx, big, small = _local_step(
        x[0], loss_target[0], w_in_full, None, None, None, None, norm1_w, conv_full, a_log, dt_bias,
        gdn_norm_w, ret_norm_w, norm2_w, norm_f_w.reshape(1, D), dist=dist)
    loss = lax.psum(loss_tile[0, 0], ("x", "y", "c"))

    g_w_in = _shard_of_window(_sum_slots(big["w_in"], 64, "w_in_grad_sum"), me)
    out = {"w_in": (g_w_in, *_adamw_rows(w["w_in"], g_w_in, m["w_in"], v["w_in"], 64, "adamw_w_in"))}
    for k in ("w_out", "w_gate", "w_up", "w_down"):
        part, recv = big[k]
        out[k] = _adamw_reduced(part, recv, chip, w[k], m[k], v[k], 128, "adamw_" + k)

    g_small = _unpack_small(_allreduce_small(_pack_small(**small), "small_grads_allreduce"))
    g_small["conv_w"] = lax.dynamic_slice_in_dim(g_small["conv_w"], me * (2 * D // NDEV), 2 * D // NDEV, axis=1)
    small_names = [k for k in WEIGHT_ORDER if k not in big_names]
    pad_conv = lambda a: jnp.pad(a, ((0, 0), (0, 2 * D - a.shape[1])))
    packs = []
    for src in (w, g_small, m, v):
        args = {k: (pad_conv(src[k]) if k == "conv_w" else src[k]) for k in small_names}
        packs.append(_pack_small(**args))
    d_pack, m_pack, v_pack = _adamw_plain(*packs[0:1], packs[1], packs[2], packs[3], name="adamw_small")
    cut_conv = lambda dct: {**dct, "conv_w": dct["conv_w"][:, :2 * D // NDEV]}
    d_small, m_small, v_small = (cut_conv(_unpack_small(p)) for p in (d_pack, m_pack, v_pack))

    def shaped(k, a):
        return a.reshape(w_shapes[k])

    w_shapes = dict(norm1_w=norm1_w.shape, w_in=w_in.shape, conv_w=conv_w.shape, a_log=a_log.shape, dt_bias=dt_bias.shape,
                    gdn_norm_w=gdn_norm_w.shape, ret_norm_w=ret_norm_w.shape, w_out=w_out.shape, norm2_w=norm2_w.shape,
                    w_gate=w_gate.shape, w_up=w_up.shape, w_down=w_down.shape, norm_f_w=norm_f_w.shape)
    grads, deltas, new_m, new_v = [], [], [], []
    for k in WEIGHT_ORDER:
        if k in big_names:
            g_, d_, m_, v_ = out[k]
        else:
            g_, d_, m_, v_ = g_small[k], d_small[k], m_small[k], v_small[k]
        grads.append(shaped(k, g_))
        deltas.append(shaped(k, d_))
        new_m.append(shaped(k, m_))
        new_v.append(shaped(k, v_))
    return (loss, dx[None], *grads, *deltas, *new_m, *new_v)
```

```python
import functools
import numpy as np
import jax
import jax.numpy as jnp
from jax import lax
from jax.experimental import pallas as pl
from jax.experimental.pallas import tpu as pltpu

F32, BF16 = jnp.float32, jnp.bfloat16
HI = lax.Precision.HIGHEST
MESH = pl.DeviceIdType.MESH
ANY = pl.BlockSpec(memory_space=pl.ANY)
VMEM_FULL = pl.BlockSpec(memory_space=pltpu.VMEM)

NDEV = 8
D = 2048
H = 8
DK = 128
DV = 256
C = 64
CPB = 4
EPS = 1e-6
ROPE_BASE = 10000.0
N_IN = 16400
O_MERGE, O_QKV, O_RET, O_BA, P_IN = 0, 8192, 12288, 16384, 16640
MERGE_W, RET_W = 4 * DV, 2 * DK + DV
LR, B1, B2, EPS_ADAM, WD, STEP = 0.001, 0.9, 0.999, 1e-08, 0.01, 10
VMEM_CAP = 60 * 1024 * 1024

NN = ((1,), (0,))
NT = ((1,), (1,))
TN = ((0,), (0,))


def _params(sem=None, est=None):
    kw = {}
    if sem is not None:
        kw["dimension_semantics"] = sem
    if est is not None:
        kw["vmem_limit_bytes"] = int(min(VMEM_CAP, max(32 * 1024 * 1024, est * 5 // 4 + (4 << 20))))
    return pltpu.CompilerParams(**kw)


def _sds(shape, dt=F32):
    return jax.ShapeDtypeStruct(tuple(shape), dt)


def _row_tile(rows, limit):
    return max(t for t in range(16, min(rows, limit) + 1, 16) if rows % t == 0)


def _bdot(a, b, dims):
    return lax.dot_general(a.astype(BF16), b.astype(BF16), (dims, ((), ())), preferred_element_type=F32)


def _hdot(a, b, dims):
    return lax.dot_general(a, b, (dims, ((), ())), precision=HI, preferred_element_type=F32)


_sigmoid = jax.nn.sigmoid


def _silu(x):
    return x * _sigmoid(x)


def _rms(x, w):
    return x * lax.rsqrt(jnp.mean(x * x, axis=-1, keepdims=True) + EPS) * w


class _Side:
    def __init__(self, ins, out_shapes, sems, start, finish, relay=None):
        self.ins, self.out_shapes, self.sems, self.start, self.finish = list(ins), list(out_shapes), list(sems), start, finish
        self.relay = relay


def _run_side(side, name):
    ni, no = len(side.ins), len(side.out_shapes)

    def body(*refs):
        ins, outs, sems = refs[:ni], refs[ni:ni + no], refs[ni + no:]
        side.start(ins, outs, *sems)
        if side.relay:
            side.relay(ins, outs, *sems)
        side.finish(ins, outs, *sems)

    return pl.pallas_call(body, in_specs=[ANY] * ni, out_specs=[ANY] * no, out_shape=side.out_shapes,
                          scratch_shapes=[pltpu.SemaphoreType.DMA(s) for s in side.sems], name=name)(*side.ins)


def _matmul(a, b, *, ta=False, tb=False, tm, tn, tk, out_dtype=F32, res=None, side=None, name):
    m = a.shape[1] if ta else a.shape[0]
    k = a.shape[0] if ta else a.shape[1]
    n = b.shape[0] if tb else b.shape[1]
    assert k == (b.shape[1] if tb else b.shape[0])
    tm, tn, tk = min(tm, m), min(tn, n), min(tk, k)
    assert m % tm == 0 and n % tn == 0 and k % tk == 0, (name, m, n, k, tm, tn, tk)
    nk = k // tk
    dims = ((0 if ta else 1,), (1 if tb else 0,))
    has_res = res is not None
    n_in = 3 if has_res else 2
    n_side_in = len(side.ins) if side else 0
    n_side_out = len(side.out_shapes) if side else 0
    grid = (m // tm, n // tn, nk)

    def body(*refs):
        a_ref, b_ref = refs[0], refs[1]
        r_ref = refs[2] if has_res else None
        o_ref = refs[n_in + n_side_in]
        if side:
            side_ins = refs[n_in:n_in + n_side_in]
            side_outs = refs[n_in + n_side_in + 1:n_in + n_side_in + 1 + n_side_out]
            side_sems = refs[len(refs) - len(side.sems):]
            step = (pl.program_id(0) * grid[1] + pl.program_id(1)) * grid[2] + pl.program_id(2)

            @pl.when(step == 0)
            def _():
                side.start(side_ins, side_outs, *side_sems)

        def finish(acc):
            if has_res:
                acc = acc + r_ref[...].astype(F32)
            o_ref[...] = acc.astype(out_dtype)

        part = _bdot(a_ref[...], b_ref[...], dims)
        if nk == 1:
            finish(part)
        else:
            acc_ref = refs[n_in + n_side_in + 1 + n_side_out]
            kk = pl.program_id(2)

            @pl.when(kk == 0)
            def _():
                acc_ref[...] = part

            @pl.when(kk > 0)
            def _():
                acc_ref[...] += part

            @pl.when(kk == nk - 1)
            def _():
                finish(acc_ref[...])

        if side:
            n_steps = grid[0] * grid[1] * grid[2]
            if side.relay:
                @pl.when(step == 3 * n_steps // 4)
                def _():
                    side.relay(side_ins, side_outs, *side_sems)

            @pl.when(step == n_steps - 1)
            def _():
                side.finish(side_ins, side_outs, *side_sems)

    a_spec = pl.BlockSpec((tk, tm), lambda i, j, kk: (kk, i)) if ta else pl.BlockSpec((tm, tk), lambda i, j, kk: (i, kk))
    b_spec = pl.BlockSpec((tn, tk), lambda i, j, kk: (j, kk)) if tb else pl.BlockSpec((tk, tn), lambda i, j, kk: (kk, j))
    o_spec = pl.BlockSpec((tm, tn), lambda i, j, kk: (i, j))
    in_specs = [a_spec, b_spec] + ([o_spec] if has_res else []) + [ANY] * n_side_in
    est = 2 * (tm * tk * a.dtype.itemsize + tk * tn * b.dtype.itemsize + tm * tn * jnp.dtype(out_dtype).itemsize)
    est += 2 * tm * tn * 4 * (1 if has_res else 0) + (tm * tn * 4 if nk > 1 else 0) + 2 * tm * tn * 4
    args = (a, b) + ((res,) if has_res else ()) + (tuple(side.ins) if side else ())
    scratch = ([pltpu.VMEM((tm, tn), F32)] if nk > 1 else []) + ([pltpu.SemaphoreType.DMA(s) for s in side.sems] if side else [])
    sem = ("arbitrary",) * 3 if side else ("parallel", "parallel", "arbitrary")
    out = pl.pallas_call(
        body, grid=grid, in_specs=in_specs, out_specs=[o_spec] + [ANY] * n_side_out,
        out_shape=[_sds((m, n), out_dtype)] + (side.out_shapes if side else []),
        scratch_shapes=scratch, name=name, compiler_params=_params(sem, est))(*args)
    return (out[0], out[1:]) if side else out[0]


def _rms_fwd(x, w, ts, name):
    s = x.shape[0]

    def body(x_ref, w_ref, o_ref, ot_ref):
        y = _rms(x_ref[...], w_ref[...]).astype(BF16)
        o_ref[...] = y
        ot_ref[...] = y.T

    row = pl.BlockSpec((ts, D), lambda i: (i, 0))
    return pl.pallas_call(body, grid=(s // ts,), in_specs=[row, pl.BlockSpec((1, D), lambda i: (0, 0))],
                          out_specs=[row, pl.BlockSpec((D, ts), lambda i: (0, i))],
                          out_shape=[_sds((s, D), BF16), _sds((D, s), BF16)], name=name,
                          compiler_params=_params(("parallel",)))(x, w)


def _rms_bwd(x, w, du, dres, ts, name):
    s = x.shape[0]

    def body(x_ref, w_ref, du_ref, dres_ref, dx_ref, dw_ref):
        _, vjp = jax.vjp(_rms, x_ref[...], w_ref[...])
        dx, dw = vjp(du_ref[...].astype(F32))
        dx_ref[...] = dx + dres_ref[...]

        @pl.when(pl.program_id(0) == 0)
        def _():
            dw_ref[...] = jnp.zeros_like(dw_ref)

        dw_ref[...] += dw

    row = pl.BlockSpec((ts, D), lambda i: (i, 0))
    vec = pl.BlockSpec((1, D), lambda i: (0, 0))
    return pl.pallas_call(body, grid=(s // ts,), in_specs=[row, vec, row, row], out_specs=[row, vec],
                          out_shape=[_sds((s, D)), _sds((1, D))], name=name,
                          compiler_params=_params(("arbitrary",), 12 * ts * D * 4))(x, w, du, dres)


def _conv_taps(xx, w, base, ts):
    acc = xx[base:base + ts] * w[0:1, :]
    for j in range(1, 4):
        acc = acc + xx[base + j:base + j + ts] * w[j:j + 1, :]
    return acc


def _causal_conv(prev8, cur, w, first):
    xx = jnp.concatenate([jnp.where(first, 0.0, prev8), cur], axis=0)
    return _conv_taps(xx, w, 5, cur.shape[0])


def _qk_post(c, scale):
    s = _silu(c)
    return s * lax.rsqrt(jnp.sum(s * s, axis=-1, keepdims=True) + EPS) * scale


def _conv_specs(ts, cw, col0):
    pcol = O_QKV // cw + col0
    cur = pl.BlockSpec((ts, cw), lambda j, i: (i, pcol + j))
    prev = pl.BlockSpec((8, cw), lambda j, i: (jnp.maximum(i * (ts // 8) - 1, 0), pcol + j))
    wsp = pl.BlockSpec((4, cw), lambda j, i: (0, col0 + j))
    return cur, prev, wsp


def _gdn_qkv_fwd(proj, conv_w, ts):
    s = proj.shape[0]

    def qk_body(cur_ref, prev_ref, w_ref, o_ref, c_ref):
        c = _causal_conv(prev_ref[...], cur_ref[...], w_ref[...], pl.program_id(1) == 0)
        scale = jnp.where(pl.program_id(0) < H, DK ** -0.5, 1.0).astype(F32)
        c_ref[...] = c
        o_ref[...] = _qk_post(c, scale)

    tq = min(2 * ts, s)
    cur, prev, wsp = _conv_specs(tq, DK, 0)
    out = pl.BlockSpec((tq, DK), lambda j, i: (i, j))
    qk, c_qk = pl.pallas_call(qk_body, grid=(2 * H, s // tq), in_specs=[cur, prev, wsp], out_specs=[out, out],
                              out_shape=[_sds((s, 2 * H * DK))] * 2, name="gdn_qk_prep",
                              compiler_params=_params(("parallel", "parallel")))(proj, proj, conv_w)

    def v_body(cur_ref, prev_ref, w_ref, o_ref, c_ref):
        c = _causal_conv(prev_ref[...], cur_ref[...], w_ref[...], pl.program_id(1) == 0)
        c_ref[...] = c
        o_ref[...] = _silu(c)

    cw = 512
    cur, prev, wsp = _conv_specs(ts, cw, 2 * H * DK // cw)
    out = pl.BlockSpec((ts, cw), lambda j, i: (i, j))
    v, c_v = pl.pallas_call(v_body, grid=(H * DV // cw, s // ts), in_specs=[cur, prev, wsp], out_specs=[out, out],
                            out_shape=[_sds((s, H * DV))] * 2, name="gdn_v_prep",
                            compiler_params=_params(("parallel", "parallel")))(proj, proj, conv_w)
    return qk, v, c_qk, c_v


def _gdn_qkv_bwd(proj, conv_w, c_qk, c_v, dqk, dv, dproj, ts):
    s = proj.shape[0]
    nt = s // ts

    def qk_body(c_ref, d_ref, o_ref):
        scale = jnp.where(pl.program_id(0) < H, DK ** -0.5, 1.0).astype(F32)
        _, vjp = jax.vjp(lambda cc: _qk_post(cc, scale), c_ref[...])
        o_ref[...] = vjp(d_ref[...])[0]

    tq = min(2 * ts, s)
    blk = pl.BlockSpec((tq, DK), lambda j, i: (i, j))
    dc_qk = pl.pallas_call(qk_body, grid=(2 * H, s // tq), in_specs=[blk, blk], out_specs=blk, out_shape=_sds((s, 2 * H * DK)),
                           name="gdn_qk_prep_bwd", compiler_params=_params(("parallel", "parallel")))(c_qk, dqk)

    def v_body(c_ref, d_ref, o_ref):
        _, vjp = jax.vjp(_silu, c_ref[...])
        o_ref[...] = vjp(d_ref[...])[0]

    cw = 512
    blk = pl.BlockSpec((ts, cw), lambda j, i: (i, j))
    dc_v = pl.pallas_call(v_body, grid=(H * DV // cw, nt), in_specs=[blk, blk], out_specs=blk, out_shape=_sds((s, H * DV)),
                          name="gdn_v_prep_bwd", compiler_params=_params(("parallel", "parallel")))(c_v, dv)

    def conv_bwd(dc, dproj, col0, ncols, name):
        def body(x_ref, xprev_ref, w_ref, dc_ref, dcnext_ref, _, da_ref, dw_ref):
            i = pl.program_id(1)
            w = w_ref[...]
            dcur = dc_ref[...]
            dd = jnp.concatenate([dcur, jnp.where(i == nt - 1, 0.0, dcnext_ref[...])], axis=0)
            acc = dd[3:3 + ts] * w[0:1, :]
            for j in range(1, 4):
                acc = acc + dd[3 - j:3 - j + ts] * w[j:j + 1, :]
            da_ref[...] = acc.astype(BF16)
            xx = jnp.concatenate([jnp.where(i == 0, 0.0, xprev_ref[...]), x_ref[...]], axis=0)

            @pl.when(i == 0)
            def _():
                dw_ref[...] = jnp.zeros_like(dw_ref)

            for j in range(4):
                dw_ref[j:j + 1, :] += jnp.sum(dcur * xx[5 + j:5 + j + ts], axis=0, keepdims=True)

        cur, prev, wsp = _conv_specs(ts, cw, col0)
        dcur = pl.BlockSpec((ts, cw), lambda j, i: (i, j))
        dnext = pl.BlockSpec((8, cw), lambda j, i: (jnp.minimum((i + 1) * (ts // 8), s // 8 - 1), j))
        pcol = O_QKV // cw + col0
        return pl.pallas_call(body, grid=(ncols // cw, nt), in_specs=[cur, prev, wsp, dcur, dnext, ANY],
                              out_specs=[pl.BlockSpec((ts, cw), lambda j, i: (i, pcol + j)), pl.BlockSpec((4, cw), lambda j, i: (0, j))],
                              out_shape=[_sds(dproj.shape, BF16), _sds((4, ncols))], input_output_aliases={5: 0}, name=name,
                              compiler_params=_params(("parallel", "arbitrary")))(proj, proj, conv_w, dc, dc, dproj)

    dproj, dw_qk = conv_bwd(dc_qk, dproj, 0, 2 * H * DK, "conv_bwd_qk")
    dproj, dw_v = conv_bwd(dc_v, dproj, 2 * H * DK // cw, H * DV, "conv_bwd_v")
    return dproj, jnp.concatenate([dw_qk, dw_v], axis=1)


def _bg(b, a, alog, dtb):
    n = b.shape[0]
    g = -jnp.exp(alog) * jax.nn.softplus(a + dtb)
    row = lax.broadcasted_iota(jnp.int32, (n, n), 0)
    col = lax.broadcasted_iota(jnp.int32, (n, n), 1)
    shift = C.bit_length() - 1
    same = (row >> shift) == (col >> shift)
    return _sigmoid(b), _hdot((same & (row >= col)).astype(F32), g, NN), _hdot(same.astype(F32), g, NN)


def _bg_fwd(proj, alog, dtb, ts):
    s = proj.shape[0]

    def body(ba_ref, alog_ref, dtb_ref, beta_ref, gc_ref, gl_ref):
        beta_ref[...], gc_ref[...], gl_ref[...] = _bg(ba_ref[:, 0:H], ba_ref[:, H:2 * H], alog_ref[...], dtb_ref[...])

    small = pl.BlockSpec((1, H), lambda i: (0, 0))
    out = pl.BlockSpec((ts, H), lambda i: (i, 0))
    return pl.pallas_call(body, grid=(s // ts,), in_specs=[pl.BlockSpec((ts, 256), lambda i: (i, O_BA // 256)), small, small],
                          out_specs=[out] * 3, out_shape=[_sds((s, H))] * 3, name="gdn_bg_prep",
                          compiler_params=_params(("parallel",)))(proj, alog, dtb)


def _bg_bwd(proj, alog, dtb, dbeta_h, dgc_h, dgl_h, dproj, ts):
    s = proj.shape[0]

    def body(ba_ref, alog_ref, dtb_ref, dbeta_ref, dgc_ref, dgl_ref, _, dba_ref, dalog_ref, ddtb_ref):
        _, vjp = jax.vjp(_bg, ba_ref[:, 0:H], ba_ref[:, H:2 * H], alog_ref[...], dtb_ref[...])
        db, da, dalog, ddtb = vjp((jnp.sum(dbeta_ref[...], axis=0), jnp.sum(dgc_ref[...], axis=0), jnp.sum(dgl_ref[...], axis=0)))
        dba_ref[...] = jnp.zeros_like(dba_ref)
        dba_ref[:, 0:H] = db.astype(BF16)
        dba_ref[:, H:2 * H] = da.astype(BF16)

        @pl.when(pl.program_id(0) == 0)
        def _():
            dalog_ref[...] = jnp.zeros_like(dalog_ref)
            ddtb_ref[...] = jnp.zeros_like(ddtb_ref)

        dalog_ref[...] += dalog
        ddtb_ref[...] += ddtb

    small = pl.BlockSpec((1, H), lambda i: (0, 0))
    per_head = pl.BlockSpec((H, ts, H), lambda i: (0, i, 0))
    return pl.pallas_call(body, grid=(s // ts,),
                          in_specs=[pl.BlockSpec((ts, 256), lambda i: (i, O_BA // 256)), small, small, per_head, per_head, per_head, ANY],
                          out_specs=[pl.BlockSpec((ts, 256), lambda i: (i, O_BA // 256)), small, small],
                          out_shape=[_sds(dproj.shape, BF16), _sds((1, H)), _sds((1, H))], input_output_aliases={6: 0},
                          name="gdn_bg_prep_bwd", compiler_params=_params(("arbitrary",)))(proj, alog, dtb, dbeta_h, dgc_h, dgl_h, dproj)


BLK = 4 * C
NNB, NTB, TNB = ((2,), (1,)), ((2,), (2,)), ((1,), (1,))


def _bdot_b(a, b, dims):
    return lax.dot_general(a.astype(BF16), b.astype(BF16), (dims, ((0,), (0,))), preferred_element_type=F32)


@jax.custom_vjp
def _inv_unit_lower(a):
    n = a.shape[-1]
    row = lax.broadcasted_iota(jnp.int32, (n, n), 0)
    col = lax.broadcasted_iota(jnp.int32, (n, n), 1)
    x = jnp.where(row == col, 1.0, 0.0).astype(F32) - a
    p = _bdot_b(a, a, NNB)
    power = 2
    while True:
        x = x + _bdot_b(x, p, NNB)
        power *= 2
        if power >= C:
            return x
        p = _bdot_b(p, p, NNB)


def _inv_fwd(a):
    t = _inv_unit_lower(a)
    return t, t


def _inv_bwd(t, dt):
    return (-_bdot_b(_bdot_b(t, dt, TNB), t, NTB),)


_inv_unit_lower.defvjp(_inv_fwd, _inv_bwd)


def _gdn_prep(q, k, v, bfull, gcfull, glfull, hmask):
    nb, n = q.shape[0], q.shape[1]
    beta = jnp.sum(bfull * hmask, axis=-1, keepdims=True)
    gc = jnp.sum(gcfull * hmask, axis=-1, keepdims=True)
    gl = jnp.sum(glfull * hmask, axis=-1, keepdims=True)
    row = lax.broadcasted_iota(jnp.int32, (n, n), 0)
    col = lax.broadcasted_iota(jnp.int32, (n, n), 1)
    shift = C.bit_length() - 1
    same = (row >> shift) == (col >> shift)
    incl, strict = same & (row >= col), same & (row > col)
    g_i = gc * jnp.ones((1, 1, n), F32)
    decay = jnp.exp(jnp.where(incl, g_i - jnp.swapaxes(g_i, 1, 2), -jnp.inf))
    kb = k * beta
    a = jnp.where(strict, _bdot_b(kb, k, NTB) * decay, 0.0)
    tinv = _inv_unit_lower(a)
    u = _bdot_b(tinv, v * beta, NNB)
    w = _bdot_b(tinv, kb * jnp.exp(gc), NNB)
    attn = _bdot_b(q, k, NTB) * decay
    fold = ((lax.broadcasted_iota(jnp.int32, (n, C), 0) & (C - 1)) == lax.broadcasted_iota(jnp.int32, (n, C), 1)).astype(F32)
    attn_c = _bdot(attn.reshape(nb * n, n), fold, NN).reshape(nb, n, C)
    return u, w, attn_c, q * jnp.exp(gc), k * jnp.exp(gl - gc), jnp.exp(gl)


def _gdn_step(u, w, attn, qg, kd, egl, state):
    v_new = u - _bdot_b(w, state, NNB)
    o = _bdot_b(qg, state, NNB) + _bdot_b(attn, v_new, NNB)
    return o, state * egl + _bdot_b(kd, v_new, TNB)


def _heads(ref, rs, width):
    return jnp.stack([ref[rs, h * width:(h + 1) * width] for h in range(H)])


def _head_mask(h):
    return (lax.broadcasted_iota(jnp.int32, (1, H), 1) == h).astype(F32)


PREP_BLOCKS = 4


def _gdn_prep_specs(r):
    small = pl.BlockSpec((r, H), lambda h, c: (c, 0))
    return [pl.BlockSpec((r, DK), lambda h, c: (c, h)), pl.BlockSpec((r, DK), lambda h, c: (c, H + h)),
            pl.BlockSpec((r, DV), lambda h, c: (c, h)), small, small, small]


def _blocked(ref):
    x = ref[...]
    return x.reshape(PREP_BLOCKS, BLK, x.shape[-1])


def _gdn_inter_specs(r):
    col = pl.BlockSpec((r, DK), lambda h, c: (c, h))
    return [pl.BlockSpec((r, DV), lambda h, c: (c, h)), col, pl.BlockSpec((1, r, C), lambda h, c: (h, c, 0)), col, col,
            pl.BlockSpec((1, r // C, 8, 128), lambda h, c: (h, c, 0, 0))]


def _gdn_prep_fwd(qk, v, beta, gc, gl):
    s = qk.shape[0]
    r = PREP_BLOCKS * BLK

    def body(q_ref, k_ref, v_ref, b_ref, gc_ref, gl_ref, u_ref, w_ref, attn_ref, qg_ref, kd_ref, egl_ref):
        u, w, attn, qg, kd, egl = _gdn_prep(_blocked(q_ref), _blocked(k_ref), _blocked(v_ref), _blocked(b_ref), _blocked(gc_ref),
                                            _blocked(gl_ref), _head_mask(pl.program_id(0)))
        u_ref[...] = u.reshape(r, DV)
        w_ref[...] = w.reshape(r, DK).astype(BF16)
        attn_ref[0] = attn.reshape(r, C).astype(BF16)
        qg_ref[...] = qg.reshape(r, DK).astype(BF16)
        kd_ref[...] = kd.reshape(r, DK).astype(BF16)
        egl = egl.reshape(r, 1)
        for j in range(r // C):
            egl_ref[0, j] = egl[j * C:j * C + 1, :] * jnp.ones((8, 128), F32)

    out_shape = [_sds((s, H * DV)), _sds((s, H * DK), BF16), _sds((H, s, C), BF16), _sds((s, H * DK), BF16),
                 _sds((s, H * DK), BF16), _sds((H, s // C, 8, 128))]
    return pl.pallas_call(body, grid=(H, s // r), in_specs=_gdn_prep_specs(r), out_specs=_gdn_inter_specs(r), out_shape=out_shape,
                          name="gdn_prep_fwd", compiler_params=_params(("parallel", "parallel")))(qk, qk, v, beta, gc, gl)


def _gdn_prep_bwd(qk, v, beta, gc, gl, du, dw, dattn, dqg, dkd, degl):
    s = qk.shape[0]
    r = PREP_BLOCKS * BLK

    def body(q_ref, k_ref, v_ref, b_ref, gc_ref, gl_ref, du_ref, dw_ref, dattn_ref, dqg_ref, dkd_ref, degl_ref,
             dq_ref, dk_ref, dv_ref, db_ref, dgc_ref, dgl_ref):
        hmask = _head_mask(pl.program_id(0))
        _, vjp = jax.vjp(lambda q, k, v, b, gc, gl: _gdn_prep(q, k, v, b, gc, gl, hmask), _blocked(q_ref), _blocked(k_ref),
                         _blocked(v_ref), _blocked(b_ref), _blocked(gc_ref), _blocked(gl_ref))
        rowid = lax.broadcasted_iota(jnp.int32, (r, 1), 0)
        degl = jnp.zeros((r, 1), F32)
        for j in range(r // C):
            degl = jnp.where(rowid == j * C, degl_ref[0, j, 0:1, 0:1], degl)
        dq, dk, dv, db, dgc, dgl = vjp((_blocked(du_ref), _blocked(dw_ref), _blocked(dattn_ref.at[0]), _blocked(dqg_ref),
                                        _blocked(dkd_ref), degl.reshape(PREP_BLOCKS, BLK, 1)))
        dq_ref[...] = dq.reshape(r, DK)
        dk_ref[...] = dk.reshape(r, DK)
        dv_ref[...] = dv.reshape(r, DV)
        db_ref[0] = db.reshape(r, H)
        dgc_ref[0] = dgc.reshape(r, H)
        dgl_ref[0] = dgl.reshape(r, H)

    col = pl.BlockSpec((r, DK), lambda h, c: (c, h))
    piece = pl.BlockSpec((1, r, H), lambda h, c: (h, c, 0))
    dq, dk, dv, db, dgc, dgl = pl.pallas_call(
        body, grid=(H, s // r), in_specs=_gdn_prep_specs(r) + _gdn_inter_specs(r),
        out_specs=[col, col, pl.BlockSpec((r, DV), lambda h, c: (c, h)), piece, piece, piece],
        out_shape=[_sds((s, H * DK)), _sds((s, H * DK)), _sds((s, H * DV))] + [_sds((H, s, H))] * 3,
        name="gdn_prep_bwd", compiler_params=_params(("parallel", "parallel"), 24 << 20))(
            qk, qk, v, beta, gc, gl, du, dw, dattn, dqg, dkd, degl)
    return jnp.concatenate([dq, dk], axis=1), dv, db, dgc, dgl


def _gdn_scan_specs(r, order):
    wide = pl.BlockSpec((r, H * DK), lambda c: (order(c), 0))
    return [pl.BlockSpec((r, H * DV), lambda c: (order(c), 0)), wide, pl.BlockSpec((H, r, C), lambda c: (0, order(c), 0)), wide, wide,
            pl.BlockSpec((H, r // C, 8, 128), lambda c: (0, order(c), 0, 0))]


def _gdn_scan_fwd(u, w, attn, qg, kd, egl):
    s = u.shape[0]
    r = CPB * C
    nb = s // r

    def body(u_ref, w_ref, attn_ref, qg_ref, kd_ref, egl_ref, o_ref, st_ref, state_ref):
        @pl.when(pl.program_id(0) == 0)
        def _():
            state_ref[...] = jnp.zeros_like(state_ref)

        state = state_ref[...]
        for i in range(CPB):
            rs = slice(i * C, (i + 1) * C)
            st_ref[:, i] = state
            o, state = _gdn_step(_heads(u_ref, rs, DV), _heads(w_ref, rs, DK), attn_ref[:, rs, :], _heads(qg_ref, rs, DK),
                                 _heads(kd_ref, rs, DK), egl_ref[:, i, 0:1, 0:1], state)
            for h in range(H):
                o_ref[rs, h * DV:(h + 1) * DV] = o[h]
        state_ref[...] = state

    out_specs = [pl.BlockSpec((r, H * DV), lambda c: (c, 0)), pl.BlockSpec((H, CPB, DK, DV), lambda c: (0, c, 0, 0))]
    return pl.pallas_call(body, grid=(nb,), in_specs=_gdn_scan_specs(r, lambda c: c), out_specs=out_specs,
                          out_shape=[_sds((s, H * DV)), _sds((H, s // C, DK, DV))],
                          scratch_shapes=[pltpu.VMEM((H, DK, DV), F32)], name="gdn_scan_fwd",
                          compiler_params=_params(("arbitrary",), 24 << 20))(u, w, attn, qg, kd, egl)


def _gdn_scan_bwd(u, w, attn, qg, kd, egl, states, do):
    s = u.shape[0]
    r = CPB * C
    nb = s // r

    def body(u_ref, w_ref, attn_ref, qg_ref, kd_ref, egl_ref, st_ref, do_ref,
             du_ref, dw_ref, dattn_ref, dqg_ref, dkd_ref, degl_ref, dstate_ref):
        @pl.when(pl.program_id(0) == 0)
        def _():
            dstate_ref[...] = jnp.zeros_like(dstate_ref)

        dstate = dstate_ref[...]
        for i in reversed(range(CPB)):
            rs = slice(i * C, (i + 1) * C)
            _, vjp = jax.vjp(_gdn_step, _heads(u_ref, rs, DV), _heads(w_ref, rs, DK).astype(F32), attn_ref[:, rs, :].astype(F32),
                             _heads(qg_ref, rs, DK).astype(F32), _heads(kd_ref, rs, DK).astype(F32), egl_ref[:, i, 0:1, 0:1],
                             st_ref[:, i])
            du, dw, dattn, dqg, dkd, degl, dstate = vjp((_heads(do_ref, rs, DV), dstate))
            dattn_ref[:, rs, :] = dattn
            degl_ref[:, i] = degl * jnp.ones((1, 8, 128), F32)
            for h in range(H):
                du_ref[rs, h * DV:(h + 1) * DV] = du[h]
                dw_ref[rs, h * DK:(h + 1) * DK] = dw[h]
                dqg_ref[rs, h * DK:(h + 1) * DK] = dqg[h]
                dkd_ref[rs, h * DK:(h + 1) * DK] = dkd[h]
        dstate_ref[...] = dstate

    rev = lambda c: nb - 1 - c
    in_specs = _gdn_scan_specs(r, rev) + [pl.BlockSpec((H, CPB, DK, DV), lambda c: (0, rev(c), 0, 0)),
                                          pl.BlockSpec((r, H * DV), lambda c: (rev(c), 0))]
    return pl.pallas_call(
        body, grid=(nb,), in_specs=in_specs, out_specs=_gdn_scan_specs(r, rev),
        out_shape=[_sds((s, H * DV)), _sds((s, H * DK)), _sds((H, s, C)), _sds((s, H * DK)), _sds((s, H * DK)),
                   _sds((H, s // C, 8, 128))],
        scratch_shapes=[pltpu.VMEM((H, DK, DV), F32)], name="gdn_scan_bwd",
        compiler_params=_params(("arbitrary",), 40 << 20))(u, w, attn, qg, kd, egl, states, do)


def _rot(x, cs, sn):
    return x * cs + pltpu.roll(x, DK // 2, 1) * sn


def _rot_t(d, cs, sn):
    return d * cs - pltpu.roll(d, DK // 2, 1) * sn


def _ret_chunk(q, k, v, state, lg):
    n = q.shape[1]
    row = lax.broadcasted_iota(jnp.int32, (n, n), 0)
    col = lax.broadcasted_iota(jnp.int32, (n, n), 1)
    dist = (row - col).astype(F32)
    dmat = jnp.exp(jnp.where(dist >= 0, dist * lg, -jnp.inf))
    scores = _bdot_b(q, k, NTB) * dmat
    pos = lax.broadcasted_iota(jnp.int32, (n, 1), 0).astype(F32)
    xi = jnp.exp((pos + 1.0) * lg)
    zeta = jnp.exp((n - 1.0 - pos) * lg)
    o = _bdot_b(scores, v, NNB) + _bdot_b(q, state, NNB) * xi
    new_state = state * jnp.exp(n * lg) + _bdot_b(k * zeta, v, TNB)
    return o, new_state


def _ret_specs(r, order):
    return [pl.BlockSpec((r, H * RET_W), lambda c: (order(c), O_RET // (H * RET_W))), pl.BlockSpec((r, DK), lambda c: (order(c), 0)),
            pl.BlockSpec((r, DK), lambda c: (order(c), 0)), pl.BlockSpec((H, 1, 1), lambda c: (0, 0, 0))]


def _ret_qkv(x_ref, cs, sn):
    q = jnp.stack([_rot(x_ref[:, h * RET_W:h * RET_W + DK], cs, sn) for h in range(H)])
    k = jnp.stack([_rot(x_ref[:, h * RET_W + DK:h * RET_W + 2 * DK], cs, sn) for h in range(H)]) * DK ** -0.5
    v = jnp.stack([x_ref[:, h * RET_W + 2 * DK:(h + 1) * RET_W] for h in range(H)])
    return q, k, v


RET_C = 256


def _ret_scan_fwd(proj, cs, sn, lgtab):
    s = proj.shape[0]
    r = min(RET_C, s)
    nb = s // r

    def body(x_ref, cs_ref, sn_ref, lg_ref, o_ref, st_ref, state_ref):
        @pl.when(pl.program_id(0) == 0)
        def _():
            state_ref[...] = jnp.zeros_like(state_ref)

        state = state_ref[...]
        st_ref[:, 0] = state
        q, k, v = _ret_qkv(x_ref, cs_ref[...], sn_ref[...])
        o, state_ref[...] = _ret_chunk(q, k, v, state, lg_ref[...])
        for h in range(H):
            o_ref[:, h * DV:(h + 1) * DV] = o[h]

    out_specs = [pl.BlockSpec((r, H * DV), lambda c: (c, 0)), pl.BlockSpec((H, 1, DK, DV), lambda c: (0, c, 0, 0))]
    return pl.pallas_call(body, grid=(nb,), in_specs=_ret_specs(r, lambda c: c), out_specs=out_specs,
                          out_shape=[_sds((s, H * DV)), _sds((H, nb, DK, DV))],
                          scratch_shapes=[pltpu.VMEM((H, DK, DV), F32)], name="ret_scan_fwd",
                          compiler_params=_params(("arbitrary",), 32 << 20))(proj, cs, sn, lgtab.reshape(H, 1, 1))


def _ret_scan_bwd(proj, cs, sn, lgtab, states, do, dproj):
    s = proj.shape[0]
    r = min(RET_C, s)
    nb = s // r

    def body(x_ref, cs_ref, sn_ref, lg_ref, st_ref, do_ref, _, d_ref, dstate_ref):
        @pl.when(pl.program_id(0) == 0)
        def _():
            dstate_ref[...] = jnp.zeros_like(dstate_ref)

        cs_, sn_ = cs_ref[...], sn_ref[...]
        lg = lg_ref[...]
        q, k, v = _ret_qkv(x_ref, cs_, sn_)
        _, vjp = jax.vjp(lambda q, k, v, st: _ret_chunk(q, k, v, st, lg), q, k, v, st_ref[:, 0])
        dq, dk, dv, dstate_ref[...] = vjp((_heads(do_ref, slice(None), DV), dstate_ref[...]))
        for h in range(H):
            d_ref[:, h * RET_W:h * RET_W + DK] = _rot_t(dq[h], cs_, sn_).astype(BF16)
            d_ref[:, h * RET_W + DK:h * RET_W + 2 * DK] = _rot_t(dk[h] * DK ** -0.5, cs_, sn_).astype(BF16)
            d_ref[:, h * RET_W + 2 * DK:(h + 1) * RET_W] = dv[h].astype(BF16)

    rev = lambda c: nb - 1 - c
    in_specs = _ret_specs(r, rev) + [pl.BlockSpec((H, 1, DK, DV), lambda c: (0, rev(c), 0, 0)),
                                     pl.BlockSpec((r, H * DV), lambda c: (rev(c), 0)), ANY]
    return pl.pallas_call(
        body, grid=(nb,), in_specs=in_specs, out_specs=pl.BlockSpec((r, H * RET_W), lambda c: (rev(c), O_RET // (H * RET_W))),
        out_shape=_sds(dproj.shape, BF16), input_output_aliases={6: 0},
        scratch_shapes=[pltpu.VMEM((H, DK, DV), F32)], name="ret_scan_bwd",
        compiler_params=_params(("arbitrary",), 48 << 20))(proj, cs, sn, lgtab.reshape(H, 1, 1), states, do, dproj)


def _merge(oa, z, ob, rg, ga, gb, wa, wb):
    ya = oa * lax.rsqrt(jnp.mean(oa * oa, axis=-1, keepdims=True) + EPS) * wa * _silu(z)
    mu = jnp.mean(ob, axis=-1, keepdims=True)
    var = jnp.mean(jnp.square(ob - mu), axis=-1, keepdims=True)
    yb = (ob - mu) * lax.rsqrt(var + EPS) * wb * _silu(rg)
    return _sigmoid(ga) * ya + _sigmoid(gb) * yb


def _merge_specs(ts):
    own = pl.BlockSpec((ts, DV), lambda h, i: (i, h))
    grp = lambda k: pl.BlockSpec((ts, DV), lambda h, i: (i, O_MERGE // DV + 4 * h + k))
    return [own, grp(0), own, grp(1), grp(2), grp(3),
            pl.BlockSpec((1, DV), lambda h, i: (0, 0)), pl.BlockSpec((1, DV), lambda h, i: (0, h))]


def _merge_fwd(oa, ob, proj, wa, wb, ts):
    s = oa.shape[0]

    def body(oa_ref, z_ref, ob_ref, rg_ref, ga_ref, gb_ref, wa_ref, wb_ref, o_ref, ot_ref):
        y = _merge(oa_ref[...], z_ref[...], ob_ref[...], rg_ref[...], ga_ref[...], gb_ref[...],
                   wa_ref[...], wb_ref[...]).astype(BF16)
        o_ref[...] = y
        ot_ref[...] = y.T

    return pl.pallas_call(body, grid=(H, s // ts), in_specs=_merge_specs(ts),
                          out_specs=[pl.BlockSpec((ts, DV), lambda h, i: (i, h)), pl.BlockSpec((DV, ts), lambda h, i: (h, i))],
                          out_shape=[_sds((s, H * DV), BF16), _sds((H * DV, s), BF16)], name="merge_fwd",
                          compiler_params=_params(("parallel", "parallel")))(oa, proj, ob, proj, proj, proj, wa, wb)


def _merge_bwd(oa, ob, proj, wa, wb, dmixed, ts):
    s = oa.shape[0]

    def body(oa_ref, z_ref, ob_ref, rg_ref, ga_ref, gb_ref, wa_ref, wb_ref, dm_ref,
             doa_ref, dob_ref, dgrp_ref, dwa_ref, dwb_ref):
        _, vjp = jax.vjp(_merge, oa_ref[...], z_ref[...], ob_ref[...], rg_ref[...], ga_ref[...], gb_ref[...],
                         wa_ref[...], wb_ref[...])
        doa, dz, dob, drg, dga, dgb, dwa, dwb = vjp(dm_ref[...].astype(F32))
        doa_ref[...] = doa
        dob_ref[...] = dob
        for k, d in enumerate((dz, drg, dga, dgb)):
            dgrp_ref[:, k * DV:(k + 1) * DV] = d.astype(BF16)
        first_tile = pl.program_id(1) == 0

        @pl.when(first_tile & (pl.program_id(0) == 0))
        def _():
            dwa_ref[...] = jnp.zeros_like(dwa_ref)

        @pl.when(first_tile)
        def _():
            dwb_ref[...] = jnp.zeros_like(dwb_ref)

        dwa_ref[...] += dwa
        dwb_ref[...] += dwb

    blk = pl.BlockSpec((ts, DV), lambda h, i: (i, h))
    out_specs = [blk, blk, pl.BlockSpec((ts, MERGE_W), lambda h, i: (i, O_MERGE // MERGE_W + h)),
                 pl.BlockSpec((1, DV), lambda h, i: (0, 0)), pl.BlockSpec((1, DV), lambda h, i: (0, h))]
    out_shape = [_sds((s, H * DV)), _sds((s, H * DV)), _sds((s, P_IN), BF16), _sds((1, DV)), _sds((1, H * DV))]
    return pl.pallas_call(body, grid=(H, s // ts), in_specs=_merge_specs(ts) + [blk], out_specs=out_specs, out_shape=out_shape,
                          name="merge_bwd", compiler_params=_params(("arbitrary", "arbitrary"), 40 * ts * DV * 4))(
                              oa, proj, ob, proj, proj, proj, wa, wb, dmixed)


def _act(hg, hu):
    return _silu(hg) * hu


def _ffn_gate_up(hn, w_gate, w_up, tm, tn):
    s, f = hn.shape[0], w_gate.shape[1]
    tm, tn = min(tm, s), min(tn, f)
    assert s % tm == 0 and f % tn == 0 and tm % 256 == 0
    sub = tm // 2

    def body(a_ref, wg_ref, wu_ref, pg_ref, pu_ref, act_ref, actt_ref):
        for r0 in range(0, tm, sub):
            rs = slice(r0, r0 + sub)
            hg = _bdot(a_ref[rs, :], wg_ref[...], NN)
            hu = _bdot(a_ref[rs, :], wu_ref[...], NN)
            y, vjp = jax.vjp(_act, hg, hu)
            pg, pu = vjp(jnp.ones_like(y))
            y = y.astype(BF16)
            pg_ref[rs, :] = pg.astype(BF16)
            pu_ref[rs, :] = pu.astype(BF16)
            act_ref[rs, :] = y
            actt_ref[:, rs] = y.T

    wsp = pl.BlockSpec((D, tn), lambda i, j: (0, j))
    blk = pl.BlockSpec((tm, tn), lambda i, j: (i, j))
    est = 2 * (tm * D * 2 + 2 * D * tn * 2 + 4 * tm * tn * 2) + 4 * sub * tn * 4
    return pl.pallas_call(body, grid=(s // tm, f // tn), in_specs=[pl.BlockSpec((tm, D), lambda i, j: (i, 0)), wsp, wsp],
                          out_specs=[blk, blk, blk, pl.BlockSpec((tn, tm), lambda i, j: (j, i))],
                          out_shape=[_sds((s, f), BF16)] * 3 + [_sds((f, s), BF16)], name="ffn_gate_up",
                          compiler_params=_params(("parallel", "parallel"), est))(hn, w_gate, w_up)


def _ffn_down_dx(dh2, w_down, pg, pu, tm, tn):
    s, f = pg.shape
    tm, tn = min(tm, s), min(tn, f)
    assert s % tm == 0 and f % tn == 0 and tm % 256 == 0
    sub = tm // 2

    def body(d_ref, w_ref, pg_ref, pu_ref, dhg_ref, dhu_ref):
        for r0 in range(0, tm, sub):
            rs = slice(r0, r0 + sub)
            dact = _bdot(d_ref[rs, :], w_ref[...], NT)
            dhg_ref[rs, :] = (dact * pg_ref[rs, :].astype(F32)).astype(BF16)
            dhu_ref[rs, :] = (dact * pu_ref[rs, :].astype(F32)).astype(BF16)

    blk = pl.BlockSpec((tm, tn), lambda i, j: (i, j))
    est = 2 * (tm * D * 4 + tn * D * 2 + 4 * tm * tn * 2) + 6 * sub * tn * 4
    return pl.pallas_call(body, grid=(s // tm, f // tn),
                          in_specs=[pl.BlockSpec((tm, D), lambda i, j: (i, 0)), pl.BlockSpec((tn, D), lambda i, j: (j, 0)), blk, blk],
                          out_specs=[blk, blk], out_shape=[_sds((s, f), BF16)] * 2, name="ffn_down_dx",
                          compiler_params=_params(("parallel", "parallel"), est))(dh2, w_down, pg, pu)


def _loss_rows(h2, wf, tgt):
    err = _rms(h2, wf) - tgt
    return 0.5 * jnp.sum(jnp.mean(err * err, axis=-1, keepdims=True), keepdims=True)


def _loss_fwd_bwd(h2, wf, tgt, ts):
    s = h2.shape[0]

    def body(h_ref, w_ref, t_ref, loss_ref, dh_ref, dhb_ref, dw_ref):
        loss, vjp = jax.vjp(_loss_rows, h_ref[...], w_ref[...], t_ref[...])
        dh, dw, _ = vjp(jnp.ones((1, 1), F32))
        dh_ref[...] = dh
        dhb_ref[...] = dh.astype(BF16)

        @pl.when(pl.program_id(0) == 0)
        def _():
            loss_ref[...] = jnp.zeros_like(loss_ref)
            dw_ref[...] = jnp.zeros_like(dw_ref)

        loss_ref[...] += loss
        dw_ref[...] += dw

    row = pl.BlockSpec((ts, D), lambda i: (i, 0))
    vec = pl.BlockSpec((1, D), lambda i: (0, 0))
    tile = pl.BlockSpec((8, 128), lambda i: (0, 0))
    return pl.pallas_call(body, grid=(s // ts,), in_specs=[row, vec, row], out_specs=[tile, row, row, vec],
                          out_shape=[_sds((8, 128)), _sds((s, D)), _sds((s, D), BF16), _sds((1, D))], name="final_norm_loss",
                          compiler_params=_params(("arbitrary",), 12 * ts * D * 4))(h2, wf, tgt)


def _rope_tables(s):
    inv = ROPE_BASE ** (-jnp.arange(0, DK, 2, dtype=F32) / DK)
    ang = jnp.arange(s, dtype=F32)[:, None] * inv[None, :]
    cos, sin = jnp.cos(ang), jnp.sin(ang)
    return jnp.concatenate([cos, cos], axis=1), jnp.concatenate([-sin, sin], axis=1)


def _local_step(x, tgt, w_in, w_out, w_gate, w_up, w_down, norm1_w, conv_w, a_log, dt_bias, gdn_norm_w, ret_norm_w, norm2_w, norm_f_w,
                dist=None):
    s = x.shape[0]
    ts = min(512, s)
    cs, sn = _rope_tables(s)
    lgtab = jnp.log1p(-jnp.exp2(-5.0 - jnp.arange(H, dtype=F32))).reshape(1, H)

    u, u_t = _rms_fwd(x, norm1_w, ts, "norm1_fwd")
    if dist is None:
        proj = _matmul(u, w_in, tm=1024, tn=1280, tk=D, name="in_proj")
    else:
        proj, gathered = _matmul(u, w_in, tm=1024, tn=1280, tk=D, side=_gather_side(dist["shards"]), name="in_proj")
        w_out, w_gate, w_up, w_down = (gathered[0].reshape(D, D), _from_slots_cols(gathered[1]), _from_slots_cols(gathered[2]),
                                       gathered[3].reshape(-1, D))
    qk, va, c_qk, c_v = _gdn_qkv_fwd(proj, conv_w, ts)
    beta, gc, gl = _bg_fwd(proj, a_log, dt_bias, ts)
    inter = _gdn_prep_fwd(qk, va, beta, gc, gl)
    oa, st_a = _gdn_scan_fwd(*inter)
    ob, st_b = _ret_scan_fwd(proj, cs, sn, lgtab)
    mixed, mixed_t = _merge_fwd(oa, ob, proj, gdn_norm_w, ret_norm_w, min(2 * ts, s))
    h1 = _matmul(mixed, w_out, tm=1024, tn=1024, tk=D, res=x, name="out_proj")
    hn, hn_t = _rms_fwd(h1, norm2_w, ts, "norm2_fwd")
    act_dg, act_du, act, act_t = _ffn_gate_up(hn, w_gate, w_up, 1024, 512)
    h2 = _matmul(act, w_down, tm=512, tn=1024, tk=5632, res=h1, name="ffn_down")
    loss, dh2, dh2_b, d_norm_f = _loss_fwd_bwd(h2, norm_f_w, tgt, ts)

    dhg, dhu = _ffn_down_dx(dh2_b, w_down, act_dg, act_du, 1024, 512)
    g_down = _matmul(act_t, dh2_b, tm=512, tn=512, tk=8192, out_dtype=BF16, name="ffn_down_dw")
    g_gate = _matmul(hn_t, dhg, tm=512, tn=512, tk=8192, out_dtype=BF16, name="ffn_gate_dw")
    g_up = _matmul(hn_t, dhu, tm=512, tn=512, tk=8192, out_dtype=BF16, name="ffn_up_dw")
    dhn = _matmul(dhg, w_gate, tb=True, tm=512, tn=1024, tk=5632, name="ffn_gate_dx")
    dhn = _matmul(dhu, w_up, tb=True, tm=512, tn=1024, tk=5632, res=dhn, name="ffn_up_dx")
    dh1, d_norm2 = _rms_bwd(h1, norm2_w, dhn, dh2, ts, "norm2_bwd")

    g_out = _matmul(mixed_t, dh1, tm=1024, tn=1024, tk=D, out_dtype=BF16, name="out_proj_dw")
    early = ["w_out", "w_gate", "w_up", "w_down"]
    if dist is None:
        dmixed = _matmul(dh1, w_out, tb=True, tm=1024, tn=1024, tk=D, out_dtype=BF16, name="out_proj_dx")
    else:
        slots = dict(w_out=g_out.reshape(NDEV, D // NDEV, D), w_gate=_to_slots_cols(g_gate), w_up=_to_slots_cols(g_up),
                     w_down=g_down.reshape(NDEV, -1, D))
        dmixed, from_sibling = _matmul(dh1, w_out, tb=True, tm=1024, tn=1024, tk=D, out_dtype=BF16,
                                       side=_sibling_side([slots[k] for k in early]), name="out_proj_dx")
        parts = [_add_sibling(slots[k], r, dist["core"], 128, "grads_add_" + k) for k, r in zip(early, from_sibling)]
    doa, dob, dproj, d_gdn_norm, d_ret_norm = _merge_bwd(oa, ob, proj, gdn_norm_w, ret_norm_w, dmixed, ts)

    dproj = _ret_scan_bwd(proj, cs, sn, lgtab, st_b, dob, dproj)
    d_inter = _gdn_scan_bwd(*inter, st_a, doa)
    dqk, dva, dbeta_h, dgc_h, dgl_h = _gdn_prep_bwd(qk, va, beta, gc, gl, *d_inter)
    dproj, d_conv = _gdn_qkv_bwd(proj, conv_w, c_qk, c_v, dqk, dva, dproj, ts)
    dproj, d_a_log, d_dt_bias = _bg_bwd(proj, a_log, dt_bias, dbeta_h, dgc_h, dgl_h, dproj, ts)

    if dist is None:
        g_in = _matmul(u_t, dproj, tm=1024, tn=1280, tk=D, out_dtype=BF16, name="in_proj_dw")
        du = _matmul(dproj, w_in, tb=True, tm=1024, tn=1024, tk=1664, name="in_proj_dx")
        big = dict(w_in=g_in, w_out=g_out, w_gate=g_gate, w_up=g_up, w_down=g_down)
    else:
        g_in, from_chips = _matmul(u_t, dproj, tm=1024, tn=1280, tk=D, out_dtype=BF16, side=_chips_side(parts), name="in_proj_dw")
        du, (from_all,) = _matmul(dproj, w_in, tb=True, tm=1024, tn=1024, tk=1664,
                                  side=_all_to_all_side(_windows_from_layout(g_in)), name="in_proj_dx")
        big = dict(w_in=from_all, **{k: (p, r) for k, p, r in zip(early, parts, from_chips)})
    dx, d_norm1 = _rms_bwd(x, norm1_w, du, dh1, ts, "norm1_bwd")

    small = dict(norm1_w=d_norm1, conv_w=d_conv, a_log=d_a_log, dt_bias=d_dt_bias, gdn_norm_w=d_gdn_norm,
                 ret_norm_w=d_ret_norm, norm2_w=d_norm2, norm_f_w=d_norm_f)
    return loss, dx, big, small


def _coords():
    return lax.axis_index("x"), lax.axis_index("y"), lax.axis_index("c")


def _gather_side(shards):
    n = len(shards)

    def plan(ins, outs, send_sems, recv_sems, local_sems):
        x, y, c = _coords()
        me, sibling = (x, y, c), (x, y, 1 - c)
        chips = [(1 - x, y), (x, 1 - y), (1 - x, 1 - y)]

        def copy(a, k, block, to, src=None):
            px, py, pc = block
            dst = outs[a].at[4 * px + 2 * py + pc]
            return pltpu.make_async_remote_copy(src_ref=dst if src is None else src, dst_ref=dst, send_sem=send_sems.at[a, k],
                                                recv_sem=recv_sems.at[a, k], device_id=to, device_id_type=MESH)

        mine = [pltpu.make_async_copy(ins[a], outs[a].at[4 * x + 2 * y + c], local_sems.at[a]) for a in range(n)]
        first = []
        for a in range(n):
            first.append(copy(a, 0, me, sibling, src=ins[a]))
            first += [copy(a, 1 + j, me, (*chip, c), src=ins[a]) for j, chip in enumerate(chips)]
        return c, me, sibling, chips, copy, mine, first

    def start(ins, outs, *sems):
        *_, mine, first = plan(ins, outs, *sems)
        for cp in mine + first:
            cp.start()

    def relay(ins, outs, *sems):
        c, me, sibling, chips, copy, mine, first = plan(ins, outs, *sems)
        for j, chip in enumerate(chips):
            for a in range(n):
                copy(a, 1 + j, (*chip, c), me).wait_recv()
                copy(a, 4 + j, (*chip, c), sibling).start()

    def finish(ins, outs, *sems):
        c, me, sibling, chips, copy, mine, first = plan(ins, outs, *sems)
        passed = [copy(a, 4 + j, (*chip, c), sibling) for j, chip in enumerate(chips) for a in range(n)]
        for a in range(n):
            copy(a, 0, sibling, me).wait_recv()
            for j, chip in enumerate(chips):
                copy(a, 4 + j, (*chip, 1 - c), me).wait_recv()
        for cp in first + passed:
            cp.wait_send()
        for cp in mine:
            cp.wait()

    return _Side(shards, [_sds((NDEV,) + a.shape, a.dtype) for a in shards], [(n, 7), (n, 7), (n,)], start, finish, relay)


def _exchange_side(ins, n_out, copies_of):
    def start(in_refs, out_refs, *sems):
        for cp in copies_of(in_refs, out_refs, *sems):
            cp.start()

    def finish(in_refs, out_refs, *sems):
        for cp in copies_of(in_refs, out_refs, *sems):
            cp.wait()

    n = len(ins)
    return _Side(ins, [_sds((n_out,) + a.shape[1:], a.dtype) for a in ins], [(n, n_out), (n, n_out)], start, finish)


def _sibling_side(slots):
    def copies_of(ins, outs, send_sems, recv_sems):
        x, y, c = _coords()
        return [pltpu.make_async_remote_copy(
            src_ref=ins[a].at[2 * j + (1 - c)], dst_ref=outs[a].at[j], send_sem=send_sems.at[a, j], recv_sem=recv_sems.at[a, j],
            device_id=(x, y, 1 - c), device_id_type=MESH) for a in range(len(slots)) for j in range(4)]

    return _exchange_side(slots, 4, copies_of)


def _chips_side(parts):
    def copies_of(ins, outs, send_sems, recv_sems):
        x, y, c = _coords()
        chips = [(1 - x, y), (x, 1 - y), (1 - x, 1 - y)]
        return [pltpu.make_async_remote_copy(
            src_ref=ins[a].at[2 * px + py], dst_ref=outs[a].at[k], send_sem=send_sems.at[a, k], recv_sem=recv_sems.at[a, k],
            device_id=(px, py, c), device_id_type=MESH) for a in range(len(parts)) for k, (px, py) in enumerate(chips)]

    return _exchange_side(parts, 3, copies_of)


def _all_to_all_side(slots):
    def plan(ins, outs, send_sems, recv_sems, local_sems):
        x, y, c = _coords()
        mine = 4 * x + 2 * y + c
        own = pltpu.make_async_copy(ins[0].at[mine], outs[0].at[mine], local_sems.at[0])
        remote = []
        for r in range(1, NDEV):
            peer = (x ^ (r >> 2), y ^ ((r >> 1) & 1), c ^ (r & 1))
            remote.append(pltpu.make_async_remote_copy(
                src_ref=ins[0].at[mine ^ r], dst_ref=outs[0].at[mine], send_sem=send_sems.at[r - 1], recv_sem=recv_sems.at[r - 1],
                device_id=peer, device_id_type=MESH))
        return own, remote

    def start(ins, outs, *sems):
        own, remote = plan(ins, outs, *sems)
        for cp in [own] + remote:
            cp.start()

    def finish(ins, outs, *sems):
        own, remote = plan(ins, outs, *sems)
        for cp in remote:
            cp.wait()
        own.wait()

    return _Side([slots], [_sds(slots.shape, slots.dtype)], [(NDEV - 1,), (NDEV - 1,), (1,)], start, finish)


def _allreduce_small(pack, name):
    rows, cols = pack.shape

    def body(in_ref, out_ref, buf_ref, send_sems, recv_sems):
        x, y, c = _coords()
        mine = 4 * x + 2 * y + c
        buf_ref[mine] = in_ref[...]
        copies = []
        for r in range(1, NDEV):
            peer = (x ^ (r >> 2), y ^ ((r >> 1) & 1), c ^ (r & 1))
            copies.append(pltpu.make_async_remote_copy(
                src_ref=in_ref, dst_ref=buf_ref.at[mine], send_sem=send_sems.at[r - 1], recv_sem=recv_sems.at[r - 1],
                device_id=peer, device_id_type=MESH))
        for cp in copies:
            cp.start()
        for r in range(1, NDEV):
            pltpu.make_async_remote_copy(
                src_ref=in_ref, dst_ref=buf_ref.at[mine ^ r], send_sem=send_sems.at[r - 1], recv_sem=recv_sems.at[r - 1],
                device_id=(x, y, c), device_id_type=MESH).wait_recv()
        for cp in copies:
            cp.wait_send()
        acc = buf_ref[0]
        for d in range(1, NDEV):
            acc = acc + buf_ref[d]
        out_ref[...] = acc

    return pl.pallas_call(
        body, in_specs=[VMEM_FULL], out_specs=VMEM_FULL, out_shape=_sds((rows, cols)),
        scratch_shapes=[pltpu.VMEM((NDEV, rows, cols), F32), pltpu.SemaphoreType.DMA((NDEV - 1,)), pltpu.SemaphoreType.DMA((NDEV - 1,))],
        name=name)(pack)


def _add_sibling(slots, recv, core, tr, name):
    _, rows, cols = slots.shape
    tr = _row_tile(rows, tr)

    def body(c_ref, a_ref, b_ref, o_ref):
        o_ref[...] = (a_ref[...].astype(F32) + b_ref[...].astype(F32)).astype(BF16)

    gs = pltpu.PrefetchScalarGridSpec(
        num_scalar_prefetch=1, grid=(4, rows // tr),
        in_specs=[pl.BlockSpec((None, tr, cols), lambda j, i, cr: (2 * j + cr[0], i, 0)),
                  pl.BlockSpec((None, tr, cols), lambda j, i, cr: (j, i, 0))],
        out_specs=pl.BlockSpec((None, tr, cols), lambda j, i, cr: (j, i, 0)))
    return pl.pallas_call(body, grid_spec=gs, out_shape=_sds((4, rows, cols), BF16), name=name,
                          compiler_params=_params(("parallel", "parallel"), 6 * tr * cols * 4))(core, slots, recv)


def _adam_math(w, g, m, v):
    m2 = B1 * m + (1.0 - B1) * g
    v2 = B2 * v + (1.0 - B2) * jnp.square(g)
    m_hat = m2 / (1.0 - B1 ** STEP)
    v_hat = v2 / (1.0 - B2 ** STEP)
    return -LR * (m_hat / (jnp.sqrt(v_hat) + EPS_ADAM) + WD * w), m2, v2


def _adamw_reduced(part, recv, chip, w, m, v, tr, name):
    rows, cols = w.shape
    tr = _row_tile(rows, tr)

    def body(j_ref, p_ref, r0_ref, r1_ref, r2_ref, w_ref, m_ref, v_ref, g_ref, d_ref, nm_ref, nv_ref):
        g = p_ref[...].astype(F32) + r0_ref[...].astype(F32) + r1_ref[...].astype(F32) + r2_ref[...].astype(F32)
        d, m2, v2 = _adam_math(w_ref[...], g, m_ref[...], v_ref[...])
        g_ref[...] = g
        d_ref[...] = d
        nm_ref[...] = m2
        nv_ref[...] = v2

    flat = pl.BlockSpec((tr, cols), lambda i, jr: (i, 0))
    gs = pltpu.PrefetchScalarGridSpec(
        num_scalar_prefetch=1, grid=(rows // tr,),
        in_specs=[pl.BlockSpec((None, tr, cols), lambda i, jr: (jr[0], i, 0))]
        + [pl.BlockSpec((None, tr, cols), functools.partial(lambda i, jr, k: (k, i, 0), k=k)) for k in range(3)] + [flat] * 3,
        out_specs=[flat] * 4)
    return pl.pallas_call(body, grid_spec=gs, out_shape=[_sds((rows, cols))] * 4, name=name,
                          compiler_params=_params(("parallel",), 22 * tr * cols * 4))(chip, part, recv, recv, recv, w, m, v)


def _sum_slots(recv, tr, name):
    _, rows, cols = recv.shape
    tr = _row_tile(rows, tr)

    def body(*refs):
        acc = refs[0][...].astype(F32)
        for p_ref in refs[1:NDEV]:
            acc = acc + p_ref[...].astype(F32)
        refs[NDEV][...] = acc

    slot = [pl.BlockSpec((None, tr, cols), functools.partial(lambda i, k: (k, i, 0), k=k)) for k in range(NDEV)]
    return pl.pallas_call(body, grid=(rows // tr,), in_specs=slot, out_specs=pl.BlockSpec((tr, cols), lambda i: (i, 0)),
                          out_shape=_sds((rows, cols)), name=name, compiler_params=_params(("parallel",)))(*([recv] * NDEV))


def _adamw_rows(w, g, m, v, tr, name):
    rows, cols = w.shape
    tr = _row_tile(rows, tr)

    def body(w_ref, g_ref, m_ref, v_ref, d_ref, nm_ref, nv_ref):
        d_ref[...], nm_ref[...], nv_ref[...] = _adam_math(w_ref[...], g_ref[...], m_ref[...], v_ref[...])

    flat = pl.BlockSpec((tr, cols), lambda i: (i, 0))
    return pl.pallas_call(body, grid=(rows // tr,), in_specs=[flat] * 4, out_specs=[flat] * 3, out_shape=[_sds((rows, cols))] * 3,
                          name=name, compiler_params=_params(("parallel",)))(w, g, m, v)


def _adamw_plain(w, g, m, v, name):
    def body(w_ref, g_ref, m_ref, v_ref, d_ref, nm_ref, nv_ref):
        d, m2, v2 = _adam_math(w_ref[...], g_ref[...], m_ref[...], v_ref[...])
        d_ref[...] = d
        nm_ref[...] = m2
        nv_ref[...] = v2

    return pl.pallas_call(body, out_shape=[_sds(w.shape)] * 3, name=name)(w, g, m, v)


def _pack_small(norm1_w, conv_w, a_log, dt_bias, gdn_norm_w, ret_norm_w, norm2_w, norm_f_w):
    misc = jnp.concatenate([gdn_norm_w.reshape(1, DV), a_log.reshape(1, H), dt_bias.reshape(1, H),
                            jnp.zeros((1, D - DV - 2 * H), F32)], axis=1)
    return jnp.concatenate([norm1_w.reshape(1, D), ret_norm_w.reshape(1, D), norm2_w.reshape(1, D), norm_f_w.reshape(1, D),
                            conv_w.reshape(8, D), misc, jnp.zeros((3, D), F32)], axis=0)


def _unpack_small(pack):
    return dict(norm1_w=pack[0:1], ret_norm_w=pack[1:2], norm2_w=pack[2:3], norm_f_w=pack[3], conv_w=pack[4:12].reshape(4, 2 * D),
                gdn_norm_w=pack[12:13, 0:DV], a_log=pack[12:13, DV:DV + H], dt_bias=pack[12:13, DV + H:DV + 2 * H])


IN_SPLITS = (4096, 2048, 8, 8, 1024, 1024, 2048, 2048, 2048, 2048)


BA_END = sum(IN_SPLITS[:4])
LANES = 128


def _padded_order_blocks():
    z0, ba0, rq0, rk0, rv0, rg0, ga0, gb0 = 4096, 6144, 6400, 7424, 8448, 10496, 12544, 14592
    cols = []
    for h in range(H):
        for base in (z0, rg0, ga0, gb0):
            cols += [base + DV * h, base + DV * h + LANES]
    cols += list(range(0, z0, LANES))
    for h in range(H):
        cols += [rq0 + DK * h, rk0 + DK * h, rv0 + DV * h, rv0 + DV * h + LANES]
    cols += [ba0, ba0 + LANES]
    blocks = np.asarray(cols, np.int32) // LANES
    assert sorted(blocks.tolist()) == list(range(P_IN // LANES))
    return blocks


def _permute_blocks(x, blocks, name):
    rows, cols = x.shape

    def body(p_ref, x_ref, o_ref):
        o_ref[...] = x_ref[...]

    gs = pltpu.PrefetchScalarGridSpec(num_scalar_prefetch=1, grid=(cols // LANES,),
                                      in_specs=[pl.BlockSpec((rows, LANES), lambda j, p: (0, p[j]))],
                                      out_specs=pl.BlockSpec((rows, LANES), lambda j, p: (0, j)))
    return pl.pallas_call(body, grid_spec=gs, out_shape=_sds((rows, cols), x.dtype), name=name,
                          compiler_params=_params(("parallel",)))(jnp.asarray(blocks), x)


def _regroup_w_in(w):
    padded = jnp.concatenate([w[:, :BA_END], jnp.zeros((w.shape[0], P_IN - N_IN), w.dtype), w[:, BA_END:]], axis=1)
    return _permute_blocks(padded, _padded_order_blocks(), "w_in_to_layout")


def _ungroup_w_in(g):
    padded = _permute_blocks(g, np.argsort(_padded_order_blocks()).astype(np.int32), "w_in_grad_from_layout")
    return jnp.concatenate([padded[:, :BA_END], padded[:, BA_END + P_IN - N_IN:]], axis=1)


SHARD_W = N_IN // NDEV
GAP = P_IN - N_IN
WIN = 2304


def _padded_col(c):
    return c + (GAP if c >= BA_END else 0)


WIN_START = [min(_padded_col(SHARD_W * d) // LANES * LANES, P_IN - WIN) for d in range(NDEV)]
WIN_OFF = [_padded_col(SHARD_W * d) - WIN_START[d] for d in range(NDEV)]
STRADDLER = BA_END // SHARD_W
STRADDLE_AT = BA_END - STRADDLER * SHARD_W
assert all(WIN_OFF[d] + SHARD_W + (GAP if d == STRADDLER else 0) <= WIN for d in range(NDEV))


def _win_off(me):
    off = jnp.int32(0)
    for d in range(NDEV):
        off = jnp.where(me == d, jnp.int32(WIN_OFF[d]), off)
    return off


def _window_of_shard(shard, me):
    rows = shard.shape[0]
    zeros = lambda n: jnp.zeros((rows, n), shard.dtype)
    plain = lax.dynamic_update_slice(zeros(WIN), shard, (0, _win_off(me)))
    o = WIN_OFF[STRADDLER]
    split = jnp.concatenate([zeros(o), shard[:, :STRADDLE_AT], zeros(GAP), shard[:, STRADDLE_AT:], zeros(WIN - o - GAP - SHARD_W)], axis=1)
    return jnp.where(me == STRADDLER, split, plain)


def _shard_of_window(win, me):
    plain = lax.dynamic_slice(win, (0, _win_off(me)), (win.shape[0], SHARD_W))
    o = WIN_OFF[STRADDLER]
    split = jnp.concatenate([win[:, o:o + STRADDLE_AT], win[:, o + STRADDLE_AT + GAP:o + GAP + SHARD_W]], axis=1)
    return jnp.where(me == STRADDLER, split, plain)


def _layout_from_windows(wins):
    _, rows, _ = wins.shape
    data = []
    for d in range(NDEV):
        lo = _padded_col(SHARD_W * d)
        data.append([(lo, lo + STRADDLE_AT), (lo + STRADDLE_AT + GAP, lo + GAP + SHARD_W)] if d == STRADDLER else [(lo, lo + SHARD_W)])
    zero_block = (0, WIN // LANES - 1)
    table = []
    for p in _padded_order_blocks():
        src = [(d, int(p) - WIN_START[d] // LANES) for d in range(NDEV)
               if any(lo < (p + 1) * LANES and hi > p * LANES for lo, hi in data[d])]
        assert len(src) <= 2 and all(0 <= b < WIN // LANES for _, b in src)
        src += [zero_block] * (2 - len(src))
        table.append([src[0][0], src[0][1], src[1][0], src[1][1]])
    table = np.asarray(table, np.int32).T.copy()

    per = 5
    assert P_IN // LANES % per == 0

    def body(t_ref, *refs):
        o_ref = refs[-1]
        for i in range(per):
            o_ref[:, i * LANES:(i + 1) * LANES] = refs[2 * i][...] + refs[2 * i + 1][...]

    src = lambda k, i: pl.BlockSpec((None, rows, LANES), lambda j, t: (t[k, per * j + i], 0, t[k + 1, per * j + i]))
    gs = pltpu.PrefetchScalarGridSpec(
        num_scalar_prefetch=1, grid=(P_IN // (per * LANES),), in_specs=[src(k, i) for i in range(per) for k in (0, 2)],
        out_specs=pl.BlockSpec((rows, per * LANES), lambda j, t: (0, j)))
    return pl.pallas_call(body, grid_spec=gs, out_shape=_sds((rows, P_IN), wins.dtype), name="w_in_from_windows",
                          compiler_params=_params(("parallel",)))(jnp.asarray(table), *([wins] * (2 * per)))


def _windows_from_layout(g):
    rows = g.shape[0]
    where = np.argsort(_padded_order_blocks())
    nb = WIN // LANES
    table = np.asarray([where[WIN_START[d] // LANES + b] for d in range(NDEV) for b in range(nb)], np.int32)

    per = 6
    assert nb % per == 0

    def body(t_ref, *refs):
        o_ref = refs[-1]
        for i in range(per):
            o_ref[:, i * LANES:(i + 1) * LANES] = refs[i][...]

    src = lambda i: pl.BlockSpec((rows, LANES), lambda d, b, t: (0, t[d * nb + per * b + i]))
    gs = pltpu.PrefetchScalarGridSpec(num_scalar_prefetch=1, grid=(NDEV, nb // per), in_specs=[src(i) for i in range(per)],
                                      out_specs=pl.BlockSpec((None, rows, per * LANES), lambda d, b, t: (d, 0, b)))
    return pl.pallas_call(body, grid_spec=gs, out_shape=_sds((NDEV, rows, WIN), g.dtype), name="w_in_grad_windows",
                          compiler_params=_params(("parallel", "parallel")))(jnp.asarray(table), *([g] * per))


def _to_slots_cols(g):
    rows, cols = g.shape
    return g.reshape(rows, NDEV, cols // NDEV).transpose(1, 0, 2)


def _from_slots_cols(a):
    n, rows, cols = a.shape
    return a.transpose(1, 0, 2).reshape(rows, n * cols)


WEIGHT_ORDER = ["norm1_w", "w_in", "conv_w", "a_log", "dt_bias", "gdn_norm_w", "ret_norm_w", "w_out", "norm2_w", "w_gate", "w_up",
                "w_down", "norm_f_w"]


def kernel(x, norm1_w, w_in, conv_w, a_log, dt_bias, gdn_norm_w, ret_norm_w, w_out, norm2_w, w_gate, w_up, w_down, norm_f_w, loss_target, m_norm1_w, m_w_in, m_conv_w, m_a_log, m_dt_bias, m_gdn_norm_w, m_ret_norm_w, m_w_out, m_norm2_w, m_w_gate, m_w_up, m_w_down, m_norm_f_w, v_norm1_w, v_w_in, v_conv_w, v_a_log, v_dt_bias, v_gdn_norm_w, v_ret_norm_w, v_w_out, v_norm2_w, v_w_gate, v_w_up, v_w_down, v_norm_f_w):
    ax, ay, ac = _coords()
    me = 4 * ax + 2 * ay + ac
    core = jnp.reshape(ac, (1,)).astype(jnp.int32)
    chip = jnp.reshape(2 * ax + ay, (1,)).astype(jnp.int32)
    w = dict(norm1_w=norm1_w, w_in=w_in[0], conv_w=conv_w[0], a_log=a_log, dt_bias=dt_bias, gdn_norm_w=gdn_norm_w,
             ret_norm_w=ret_norm_w, w_out=w_out[0], norm2_w=norm2_w, w_gate=w_gate[0], w_up=w_up[0], w_down=w_down[0],
             norm_f_w=norm_f_w)
    m = dict(norm1_w=m_norm1_w, w_in=m_w_in[0], conv_w=m_conv_w[0], a_log=m_a_log, dt_bias=m_dt_bias, gdn_norm_w=m_gdn_norm_w,
             ret_norm_w=m_ret_norm_w, w_out=m_w_out[0], norm2_w=m_norm2_w, w_gate=m_w_gate[0], w_up=m_w_up[0], w_down=m_w_down[0],
             norm_f_w=m_norm_f_w)
    v = dict(norm1_w=v_norm1_w, w_in=v_w_in[0], conv_w=v_conv_w[0], a_log=v_a_log, dt_bias=v_dt_bias, gdn_norm_w=v_gdn_norm_w,
             ret_norm_w=v_ret_norm_w, w_out=v_w_out[0], norm2_w=v_norm2_w, w_gate=v_w_gate[0], w_up=v_w_up[0], w_down=v_w_down[0],
             norm_f_w=v_norm_f_w)
    big_names = ["w_in", "w_out", "w_gate", "w_up", "w_down"]

    w_in_wins, conv_all = _run_side(_gather_side([_window_of_shard(w["w_in"].astype(BF16), me), w["conv_w"]]), "w_in_allgather")
    w_in_full = _layout_from_windows(w_in_wins)
    conv_full = _from_slots_cols(conv_all)
    dist = dict(core=core, shards=[w[k].astype(BF16) for k in ("w_out", "w_gate", "w_up", "w_down")])

    loss_tile, dx, big, small = _local_step(
        x[0], loss_target[0], w_in_full, None, None, None, None, norm1_w, conv_full, a_log, dt_bias,
        gdn_norm_w, ret_norm_w, norm2_w, norm_f_w.reshape(1, D), dist=dist)
    loss = lax.psum(loss_tile[0, 0], ("x", "y", "c"))

    g_w_in = _shard_of_window(_sum_slots(big["w_in"], 64, "w_in_grad_sum"), me)
    out = {"w_in": (g_w_in, *_adamw_rows(w["w_in"], g_w_in, m["w_in"], v["w_in"], 64, "adamw_w_in"))}
    for k in ("w_out", "w_gate", "w_up", "w_down"):
        part, recv = big[k]
        out[k] = _adamw_reduced(part, recv, chip, w[k], m[k], v[k], 128, "adamw_" + k)

    g_small = _unpack_small(_allreduce_small(_pack_small(**small), "small_grads_allreduce"))
    g_small["conv_w"] = lax.dynamic_slice_in_dim(g_small["conv_w"], me * (2 * D // NDEV), 2 * D // NDEV, axis=1)
    small_names = [k for k in WEIGHT_ORDER if k not in big_names]
    pad_conv = lambda a: jnp.pad(a, ((0, 0), (0, 2 * D - a.shape[1])))
    packs = []
    for src in (w, g_small, m, v):
        args = {k: (pad_conv(src[k]) if k == "conv_w" else src[k]) for k in small_names}
        packs.append(_pack_small(**args))
    d_pack, m_pack, v_pack = _adamw_plain(*packs[0:1], packs[1], packs[2], packs[3], name="adamw_small")
    cut_conv = lambda dct: {**dct, "conv_w": dct["conv_w"][:, :2 * D // NDEV]}
    d_small, m_small, v_small = (cut_conv(_unpack_small(p)) for p in (d_pack, m_pack, v_pack))

    def shaped(k, a):
        return a.reshape(w_shapes[k])

    w_shapes = dict(norm1_w=norm1_w.shape, w_in=w_in.shape, conv_w=conv_w.shape, a_log=a_log.shape, dt_bias=dt_bias.shape,
                    gdn_norm_w=gdn_norm_w.shape, ret_norm_w=ret_norm_w.shape, w_out=w_out.shape, norm2_w=norm2_w.shape,
                    w_gate=w_gate.shape, w_up=w_up.shape, w_down=w_down.shape, norm_f_w=norm_f_w.shape)
    grads, deltas, new_m, new_v = [], [], [], []
    for k in WEIGHT_ORDER:
        if k in big_names:
            g_, d_, m_, v_ = out[k]
        else:
            g_, d_, m_, v_ = g_small[k], d_small[k], m_small[k], v_small[k]
        grads.append(shaped(k, g_))
        deltas.append(shaped(k, d_))
        new_m.append(shaped(k, m_))
        new_v.append(shaped(k, v_))
    return (loss, dx[None], *grads, *deltas, *new_m, *new_v)
```
